```python
import math
import jax, jax.numpy as jnp
from jax import lax
import numpy as np

D_MODEL = 1024
BATCH = 8
SEQ = 8192
DEPTH = 1

N_META = 16
D_MIX = 2 * D_MODEL
D_SSD = D_MIX // 2
SSD_HEAD_DIM = 64
SSD_HEADS = D_SSD // SSD_HEAD_DIM
SSD_GROUPS = 2
SSD_HEADS_PER_GROUP = SSD_HEADS // SSD_GROUPS
SSD_STATE = 128
SSD_CONV = 4
SSD_CHUNK = 256
D_XBC = D_SSD + 2 * SSD_GROUPS * SSD_STATE
D_S5 = D_MIX - D_SSD
S5_GROUP_WIDTH = 16
S5_GROUPS = D_S5 // S5_GROUP_WIDTH
S5_STATE = 64
D_FF = 4 * D_MODEL
D_IN_PROJ = D_SSD + D_XBC + SSD_HEADS + D_S5
NORM_EPS = 1e-5
DT_MIN = 0.001
DT_MAX = 0.1

kernel_name = 'hymba_ssd_s5_hybrid_block'


def _rmsnorm(x, g):
    xf = x.astype(jnp.float32)
    y = xf * lax.rsqrt(jnp.mean(xf * xf, axis=-1, keepdims=True) + NORM_EPS)
    return (y * g.astype(jnp.float32)).astype(x.dtype)


def _causal_depthwise_conv(x, w, b):
    k, c = w.shape
    y = lax.conv_general_dilated(x, w[:, None, :].astype(x.dtype), window_strides=(1,), padding=[(k - 1, 0)], dimension_numbers=('NWC', 'WIO', 'NWC'), feature_group_count=c)
    return y + b.astype(x.dtype)


def _ssd_mixer(xbc, dt_raw, z, dt_bias, a_log, d_skip, g_norm):
    bsz, length, _ = xbc.shape
    f32 = jnp.float32
    xbc = xbc.astype(f32)
    x_in = xbc[..., :D_SSD]
    b_in = xbc[..., D_SSD:D_SSD + SSD_GROUPS * SSD_STATE]
    c_in = xbc[..., D_SSD + SSD_GROUPS * SSD_STATE:]
    dt = jax.nn.softplus(dt_raw.astype(f32) + dt_bias.astype(f32))
    front = SSD_CHUNK - N_META
    n_real_chunks = -(-(length - N_META) // SSD_CHUNK)
    total = SSD_CHUNK * (1 + n_real_chunks)
    back = total - front - length
    pad = lambda t: jnp.pad(t, ((0, 0), (front, back), (0, 0)))
    n_chunks = total // SSD_CHUNK
    shp = (bsz, n_chunks, SSD_CHUNK, SSD_GROUPS)
    xc = pad(x_in).reshape(shp + (SSD_HEADS_PER_GROUP, SSD_HEAD_DIM))
    bc = pad(b_in).reshape(shp + (SSD_STATE,))
    cc = pad(c_in).reshape(shp + (SSD_STATE,))
    dtc = pad(dt).reshape(shp + (SSD_HEADS_PER_GROUP,))
    a = -jnp.exp(a_log.astype(f32)).reshape(SSD_GROUPS, SSD_HEADS_PER_GROUP)
    a_cs = jnp.cumsum(dtc * a, axis=2)
    xdt = xc * dtc[..., None]
    causal = jnp.tril(jnp.ones((SSD_CHUNK, SSD_CHUNK), dtype=bool))[:, :, None, None]
    seg = a_cs[:, :, :, None] - a_cs[:, :, None, :]
    decay = jnp.exp(jnp.where(causal, seg, -jnp.inf))
    cb = jnp.einsum('bclgn,bcsgn->bclsg', cc, bc)
    y_diag = jnp.einsum('bclsg,bclsgr,bcsgrp->bclgrp', cb, decay, xdt)
    decay_to_end = jnp.exp(a_cs[:, :, -1:] - a_cs)
    chunk_states = jnp.einsum('bclgn,bclgr,bclgrp->bcgrpn', bc, decay_to_end, xdt)
    chunk_decay = jnp.exp(a_cs[:, :, -1])

    def step(state, inp):
        dec, st = inp
        return state * dec[..., None, None] + st, state

    init = jnp.zeros((bsz, SSD_GROUPS, SSD_HEADS_PER_GROUP, SSD_HEAD_DIM, SSD_STATE), f32)
    _, prev = lax.scan(step, init, (jnp.moveaxis(chunk_decay, 1, 0), jnp.moveaxis(chunk_states, 1, 0)))
    prev = jnp.moveaxis(prev, 0, 1)
    y_off = jnp.einsum('bclgn,bcgrpn,bclgr->bclgrp', cc, prev, jnp.exp(a_cs))
    d = d_skip.astype(f32).reshape(SSD_GROUPS, SSD_HEADS_PER_GROUP, 1)
    y = (y_diag + y_off + xc * d).reshape(bsz, total, D_SSD)[:, front:front + length]
    y = y * jax.nn.silu(z.astype(f32))
    return _rmsnorm(y, g_norm)


def _s5_mixer(u, lam_re, lam_im, log_step, b_re, b_im, c_re, c_im, d_skip, w_glu, b_glu, g_norm):
    bsz, length, _ = u.shape
    f32 = jnp.float32
    u = u.astype(f32).reshape(bsz, length, S5_GROUPS, S5_GROUP_WIDTH)
    lr = lam_re.astype(f32)
    li = lam_im.astype(f32)
    step = jnp.exp(log_step.astype(f32))[:, None]
    mag = jnp.exp(lr * step)
    ab_re = mag * jnp.cos(li * step)
    ab_im = mag * jnp.sin(li * step)
    den = lr * lr + li * li
    coef_re = ((ab_re - 1.0) * lr + ab_im * li) / den
    coef_im = (ab_im * lr - (ab_re - 1.0) * li) / den
    br = b_re.astype(f32)
    bi = b_im.astype(f32)
    bb_re = coef_re[..., None] * br - coef_im[..., None] * bi
    bb_im = coef_re[..., None] * bi + coef_im[..., None] * br
    bu_re = jnp.einsum('blgh,gph->blgp', u, bb_re)
    bu_im = jnp.einsum('blgh,gph->blgp', u, bb_im)
    a_re = jnp.broadcast_to(ab_re, (1, length) + ab_re.shape)
    a_im = jnp.broadcast_to(ab_im, (1, length) + ab_im.shape)

    def combine(e_i, e_j):
        ar_i, ai_i, br_i, bi_i = e_i
        ar_j, ai_j, br_j, bi_j = e_j
        return (ar_j * ar_i - ai_j * ai_i,
                ar_j * ai_i + ai_j * ar_i,
                ar_j * br_i - ai_j * bi_i + br_j,
                ar_j * bi_i + ai_j * br_i + bi_j)

    _, _, s_re, s_im = lax.associative_scan(combine, (a_re, a_im, bu_re, bu_im), axis=1)
    y = (jnp.einsum('blgp,ghp->blgh', s_re, c_re.astype(f32))
         - jnp.einsum('blgp,ghp->blgh', s_im, c_im.astype(f32))
         + u * d_skip.astype(f32))
    y = jax.nn.gelu(y.reshape(bsz, length, D_S5), approximate=False)
    v = y @ w_glu.astype(f32) + b_glu.astype(f32)
    y = v[..., :D_S5] * jax.nn.sigmoid(v[..., D_S5:])
    return _rmsnorm(y, g_norm)


def _fwd_setup_inputs(seed: int = 0) -> dict:
    key = jax.random.key(seed)
    ks = jax.random.split(key, 32)
    f32 = jnp.float32
    nrm = lambda k, s, sc: jax.random.normal(k, s, f32) * sc
    gain = lambda k, s: 1.0 + 0.01 * jax.random.normal(k, s, f32)
    dt = jnp.exp(jax.random.uniform(ks[6], (DEPTH, SSD_HEADS), f32) * (math.log(DT_MAX) - math.log(DT_MIN)) + math.log(DT_MIN))
    dt = jnp.maximum(dt, 1e-4)
    dt_bias = dt + jnp.log(-jnp.expm1(-dt))
    n_idx = jnp.arange(S5_STATE, dtype=f32)
    return {
        'x': nrm(ks[0], (BATCH, SEQ, D_MODEL), 1.0),
        'meta_tokens': nrm(ks[1], (N_META, D_MODEL), 1.0),
        'g_mix': gain(ks[2], (DEPTH, D_MODEL)),
        'w_in': nrm(ks[3], (DEPTH, D_MODEL, D_IN_PROJ), D_MODEL ** -0.5),
        'conv_w': nrm(ks[4], (DEPTH, SSD_CONV, D_XBC), SSD_CONV ** -0.5),
        'conv_b': nrm(ks[5], (DEPTH, D_XBC), 0.01),
        'dt_bias': dt_bias,
        'a_log': jnp.log(jax.random.uniform(ks[7], (DEPTH, SSD_HEADS), f32, 1.0, 16.0)),
        'd_ssd': gain(ks[8], (DEPTH, SSD_HEADS)),
        'g_ssd': gain(ks[9], (DEPTH, D_SSD)),
        'lam_re': -0.5 + nrm(ks[10], (DEPTH, S5_GROUPS, S5_STATE), 0.01),
        'lam_im': math.pi * n_idx + nrm(ks[11], (DEPTH, S5_GROUPS, S5_STATE), 0.01),
        'log_step': jax.random.uniform(ks[12], (DEPTH, S5_GROUPS), f32, math.log(DT_MIN), math.log(DT_MAX)),
        'b_re': nrm(ks[13], (DEPTH, S5_GROUPS, S5_STATE, S5_GROUP_WIDTH), (2 * S5_GROUP_WIDTH) ** -0.5),
        'b_im': nrm(ks[14], (DEPTH, S5_GROUPS, S5_STATE, S5_GROUP_WIDTH), (2 * S5_GROUP_WIDTH) ** -0.5),
        'c_re': nrm(ks[15], (DEPTH, S5_GROUPS, S5_GROUP_WIDTH, S5_STATE), S5_STATE ** -0.5),
        'c_im': nrm(ks[16], (DEPTH, S5_GROUPS, S5_GROUP_WIDTH, S5_STATE), S5_STATE ** -0.5),
        'd_s5': nrm(ks[17], (DEPTH, S5_GROUPS, S5_GROUP_WIDTH), 1.0),
        'w_glu': nrm(ks[18], (DEPTH, D_S5, 2 * D_S5), D_S5 ** -0.5),
        'b_glu': nrm(ks[19], (DEPTH, 2 * D_S5), 0.01),
        'g_s5': gain(ks[20], (DEPTH, D_S5)),
        'w_out': nrm(ks[21], (DEPTH, D_MIX, D_MODEL), D_MIX ** -0.5),
        'g_mlp': gain(ks[22], (DEPTH, D_MODEL)),
        'w_up': nrm(ks[23], (DEPTH, D_MODEL, D_FF), D_MODEL ** -0.5),
        'w_down': nrm(ks[24], (DEPTH, D_FF, D_MODEL), D_FF ** -0.5),
        'g_final': gain(ks[25], (D_MODEL,)),
    }


def _fwd_reference(x, meta_tokens, g_mix, w_in, conv_w, conv_b, dt_bias, a_log, d_ssd, g_ssd, lam_re, lam_im, log_step, b_re, b_im, c_re, c_im, d_s5, w_glu, b_glu, g_s5, w_out, g_mlp, w_up, w_down, g_final):
    bsz = x.shape[0]
    meta = jnp.broadcast_to(meta_tokens.astype(x.dtype)[None], (bsz, N_META, D_MODEL))
    h = jnp.concatenate([meta, x], axis=1)
    o_xbc = D_SSD
    o_dt = D_SSD + D_XBC
    o_u = o_dt + SSD_HEADS
    for layer in range(DEPTH):
        n = _rmsnorm(h, g_mix[layer])
        proj = n @ w_in[layer]
        z = proj[..., :o_xbc]
        xbc = jax.nn.silu(_causal_depthwise_conv(proj[..., o_xbc:o_dt], conv_w[layer], conv_b[layer]))
        dt_raw = proj[..., o_dt:o_u]
        u = proj[..., o_u:]
        y_ssd = _ssd_mixer(xbc, dt_raw, z, dt_bias[layer], a_log[layer], d_ssd[layer], g_ssd[layer])
        y_s5 = _s5_mixer(u, lam_re[layer], lam_im[layer], log_step[layer], b_re[layer], b_im[layer], c_re[layer], c_im[layer], d_s5[layer], w_glu[layer], b_glu[layer], g_s5[layer])
        mix = jnp.concatenate([y_ssd.astype(h.dtype), y_s5.astype(h.dtype)], axis=-1)
        h = h + mix @ w_out[layer]
        m = _rmsnorm(h, g_mlp[layer]) @ w_up[layer]
        h = h + jnp.square(jax.nn.relu(m)) @ w_down[layer]
    return _rmsnorm(h, g_final)[:, N_META:].astype(x.dtype)


import jax as _jax
import jax.numpy as _jnp

TWIN_FORMAT = 'train_step'
FWD_PARAMS = ['x', 'meta_tokens', 'g_mix', 'w_in', 'conv_w', 'conv_b', 'dt_bias', 'a_log', 'd_ssd', 'g_ssd', 'lam_re', 'lam_im', 'log_step', 'b_re', 'b_im', 'c_re', 'c_im', 'd_s5', 'w_glu', 'b_glu', 'g_s5', 'w_out', 'g_mlp', 'w_up', 'w_down', 'g_final']
TWIN_WEIGHTS = ['meta_tokens', 'g_mix', 'w_in', 'conv_w', 'conv_b', 'dt_bias', 'a_log', 'd_ssd', 'g_ssd', 'lam_re', 'lam_im', 'log_step', 'b_re', 'b_im', 'c_re', 'c_im', 'd_s5', 'w_glu', 'b_glu', 'g_s5', 'w_out', 'g_mlp', 'w_up', 'w_down', 'g_final']
TWIN_DIFF_INPUT = 'x'
TWIN_INPUTS = ['x', 'meta_tokens', 'g_mix', 'w_in', 'conv_w', 'conv_b', 'dt_bias', 'a_log', 'd_ssd', 'g_ssd', 'lam_re', 'lam_im', 'log_step', 'b_re', 'b_im', 'c_re', 'c_im', 'd_s5', 'w_glu', 'b_glu', 'g_s5', 'w_out', 'g_mlp', 'w_up', 'w_down', 'g_final', 'loss_target', 'm_meta_tokens', 'm_g_mix', 'm_w_in', 'm_conv_w', 'm_conv_b', 'm_dt_bias', 'm_a_log', 'm_d_ssd', 'm_g_ssd', 'm_lam_re', 'm_lam_im', 'm_log_step', 'm_b_re', 'm_b_im', 'm_c_re', 'm_c_im', 'm_d_s5', 'm_w_glu', 'm_b_glu', 'm_g_s5', 'm_w_out', 'm_g_mlp', 'm_w_up', 'm_w_down', 'm_g_final', 'v_meta_tokens', 'v_g_mix', 'v_w_in', 'v_conv_w', 'v_conv_b', 'v_dt_bias', 'v_a_log', 'v_d_ssd', 'v_g_ssd', 'v_lam_re', 'v_lam_im', 'v_log_step', 'v_b_re', 'v_b_im', 'v_c_re', 'v_c_im', 'v_d_s5', 'v_w_glu', 'v_b_glu', 'v_g_s5', 'v_w_out', 'v_g_mlp', 'v_w_up', 'v_w_down', 'v_g_final']
TWIN_OUTPUTS = ['loss', 'grad_x', 'grad_meta_tokens', 'grad_g_mix', 'grad_w_in', 'grad_conv_w', 'grad_conv_b', 'grad_dt_bias', 'grad_a_log', 'grad_d_ssd', 'grad_g_ssd', 'grad_lam_re', 'grad_lam_im', 'grad_log_step', 'grad_b_re', 'grad_b_im', 'grad_c_re', 'grad_c_im', 'grad_d_s5', 'grad_w_glu', 'grad_b_glu', 'grad_g_s5', 'grad_w_out', 'grad_g_mlp', 'grad_w_up', 'grad_w_down', 'grad_g_final', 'delta_meta_tokens', 'delta_g_mix', 'delta_w_in', 'delta_conv_w', 'delta_conv_b', 'delta_dt_bias', 'delta_a_log', 'delta_d_ssd', 'delta_g_ssd', 'delta_lam_re', 'delta_lam_im', 'delta_log_step', 'delta_b_re', 'delta_b_im', 'delta_c_re', 'delta_c_im', 'delta_d_s5', 'delta_w_glu', 'delta_b_glu', 'delta_g_s5', 'delta_w_out', 'delta_g_mlp', 'delta_w_up', 'delta_w_down', 'delta_g_final', 'new_m_meta_tokens', 'new_m_g_mix', 'new_m_w_in', 'new_m_conv_w', 'new_m_conv_b', 'new_m_dt_bias', 'new_m_a_log', 'new_m_d_ssd', 'new_m_g_ssd', 'new_m_lam_re', 'new_m_lam_im', 'new_m_log_step', 'new_m_b_re', 'new_m_b_im', 'new_m_c_re', 'new_m_c_im', 'new_m_d_s5', 'new_m_w_glu', 'new_m_b_glu', 'new_m_g_s5', 'new_m_w_out', 'new_m_g_mlp', 'new_m_w_up', 'new_m_w_down', 'new_m_g_final', 'new_v_meta_tokens', 'new_v_g_mix', 'new_v_w_in', 'new_v_conv_w', 'new_v_conv_b', 'new_v_dt_bias', 'new_v_a_log', 'new_v_d_ssd', 'new_v_g_ssd', 'new_v_lam_re', 'new_v_lam_im', 'new_v_log_step', 'new_v_b_re', 'new_v_b_im', 'new_v_c_re', 'new_v_c_im', 'new_v_d_s5', 'new_v_w_glu', 'new_v_b_glu', 'new_v_g_s5', 'new_v_w_out', 'new_v_g_mlp', 'new_v_w_up', 'new_v_w_down', 'new_v_g_final']
TWIN_LEAF_KINDS = {'loss': 'loss', 'grad_x': 'grad_x', 'grad_meta_tokens': 'grad_w', 'grad_g_mix': 'grad_w', 'grad_w_in': 'grad_w', 'grad_conv_w': 'grad_w', 'grad_conv_b': 'grad_w', 'grad_dt_bias': 'grad_w', 'grad_a_log': 'grad_w', 'grad_d_ssd': 'grad_w', 'grad_g_ssd': 'grad_w', 'grad_lam_re': 'grad_w', 'grad_lam_im': 'grad_w', 'grad_log_step': 'grad_w', 'grad_b_re': 'grad_w', 'grad_b_im': 'grad_w', 'grad_c_re': 'grad_w', 'grad_c_im': 'grad_w', 'grad_d_s5': 'grad_w', 'grad_w_glu': 'grad_w', 'grad_b_glu': 'grad_w', 'grad_g_s5': 'grad_w', 'grad_w_out': 'grad_w', 'grad_g_mlp': 'grad_w', 'grad_w_up': 'grad_w', 'grad_w_down': 'grad_w', 'grad_g_final': 'grad_w', 'delta_meta_tokens': 'delta_w', 'delta_g_mix': 'delta_w', 'delta_w_in': 'delta_w', 'delta_conv_w': 'delta_w', 'delta_conv_b': 'delta_w', 'delta_dt_bias': 'delta_w', 'delta_a_log': 'delta_w', 'delta_d_ssd': 'delta_w', 'delta_g_ssd': 'delta_w', 'delta_lam_re': 'delta_w', 'delta_lam_im': 'delta_w', 'delta_log_step': 'delta_w', 'delta_b_re': 'delta_w', 'delta_b_im': 'delta_w', 'delta_c_re': 'delta_w', 'delta_c_im': 'delta_w', 'delta_d_s5': 'delta_w', 'delta_w_glu': 'delta_w', 'delta_b_glu': 'delta_w', 'delta_g_s5': 'delta_w', 'delta_w_out': 'delta_w', 'delta_g_mlp': 'delta_w', 'delta_w_up': 'delta_w', 'delta_w_down': 'delta_w', 'delta_g_final': 'delta_w', 'new_m_meta_tokens': 'new_m', 'new_m_g_mix': 'new_m', 'new_m_w_in': 'new_m', 'new_m_conv_w': 'new_m', 'new_m_conv_b': 'new_m', 'new_m_dt_bias': 'new_m', 'new_m_a_log': 'new_m', 'new_m_d_ssd': 'new_m', 'new_m_g_ssd': 'new_m', 'new_m_lam_re': 'new_m', 'new_m_lam_im': 'new_m', 'new_m_log_step': 'new_m', 'new_m_b_re': 'new_m', 'new_m_b_im': 'new_m', 'new_m_c_re': 'new_m', 'new_m_c_im': 'new_m', 'new_m_d_s5': 'new_m', 'new_m_w_glu': 'new_m', 'new_m_b_glu': 'new_m', 'new_m_g_s5': 'new_m', 'new_m_w_out': 'new_m', 'new_m_g_mlp': 'new_m', 'new_m_w_up': 'new_m', 'new_m_w_down': 'new_m', 'new_m_g_final': 'new_m', 'new_v_meta_tokens': 'new_v', 'new_v_g_mix': 'new_v', 'new_v_w_in': 'new_v', 'new_v_conv_w': 'new_v', 'new_v_conv_b': 'new_v', 'new_v_dt_bias': 'new_v', 'new_v_a_log': 'new_v', 'new_v_d_ssd': 'new_v', 'new_v_g_ssd': 'new_v', 'new_v_lam_re': 'new_v', 'new_v_lam_im': 'new_v', 'new_v_log_step': 'new_v', 'new_v_b_re': 'new_v', 'new_v_b_im': 'new_v', 'new_v_c_re': 'new_v', 'new_v_c_im': 'new_v', 'new_v_d_s5': 'new_v', 'new_v_w_glu': 'new_v', 'new_v_b_glu': 'new_v', 'new_v_g_s5': 'new_v', 'new_v_w_out': 'new_v', 'new_v_g_mlp': 'new_v', 'new_v_w_up': 'new_v', 'new_v_w_down': 'new_v', 'new_v_g_final': 'new_v'}


def _forward(args):
    return _fwd_reference(*[args[k] for k in FWD_PARAMS])


def _output_shape():
    def fwd():
        inp = _fwd_setup_inputs(0)
        return _fwd_reference(*[inp[k] for k in FWD_PARAMS])
    out = _jax.eval_shape(fwd)
    return out.shape, out.dtype

N_MICROBATCH = 1
ADAM_LR = 0.001
ADAM_B1 = 0.9
ADAM_B2 = 0.999
ADAM_EPS = 1e-08
ADAM_WD = 0.01
ADAM_STEP = 10
PER_EXAMPLE_BATCH_AXIS = {'x': 0, 'loss_target': 0}
SHARED_INPUTS = []
_WEIGHT_DTYPES = {'meta_tokens': _jnp.float32, 'g_mix': _jnp.float32, 'w_in': _jnp.float32, 'conv_w': _jnp.float32, 'conv_b': _jnp.float32, 'dt_bias': _jnp.float32, 'a_log': _jnp.float32, 'd_ssd': _jnp.float32, 'g_ssd': _jnp.float32, 'lam_re': _jnp.float32, 'lam_im': _jnp.float32, 'log_step': _jnp.float32, 'b_re': _jnp.float32, 'b_im': _jnp.float32, 'c_re': _jnp.float32, 'c_im': _jnp.float32, 'd_s5': _jnp.float32, 'w_glu': _jnp.float32, 'b_glu': _jnp.float32, 'g_s5': _jnp.float32, 'w_out': _jnp.float32, 'g_mlp': _jnp.float32, 'w_up': _jnp.float32, 'w_down': _jnp.float32, 'g_final': _jnp.float32}
MOMENT_SCALE = {'meta_tokens': 4.723488e-03, 'g_mix': 2.545592e-01, 'w_in': 1.339500e-01, 'conv_w': 1.223842e-01, 'conv_b': 1.875609e-01, 'dt_bias': 8.298557e-01, 'a_log': 1.817351e+00, 'd_ssd': 8.011035e-01, 'g_ssd': 1.544382e-01, 'lam_re': 1.024531e-02, 'lam_im': 9.968204e-03, 'log_step': 6.155167e+00, 'b_re': 6.720014e-03, 'b_im': 6.668968e-03, 'c_re': 9.584256e-03, 'c_im': 9.570828e-03, 'd_s5': 1.474148e-01, 'w_glu': 1.029094e-01, 'b_glu': 1.766146e-01, 'g_s5': 1.346908e-01, 'w_out': 1.949543e-01, 'g_mlp': 1.945825e-01, 'w_up': 9.586111e-02, 'w_down': 1.804417e-01, 'g_final': 6.499702e+01}


def _to_microbatches(a, axis):
    t = _jnp.moveaxis(a, axis, 0)
    t = t.reshape((N_MICROBATCH, t.shape[0] // N_MICROBATCH) + t.shape[1:])
    return _jnp.moveaxis(t, 1, axis + 1)


def setup_inputs(seed: int = 0) -> dict:
    inp = _fwd_setup_inputs(seed)
    key = _jax.random.fold_in(_jax.random.key(seed), 7919)
    shape, _ = _output_shape()
    out = dict(inp)
    out["loss_target"] = _jax.random.normal(_jax.random.fold_in(key, 0), shape, _jnp.float32)
    for i, name in enumerate(TWIN_WEIGHTS):
        w = inp[name].astype(_jnp.float32)
        if MOMENT_SCALE is None:
            s = _jnp.sqrt(_jnp.mean(_jnp.square(w)) + 1e-30)
        else:
            s = MOMENT_SCALE[name]
        km, kv = _jax.random.split(_jax.random.fold_in(key, i + 1))
        out[name] = w
        out["m_" + name] = s * _jax.random.normal(km, w.shape, _jnp.float32)
        out["v_" + name] = (s * s) * _jax.random.uniform(kv, w.shape, _jnp.float32, 0.5, 1.5)
    if N_MICROBATCH > 1:
        for name, axis in PER_EXAMPLE_BATCH_AXIS.items():
            out[name] = _to_microbatches(out[name], axis)
    return {'x': out['x'], 'meta_tokens': out['meta_tokens'], 'g_mix': out['g_mix'], 'w_in': out['w_in'], 'conv_w': out['conv_w'], 'conv_b': out['conv_b'], 'dt_bias': out['dt_bias'], 'a_log': out['a_log'], 'd_ssd': out['d_ssd'], 'g_ssd': out['g_ssd'], 'lam_re': out['lam_re'], 'lam_im': out['lam_im'], 'log_step': out['log_step'], 'b_re': out['b_re'], 'b_im': out['b_im'], 'c_re': out['c_re'], 'c_im': out['c_im'], 'd_s5': out['d_s5'], 'w_glu': out['w_glu'], 'b_glu': out['b_glu'], 'g_s5': out['g_s5'], 'w_out': out['w_out'], 'g_mlp': out['g_mlp'], 'w_up': out['w_up'], 'w_down': out['w_down'], 'g_final': out['g_final'], 'loss_target': out['loss_target'], 'm_meta_tokens': out['m_meta_tokens'], 'm_g_mix': out['m_g_mix'], 'm_w_in': out['m_w_in'], 'm_conv_w': out['m_conv_w'], 'm_conv_b': out['m_conv_b'], 'm_dt_bias': out['m_dt_bias'], 'm_a_log': out['m_a_log'], 'm_d_ssd': out['m_d_ssd'], 'm_g_ssd': out['m_g_ssd'], 'm_lam_re': out['m_lam_re'], 'm_lam_im': out['m_lam_im'], 'm_log_step': out['m_log_step'], 'm_b_re': out['m_b_re'], 'm_b_im': out['m_b_im'], 'm_c_re': out['m_c_re'], 'm_c_im': out['m_c_im'], 'm_d_s5': out['m_d_s5'], 'm_w_glu': out['m_w_glu'], 'm_b_glu': out['m_b_glu'], 'm_g_s5': out['m_g_s5'], 'm_w_out': out['m_w_out'], 'm_g_mlp': out['m_g_mlp'], 'm_w_up': out['m_w_up'], 'm_w_down': out['m_w_down'], 'm_g_final': out['m_g_final'], 'v_meta_tokens': out['v_meta_tokens'], 'v_g_mix': out['v_g_mix'], 'v_w_in': out['v_w_in'], 'v_conv_w': out['v_conv_w'], 'v_conv_b': out['v_conv_b'], 'v_dt_bias': out['v_dt_bias'], 'v_a_log': out['v_a_log'], 'v_d_ssd': out['v_d_ssd'], 'v_g_ssd': out['v_g_ssd'], 'v_lam_re': out['v_lam_re'], 'v_lam_im': out['v_lam_im'], 'v_log_step': out['v_log_step'], 'v_b_re': out['v_b_re'], 'v_b_im': out['v_b_im'], 'v_c_re': out['v_c_re'], 'v_c_im': out['v_c_im'], 'v_d_s5': out['v_d_s5'], 'v_w_glu': out['v_w_glu'], 'v_b_glu': out['v_b_glu'], 'v_g_s5': out['v_g_s5'], 'v_w_out': out['v_w_out'], 'v_g_mlp': out['v_g_mlp'], 'v_w_up': out['v_w_up'], 'v_w_down': out['v_w_down'], 'v_g_final': out['v_g_final']}


def _loss(weights, diff, rest, loss_target):
    with _jax.named_scope("forward"):
        args = {**rest, TWIN_DIFF_INPUT: diff, **{k: w.astype(_WEIGHT_DTYPES[k]) for k, w in weights.items()}}
        y = _forward(args)
    with _jax.named_scope("loss_head"):
        err = _jnp.square(y.astype(_jnp.float32) - loss_target)
        return 0.5 * _jnp.sum(_jnp.mean(err, axis=-1)) if err.ndim else 0.5 * err


def _adamw(w, g, m, v):
    m = ADAM_B1 * m + (1.0 - ADAM_B1) * g
    v = ADAM_B2 * v + (1.0 - ADAM_B2) * _jnp.square(g)
    m_hat = m / (1.0 - ADAM_B1 ** ADAM_STEP)
    v_hat = v / (1.0 - ADAM_B2 ** ADAM_STEP)
    delta = -ADAM_LR * (m_hat / (_jnp.sqrt(v_hat) + ADAM_EPS) + ADAM_WD * w)
    return delta, m, v


def reference(x, meta_tokens, g_mix, w_in, conv_w, conv_b, dt_bias, a_log, d_ssd, g_ssd, lam_re, lam_im, log_step, b_re, b_im, c_re, c_im, d_s5, w_glu, b_glu, g_s5, w_out, g_mlp, w_up, w_down, g_final, loss_target, m_meta_tokens, m_g_mix, m_w_in, m_conv_w, m_conv_b, m_dt_bias, m_a_log, m_d_ssd, m_g_ssd, m_lam_re, m_lam_im, m_log_step, m_b_re, m_b_im, m_c_re, m_c_im, m_d_s5, m_w_glu, m_b_glu, m_g_s5, m_w_out, m_g_mlp, m_w_up, m_w_down, m_g_final, v_meta_tokens, v_g_mix, v_w_in, v_conv_w, v_conv_b, v_dt_bias, v_a_log, v_d_ssd, v_g_ssd, v_lam_re, v_lam_im, v_log_step, v_b_re, v_b_im, v_c_re, v_c_im, v_d_s5, v_w_glu, v_b_glu, v_g_s5, v_w_out, v_g_mlp, v_w_up, v_w_down, v_g_final):
    given = dict(x=x, meta_tokens=meta_tokens, g_mix=g_mix, w_in=w_in, conv_w=conv_w, conv_b=conv_b, dt_bias=dt_bias, a_log=a_log, d_ssd=d_ssd, g_ssd=g_ssd, lam_re=lam_re, lam_im=lam_im, log_step=log_step, b_re=b_re, b_im=b_im, c_re=c_re, c_im=c_im, d_s5=d_s5, w_glu=w_glu, b_glu=b_glu, g_s5=g_s5, w_out=w_out, g_mlp=g_mlp, w_up=w_up, w_down=w_down, g_final=g_final, loss_target=loss_target, m_meta_tokens=m_meta_tokens, m_g_mix=m_g_mix, m_w_in=m_w_in, m_conv_w=m_conv_w, m_conv_b=m_conv_b, m_dt_bias=m_dt_bias, m_a_log=m_a_log, m_d_ssd=m_d_ssd, m_g_ssd=m_g_ssd, m_lam_re=m_lam_re, m_lam_im=m_lam_im, m_log_step=m_log_step, m_b_re=m_b_re, m_b_im=m_b_im, m_c_re=m_c_re, m_c_im=m_c_im, m_d_s5=m_d_s5, m_w_glu=m_w_glu, m_b_glu=m_b_glu, m_g_s5=m_g_s5, m_w_out=m_w_out, m_g_mlp=m_g_mlp, m_w_up=m_w_up, m_w_down=m_w_down, m_g_final=m_g_final, v_meta_tokens=v_meta_tokens, v_g_mix=v_g_mix, v_w_in=v_w_in, v_conv_w=v_conv_w, v_conv_b=v_conv_b, v_dt_bias=v_dt_bias, v_a_log=v_a_log, v_d_ssd=v_d_ssd, v_g_ssd=v_g_ssd, v_lam_re=v_lam_re, v_lam_im=v_lam_im, v_log_step=v_log_step, v_b_re=v_b_re, v_b_im=v_b_im, v_c_re=v_c_re, v_c_im=v_c_im, v_d_s5=v_d_s5, v_w_glu=v_w_glu, v_b_glu=v_b_glu, v_g_s5=v_g_s5, v_w_out=v_w_out, v_g_mlp=v_g_mlp, v_w_up=v_w_up, v_w_down=v_w_down, v_g_final=v_g_final)
    weights = {n: given[n] for n in TWIN_WEIGHTS}
    shared = {n: given[n] for n in SHARED_INPUTS}
    per_example = {n: given[n] for n in ['x']}
    grad_fn = _jax.value_and_grad(_loss, argnums=(0, 1))

    def one_microbatch(ex, loss_target):
        ex = dict(ex)
        diff = ex.pop(TWIN_DIFF_INPUT)
        return grad_fn(weights, diff, {**shared, **ex}, loss_target)

    if N_MICROBATCH == 1:
        loss, (grad_w, grad_x) = one_microbatch(per_example, given["loss_target"])
    else:
        def body(carry, xs):
            loss_sum, grad_sum = carry
            l_k, (gw_k, gx_k) = one_microbatch(xs[0], xs[1])
            with _jax.named_scope("update"):
                return (loss_sum + l_k, _jax.tree.map(_jnp.add, grad_sum, gw_k)), gx_k

        init = (_jnp.zeros((), _jnp.float32), _jax.tree.map(_jnp.zeros_like, weights))
        (loss, grad_w), grad_x = _jax.lax.scan(body, init, (per_example, given["loss_target"]))
    with _jax.named_scope("update"):
        delta_w, new_m, new_v = {}, {}, {}
        for n in TWIN_WEIGHTS:
            delta_w[n], new_m[n], new_v[n] = _adamw(weights[n], grad_w[n], given["m_" + n], given["v_" + n])
    return (loss, grad_x, *[grad_w[n] for n in TWIN_WEIGHTS], *[delta_w[n] for n in TWIN_WEIGHTS],
            *[new_m[n] for n in TWIN_WEIGHTS], *[new_v[n] for n in TWIN_WEIGHTS])
```

```python
import functools
import math

import jax
import jax.numpy as jnp
from jax import lax
from jax.experimental import pallas as pl
from jax.experimental.pallas import tpu as pltpu

f32 = jnp.float32
bf16 = jnp.bfloat16
HI = lax.Precision.HIGHEST

D = 1024
N_META = 16
CH = 256
HEADS = 16
HEAD_DIM = 64
NSTATE = 128
D_XBC = 1536
S5_P = 64
NQ = 8
NSEG = 8
SEG_ROWS = 32
EPS = 1e-5
O_Z, O_U, O_XBC, O_DT, W_PROJ = 0, 1024, 2048, 3584, 3712
VMEM_LIMIT = 56 * 1024 * 1024

ADAM_LR, ADAM_B1, ADAM_B2, ADAM_EPS, ADAM_WD, ADAM_STEP = 0.001, 0.9, 0.999, 1e-08, 0.01, 10


def _cp(sem=None):
    return pltpu.CompilerParams(dimension_semantics=sem, vmem_limit_bytes=VMEM_LIMIT)


def _sigmoid(v):
    return 1.0 / (1.0 + jnp.exp(-v))


def _rsum8(v):
    r, c = v.shape
    return jnp.sum(v.reshape(r // 8, 8, c), axis=0)


def _rms(h, g):
    r = lax.rsqrt(jnp.mean(h * h, axis=-1, keepdims=True) + EPS)
    return h * r * g


def _rms_bwd(dy, h, g):
    r = lax.rsqrt(jnp.mean(h * h, axis=-1, keepdims=True) + EPS)
    n = h * r
    dn = dy * g
    dh = r * (dn - n * jnp.mean(dn * n, axis=-1, keepdims=True))
    return dh, _rsum8(dy * n)


def _rowwise(name, fn, nblk, row_ins, par_ins, out_rows, out_accs, bm=CH):
    n_r, n_p, n_o, n_a = len(row_ins), len(par_ins), len(out_rows), len(out_accs)

    def body(*refs):
        i = pl.program_id(0)
        vals = [r[...] for r in refs[:n_r + n_p]]
        ro, ao = fn(i, *vals)
        outs = refs[n_r + n_p:]
        for r, v in zip(outs[:n_o], ro):
            r[...] = v.astype(r.dtype)

        @pl.when(i == 0)
        def _():
            for r in outs[n_o:]:
                r[...] = jnp.zeros_like(r)
        for r, v in zip(outs[n_o:], ao):
            r[...] += v

    in_specs = [pl.BlockSpec((bm, a.shape[1]), lambda i: (i, 0)) for a in row_ins]
    in_specs += [pl.BlockSpec(a.shape, lambda i, nd=a.ndim: (0,) * nd) for a in par_ins]
    out_specs = [pl.BlockSpec((bm, c), lambda i: (i, 0)) for c, _ in out_rows]
    out_specs += [pl.BlockSpec((8, c), lambda i: (0, 0)) for c in out_accs]
    out_shape = [jax.ShapeDtypeStruct((nblk * bm, c), dt) for c, dt in out_rows]
    out_shape += [jax.ShapeDtypeStruct((8, c), f32) for c in out_accs]
    res = pl.pallas_call(body, name=name, grid=(nblk,), in_specs=in_specs, out_specs=out_specs, out_shape=out_shape,
                         compiler_params=_cp(("arbitrary",)))(*row_ins, *par_ins)
    return res[:n_o], res[n_o:]


def _mm(name, nblk, row_ins, par_ins, w, pre, post, out_rows, out_accs, bm=CH):
    n_r, n_p, n_o, n_a = len(row_ins), len(par_ins), len(out_rows), len(out_accs)

    def body(*refs):
        i = pl.program_id(0)
        rows = [r[...] for r in refs[:n_r]]
        pars = [r[...] for r in refs[n_r:n_r + n_p]]
        w_ref = refs[n_r + n_p]
        a = pre(rows, pars)
        acc = jnp.dot(a, w_ref[...], preferred_element_type=f32)
        ro, ao = post(i, acc, rows, pars)
        outs = refs[n_r + n_p + 1:]
        for r, v in zip(outs[:n_o], ro):
            r[...] = v.astype(r.dtype)

        @pl.when(i == 0)
        def _():
            for r in outs[n_o:]:
                r[...] = jnp.zeros_like(r)
        for r, v in zip(outs[n_o:], ao):
            r[...] += v

    in_specs = [pl.BlockSpec((bm, a.shape[1]), lambda i: (i, 0)) for a in row_ins]
    in_specs += [pl.BlockSpec(a.shape, lambda i, nd=a.ndim: (0,) * nd) for a in par_ins]
    in_specs += [pl.BlockSpec(w.shape, lambda i: (0, 0))]
    out_specs = [pl.BlockSpec((bm, c), lambda i: (i, 0)) for c, _ in out_rows]
    out_specs += [pl.BlockSpec((8, c), lambda i: (0, 0)) for c in out_accs]
    out_shape = [jax.ShapeDtypeStruct((nblk * bm, c), dt) for c, dt in out_rows]
    out_shape += [jax.ShapeDtypeStruct((8, c), f32) for c in out_accs]
    res = pl.pallas_call(body, name=name, grid=(nblk,), in_specs=in_specs, out_specs=out_specs, out_shape=out_shape,
                         compiler_params=_cp(("arbitrary",)))(*row_ins, *par_ins, w)
    return res[:n_o], res[n_o:]


def _mm_tn(name, a, b, bk, bn, bm=CH):
    m, k1 = a.shape
    n = b.shape[1]
    steps = m // bm

    def body(a_ref, b_ref, o_ref):
        @pl.when(pl.program_id(2) == 0)
        def _():
            o_ref[...] = jnp.zeros_like(o_ref)
        o_ref[...] += lax.dot_general(a_ref[...], b_ref[...], (((0,), (0,)), ((), ())), preferred_element_type=f32)

    return pl.pallas_call(
        body, name=name, grid=(k1 // bk, n // bn, steps),
        in_specs=[pl.BlockSpec((bm, bk), lambda i, j, k: (k, i)), pl.BlockSpec((bm, bn), lambda i, j, k: (k, j))],
        out_specs=pl.BlockSpec((bk, bn), lambda i, j, k: (i, j)),
        out_shape=jax.ShapeDtypeStruct((k1, n), f32),
        compiler_params=_cp(("arbitrary", "arbitrary", "arbitrary")))(a, b)


def _head_expand():
    h = lax.broadcasted_iota(jnp.int32, (HEADS, D), 0)
    c = lax.broadcasted_iota(jnp.int32, (HEADS, D), 1)
    return jnp.where((c >> 6) == h, 1.0, 0.0).astype(f32)


def _ssd_common(i, P, prev8, cw, cb, dtb, alog):
    z = P[:, O_Z:O_Z + D]
    xp = P[:, O_XBC:O_XBC + D_XBC]
    dt_raw = P[:, O_DT:O_DT + HEADS]
    row = lax.broadcasted_iota(jnp.int32, (CH, 1), 0)
    row8 = lax.broadcasted_iota(jnp.int32, (8, 1), 0)
    taps = [xp]
    for k in (1, 2, 3):
        rolled = pltpu.roll(xp, k, 0)
        fix = pltpu.roll(prev8, k, 0)
        top = jnp.where(row8 < k, fix, rolled[0:8])
        taps.append(jnp.concatenate([top, rolled[8:]], axis=0))
    xc = cb + cw[3:4] * taps[0] + cw[2:3] * taps[1] + cw[1:2] * taps[2] + cw[0:1] * taps[3]
    sg = _sigmoid(xc)
    xbc = xc * sg
    live = jnp.where(jnp.logical_or(i > 0, row >= CH - N_META), 1.0, 0.0)
    pre = dt_raw + dtb
    dt = jnp.where(pre > 20.0, pre, jnp.log(1.0 + jnp.exp(jnp.minimum(pre, 20.0)))) * live
    a = -jnp.exp(alog)
    dta = dt * a
    r_i = lax.broadcasted_iota(jnp.int32, (CH, CH), 0)
    c_i = lax.broadcasted_iota(jnp.int32, (CH, CH), 1)
    tril = r_i >= c_i
    trilf = jnp.where(tril, 1.0, 0.0)
    acs = jnp.dot(trilf, dta, preferred_element_type=f32, precision=HI)
    acs_t = lax.dot_general(dta, jnp.where(r_i <= c_i, 1.0, 0.0), (((0,), (0,)), ((), ())),
                            preferred_element_type=f32, precision=HI)
    e = _head_expand()
    acs_e = jnp.dot(acs, e, preferred_element_type=f32, precision=HI)
    dt_e = jnp.dot(dt, e, preferred_element_type=f32, precision=HI)
    return dict(z=z, xp=xp, taps=taps, xc=xc, sg=sg, xbc=xbc, live=live, pre=pre, dt=dt, a=a, tril=tril, trilf=trilf,
                acs=acs, acs_t=acs_t, e=e, acs_e=acs_e, dt_e=dt_e)


def _lmat(c, g, gh, h):
    seg = c["acs"][:, h:h + 1] - c["acs_t"][h:h + 1, :]
    return jnp.where(c["tril"], jnp.exp(jnp.minimum(seg, 0.0)), 0.0)


def _pair_masks():
    lane = lax.broadcasted_iota(jnp.int32, (1, 128), 1)
    return jnp.where(lane < HEAD_DIM, 1.0, 0.0), jnp.where(lane >= HEAD_DIM, 1.0, 0.0)


def _ssd_fwd(proj, conv_w, conv_b, dt_bias, a_log, d_ssd, g_ssd, nch):
    def body(p_ref, cw_ref, cb_ref, dtb_ref, al_ref, d_ref, g_ref, y_ref, ys_ref, st_ref, prev8_ref, state_ref):
        i = pl.program_id(0)

        @pl.when(i == 0)
        def _():
            prev8_ref[...] = jnp.zeros_like(prev8_ref)
            state_ref[...] = jnp.zeros_like(state_ref)

        P = p_ref[...]
        c = _ssd_common(i, P, prev8_ref[...], cw_ref[...], cb_ref[...], dtb_ref[...], al_ref[...])
        prev8_ref[...] = c["xp"][CH - 8:CH]
        xbc = c["xbc"]
        x = xbc[:, 0:D]
        xdt = x * c["dt_e"]
        a_last_e = c["acs_e"][CH - 1:CH, :]
        w_end = (xdt * jnp.exp(a_last_e - c["acs_e"])).astype(bf16)
        m0, m1 = _pair_masks()
        ys = []
        for g in range(2):
            bg = xbc[:, D + NSTATE * g:D + NSTATE * (g + 1)].astype(bf16)
            cg = xbc[:, D + 2 * NSTATE + NSTATE * g:D + 2 * NSTATE + NSTATE * (g + 1)].astype(bf16)
            gmat = lax.dot_general(cg, bg, (((1,), (1,)), ((), ())), preferred_element_type=f32)
            st = state_ref[g]
            st_ref[0, g] = st
            sl = slice(512 * g, 512 * (g + 1))
            y_off = jnp.dot(cg, st.astype(bf16), preferred_element_type=f32) * jnp.exp(c["acs_e"][:, sl])
            contrib = lax.dot_general(bg, w_end[:, sl], (((0,), (0,)), ((), ())), preferred_element_type=f32)
            state_ref[g] = st * jnp.exp(a_last_e[:, sl]) + contrib
            yd = []
            for pr in range(4):
                h0 = 8 * g + 2 * pr
                xp2 = xdt[:, 128 * (4 * g + pr):128 * (4 * g + pr + 1)]
                ma = (gmat * _lmat(c, g, 0, h0)).astype(bf16)
                mb = (gmat * _lmat(c, g, 0, h0 + 1)).astype(bf16)
                yd.append(jnp.dot(ma, (xp2 * m0).astype(bf16), preferred_element_type=f32)
                          + jnp.dot(mb, (xp2 * m1).astype(bf16), preferred_element_type=f32))
            ys.append(jnp.concatenate(yd, axis=1) + y_off)
        d_e = jnp.dot(d_ref[...], c["e"], preferred_element_type=f32, precision=HI)
        y = jnp.concatenate(ys, axis=1) + x * d_e
        y_ref[...] = y
        yg = y * (c["z"] * _sigmoid(c["z"]))
        ys_ref[...] = _rms(yg, g_ref[...]).astype(bf16)

    full = lambda a: pl.BlockSpec(a.shape, lambda i, nd=a.ndim: (0,) * nd)
    return pl.pallas_call(
        body, name="ssd_fwd", grid=(nch,),
        in_specs=[pl.BlockSpec((CH, W_PROJ), lambda i: (i, 0))] + [full(a) for a in (conv_w, conv_b, dt_bias, a_log, d_ssd, g_ssd)],
        out_specs=[pl.BlockSpec((CH, D), lambda i: (i, 0)), pl.BlockSpec((CH, D), lambda i: (i, 0)),
                   pl.BlockSpec((1, 2, NSTATE, 512), lambda i: (i, 0, 0, 0))],
        out_shape=[jax.ShapeDtypeStruct((nch * CH, D), f32), jax.ShapeDtypeStruct((nch * CH, D), bf16),
                   jax.ShapeDtypeStruct((nch, 2, NSTATE, 512), f32)],
        scratch_shapes=[pltpu.VMEM((8, D_XBC), f32), pltpu.VMEM((2, NSTATE, 512), f32)],
        compiler_params=_cp(("arbitrary",)))(proj, conv_w, conv_b, dt_bias, a_log, d_ssd, g_ssd)


def _ssd_bwd(proj, y, dys, du, states, conv_w, conv_b, dt_bias, a_log, d_ssd, g_ssd, nch):
    def body(p_ref, ph_ref, y_ref, dys_ref, du_ref, st_ref, cw_ref, cb_ref, dtb_ref, al_ref, d_ref, g_ref,
             dp_ref, dcw_ref, dcb_ref, ddtb_ref, dal_ref, dd_ref, dg_ref, nxt8_ref, dst_ref):
        step = pl.program_id(0)
        i = nch - 1 - step

        @pl.when(step == 0)
        def _():
            nxt8_ref[...] = jnp.zeros_like(nxt8_ref)
            dst_ref[...] = jnp.zeros_like(dst_ref)
            for r in (dcw_ref, dcb_ref, ddtb_ref, dal_ref, dd_ref, dg_ref):
                r[...] = jnp.zeros_like(r)

        P = p_ref[...]
        prev8 = jnp.where(i > 0, ph_ref[...][:, O_XBC:O_XBC + D_XBC], 0.0)
        c = _ssd_common(i, P, prev8, cw_ref[...], cb_ref[...], dtb_ref[...], al_ref[...])
        xbc, z, e = c["xbc"], c["z"], c["e"]
        x = xbc[:, 0:D]
        yv = y_ref[...]
        sz = _sigmoid(z)
        silu_z = z * sz
        dyg, dg8 = _rms_bwd(dys_ref[...], yv * silu_z, g_ref[...])
        dg_ref[...] += dg8
        dy = dyg * silu_z
        dz = dyg * yv * (sz * (1.0 + z * (1.0 - sz)))
        d_e = jnp.dot(d_ref[...], e, preferred_element_type=f32, precision=HI)
        dd_ref[...] += _rsum8(dy * x)
        xdt = x * c["dt_e"]
        a_last_e = c["acs_e"][CH - 1:CH, :]
        e_end = jnp.exp(a_last_e - c["acs_e"])
        w_end = xdt * e_end
        e_acs = jnp.exp(c["acs_e"])
        dy_dec = dy * e_acs
        m0, m1 = _pair_masks()
        lane16 = lax.broadcasted_iota(jnp.int32, (1, HEADS), 1)
        row16 = lax.broadcasted_iota(jnp.int32, (HEADS, 1), 0)
        dacs = jnp.zeros((CH, HEADS), f32)
        dacs_t = jnp.zeros((HEADS, CH), f32)
        dxdt_parts, dbs, dcs, zparts, yoff_parts = [], [], [], [], []
        dlast_parts = []
        for g in range(2):
            sl = slice(512 * g, 512 * (g + 1))
            bg = xbc[:, D + NSTATE * g:D + NSTATE * (g + 1)].astype(bf16)
            cg = xbc[:, D + 2 * NSTATE + NSTATE * g:D + 2 * NSTATE + NSTATE * (g + 1)].astype(bf16)
            gmat = lax.dot_general(cg, bg, (((1,), (1,)), ((), ())), preferred_element_type=f32)
            st = st_ref[0, g]
            dstn = dst_ref[g]
            dstn_b = dstn.astype(bf16)
            y_off = jnp.dot(cg, st.astype(bf16), preferred_element_type=f32) * e_acs[:, sl]
            yoff_parts.append(y_off)
            bds = jnp.dot(bg, dstn_b, preferred_element_type=f32)
            zparts.append(w_end[:, sl] * bds)
            dlast_parts.append(jnp.sum(dstn * st, axis=0, keepdims=True) * jnp.exp(a_last_e[:, sl]))
            dg_acc = jnp.zeros((CH, CH), f32)
            dxd = []
            for pr in range(4):
                lo = 128 * (4 * g + pr)
                xp2 = xdt[:, lo:lo + 128].astype(bf16)
                dy2 = dy[:, lo:lo + 128]
                outp = jnp.zeros((CH, 128), f32)
                for hh, msk in ((0, m0), (1, m1)):
                    h = 8 * g + 2 * pr + hh
                    lm = _lmat(c, g, 0, h)
                    dyh = (dy2 * msk).astype(bf16)
                    mh = (gmat * lm).astype(bf16)
                    outp = outp + lax.dot_general(mh, dyh, (((0,), (0,)), ((), ())), preferred_element_type=f32)
                    dm = lax.dot_general(dyh, xp2, (((1,), (1,)), ((), ())), preferred_element_type=f32)
                    dml = dm * lm
                    dg_acc = dg_acc + dml
                    q = dml * gmat
                    rs = jnp.sum(q, axis=1, keepdims=True)
                    cs = jnp.sum(q, axis=0, keepdims=True)
                    dacs = dacs + jnp.where(lane16 == h, rs, 0.0)
                    dacs_t = dacs_t + jnp.where(row16 == h, cs, 0.0)
                dxd.append(outp)
            dxdt_parts.append(jnp.concatenate(dxd, axis=1) + e_end[:, sl] * bds)
            dgb = dg_acc.astype(bf16)
            dcs.append(jnp.dot(dgb, bg, preferred_element_type=f32)
                       + lax.dot_general(dy_dec[:, sl].astype(bf16), st.astype(bf16), (((1,), (1,)), ((), ())),
                                         preferred_element_type=f32))
            dbs.append(lax.dot_general(dgb, cg, (((0,), (0,)), ((), ())), preferred_element_type=f32)
                       + lax.dot_general(w_end[:, sl].astype(bf16), dstn_b, (((1,), (1,)), ((), ())),
                                         preferred_element_type=f32))
            dst_ref[g] = dstn * jnp.exp(a_last_e[:, sl]) + lax.dot_general(
                cg, dy_dec[:, sl].astype(bf16), (((0,), (0,)), ((), ())), preferred_element_type=f32)
        dxdt = jnp.concatenate(dxdt_parts, axis=1)
        zfull = jnp.concatenate(zparts, axis=1)
        y_off_full = jnp.concatenate(yoff_parts, axis=1)
        dlast = jnp.concatenate(dlast_parts, axis=1)
        red = lambda v: lax.dot_general(v, e, (((1,), (1,)), ((), ())), preferred_element_type=f32, precision=HI)
        eye16 = jnp.where(lax.broadcasted_iota(jnp.int32, (HEADS, HEADS), 0) == lax.broadcasted_iota(jnp.int32, (HEADS, HEADS), 1), 1.0, 0.0)
        dacs = dacs - lax.dot_general(dacs_t, eye16, (((0,), (0,)), ((), ())), preferred_element_type=f32, precision=HI)
        zred = red(zfull)
        dacs = dacs + red(dy * y_off_full) - zred
        last_term = jnp.sum(zred, axis=0, keepdims=True) + red(dlast)
        rowc = lax.broadcasted_iota(jnp.int32, (CH, 1), 0)
        dacs = dacs + jnp.where(rowc == CH - 1, last_term, 0.0)
        r_i = lax.broadcasted_iota(jnp.int32, (CH, CH), 0)
        c_i = lax.broadcasted_iota(jnp.int32, (CH, CH), 1)
        ddta = jnp.dot(jnp.where(c_i >= r_i, 1.0, 0.0), dacs, preferred_element_type=f32, precision=HI)
        ddt = ddta * c["a"] + red(dxdt * x)
        dal_ref[...] += _rsum8(ddta * c["dt"] * c["a"])
        ddt_raw = ddt * _sigmoid(c["pre"]) * c["live"]
        ddtb_ref[...] += _rsum8(ddt_raw)
        dx = dy * d_e + dxdt * c["dt_e"]
        dxbc = jnp.concatenate([dx, dbs[0], dbs[1], dcs[0], dcs[1]], axis=1)
        sg = c["sg"]
        dxc = dxbc * (sg * (1.0 + c["xc"] * (1.0 - sg)))
        dcb_ref[...] += _rsum8(dxc)
        taps = c["taps"]
        row8 = lax.broadcasted_iota(jnp.int32, (8, 1), 0)
        for k in range(4):
            dcw_ref[...] += jnp.where(row8 == k, jnp.sum(dxc * taps[3 - k], axis=0, keepdims=True), 0.0)
        cw = cw_ref[...]
        dxp = cw[3:4] * dxc
        nxt8 = nxt8_ref[...]
        for j in (1, 2, 3):
            rolled = pltpu.roll(dxc, CH - j, 0)
            fix = pltpu.roll(nxt8, 8 - j, 0)
            bot = jnp.where(row8 >= 8 - j, fix, rolled[CH - 8:CH])
            dxp = dxp + cw[3 - j:4 - j] * jnp.concatenate([rolled[:CH - 8], bot], axis=0)
        nxt8_ref[...] = dxc[0:8]
        dp_ref[:, O_Z:O_Z + D] = dz.astype(bf16)
        dp_ref[:, O_U:O_U + D] = du_ref[...].astype(bf16)
        dp_ref[:, O_XBC:O_XBC + D_XBC] = dxp.astype(bf16)
        dp_ref[:, O_DT:W_PROJ] = jnp.zeros((CH, W_PROJ - O_DT), bf16)
        dp_ref[:, O_DT:O_DT + HEADS] = ddt_raw.astype(bf16)

    full = lambda a: pl.BlockSpec(a.shape, lambda s, nd=a.ndim: (0,) * nd)
    rev = lambda s: (nch - 1 - s, 0)
    acc = lambda cdim: pl.BlockSpec((8, cdim), lambda s: (0, 0))
    return pl.pallas_call(
        body, name="ssd_bwd", grid=(nch,),
        in_specs=[pl.BlockSpec((CH, W_PROJ), rev),
                  pl.BlockSpec((8, W_PROJ), lambda s: (jnp.maximum((nch - 1 - s) * (CH // 8) - 1, 0), 0)),
                  pl.BlockSpec((CH, D), rev), pl.BlockSpec((CH, D), rev), pl.BlockSpec((CH, D), rev),
                  pl.BlockSpec((1, 2, NSTATE, 512), lambda s: (nch - 1 - s, 0, 0, 0))]
        + [full(a) for a in (conv_w, conv_b, dt_bias, a_log, d_ssd, g_ssd)],
        out_specs=[pl.BlockSpec((CH, W_PROJ), rev), acc(D_XBC), acc(D_XBC), acc(HEADS), acc(HEADS), acc(D), acc(D)],
        out_shape=[jax.ShapeDtypeStruct((nch * CH, W_PROJ), bf16)]
        + [jax.ShapeDtypeStruct((8, cdim), f32) for cdim in (D_XBC, D_XBC, HEADS, HEADS, D, D)],
        scratch_shapes=[pltpu.VMEM((8, D_XBC), f32), pltpu.VMEM((2, NSTATE, 512), f32)],
        compiler_params=_cp(("arbitrary",)))(proj, proj, y, dys, du, states, conv_w, conv_b, dt_bias, a_log, d_ssd, g_ssd)


def _perm():
    r = lax.broadcasted_iota(jnp.int32, (CH, CH), 0)
    c = lax.broadcasted_iota(jnp.int32, (CH, CH), 1)
    return jnp.where(((r % NSEG) * SEG_ROWS + r // NSEG) == c, 1.0, 0.0).astype(bf16)


def _scan_fwd(bu_ref, out_ref, q, ar, ai, init):
    def step(j, carry):
        re, im = carry
        r0 = pl.multiple_of(j * NSEG, NSEG)
        b = bu_ref[pl.ds(r0, NSEG), :]
        nre = ar * re - ai * im + b[:, :512]
        nim = ar * im + ai * re + b[:, 512:]
        if out_ref is not None:
            out_ref[0, q, pl.ds(r0, NSEG), :] = jnp.concatenate([nre, nim], axis=1)
        return nre, nim
    return lax.fori_loop(0, SEG_ROWS, step, init)


def _s5_fwd_ends(proj3, bbq, ab, nblk):
    def body(u_ref, bb_ref, ab_ref, o_ref, bu_ref):
        jb = pl.program_id(0)

        @pl.when(jb == 0)
        def _():
            o_ref[...] = jnp.zeros_like(o_ref)
        xp = jnp.dot(_perm(), u_ref[...].reshape(CH, D).astype(bf16), preferred_element_type=f32).astype(bf16)
        for q in range(NQ):
            bu_ref[...] = jnp.dot(xp[:, 128 * q:128 * (q + 1)], bb_ref[q], preferred_element_type=f32)
            ar = jnp.broadcast_to(ab_ref[q, 0:1, :], (NSEG, 512))
            ai = jnp.broadcast_to(ab_ref[q, 1:2, :], (NSEG, 512))
            st = o_ref[q]
            re, im = _scan_fwd(bu_ref, None, q, ar, ai, (st[:, :512], st[:, 512:]))
            o_ref[q] = jnp.concatenate([re, im], axis=1)

    return pl.pallas_call(
        body, name="s5_fwd_ends", grid=(nblk,),
        in_specs=[pl.BlockSpec((NSEG, SEG_ROWS, D), lambda j: (0, j, O_U // D)),
                  pl.BlockSpec(bbq.shape, lambda j: (0, 0, 0)), pl.BlockSpec(ab.shape, lambda j: (0, 0, 0))],
        out_specs=pl.BlockSpec((NQ, NSEG, D), lambda j: (0, 0, 0)),
        out_shape=jax.ShapeDtypeStruct((NQ, NSEG, D), f32),
        scratch_shapes=[pltpu.VMEM((CH, D), f32)],
        compiler_params=_cp(("arbitrary",)))(proj3, bbq, ab)


def _s5_fwd(proj3, bbq, ccq, ab, s_enter, d_skip, nblk):
    def body(u_ref, bb_ref, cc_ref, ab_ref, se_ref, d_ref, s_ref, yl_ref, y5_ref, bu_ref, st_ref):
        jb = pl.program_id(0)

        @pl.when(jb == 0)
        def _():
            st_ref[...] = se_ref[...]
        u = u_ref[...].reshape(CH, D)
        perm = _perm()
        xp = jnp.dot(perm, u.astype(bf16), preferred_element_type=f32).astype(bf16)
        ys = []
        for q in range(NQ):
            bu_ref[...] = jnp.dot(xp[:, 128 * q:128 * (q + 1)], bb_ref[q], preferred_element_type=f32)
            ar = jnp.broadcast_to(ab_ref[q, 0:1, :], (NSEG, 512))
            ai = jnp.broadcast_to(ab_ref[q, 1:2, :], (NSEG, 512))
            st = st_ref[q]
            re, im = _scan_fwd(bu_ref, s_ref, q, ar, ai, (st[:, :512], st[:, 512:]))
            st_ref[q] = jnp.concatenate([re, im], axis=1)
            sq = s_ref[0, q].astype(bf16)
            su = lax.dot_general(perm, sq, (((0,), (0,)), ((), ())), preferred_element_type=f32).astype(bf16)
            ys.append(jnp.dot(su, cc_ref[q], preferred_element_type=f32))
        yl = jnp.concatenate(ys, axis=1) + u * d_ref[...]
        yl_ref[...] = yl.reshape(NSEG, SEG_ROWS, D)
        y5 = 0.5 * yl * (1.0 + lax.erf(yl * (1.0 / math.sqrt(2.0))))
        y5_ref[...] = y5.astype(bf16).reshape(NSEG, SEG_ROWS, D)

    seg = nblk * SEG_ROWS
    return pl.pallas_call(
        body, name="s5_fwd", grid=(nblk,),
        in_specs=[pl.BlockSpec((NSEG, SEG_ROWS, D), lambda j: (0, j, O_U // D)),
                  pl.BlockSpec(bbq.shape, lambda j: (0, 0, 0)), pl.BlockSpec(ccq.shape, lambda j: (0, 0, 0)),
                  pl.BlockSpec(ab.shape, lambda j: (0, 0, 0)), pl.BlockSpec(s_enter.shape, lambda j: (0, 0, 0)),
                  pl.BlockSpec(d_skip.shape, lambda j: (0, 0))],
        out_specs=[pl.BlockSpec((1, NQ, CH, D), lambda j: (j, 0, 0, 0)),
                   pl.BlockSpec((NSEG, SEG_ROWS, D), lambda j: (0, j, 0)), pl.BlockSpec((NSEG, SEG_ROWS, D), lambda j: (0, j, 0))],
        out_shape=[jax.ShapeDtypeStruct((nblk, NQ, CH, D), f32), jax.ShapeDtypeStruct((NSEG, seg, D), f32),
                   jax.ShapeDtypeStruct((NSEG, seg, D), bf16)],
        scratch_shapes=[pltpu.VMEM((CH, D), f32), pltpu.VMEM((NQ, NSEG, D), f32)],
        compiler_params=_cp(("arbitrary",)))(proj3, bbq, ccq, ab, s_enter, d_skip)


def _scan_bwd(gs_ref, a_ref, s_ref, q, ar, ai, init, dinit):
    def step(t, carry):
        re, im, dar, dai = carry
        j = SEG_ROWS - 1 - t
        r0 = pl.multiple_of(j * NSEG, NSEG)
        if s_ref is not None:
            s = s_ref[0, q, pl.ds(r0, NSEG), :]
            sre, sim = s[:, :512], s[:, 512:]
            dar = dar + re * sre + im * sim
            dai = dai + im * sre - re * sim
        g = gs_ref[pl.ds(r0, NSEG), :]
        nre = g[:, :512] + ar * re + ai * im
        nim = g[:, 512:] - ai * re + ar * im
        if a_ref is not None:
            a_ref[pl.ds(r0, NSEG), :] = jnp.concatenate([nre, nim], axis=1)
        return nre, nim, dar, dai
    return lax.fori_loop(0, SEG_ROWS, step, init + dinit)


def _s5_bwd_starts(dyl3, cctq, ab, nblk):
    def body(dy_ref, cct_ref, ab_ref, o_ref, gs_ref):
        @pl.when(pl.program_id(0) == 0)
        def _():
            o_ref[...] = jnp.zeros_like(o_ref)
        dp = jnp.dot(_perm(), dy_ref[...].reshape(CH, D).astype(bf16), preferred_element_type=f32).astype(bf16)
        zero = jnp.zeros((NSEG, 512), f32)
        for q in range(NQ):
            gs_ref[...] = jnp.dot(dp[:, 128 * q:128 * (q + 1)], cct_ref[q], preferred_element_type=f32)
            ar = jnp.broadcast_to(ab_ref[q, 0:1, :], (NSEG, 512))
            ai = jnp.broadcast_to(ab_ref[q, 1:2, :], (NSEG, 512))
            st = o_ref[q]
            re, im, _, _ = _scan_bwd(gs_ref, None, None, q, ar, ai, (st[:, :512], st[:, 512:]), (zero, zero))
            o_ref[q] = jnp.concatenate([re, im], axis=1)

    return pl.pallas_call(
        body, name="s5_bwd_starts", grid=(nblk,),
        in_specs=[pl.BlockSpec((NSEG, SEG_ROWS, D), lambda s: (0, nblk - 1 - s, 0)),
                  pl.BlockSpec(cctq.shape, lambda s: (0, 0, 0)), pl.BlockSpec(ab.shape, lambda s: (0, 0, 0))],
        out_specs=pl.BlockSpec((NQ, NSEG, D), lambda s: (0, 0, 0)),
        out_shape=jax.ShapeDtypeStruct((NQ, NSEG, D), f32),
        scratch_shapes=[pltpu.VMEM((CH, D), f32)],
        compiler_params=_cp(("arbitrary",)))(dyl3, cctq, ab)


def _s5_bwd(proj3, dyl3, s_all, bbtq, cctq, ab, a_enter, d_skip, nblk):
    def body(u_ref, dy_ref, s_ref, bbt_ref, cct_ref, ab_ref, ae_ref, d_ref,
             du_ref, dcc_ref, dbb_ref, dab_ref, dd_ref, gs_ref, a_ref, st_ref):
        step = pl.program_id(0)

        @pl.when(step == 0)
        def _():
            st_ref[...] = ae_ref[...]
            for r in (dcc_ref, dbb_ref, dab_ref, dd_ref):
                r[...] = jnp.zeros_like(r)
        u = u_ref[...].reshape(CH, D)
        dyl = dy_ref[...].reshape(CH, D)
        dd_ref[...] += _rsum8(dyl * u)
        perm = _perm()
        xp = jnp.dot(perm, u.astype(bf16), preferred_element_type=f32).astype(bf16)
        dp = jnp.dot(perm, dyl.astype(bf16), preferred_element_type=f32).astype(bf16)
        dus = []
        for q in range(NQ):
            gs_ref[...] = jnp.dot(dp[:, 128 * q:128 * (q + 1)], cct_ref[q], preferred_element_type=f32)
            ar = jnp.broadcast_to(ab_ref[q, 0:1, :], (NSEG, 512))
            ai = jnp.broadcast_to(ab_ref[q, 1:2, :], (NSEG, 512))
            st = st_ref[q]
            dab = dab_ref[q]
            re, im, dar, dai = _scan_bwd(gs_ref, a_ref, s_ref, q, ar, ai, (st[:, :512], st[:, 512:]),
                                         (dab[:, :512], dab[:, 512:]))
            st_ref[q] = jnp.concatenate([re, im], axis=1)
            dab_ref[q] = jnp.concatenate([dar, dai], axis=1)
            ab16 = a_ref[...].astype(bf16)
            dcc_ref[q] += lax.dot_general(dp[:, 128 * q:128 * (q + 1)], s_ref[0, q].astype(bf16), (((0,), (0,)), ((), ())),
                                          preferred_element_type=f32)
            dbb_ref[q] += lax.dot_general(xp[:, 128 * q:128 * (q + 1)], ab16, (((0,), (0,)), ((), ())),
                                          preferred_element_type=f32)
            au = lax.dot_general(perm, ab16, (((0,), (0,)), ((), ())), preferred_element_type=f32).astype(bf16)
            dus.append(jnp.dot(au, bbt_ref[q], preferred_element_type=f32))
        du = jnp.concatenate(dus, axis=1) + dyl * d_ref[...]
        du_ref[...] = du.reshape(NSEG, SEG_ROWS, D)

    seg = nblk * SEG_ROWS
    full3 = lambda a: pl.BlockSpec(a.shape, lambda s: (0, 0, 0))
    return pl.pallas_call(
        body, name="s5_bwd", grid=(nblk,),
        in_specs=[pl.BlockSpec((NSEG, SEG_ROWS, D), lambda s: (0, nblk - 1 - s, O_U // D)),
                  pl.BlockSpec((NSEG, SEG_ROWS, D), lambda s: (0, nblk - 1 - s, 0)),
                  pl.BlockSpec((1, NQ, CH, D), lambda s: (nblk - 1 - s, 0, 0, 0)),
                  full3(bbtq), full3(cctq), full3(ab), full3(a_enter), pl.BlockSpec(d_skip.shape, lambda s: (0, 0))],
        out_specs=[pl.BlockSpec((NSEG, SEG_ROWS, D), lambda s: (0, nblk - 1 - s, 0)),
                   pl.BlockSpec((NQ, 128, D), lambda s: (0, 0, 0)), pl.BlockSpec((NQ, 128, D), lambda s: (0, 0, 0)),
                   pl.BlockSpec((NQ, NSEG, D), lambda s: (0, 0, 0)), pl.BlockSpec((8, D), lambda s: (0, 0))],
        out_shape=[jax.ShapeDtypeStruct((NSEG, seg, D), f32), jax.ShapeDtypeStruct((NQ, 128, D), f32),
                   jax.ShapeDtypeStruct((NQ, 128, D), f32), jax.ShapeDtypeStruct((NQ, NSEG, D), f32),
                   jax.ShapeDtypeStruct((8, D), f32)],
        scratch_shapes=[pltpu.VMEM((CH, D), f32), pltpu.VMEM((CH, D), f32), pltpu.VMEM((NQ, NSEG, D), f32)],
        compiler_params=_cp(("arbitrary",)))(proj3, dyl3, s_all, bbtq, cctq, ab, a_enter, d_skip)


def _s5_tables(lam_re, lam_im, log_step, b_re, b_im):
    step = jnp.exp(log_step)[:, None]
    mag = jnp.exp(lam_re * step)
    ab_re = mag * jnp.cos(lam_im * step)
    ab_im = mag * jnp.sin(lam_im * step)
    den = lam_re * lam_re + lam_im * lam_im
    coef_re = ((ab_re - 1.0) * lam_re + ab_im * lam_im) / den
    coef_im = (ab_im * lam_re - (ab_re - 1.0) * lam_im) / den
    bb_re = coef_re[..., None] * b_re - coef_im[..., None] * b_im
    bb_im = coef_re[..., None] * b_im + coef_im[..., None] * b_re
    return ab_re, ab_im, bb_re, bb_im


def _blockdiag_in(m_re, m_im):
    eye = jnp.eye(8, dtype=f32)

    def one(m):
        m = m.reshape(NQ, 8, S5_P, 16)
        return jnp.einsum("qgph,gk->qghkp", m, eye).reshape(NQ, 128, 512)
    return jnp.concatenate([one(m_re), one(m_im)], axis=2)


def _blockdiag_in_grad(dm):
    def one(x):
        x = x.reshape(NQ, 8, 16, 8, S5_P)
        return jnp.einsum("qghgp->qgph", x).reshape(NQ * 8, S5_P, 16)
    return one(dm[:, :, :512]), one(dm[:, :, 512:])


def _lanes(t_re, t_im):
    return jnp.concatenate([t_re.reshape(NQ, 512), t_im.reshape(NQ, 512)], axis=1)


def _cpow(re, im, n):
    rr, ri = None, None
    br, bi = re, im
    while n:
        if n & 1:
            rr, ri = (br, bi) if rr is None else (rr * br - ri * bi, rr * bi + ri * br)
        n >>= 1
        if n:
            br, bi = br * br - bi * bi, 2.0 * br * bi
    return rr, ri


def _chain_segments(loc, pw, reverse):
    pr, pi = pw[:, None, :512], pw[:, None, 512:]
    cur = jnp.zeros_like(loc[:, 0:1, :])
    outs = [None] * NSEG
    order = range(NSEG - 1, -1, -1) if reverse else range(NSEG)
    for k in order:
        outs[k] = cur
        cre, cim = cur[..., :512], cur[..., 512:]
        lre, lim = loc[:, k:k + 1, :512], loc[:, k:k + 1, 512:]
        cur = jnp.concatenate([pr * cre - pi * cim + lre, pr * cim + pi * cre + lim], axis=-1)
    return jnp.concatenate(outs, axis=1)


def _local_step(x2, tgt2, meta, p, wb):
    seq = x2.shape[0]
    nch = 1 + seq // CH
    lp = nch * CH
    seg = lp // NSEG
    nblk = seg // SEG_ROWS
    h0 = jnp.concatenate([jnp.zeros((CH - N_META, D), f32), meta, x2], axis=0)
    tgt = jnp.concatenate([jnp.zeros((CH, D), f32), tgt2], axis=0)

    (proj, n0), _ = _mm("in_proj", nch, [h0], [p["g_mix"]], wb["w_in"],
                        lambda r, q: _rms(r[0], q[0]).astype(bf16),
                        lambda i, acc, r, q: ((acc, _rms(r[0], q[0])), ()),
                        [(W_PROJ, f32), (D, bf16)], [])
    y, y_ssd, states = _ssd_fwd(proj, p["conv_w"], p["conv_b"], p["dt_bias"], p["a_log"], p["d_ssd"], p["g_ssd"], nch)

    ab_re, ab_im, bb_re, bb_im = _s5_tables(p["lam_re"], p["lam_im"], p["log_step"], p["b_re"], p["b_im"])
    ab = jnp.stack([ab_re.reshape(NQ, 512), ab_im.reshape(NQ, 512)], axis=1)
    bbq = _blockdiag_in(bb_re, bb_im)
    ccq = _blockdiag_in(jnp.swapaxes(p["c_re"], 1, 2), -jnp.swapaxes(p["c_im"], 1, 2))
    cctq = ccq.astype(bf16)
    ccq_t = jnp.swapaxes(ccq, 1, 2).astype(bf16)
    bbq_b = bbq.astype(bf16)
    bbtq = jnp.swapaxes(bbq, 1, 2).astype(bf16)
    pw_re, pw_im = _cpow(ab_re, ab_im, seg)
    proj3 = proj.reshape(NSEG, seg, W_PROJ)
    d_skip = p["d_s5"].reshape(1, D)
    ends = _s5_fwd_ends(proj3, bbq_b, ab, nblk)
    s_enter = _chain_segments(ends, _lanes(pw_re, pw_im), reverse=False)
    s_all, ylin3, y53 = _s5_fwd(proj3, bbq_b, ccq_t, ab, s_enter, d_skip, nblk)
    ylin = ylin3.reshape(lp, D)
    y5 = y53.reshape(lp, D)

    def glu_post(i, acc, r, q):
        v = acc + q[0]
        v1, v2 = v[:, :D], v[:, D:]
        return (v, _rms(v1 * _sigmoid(v2), q[1])), ()
    (v, y_s5), _ = _mm("glu", nch, [y5], [p["b_glu"], p["g_s5"]], wb["w_glu"], lambda r, q: r[0], glu_post,
                       [(2 * D, f32), (D, bf16)], [])
    (h1,), _ = _mm("out_proj", nch, [y_ssd, y_s5, h0], [], wb["w_out"],
                   lambda r, q: jnp.concatenate([r[0], r[1]], axis=1),
                   lambda i, acc, r, q: ((r[2] + acc,), ()), [(D, f32)], [])
    (m, n1), _ = _mm("up_proj", nch, [h1], [p["g_mlp"]], wb["w_up"],
                     lambda r, q: _rms(r[0], q[0]).astype(bf16),
                     lambda i, acc, r, q: ((acc, _rms(r[0], q[0])), ()),
                     [(4 * D, f32), (D, bf16)], [])
    (h2, act), _ = _mm("down_proj", nch, [m, h1], [], wb["w_down"],
                       lambda r, q: jnp.square(jnp.maximum(r[0], 0.0)).astype(bf16),
                       lambda i, acc, r, q: ((r[1] + acc, jnp.square(jnp.maximum(r[0], 0.0))), ()),
                       [(D, f32), (4 * D, bf16)], [])

    def final_fn(i, h, t, g):
        live = jnp.where(i > 0, 1.0, 0.0)
        out = _rms(h, g)
        err = (out - t) * live
        dh, dg8 = _rms_bwd(err * (1.0 / D), h, g)
        return (dh,), (_rsum8(err * err), dg8)
    (dh2,), (loss8, dgf8) = _rowwise("final", final_fn, nch, [h2, tgt], [p["g_final"]], [(D, f32)], [D, D])
    loss = 0.5 / D * jnp.sum(loss8)

    (dm,), _ = _mm("down_bwd", nch, [dh2, m], [], wb["w_down_t"], lambda r, q: r[0].astype(bf16),
                   lambda i, acc, r, q: ((acc * 2.0 * jnp.maximum(r[1], 0.0),), ()), [(4 * D, bf16)], [])
    dw_down = _mm_tn("dw_down", act, dh2.astype(bf16), 1024, 1024)

    def up_post(i, acc, r, q):
        dh, dg8 = _rms_bwd(acc, r[1], q[0])
        return (r[2] + dh,), (dg8,)
    (dh1,), (dgmlp8,) = _mm("up_bwd", nch, [dm, h1, dh2], [p["g_mlp"]], wb["w_up_t"], lambda r, q: r[0], up_post,
                            [(D, f32)], [D])
    dw_up = _mm_tn("dw_up", n1, dm, 1024, 1024)
    dh1_b = dh1.astype(bf16)

    def out_post(i, acc, r, q):
        v_ = r[1]
        v1, v2 = v_[:, :D], v_[:, D:]
        s2 = _sigmoid(v2)
        glu = v1 * s2
        dglu, dg8 = _rms_bwd(acc[:, D:], glu, q[0])
        dv = jnp.concatenate([dglu * s2, dglu * v1 * s2 * (1.0 - s2)], axis=1)
        return (acc[:, :D], dv), (dg8, _rsum8(dv))
    (dys, dv), (dgs58, dbglu8) = _mm("out_bwd", nch, [dh1_b, v], [p["g_s5"]], wb["w_out_t"], lambda r, q: r[0], out_post,
                                     [(D, f32), (2 * D, bf16)], [D, 2 * D])
    dw_out = jnp.concatenate([_mm_tn("dw_out_a", y_ssd, dh1_b, 1024, 1024), _mm_tn("dw_out_b", y_s5, dh1_b, 1024, 1024)], axis=0)

    def glu_bwd_post(i, acc, r, q):
        yl = r[1]
        cdf = 0.5 * (1.0 + lax.erf(yl * (1.0 / math.sqrt(2.0))))
        pdf = jnp.exp(-0.5 * yl * yl) * (1.0 / math.sqrt(2.0 * math.pi))
        return (acc * (cdf + yl * pdf),), ()
    (dylin,), _ = _mm("glu_bwd", nch, [dv, ylin], [], wb["w_glu_t"], lambda r, q: r[0], glu_bwd_post, [(D, f32)], [])
    dw_glu = _mm_tn("dw_glu", y5, dv, 1024, 1024)

    dyl3 = dylin.reshape(NSEG, seg, D)
    starts = _s5_bwd_starts(dyl3, cctq, ab, nblk)
    a_enter = _chain_segments(starts, _lanes(pw_re, -pw_im), reverse=True)
    du3, dcc, dbb, dab8, dds5 = _s5_bwd(proj3, dyl3, s_all, bbtq, cctq, ab, a_enter, d_skip, nblk)

    dproj, dcw8, dcb8, ddtb8, dal8, dd8, dgssd8 = _ssd_bwd(
        proj, y, dys, du3.reshape(lp, D), states, p["conv_w"], p["conv_b"], p["dt_bias"], p["a_log"], p["d_ssd"], p["g_ssd"], nch)

    def in_post(i, acc, r, q):
        dh, dg8 = _rms_bwd(acc, r[1], q[0])
        return (r[2] + dh,), (dg8,)
    (dh0,), (dgmix8,) = _mm("in_bwd", nch, [dproj, h0, dh1], [p["g_mix"]], wb["w_in_t"], lambda r, q: r[0], in_post,
                            [(D, f32)], [D])
    dw_in = _mm_tn("dw_in", n0, dproj, 512, W_PROJ)

    dab = jnp.sum(dab8, axis=1)
    d_ab_re, d_ab_im = dab[:, :512].reshape(NQ * 8, S5_P), dab[:, 512:].reshape(NQ * 8, S5_P)
    dbb_re, dbb_im = _blockdiag_in_grad(dbb)
    dcr, dci = _blockdiag_in_grad(dcc)
    _, vjp = jax.vjp(_s5_tables, p["lam_re"], p["lam_im"], p["log_step"], p["b_re"], p["b_im"])
    dlam_re, dlam_im, dlog_step, db_re, db_im = vjp((d_ab_re, d_ab_im, dbb_re, dbb_im))

    s8 = lambda a: jnp.sum(a, axis=0, keepdims=True)
    hsum = lambda a: jnp.sum(s8(a).reshape(HEADS, HEAD_DIM), axis=1).reshape(1, HEADS)
    grads = dict(
        g_mix=s8(dgmix8), w_in=dw_in, conv_w=dcw8[0:4], conv_b=s8(dcb8), dt_bias=s8(ddtb8), a_log=s8(dal8),
        d_ssd=hsum(dd8), g_ssd=s8(dgssd8), lam_re=dlam_re, lam_im=dlam_im, log_step=dlog_step, b_re=db_re, b_im=db_im,
        c_re=jnp.swapaxes(dcr, 1, 2), c_im=-jnp.swapaxes(dci, 1, 2), d_s5=s8(dds5).reshape(NQ * 8, 16),
        w_glu=dw_glu, b_glu=s8(dbglu8), g_s5=s8(dgs58), w_out=dw_out, g_mlp=s8(dgmlp8), w_up=dw_up, w_down=dw_down,
        g_final=s8(dgf8).reshape(D))
    return loss, dh0, grads


def _perm_w_in(w):
    return jnp.concatenate([w[:, 0:1024], w[:, 2576:3600], w[:, 1024:2560], w[:, 2560:2576],
                            jnp.zeros((w.shape[0], W_PROJ - 3600), w.dtype)], axis=1)


def _unperm_w_in_grad(g):
    return jnp.concatenate([g[:, 0:1024], g[:, O_XBC:O_XBC + D_XBC], g[:, O_DT:O_DT + HEADS], g[:, O_U:O_U + D]], axis=1)


def _prep_weights(w_in, w_glu, w_out, w_up, w_down):
    b = lambda a: a.astype(bf16)
    wi = _perm_w_in(b(w_in))
    return dict(w_in=wi, w_in_t=wi.T, w_glu=b(w_glu), w_glu_t=b(w_glu).T, w_out=b(w_out), w_out_t=b(w_out).T,
                w_up=b(w_up), w_up_t=b(w_up).T, w_down=b(w_down), w_down_t=b(w_down).T)


MESH = pl.DeviceIdType.MESH
_ANY = pl.BlockSpec(memory_space=pl.ANY)


def _place():
    return lax.axis_index("x"), lax.axis_index("y"), lax.axis_index("c")


def _allgather8(x_shard, name):
    m_per, n = x_shard.shape

    def body(x_ref, out_ref, send_sems, recv_sems, local_sem):
        x, y, c = _place()
        me, sibling = (x, y, c), (x, y, 1 - c)
        chips = [(1 - x, y), (x, 1 - y), (1 - x, 1 - y)]

        def rows(px, py, pc):
            return out_ref.at[pl.ds((4 * px + 2 * py + pc) * m_per, m_per), :]

        def copy(k, block, to, src=None):
            return pltpu.make_async_remote_copy(
                src_ref=rows(*block) if src is None else src, dst_ref=rows(*block),
                send_sem=send_sems.at[k], recv_sem=recv_sems.at[k], device_id=to, device_id_type=MESH)

        mine = pltpu.make_async_copy(x_ref, rows(*me), local_sem)
        mine.start()
        first = [copy(0, me, sibling, src=x_ref)]
        first += [copy(1 + j, me, (*chip, c), src=x_ref) for j, chip in enumerate(chips)]
        for cp in first:
            cp.start()
        passed = [copy(4 + j, (*chip, c), sibling) for j, chip in enumerate(chips)]
        for j, chip in enumerate(chips):
            copy(1 + j, (*chip, c), me).wait_recv()
            passed[j].start()
        copy(0, sibling, me).wait_recv()
        for j, chip in enumerate(chips):
            copy(4 + j, (*chip, 1 - c), me).wait_recv()
        for cp in first + passed:
            cp.wait_send()
        mine.wait()

    return pl.pallas_call(
        body, name=name, out_shape=jax.ShapeDtypeStruct((8 * m_per, n), x_shard.dtype),
        in_specs=[_ANY], out_specs=_ANY,
        scratch_shapes=[pltpu.SemaphoreType.DMA((7,)), pltpu.SemaphoreType.DMA((7,)), pltpu.SemaphoreType.DMA])(x_shard)


def _swap_sibling(src, name):
    def body(src_ref, out_ref, send_sem, recv_sem):
        x, y, c = _place()
        cp = pltpu.make_async_remote_copy(src_ref=src_ref.at[1 - c], dst_ref=out_ref, send_sem=send_sem, recv_sem=recv_sem,
                                          device_id=(x, y, 1 - c), device_id_type=MESH)
        cp.start()
        cp.wait()

    return pl.pallas_call(
        body, name=name, out_shape=jax.ShapeDtypeStruct(src.shape[1:], src.dtype), in_specs=[_ANY], out_specs=_ANY,
        scratch_shapes=[pltpu.SemaphoreType.DMA, pltpu.SemaphoreType.DMA])(src)


def _exchange_chips(p, name):
    def body(p_ref, out_ref, send_sems, recv_sems):
        x, y, c = _place()
        chips = [(1 - x, y), (x, 1 - y), (1 - x, 1 - y)]
        cps = [pltpu.make_async_remote_copy(src_ref=p_ref.at[2 * cx + cy], dst_ref=out_ref.at[j], send_sem=send_sems.at[j],
                                            recv_sem=recv_sems.at[j], device_id=(cx, cy, c), device_id_type=MESH)
               for j, (cx, cy) in enumerate(chips)]
        for cp in cps:
            cp.start()
        for cp in cps:
            cp.wait()

    return pl.pallas_call(
        body, name=name, out_shape=jax.ShapeDtypeStruct((3,) + p.shape[1:], p.dtype), in_specs=[_ANY], out_specs=_ANY,
        scratch_shapes=[pltpu.SemaphoreType.DMA((3,)), pltpu.SemaphoreType.DMA((3,))])(p)


def _share_sibling(r, name):
    def body(r_ref, out_ref, send_sem, recv_sem, local_sem):
        x, y, c = _place()
        local = pltpu.make_async_copy(r_ref, out_ref.at[c], local_sem)
        local.start()
        cp = pltpu.make_async_remote_copy(src_ref=r_ref, dst_ref=out_ref.at[c], send_sem=send_sem, recv_sem=recv_sem,
                                          device_id=(x, y, 1 - c), device_id_type=MESH)
        cp.start()
        cp.wait()
        local.wait()

    return pl.pallas_call(
        body, name=name, out_shape=jax.ShapeDtypeStruct((2,) + r.shape, r.dtype), in_specs=[_ANY], out_specs=_ANY,
        scratch_shapes=[pltpu.SemaphoreType.DMA, pltpu.SemaphoreType.DMA, pltpu.SemaphoreType.DMA])(r)


PACK_ROWS = 4000
HALF_ROWS = PACK_ROWS // 2
R_IN, R_GLU, R_OUT, R_UP, R_DOWN, R_SPARE = 0, 900, 1412, 1924, 2948, 3972
SH_CONVW, SH_META = 4 * 384, 16 * 256
SPARE_ROWS = 17


def _pack_shard(w_in, w_glu, w_out, w_up, w_down, spare):
    dt = w_in.dtype
    parts = [w_in.reshape(900, D), w_glu.reshape(512, D), w_out, w_up, w_down]
    if spare is None:
        parts.append(jnp.zeros((PACK_ROWS - R_SPARE, D), dt))
    else:
        parts += [spare, jnp.zeros((PACK_ROWS - R_SPARE - spare.shape[0], D), dt)]
    return jnp.concatenate(parts, axis=0)


def _unpack_shard(slab):
    return (slab[R_IN:R_GLU].reshape(D, 900), slab[R_GLU:R_OUT].reshape(D, 512), slab[R_OUT:R_UP], slab[R_UP:R_DOWN],
            slab[R_DOWN:R_SPARE])


SMALL = [("g_mix", (1, 1024)), ("conv_b", (1, 1536)), ("dt_bias", (1, 16)), ("a_log", (1, 16)), ("d_ssd", (1, 16)),
         ("g_ssd", (1, 1024)), ("lam_re", (1, 64, 64)), ("lam_im", (1, 64, 64)), ("log_step", (1, 64)),
         ("b_re", (1, 64, 64, 16)), ("b_im", (1, 64, 64, 16)), ("c_re", (1, 64, 16, 64)), ("c_im", (1, 64, 16, 64)),
         ("d_s5", (1, 64, 16)), ("b_glu", (1, 2048)), ("g_s5", (1, 1024)), ("g_mlp", (1, 1024)), ("g_final", (1024,))]


def _pack_small(arrs, rows):
    flat = jnp.concatenate([a.reshape(-1).astype(f32) for a in arrs])
    return jnp.concatenate([flat, jnp.zeros((rows * D - flat.shape[0],), f32)]).reshape(rows, D)


def _unpack_small(slab, shapes):
    flat = slab.reshape(-1)
    out, o = [], 0
    for shp in shapes:
        n = math.prod(shp)
        out.append(flat[o:o + n].reshape(shp))
        o += n
    return out


def _sum8(g, rows):
    def body(g_ref, o_ref):
        acc = g_ref[0]
        for k in range(1, 8):
            acc = acc + g_ref[k]
        o_ref[...] = acc
    return pl.pallas_call(body, name="sum8", out_shape=jax.ShapeDtypeStruct((rows, D), f32),
                          compiler_params=_cp())(g.reshape(8, rows, D))


def _adamw(name, w, g, m, v, bm):
    def fn(i, w_, g_, m_, v_):
        m2 = ADAM_B1 * m_ + (1.0 - ADAM_B1) * g_
        v2 = ADAM_B2 * v_ + (1.0 - ADAM_B2) * jnp.square(g_)
        m_hat = m2 / (1.0 - ADAM_B1 ** ADAM_STEP)
        v_hat = v2 / (1.0 - ADAM_B2 ** ADAM_STEP)
        delta = -ADAM_LR * (m_hat / (jnp.sqrt(v_hat) + ADAM_EPS) + ADAM_WD * w_)
        return (delta, m2, v2), ()
    c = w.shape[1]
    (d, m2, v2), _ = _rowwise(name, fn, w.shape[0] // bm, [w, g, m, v], [], [(c, f32)] * 3, [], bm=bm)
    return d, m2, v2


def _add_rows(name, parts, out_dtype, rows, bm):
    def fn(i, *vs):
        acc = vs[0].astype(f32)
        for t in vs[1:]:
            acc = acc + t.astype(f32)
        return (acc,), ()
    (o,), _ = _rowwise(name, fn, rows // bm, parts, [], [(D, out_dtype)], [], bm=bm)
    return o


def kernel(x, meta_tokens, g_mix, w_in, conv_w, conv_b, dt_bias, a_log, d_ssd, g_ssd, lam_re, lam_im, log_step, b_re, b_im, c_re, c_im, d_s5, w_glu, b_glu, g_s5, w_out, g_mlp, w_up, w_down, g_final, loss_target, m_meta_tokens, m_g_mix, m_w_in, m_conv_w, m_conv_b, m_dt_bias, m_a_log, m_d_ssd, m_g_ssd, m_lam_re, m_lam_im, m_log_step, m_b_re, m_b_im, m_c_re, m_c_im, m_d_s5, m_w_glu, m_b_glu, m_g_s5, m_w_out, m_g_mlp, m_w_up, m_w_down, m_g_final, v_meta_tokens, v_g_mix, v_w_in, v_conv_w, v_conv_b, v_dt_bias, v_a_log, v_d_ssd, v_g_ssd, v_lam_re, v_lam_im, v_log_step, v_b_re, v_b_im, v_c_re, v_c_im, v_d_s5, v_w_glu, v_b_glu, v_g_s5, v_w_out, v_g_mlp, v_w_up, v_w_down, v_g_final):
    given = dict(locals())
    cx, cy, cc = _place()
    chip = 2 * cx + cy

    small_f = jnp.concatenate([conv_w.reshape(-1), meta_tokens.reshape(-1)])
    t_hi = small_f.astype(bf16)
    r_1 = small_f - t_hi.astype(f32)
    t_mid = r_1.astype(bf16)
    t_lo = (r_1 - t_mid.astype(f32)).astype(bf16)
    terms = jnp.concatenate([t_hi, t_mid, t_lo])
    spare = jnp.concatenate([terms, jnp.zeros((SPARE_ROWS * D - terms.shape[0],), bf16)]).reshape(SPARE_ROWS, D)
    slab = _pack_shard(w_in[0].astype(bf16), w_glu[0].astype(bf16), w_out[0].astype(bf16), w_up[0].astype(bf16),
                       w_down[0].astype(bf16), spare)
    my_half = lax.dynamic_slice_in_dim(slab, cc * HALF_ROWS, HALF_ROWS, axis=0)
    gathered = _allgather8(my_half, "gather_weights").reshape(4, PACK_ROWS, D)
    parts = [_unpack_shard(gathered[s]) for s in range(4)]
    full_in = jnp.concatenate([pt[0] for pt in parts], axis=1)
    full_glu = jnp.concatenate([pt[1] for pt in parts], axis=1)
    full_out = jnp.concatenate([pt[2] for pt in parts], axis=0)
    full_up = jnp.concatenate([pt[3] for pt in parts], axis=1)
    full_down = jnp.concatenate([pt[4] for pt in parts], axis=0)
    wb = _prep_weights(full_in, full_glu, full_out, full_up, full_down)
    n_sf = SH_CONVW + SH_META
    tr = gathered[:, R_SPARE:R_SPARE + SPARE_ROWS].reshape(4, SPARE_ROWS * D)[:, :3 * n_sf].astype(f32).reshape(4, 3, n_sf)
    sp = tr[:, 0] + tr[:, 1] + tr[:, 2]
    conv_w_full = jnp.concatenate([sp[s, :SH_CONVW].reshape(4, 384) for s in range(4)], axis=1)
    meta_full = jnp.concatenate([sp[s, SH_CONVW:].reshape(16, 256) for s in range(4)], axis=1)

    p = dict(g_mix=g_mix, conv_w=conv_w_full, conv_b=conv_b, dt_bias=dt_bias, a_log=a_log, d_ssd=d_ssd, g_ssd=g_ssd,
             lam_re=lam_re[0], lam_im=lam_im[0], log_step=log_step[0], b_re=b_re[0], b_im=b_im[0], c_re=c_re[0], c_im=c_im[0],
             d_s5=d_s5[0], b_glu=b_glu, g_s5=g_s5, g_mlp=g_mlp, g_final=g_final.reshape(1, D))
    loss_part, dh0, g = _local_step(x[0], loss_target[0], meta_full, p, wb)
    grad_x = dh0[CH:].reshape(x.shape)

    gin = _unperm_w_in_grad(g["w_in"])
    slabs = jnp.stack([_pack_shard(gin[:, 900 * s:900 * (s + 1)], g["w_glu"][:, 512 * s:512 * (s + 1)],
                                   g["w_out"][512 * s:512 * (s + 1)], g["w_up"][:, 1024 * s:1024 * (s + 1)],
                                   g["w_down"][1024 * s:1024 * (s + 1)], None).astype(bf16) for s in range(4)])
    halves = jnp.swapaxes(slabs.reshape(4, 2, HALF_ROWS, D), 0, 1)
    from_sib = _swap_sibling(halves, "rs_pair")
    mine = lax.dynamic_index_in_dim(halves, cc, axis=0, keepdims=False)
    pair = _add_rows("rs_pair_add", [mine.reshape(4 * HALF_ROWS, D), from_sib.reshape(4 * HALF_ROWS, D)], bf16,
                     4 * HALF_ROWS, 400).reshape(4, HALF_ROWS, D)
    from_chips = _exchange_chips(pair, "rs_chips")
    own = lax.dynamic_index_in_dim(pair, chip, axis=0, keepdims=False)
    red = _add_rows("rs_chip_add", [own, from_chips[0], from_chips[1], from_chips[2]], f32, HALF_ROWS, 400)
    shard_g = _share_sibling(red, "rs_share").reshape(PACK_ROWS, D)
    g_in, g_glu, g_out, g_up, g_down = _unpack_shard(shard_g)

    small_g = [g[n] for n, _ in SMALL] + [g["conv_w"], dh0[CH - N_META:CH], loss_part.reshape(1)]
    n_small = sum(math.prod(s) for _, s in SMALL) + 4 * D_XBC + N_META * D + 1
    rows_small = -(-n_small // (8 * D)) * 8
    total = _sum8(_allgather8(_pack_small(small_g, rows_small), "gather_small"), rows_small)
    outs = _unpack_small(total, [s for _, s in SMALL] + [(4, D_XBC), (N_META, D), ()])
    gs = {n: o for (n, _), o in zip(SMALL, outs)}
    g_conv_w = lax.dynamic_slice_in_dim(outs[-3], chip * 384, 384, axis=1).reshape(conv_w.shape)
    g_meta = lax.dynamic_slice_in_dim(outs[-2], chip * 256, 256, axis=1)
    loss = outs[-1]

    grads = dict(gs, meta_tokens=g_meta, conv_w=g_conv_w, w_in=g_in.reshape(w_in.shape), w_glu=g_glu.reshape(w_glu.shape),
                 w_out=g_out.reshape(w_out.shape), w_up=g_up.reshape(w_up.shape), w_down=g_down.reshape(w_down.shape))
    delta, new_m, new_v = {}, {}, {}
    for n, bm in (("w_in", 256), ("w_glu", 256), ("w_out", 256), ("w_up", 256), ("w_down", 256)):
        shp = given[n].shape
        two = lambda a: a.reshape(shp[1], shp[2])
        d_, m_, v_ = _adamw("adamw_" + n, two(given[n]), two(grads[n]), two(given["m_" + n]), two(given["v_" + n]), bm)
        delta[n], new_m[n], new_v[n] = d_.reshape(shp), m_.reshape(shp), v_.reshape(shp)
    names = [n for n, _ in SMALL] + ["conv_w", "meta_tokens"]
    shapes = [given[n].shape for n in names]
    n_adam = sum(math.prod(s) for s in shapes)
    rows_adam = -(-n_adam // (8 * D)) * 8
    pw = _pack_small([given[n] for n in names], rows_adam)
    pg = _pack_small([grads[n] for n in names], rows_adam)
    pm = _pack_small([given["m_" + n] for n in names], rows_adam)
    pv = _pack_small([given["v_" + n] for n in names], rows_adam)
    d_, m_, v_ = _adamw("adamw_small", pw, pg, pm, pv, rows_adam)
    for n, a, b, c_ in zip(names, _unpack_small(d_, shapes), _unpack_small(m_, shapes), _unpack_small(v_, shapes)):
        delta[n], new_m[n], new_v[n] = a, b, c_

    order = ["meta_tokens", "g_mix", "w_in", "conv_w", "conv_b", "dt_bias", "a_log", "d_ssd", "g_ssd", "lam_re", "lam_im", "log_step",
             "b_re", "b_im", "c_re", "c_im", "d_s5", "w_glu", "b_glu", "g_s5", "w_out", "g_mlp", "w_up", "w_down", "g_final"]
    grads_out = [grads[n].reshape(given[n].shape) for n in order]
    return (loss, grad_x, *grads_out, *[delta[n] for n in order], *[new_m[n] for n in order], *[new_v[n] for n in order])
```

```python
import math

import jax
import jax.numpy as jnp
from jax import lax
from jax.experimental import pallas as pl
from jax.experimental.pallas import tpu as pltpu

f32 = jnp.float32
bf16 = jnp.bfloat16

D = 1024
N_META = 16
CH = 256
HEADS = 16
HEAD_DIM = 64
NSTATE = 128
D_XBC = 1536
S5_P = 64
NQ = 8
PITCH = CH + 4
EPS = 1e-5
O_Z, O_U, O_XBC, O_DT, W_PROJ = 0, 1024, 2048, 3584, 3712
VMEM_LIMIT = 60 * 1024 * 1024

ADAM_LR, ADAM_B1, ADAM_B2, ADAM_EPS, ADAM_WD, ADAM_STEP = 0.001, 0.9, 0.999, 1e-08, 0.01, 10

NT = (((1,), (1,)), ((), ()))
TN = (((0,), (0,)), ((), ()))
_ANY = pl.BlockSpec(memory_space=pl.ANY)


def _cp(sem=None):
    return pltpu.CompilerParams(dimension_semantics=sem, vmem_limit_bytes=VMEM_LIMIT)


def _sigmoid(v):
    return 1.0 / (1.0 + jnp.exp(-v))


def _rsum8(v):
    r, c = v.shape
    return jnp.sum(v.reshape(r // 8, 8, c), axis=0)


def _rms(h, g):
    r = lax.rsqrt(jnp.mean(h * h, axis=-1, keepdims=True) + EPS)
    return h * r * g


def _rms_bwd(dy, h, g):
    r = lax.rsqrt(jnp.mean(h * h, axis=-1, keepdims=True) + EPS)
    n = h * r
    dn = dy * g
    dh = r * (dn - n * jnp.mean(dn * n, axis=-1, keepdims=True))
    return dh, _rsum8(dy * n)


def _dot(a, b, dims=None):
    if dims is None:
        return jnp.dot(a, b, preferred_element_type=f32)
    return lax.dot_general(a, b, dims, preferred_element_type=f32)


def _split_dot(v, m01, dims, terms, v_is_lhs=True):
    out, r = None, v
    for _ in range(terms):
        piece = r.astype(bf16)
        o = _dot(piece, m01, dims) if v_is_lhs else _dot(m01, piece, dims)
        out = o if out is None else out + o
        r = r - piece.astype(f32)
    return out


def _rowwise(name, fn, nblk, rows=(), shifted=(), pars=(), refs=(), out_rows=(), out_accs=(), out_first=(), out_shifted=(), bm=CH):
    n_r, n_s, n_p, n_w = len(rows), len(shifted), len(pars), len(refs)
    n_in = n_r + n_s + n_p + n_w
    n_o, n_a, n_f, n_so = len(out_rows), len(out_accs), len(out_first), len(out_shifted)

    def body(*all_refs):
        i = pl.program_id(0)
        ins = all_refs[:n_in]
        outs = all_refs[n_in:]
        rv = [r[...] for r in ins[:n_r]]
        sv = [r[...] for r in ins[n_r:n_r + n_s]]
        pv = [r[...] for r in ins[n_r + n_s:n_r + n_s + n_p]]
        ro, ao, fo, so = fn(i, rv, sv, pv, list(ins[n_r + n_s + n_p:]))
        for r, v in zip(outs[:n_o], ro):
            r[...] = v.astype(r.dtype)
        accs = outs[n_o:n_o + n_a]

        @pl.when(i == 0)
        def _():
            for r in accs:
                r[...] = jnp.zeros_like(r)
            for r, v in zip(outs[n_o + n_a:n_o + n_a + n_f], fo):
                r[...] = v.astype(r.dtype)
        for r, v in zip(accs, ao):
            r[...] += v
        for r, v in zip(outs[n_o + n_a + n_f:], so):
            r[...] = v.astype(r.dtype)

    prev = lambda i: (jnp.maximum(i - 1, 0), 0)
    in_specs = [pl.BlockSpec((bm, a.shape[1]), lambda i: (i, 0)) for a in rows]
    in_specs += [pl.BlockSpec((bm, a.shape[1]), prev) for a in shifted]
    in_specs += [pl.BlockSpec(a.shape, lambda i, nd=a.ndim: (0,) * nd) for a in pars]
    in_specs += [spec for _, spec in refs]
    out_specs = [pl.BlockSpec((bm, c), lambda i: (i, 0)) for c, _ in out_rows]
    out_specs += [pl.BlockSpec((8, c), lambda i: (0, 0)) for c in out_accs]
    out_specs += [pl.BlockSpec((bm, c), lambda i: (0, 0)) for c, _ in out_first]
    out_specs += [pl.BlockSpec((bm, c), prev) for c, _ in out_shifted]
    out_shape = [jax.ShapeDtypeStruct((nblk * bm, c), dt) for c, dt in out_rows]
    out_shape += [jax.ShapeDtypeStruct((8, c), f32) for c in out_accs]
    out_shape += [jax.ShapeDtypeStruct((bm, c), dt) for c, dt in out_first]
    out_shape += [jax.ShapeDtypeStruct(((nblk - 1) * bm, c), dt) for c, dt in out_shifted]
    res = pl.pallas_call(body, name=name, grid=(nblk,), in_specs=in_specs, out_specs=out_specs, out_shape=out_shape,
                         compiler_params=_cp(("arbitrary",)))(*rows, *shifted, *pars, *[a for a, _ in refs])
    return res[:n_o], res[n_o:n_o + n_a], res[n_o + n_a:n_o + n_a + n_f], res[n_o + n_a + n_f:]


PACK_ROWS = 4096
HALF_ROWS = PACK_ROWS // 2
R_UP, R_DOWN, R_GLU, R_OUT, R_IN, R_SPARE = 0, 1024, 2048, 2560, 3072, 3972


def _contract_rows(lp):
    return lp // 8 if lp % (8 * 16) == 0 else CH


def _dw_into(name, a, b, slab, ka, a_sharded, row_blk, n_s, s0):
    lp = a.shape[0]
    bm = _contract_rows(lp)
    steps = lp // bm

    def body(a_ref, b_ref, *rest):
        o_ref, acc = rest[-2], rest[-1]
        k = pl.program_id(1)

        @pl.when(k == 0)
        def _():
            acc[...] = jnp.zeros_like(acc)
        acc[...] += _dot(a_ref[...], b_ref[...], TN)

        @pl.when(k == steps - 1)
        def _():
            o_ref[0] = acc[...].astype(bf16)

    in_specs = [pl.BlockSpec((bm, ka), (lambda s, k: (k, s)) if a_sharded else (lambda s, k: (k, 0))),
                pl.BlockSpec((bm, D), (lambda s, k: (k, 0)) if a_sharded else (lambda s, k: (k, s)))]
    args = [a, b]
    aliases = {}
    if slab is not None:
        in_specs.append(_ANY)
        args.append(slab)
        aliases = {2: 0}
    return pl.pallas_call(
        body, name=name, grid=(n_s, steps), in_specs=in_specs,
        out_specs=pl.BlockSpec((1, ka, D), lambda s, k: (s0 + s, row_blk, 0)),
        out_shape=jax.ShapeDtypeStruct((4, PACK_ROWS, D), bf16),
        scratch_shapes=[pltpu.VMEM((ka, D), f32)], input_output_aliases=aliases,
        compiler_params=_cp(("arbitrary", "arbitrary")))(*args)


def _dw_in(n0, dproj):
    lp = n0.shape[0]
    bm = _contract_rows(lp) // 2 if _contract_rows(lp) % 32 == 0 else CH
    steps = lp // bm
    bk = 512

    def body(a_ref, b_ref, o_ref, acc):
        k = pl.program_id(1)

        @pl.when(k == 0)
        def _():
            acc[...] = jnp.zeros_like(acc)
        acc[...] += _dot(a_ref[...], b_ref[...], TN)

        @pl.when(k == steps - 1)
        def _():
            o_ref[...] = acc[...].astype(bf16)

    return pl.pallas_call(
        body, name="dw_in", grid=(D // bk, steps),
        in_specs=[pl.BlockSpec((bm, bk), lambda i, k: (k, i)), pl.BlockSpec((bm, W_PROJ), lambda i, k: (k, 0))],
        out_specs=pl.BlockSpec((bk, W_PROJ), lambda i, k: (i, 0)),
        out_shape=jax.ShapeDtypeStruct((D, W_PROJ), bf16),
        scratch_shapes=[pltpu.VMEM((bk, W_PROJ), f32)],
        compiler_params=_cp(("arbitrary", "arbitrary")))(n0, dproj)


def _head_expand():
    h = lax.broadcasted_iota(jnp.int32, (HEADS, D), 0)
    c = lax.broadcasted_iota(jnp.int32, (HEADS, D), 1)
    return jnp.where((c >> 6) == h, 1.0, 0.0).astype(bf16)


def _ssd_common(i, P, prev8, cw, cb, dtb, alog):
    z = P[:, O_Z:O_Z + D]
    xp = P[:, O_XBC:O_XBC + D_XBC]
    dt_raw = P[:, O_DT:O_DT + HEADS]
    row = lax.broadcasted_iota(jnp.int32, (CH, 1), 0)
    row8 = lax.broadcasted_iota(jnp.int32, (8, 1), 0)
    taps = [xp]
    for k in (1, 2, 3):
        rolled = pltpu.roll(xp, k, 0)
        fix = pltpu.roll(prev8, k, 0)
        top = jnp.where(row8 < k, fix, rolled[0:8])
        taps.append(jnp.concatenate([top, rolled[8:]], axis=0))
    xc = cb + cw[3:4] * taps[0] + cw[2:3] * taps[1] + cw[1:2] * taps[2] + cw[0:1] * taps[3]
    sg = _sigmoid(xc)
    xbc = xc * sg
    live = jnp.where(jnp.logical_or(i > 0, row >= CH - N_META), 1.0, 0.0)
    pre = dt_raw + dtb
    dt = jnp.where(pre > 20.0, pre, jnp.log(1.0 + jnp.exp(jnp.minimum(pre, 20.0)))) * live
    a = -jnp.exp(alog)
    dta = dt * a
    r_i = lax.broadcasted_iota(jnp.int32, (CH, CH), 0)
    c_i = lax.broadcasted_iota(jnp.int32, (CH, CH), 1)
    tril = r_i >= c_i
    acs = _split_dot(dta, jnp.where(tril, 1.0, 0.0).astype(bf16), None, 3, v_is_lhs=False)
    acs_t = _split_dot(dta, jnp.where(r_i <= c_i, 1.0, 0.0).astype(bf16), TN, 3)
    e = _head_expand()
    acs_e = _split_dot(acs, e, None, 3)
    dt_e = _split_dot(dt, e, None, 3)
    return dict(z=z, xp=xp, taps=taps, xc=xc, sg=sg, xbc=xbc, live=live, pre=pre, dt=dt, a=a, tril=tril,
                acs=acs, acs_t=acs_t, e=e, acs_e=acs_e, dt_e=dt_e)


def _lmat(c, h):
    seg = c["acs"][:, h:h + 1] - c["acs_t"][h:h + 1, :]
    return jnp.where(c["tril"], jnp.exp(jnp.minimum(seg, 0.0)), 0.0)


def _pair_masks():
    lane = lax.broadcasted_iota(jnp.int32, (1, 128), 1)
    return jnp.where(lane < HEAD_DIM, 1.0, 0.0), jnp.where(lane >= HEAD_DIM, 1.0, 0.0)


def _ssd_fwd(proj, conv_w, conv_b, dt_bias, a_log, d_ssd, g_ssd, nch):
    def body(p_ref, cw_ref, cb_ref, dtb_ref, al_ref, d_ref, g_ref, y_ref, ys_ref, st_ref, prev8_ref, state_ref):
        i = pl.program_id(0)

        @pl.when(i == 0)
        def _():
            prev8_ref[...] = jnp.zeros_like(prev8_ref)
            state_ref[...] = jnp.zeros_like(state_ref)

        P = p_ref[...]
        c = _ssd_common(i, P, prev8_ref[...], cw_ref[...], cb_ref[...], dtb_ref[...], al_ref[...])
        prev8_ref[...] = c["xp"][CH - 8:CH]
        xbc = c["xbc"]
        x = xbc[:, 0:D]
        xdt = x * c["dt_e"]
        a_last_e = c["acs_e"][CH - 1:CH, :]
        w_end = (xdt * jnp.exp(a_last_e - c["acs_e"])).astype(bf16)
        m0, m1 = _pair_masks()
        ys = []
        for g in range(2):
            bg = xbc[:, D + NSTATE * g:D + NSTATE * (g + 1)].astype(bf16)
            cg = xbc[:, D + 2 * NSTATE + NSTATE * g:D + 2 * NSTATE + NSTATE * (g + 1)].astype(bf16)
            gmat = _dot(cg, bg, NT)
            st = state_ref[g]
            st_ref[0, g] = st
            sl = slice(512 * g, 512 * (g + 1))
            y_off = _dot(cg, st.astype(bf16)) * jnp.exp(c["acs_e"][:, sl])
            contrib = _dot(bg, w_end[:, sl], TN)
            state_ref[g] = st * jnp.exp(a_last_e[:, sl]) + contrib
            yd = []
            for pr in range(4):
                h0 = 8 * g + 2 * pr
                xp2 = xdt[:, 128 * (4 * g + pr):128 * (4 * g + pr + 1)]
                ma = (gmat * _lmat(c, h0)).astype(bf16)
                mb = (gmat * _lmat(c, h0 + 1)).astype(bf16)
                yd.append(_dot(ma, (xp2 * m0).astype(bf16)) + _dot(mb, (xp2 * m1).astype(bf16)))
            ys.append(jnp.concatenate(yd, axis=1) + y_off)
        d_e = _split_dot(d_ref[...], c["e"], None, 3)
        y = jnp.concatenate(ys, axis=1) + x * d_e
        y_ref[...] = y
        yg = y * (c["z"] * _sigmoid(c["z"]))
        ys_ref[...] = _rms(yg, g_ref[...]).astype(bf16)

    full = lambda a: pl.BlockSpec(a.shape, lambda i, nd=a.ndim: (0,) * nd)
    return pl.pallas_call(
        body, name="ssd_fwd", grid=(nch,),
        in_specs=[pl.BlockSpec((CH, W_PROJ), lambda i: (i, 0))] + [full(a) for a in (conv_w, conv_b, dt_bias, a_log, d_ssd, g_ssd)],
        out_specs=[pl.BlockSpec((CH, D), lambda i: (i, 0)), pl.BlockSpec((CH, D), lambda i: (i, 0)),
                   pl.BlockSpec((1, 2, NSTATE, 512), lambda i: (i, 0, 0, 0))],
        out_shape=[jax.ShapeDtypeStruct((nch * CH, D), f32), jax.ShapeDtypeStruct((nch * CH, D), bf16),
                   jax.ShapeDtypeStruct((nch, 2, NSTATE, 512), f32)],
        scratch_shapes=[pltpu.VMEM((8, D_XBC), f32), pltpu.VMEM((2, NSTATE, 512), f32)],
        compiler_params=_cp(("arbitrary",)))(proj, conv_w, conv_b, dt_bias, a_log, d_ssd, g_ssd)


def _ssd_bwd(proj, y, dys, du, states, conv_w, conv_b, dt_bias, a_log, d_ssd, g_ssd, nch):
    def body(p_ref, ph_ref, y_ref, dys_ref, du_ref, st_ref, cw_ref, cb_ref, dtb_ref, al_ref, d_ref, g_ref,
             dp_ref, dcw_ref, dcb_ref, ddtb_ref, dal_ref, dd_ref, dg_ref, nxt8_ref, dst_ref):
        step = pl.program_id(0)
        i = nch - 1 - step

        @pl.when(step == 0)
        def _():
            nxt8_ref[...] = jnp.zeros_like(nxt8_ref)
            dst_ref[...] = jnp.zeros_like(dst_ref)
            for r in (dcw_ref, dcb_ref, ddtb_ref, dal_ref, dd_ref, dg_ref):
                r[...] = jnp.zeros_like(r)

        P = p_ref[...]
        prev8 = jnp.where(i > 0, ph_ref[...][:, O_XBC:O_XBC + D_XBC], 0.0)
        c = _ssd_common(i, P, prev8, cw_ref[...], cb_ref[...], dtb_ref[...], al_ref[...])
        xbc, z, e = c["xbc"], c["z"], c["e"]
        x = xbc[:, 0:D]
        yv = y_ref[...]
        sz = _sigmoid(z)
        silu_z = z * sz
        dyg, dg8 = _rms_bwd(dys_ref[...], yv * silu_z, g_ref[...])
        dg_ref[...] += dg8
        dy = dyg * silu_z
        dz = dyg * yv * (sz * (1.0 + z * (1.0 - sz)))
        d_e = _split_dot(d_ref[...], e, None, 3)
        dd_ref[...] += _rsum8(dy * x)
        xdt = x * c["dt_e"]
        a_last_e = c["acs_e"][CH - 1:CH, :]
        e_end = jnp.exp(a_last_e - c["acs_e"])
        w_end = xdt * e_end
        e_acs = jnp.exp(c["acs_e"])
        dy_dec = dy * e_acs
        m0, m1 = _pair_masks()
        lane16 = lax.broadcasted_iota(jnp.int32, (1, HEADS), 1)
        row16 = lax.broadcasted_iota(jnp.int32, (HEADS, 1), 0)
        dacs = jnp.zeros((CH, HEADS), f32)
        dacs_t = jnp.zeros((HEADS, CH), f32)
        dxdt_parts, dbs, dcs, zparts, yoff_parts, dlast_parts = [], [], [], [], [], []
        for g in range(2):
            sl = slice(512 * g, 512 * (g + 1))
            bg = xbc[:, D + NSTATE * g:D + NSTATE * (g + 1)].astype(bf16)
            cg = xbc[:, D + 2 * NSTATE + NSTATE * g:D + 2 * NSTATE + NSTATE * (g + 1)].astype(bf16)
            gmat = _dot(cg, bg, NT)
            st = st_ref[0, g]
            dstn = dst_ref[g]
            dstn_b = dstn.astype(bf16)
            y_off = _dot(cg, st.astype(bf16)) * e_acs[:, sl]
            yoff_parts.append(y_off)
            bds = _dot(bg, dstn_b)
            zparts.append(w_end[:, sl] * bds)
            dlast_parts.append(jnp.sum(dstn * st, axis=0, keepdims=True) * jnp.exp(a_last_e[:, sl]))
            dg_acc = jnp.zeros((CH, CH), f32)
            dxd = []
            for pr in range(4):
                lo = 128 * (4 * g + pr)
                xp2 = xdt[:, lo:lo + 128].astype(bf16)
                dy2 = dy[:, lo:lo + 128]
                outp = jnp.zeros((CH, 128), f32)
                for hh, msk in ((0, m0), (1, m1)):
                    h = 8 * g + 2 * pr + hh
                    lm = _lmat(c, h)
                    dyh = (dy2 * msk).astype(bf16)
                    mh = (gmat * lm).astype(bf16)
                    outp = outp + _dot(mh, dyh, TN)
                    dml = _dot(dyh, xp2, NT) * lm
                    dg_acc = dg_acc + dml
                    q = dml * gmat
                    dacs = dacs + jnp.where(lane16 == h, jnp.sum(q, axis=1, keepdims=True), 0.0)
                    dacs_t = dacs_t + jnp.where(row16 == h, jnp.sum(q, axis=0, keepdims=True), 0.0)
                dxd.append(outp)
            dxdt_parts.append(jnp.concatenate(dxd, axis=1) + e_end[:, sl] * bds)
            dgb = dg_acc.astype(bf16)
            dcs.append(_dot(dgb, bg) + _dot(dy_dec[:, sl].astype(bf16), st.astype(bf16), NT))
            dbs.append(_dot(dgb, cg, TN) + _dot(w_end[:, sl].astype(bf16), dstn_b, NT))
            dst_ref[g] = dstn * jnp.exp(a_last_e[:, sl]) + _dot(cg, dy_dec[:, sl].astype(bf16), TN)
        dxdt = jnp.concatenate(dxdt_parts, axis=1)
        zfull = jnp.concatenate(zparts, axis=1)
        y_off_full = jnp.concatenate(yoff_parts, axis=1)
        dlast = jnp.concatenate(dlast_parts, axis=1)
        red = lambda v: _split_dot(v, e, NT, 2)
        eye16 = jnp.where(lax.broadcasted_iota(jnp.int32, (HEADS, HEADS), 0) == lax.broadcasted_iota(jnp.int32, (HEADS, HEADS), 1),
                          1.0, 0.0).astype(bf16)
        dacs = dacs - _split_dot(dacs_t, eye16, TN, 3)
        zred = red(zfull)
        dacs = dacs + red(dy * y_off_full) - zred
        last_term = jnp.sum(zred, axis=0, keepdims=True) + red(dlast)
        rowc = lax.broadcasted_iota(jnp.int32, (CH, 1), 0)
        dacs = dacs + jnp.where(rowc == CH - 1, last_term, 0.0)
        r_i = lax.broadcasted_iota(jnp.int32, (CH, CH), 0)
        c_i = lax.broadcasted_iota(jnp.int32, (CH, CH), 1)
        ddta = _split_dot(dacs, jnp.where(c_i >= r_i, 1.0, 0.0).astype(bf16), None, 3, v_is_lhs=False)
        ddt = ddta * c["a"] + red(dxdt * x)
        dal_ref[...] += _rsum8(ddta * c["dt"] * c["a"])
        ddt_raw = ddt * _sigmoid(c["pre"]) * c["live"]
        ddtb_ref[...] += _rsum8(ddt_raw)
        dx = dy * d_e + dxdt * c["dt_e"]
        dxbc = jnp.concatenate([dx, dbs[0], dbs[1], dcs[0], dcs[1]], axis=1)
        sg = c["sg"]
        dxc = dxbc * (sg * (1.0 + c["xc"] * (1.0 - sg)))
        dcb_ref[...] += _rsum8(dxc)
        taps = c["taps"]
        row8 = lax.broadcasted_iota(jnp.int32, (8, 1), 0)
        for k in range(4):
            dcw_ref[...] += jnp.where(row8 == k, jnp.sum(dxc * taps[3 - k], axis=0, keepdims=True), 0.0)
        cw = cw_ref[...]
        dxp = cw[3:4] * dxc
        nxt8 = nxt8_ref[...]
        for j in (1, 2, 3):
            rolled = pltpu.roll(dxc, CH - j, 0)
            fix = pltpu.roll(nxt8, 8 - j, 0)
            bot = jnp.where(row8 >= 8 - j, fix, rolled[CH - 8:CH])
            dxp = dxp + cw[3 - j:4 - j] * jnp.concatenate([rolled[:CH - 8], bot], axis=0)
        nxt8_ref[...] = dxc[0:8]
        dp_ref[:, O_Z:O_Z + D] = dz.astype(bf16)
        dp_ref[:, O_U:O_U + D] = du_ref[...].astype(bf16)
        dp_ref[:, O_XBC:O_XBC + D_XBC] = dxp.astype(bf16)
        dp_ref[:, O_DT:W_PROJ] = jnp.zeros((CH, W_PROJ - O_DT), bf16)
        dp_ref[:, O_DT:O_DT + HEADS] = ddt_raw.astype(bf16)

    full = lambda a: pl.BlockSpec(a.shape, lambda s, nd=a.ndim: (0,) * nd)
    rev = lambda s: (nch - 1 - s, 0)
    acc = lambda cdim: pl.BlockSpec((8, cdim), lambda s: (0, 0))
    return pl.pallas_call(
        body, name="ssd_bwd", grid=(nch,),
        in_specs=[pl.BlockSpec((CH, W_PROJ), rev),
                  pl.BlockSpec((8, W_PROJ), lambda s: (jnp.maximum((nch - 1 - s) * (CH // 8) - 1, 0), 0)),
                  pl.BlockSpec((CH, D), rev), pl.BlockSpec((CH, D), rev), pl.BlockSpec((CH, D), rev),
                  pl.BlockSpec((1, 2, NSTATE, 512), lambda s: (nch - 1 - s, 0, 0, 0))]
        + [full(a) for a in (conv_w, conv_b, dt_bias, a_log, d_ssd, g_ssd)],
        out_specs=[pl.BlockSpec((CH, W_PROJ), rev), acc(D_XBC), acc(D_XBC), acc(HEADS), acc(HEADS), acc(D), acc(D)],
        out_shape=[jax.ShapeDtypeStruct((nch * CH, W_PROJ), bf16)]
        + [jax.ShapeDtypeStruct((8, cdim), f32) for cdim in (D_XBC, D_XBC, HEADS, HEADS, D, D)],
        scratch_shapes=[pltpu.VMEM((8, D_XBC), f32), pltpu.VMEM((2, NSTATE, 512), f32)],
        compiler_params=_cp(("arbitrary",)))(proj, proj, y, dys, du, states, conv_w, conv_b, dt_bias, a_log, d_ssd, g_ssd)


SCAN_UNROLL = 8


def _to_slabs(slab_ref, q, mat):
    for ls in range(8):
        slab_ref[ls, pl.ds(PITCH * q, CH), :] = mat[:, 128 * ls:128 * (ls + 1)]


def _from_slabs(slab, q):
    return jnp.concatenate([slab(ls, PITCH * q) for ls in range(8)], axis=1)


def _tile(slab_ref, ls, t, lead=None):
    idx = (ls, pl.ds(t, 8, stride=PITCH), slice(None))
    return slab_ref[idx] if lead is None else slab_ref[(lead,) + idx]


def _s5_fwd(proj, bbq, ccq_t, ar, ai, d_skip, nch):
    def body(u_ref, bb_ref, cc_ref, ar_ref, ai_ref, d_ref, s_ref, yl_ref, y5_ref, bu_ref, st_ref):
        @pl.when(pl.program_id(0) == 0)
        def _():
            st_ref[...] = jnp.zeros_like(st_ref)
        u = u_ref[...]
        ub = u.astype(bf16)
        for q in range(NQ):
            _to_slabs(bu_ref, q, _dot(ub[:, 128 * q:128 * (q + 1)], bb_ref[q]))
        ar_t = [ar_ref[:, 128 * l:128 * (l + 1)] for l in range(4)]
        ai_t = [ai_ref[:, 128 * l:128 * (l + 1)] for l in range(4)]

        def one(t, carry):
            re, im = carry
            nre, nim = [], []
            for l in range(4):
                a = ar_t[l] * re[l] - ai_t[l] * im[l] + _tile(bu_ref, l, t)
                b = ar_t[l] * im[l] + ai_t[l] * re[l] + _tile(bu_ref, l + 4, t)
                s_ref[0, l, pl.ds(t, 8, stride=PITCH), :] = a
                s_ref[0, l + 4, pl.ds(t, 8, stride=PITCH), :] = b
                nre.append(a)
                nim.append(b)
            return tuple(nre), tuple(nim)

        def step(tt, carry):
            for k in range(SCAN_UNROLL):
                carry = one(tt * SCAN_UNROLL + k, carry)
            return carry
        init = (tuple(st_ref[l] for l in range(4)), tuple(st_ref[l + 4] for l in range(4)))
        re, im = lax.fori_loop(0, CH // SCAN_UNROLL, step, init)
        for l in range(4):
            st_ref[l] = re[l]
            st_ref[l + 4] = im[l]
        ys = []
        for q in range(NQ):
            sq = _from_slabs(lambda ls, r0: s_ref[0, ls, pl.ds(r0, CH), :], q).astype(bf16)
            ys.append(_dot(sq, cc_ref[q]))
        yl = jnp.concatenate(ys, axis=1) + u * d_ref[...]
        yl_ref[...] = yl
        y5_ref[...] = (0.5 * yl * (1.0 + lax.erf(yl * (1.0 / math.sqrt(2.0))))).astype(bf16)

    const = lambda a: pl.BlockSpec(a.shape, lambda i, nd=a.ndim: (0,) * nd)
    return pl.pallas_call(
        body, name="s5_fwd", grid=(nch,),
        in_specs=[pl.BlockSpec((CH, D), lambda i: (i, O_U // D)), const(bbq), const(ccq_t), const(ar), const(ai), const(d_skip)],
        out_specs=[pl.BlockSpec((1, 8, 8 * PITCH, 128), lambda i: (i, 0, 0, 0)),
                   pl.BlockSpec((CH, D), lambda i: (i, 0)), pl.BlockSpec((CH, D), lambda i: (i, 0))],
        out_shape=[jax.ShapeDtypeStruct((nch, 8, 8 * PITCH, 128), f32), jax.ShapeDtypeStruct((nch * CH, D), f32),
                   jax.ShapeDtypeStruct((nch * CH, D), bf16)],
        scratch_shapes=[pltpu.VMEM((8, 8 * PITCH, 128), f32), pltpu.VMEM((8, 8, 128), f32)],
        compiler_params=_cp(("arbitrary",)))(proj, bbq, ccq_t, ar, ai, d_skip)


def _s5_bwd(proj, dyl, s_all, bbtq, cctq, ar, ai, d_skip, nch):
    def body(u_ref, dy_ref, s_ref, bbt_ref, cct_ref, ar_ref, ai_ref, d_ref,
             du_ref, dcc_ref, dbb_ref, dab_ref, dd_ref, ga_ref, st_ref):
        @pl.when(pl.program_id(0) == 0)
        def _():
            st_ref[...] = jnp.zeros_like(st_ref)
            for r in (dcc_ref, dbb_ref, dab_ref, dd_ref):
                r[...] = jnp.zeros_like(r)
        u = u_ref[...]
        dyl_v = dy_ref[...]
        dd_ref[...] += _rsum8(dyl_v * u)
        ub = u.astype(bf16)
        dyb = dyl_v.astype(bf16)
        for q in range(NQ):
            _to_slabs(ga_ref, q, _dot(dyb[:, 128 * q:128 * (q + 1)], cct_ref[q]))
        ar_t = [ar_ref[:, 128 * l:128 * (l + 1)] for l in range(4)]
        ai_t = [ai_ref[:, 128 * l:128 * (l + 1)] for l in range(4)]

        def one(t, carry):
            re, im, dar, dai = carry
            nre, nim, ndar, ndai = [], [], [], []
            for l in range(4):
                sre = _tile(s_ref, l, t, lead=0)
                sim = _tile(s_ref, l + 4, t, lead=0)
                ndar.append(dar[l] + re[l] * sre + im[l] * sim)
                ndai.append(dai[l] + im[l] * sre - re[l] * sim)
                a = _tile(ga_ref, l, t) + ar_t[l] * re[l] + ai_t[l] * im[l]
                b = _tile(ga_ref, l + 4, t) - ai_t[l] * re[l] + ar_t[l] * im[l]
                ga_ref[l, pl.ds(t, 8, stride=PITCH), :] = a
                ga_ref[l + 4, pl.ds(t, 8, stride=PITCH), :] = b
                nre.append(a)
                nim.append(b)
            return tuple(nre), tuple(nim), tuple(ndar), tuple(ndai)

        def step(tt, carry):
            for k in range(SCAN_UNROLL):
                carry = one(CH - 1 - (tt * SCAN_UNROLL + k), carry)
            return carry
        four = lambda ref, o: tuple(ref[l + o] for l in range(4))
        re, im, dar, dai = lax.fori_loop(0, CH // SCAN_UNROLL, step,
                                         (four(st_ref, 0), four(st_ref, 4), four(dab_ref, 0), four(dab_ref, 4)))
        for l in range(4):
            st_ref[l], st_ref[l + 4] = re[l], im[l]
            dab_ref[l], dab_ref[l + 4] = dar[l], dai[l]
        dus = []
        for q in range(NQ):
            aq = _from_slabs(lambda ls, r0: ga_ref[ls, pl.ds(r0, CH), :], q).astype(bf16)
            sq = _from_slabs(lambda ls, r0: s_ref[0, ls, pl.ds(r0, CH), :], q).astype(bf16)
            dcc_ref[q] += _dot(dyb[:, 128 * q:128 * (q + 1)], sq, TN)
            dbb_ref[q] += _dot(ub[:, 128 * q:128 * (q + 1)], aq, TN)
            dus.append(_dot(aq, bbt_ref[q]))
        du_ref[...] = jnp.concatenate(dus, axis=1) + dyl_v * d_ref[...]

    const = lambda a: pl.BlockSpec(a.shape, lambda s, nd=a.ndim: (0,) * nd)
    rev = lambda s: (nch - 1 - s, 0)
    return pl.pallas_call(
        body, name="s5_bwd", grid=(nch,),
        in_specs=[pl.BlockSpec((CH, D), lambda s: (nch - 1 - s, O_U // D)), pl.BlockSpec((CH, D), rev),
                  pl.BlockSpec((1, 8, 8 * PITCH, 128), lambda s: (nch - 1 - s, 0, 0, 0)),
                  const(bbtq), const(cctq), const(ar), const(ai), const(d_skip)],
        out_specs=[pl.BlockSpec((CH, D), rev), pl.BlockSpec((NQ, 128, D), lambda s: (0, 0, 0)),
                   pl.BlockSpec((NQ, 128, D), lambda s: (0, 0, 0)), pl.BlockSpec((8, 8, 128), lambda s: (0, 0, 0)),
                   pl.BlockSpec((8, D), lambda s: (0, 0))],
        out_shape=[jax.ShapeDtypeStruct((nch * CH, D), f32), jax.ShapeDtypeStruct((NQ, 128, D), f32),
                   jax.ShapeDtypeStruct((NQ, 128, D), f32), jax.ShapeDtypeStruct((8, 8, 128), f32),
                   jax.ShapeDtypeStruct((8, D), f32)],
        scratch_shapes=[pltpu.VMEM((8, 8 * PITCH, 128), f32), pltpu.VMEM((8, 8, 128), f32)],
        compiler_params=_cp(("arbitrary",)))(proj, dyl, s_all, bbtq, cctq, ar, ai, d_skip)


def _s5_tables(lam_re, lam_im, log_step, b_re, b_im):
    step = jnp.exp(log_step)[:, None]
    mag = jnp.exp(lam_re * step)
    ab_re = mag * jnp.cos(lam_im * step)
    ab_im = mag * jnp.sin(lam_im * step)
    den = lam_re * lam_re + lam_im * lam_im
    coef_re = ((ab_re - 1.0) * lam_re + ab_im * lam_im) / den
    coef_im = (ab_im * lam_re - (ab_re - 1.0) * lam_im) / den
    bb_re = coef_re[..., None] * b_re - coef_im[..., None] * b_im
    bb_im = coef_re[..., None] * b_im + coef_im[..., None] * b_re
    return ab_re, ab_im, bb_re, bb_im


def _blockdiag_in(m_re, m_im):
    eye = jnp.eye(8, dtype=f32)

    def one(m):
        m = m.reshape(NQ, 8, S5_P, 16)
        return jnp.einsum("qgph,gk->qghkp", m, eye).reshape(NQ, 128, 512)
    return jnp.concatenate([one(m_re), one(m_im)], axis=2)


def _blockdiag_in_grad(dm):
    def one(x):
        x = x.reshape(NQ, 8, 16, 8, S5_P)
        return jnp.einsum("qghgp->qgph", x).reshape(NQ * 8, S5_P, 16)
    return one(dm[:, :, :512]), one(dm[:, :, 512:])


def _local_step(x2, tgt2, meta, p, w_in_t, slabs):
    seq = x2.shape[0]
    nch = 1 + seq // CH
    metablk = jnp.concatenate([jnp.zeros((CH - N_META, D), f32), meta], axis=0)
    w_full = (w_in_t, pl.BlockSpec(w_in_t.shape, lambda i: (0, 0)))
    w_up = (slabs, pl.BlockSpec((4, 1024, D), lambda i: (0, R_UP // 1024, 0)))
    w_down = (slabs, pl.BlockSpec((4, 1024, D), lambda i: (0, R_DOWN // 1024, 0)))
    w_glu_t = (slabs, pl.BlockSpec((4, 512, D), lambda i: (0, R_GLU // 512, 0)))
    w_out = (slabs, pl.BlockSpec((4, 512, D), lambda i: (0, R_OUT // 512, 0)))
    h0_of = lambda i, s, q: jnp.where(i == 0, q[0], s[0])

    def in_fn(i, r, s, q, w):
        nb = _rms(h0_of(i, s, q), q[1]).astype(bf16)
        return [_dot(nb, w[0][...], NT), nb], [], [], []
    (proj, n0), _, _, _ = _rowwise("in_proj", in_fn, nch, shifted=[x2], pars=[metablk, p["g_mix"]], refs=[w_full],
                                   out_rows=[(W_PROJ, f32), (D, bf16)])
    y, y_ssd, states = _ssd_fwd(proj, p["conv_w"], p["conv_b"], p["dt_bias"], p["a_log"], p["d_ssd"], p["g_ssd"], nch)

    ab_re, ab_im, bb_re, bb_im = _s5_tables(p["lam_re"], p["lam_im"], p["log_step"], p["b_re"], p["b_im"])
    ar, ai = ab_re.reshape(NQ, 512), ab_im.reshape(NQ, 512)
    bbq = _blockdiag_in(bb_re, bb_im)
    ccq = _blockdiag_in(jnp.swapaxes(p["c_re"], 1, 2), -jnp.swapaxes(p["c_im"], 1, 2))
    d_skip = p["d_s5"].reshape(1, D)
    s_all, ylin, y5 = _s5_fwd(proj, bbq.astype(bf16), jnp.swapaxes(ccq, 1, 2).astype(bf16), ar, ai, d_skip, nch)

    def glu_fn(i, r, s, q, w):
        v = jnp.concatenate([_dot(r[0], w[0][k], NT) for k in range(4)], axis=1) + q[0]
        return [v, _rms(v[:, :D] * _sigmoid(v[:, D:]), q[1])], [], [], []
    (v, y_s5), _, _, _ = _rowwise("glu", glu_fn, nch, rows=[y5], pars=[p["b_glu"], p["g_s5"]], refs=[w_glu_t],
                                  out_rows=[(2 * D, f32), (D, bf16)])

    def out_fn(i, r, s, q, w):
        acc = (_dot(r[0][:, :512], w[0][0]) + _dot(r[0][:, 512:], w[0][1]) + _dot(r[1][:, :512], w[0][2])
               + _dot(r[1][:, 512:], w[0][3]))
        return [h0_of(i, s, q) + acc], [], [], []
    (h1,), _, _, _ = _rowwise("out_proj", out_fn, nch, rows=[y_ssd, y_s5], shifted=[x2], pars=[metablk], refs=[w_out],
                              out_rows=[(D, f32)])

    def up_fn(i, r, s, q, w):
        nb = _rms(r[0], q[0]).astype(bf16)
        return [jnp.concatenate([_dot(nb, w[0][k]) for k in range(4)], axis=1), nb], [], [], []
    (m, n1), _, _, _ = _rowwise("up_proj", up_fn, nch, rows=[h1], pars=[p["g_mlp"]], refs=[w_up],
                                out_rows=[(4 * D, f32), (D, bf16)])

    def down_fn(i, r, s, q, w):
        act = jnp.square(jnp.maximum(r[0], 0.0)).astype(bf16)
        acc = _dot(act[:, :D], w[0][0])
        for k in range(1, 4):
            acc = acc + _dot(act[:, D * k:D * (k + 1)], w[0][k])
        return [r[1] + acc, act], [], [], []
    (h2, act), _, _, _ = _rowwise("down_proj", down_fn, nch, rows=[m, h1], refs=[w_down], out_rows=[(D, f32), (4 * D, bf16)])

    def final_fn(i, r, s, q, w):
        live = jnp.where(i > 0, 1.0, 0.0)
        err = (_rms(r[0], q[0]) - s[0]) * live
        dh, dg8 = _rms_bwd(err * (1.0 / D), r[0], q[0])
        return [dh, dh], [_rsum8(err * err), dg8], [], []
    (dh2, dh2_b), (loss8, dgf8), _, _ = _rowwise("final", final_fn, nch, rows=[h2], shifted=[tgt2], pars=[p["g_final"]],
                                                  out_rows=[(D, f32), (D, bf16)], out_accs=[D, D])
    loss = 0.5 / D * jnp.sum(loss8)

    def down_bwd_fn(i, r, s, q, w):
        da = jnp.concatenate([_dot(r[0], w[0][k], NT) for k in range(4)], axis=1)
        return [da * 2.0 * jnp.maximum(r[1], 0.0)], [], [], []
    (dm,), _, _, _ = _rowwise("down_bwd", down_bwd_fn, nch, rows=[dh2_b, m], refs=[w_down], out_rows=[(4 * D, bf16)])
    gslab = _dw_into("dw_down", act, dh2_b, None, 1024, True, R_DOWN // 1024, 4, 0)

    def up_bwd_fn(i, r, s, q, w):
        acc = _dot(r[0][:, :D], w[0][0], NT)
        for k in range(1, 4):
            acc = acc + _dot(r[0][:, D * k:D * (k + 1)], w[0][k], NT)
        dh, dg8 = _rms_bwd(acc, r[1], q[0])
        dh1_ = r[2] + dh
        return [dh1_, dh1_], [dg8], [], []
    (dh1, dh1_b), (dgmlp8,), _, _ = _rowwise("up_bwd", up_bwd_fn, nch, rows=[dm, h1, dh2], pars=[p["g_mlp"]], refs=[w_up],
                                             out_rows=[(D, f32), (D, bf16)], out_accs=[D])
    gslab = _dw_into("dw_up", n1, dm, gslab, 1024, False, R_UP // 1024, 4, 0)

    def out_bwd_fn(i, r, s, q, w):
        dmix = [_dot(r[0], w[0][k], NT) for k in range(4)]
        v1, v2 = r[1][:, :D], r[1][:, D:]
        s2 = _sigmoid(v2)
        dglu, dg8 = _rms_bwd(jnp.concatenate(dmix[2:], axis=1), v1 * s2, q[0])
        dv = jnp.concatenate([dglu * s2, dglu * v1 * s2 * (1.0 - s2)], axis=1)
        return [jnp.concatenate(dmix[:2], axis=1), dv], [dg8, _rsum8(dv)], [], []
    (dys, dv), (dgs58, dbglu8), _, _ = _rowwise("out_bwd", out_bwd_fn, nch, rows=[dh1_b, v], pars=[p["g_s5"]], refs=[w_out],
                                                out_rows=[(D, f32), (2 * D, bf16)], out_accs=[D, 2 * D])
    gslab = _dw_into("dw_out_a", y_ssd, dh1_b, gslab, 512, True, R_OUT // 512, 2, 0)
    gslab = _dw_into("dw_out_b", y_s5, dh1_b, gslab, 512, True, R_OUT // 512, 2, 2)

    def glu_bwd_fn(i, r, s, q, w):
        acc = _dot(r[0][:, :512], w[0][0])
        for k in range(1, 4):
            acc = acc + _dot(r[0][:, 512 * k:512 * (k + 1)], w[0][k])
        yl = r[1]
        cdf = 0.5 * (1.0 + lax.erf(yl * (1.0 / math.sqrt(2.0))))
        pdf = jnp.exp(-0.5 * yl * yl) * (1.0 / math.sqrt(2.0 * math.pi))
        return [acc * (cdf + yl * pdf)], [], [], []
    (dylin,), _, _, _ = _rowwise("glu_bwd", glu_bwd_fn, nch, rows=[dv, ylin], refs=[w_glu_t], out_rows=[(D, f32)])
    gslab = _dw_into("dw_glu", dv, y5, gslab, 512, True, R_GLU // 512, 4, 0)

    du, dcc, dbb, dab, dds5 = _s5_bwd(proj, dylin, s_all, jnp.swapaxes(bbq, 1, 2).astype(bf16), ccq.astype(bf16), ar, ai, d_skip, nch)
    dproj, dcw8, dcb8, ddtb8, dal8, dd8, dgssd8 = _ssd_bwd(
        proj, y, dys, du, states, p["conv_w"], p["conv_b"], p["dt_bias"], p["a_log"], p["d_ssd"], p["g_ssd"], nch)

    def in_bwd_fn(i, r, s, q, w):
        dh, dg8 = _rms_bwd(_dot(r[0], w[0][...]), h0_of(i, s, q), q[1])
        dh0 = r[1] + dh
        return [], [dg8], [dh0], [dh0]
    _, (dgmix8,), (dmeta_blk,), (grad_x,) = _rowwise("in_bwd", in_bwd_fn, nch, rows=[dproj, dh1], shifted=[x2],
                                                      pars=[metablk, p["g_mix"]], refs=[w_full], out_accs=[D],
                                                      out_first=[(D, f32)], out_shifted=[(D, f32)])
    dw_in = _dw_in(n0, dproj)

    dab_q = jnp.swapaxes(dab.reshape(2, 4, NQ, 128), 1, 2).reshape(2, NQ * 8, S5_P)
    dbb_re, dbb_im = _blockdiag_in_grad(dbb)
    dcr, dci = _blockdiag_in_grad(dcc)
    _, vjp = jax.vjp(_s5_tables, p["lam_re"], p["lam_im"], p["log_step"], p["b_re"], p["b_im"])
    dlam_re, dlam_im, dlog_step, db_re, db_im = vjp((dab_q[0], dab_q[1], dbb_re, dbb_im))

    s8 = lambda a: jnp.sum(a, axis=0, keepdims=True)
    hsum = lambda a: jnp.sum(s8(a).reshape(HEADS, HEAD_DIM), axis=1).reshape(1, HEADS)
    small = dict(
        g_mix=s8(dgmix8), conv_w=dcw8[0:4], conv_b=s8(dcb8), dt_bias=s8(ddtb8), a_log=s8(dal8),
        d_ssd=hsum(dd8), g_ssd=s8(dgssd8), lam_re=dlam_re, lam_im=dlam_im, log_step=dlog_step, b_re=db_re, b_im=db_im,
        c_re=jnp.swapaxes(dcr, 1, 2), c_im=-jnp.swapaxes(dci, 1, 2), d_s5=s8(dds5).reshape(NQ * 8, 16),
        b_glu=s8(dbglu8), g_s5=s8(dgs58), g_mlp=s8(dgmlp8), g_final=s8(dgf8).reshape(D))
    return loss, grad_x, dmeta_blk, gslab, dw_in, small


def _perm_rows_w_in(wt):
    return jnp.concatenate([wt[0:1024], wt[2576:3600], wt[1024:2560], wt[2560:2576],
                            jnp.zeros((W_PROJ - 3600, wt.shape[1]), wt.dtype)], axis=0)


def _unperm_cols_w_in(g):
    return jnp.concatenate([g[:, 0:1024], g[:, O_XBC:O_XBC + D_XBC], g[:, O_DT:O_DT + HEADS], g[:, O_U:O_U + D]], axis=1)


def _pack_shard(w_in, w_glu, w_out, w_up, w_down, spare):
    dt = w_in.dtype
    parts = [w_up, w_down, w_glu.T, w_out, w_in.T, spare,
             jnp.zeros((PACK_ROWS - R_SPARE - spare.shape[0], D), dt)]
    return jnp.concatenate(parts, axis=0)


MESH = pl.DeviceIdType.MESH


def _place():
    return lax.axis_index("x"), lax.axis_index("y"), lax.axis_index("c")


def _allgather8(x_shard, name):
    m_per, n = x_shard.shape

    def body(x_ref, out_ref, send_sems, recv_sems):
        x, y, c = _place()
        me, sibling = (x, y, c), (x, y, 1 - c)
        chips = [(1 - x, y), (x, 1 - y), (1 - x, 1 - y)]

        def rows(px, py, pc):
            return out_ref.at[pl.ds((4 * px + 2 * py + pc) * m_per, m_per), :]

        def copy(k, block, to, src=None):
            return pltpu.make_async_remote_copy(
                src_ref=rows(*block) if src is None else src, dst_ref=rows(*block),
                send_sem=send_sems.at[k], recv_sem=recv_sems.at[k], device_id=to, device_id_type=MESH)

        first = [copy(0, me, sibling, src=x_ref)]
        first += [copy(1 + j, me, (*chip, c), src=x_ref) for j, chip in enumerate(chips)]
        for cp in first:
            cp.start()
        passed = [copy(4 + j, (*chip, c), sibling) for j, chip in enumerate(chips)]
        for j, chip in enumerate(chips):
            copy(1 + j, (*chip, c), me).wait_recv()
            passed[j].start()
        copy(0, sibling, me).wait_recv()
        for j, chip in enumerate(chips):
            copy(4 + j, (*chip, 1 - c), me).wait_recv()
        for cp in first + passed:
            cp.wait_send()

    return pl.pallas_call(
        body, name=name, out_shape=jax.ShapeDtypeStruct((8 * m_per, n), x_shard.dtype),
        in_specs=[_ANY], out_specs=_ANY,
        scratch_shapes=[pltpu.SemaphoreType.DMA((7,)), pltpu.SemaphoreType.DMA((7,))])(x_shard)


def _swap_halves(src, name):
    def body(src_ref, out_ref, send_sem, recv_sem):
        x, y, c = _place()
        cp = pltpu.make_async_remote_copy(src_ref=src_ref.at[:, pl.ds((1 - c) * HALF_ROWS, HALF_ROWS), :], dst_ref=out_ref,
                                          send_sem=send_sem, recv_sem=recv_sem, device_id=(x, y, 1 - c), device_id_type=MESH)
        cp.start()
        cp.wait()

    return pl.pallas_call(
        body, name=name, out_shape=jax.ShapeDtypeStruct((4, HALF_ROWS, src.shape[2]), src.dtype), in_specs=[_ANY], out_specs=_ANY,
        scratch_shapes=[pltpu.SemaphoreType.DMA, pltpu.SemaphoreType.DMA])(src)


def _exchange_chips(p, name):
    def body(p_ref, out_ref, send_sems, recv_sems):
        x, y, c = _place()
        chips = [(1 - x, y), (x, 1 - y), (1 - x, 1 - y)]
        cps = [pltpu.make_async_remote_copy(src_ref=p_ref.at[2 * cx + cy], dst_ref=out_ref.at[j], send_sem=send_sems.at[j],
                                            recv_sem=recv_sems.at[j], device_id=(cx, cy, c), device_id_type=MESH)
               for j, (cx, cy) in enumerate(chips)]
        for cp in cps:
            cp.start()
        for cp in cps:
            cp.wait()

    return pl.pallas_call(
        body, name=name, out_shape=jax.ShapeDtypeStruct((3,) + p.shape[1:], p.dtype), in_specs=[_ANY], out_specs=_ANY,
        scratch_shapes=[pltpu.SemaphoreType.DMA((3,)), pltpu.SemaphoreType.DMA((3,))])(p)


def _swap_sibling(r, name):
    def body(r_ref, out_ref, send_sem, recv_sem):
        x, y, c = _place()
        cp = pltpu.make_async_remote_copy(src_ref=r_ref, dst_ref=out_ref, send_sem=send_sem, recv_sem=recv_sem,
                                          device_id=(x, y, 1 - c), device_id_type=MESH)
        cp.start()
        cp.wait()

    return pl.pallas_call(
        body, name=name, out_shape=jax.ShapeDtypeStruct(r.shape, r.dtype), in_specs=[_ANY], out_specs=_ANY,
        scratch_shapes=[pltpu.SemaphoreType.DMA, pltpu.SemaphoreType.DMA])(r)


SH_CONVW, SH_META = 4 * 384, 16 * 256
SPARE_ROWS = 17

SMALL = [("g_mix", (1, 1024)), ("conv_b", (1, 1536)), ("dt_bias", (1, 16)), ("a_log", (1, 16)), ("d_ssd", (1, 16)),
         ("g_ssd", (1, 1024)), ("lam_re", (1, 64, 64)), ("lam_im", (1, 64, 64)), ("log_step", (1, 64)),
         ("b_re", (1, 64, 64, 16)), ("b_im", (1, 64, 64, 16)), ("c_re", (1, 64, 16, 64)), ("c_im", (1, 64, 16, 64)),
         ("d_s5", (1, 64, 16)), ("b_glu", (1, 2048)), ("g_s5", (1, 1024)), ("g_mlp", (1, 1024)), ("g_final", (1024,))]


def _pack_small(arrs, rows):
    flat = jnp.concatenate([a.reshape(-1).astype(f32) for a in arrs])
    return jnp.concatenate([flat, jnp.zeros((rows * D - flat.shape[0],), f32)]).reshape(rows, D)


def _unpack_small(slab, shapes):
    flat = slab.reshape(-1)
    out, o = [], 0
    for shp in shapes:
        n = math.prod(shp)
        out.append(flat[o:o + n].reshape(shp))
        o += n
    return out


def _sum8(g, rows):
    def body(g_ref, o_ref):
        acc = g_ref[0]
        for k in range(1, 8):
            acc = acc + g_ref[k]
        o_ref[...] = acc
    return pl.pallas_call(body, name="sum8", out_shape=jax.ShapeDtypeStruct((rows, D), f32),
                          compiler_params=_cp())(g.reshape(8, rows, D))


def _adamw(name, w, g, m, v, bm):
    def fn(i, r, s, q, refs):
        w_, g_, m_, v_ = r
        m2 = ADAM_B1 * m_ + (1.0 - ADAM_B1) * g_
        v2 = ADAM_B2 * v_ + (1.0 - ADAM_B2) * jnp.square(g_)
        m_hat = m2 / (1.0 - ADAM_B1 ** ADAM_STEP)
        v_hat = v2 / (1.0 - ADAM_B2 ** ADAM_STEP)
        delta = -ADAM_LR * (m_hat / (jnp.sqrt(v_hat) + ADAM_EPS) + ADAM_WD * w_)
        return [delta, m2, v2], [], [], []
    c = w.shape[1]
    (d, m2, v2), _, _, _ = _rowwise(name, fn, w.shape[0] // bm, rows=[w, g, m, v], out_rows=[(c, f32)] * 3, bm=bm)
    return d, m2, v2


def _add_rows(name, parts, out_dtype, rows, bm):
    def fn(i, r, s, q, refs):
        acc = r[0].astype(f32)
        for t in r[1:]:
            acc = acc + t.astype(f32)
        return [acc], [], [], []
    (o,), _, _, _ = _rowwise(name, fn, rows // bm, rows=parts, out_rows=[(D, out_dtype)], bm=bm)
    return o


def kernel(x, meta_tokens, g_mix, w_in, conv_w, conv_b, dt_bias, a_log, d_ssd, g_ssd, lam_re, lam_im, log_step, b_re, b_im, c_re, c_im, d_s5, w_glu, b_glu, g_s5, w_out, g_mlp, w_up, w_down, g_final, loss_target, m_meta_tokens, m_g_mix, m_w_in, m_conv_w, m_conv_b, m_dt_bias, m_a_log, m_d_ssd, m_g_ssd, m_lam_re, m_lam_im, m_log_step, m_b_re, m_b_im, m_c_re, m_c_im, m_d_s5, m_w_glu, m_b_glu, m_g_s5, m_w_out, m_g_mlp, m_w_up, m_w_down, m_g_final, v_meta_tokens, v_g_mix, v_w_in, v_conv_w, v_conv_b, v_dt_bias, v_a_log, v_d_ssd, v_g_ssd, v_lam_re, v_lam_im, v_log_step, v_b_re, v_b_im, v_c_re, v_c_im, v_d_s5, v_w_glu, v_b_glu, v_g_s5, v_w_out, v_g_mlp, v_w_up, v_w_down, v_g_final):
    given = dict(locals())
    cx, cy, cc = _place()
    chip = 2 * cx + cy
    me = 2 * chip + cc

    small_f = jnp.concatenate([conv_w.reshape(-1), meta_tokens.reshape(-1)])
    t_hi = small_f.astype(bf16)
    r_1 = small_f - t_hi.astype(f32)
    t_mid = r_1.astype(bf16)
    t_lo = (r_1 - t_mid.astype(f32)).astype(bf16)
    terms = jnp.concatenate([t_hi, t_mid, t_lo])
    spare = jnp.concatenate([terms, jnp.zeros((SPARE_ROWS * D - terms.shape[0],), bf16)]).reshape(SPARE_ROWS, D)
    slab = _pack_shard(w_in[0].astype(bf16), w_glu[0].astype(bf16), w_out[0].astype(bf16), w_up[0].astype(bf16),
                       w_down[0].astype(bf16), spare)
    my_half = lax.dynamic_slice_in_dim(slab, cc * HALF_ROWS, HALF_ROWS, axis=0)
    gathered = _allgather8(my_half, "gather_weights")
    gathered = lax.dynamic_update_slice_in_dim(gathered, my_half, me * HALF_ROWS, axis=0).reshape(4, PACK_ROWS, D)
    w_in_t = _perm_rows_w_in(jnp.concatenate([gathered[s, R_IN:R_SPARE] for s in range(4)], axis=0))
    n_sf = SH_CONVW + SH_META
    tr = gathered[:, R_SPARE:R_SPARE + SPARE_ROWS].reshape(4, SPARE_ROWS * D)[:, :3 * n_sf].astype(f32).reshape(4, 3, n_sf)
    sp = tr[:, 0] + tr[:, 1] + tr[:, 2]
    conv_w_full = jnp.concatenate([sp[s, :SH_CONVW].reshape(4, 384) for s in range(4)], axis=1)
    meta_full = jnp.concatenate([sp[s, SH_CONVW:].reshape(16, 256) for s in range(4)], axis=1)

    p = dict(g_mix=g_mix, conv_w=conv_w_full, conv_b=conv_b, dt_bias=dt_bias, a_log=a_log, d_ssd=d_ssd, g_ssd=g_ssd,
             lam_re=lam_re[0], lam_im=lam_im[0], log_step=log_step[0], b_re=b_re[0], b_im=b_im[0], c_re=c_re[0], c_im=c_im[0],
             d_s5=d_s5[0], b_glu=b_glu, g_s5=g_s5, g_mlp=g_mlp, g_final=g_final.reshape(1, D))
    loss_part, grad_x, dmeta_blk, gslab, dw_in, g = _local_step(x[0], loss_target[0], meta_full, p, w_in_t, gathered)
    grad_x = grad_x.reshape(x.shape)

    gin_t = _unperm_cols_w_in(dw_in).T
    gslab = lax.dynamic_update_slice(gslab, gin_t.reshape(4, 900, D), (0, R_IN, 0))
    from_sib = _swap_halves(gslab, "rs_pair")
    mine = lax.dynamic_slice_in_dim(gslab, cc * HALF_ROWS, HALF_ROWS, axis=1)
    pair = _add_rows("rs_pair_add", [mine.reshape(4 * HALF_ROWS, D), from_sib.reshape(4 * HALF_ROWS, D)], bf16,
                     4 * HALF_ROWS, 512).reshape(4, HALF_ROWS, D)
    from_chips = _exchange_chips(pair, "rs_chips")
    own = lax.dynamic_index_in_dim(pair, chip, axis=0, keepdims=False)
    red = _add_rows("rs_chip_add", [own, from_chips[0], from_chips[1], from_chips[2]], f32, HALF_ROWS, 512)
    other = _swap_sibling(red, "rs_share")
    lower = jnp.where(cc == 0, red, other)
    upper = jnp.where(cc == 0, other, red)
    g_up, g_down = lower[R_UP:R_UP + 1024], lower[R_DOWN:R_DOWN + 1024]
    g_glu, g_out = upper[R_GLU - HALF_ROWS:R_OUT - HALF_ROWS].T, upper[R_OUT - HALF_ROWS:R_IN - HALF_ROWS]
    g_in = upper[R_IN - HALF_ROWS:R_SPARE - HALF_ROWS].T

    small_g = [g[n] for n, _ in SMALL] + [g["conv_w"], dmeta_blk[CH - N_META:CH], loss_part.reshape(1)]
    n_small = sum(math.prod(s) for _, s in SMALL) + 4 * D_XBC + N_META * D + 1
    rows_small = -(-n_small // (8 * D)) * 8
    mine_small = _pack_small(small_g, rows_small)
    all_small = lax.dynamic_update_slice_in_dim(_allgather8(mine_small, "gather_small"), mine_small, me * rows_small, axis=0)
    total = _sum8(all_small, rows_small)
    outs = _unpack_small(total, [s for _, s in SMALL] + [(4, D_XBC), (N_META, D), ()])
    gs = {n: o for (n, _), o in zip(SMALL, outs)}
    g_conv_w = lax.dynamic_slice_in_dim(outs[-3], chip * 384, 384, axis=1).reshape(conv_w.shape)
    g_meta = lax.dynamic_slice_in_dim(outs[-2], chip * 256, 256, axis=1)
    loss = outs[-1]

    grads = dict(gs, meta_tokens=g_meta, conv_w=g_conv_w, w_in=g_in.reshape(w_in.shape), w_glu=g_glu.reshape(w_glu.shape),
                 w_out=g_out.reshape(w_out.shape), w_up=g_up.reshape(w_up.shape), w_down=g_down.reshape(w_down.shape))
    delta, new_m, new_v = {}, {}, {}
    for n in ("w_in", "w_glu", "w_out", "w_up", "w_down"):
        shp = given[n].shape
        two = lambda a: a.reshape(shp[1], shp[2])
        d_, m_, v_ = _adamw("adamw_" + n, two(given[n]), two(grads[n]), two(given["m_" + n]), two(given["v_" + n]), 256)
        delta[n], new_m[n], new_v[n] = d_.reshape(shp), m_.reshape(shp), v_.reshape(shp)
    names = [n for n, _ in SMALL] + ["conv_w", "meta_tokens"]
    shapes = [given[n].shape for n in names]
    rows_adam = -(-sum(math.prod(s) for s in shapes) // (8 * D)) * 8
    d_, m_, v_ = _adamw("adamw_small", _pack_small([given[n] for n in names], rows_adam),
                        _pack_small([grads[n] for n in names], rows_adam),
                        _pack_small([given["m_" + n] for n in names], rows_adam),
                        _pack_small([given["v_" + n] for n in names], rows_adam), rows_adam)
    for n, a, b, c_ in zip(names, _unpack_small(d_, shapes), _unpack_small(m_, shapes), _unpack_small(v_, shapes)):
        delta[n], new_m[n], new_v[n] = a, b, c_

    order = ["meta_tokens", "g_mix", "w_in", "conv_w", "conv_b", "dt_bias", "a_log", "d_ssd", "g_ssd", "lam_re", "lam_im", "log_step",
             "b_re", "b_im", "c_re", "c_im", "d_s5", "w_glu", "b_glu", "g_s5", "w_out", "g_mlp", "w_up", "w_down", "g_final"]
    grads_out = [grads[n].reshape(given[n].shape) for n in order]
    return (loss, grad_x, *grads_out, *[delta[n] for n in order], *[new_m[n] for n in order], *[new_v[n] for n in order])
```

```python
import math

import jax
import jax.numpy as jnp
from jax import lax
from jax.experimental import pallas as pl
from jax.experimental.pallas import tpu as pltpu

f32 = jnp.float32
bf16 = jnp.bfloat16

D = 1024
N_META = 16
CH = 256
HEADS = 16
HEAD_DIM = 64
NSTATE = 128
D_XBC = 1536
S5_P = 64
NQ = 8
PITCH = CH + 4
EPS = 1e-5
O_Z, O_U, O_XBC, O_DT, W_PROJ = 0, 1024, 2048, 3584, 3712
VMEM_LIMIT = 60 * 1024 * 1024

ADAM_LR, ADAM_B1, ADAM_B2, ADAM_EPS, ADAM_WD, ADAM_STEP = 0.001, 0.9, 0.999, 1e-08, 0.01, 10

NT = (((1,), (1,)), ((), ()))
TN = (((0,), (0,)), ((), ()))
_ANY = pl.BlockSpec(memory_space=pl.ANY)


def _cp(sem=None):
    return pltpu.CompilerParams(dimension_semantics=sem, vmem_limit_bytes=VMEM_LIMIT)


def _sigmoid(v):
    return 1.0 / (1.0 + jnp.exp(-v))


def _rsum8(v):
    r, c = v.shape
    return jnp.sum(v.reshape(r // 8, 8, c), axis=0)


def _rms(h, g):
    r = lax.rsqrt(jnp.mean(h * h, axis=-1, keepdims=True) + EPS)
    return h * r * g


def _rms_bwd(dy, h, g):
    r = lax.rsqrt(jnp.mean(h * h, axis=-1, keepdims=True) + EPS)
    n = h * r
    dn = dy * g
    dh = r * (dn - n * jnp.mean(dn * n, axis=-1, keepdims=True))
    return dh, _rsum8(dy * n)


def _dot(a, b, dims=None):
    if dims is None:
        return jnp.dot(a, b, preferred_element_type=f32)
    return lax.dot_general(a, b, dims, preferred_element_type=f32)


def _split_dot(v, m01, dims, terms, v_is_lhs=True):
    out, r = None, v
    for _ in range(terms):
        piece = r.astype(bf16)
        o = _dot(piece, m01, dims) if v_is_lhs else _dot(m01, piece, dims)
        out = o if out is None else out + o
        r = r - piece.astype(f32)
    return out


def _rowwise(name, fn, nblk, rows=(), shifted=(), pars=(), refs=(), out_rows=(), out_accs=(), out_first=(), out_shifted=(), bm=CH):
    n_r, n_s, n_p, n_w = len(rows), len(shifted), len(pars), len(refs)
    n_in = n_r + n_s + n_p + n_w
    n_o, n_a, n_f, n_so = len(out_rows), len(out_accs), len(out_first), len(out_shifted)

    def body(*all_refs):
        i = pl.program_id(0)
        ins = all_refs[:n_in]
        outs = all_refs[n_in:]
        rv = [r[...] for r in ins[:n_r]]
        sv = [r[...] for r in ins[n_r:n_r + n_s]]
        pv = [r[...] for r in ins[n_r + n_s:n_r + n_s + n_p]]
        ro, ao, fo, so = fn(i, rv, sv, pv, list(ins[n_r + n_s + n_p:]))
        for r, v in zip(outs[:n_o], ro):
            r[...] = v.astype(r.dtype)
        accs = outs[n_o:n_o + n_a]

        @pl.when(i == 0)
        def _():
            for r in accs:
                r[...] = jnp.zeros_like(r)
            for r, v in zip(outs[n_o + n_a:n_o + n_a + n_f], fo):
                r[...] = v.astype(r.dtype)
        for r, v in zip(accs, ao):
            r[...] += v
        for r, v in zip(outs[n_o + n_a + n_f:], so):
            r[...] = v.astype(r.dtype)

    prev = lambda i: (jnp.maximum(i - 1, 0), 0)
    in_specs = [pl.BlockSpec((bm, a.shape[1]), lambda i: (i, 0)) for a in rows]
    in_specs += [pl.BlockSpec((bm, a.shape[1]), prev) for a in shifted]
    in_specs += [pl.BlockSpec(a.shape, lambda i, nd=a.ndim: (0,) * nd) for a in pars]
    in_specs += [spec for _, spec in refs]
    out_specs = [pl.BlockSpec((bm, c), lambda i: (i, 0)) for c, _ in out_rows]
    out_specs += [pl.BlockSpec((8, c), lambda i: (0, 0)) for c in out_accs]
    out_specs += [pl.BlockSpec((bm, c), lambda i: (0, 0)) for c, _ in out_first]
    out_specs += [pl.BlockSpec((bm, c), prev) for c, _ in out_shifted]
    out_shape = [jax.ShapeDtypeStruct((nblk * bm, c), dt) for c, dt in out_rows]
    out_shape += [jax.ShapeDtypeStruct((8, c), f32) for c in out_accs]
    out_shape += [jax.ShapeDtypeStruct((bm, c), dt) for c, dt in out_first]
    out_shape += [jax.ShapeDtypeStruct(((nblk - 1) * bm, c), dt) for c, dt in out_shifted]
    res = pl.pallas_call(body, name=name, grid=(nblk,), in_specs=in_specs, out_specs=out_specs, out_shape=out_shape,
                         compiler_params=_cp(("arbitrary",)))(*rows, *shifted, *pars, *[a for a, _ in refs])
    return res[:n_o], res[n_o:n_o + n_a], res[n_o + n_a:n_o + n_a + n_f], res[n_o + n_a + n_f:]


PACK_ROWS = 4096
HALF_ROWS = PACK_ROWS // 2
R_UP, R_DOWN, R_GLU, R_OUT, R_IN, R_SPARE = 0, 1024, 2048, 2560, 3072, 3972


def _contract_rows(lp):
    return lp // 8 if lp % (8 * 16) == 0 else CH


def _dw_into(name, a, b, slab, ka, a_sharded, row_blk, n_s, s0):
    lp = a.shape[0]
    bm = _contract_rows(lp)
    steps = lp // bm

    def body(a_ref, b_ref, *rest):
        o_ref, acc = rest[-2], rest[-1]
        k = pl.program_id(1)

        @pl.when(k == 0)
        def _():
            acc[...] = jnp.zeros_like(acc)
        acc[...] += _dot(a_ref[...], b_ref[...], TN)

        @pl.when(k == steps - 1)
        def _():
            o_ref[0] = acc[...].astype(bf16)

    in_specs = [pl.BlockSpec((bm, ka), (lambda s, k: (k, s)) if a_sharded else (lambda s, k: (k, 0))),
                pl.BlockSpec((bm, D), (lambda s, k: (k, 0)) if a_sharded else (lambda s, k: (k, s)))]
    args = [a, b]
    aliases = {}
    if slab is not None:
        in_specs.append(_ANY)
        args.append(slab)
        aliases = {2: 0}
    return pl.pallas_call(
        body, name=name, grid=(n_s, steps), in_specs=in_specs,
        out_specs=pl.BlockSpec((1, ka, D), lambda s, k: (s0 + s, row_blk, 0)),
        out_shape=jax.ShapeDtypeStruct((4, PACK_ROWS, D), bf16),
        scratch_shapes=[pltpu.VMEM((ka, D), f32)], input_output_aliases=aliases,
        compiler_params=_cp(("arbitrary", "arbitrary")))(*args)


def _dw_in(n0, dproj):
    lp = n0.shape[0]
    bm = _contract_rows(lp) // 2 if _contract_rows(lp) % 32 == 0 else CH
    steps = lp // bm
    bk = 512

    def body(a_ref, b_ref, o_ref, acc):
        k = pl.program_id(1)

        @pl.when(k == 0)
        def _():
            acc[...] = jnp.zeros_like(acc)
        acc[...] += _dot(a_ref[...], b_ref[...], TN)

        @pl.when(k == steps - 1)
        def _():
            o_ref[...] = acc[...].astype(bf16)

    return pl.pallas_call(
        body, name="dw_in", grid=(D // bk, steps),
        in_specs=[pl.BlockSpec((bm, bk), lambda i, k: (k, i)), pl.BlockSpec((bm, W_PROJ), lambda i, k: (k, 0))],
        out_specs=pl.BlockSpec((bk, W_PROJ), lambda i, k: (i, 0)),
        out_shape=jax.ShapeDtypeStruct((D, W_PROJ), bf16),
        scratch_shapes=[pltpu.VMEM((bk, W_PROJ), f32)],
        compiler_params=_cp(("arbitrary", "arbitrary")))(n0, dproj)


def _head_expand():
    h = lax.broadcasted_iota(jnp.int32, (HEADS, D), 0)
    c = lax.broadcasted_iota(jnp.int32, (HEADS, D), 1)
    return jnp.where((c >> 6) == h, 1.0, 0.0).astype(bf16)


def _ssd_common(i, P, prev8, cw, cb, dtb, alog, xc=None):
    z = P[:, O_Z:O_Z + D]
    xp = P[:, O_XBC:O_XBC + D_XBC]
    dt_raw = P[:, O_DT:O_DT + HEADS]
    row = lax.broadcasted_iota(jnp.int32, (CH, 1), 0)
    if xc is None:
        row8 = lax.broadcasted_iota(jnp.int32, (8, 1), 0)
        xc = cb + cw[3:4] * xp
        for k in (1, 2, 3):
            rolled = pltpu.roll(xp, k, 0)
            fix = pltpu.roll(prev8, k, 0)
            top = jnp.where(row8 < k, fix, rolled[0:8])
            xc = xc + cw[3 - k:4 - k] * jnp.concatenate([top, rolled[8:]], axis=0)
    sg = _sigmoid(xc)
    xbc = xc * sg
    live = jnp.where(jnp.logical_or(i > 0, row >= CH - N_META), 1.0, 0.0)
    pre = dt_raw + dtb
    dt = jnp.where(pre > 20.0, pre, jnp.log(1.0 + jnp.exp(jnp.minimum(pre, 20.0)))) * live
    a = -jnp.exp(alog)
    dta = dt * a
    r_i = lax.broadcasted_iota(jnp.int32, (CH, CH), 0)
    c_i = lax.broadcasted_iota(jnp.int32, (CH, CH), 1)
    tril = r_i >= c_i
    acs = _split_dot(dta, jnp.where(tril, 1.0, 0.0).astype(bf16), None, 3, v_is_lhs=False)
    acs_t = _split_dot(dta, jnp.where(r_i <= c_i, 1.0, 0.0).astype(bf16), TN, 3)
    e = _head_expand()
    acs_e = _split_dot(acs, e, None, 3)
    dt_e = _split_dot(dt, e, None, 3)
    return dict(z=z, xp=xp, xc=xc, sg=sg, xbc=xbc, live=live, pre=pre, dt=dt, a=a, tril=tril,
                acs=acs, acs_t=acs_t, e=e, acs_e=acs_e, dt_e=dt_e)


def _lmat(c, h):
    seg = c["acs"][:, h:h + 1] - c["acs_t"][h:h + 1, :]
    return jnp.where(c["tril"], jnp.exp(jnp.minimum(seg, 0.0)), 0.0)


def _pair_masks():
    lane = lax.broadcasted_iota(jnp.int32, (1, 128), 1)
    return jnp.where(lane < HEAD_DIM, 1.0, 0.0), jnp.where(lane >= HEAD_DIM, 1.0, 0.0)


def _ssd_fwd(proj, conv_w, conv_b, dt_bias, a_log, d_ssd, g_ssd, nch):
    def body(p_ref, cw_ref, cb_ref, dtb_ref, al_ref, d_ref, g_ref, y_ref, ys_ref, st_ref, xc_ref, prev8_ref, state_ref):
        i = pl.program_id(0)

        @pl.when(i == 0)
        def _():
            prev8_ref[...] = jnp.zeros_like(prev8_ref)
            state_ref[...] = jnp.zeros_like(state_ref)

        P = p_ref[...]
        c = _ssd_common(i, P, prev8_ref[...], cw_ref[...], cb_ref[...], dtb_ref[...], al_ref[...])
        prev8_ref[...] = c["xp"][CH - 8:CH]
        xc_ref[...] = c["xc"]
        xbc = c["xbc"]
        x = xbc[:, 0:D]
        xdt = x * c["dt_e"]
        a_last_e = c["acs_e"][CH - 1:CH, :]
        w_end = (xdt * jnp.exp(a_last_e - c["acs_e"])).astype(bf16)
        m0, m1 = _pair_masks()
        ys = []
        for g in range(2):
            bg = xbc[:, D + NSTATE * g:D + NSTATE * (g + 1)].astype(bf16)
            cg = xbc[:, D + 2 * NSTATE + NSTATE * g:D + 2 * NSTATE + NSTATE * (g + 1)].astype(bf16)
            gmat = _dot(cg, bg, NT)
            st = state_ref[g]
            st_ref[0, g] = st
            sl = slice(512 * g, 512 * (g + 1))
            y_off = _dot(cg, st.astype(bf16)) * jnp.exp(c["acs_e"][:, sl])
            contrib = _dot(bg, w_end[:, sl], TN)
            state_ref[g] = st * jnp.exp(a_last_e[:, sl]) + contrib
            yd = []
            for pr in range(4):
                h0 = 8 * g + 2 * pr
                xp2 = xdt[:, 128 * (4 * g + pr):128 * (4 * g + pr + 1)]
                ma = (gmat * _lmat(c, h0)).astype(bf16)
                mb = (gmat * _lmat(c, h0 + 1)).astype(bf16)
                yd.append(_dot(ma, (xp2 * m0).astype(bf16)) + _dot(mb, (xp2 * m1).astype(bf16)))
            ys.append(jnp.concatenate(yd, axis=1) + y_off)
        d_e = _split_dot(d_ref[...], c["e"], None, 3)
        y = jnp.concatenate(ys, axis=1) + x * d_e
        y_ref[...] = y
        yg = y * (c["z"] * _sigmoid(c["z"]))
        ys_ref[...] = _rms(yg, g_ref[...]).astype(bf16)

    full = lambda a: pl.BlockSpec(a.shape, lambda i, nd=a.ndim: (0,) * nd)
    return pl.pallas_call(
        body, name="ssd_fwd", grid=(nch,),
        in_specs=[pl.BlockSpec((CH, W_PROJ), lambda i: (i, 0))] + [full(a) for a in (conv_w, conv_b, dt_bias, a_log, d_ssd, g_ssd)],
        out_specs=[pl.BlockSpec((CH, D), lambda i: (i, 0)), pl.BlockSpec((CH, D), lambda i: (i, 0)),
                   pl.BlockSpec((1, 2, NSTATE, 512), lambda i: (i, 0, 0, 0)), pl.BlockSpec((CH, D_XBC), lambda i: (i, 0))],
        out_shape=[jax.ShapeDtypeStruct((nch * CH, D), f32), jax.ShapeDtypeStruct((nch * CH, D), bf16),
                   jax.ShapeDtypeStruct((nch, 2, NSTATE, 512), f32), jax.ShapeDtypeStruct((nch * CH, D_XBC), f32)],
        scratch_shapes=[pltpu.VMEM((8, D_XBC), f32), pltpu.VMEM((2, NSTATE, 512), f32)],
        compiler_params=_cp(("arbitrary",)))(proj, conv_w, conv_b, dt_bias, a_log, d_ssd, g_ssd)


def _ssd_bwd(proj, xc_all, y, dys, du, states, conv_w, conv_b, dt_bias, a_log, d_ssd, g_ssd, nch):
    def body(p_ref, xc_ref, y_ref, dys_ref, du_ref, st_ref, cw_ref, cb_ref, dtb_ref, al_ref, d_ref, g_ref,
             dp_ref, dcw_ref, dcb_ref, ddtb_ref, dal_ref, dd_ref, dg_ref, nxt8_ref, dst_ref):
        step = pl.program_id(0)
        i = nch - 1 - step

        @pl.when(step == 0)
        def _():
            nxt8_ref[...] = jnp.zeros_like(nxt8_ref)
            dst_ref[...] = jnp.zeros_like(dst_ref)
            for r in (dcw_ref, dcb_ref, ddtb_ref, dal_ref, dd_ref, dg_ref):
                r[...] = jnp.zeros_like(r)

        P = p_ref[...]
        c = _ssd_common(i, P, None, cw_ref[...], cb_ref[...], dtb_ref[...], al_ref[...], xc=xc_ref[...])
        xbc, z, e = c["xbc"], c["z"], c["e"]
        x = xbc[:, 0:D]
        yv = y_ref[...]
        sz = _sigmoid(z)
        silu_z = z * sz
        dyg, dg8 = _rms_bwd(dys_ref[...], yv * silu_z, g_ref[...])
        dg_ref[...] += dg8
        dy = dyg * silu_z
        dz = dyg * yv * (sz * (1.0 + z * (1.0 - sz)))
        d_e = _split_dot(d_ref[...], e, None, 3)
        dd_ref[...] += _rsum8(dy * x)
        xdt = x * c["dt_e"]
        a_last_e = c["acs_e"][CH - 1:CH, :]
        e_end = jnp.exp(a_last_e - c["acs_e"])
        w_end = xdt * e_end
        e_acs = jnp.exp(c["acs_e"])
        dy_dec = dy * e_acs
        m0, m1 = _pair_masks()
        lane16 = lax.broadcasted_iota(jnp.int32, (1, HEADS), 1)
        row16 = lax.broadcasted_iota(jnp.int32, (HEADS, 1), 0)
        dacs = jnp.zeros((CH, HEADS), f32)
        dacs_t = jnp.zeros((HEADS, CH), f32)
        dxdt_parts, dbs, dcs, zparts, yoff_parts, dlast_parts = [], [], [], [], [], []
        for g in range(2):
            sl = slice(512 * g, 512 * (g + 1))
            bg = xbc[:, D + NSTATE * g:D + NSTATE * (g + 1)].astype(bf16)
            cg = xbc[:, D + 2 * NSTATE + NSTATE * g:D + 2 * NSTATE + NSTATE * (g + 1)].astype(bf16)
            gmat = _dot(cg, bg, NT)
            st = st_ref[0, g]
            dstn = dst_ref[g]
            dstn_b = dstn.astype(bf16)
            y_off = _dot(cg, st.astype(bf16)) * e_acs[:, sl]
            yoff_parts.append(y_off)
            bds = _dot(bg, dstn_b)
            zparts.append(w_end[:, sl] * bds)
            dlast_parts.append(jnp.sum(dstn * st, axis=0, keepdims=True) * jnp.exp(a_last_e[:, sl]))
            dg_acc = jnp.zeros((CH, CH), f32)
            dxd = []
            for pr in range(4):
                lo = 128 * (4 * g + pr)
                xp2 = xdt[:, lo:lo + 128].astype(bf16)
                dy2 = dy[:, lo:lo + 128]
                outp = jnp.zeros((CH, 128), f32)
                for hh, msk in ((0, m0), (1, m1)):
                    h = 8 * g + 2 * pr + hh
                    lm = _lmat(c, h)
                    dyh = (dy2 * msk).astype(bf16)
                    mh = (gmat * lm).astype(bf16)
                    outp = outp + _dot(mh, dyh, TN)
                    dml = _dot(dyh, xp2, NT) * lm
                    dg_acc = dg_acc + dml
                    q = dml * gmat
                    dacs = dacs + jnp.where(lane16 == h, jnp.sum(q, axis=1, keepdims=True), 0.0)
                    dacs_t = dacs_t + jnp.where(row16 == h, jnp.sum(q, axis=0, keepdims=True), 0.0)
                dxd.append(outp)
            dxdt_parts.append(jnp.concatenate(dxd, axis=1) + e_end[:, sl] * bds)
            dgb = dg_acc.astype(bf16)
            dcs.append(_dot(dgb, bg) + _dot(dy_dec[:, sl].astype(bf16), st.astype(bf16), NT))
            dbs.append(_dot(dgb, cg, TN) + _dot(w_end[:, sl].astype(bf16), dstn_b, NT))
            dst_ref[g] = dstn * jnp.exp(a_last_e[:, sl]) + _dot(cg, dy_dec[:, sl].astype(bf16), TN)
        dxdt = jnp.concatenate(dxdt_parts, axis=1)
        zfull = jnp.concatenate(zparts, axis=1)
        y_off_full = jnp.concatenate(yoff_parts, axis=1)
        dlast = jnp.concatenate(dlast_parts, axis=1)
        red = lambda v: _split_dot(v, e, NT, 2)
        eye16 = jnp.where(lax.broadcasted_iota(jnp.int32, (HEADS, HEADS), 0) == lax.broadcasted_iota(jnp.int32, (HEADS, HEADS), 1),
                          1.0, 0.0).astype(bf16)
        dacs = dacs - _split_dot(dacs_t, eye16, TN, 3)
        zred = red(zfull)
        dacs = dacs + red(dy * y_off_full) - zred
        last_term = jnp.sum(zred, axis=0, keepdims=True) + red(dlast)
        rowc = lax.broadcasted_iota(jnp.int32, (CH, 1), 0)
        dacs = dacs + jnp.where(rowc == CH - 1, last_term, 0.0)
        r_i = lax.broadcasted_iota(jnp.int32, (CH, CH), 0)
        c_i = lax.broadcasted_iota(jnp.int32, (CH, CH), 1)
        ddta = _split_dot(dacs, jnp.where(c_i >= r_i, 1.0, 0.0).astype(bf16), None, 3, v_is_lhs=False)
        ddt = ddta * c["a"] + red(dxdt * x)
        dal_ref[...] += _rsum8(ddta * c["dt"] * c["a"])
        ddt_raw = ddt * _sigmoid(c["pre"]) * c["live"]
        ddtb_ref[...] += _rsum8(ddt_raw)
        dx = dy * d_e + dxdt * c["dt_e"]
        dxbc = jnp.concatenate([dx, dbs[0], dbs[1], dcs[0], dcs[1]], axis=1)
        sg = c["sg"]
        dxc = dxbc * (sg * (1.0 + c["xc"] * (1.0 - sg)))
        dcb_ref[...] += _rsum8(dxc)
        xp = c["xp"]
        row8 = lax.broadcasted_iota(jnp.int32, (8, 1), 0)
        cw = cw_ref[...]
        dxp = cw[3:4] * dxc
        dcw = jnp.where(row8 == 3, jnp.sum(dxc * xp, axis=0, keepdims=True), 0.0)
        nxt8 = nxt8_ref[...]
        for j in (1, 2, 3):
            rolled = pltpu.roll(dxc, CH - j, 0)
            fix = pltpu.roll(nxt8, 8 - j, 0)
            bot = jnp.where(row8 >= 8 - j, fix, rolled[CH - 8:CH])
            later = jnp.concatenate([rolled[:CH - 8], bot], axis=0)
            dxp = dxp + cw[3 - j:4 - j] * later
            dcw = dcw + jnp.where(row8 == 3 - j, jnp.sum(later * xp, axis=0, keepdims=True), 0.0)
        dcw_ref[...] += dcw
        nxt8_ref[...] = dxc[0:8]
        dp_ref[:, O_Z:O_Z + D] = dz.astype(bf16)
        dp_ref[:, O_U:O_U + D] = du_ref[...].astype(bf16)
        dp_ref[:, O_XBC:O_XBC + D_XBC] = dxp.astype(bf16)
        dp_ref[:, O_DT:W_PROJ] = jnp.zeros((CH, W_PROJ - O_DT), bf16)
        dp_ref[:, O_DT:O_DT + HEADS] = ddt_raw.astype(bf16)

    full = lambda a: pl.BlockSpec(a.shape, lambda s, nd=a.ndim: (0,) * nd)
    rev = lambda s: (nch - 1 - s, 0)
    acc = lambda cdim: pl.BlockSpec((8, cdim), lambda s: (0, 0))
    return pl.pallas_call(
        body, name="ssd_bwd", grid=(nch,),
        in_specs=[pl.BlockSpec((CH, W_PROJ), rev), pl.BlockSpec((CH, D_XBC), rev),
                  pl.BlockSpec((CH, D), rev), pl.BlockSpec((CH, D), rev), pl.BlockSpec((CH, D), rev),
                  pl.BlockSpec((1, 2, NSTATE, 512), lambda s: (nch - 1 - s, 0, 0, 0))]
        + [full(a) for a in (conv_w, conv_b, dt_bias, a_log, d_ssd, g_ssd)],
        out_specs=[pl.BlockSpec((CH, W_PROJ), rev), acc(D_XBC), acc(D_XBC), acc(HEADS), acc(HEADS), acc(D), acc(D)],
        out_shape=[jax.ShapeDtypeStruct((nch * CH, W_PROJ), bf16)]
        + [jax.ShapeDtypeStruct((8, cdim), f32) for cdim in (D_XBC, D_XBC, HEADS, HEADS, D, D)],
        scratch_shapes=[pltpu.VMEM((8, D_XBC), f32), pltpu.VMEM((2, NSTATE, 512), f32)],
        compiler_params=_cp(("arbitrary",)))(proj, xc_all, y, dys, du, states, conv_w, conv_b, dt_bias, a_log, d_ssd, g_ssd)


SCAN_UNROLL = 8


def _to_slabs(slab_ref, q, mat):
    for ls in range(8):
        slab_ref[ls, pl.ds(PITCH * q, CH), :] = mat[:, 128 * ls:128 * (ls + 1)]


def _from_slabs(slab, q):
    return jnp.concatenate([slab(ls, PITCH * q) for ls in range(8)], axis=1)


def _tile(slab_ref, ls, t, lead=None):
    idx = (ls, pl.ds(t, 8, stride=PITCH), slice(None))
    return slab_ref[idx] if lead is None else slab_ref[(lead,) + idx]


def _s5_fwd(proj, bbq, ccq_t, ar, ai, d_skip, nch):
    def body(u_ref, bb_ref, cc_ref, ar_ref, ai_ref, d_ref, s_ref, yl_ref, y5_ref, bu_ref, st_ref):
        @pl.when(pl.program_id(0) == 0)
        def _():
            st_ref[...] = jnp.zeros_like(st_ref)
        u = u_ref[...]
        ub = u.astype(bf16)
        for q in range(NQ):
            _to_slabs(bu_ref, q, _dot(ub[:, 128 * q:128 * (q + 1)], bb_ref[q]))
        ar_t = [ar_ref[:, 128 * l:128 * (l + 1)] for l in range(4)]
        ai_t = [ai_ref[:, 128 * l:128 * (l + 1)] for l in range(4)]

        def one(t, carry):
            re, im = carry
            nre, nim = [], []
            for l in range(4):
                a = ar_t[l] * re[l] - ai_t[l] * im[l] + _tile(bu_ref, l, t)
                b = ar_t[l] * im[l] + ai_t[l] * re[l] + _tile(bu_ref, l + 4, t)
                s_ref[0, l, pl.ds(t, 8, stride=PITCH), :] = a
                s_ref[0, l + 4, pl.ds(t, 8, stride=PITCH), :] = b
                nre.append(a)
                nim.append(b)
            return tuple(nre), tuple(nim)

        def step(tt, carry):
            for k in range(SCAN_UNROLL):
                carry = one(tt * SCAN_UNROLL + k, carry)
            return carry
        init = (tuple(st_ref[l] for l in range(4)), tuple(st_ref[l + 4] for l in range(4)))
        re, im = lax.fori_loop(0, CH // SCAN_UNROLL, step, init)
        for l in range(4):
            st_ref[l] = re[l]
            st_ref[l + 4] = im[l]
        ys = []
        for q in range(NQ):
            sq = _from_slabs(lambda ls, r0: s_ref[0, ls, pl.ds(r0, CH), :], q).astype(bf16)
            ys.append(_dot(sq, cc_ref[q]))
        yl = jnp.concatenate(ys, axis=1) + u * d_ref[...]
        yl_ref[...] = yl
        y5_ref[...] = (0.5 * yl * (1.0 + lax.erf(yl * (1.0 / math.sqrt(2.0))))).astype(bf16)

    const = lambda a: pl.BlockSpec(a.shape, lambda i, nd=a.ndim: (0,) * nd)
    return pl.pallas_call(
        body, name="s5_fwd", grid=(nch,),
        in_specs=[pl.BlockSpec((CH, D), lambda i: (i, O_U // D)), const(bbq), const(ccq_t), const(ar), const(ai), const(d_skip)],
        out_specs=[pl.BlockSpec((1, 8, 8 * PITCH, 128), lambda i: (i, 0, 0, 0)),
                   pl.BlockSpec((CH, D), lambda i: (i, 0)), pl.BlockSpec((CH, D), lambda i: (i, 0))],
        out_shape=[jax.ShapeDtypeStruct((nch, 8, 8 * PITCH, 128), f32), jax.ShapeDtypeStruct((nch * CH, D), f32),
                   jax.ShapeDtypeStruct((nch * CH, D), bf16)],
        scratch_shapes=[pltpu.VMEM((8, 8 * PITCH, 128), f32), pltpu.VMEM((8, 8, 128), f32)],
        compiler_params=_cp(("arbitrary",)))(proj, bbq, ccq_t, ar, ai, d_skip)


def _s5_bwd(proj, dyl, s_all, bbtq, cctq, ar, ai, d_skip, nch):
    def body(u_ref, dy_ref, s_ref, bbt_ref, cct_ref, ar_ref, ai_ref, d_ref,
             du_ref, dcc_ref, dbb_ref, dab_ref, dd_ref, ga_ref, st_ref):
        @pl.when(pl.program_id(0) == 0)
        def _():
            st_ref[...] = jnp.zeros_like(st_ref)
            for r in (dcc_ref, dbb_ref, dab_ref, dd_ref):
                r[...] = jnp.zeros_like(r)
        u = u_ref[...]
        dyl_v = dy_ref[...]
        dd_ref[...] += _rsum8(dyl_v * u)
        ub = u.astype(bf16)
        dyb = dyl_v.astype(bf16)
        for q in range(NQ):
            _to_slabs(ga_ref, q, _dot(dyb[:, 128 * q:128 * (q + 1)], cct_ref[q]))
        ar_t = [ar_ref[:, 128 * l:128 * (l + 1)] for l in range(4)]
        ai_t = [ai_ref[:, 128 * l:128 * (l + 1)] for l in range(4)]

        def one(t, carry):
            re, im, dar, dai = carry
            nre, nim, ndar, ndai = [], [], [], []
            for l in range(4):
                sre = _tile(s_ref, l, t, lead=0)
                sim = _tile(s_ref, l + 4, t, lead=0)
                ndar.append(dar[l] + re[l] * sre + im[l] * sim)
                ndai.append(dai[l] + im[l] * sre - re[l] * sim)
                a = _tile(ga_ref, l, t) + ar_t[l] * re[l] + ai_t[l] * im[l]
                b = _tile(ga_ref, l + 4, t) - ai_t[l] * re[l] + ar_t[l] * im[l]
                ga_ref[l, pl.ds(t, 8, stride=PITCH), :] = a
                ga_ref[l + 4, pl.ds(t, 8, stride=PITCH), :] = b
                nre.append(a)
                nim.append(b)
            return tuple(nre), tuple(nim), tuple(ndar), tuple(ndai)

        def step(tt, carry):
            for k in range(SCAN_UNROLL):
                carry = one(CH - 1 - (tt * SCAN_UNROLL + k), carry)
            return carry
        four = lambda ref, o: tuple(ref[l + o] for l in range(4))
        re, im, dar, dai = lax.fori_loop(0, CH // SCAN_UNROLL, step,
                                         (four(st_ref, 0), four(st_ref, 4), four(dab_ref, 0), four(dab_ref, 4)))
        for l in range(4):
            st_ref[l], st_ref[l + 4] = re[l], im[l]
            dab_ref[l], dab_ref[l + 4] = dar[l], dai[l]
        dus = []
        for q in range(NQ):
            aq = _from_slabs(lambda ls, r0: ga_ref[ls, pl.ds(r0, CH), :], q).astype(bf16)
            sq = _from_slabs(lambda ls, r0: s_ref[0, ls, pl.ds(r0, CH), :], q).astype(bf16)
            dcc_ref[q] += _dot(dyb[:, 128 * q:128 * (q + 1)], sq, TN)
            dbb_ref[q] += _dot(ub[:, 128 * q:128 * (q + 1)], aq, TN)
            dus.append(_dot(aq, bbt_ref[q]))
        du_ref[...] = jnp.concatenate(dus, axis=1) + dyl_v * d_ref[...]

    const = lambda a: pl.BlockSpec(a.shape, lambda s, nd=a.ndim: (0,) * nd)
    rev = lambda s: (nch - 1 - s, 0)
    return pl.pallas_call(
        body, name="s5_bwd", grid=(nch,),
        in_specs=[pl.BlockSpec((CH, D), lambda s: (nch - 1 - s, O_U // D)), pl.BlockSpec((CH, D), rev),
                  pl.BlockSpec((1, 8, 8 * PITCH, 128), lambda s: (nch - 1 - s, 0, 0, 0)),
                  const(bbtq), const(cctq), const(ar), const(ai), const(d_skip)],
        out_specs=[pl.BlockSpec((CH, D), rev), pl.BlockSpec((NQ, 128, D), lambda s: (0, 0, 0)),
                   pl.BlockSpec((NQ, 128, D), lambda s: (0, 0, 0)), pl.BlockSpec((8, 8, 128), lambda s: (0, 0, 0)),
                   pl.BlockSpec((8, D), lambda s: (0, 0))],
        out_shape=[jax.ShapeDtypeStruct((nch * CH, D), f32), jax.ShapeDtypeStruct((NQ, 128, D), f32),
                   jax.ShapeDtypeStruct((NQ, 128, D), f32), jax.ShapeDtypeStruct((8, 8, 128), f32),
                   jax.ShapeDtypeStruct((8, D), f32)],
        scratch_shapes=[pltpu.VMEM((8, 8 * PITCH, 128), f32), pltpu.VMEM((8, 8, 128), f32)],
        compiler_params=_cp(("arbitrary",)))(proj, dyl, s_all, bbtq, cctq, ar, ai, d_skip)


def _s5_tables(lam_re, lam_im, log_step, b_re, b_im):
    step = jnp.exp(log_step)[:, None]
    mag = jnp.exp(lam_re * step)
    ab_re = mag * jnp.cos(lam_im * step)
    ab_im = mag * jnp.sin(lam_im * step)
    den = lam_re * lam_re + lam_im * lam_im
    coef_re = ((ab_re - 1.0) * lam_re + ab_im * lam_im) / den
    coef_im = (ab_im * lam_re - (ab_re - 1.0) * lam_im) / den
    bb_re = coef_re[..., None] * b_re - coef_im[..., None] * b_im
    bb_im = coef_re[..., None] * b_im + coef_im[..., None] * b_re
    return ab_re, ab_im, bb_re, bb_im


def _blockdiag_in(m_re, m_im):
    eye = jnp.eye(8, dtype=f32)

    def one(m):
        m = m.reshape(NQ, 8, S5_P, 16)
        return jnp.einsum("qgph,gk->qghkp", m, eye).reshape(NQ, 128, 512)
    return jnp.concatenate([one(m_re), one(m_im)], axis=2)


def _blockdiag_in_grad(dm):
    def one(x):
        x = x.reshape(NQ, 8, 16, 8, S5_P)
        return jnp.einsum("qghgp->qgph", x).reshape(NQ * 8, S5_P, 16)
    return one(dm[:, :, :512]), one(dm[:, :, 512:])


def _local_step(x2, tgt2, meta, p, w_in_t, slabs):
    seq = x2.shape[0]
    nch = 1 + seq // CH
    metablk = jnp.concatenate([jnp.zeros((CH - N_META, D), f32), meta], axis=0)
    w_full = (w_in_t, pl.BlockSpec(w_in_t.shape, lambda i: (0, 0)))
    w_up = (slabs, pl.BlockSpec((4, 1024, D), lambda i: (0, R_UP // 1024, 0)))
    w_down = (slabs, pl.BlockSpec((4, 1024, D), lambda i: (0, R_DOWN // 1024, 0)))
    w_glu_t = (slabs, pl.BlockSpec((4, 512, D), lambda i: (0, R_GLU // 512, 0)))
    w_out = (slabs, pl.BlockSpec((4, 512, D), lambda i: (0, R_OUT // 512, 0)))
    h0_of = lambda i, s, q: jnp.where(i == 0, q[0], s[0])

    def in_fn(i, r, s, q, w):
        nb = _rms(h0_of(i, s, q), q[1]).astype(bf16)
        return [_dot(nb, w[0][...], NT), nb], [], [], []
    (proj, n0), _, _, _ = _rowwise("in_proj", in_fn, nch, shifted=[x2], pars=[metablk, p["g_mix"]], refs=[w_full],
                                   out_rows=[(W_PROJ, f32), (D, bf16)])
    y, y_ssd, states, xc_all = _ssd_fwd(proj, p["conv_w"], p["conv_b"], p["dt_bias"], p["a_log"], p["d_ssd"], p["g_ssd"], nch)

    ab_re, ab_im, bb_re, bb_im = _s5_tables(p["lam_re"], p["lam_im"], p["log_step"], p["b_re"], p["b_im"])
    ar, ai = ab_re.reshape(NQ, 512), ab_im.reshape(NQ, 512)
    bbq = _blockdiag_in(bb_re, bb_im)
    ccq = _blockdiag_in(jnp.swapaxes(p["c_re"], 1, 2), -jnp.swapaxes(p["c_im"], 1, 2))
    d_skip = p["d_s5"].reshape(1, D)
    s_all, ylin, y5 = _s5_fwd(proj, bbq.astype(bf16), jnp.swapaxes(ccq, 1, 2).astype(bf16), ar, ai, d_skip, nch)

    def glu_fn(i, r, s, q, w):
        v = jnp.concatenate([_dot(r[0], w[0][k], NT) for k in range(4)], axis=1) + q[0]
        return [v, _rms(v[:, :D] * _sigmoid(v[:, D:]), q[1])], [], [], []
    (v, y_s5), _, _, _ = _rowwise("glu", glu_fn, nch, rows=[y5], pars=[p["b_glu"], p["g_s5"]], refs=[w_glu_t],
                                  out_rows=[(2 * D, f32), (D, bf16)])

    def out_fn(i, r, s, q, w):
        acc = (_dot(r[0][:, :512], w[0][0]) + _dot(r[0][:, 512:], w[0][1]) + _dot(r[1][:, :512], w[0][2])
               + _dot(r[1][:, 512:], w[0][3]))
        return [h0_of(i, s, q) + acc], [], [], []
    (h1,), _, _, _ = _rowwise("out_proj", out_fn, nch, rows=[y_ssd, y_s5], shifted=[x2], pars=[metablk], refs=[w_out],
                              out_rows=[(D, f32)])

    def up_fn(i, r, s, q, w):
        nb = _rms(r[0], q[0]).astype(bf16)
        act = [jnp.square(jnp.maximum(_dot(nb, w[0][k]), 0.0)).astype(bf16) for k in range(4)]
        return [jnp.concatenate(act, axis=1), nb], [], [], []
    (act, n1), _, _, _ = _rowwise("up_proj", up_fn, nch, rows=[h1], pars=[p["g_mlp"]], refs=[w_up],
                                  out_rows=[(4 * D, bf16), (D, bf16)])

    def down_fn(i, r, s, q, w):
        acc = _dot(r[0][:, :D], w[0][0])
        for k in range(1, 4):
            acc = acc + _dot(r[0][:, D * k:D * (k + 1)], w[0][k])
        return [r[1] + acc], [], [], []
    (h2,), _, _, _ = _rowwise("down_proj", down_fn, nch, rows=[act, h1], refs=[w_down], out_rows=[(D, f32)])

    def final_fn(i, r, s, q, w):
        live = jnp.where(i > 0, 1.0, 0.0)
        err = (_rms(r[0], q[0]) - s[0]) * live
        dh, dg8 = _rms_bwd(err * (1.0 / D), r[0], q[0])
        return [dh, dh], [_rsum8(err * err), dg8], [], []
    (dh2, dh2_b), (loss8, dgf8), _, _ = _rowwise("final", final_fn, nch, rows=[h2], shifted=[tgt2], pars=[p["g_final"]],
                                                  out_rows=[(D, f32), (D, bf16)], out_accs=[D, D])
    loss = 0.5 / D * jnp.sum(loss8)

    def down_bwd_fn(i, r, s, q, w):
        dm_ = [_dot(r[0], w[0][k], NT) * 2.0 * jnp.sqrt(r[1][:, D * k:D * (k + 1)].astype(f32)) for k in range(4)]
        return [jnp.concatenate(dm_, axis=1)], [], [], []
    (dm,), _, _, _ = _rowwise("down_bwd", down_bwd_fn, nch, rows=[dh2_b, act], refs=[w_down], out_rows=[(4 * D, bf16)])
    gslab = _dw_into("dw_down", act, dh2_b, None, 1024, True, R_DOWN // 1024, 4, 0)

    def up_bwd_fn(i, r, s, q, w):
        acc = _dot(r[0][:, :D], w[0][0], NT)
        for k in range(1, 4):
            acc = acc + _dot(r[0][:, D * k:D * (k + 1)], w[0][k], NT)
        dh, dg8 = _rms_bwd(acc, r[1], q[0])
        dh1_ = r[2] + dh
        return [dh1_, dh1_], [dg8], [], []
    (dh1, dh1_b), (dgmlp8,), _, _ = _rowwise("up_bwd", up_bwd_fn, nch, rows=[dm, h1, dh2], pars=[p["g_mlp"]], refs=[w_up],
                                             out_rows=[(D, f32), (D, bf16)], out_accs=[D])
    gslab = _dw_into("dw_up", n1, dm, gslab, 1024, False, R_UP // 1024, 4, 0)

    def out_bwd_fn(i, r, s, q, w):
        dmix = [_dot(r[0], w[0][k], NT) for k in range(4)]
        v1, v2 = r[1][:, :D], r[1][:, D:]
        s2 = _sigmoid(v2)
        dglu, dg8 = _rms_bwd(jnp.concatenate(dmix[2:], axis=1), v1 * s2, q[0])
        dv = jnp.concatenate([dglu * s2, dglu * v1 * s2 * (1.0 - s2)], axis=1)
        return [jnp.concatenate(dmix[:2], axis=1), dv], [dg8, _rsum8(dv)], [], []
    (dys, dv), (dgs58, dbglu8), _, _ = _rowwise("out_bwd", out_bwd_fn, nch, rows=[dh1_b, v], pars=[p["g_s5"]], refs=[w_out],
                                                out_rows=[(D, f32), (2 * D, bf16)], out_accs=[D, 2 * D])
    gslab = _dw_into("dw_out_a", y_ssd, dh1_b, gslab, 512, True, R_OUT // 512, 2, 0)
    gslab = _dw_into("dw_out_b", y_s5, dh1_b, gslab, 512, True, R_OUT // 512, 2, 2)

    def glu_bwd_fn(i, r, s, q, w):
        acc = _dot(r[0][:, :512], w[0][0])
        for k in range(1, 4):
            acc = acc + _dot(r[0][:, 512 * k:512 * (k + 1)], w[0][k])
        yl = r[1]
        cdf = 0.5 * (1.0 + lax.erf(yl * (1.0 / math.sqrt(2.0))))
        pdf = jnp.exp(-0.5 * yl * yl) * (1.0 / math.sqrt(2.0 * math.pi))
        return [acc * (cdf + yl * pdf)], [], [], []
    (dylin,), _, _, _ = _rowwise("glu_bwd", glu_bwd_fn, nch, rows=[dv, ylin], refs=[w_glu_t], out_rows=[(D, f32)])
    gslab = _dw_into("dw_glu", dv, y5, gslab, 512, True, R_GLU // 512, 4, 0)

    du, dcc, dbb, dab, dds5 = _s5_bwd(proj, dylin, s_all, jnp.swapaxes(bbq, 1, 2).astype(bf16), ccq.astype(bf16), ar, ai, d_skip, nch)
    dproj, dcw8, dcb8, ddtb8, dal8, dd8, dgssd8 = _ssd_bwd(
        proj, xc_all, y, dys, du, states, p["conv_w"], p["conv_b"], p["dt_bias"], p["a_log"], p["d_ssd"], p["g_ssd"], nch)

    def in_bwd_fn(i, r, s, q, w):
        dh, dg8 = _rms_bwd(_dot(r[0], w[0][...]), h0_of(i, s, q), q[1])
        dh0 = r[1] + dh
        return [], [dg8], [dh0], [dh0]
    _, (dgmix8,), (dmeta_blk,), (grad_x,) = _rowwise("in_bwd", in_bwd_fn, nch, rows=[dproj, dh1], shifted=[x2],
                                                      pars=[metablk, p["g_mix"]], refs=[w_full], out_accs=[D],
                                                      out_first=[(D, f32)], out_shifted=[(D, f32)])
    dw_in = _dw_in(n0, dproj)

    dab_q = jnp.swapaxes(dab.reshape(2, 4, NQ, 128), 1, 2).reshape(2, NQ * 8, S5_P)
    dbb_re, dbb_im = _blockdiag_in_grad(dbb)
    dcr, dci = _blockdiag_in_grad(dcc)
    _, vjp = jax.vjp(_s5_tables, p["lam_re"], p["lam_im"], p["log_step"], p["b_re"], p["b_im"])
    dlam_re, dlam_im, dlog_step, db_re, db_im = vjp((dab_q[0], dab_q[1], dbb_re, dbb_im))

    s8 = lambda a: jnp.sum(a, axis=0, keepdims=True)
    hsum = lambda a: jnp.sum(s8(a).reshape(HEADS, HEAD_DIM), axis=1).reshape(1, HEADS)
    small = dict(
        g_mix=s8(dgmix8), conv_w=dcw8[0:4], conv_b=s8(dcb8), dt_bias=s8(ddtb8), a_log=s8(dal8),
        d_ssd=hsum(dd8), g_ssd=s8(dgssd8), lam_re=dlam_re, lam_im=dlam_im, log_step=dlog_step, b_re=db_re, b_im=db_im,
        c_re=jnp.swapaxes(dcr, 1, 2), c_im=-jnp.swapaxes(dci, 1, 2), d_s5=s8(dds5).reshape(NQ * 8, 16),
        b_glu=s8(dbglu8), g_s5=s8(dgs58), g_mlp=s8(dgmlp8), g_final=s8(dgf8).reshape(D))
    return loss, grad_x, dmeta_blk, gslab, dw_in, small


def _perm_rows_w_in(wt):
    return jnp.concatenate([wt[0:1024], wt[2576:3600], wt[1024:2560], wt[2560:2576],
                            jnp.zeros((W_PROJ - 3600, wt.shape[1]), wt.dtype)], axis=0)


def _unperm_cols_w_in(g):
    return jnp.concatenate([g[:, 0:1024], g[:, O_XBC:O_XBC + D_XBC], g[:, O_DT:O_DT + HEADS], g[:, O_U:O_U + D]], axis=1)


def _pack_shard(w_in, w_glu, w_out, w_up, w_down, spare):
    dt = w_in.dtype
    parts = [w_up, w_down, w_glu.T, w_out, w_in.T, spare,
             jnp.zeros((PACK_ROWS - R_SPARE - spare.shape[0], D), dt)]
    return jnp.concatenate(parts, axis=0)


MESH = pl.DeviceIdType.MESH


def _place():
    return lax.axis_index("x"), lax.axis_index("y"), lax.axis_index("c")


def _allgather8(x_shard, name):
    m_per, n = x_shard.shape

    def body(x_ref, out_ref, send_sems, recv_sems, stage, local_sems):
        x, y, c = _place()
        me, sibling = (x, y, c), (x, y, 1 - c)
        chips = [(1 - x, y), (x, 1 - y), (1 - x, 1 - y)]

        def rows(px, py, pc):
            return out_ref.at[pl.ds((4 * px + 2 * py + pc) * m_per, m_per), :]

        def copy(k, block, to, src=None):
            return pltpu.make_async_remote_copy(
                src_ref=rows(*block) if src is None else src, dst_ref=rows(*block),
                send_sem=send_sems.at[k], recv_sem=recv_sems.at[k], device_id=to, device_id_type=MESH)

        load = pltpu.make_async_copy(x_ref, stage, local_sems.at[0])
        load.start()
        first = [copy(0, me, sibling, src=x_ref)]
        first += [copy(1 + j, me, (*chip, c), src=x_ref) for j, chip in enumerate(chips)]
        for cp in first:
            cp.start()
        load.wait()
        store = pltpu.make_async_copy(stage, rows(*me), local_sems.at[1])
        store.start()
        passed = [copy(4 + j, (*chip, c), sibling) for j, chip in enumerate(chips)]
        for j, chip in enumerate(chips):
            copy(1 + j, (*chip, c), me).wait_recv()
            passed[j].start()
        copy(0, sibling, me).wait_recv()
        for j, chip in enumerate(chips):
            copy(4 + j, (*chip, 1 - c), me).wait_recv()
        for cp in first + passed:
            cp.wait_send()
        store.wait()

    return pl.pallas_call(
        body, name=name, out_shape=jax.ShapeDtypeStruct((8 * m_per, n), x_shard.dtype),
        in_specs=[_ANY], out_specs=_ANY,
        scratch_shapes=[pltpu.SemaphoreType.DMA((7,)), pltpu.SemaphoreType.DMA((7,)), pltpu.VMEM((m_per, n), x_shard.dtype),
                        pltpu.SemaphoreType.DMA((2,))])(x_shard)


def _swap_halves(src, name):
    def body(src_ref, out_ref, send_sem, recv_sem):
        x, y, c = _place()
        cp = pltpu.make_async_remote_copy(src_ref=src_ref.at[:, pl.ds((1 - c) * HALF_ROWS, HALF_ROWS), :], dst_ref=out_ref,
                                          send_sem=send_sem, recv_sem=recv_sem, device_id=(x, y, 1 - c), device_id_type=MESH)
        cp.start()
        cp.wait()

    return pl.pallas_call(
        body, name=name, out_shape=jax.ShapeDtypeStruct((4, HALF_ROWS, src.shape[2]), src.dtype), in_specs=[_ANY], out_specs=_ANY,
        scratch_shapes=[pltpu.SemaphoreType.DMA, pltpu.SemaphoreType.DMA])(src)


def _exchange_chips(p, name):
    def body(p_ref, out_ref, send_sems, recv_sems):
        x, y, c = _place()
        chips = [(1 - x, y), (x, 1 - y), (1 - x, 1 - y)]
        cps = [pltpu.make_async_remote_copy(src_ref=p_ref.at[2 * cx + cy], dst_ref=out_ref.at[j], send_sem=send_sems.at[j],
                                            recv_sem=recv_sems.at[j], device_id=(cx, cy, c), device_id_type=MESH)
               for j, (cx, cy) in enumerate(chips)]
        for cp in cps:
            cp.start()
        for cp in cps:
            cp.wait()

    return pl.pallas_call(
        body, name=name, out_shape=jax.ShapeDtypeStruct((3,) + p.shape[1:], p.dtype), in_specs=[_ANY], out_specs=_ANY,
        scratch_shapes=[pltpu.SemaphoreType.DMA((3,)), pltpu.SemaphoreType.DMA((3,))])(p)


def _swap_sibling(r, name):
    def body(r_ref, out_ref, send_sem, recv_sem):
        x, y, c = _place()
        cp = pltpu.make_async_remote_copy(src_ref=r_ref, dst_ref=out_ref, send_sem=send_sem, recv_sem=recv_sem,
                                          device_id=(x, y, 1 - c), device_id_type=MESH)
        cp.start()
        cp.wait()

    return pl.pallas_call(
        body, name=name, out_shape=jax.ShapeDtypeStruct(r.shape, r.dtype), in_specs=[_ANY], out_specs=_ANY,
        scratch_shapes=[pltpu.SemaphoreType.DMA, pltpu.SemaphoreType.DMA])(r)


SH_CONVW, SH_META = 4 * 384, 16 * 256
SPARE_ROWS = 17

SMALL = [("g_mix", (1, 1024)), ("conv_b", (1, 1536)), ("dt_bias", (1, 16)), ("a_log", (1, 16)), ("d_ssd", (1, 16)),
         ("g_ssd", (1, 1024)), ("lam_re", (1, 64, 64)), ("lam_im", (1, 64, 64)), ("log_step", (1, 64)),
         ("b_re", (1, 64, 64, 16)), ("b_im", (1, 64, 64, 16)), ("c_re", (1, 64, 16, 64)), ("c_im", (1, 64, 16, 64)),
         ("d_s5", (1, 64, 16)), ("b_glu", (1, 2048)), ("g_s5", (1, 1024)), ("g_mlp", (1, 1024)), ("g_final", (1024,))]


def _pack_small(arrs, rows):
    flat = jnp.concatenate([a.reshape(-1).astype(f32) for a in arrs])
    return jnp.concatenate([flat, jnp.zeros((rows * D - flat.shape[0],), f32)]).reshape(rows, D)


def _unpack_small(slab, shapes):
    flat = slab.reshape(-1)
    out, o = [], 0
    for shp in shapes:
        n = math.prod(shp)
        out.append(flat[o:o + n].reshape(shp))
        o += n
    return out


def _sum8(g, rows):
    def body(g_ref, o_ref):
        acc = g_ref[0]
        for k in range(1, 8):
            acc = acc + g_ref[k]
        o_ref[...] = acc
    return pl.pallas_call(body, name="sum8", out_shape=jax.ShapeDtypeStruct((rows, D), f32),
                          compiler_params=_cp())(g.reshape(8, rows, D))


def _adam_math(w_, g_, m_, v_):
    m2 = ADAM_B1 * m_ + (1.0 - ADAM_B1) * g_
    v2 = ADAM_B2 * v_ + (1.0 - ADAM_B2) * jnp.square(g_)
    m_hat = m2 / (1.0 - ADAM_B1 ** ADAM_STEP)
    v_hat = v2 / (1.0 - ADAM_B2 ** ADAM_STEP)
    delta = -ADAM_LR * (m_hat / (jnp.sqrt(v_hat) + ADAM_EPS) + ADAM_WD * w_)
    return delta, m2, v2


def _adamw(name, w, g, m, v, bm):
    def fn(i, r, s, q, refs):
        return list(_adam_math(*r)), [], [], []
    c = w.shape[1]
    (d, m2, v2), _, _, _ = _rowwise(name, fn, w.shape[0] // bm, rows=[w, g, m, v], out_rows=[(c, f32)] * 3, bm=bm)
    return d, m2, v2


def _adamw_whole(name, w, g, m, v):
    def body(w_ref, g_ref, m_ref, v_ref, d_ref, m2_ref, v2_ref):
        d_ref[...], m2_ref[...], v2_ref[...] = _adam_math(w_ref[...], g_ref[...], m_ref[...], v_ref[...])
    return pl.pallas_call(body, name=name, out_shape=[jax.ShapeDtypeStruct(w.shape, f32)] * 3, compiler_params=_cp())(w, g, m, v)


def _add_rows(name, parts, out_dtype, rows, bm):
    def fn(i, r, s, q, refs):
        acc = r[0].astype(f32)
        for t in r[1:]:
            acc = acc + t.astype(f32)
        return [acc], [], [], []
    (o,), _, _, _ = _rowwise(name, fn, rows // bm, rows=parts, out_rows=[(D, out_dtype)], bm=bm)
    return o


def kernel(x, meta_tokens, g_mix, w_in, conv_w, conv_b, dt_bias, a_log, d_ssd, g_ssd, lam_re, lam_im, log_step, b_re, b_im, c_re, c_im, d_s5, w_glu, b_glu, g_s5, w_out, g_mlp, w_up, w_down, g_final, loss_target, m_meta_tokens, m_g_mix, m_w_in, m_conv_w, m_conv_b, m_dt_bias, m_a_log, m_d_ssd, m_g_ssd, m_lam_re, m_lam_im, m_log_step, m_b_re, m_b_im, m_c_re, m_c_im, m_d_s5, m_w_glu, m_b_glu, m_g_s5, m_w_out, m_g_mlp, m_w_up, m_w_down, m_g_final, v_meta_tokens, v_g_mix, v_w_in, v_conv_w, v_conv_b, v_dt_bias, v_a_log, v_d_ssd, v_g_ssd, v_lam_re, v_lam_im, v_log_step, v_b_re, v_b_im, v_c_re, v_c_im, v_d_s5, v_w_glu, v_b_glu, v_g_s5, v_w_out, v_g_mlp, v_w_up, v_w_down, v_g_final):
    given = dict(locals())
    cx, cy, cc = _place()
    chip = 2 * cx + cy
    me = 2 * chip + cc

    small_f = jnp.concatenate([conv_w.reshape(-1), meta_tokens.reshape(-1)])
    t_hi = small_f.astype(bf16)
    r_1 = small_f - t_hi.astype(f32)
    t_mid = r_1.astype(bf16)
    t_lo = (r_1 - t_mid.astype(f32)).astype(bf16)
    terms = jnp.concatenate([t_hi, t_mid, t_lo])
    spare = jnp.concatenate([terms, jnp.zeros((SPARE_ROWS * D - terms.shape[0],), bf16)]).reshape(SPARE_ROWS, D)
    slab = _pack_shard(w_in[0].astype(bf16), w_glu[0].astype(bf16), w_out[0].astype(bf16), w_up[0].astype(bf16),
                       w_down[0].astype(bf16), spare)
    my_half = lax.dynamic_slice_in_dim(slab, cc * HALF_ROWS, HALF_ROWS, axis=0)
    gathered = _allgather8(my_half, "gather_weights").reshape(4, PACK_ROWS, D)
    w_in_t = _perm_rows_w_in(jnp.concatenate([gathered[s, R_IN:R_SPARE] for s in range(4)], axis=0))
    n_sf = SH_CONVW + SH_META
    tr = gathered[:, R_SPARE:R_SPARE + SPARE_ROWS].reshape(4, SPARE_ROWS * D)[:, :3 * n_sf].astype(f32).reshape(4, 3, n_sf)
    sp = tr[:, 0] + tr[:, 1] + tr[:, 2]
    conv_w_full = jnp.concatenate([sp[s, :SH_CONVW].reshape(4, 384) for s in range(4)], axis=1)
    meta_full = jnp.concatenate([sp[s, SH_CONVW:].reshape(16, 256) for s in range(4)], axis=1)

    p = dict(g_mix=g_mix, conv_w=conv_w_full, conv_b=conv_b, dt_bias=dt_bias, a_log=a_log, d_ssd=d_ssd, g_ssd=g_ssd,
             lam_re=lam_re[0], lam_im=lam_im[0], log_step=log_step[0], b_re=b_re[0], b_im=b_im[0], c_re=c_re[0], c_im=c_im[0],
             d_s5=d_s5[0], b_glu=b_glu, g_s5=g_s5, g_mlp=g_mlp, g_final=g_final.reshape(1, D))
    loss_part, grad_x, dmeta_blk, gslab, dw_in, g = _local_step(x[0], loss_target[0], meta_full, p, w_in_t, gathered)
    grad_x = grad_x.reshape(x.shape)

    gin_t = _unperm_cols_w_in(dw_in).T
    gslab = lax.dynamic_update_slice(gslab, gin_t.reshape(4, 900, D), (0, R_IN, 0))
    from_sib = _swap_halves(gslab, "rs_pair")
    mine = lax.dynamic_slice_in_dim(gslab, cc * HALF_ROWS, HALF_ROWS, axis=1)
    pair = _add_rows("rs_pair_add", [mine.reshape(4 * HALF_ROWS, D), from_sib.reshape(4 * HALF_ROWS, D)], bf16,
                     4 * HALF_ROWS, 512).reshape(4, HALF_ROWS, D)
    from_chips = _exchange_chips(pair, "rs_chips")
    own = lax.dynamic_index_in_dim(pair, chip, axis=0, keepdims=False)
    red = _add_rows("rs_chip_add", [own, from_chips[0], from_chips[1], from_chips[2]], f32, HALF_ROWS, 512)
    other = _swap_sibling(red, "rs_share")
    lower = jnp.where(cc == 0, red, other)
    upper = jnp.where(cc == 0, other, red)
    g_up, g_down = lower[R_UP:R_UP + 1024], lower[R_DOWN:R_DOWN + 1024]
    g_glu, g_out = upper[R_GLU - HALF_ROWS:R_OUT - HALF_ROWS].T, upper[R_OUT - HALF_ROWS:R_IN - HALF_ROWS]
    g_in = upper[R_IN - HALF_ROWS:R_SPARE - HALF_ROWS].T

    small_g = [g[n] for n, _ in SMALL] + [g["conv_w"], dmeta_blk[CH - N_META:CH], loss_part.reshape(1)]
    n_small = sum(math.prod(s) for _, s in SMALL) + 4 * D_XBC + N_META * D + 1
    rows_small = -(-n_small // (8 * D)) * 8
    total = _sum8(_allgather8(_pack_small(small_g, rows_small), "gather_small"), rows_small)
    outs = _unpack_small(total, [s for _, s in SMALL] + [(4, D_XBC), (N_META, D), ()])
    gs = {n: o for (n, _), o in zip(SMALL, outs)}
    g_conv_w = lax.dynamic_slice_in_dim(outs[-3], chip * 384, 384, axis=1).reshape(conv_w.shape)
    g_meta = lax.dynamic_slice_in_dim(outs[-2], chip * 256, 256, axis=1)
    loss = outs[-1]

    grads = dict(gs, meta_tokens=g_meta, conv_w=g_conv_w, w_in=g_in.reshape(w_in.shape), w_glu=g_glu.reshape(w_glu.shape),
                 w_out=g_out.reshape(w_out.shape), w_up=g_up.reshape(w_up.shape), w_down=g_down.reshape(w_down.shape))
    delta, new_m, new_v = {}, {}, {}
    for n in ("w_in", "w_glu", "w_out", "w_up", "w_down"):
        shp = given[n].shape
        two = lambda a: a.reshape(shp[1], shp[2])
        d_, m_, v_ = _adamw("adamw_" + n, two(given[n]), two(grads[n]), two(given["m_" + n]), two(given["v_" + n]), 256)
        delta[n], new_m[n], new_v[n] = d_.reshape(shp), m_.reshape(shp), v_.reshape(shp)
    for n in [n for n, _ in SMALL] + ["conv_w", "meta_tokens"]:
        shp = given[n].shape
        two = (lambda a: a.reshape(1, -1)) if len(shp) == 1 else (lambda a: a)
        d_, m_, v_ = _adamw_whole("adamw_" + n, two(given[n]), two(grads[n].reshape(shp)), two(given["m_" + n]), two(given["v_" + n]))
        delta[n], new_m[n], new_v[n] = d_.reshape(shp), m_.reshape(shp), v_.reshape(shp)

    order = ["meta_tokens", "g_mix", "w_in", "conv_w", "conv_b", "dt_bias", "a_log", "d_ssd", "g_ssd", "lam_re", "lam_im", "log_step",
             "b_re", "b_im", "c_re", "c_im", "d_s5", "w_glu", "b_glu", "g_s5", "w_out", "g_mlp", "w_up", "w_down", "g_final"]
    grads_out = [grads[n].reshape(given[n].shape) for n in order]
    return (loss, grad_x, *grads_out, *[delta[n] for n in order], *[new_m[n] for n in order], *[new_v[n] for n in order])
```

```python
import math

import jax
import jax.numpy as jnp
from jax import lax
from jax.experimental import pallas as pl
from jax.experimental.pallas import tpu as pltpu

f32 = jnp.float32
bf16 = jnp.bfloat16

D = 1024
N_META = 16
CH = 256
HEADS = 16
HEAD_DIM = 64
NSTATE = 128
D_XBC = 1536
S5_P = 64
NQ = 8
PITCH = CH + 4
EPS = 1e-5
O_Z, O_U, O_XBC, O_DT, W_PROJ = 0, 1024, 2048, 3584, 3712
VMEM_LIMIT = 60 * 1024 * 1024

ADAM_LR, ADAM_B1, ADAM_B2, ADAM_EPS, ADAM_WD, ADAM_STEP = 0.001, 0.9, 0.999, 1e-08, 0.01, 10

NT = (((1,), (1,)), ((), ()))
TN = (((0,), (0,)), ((), ()))
_ANY = pl.BlockSpec(memory_space=pl.ANY)


def _cp(sem=None):
    return pltpu.CompilerParams(dimension_semantics=sem, vmem_limit_bytes=VMEM_LIMIT)


def _sigmoid(v):
    return 1.0 / (1.0 + jnp.exp(-v))


def _rsum8(v):
    r, c = v.shape
    return jnp.sum(v.reshape(r // 8, 8, c), axis=0)


def _rms(h, g):
    r = lax.rsqrt(jnp.mean(h * h, axis=-1, keepdims=True) + EPS)
    return h * r * g


def _rms_bwd(dy, h, g):
    r = lax.rsqrt(jnp.mean(h * h, axis=-1, keepdims=True) + EPS)
    n = h * r
    dn = dy * g
    dh = r * (dn - n * jnp.mean(dn * n, axis=-1, keepdims=True))
    return dh, _rsum8(dy * n)


def _dot(a, b, dims=None):
    if dims is None:
        return jnp.dot(a, b, preferred_element_type=f32)
    return lax.dot_general(a, b, dims, preferred_element_type=f32)


def _split_dot(v, m01, dims, terms, v_is_lhs=True):
    out, r = None, v
    for _ in range(terms):
        piece = r.astype(bf16)
        o = _dot(piece, m01, dims) if v_is_lhs else _dot(m01, piece, dims)
        out = o if out is None else out + o
        r = r - piece.astype(f32)
    return out


MESH = pl.DeviceIdType.MESH


def _place():
    return lax.axis_index("x"), lax.axis_index("y"), lax.axis_index("c")


def _flip(v, f):
    return 1 - v if f else v


def _call(body, comm, *, name, nsteps, in_specs, out_specs, out_shape, scratch_shapes, args):
    n_in, n_out, n_scr = len(in_specs), len(out_specs), len(scratch_shapes)
    if comm is None:
        res = pl.pallas_call(body, name=name, grid=(nsteps,), in_specs=in_specs, out_specs=out_specs, out_shape=out_shape,
                             scratch_shapes=scratch_shapes, compiler_params=_cp(("arbitrary",)))(*args)
        return list(res), []
    c_in, c_out = len(comm["ins"]), len(comm["outs"])

    def wrapped(*refs):
        o0 = n_in + c_in
        s0 = o0 + n_out + c_out
        cparts = (refs[n_in:o0], refs[o0 + n_out:s0], refs[s0 + n_scr:])

        @pl.when(pl.program_id(0) == 0)
        def _():
            comm["start"](*cparts)
        body(*refs[:n_in], *refs[o0:o0 + n_out], *refs[s0:s0 + n_scr])

        @pl.when(pl.program_id(0) == nsteps - 1)
        def _():
            comm["finish"](*cparts)

    any_spec = pl.BlockSpec(memory_space=pl.ANY)
    res = pl.pallas_call(
        wrapped, name=name, grid=(nsteps,), in_specs=list(in_specs) + [any_spec] * c_in,
        out_specs=list(out_specs) + [any_spec] * c_out, out_shape=list(out_shape) + list(comm["outs"]),
        scratch_shapes=list(scratch_shapes) + list(comm["scratch"]),
        compiler_params=_cp(("arbitrary",)))(*args, *comm["ins"])
    return list(res[:n_out]), list(res[n_out:])


def _gather_piece(slab, r0, rows):
    half = rows // 2

    def copies(slab_ref, out_ref, send_sems, recv_sems):
        x, y, c = _place()
        cps = []
        for fx, fy in ((1, 0), (0, 1), (1, 1)):
            for fc in (0, 1):
                k = (2 * fx + fy - 1) * 2 + fc
                cps.append(pltpu.make_async_remote_copy(
                    src_ref=slab_ref.at[pl.ds(r0 + c * half, half), :], dst_ref=out_ref.at[2 * x + y, pl.ds(c * half, half), :],
                    send_sem=send_sems.at[k], recv_sem=recv_sems.at[k],
                    device_id=(_flip(x, fx), _flip(y, fy), _flip(c, fc)), device_id_type=MESH))
        return cps

    def start(ins, outs, scr):
        send_sems, recv_sems, stage, local_sems = scr
        x, y, _ = _place()
        load = pltpu.make_async_copy(ins[0].at[pl.ds(r0, rows), :], stage, local_sems.at[0])
        load.start()
        for cp in copies(ins[0], outs[0], send_sems, recv_sems):
            cp.start()
        load.wait()
        pltpu.make_async_copy(stage, outs[0].at[2 * x + y], local_sems.at[1]).start()

    def finish(ins, outs, scr):
        send_sems, recv_sems, stage, local_sems = scr
        x, y, _ = _place()
        for cp in copies(ins[0], outs[0], send_sems, recv_sems):
            cp.wait()
        pltpu.make_async_copy(stage, outs[0].at[2 * x + y], local_sems.at[1]).wait()

    return dict(ins=[slab], outs=[jax.ShapeDtypeStruct((4, rows, D), bf16)],
                scratch=[pltpu.SemaphoreType.DMA((6,)), pltpu.SemaphoreType.DMA((6,)), pltpu.VMEM((rows, D), bf16),
                         pltpu.SemaphoreType.DMA((2,))], start=start, finish=finish)


def _scatter_piece(gpiece):
    half = gpiece.shape[1] // 2

    def copies(g_ref, land_ref, send_sems, recv_sems):
        x, y, c = _place()
        cps = []
        for fx in (0, 1):
            for fy in (0, 1):
                for fc in (0, 1):
                    k = 4 * fx + 2 * fy + fc - 1
                    if k < 0:
                        continue
                    px, py, pc = _flip(x, fx), _flip(y, fy), _flip(c, fc)
                    cps.append(pltpu.make_async_remote_copy(
                        src_ref=g_ref.at[2 * px + py, pl.ds(pc * half, half), :], dst_ref=land_ref.at[k],
                        send_sem=send_sems.at[k], recv_sem=recv_sems.at[k], device_id=(px, py, pc), device_id_type=MESH))
        return cps

    def start(ins, outs, scr):
        for cp in copies(ins[0], outs[0], *scr):
            cp.start()

    def finish(ins, outs, scr):
        for cp in copies(ins[0], outs[0], *scr):
            cp.wait()

    return dict(ins=[gpiece], outs=[jax.ShapeDtypeStruct((7, half, D), gpiece.dtype)],
                scratch=[pltpu.SemaphoreType.DMA((7,)), pltpu.SemaphoreType.DMA((7,))], start=start, finish=finish)


def _rowwise(name, fn, nblk, rows=(), shifted=(), pars=(), refs=(), out_rows=(), out_accs=(), out_first=(), out_shifted=(), bm=CH,
             comm=None):
    n_r, n_s, n_p, n_w = len(rows), len(shifted), len(pars), len(refs)
    n_in = n_r + n_s + n_p + n_w
    n_o, n_a, n_f, n_so = len(out_rows), len(out_accs), len(out_first), len(out_shifted)

    def body(*all_refs):
        i = pl.program_id(0)
        ins = all_refs[:n_in]
        outs = all_refs[n_in:]
        rv = [r[...] for r in ins[:n_r]]
        sv = [r[...] for r in ins[n_r:n_r + n_s]]
        pv = [r[...] for r in ins[n_r + n_s:n_r + n_s + n_p]]
        ro, ao, fo, so = fn(i, rv, sv, pv, list(ins[n_r + n_s + n_p:]))
        for r, v in zip(outs[:n_o], ro):
            r[...] = v.astype(r.dtype)
        accs = outs[n_o:n_o + n_a]

        @pl.when(i == 0)
        def _():
            for r in accs:
                r[...] = jnp.zeros_like(r)
            for r, v in zip(outs[n_o + n_a:n_o + n_a + n_f], fo):
                r[...] = v.astype(r.dtype)
        for r, v in zip(accs, ao):
            r[...] += v
        for r, v in zip(outs[n_o + n_a + n_f:], so):
            r[...] = v.astype(r.dtype)

    prev = lambda i: (jnp.maximum(i - 1, 0), 0)
    in_specs = [pl.BlockSpec((bm, a.shape[1]), lambda i: (i, 0)) for a in rows]
    in_specs += [pl.BlockSpec((bm, a.shape[1]), prev) for a in shifted]
    in_specs += [pl.BlockSpec(a.shape, lambda i, nd=a.ndim: (0,) * nd) for a in pars]
    in_specs += [spec for _, spec in refs]
    out_specs = [pl.BlockSpec((bm, c), lambda i: (i, 0)) for c, _ in out_rows]
    out_specs += [pl.BlockSpec((8, c), lambda i: (0, 0)) for c in out_accs]
    out_specs += [pl.BlockSpec((bm, c), lambda i: (0, 0)) for c, _ in out_first]
    out_specs += [pl.BlockSpec((bm, c), prev) for c, _ in out_shifted]
    out_shape = [jax.ShapeDtypeStruct((nblk * bm, c), dt) for c, dt in out_rows]
    out_shape += [jax.ShapeDtypeStruct((8, c), f32) for c in out_accs]
    out_shape += [jax.ShapeDtypeStruct((bm, c), dt) for c, dt in out_first]
    out_shape += [jax.ShapeDtypeStruct(((nblk - 1) * bm, c), dt) for c, dt in out_shifted]
    res, cres = _call(body, comm, name=name, nsteps=nblk, in_specs=in_specs, out_specs=out_specs, out_shape=out_shape,
                      scratch_shapes=[], args=[*rows, *shifted, *pars, *[a for a, _ in refs]])
    parts = (res[:n_o], res[n_o:n_o + n_a], res[n_o + n_a:n_o + n_a + n_f], res[n_o + n_a + n_f:])
    return parts if comm is None else parts + (cres,)


PACK_ROWS = 4096
HALF_ROWS = PACK_ROWS // 2
R_UP, R_DOWN, R_GLU, R_OUT, R_IN, R_SPARE = 0, 1024, 2048, 2560, 3072, 3972


def _contract_rows(lp):
    return lp // 8 if lp % (8 * 16) == 0 else CH


def _dw_into(name, a, b, slab, ka, a_sharded, row_blk, n_s, s0, piece_rows=2048):
    lp = a.shape[0]
    bm = _contract_rows(lp)
    steps = lp // bm

    def body(a_ref, b_ref, *rest):
        o_ref, acc = rest[-2], rest[-1]
        k = pl.program_id(1)

        @pl.when(k == 0)
        def _():
            acc[...] = jnp.zeros_like(acc)
        acc[...] += _dot(a_ref[...], b_ref[...], TN)

        @pl.when(k == steps - 1)
        def _():
            o_ref[0] = acc[...].astype(bf16)

    in_specs = [pl.BlockSpec((bm, ka), (lambda s, k: (k, s)) if a_sharded else (lambda s, k: (k, 0))),
                pl.BlockSpec((bm, D), (lambda s, k: (k, 0)) if a_sharded else (lambda s, k: (k, s)))]
    args = [a, b]
    aliases = {}
    if slab is not None:
        in_specs.append(_ANY)
        args.append(slab)
        aliases = {2: 0}
    return pl.pallas_call(
        body, name=name, grid=(n_s, steps), in_specs=in_specs,
        out_specs=pl.BlockSpec((1, ka, D), lambda s, k: (s0 + s, row_blk, 0)),
        out_shape=jax.ShapeDtypeStruct((4, piece_rows, D), bf16),
        scratch_shapes=[pltpu.VMEM((ka, D), f32)], input_output_aliases=aliases,
        compiler_params=_cp(("arbitrary", "arbitrary")))(*args)


def _dw_in_t(dproj, n0):
    lp = n0.shape[0]
    bm = _contract_rows(lp) // 2 if _contract_rows(lp) % 32 == 0 else CH
    steps = lp // bm
    bn = 512

    def body(a_ref, b_ref, o_ref, acc):
        k = pl.program_id(1)

        @pl.when(k == 0)
        def _():
            acc[...] = jnp.zeros_like(acc)
        acc[...] += _dot(a_ref[...], b_ref[...], TN)

        @pl.when(k == steps - 1)
        def _():
            o_ref[...] = acc[...].astype(bf16)

    return pl.pallas_call(
        body, name="dw_in", grid=(D // bn, steps),
        in_specs=[pl.BlockSpec((bm, W_PROJ), lambda j, k: (k, 0)), pl.BlockSpec((bm, bn), lambda j, k: (k, j))],
        out_specs=pl.BlockSpec((W_PROJ, bn), lambda j, k: (0, j)),
        out_shape=jax.ShapeDtypeStruct((W_PROJ, D), bf16),
        scratch_shapes=[pltpu.VMEM((W_PROJ, bn), f32)],
        compiler_params=_cp(("arbitrary", "arbitrary")))(dproj, n0)


def _head_expand():
    h = lax.broadcasted_iota(jnp.int32, (HEADS, D), 0)
    c = lax.broadcasted_iota(jnp.int32, (HEADS, D), 1)
    return jnp.where((c >> 6) == h, 1.0, 0.0).astype(bf16)


def _ssd_common(i, P, prev8, cw, cb, dtb, alog, xc=None):
    z = P[:, O_Z:O_Z + D]
    xp = P[:, O_XBC:O_XBC + D_XBC]
    dt_raw = P[:, O_DT:O_DT + HEADS]
    row = lax.broadcasted_iota(jnp.int32, (CH, 1), 0)
    if xc is None:
        row8 = lax.broadcasted_iota(jnp.int32, (8, 1), 0)
        xc = cb + cw[3:4] * xp
        for k in (1, 2, 3):
            rolled = pltpu.roll(xp, k, 0)
            fix = pltpu.roll(prev8, k, 0)
            top = jnp.where(row8 < k, fix, rolled[0:8])
            xc = xc + cw[3 - k:4 - k] * jnp.concatenate([top, rolled[8:]], axis=0)
    sg = _sigmoid(xc)
    xbc = xc * sg
    live = jnp.where(jnp.logical_or(i > 0, row >= CH - N_META), 1.0, 0.0)
    pre = dt_raw + dtb
    dt = jnp.where(pre > 20.0, pre, jnp.log(1.0 + jnp.exp(jnp.minimum(pre, 20.0)))) * live
    a = -jnp.exp(alog)
    dta = dt * a
    r_i = lax.broadcasted_iota(jnp.int32, (CH, CH), 0)
    c_i = lax.broadcasted_iota(jnp.int32, (CH, CH), 1)
    tril = r_i >= c_i
    acs = _split_dot(dta, jnp.where(tril, 1.0, 0.0).astype(bf16), None, 3, v_is_lhs=False)
    acs_t = _split_dot(dta, jnp.where(r_i <= c_i, 1.0, 0.0).astype(bf16), TN, 3)
    e = _head_expand()
    acs_e = _split_dot(acs, e, None, 3)
    dt_e = _split_dot(dt, e, None, 3)
    return dict(z=z, xp=xp, xc=xc, sg=sg, xbc=xbc, live=live, pre=pre, dt=dt, a=a, tril=tril,
                acs=acs, acs_t=acs_t, e=e, acs_e=acs_e, dt_e=dt_e)


def _lmat(c, h):
    seg = c["acs"][:, h:h + 1] - c["acs_t"][h:h + 1, :]
    return jnp.where(c["tril"], jnp.exp(jnp.minimum(seg, 0.0)), 0.0)


def _pair_masks():
    lane = lax.broadcasted_iota(jnp.int32, (1, 128), 1)
    return jnp.where(lane < HEAD_DIM, 1.0, 0.0), jnp.where(lane >= HEAD_DIM, 1.0, 0.0)


def _ssd_fwd(proj, conv_w, conv_b, dt_bias, a_log, d_ssd, g_ssd, nch, comm=None):
    def body(p_ref, cw_ref, cb_ref, dtb_ref, al_ref, d_ref, g_ref, y_ref, ys_ref, st_ref, xc_ref, prev8_ref, state_ref):
        i = pl.program_id(0)

        @pl.when(i == 0)
        def _():
            prev8_ref[...] = jnp.zeros_like(prev8_ref)
            state_ref[...] = jnp.zeros_like(state_ref)

        P = p_ref[...]
        c = _ssd_common(i, P, prev8_ref[...], cw_ref[...], cb_ref[...], dtb_ref[...], al_ref[...])
        prev8_ref[...] = c["xp"][CH - 8:CH]
        xc_ref[...] = c["xc"]
        xbc = c["xbc"]
        x = xbc[:, 0:D]
        xdt = x * c["dt_e"]
        a_last_e = c["acs_e"][CH - 1:CH, :]
        w_end = (xdt * jnp.exp(a_last_e - c["acs_e"])).astype(bf16)
        m0, m1 = _pair_masks()
        ys = []
        for g in range(2):
            bg = xbc[:, D + NSTATE * g:D + NSTATE * (g + 1)].astype(bf16)
            cg = xbc[:, D + 2 * NSTATE + NSTATE * g:D + 2 * NSTATE + NSTATE * (g + 1)].astype(bf16)
            gmat = _dot(cg, bg, NT)
            st = state_ref[g]
            st_ref[0, g] = st
            sl = slice(512 * g, 512 * (g + 1))
            y_off = _dot(cg, st.astype(bf16)) * jnp.exp(c["acs_e"][:, sl])
            contrib = _dot(bg, w_end[:, sl], TN)
            state_ref[g] = st * jnp.exp(a_last_e[:, sl]) + contrib
            yd = []
            for pr in range(4):
                h0 = 8 * g + 2 * pr
                xp2 = xdt[:, 128 * (4 * g + pr):128 * (4 * g + pr + 1)]
                ma = (gmat * _lmat(c, h0)).astype(bf16)
                mb = (gmat * _lmat(c, h0 + 1)).astype(bf16)
                yd.append(_dot(ma, (xp2 * m0).astype(bf16)) + _dot(mb, (xp2 * m1).astype(bf16)))
            ys.append(jnp.concatenate(yd, axis=1) + y_off)
        d_e = _split_dot(d_ref[...], c["e"], None, 3)
        y = jnp.concatenate(ys, axis=1) + x * d_e
        y_ref[...] = y
        yg = y * (c["z"] * _sigmoid(c["z"]))
        ys_ref[...] = _rms(yg, g_ref[...]).astype(bf16)

    full = lambda a: pl.BlockSpec(a.shape, lambda i, nd=a.ndim: (0,) * nd)
    return _call(
        body, comm, name="ssd_fwd", nsteps=nch,
        in_specs=[pl.BlockSpec((CH, W_PROJ), lambda i: (i, 0))] + [full(a) for a in (conv_w, conv_b, dt_bias, a_log, d_ssd, g_ssd)],
        out_specs=[pl.BlockSpec((CH, D), lambda i: (i, 0)), pl.BlockSpec((CH, D), lambda i: (i, 0)),
                   pl.BlockSpec((1, 2, NSTATE, 512), lambda i: (i, 0, 0, 0)), pl.BlockSpec((CH, D_XBC), lambda i: (i, 0))],
        out_shape=[jax.ShapeDtypeStruct((nch * CH, D), f32), jax.ShapeDtypeStruct((nch * CH, D), bf16),
                   jax.ShapeDtypeStruct((nch, 2, NSTATE, 512), f32), jax.ShapeDtypeStruct((nch * CH, D_XBC), f32)],
        scratch_shapes=[pltpu.VMEM((8, D_XBC), f32), pltpu.VMEM((2, NSTATE, 512), f32)],
        args=[proj, conv_w, conv_b, dt_bias, a_log, d_ssd, g_ssd])


def _ssd_bwd(proj, xc_all, y, dys, du, states, conv_w, conv_b, dt_bias, a_log, d_ssd, g_ssd, nch, comm=None):
    def body(p_ref, xc_ref, y_ref, dys_ref, du_ref, st_ref, cw_ref, cb_ref, dtb_ref, al_ref, d_ref, g_ref,
             dp_ref, dcw_ref, dcb_ref, ddtb_ref, dal_ref, dd_ref, dg_ref, nxt8_ref, dst_ref):
        step = pl.program_id(0)
        i = nch - 1 - step

        @pl.when(step == 0)
        def _():
            nxt8_ref[...] = jnp.zeros_like(nxt8_ref)
            dst_ref[...] = jnp.zeros_like(dst_ref)
            for r in (dcw_ref, dcb_ref, ddtb_ref, dal_ref, dd_ref, dg_ref):
                r[...] = jnp.zeros_like(r)

        P = p_ref[...]
        c = _ssd_common(i, P, None, cw_ref[...], cb_ref[...], dtb_ref[...], al_ref[...], xc=xc_ref[...])
        xbc, z, e = c["xbc"], c["z"], c["e"]
        x = xbc[:, 0:D]
        yv = y_ref[...]
        sz = _sigmoid(z)
        silu_z = z * sz
        dyg, dg8 = _rms_bwd(dys_ref[...], yv * silu_z, g_ref[...])
        dg_ref[...] += dg8
        dy = dyg * silu_z
        dz = dyg * yv * (sz * (1.0 + z * (1.0 - sz)))
        d_e = _split_dot(d_ref[...], e, None, 3)
        dd_ref[...] += _rsum8(dy * x)
        xdt = x * c["dt_e"]
        a_last_e = c["acs_e"][CH - 1:CH, :]
        e_end = jnp.exp(a_last_e - c["acs_e"])
        w_end = xdt * e_end
        e_acs = jnp.exp(c["acs_e"])
        dy_dec = dy * e_acs
        m0, m1 = _pair_masks()
        lane16 = lax.broadcasted_iota(jnp.int32, (1, HEADS), 1)
        row16 = lax.broadcasted_iota(jnp.int32, (HEADS, 1), 0)
        dacs = jnp.zeros((CH, HEADS), f32)
        dacs_t = jnp.zeros((HEADS, CH), f32)
        dxdt_parts, dbs, dcs, zparts, yoff_parts, dlast_parts = [], [], [], [], [], []
        for g in range(2):
            sl = slice(512 * g, 512 * (g + 1))
            bg = xbc[:, D + NSTATE * g:D + NSTATE * (g + 1)].astype(bf16)
            cg = xbc[:, D + 2 * NSTATE + NSTATE * g:D + 2 * NSTATE + NSTATE * (g + 1)].astype(bf16)
            gmat = _dot(cg, bg, NT)
            st = st_ref[0, g]
            dstn = dst_ref[g]
            dstn_b = dstn.astype(bf16)
            y_off = _dot(cg, st.astype(bf16)) * e_acs[:, sl]
            yoff_parts.append(y_off)
            bds = _dot(bg, dstn_b)
            zparts.append(w_end[:, sl] * bds)
            dlast_parts.append(jnp.sum(dstn * st, axis=0, keepdims=True) * jnp.exp(a_last_e[:, sl]))
            dg_acc = jnp.zeros((CH, CH), f32)
            dxd = []
            for pr in range(4):
                lo = 128 * (4 * g + pr)
                xp2 = xdt[:, lo:lo + 128].astype(bf16)
                dy2 = dy[:, lo:lo + 128]
                outp = jnp.zeros((CH, 128), f32)
                for hh, msk in ((0, m0), (1, m1)):
                    h = 8 * g + 2 * pr + hh
                    lm = _lmat(c, h)
                    dyh = (dy2 * msk).astype(bf16)
                    mh = (gmat * lm).astype(bf16)
                    outp = outp + _dot(mh, dyh, TN)
                    dml = _dot(dyh, xp2, NT) * lm
                    dg_acc = dg_acc + dml
                    q = dml * gmat
                    dacs = dacs + jnp.where(lane16 == h, jnp.sum(q, axis=1, keepdims=True), 0.0)
                    dacs_t = dacs_t + jnp.where(row16 == h, jnp.sum(q, axis=0, keepdims=True), 0.0)
                dxd.append(outp)
            dxdt_parts.append(jnp.concatenate(dxd, axis=1) + e_end[:, sl] * bds)
            dgb = dg_acc.astype(bf16)
            dcs.append(_dot(dgb, bg) + _dot(dy_dec[:, sl].astype(bf16), st.astype(bf16), NT))
            dbs.append(_dot(dgb, cg, TN) + _dot(w_end[:, sl].astype(bf16), dstn_b, NT))
            dst_ref[g] = dstn * jnp.exp(a_last_e[:, sl]) + _dot(cg, dy_dec[:, sl].astype(bf16), TN)
        dxdt = jnp.concatenate(dxdt_parts, axis=1)
        zfull = jnp.concatenate(zparts, axis=1)
        y_off_full = jnp.concatenate(yoff_parts, axis=1)
        dlast = jnp.concatenate(dlast_parts, axis=1)
        red = lambda v: _split_dot(v, e, NT, 2)
        eye16 = jnp.where(lax.broadcasted_iota(jnp.int32, (HEADS, HEADS), 0) == lax.broadcasted_iota(jnp.int32, (HEADS, HEADS), 1),
                          1.0, 0.0).astype(bf16)
        dacs = dacs - _split_dot(dacs_t, eye16, TN, 3)
        zred = red(zfull)
        dacs = dacs + red(dy * y_off_full) - zred
        last_term = jnp.sum(zred, axis=0, keepdims=True) + red(dlast)
        rowc = lax.broadcasted_iota(jnp.int32, (CH, 1), 0)
        dacs = dacs + jnp.where(rowc == CH - 1, last_term, 0.0)
        r_i = lax.broadcasted_iota(jnp.int32, (CH, CH), 0)
        c_i = lax.broadcasted_iota(jnp.int32, (CH, CH), 1)
        ddta = _split_dot(dacs, jnp.where(c_i >= r_i, 1.0, 0.0).astype(bf16), None, 3, v_is_lhs=False)
        ddt = ddta * c["a"] + red(dxdt * x)
        dal_ref[...] += _rsum8(ddta * c["dt"] * c["a"])
        ddt_raw = ddt * _sigmoid(c["pre"]) * c["live"]
        ddtb_ref[...] += _rsum8(ddt_raw)
        dx = dy * d_e + dxdt * c["dt_e"]
        dxbc = jnp.concatenate([dx, dbs[0], dbs[1], dcs[0], dcs[1]], axis=1)
        sg = c["sg"]
        dxc = dxbc * (sg * (1.0 + c["xc"] * (1.0 - sg)))
        dcb_ref[...] += _rsum8(dxc)
        xp = c["xp"]
        row8 = lax.broadcasted_iota(jnp.int32, (8, 1), 0)
        cw = cw_ref[...]
        dxp = cw[3:4] * dxc
        dcw = jnp.where(row8 == 3, jnp.sum(dxc * xp, axis=0, keepdims=True), 0.0)
        nxt8 = nxt8_ref[...]
        for j in (1, 2, 3):
            rolled = pltpu.roll(dxc, CH - j, 0)
            fix = pltpu.roll(nxt8, 8 - j, 0)
            bot = jnp.where(row8 >= 8 - j, fix, rolled[CH - 8:CH])
            later = jnp.concatenate([rolled[:CH - 8], bot], axis=0)
            dxp = dxp + cw[3 - j:4 - j] * later
            dcw = dcw + jnp.where(row8 == 3 - j, jnp.sum(later * xp, axis=0, keepdims=True), 0.0)
        dcw_ref[...] += dcw
        nxt8_ref[...] = dxc[0:8]
        dp_ref[:, O_Z:O_Z + D] = dz.astype(bf16)
        dp_ref[:, O_U:O_U + D] = du_ref[...].astype(bf16)
        dp_ref[:, O_XBC:O_XBC + D_XBC] = dxp.astype(bf16)
        dp_ref[:, O_DT:W_PROJ] = jnp.zeros((CH, W_PROJ - O_DT), bf16)
        dp_ref[:, O_DT:O_DT + HEADS] = ddt_raw.astype(bf16)

    full = lambda a: pl.BlockSpec(a.shape, lambda s, nd=a.ndim: (0,) * nd)
    rev = lambda s: (nch - 1 - s, 0)
    acc = lambda cdim: pl.BlockSpec((8, cdim), lambda s: (0, 0))
    return _call(
        body, comm, name="ssd_bwd", nsteps=nch,
        in_specs=[pl.BlockSpec((CH, W_PROJ), rev), pl.BlockSpec((CH, D_XBC), rev),
                  pl.BlockSpec((CH, D), rev), pl.BlockSpec((CH, D), rev), pl.BlockSpec((CH, D), rev),
                  pl.BlockSpec((1, 2, NSTATE, 512), lambda s: (nch - 1 - s, 0, 0, 0))]
        + [full(a) for a in (conv_w, conv_b, dt_bias, a_log, d_ssd, g_ssd)],
        out_specs=[pl.BlockSpec((CH, W_PROJ), rev), acc(D_XBC), acc(D_XBC), acc(HEADS), acc(HEADS), acc(D), acc(D)],
        out_shape=[jax.ShapeDtypeStruct((nch * CH, W_PROJ), bf16)]
        + [jax.ShapeDtypeStruct((8, cdim), f32) for cdim in (D_XBC, D_XBC, HEADS, HEADS, D, D)],
        scratch_shapes=[pltpu.VMEM((8, D_XBC), f32), pltpu.VMEM((2, NSTATE, 512), f32)],
        args=[proj, xc_all, y, dys, du, states, conv_w, conv_b, dt_bias, a_log, d_ssd, g_ssd])


SCAN_UNROLL = 8


def _to_slabs(slab_ref, q, mat):
    for ls in range(8):
        slab_ref[ls, pl.ds(PITCH * q, CH), :] = mat[:, 128 * ls:128 * (ls + 1)]


def _from_slabs(slab, q):
    return jnp.concatenate([slab(ls, PITCH * q) for ls in range(8)], axis=1)


def _tile(slab_ref, ls, t, lead=None):
    idx = (ls, pl.ds(t, 8, stride=PITCH), slice(None))
    return slab_ref[idx] if lead is None else slab_ref[(lead,) + idx]


def _s5_fwd(proj, bbq, ccq_t, ar, ai, d_skip, nch, comm=None):
    def body(u_ref, bb_ref, cc_ref, ar_ref, ai_ref, d_ref, s_ref, yl_ref, y5_ref, bu_ref, st_ref):
        @pl.when(pl.program_id(0) == 0)
        def _():
            st_ref[...] = jnp.zeros_like(st_ref)
        u = u_ref[...]
        ub = u.astype(bf16)
        for q in range(NQ):
            _to_slabs(bu_ref, q, _dot(ub[:, 128 * q:128 * (q + 1)], bb_ref[q]))
        ar_t = [ar_ref[:, 128 * l:128 * (l + 1)] for l in range(4)]
        ai_t = [ai_ref[:, 128 * l:128 * (l + 1)] for l in range(4)]

        def one(t, carry):
            re, im = carry
            nre, nim = [], []
            for l in range(4):
                a = ar_t[l] * re[l] - ai_t[l] * im[l] + _tile(bu_ref, l, t)
                b = ar_t[l] * im[l] + ai_t[l] * re[l] + _tile(bu_ref, l + 4, t)
                s_ref[0, l, pl.ds(t, 8, stride=PITCH), :] = a
                s_ref[0, l + 4, pl.ds(t, 8, stride=PITCH), :] = b
                nre.append(a)
                nim.append(b)
            return tuple(nre), tuple(nim)

        def step(tt, carry):
            for k in range(SCAN_UNROLL):
                carry = one(tt * SCAN_UNROLL + k, carry)
            return carry
        init = (tuple(st_ref[l] for l in range(4)), tuple(st_ref[l + 4] for l in range(4)))
        re, im = lax.fori_loop(0, CH // SCAN_UNROLL, step, init)
        for l in range(4):
            st_ref[l] = re[l]
            st_ref[l + 4] = im[l]
        ys = []
        for q in range(NQ):
            sq = _from_slabs(lambda ls, r0: s_ref[0, ls, pl.ds(r0, CH), :], q).astype(bf16)
            ys.append(_dot(sq, cc_ref[q]))
        yl = jnp.concatenate(ys, axis=1) + u * d_ref[...]
        yl_ref[...] = yl
        y5_ref[...] = (0.5 * yl * (1.0 + lax.erf(yl * (1.0 / math.sqrt(2.0))))).astype(bf16)

    const = lambda a: pl.BlockSpec(a.shape, lambda i, nd=a.ndim: (0,) * nd)
    return _call(
        body, comm, name="s5_fwd", nsteps=nch,
        in_specs=[pl.BlockSpec((CH, D), lambda i: (i, O_U // D)), const(bbq), const(ccq_t), const(ar), const(ai), const(d_skip)],
        out_specs=[pl.BlockSpec((1, 8, 8 * PITCH, 128), lambda i: (i, 0, 0, 0)),
                   pl.BlockSpec((CH, D), lambda i: (i, 0)), pl.BlockSpec((CH, D), lambda i: (i, 0))],
        out_shape=[jax.ShapeDtypeStruct((nch, 8, 8 * PITCH, 128), f32), jax.ShapeDtypeStruct((nch * CH, D), f32),
                   jax.ShapeDtypeStruct((nch * CH, D), bf16)],
        scratch_shapes=[pltpu.VMEM((8, 8 * PITCH, 128), f32), pltpu.VMEM((8, 8, 128), f32)],
        args=[proj, bbq, ccq_t, ar, ai, d_skip])


def _s5_bwd(proj, dyl, s_all, bbtq, cctq, ar, ai, d_skip, nch, comm=None):
    def body(u_ref, dy_ref, s_ref, bbt_ref, cct_ref, ar_ref, ai_ref, d_ref,
             du_ref, dcc_ref, dbb_ref, dab_ref, dd_ref, ga_ref, st_ref):
        @pl.when(pl.program_id(0) == 0)
        def _():
            st_ref[...] = jnp.zeros_like(st_ref)
            for r in (dcc_ref, dbb_ref, dab_ref, dd_ref):
                r[...] = jnp.zeros_like(r)
        u = u_ref[...]
        dyl_v = dy_ref[...]
        dd_ref[...] += _rsum8(dyl_v * u)
        ub = u.astype(bf16)
        dyb = dyl_v.astype(bf16)
        for q in range(NQ):
            _to_slabs(ga_ref, q, _dot(dyb[:, 128 * q:128 * (q + 1)], cct_ref[q]))
        ar_t = [ar_ref[:, 128 * l:128 * (l + 1)] for l in range(4)]
        ai_t = [ai_ref[:, 128 * l:128 * (l + 1)] for l in range(4)]

        def one(t, carry):
            re, im, dar, dai = carry
            nre, nim, ndar, ndai = [], [], [], []
            for l in range(4):
                sre = _tile(s_ref, l, t, lead=0)
                sim = _tile(s_ref, l + 4, t, lead=0)
                ndar.append(dar[l] + re[l] * sre + im[l] * sim)
                ndai.append(dai[l] + im[l] * sre - re[l] * sim)
                a = _tile(ga_ref, l, t) + ar_t[l] * re[l] + ai_t[l] * im[l]
                b = _tile(ga_ref, l + 4, t) - ai_t[l] * re[l] + ar_t[l] * im[l]
                ga_ref[l, pl.ds(t, 8, stride=PITCH), :] = a
                ga_ref[l + 4, pl.ds(t, 8, stride=PITCH), :] = b
                nre.append(a)
                nim.append(b)
            return tuple(nre), tuple(nim), tuple(ndar), tuple(ndai)

        def step(tt, carry):
            for k in range(SCAN_UNROLL):
                carry = one(CH - 1 - (tt * SCAN_UNROLL + k), carry)
            return carry
        four = lambda ref, o: tuple(ref[l + o] for l in range(4))
        re, im, dar, dai = lax.fori_loop(0, CH // SCAN_UNROLL, step,
                                         (four(st_ref, 0), four(st_ref, 4), four(dab_ref, 0), four(dab_ref, 4)))
        for l in range(4):
            st_ref[l], st_ref[l + 4] = re[l], im[l]
            dab_ref[l], dab_ref[l + 4] = dar[l], dai[l]
        dus = []
        for q in range(NQ):
            aq = _from_slabs(lambda ls, r0: ga_ref[ls, pl.ds(r0, CH), :], q).astype(bf16)
            sq = _from_slabs(lambda ls, r0: s_ref[0, ls, pl.ds(r0, CH), :], q).astype(bf16)
            dcc_ref[q] += _dot(dyb[:, 128 * q:128 * (q + 1)], sq, TN)
            dbb_ref[q] += _dot(ub[:, 128 * q:128 * (q + 1)], aq, TN)
            dus.append(_dot(aq, bbt_ref[q]))
        du_ref[...] = jnp.concatenate(dus, axis=1) + dyl_v * d_ref[...]

    const = lambda a: pl.BlockSpec(a.shape, lambda s, nd=a.ndim: (0,) * nd)
    rev = lambda s: (nch - 1 - s, 0)
    return _call(
        body, comm, name="s5_bwd", nsteps=nch,
        in_specs=[pl.BlockSpec((CH, D), lambda s: (nch - 1 - s, O_U // D)), pl.BlockSpec((CH, D), rev),
                  pl.BlockSpec((1, 8, 8 * PITCH, 128), lambda s: (nch - 1 - s, 0, 0, 0)),
                  const(bbtq), const(cctq), const(ar), const(ai), const(d_skip)],
        out_specs=[pl.BlockSpec((CH, D), rev), pl.BlockSpec((NQ, 128, D), lambda s: (0, 0, 0)),
                   pl.BlockSpec((NQ, 128, D), lambda s: (0, 0, 0)), pl.BlockSpec((8, 8, 128), lambda s: (0, 0, 0)),
                   pl.BlockSpec((8, D), lambda s: (0, 0))],
        out_shape=[jax.ShapeDtypeStruct((nch * CH, D), f32), jax.ShapeDtypeStruct((NQ, 128, D), f32),
                   jax.ShapeDtypeStruct((NQ, 128, D), f32), jax.ShapeDtypeStruct((8, 8, 128), f32),
                   jax.ShapeDtypeStruct((8, D), f32)],
        scratch_shapes=[pltpu.VMEM((8, 8 * PITCH, 128), f32), pltpu.VMEM((8, 8, 128), f32)],
        args=[proj, dyl, s_all, bbtq, cctq, ar, ai, d_skip])


def _s5_tables(lam_re, lam_im, log_step, b_re, b_im):
    step = jnp.exp(log_step)[:, None]
    mag = jnp.exp(lam_re * step)
    ab_re = mag * jnp.cos(lam_im * step)
    ab_im = mag * jnp.sin(lam_im * step)
    den = lam_re * lam_re + lam_im * lam_im
    coef_re = ((ab_re - 1.0) * lam_re + ab_im * lam_im) / den
    coef_im = (ab_im * lam_re - (ab_re - 1.0) * lam_im) / den
    bb_re = coef_re[..., None] * b_re - coef_im[..., None] * b_im
    bb_im = coef_re[..., None] * b_im + coef_im[..., None] * b_re
    return ab_re, ab_im, bb_re, bb_im


def _blockdiag_in(m_re, m_im):
    eye = jnp.eye(8, dtype=f32)

    def one(m):
        m = m.reshape(NQ, 8, S5_P, 16)
        return jnp.einsum("qgph,gk->qghkp", m, eye).reshape(NQ, 128, 512)
    return jnp.concatenate([one(m_re), one(m_im)], axis=2)


def _blockdiag_in_grad(dm):
    def one(x):
        x = x.reshape(NQ, 8, 16, 8, S5_P)
        return jnp.einsum("qghgp->qgph", x).reshape(NQ * 8, S5_P, 16)
    return one(dm[:, :, :512]), one(dm[:, :, 512:])


def _local_step(x2, tgt2, meta, p, w_in_t, slab):
    seq = x2.shape[0]
    nch = 1 + seq // CH
    metablk = jnp.concatenate([jnp.zeros((CH - N_META, D), f32), meta], axis=0)
    w_full = (w_in_t, pl.BlockSpec(w_in_t.shape, lambda i: (0, 0)))
    h0_of = lambda i, s, q: jnp.where(i == 0, q[0], s[0])

    def in_fn(i, r, s, q, w):
        nb = _rms(h0_of(i, s, q), q[1]).astype(bf16)
        return [_dot(nb, w[0][...], NT), nb], [], [], []
    (proj, n0), _, _, _, (g_up,) = _rowwise("in_proj", in_fn, nch, shifted=[x2], pars=[metablk, p["g_mix"]], refs=[w_full],
                                            out_rows=[(W_PROJ, f32), (D, bf16)], comm=_gather_piece(slab, R_UP, 1024))
    (y, y_ssd, states, xc_all), (g_down,) = _ssd_fwd(proj, p["conv_w"], p["conv_b"], p["dt_bias"], p["a_log"], p["d_ssd"],
                                                     p["g_ssd"], nch, comm=_gather_piece(slab, R_DOWN, 1024))

    ab_re, ab_im, bb_re, bb_im = _s5_tables(p["lam_re"], p["lam_im"], p["log_step"], p["b_re"], p["b_im"])
    ar, ai = ab_re.reshape(NQ, 512), ab_im.reshape(NQ, 512)
    bbq = _blockdiag_in(bb_re, bb_im)
    ccq = _blockdiag_in(jnp.swapaxes(p["c_re"], 1, 2), -jnp.swapaxes(p["c_im"], 1, 2))
    d_skip = p["d_s5"].reshape(1, D)
    (s_all, ylin, y5), (g_go,) = _s5_fwd(proj, bbq.astype(bf16), jnp.swapaxes(ccq, 1, 2).astype(bf16), ar, ai, d_skip, nch,
                                         comm=_gather_piece(slab, R_GLU, 1024))
    whole = lambda a: (a, pl.BlockSpec(a.shape, lambda i: (0, 0, 0)))
    w_up, w_down = whole(g_up), whole(g_down)
    w_glu_t = (g_go, pl.BlockSpec((4, 512, D), lambda i: (0, 0, 0)))
    w_out = (g_go, pl.BlockSpec((4, 512, D), lambda i: (0, 1, 0)))

    def glu_fn(i, r, s, q, w):
        v = jnp.concatenate([_dot(r[0], w[0][k], NT) for k in range(4)], axis=1) + q[0]
        return [v, _rms(v[:, :D] * _sigmoid(v[:, D:]), q[1])], [], [], []
    (v, y_s5), _, _, _ = _rowwise("glu", glu_fn, nch, rows=[y5], pars=[p["b_glu"], p["g_s5"]], refs=[w_glu_t],
                                  out_rows=[(2 * D, f32), (D, bf16)])

    def out_fn(i, r, s, q, w):
        acc = (_dot(r[0][:, :512], w[0][0]) + _dot(r[0][:, 512:], w[0][1]) + _dot(r[1][:, :512], w[0][2])
               + _dot(r[1][:, 512:], w[0][3]))
        return [h0_of(i, s, q) + acc], [], [], []
    (h1,), _, _, _ = _rowwise("out_proj", out_fn, nch, rows=[y_ssd, y_s5], shifted=[x2], pars=[metablk], refs=[w_out],
                              out_rows=[(D, f32)])

    def up_fn(i, r, s, q, w):
        nb = _rms(r[0], q[0]).astype(bf16)
        act = [jnp.square(jnp.maximum(_dot(nb, w[0][k]), 0.0)).astype(bf16) for k in range(4)]
        return [jnp.concatenate(act, axis=1), nb], [], [], []
    (act, n1), _, _, _ = _rowwise("up_proj", up_fn, nch, rows=[h1], pars=[p["g_mlp"]], refs=[w_up],
                                  out_rows=[(4 * D, bf16), (D, bf16)])

    def down_fn(i, r, s, q, w):
        acc = _dot(r[0][:, :D], w[0][0])
        for k in range(1, 4):
            acc = acc + _dot(r[0][:, D * k:D * (k + 1)], w[0][k])
        return [r[1] + acc], [], [], []
    (h2,), _, _, _ = _rowwise("down_proj", down_fn, nch, rows=[act, h1], refs=[w_down], out_rows=[(D, f32)])

    def final_fn(i, r, s, q, w):
        live = jnp.where(i > 0, 1.0, 0.0)
        err = (_rms(r[0], q[0]) - s[0]) * live
        dh, dg8 = _rms_bwd(err * (1.0 / D), r[0], q[0])
        return [dh, dh], [_rsum8(err * err), dg8], [], []
    (dh2, dh2_b), (loss8, dgf8), _, _ = _rowwise("final", final_fn, nch, rows=[h2], shifted=[tgt2], pars=[p["g_final"]],
                                                  out_rows=[(D, f32), (D, bf16)], out_accs=[D, D])
    loss = 0.5 / D * jnp.sum(loss8)

    def down_bwd_fn(i, r, s, q, w):
        dm_ = [_dot(r[0], w[0][k], NT) * 2.0 * jnp.sqrt(r[1][:, D * k:D * (k + 1)].astype(f32)) for k in range(4)]
        return [jnp.concatenate(dm_, axis=1)], [], [], []
    (dm,), _, _, _ = _rowwise("down_bwd", down_bwd_fn, nch, rows=[dh2_b, act], refs=[w_down], out_rows=[(4 * D, bf16)])
    g_a = _dw_into("dw_down", act, dh2_b, None, 1024, True, 1, 4, 0, piece_rows=2048)

    def up_bwd_fn(i, r, s, q, w):
        acc = _dot(r[0][:, :D], w[0][0], NT)
        for k in range(1, 4):
            acc = acc + _dot(r[0][:, D * k:D * (k + 1)], w[0][k], NT)
        dh, dg8 = _rms_bwd(acc, r[1], q[0])
        dh1_ = r[2] + dh
        return [dh1_, dh1_], [dg8], [], []
    (dh1, dh1_b), (dgmlp8,), _, _ = _rowwise("up_bwd", up_bwd_fn, nch, rows=[dm, h1, dh2], pars=[p["g_mlp"]], refs=[w_up],
                                             out_rows=[(D, f32), (D, bf16)], out_accs=[D])
    g_a = _dw_into("dw_up", n1, dm, g_a, 1024, False, 0, 4, 0, piece_rows=2048)

    def out_bwd_fn(i, r, s, q, w):
        dmix = [_dot(r[0], w[0][k], NT) for k in range(4)]
        v1, v2 = r[1][:, :D], r[1][:, D:]
        s2 = _sigmoid(v2)
        dglu, dg8 = _rms_bwd(jnp.concatenate(dmix[2:], axis=1), v1 * s2, q[0])
        dv = jnp.concatenate([dglu * s2, dglu * v1 * s2 * (1.0 - s2)], axis=1)
        return [jnp.concatenate(dmix[:2], axis=1), dv], [dg8, _rsum8(dv)], [], []
    (dys, dv), (dgs58, dbglu8), _, _ = _rowwise("out_bwd", out_bwd_fn, nch, rows=[dh1_b, v], pars=[p["g_s5"]], refs=[w_out],
                                                out_rows=[(D, f32), (2 * D, bf16)], out_accs=[D, 2 * D])
    g_b = _dw_into("dw_out_a", y_ssd, dh1_b, None, 512, True, 1, 2, 0, piece_rows=1024)
    g_b = _dw_into("dw_out_b", y_s5, dh1_b, g_b, 512, True, 1, 2, 2, piece_rows=1024)

    def glu_bwd_fn(i, r, s, q, w):
        acc = _dot(r[0][:, :512], w[0][0])
        for k in range(1, 4):
            acc = acc + _dot(r[0][:, 512 * k:512 * (k + 1)], w[0][k])
        yl = r[1]
        cdf = 0.5 * (1.0 + lax.erf(yl * (1.0 / math.sqrt(2.0))))
        pdf = jnp.exp(-0.5 * yl * yl) * (1.0 / math.sqrt(2.0 * math.pi))
        return [acc * (cdf + yl * pdf)], [], [], []
    (dylin,), _, _, _ = _rowwise("glu_bwd", glu_bwd_fn, nch, rows=[dv, ylin], refs=[w_glu_t], out_rows=[(D, f32)])
    g_b = _dw_into("dw_glu", dv, y5, g_b, 512, True, 0, 4, 0, piece_rows=1024)

    (du, dcc, dbb, dab, dds5), (land_a,) = _s5_bwd(proj, dylin, s_all, jnp.swapaxes(bbq, 1, 2).astype(bf16), ccq.astype(bf16),
                                                   ar, ai, d_skip, nch, comm=_scatter_piece(g_a))
    (dproj, dcw8, dcb8, ddtb8, dal8, dd8, dgssd8), (land_b,) = _ssd_bwd(
        proj, xc_all, y, dys, du, states, p["conv_w"], p["conv_b"], p["dt_bias"], p["a_log"], p["d_ssd"], p["g_ssd"], nch,
        comm=_scatter_piece(g_b))

    gt = _dw_in_t(dproj, n0)
    gt = jnp.concatenate([gt[0:1024], gt[O_XBC:O_XBC + D_XBC], gt[O_DT:O_DT + HEADS], gt[O_U:O_U + D]], axis=0).reshape(4, 900, D)
    g_c = jnp.concatenate([gt, jnp.zeros((4, 1024 - 900, D), bf16)], axis=1)

    def in_bwd_fn(i, r, s, q, w):
        dh, dg8 = _rms_bwd(_dot(r[0], w[0][...]), h0_of(i, s, q), q[1])
        dh0 = r[1] + dh
        return [], [dg8], [dh0], [dh0]
    _, (dgmix8,), (dmeta_blk,), (grad_x,), (land_c,) = _rowwise(
        "in_bwd", in_bwd_fn, nch, rows=[dproj, dh1], shifted=[x2], pars=[metablk, p["g_mix"]], refs=[w_full], out_accs=[D],
        out_first=[(D, f32)], out_shifted=[(D, f32)], comm=_scatter_piece(g_c))

    dab_q = jnp.swapaxes(dab.reshape(2, 4, NQ, 128), 1, 2).reshape(2, NQ * 8, S5_P)
    dbb_re, dbb_im = _blockdiag_in_grad(dbb)
    dcr, dci = _blockdiag_in_grad(dcc)
    _, vjp = jax.vjp(_s5_tables, p["lam_re"], p["lam_im"], p["log_step"], p["b_re"], p["b_im"])
    dlam_re, dlam_im, dlog_step, db_re, db_im = vjp((dab_q[0], dab_q[1], dbb_re, dbb_im))

    s8 = lambda a: jnp.sum(a, axis=0, keepdims=True)
    hsum = lambda a: jnp.sum(s8(a).reshape(HEADS, HEAD_DIM), axis=1).reshape(1, HEADS)
    small = dict(
        g_mix=s8(dgmix8), conv_w=dcw8[0:4], conv_b=s8(dcb8), dt_bias=s8(ddtb8), a_log=s8(dal8),
        d_ssd=hsum(dd8), g_ssd=s8(dgssd8), lam_re=dlam_re, lam_im=dlam_im, log_step=dlog_step, b_re=db_re, b_im=db_im,
        c_re=jnp.swapaxes(dcr, 1, 2), c_im=-jnp.swapaxes(dci, 1, 2), d_s5=s8(dds5).reshape(NQ * 8, 16),
        b_glu=s8(dbglu8), g_s5=s8(dgs58), g_mlp=s8(dgmlp8), g_final=s8(dgf8).reshape(D))
    return loss, grad_x, dmeta_blk, [(g_a, land_a), (g_b, land_b), (g_c, land_c)], small


def _perm_rows_w_in(wt):
    return jnp.concatenate([wt[0:1024], wt[2576:3600], wt[1024:2560], wt[2560:2576],
                            jnp.zeros((W_PROJ - 3600, wt.shape[1]), wt.dtype)], axis=0)


def _unperm_cols_w_in(g):
    return jnp.concatenate([g[:, 0:1024], g[:, O_XBC:O_XBC + D_XBC], g[:, O_DT:O_DT + HEADS], g[:, O_U:O_U + D]], axis=1)


def _pack_shard(w_in, w_glu, w_out, w_up, w_down, spare):
    dt = w_in.dtype
    parts = [w_up, w_down, w_glu.T, w_out, w_in.T, spare,
             jnp.zeros((PACK_ROWS - R_SPARE - spare.shape[0], D), dt)]
    return jnp.concatenate(parts, axis=0)


def _allgather8(x_shard, name):
    m_per, n = x_shard.shape

    def body(x_ref, out_ref, send_sems, recv_sems, stage, local_sems):
        x, y, c = _place()
        me, sibling = (x, y, c), (x, y, 1 - c)
        chips = [(1 - x, y), (x, 1 - y), (1 - x, 1 - y)]

        def rows(px, py, pc):
            return out_ref.at[pl.ds((4 * px + 2 * py + pc) * m_per, m_per), :]

        def copy(k, block, to, src=None):
            return pltpu.make_async_remote_copy(
                src_ref=rows(*block) if src is None else src, dst_ref=rows(*block),
                send_sem=send_sems.at[k], recv_sem=recv_sems.at[k], device_id=to, device_id_type=MESH)

        load = pltpu.make_async_copy(x_ref, stage, local_sems.at[0])
        load.start()
        first = [copy(0, me, sibling, src=x_ref)]
        first += [copy(1 + j, me, (*chip, c), src=x_ref) for j, chip in enumerate(chips)]
        for cp in first:
            cp.start()
        load.wait()
        store = pltpu.make_async_copy(stage, rows(*me), local_sems.at[1])
        store.start()
        passed = [copy(4 + j, (*chip, c), sibling) for j, chip in enumerate(chips)]
        for j, chip in enumerate(chips):
            copy(1 + j, (*chip, c), me).wait_recv()
            passed[j].start()
        copy(0, sibling, me).wait_recv()
        for j, chip in enumerate(chips):
            copy(4 + j, (*chip, 1 - c), me).wait_recv()
        for cp in first + passed:
            cp.wait_send()
        store.wait()

    return pl.pallas_call(
        body, name=name, out_shape=jax.ShapeDtypeStruct((8 * m_per, n), x_shard.dtype),
        in_specs=[_ANY], out_specs=_ANY,
        scratch_shapes=[pltpu.SemaphoreType.DMA((7,)), pltpu.SemaphoreType.DMA((7,)), pltpu.VMEM((m_per, n), x_shard.dtype),
                        pltpu.SemaphoreType.DMA((2,))])(x_shard)


def _swap_sibling(r, name):
    def body(r_ref, out_ref, send_sem, recv_sem):
        x, y, c = _place()
        cp = pltpu.make_async_remote_copy(src_ref=r_ref, dst_ref=out_ref, send_sem=send_sem, recv_sem=recv_sem,
                                          device_id=(x, y, 1 - c), device_id_type=MESH)
        cp.start()
        cp.wait()

    return pl.pallas_call(
        body, name=name, out_shape=jax.ShapeDtypeStruct(r.shape, r.dtype), in_specs=[_ANY], out_specs=_ANY,
        scratch_shapes=[pltpu.SemaphoreType.DMA, pltpu.SemaphoreType.DMA])(r)


SH_CONVW, SH_META = 4 * 384, 16 * 256
SPARE_ROWS = 17

SMALL = [("g_mix", (1, 1024)), ("conv_b", (1, 1536)), ("dt_bias", (1, 16)), ("a_log", (1, 16)), ("d_ssd", (1, 16)),
         ("g_ssd", (1, 1024)), ("lam_re", (1, 64, 64)), ("lam_im", (1, 64, 64)), ("log_step", (1, 64)),
         ("b_re", (1, 64, 64, 16)), ("b_im", (1, 64, 64, 16)), ("c_re", (1, 64, 16, 64)), ("c_im", (1, 64, 16, 64)),
         ("d_s5", (1, 64, 16)), ("b_glu", (1, 2048)), ("g_s5", (1, 1024)), ("g_mlp", (1, 1024)), ("g_final", (1024,))]


def _pack_small(arrs, rows):
    flat = jnp.concatenate([a.reshape(-1).astype(f32) for a in arrs])
    return jnp.concatenate([flat, jnp.zeros((rows * D - flat.shape[0],), f32)]).reshape(rows, D)


def _unpack_small(slab, shapes):
    flat = slab.reshape(-1)
    out, o = [], 0
    for shp in shapes:
        n = math.prod(shp)
        out.append(flat[o:o + n].reshape(shp))
        o += n
    return out


def _sum8(g, rows):
    def body(g_ref, o_ref):
        acc = g_ref[0]
        for k in range(1, 8):
            acc = acc + g_ref[k]
        o_ref[...] = acc
    return pl.pallas_call(body, name="sum8", out_shape=jax.ShapeDtypeStruct((rows, D), f32),
                          compiler_params=_cp())(g.reshape(8, rows, D))


def _adam_math(w_, g_, m_, v_):
    m2 = ADAM_B1 * m_ + (1.0 - ADAM_B1) * g_
    v2 = ADAM_B2 * v_ + (1.0 - ADAM_B2) * jnp.square(g_)
    m_hat = m2 / (1.0 - ADAM_B1 ** ADAM_STEP)
    v_hat = v2 / (1.0 - ADAM_B2 ** ADAM_STEP)
    delta = -ADAM_LR * (m_hat / (jnp.sqrt(v_hat) + ADAM_EPS) + ADAM_WD * w_)
    return delta, m2, v2


def _adamw(name, w, g, m, v, bm):
    def fn(i, r, s, q, refs):
        return list(_adam_math(*r)), [], [], []
    c = w.shape[1]
    (d, m2, v2), _, _, _ = _rowwise(name, fn, w.shape[0] // bm, rows=[w, g, m, v], out_rows=[(c, f32)] * 3, bm=bm)
    return d, m2, v2


def _adamw_whole(name, w, g, m, v):
    def body(w_ref, g_ref, m_ref, v_ref, d_ref, m2_ref, v2_ref):
        d_ref[...], m2_ref[...], v2_ref[...] = _adam_math(w_ref[...], g_ref[...], m_ref[...], v_ref[...])
    return pl.pallas_call(body, name=name, out_shape=[jax.ShapeDtypeStruct(w.shape, f32)] * 3, compiler_params=_cp())(w, g, m, v)


def _sum_parts(name, own, land):
    def fn(i, r, s, q, refs):
        acc = r[0].astype(f32)
        for k in range(7):
            acc = acc + refs[0][k].astype(f32)
        return [acc], [], [], []
    (o,), _, _, _ = _rowwise(name, fn, own.shape[0] // CH, rows=[own],
                             refs=[(land, pl.BlockSpec((7, CH, D), lambda i: (0, i, 0)))], out_rows=[(D, f32)])
    return o


def kernel(x, meta_tokens, g_mix, w_in, conv_w, conv_b, dt_bias, a_log, d_ssd, g_ssd, lam_re, lam_im, log_step, b_re, b_im, c_re, c_im, d_s5, w_glu, b_glu, g_s5, w_out, g_mlp, w_up, w_down, g_final, loss_target, m_meta_tokens, m_g_mix, m_w_in, m_conv_w, m_conv_b, m_dt_bias, m_a_log, m_d_ssd, m_g_ssd, m_lam_re, m_lam_im, m_log_step, m_b_re, m_b_im, m_c_re, m_c_im, m_d_s5, m_w_glu, m_b_glu, m_g_s5, m_w_out, m_g_mlp, m_w_up, m_w_down, m_g_final, v_meta_tokens, v_g_mix, v_w_in, v_conv_w, v_conv_b, v_dt_bias, v_a_log, v_d_ssd, v_g_ssd, v_lam_re, v_lam_im, v_log_step, v_b_re, v_b_im, v_c_re, v_c_im, v_d_s5, v_w_glu, v_b_glu, v_g_s5, v_w_out, v_g_mlp, v_w_up, v_w_down, v_g_final):
    given = dict(locals())
    cx, cy, cc = _place()
    chip = 2 * cx + cy

    small_f = jnp.concatenate([conv_w.reshape(-1), meta_tokens.reshape(-1)])
    t_hi = small_f.astype(bf16)
    r_1 = small_f - t_hi.astype(f32)
    t_mid = r_1.astype(bf16)
    t_lo = (r_1 - t_mid.astype(f32)).astype(bf16)
    terms = jnp.concatenate([t_hi, t_mid, t_lo])
    spare = jnp.concatenate([terms, jnp.zeros((SPARE_ROWS * D - terms.shape[0],), bf16)]).reshape(SPARE_ROWS, D)
    slab = _pack_shard(w_in[0].astype(bf16), w_glu[0].astype(bf16), w_out[0].astype(bf16), w_up[0].astype(bf16),
                       w_down[0].astype(bf16), spare)
    my_half = lax.dynamic_slice_in_dim(slab, R_IN + cc * 512, 512, axis=0)
    gathered = _allgather8(my_half, "gather_w_in").reshape(4, 1024, D)
    w_in_t = _perm_rows_w_in(jnp.concatenate([gathered[s, 0:900] for s in range(4)], axis=0))
    n_sf = SH_CONVW + SH_META
    tr = gathered[:, 900:900 + SPARE_ROWS].reshape(4, SPARE_ROWS * D)[:, :3 * n_sf].astype(f32).reshape(4, 3, n_sf)
    sp = tr[:, 0] + tr[:, 1] + tr[:, 2]
    conv_w_full = jnp.concatenate([sp[s, :SH_CONVW].reshape(4, 384) for s in range(4)], axis=1)
    meta_full = jnp.concatenate([sp[s, SH_CONVW:].reshape(16, 256) for s in range(4)], axis=1)

    p = dict(g_mix=g_mix, conv_w=conv_w_full, conv_b=conv_b, dt_bias=dt_bias, a_log=a_log, d_ssd=d_ssd, g_ssd=g_ssd,
             lam_re=lam_re[0], lam_im=lam_im[0], log_step=log_step[0], b_re=b_re[0], b_im=b_im[0], c_re=c_re[0], c_im=c_im[0],
             d_s5=d_s5[0], b_glu=b_glu, g_s5=g_s5, g_mlp=g_mlp, g_final=g_final.reshape(1, D))
    loss_part, grad_x, dmeta_blk, pieces, g = _local_step(x[0], loss_target[0], meta_full, p, w_in_t, slab)
    grad_x = grad_x.reshape(x.shape)

    reds = []
    for k, (gp, land) in enumerate(pieces):
        half = gp.shape[1] // 2
        own = lax.dynamic_slice(gp, (chip, cc * half, 0), (1, half, D)).reshape(half, D)
        reds.append(_sum_parts("rs_sum_%d" % k, own, land))
    red = jnp.concatenate(reds, axis=0)
    other = _swap_sibling(red, "rs_share")
    first = jnp.where(cc == 0, red, other)
    second = jnp.where(cc == 0, other, red)
    g_up, g_down = first[0:1024], second[0:1024]
    g_glu, g_out = first[1024:1536].T, second[1024:1536]
    g_in = jnp.concatenate([first[1536:2048], second[1536:1536 + 900 - 512]], axis=0).T

    small_g = [g[n] for n, _ in SMALL] + [g["conv_w"], dmeta_blk[CH - N_META:CH], loss_part.reshape(1)]
    n_small = sum(math.prod(s) for _, s in SMALL) + 4 * D_XBC + N_META * D + 1
    rows_small = -(-n_small // (8 * D)) * 8
    total = _sum8(_allgather8(_pack_small(small_g, rows_small), "gather_small"), rows_small)
    outs = _unpack_small(total, [s for _, s in SMALL] + [(4, D_XBC), (N_META, D), ()])
    gs = {n: o for (n, _), o in zip(SMALL, outs)}
    g_conv_w = lax.dynamic_slice_in_dim(outs[-3], chip * 384, 384, axis=1).reshape(conv_w.shape)
    g_meta = lax.dynamic_slice_in_dim(outs[-2], chip * 256, 256, axis=1)
    loss = outs[-1]

    grads = dict(gs, meta_tokens=g_meta, conv_w=g_conv_w, w_in=g_in.reshape(w_in.shape), w_glu=g_glu.reshape(w_glu.shape),
                 w_out=g_out.reshape(w_out.shape), w_up=g_up.reshape(w_up.shape), w_down=g_down.reshape(w_down.shape))
    delta, new_m, new_v = {}, {}, {}
    for n in ("w_in", "w_glu", "w_out", "w_up", "w_down"):
        shp = given[n].shape
        two = lambda a: a.reshape(shp[1], shp[2])
        d_, m_, v_ = _adamw("adamw_" + n, two(given[n]), two(grads[n]), two(given["m_" + n]), two(given["v_" + n]), 256)
        delta[n], new_m[n], new_v[n] = d_.reshape(shp), m_.reshape(shp), v_.reshape(shp)
    for n in [n for n, _ in SMALL] + ["conv_w", "meta_tokens"]:
        shp = given[n].shape
        two = (lambda a: a.reshape(1, -1)) if len(shp) == 1 else (lambda a: a)
        d_, m_, v_ = _adamw_whole("adamw_" + n, two(given[n]), two(grads[n].reshape(shp)), two(given["m_" + n]), two(given["v_" + n]))
        delta[n], new_m[n], new_v[n] = d_.reshape(shp), m_.reshape(shp), v_.reshape(shp)

    order = ["meta_tokens", "g_mix", "w_in", "conv_w", "conv_b", "dt_bias", "a_log", "d_ssd", "g_ssd", "lam_re", "lam_im", "log_step",
             "b_re", "b_im", "c_re", "c_im", "d_s5", "w_glu", "b_glu", "g_s5", "w_out", "g_mlp", "w_up", "w_down", "g_final"]
    grads_out = [grads[n].reshape(given[n].shape) for n in order]
    return (loss, grad_x, *grads_out, *[delta[n] for n in order], *[new_m[n] for n in order], *[new_v[n] for n in order])
```

```python
import math

import jax
import jax.numpy as jnp
from jax import lax
from jax.experimental import pallas as pl
from jax.experimental.pallas import tpu as pltpu

f32 = jnp.float32
bf16 = jnp.bfloat16

D = 1024
N_META = 16
CH = 256
HEADS = 16
HEAD_DIM = 64
NSTATE = 128
D_XBC = 1536
S5_P = 64
NQ = 8
PITCH = CH + 4
EPS = 1e-5
O_Z, O_U, O_XBC, O_DT, W_PROJ = 0, 1024, 2048, 3584, 3712
VMEM_LIMIT = 60 * 1024 * 1024

ADAM_LR, ADAM_B1, ADAM_B2, ADAM_EPS, ADAM_WD, ADAM_STEP = 0.001, 0.9, 0.999, 1e-08, 0.01, 10

NT = (((1,), (1,)), ((), ()))
TN = (((0,), (0,)), ((), ()))
_ANY = pl.BlockSpec(memory_space=pl.ANY)


def _cp(sem=None):
    return pltpu.CompilerParams(dimension_semantics=sem, vmem_limit_bytes=VMEM_LIMIT)


def _sigmoid(v):
    return 1.0 / (1.0 + jnp.exp(-v))


def _rsum8(v):
    r, c = v.shape
    return jnp.sum(v.reshape(r // 8, 8, c), axis=0)


def _rms(h, g):
    r = lax.rsqrt(jnp.mean(h * h, axis=-1, keepdims=True) + EPS)
    return h * r * g


def _rms_bwd(dy, h, g):
    r = lax.rsqrt(jnp.mean(h * h, axis=-1, keepdims=True) + EPS)
    n = h * r
    dn = dy * g
    dh = r * (dn - n * jnp.mean(dn * n, axis=-1, keepdims=True))
    return dh, _rsum8(dy * n)


def _dot(a, b, dims=None):
    if dims is None:
        return jnp.dot(a, b, preferred_element_type=f32)
    return lax.dot_general(a, b, dims, preferred_element_type=f32)


def _split_dot(v, m01, dims, terms, v_is_lhs=True):
    out, r = None, v
    for _ in range(terms):
        piece = r.astype(bf16)
        o = _dot(piece, m01, dims) if v_is_lhs else _dot(m01, piece, dims)
        out = o if out is None else out + o
        r = r - piece.astype(f32)
    return out


MESH = pl.DeviceIdType.MESH


def _place():
    return lax.axis_index("x"), lax.axis_index("y"), lax.axis_index("c")


def _flip(v, f):
    return 1 - v if f else v


def _call(body, comm, *, name, nsteps, in_specs, out_specs, out_shape, scratch_shapes, args):
    n_in, n_out, n_scr = len(in_specs), len(out_specs), len(scratch_shapes)
    if comm is None:
        res = pl.pallas_call(body, name=name, grid=(nsteps,), in_specs=in_specs, out_specs=out_specs, out_shape=out_shape,
                             scratch_shapes=scratch_shapes, compiler_params=_cp(("arbitrary",)))(*args)
        return list(res), []
    c_in, c_out = len(comm["ins"]), len(comm["outs"])

    def wrapped(*refs):
        o0 = n_in + c_in
        s0 = o0 + n_out + c_out
        cparts = (refs[n_in:o0], refs[o0 + n_out:s0], refs[s0 + n_scr:])

        @pl.when(pl.program_id(0) == 0)
        def _():
            comm["start"](*cparts)
        body(*refs[:n_in], *refs[o0:o0 + n_out], *refs[s0:s0 + n_scr])

        @pl.when(pl.program_id(0) == nsteps - 1)
        def _():
            comm["finish"](*cparts)

    any_spec = pl.BlockSpec(memory_space=pl.ANY)
    res = pl.pallas_call(
        wrapped, name=name, grid=(nsteps,), in_specs=list(in_specs) + [any_spec] * c_in,
        out_specs=list(out_specs) + [any_spec] * c_out, out_shape=list(out_shape) + list(comm["outs"]),
        scratch_shapes=list(scratch_shapes) + list(comm["scratch"]),
        compiler_params=_cp(("arbitrary",)))(*args, *comm["ins"])
    return list(res[:n_out]), list(res[n_out:])


def _gather_piece(slab, r0, rows):
    half = rows // 2

    def copies(slab_ref, out_ref, send_sems, recv_sems):
        x, y, c = _place()
        cps = []
        for fx, fy in ((1, 0), (0, 1), (1, 1)):
            for fc in (0, 1):
                k = (2 * fx + fy - 1) * 2 + fc
                cps.append(pltpu.make_async_remote_copy(
                    src_ref=slab_ref.at[pl.ds(r0 + c * half, half), :], dst_ref=out_ref.at[2 * x + y, pl.ds(c * half, half), :],
                    send_sem=send_sems.at[k], recv_sem=recv_sems.at[k],
                    device_id=(_flip(x, fx), _flip(y, fy), _flip(c, fc)), device_id_type=MESH))
        return cps

    def start(ins, outs, scr):
        send_sems, recv_sems, stage, local_sems = scr
        x, y, _ = _place()
        load = pltpu.make_async_copy(ins[0].at[pl.ds(r0, rows), :], stage, local_sems.at[0])
        load.start()
        for cp in copies(ins[0], outs[0], send_sems, recv_sems):
            cp.start()
        load.wait()
        pltpu.make_async_copy(stage, outs[0].at[2 * x + y], local_sems.at[1]).start()

    def finish(ins, outs, scr):
        send_sems, recv_sems, stage, local_sems = scr
        x, y, _ = _place()
        for cp in copies(ins[0], outs[0], send_sems, recv_sems):
            cp.wait()
        pltpu.make_async_copy(stage, outs[0].at[2 * x + y], local_sems.at[1]).wait()

    return dict(ins=[slab], outs=[jax.ShapeDtypeStruct((4, rows, D), bf16)],
                scratch=[pltpu.SemaphoreType.DMA((6,)), pltpu.SemaphoreType.DMA((6,)), pltpu.VMEM((rows, D), bf16),
                         pltpu.SemaphoreType.DMA((2,))], start=start, finish=finish)


def _scatter_piece(gpiece):
    half = gpiece.shape[1] // 2

    def copies(g_ref, land_ref, send_sems, recv_sems):
        x, y, c = _place()
        cps = []
        for fx in (0, 1):
            for fy in (0, 1):
                for fc in (0, 1):
                    k = 4 * fx + 2 * fy + fc - 1
                    if k < 0:
                        continue
                    px, py, pc = _flip(x, fx), _flip(y, fy), _flip(c, fc)
                    cps.append(pltpu.make_async_remote_copy(
                        src_ref=g_ref.at[2 * px + py, pl.ds(pc * half, half), :], dst_ref=land_ref.at[k],
                        send_sem=send_sems.at[k], recv_sem=recv_sems.at[k], device_id=(px, py, pc), device_id_type=MESH))
        return cps

    def start(ins, outs, scr):
        for cp in copies(ins[0], outs[0], *scr):
            cp.start()

    def finish(ins, outs, scr):
        for cp in copies(ins[0], outs[0], *scr):
            cp.wait()

    return dict(ins=[gpiece], outs=[jax.ShapeDtypeStruct((7, half, D), gpiece.dtype)],
                scratch=[pltpu.SemaphoreType.DMA((7,)), pltpu.SemaphoreType.DMA((7,))], start=start, finish=finish)


def _rowwise(name, fn, nblk, rows=(), shifted=(), pars=(), refs=(), out_rows=(), out_accs=(), out_first=(), out_shifted=(), bm=CH,
             comm=None):
    n_r, n_s, n_p, n_w = len(rows), len(shifted), len(pars), len(refs)
    n_in = n_r + n_s + n_p + n_w
    n_o, n_a, n_f, n_so = len(out_rows), len(out_accs), len(out_first), len(out_shifted)

    def body(*all_refs):
        i = pl.program_id(0)
        ins = all_refs[:n_in]
        outs = all_refs[n_in:]
        rv = [r[...] for r in ins[:n_r]]
        sv = [r[...] for r in ins[n_r:n_r + n_s]]
        pv = [r[...] for r in ins[n_r + n_s:n_r + n_s + n_p]]
        ro, ao, fo, so = fn(i, rv, sv, pv, list(ins[n_r + n_s + n_p:]))
        for r, v in zip(outs[:n_o], ro):
            r[...] = v.astype(r.dtype)
        accs = outs[n_o:n_o + n_a]

        @pl.when(i == 0)
        def _():
            for r in accs:
                r[...] = jnp.zeros_like(r)
            for r, v in zip(outs[n_o + n_a:n_o + n_a + n_f], fo):
                r[...] = v.astype(r.dtype)
        for r, v in zip(accs, ao):
            r[...] += v
        for r, v in zip(outs[n_o + n_a + n_f:], so):
            r[...] = v.astype(r.dtype)

    prev = lambda i: (jnp.maximum(i - 1, 0), 0)
    in_specs = [pl.BlockSpec((bm, a.shape[1]), lambda i: (i, 0)) for a in rows]
    in_specs += [pl.BlockSpec((bm, a.shape[1]), prev) for a in shifted]
    in_specs += [pl.BlockSpec(a.shape, lambda i, nd=a.ndim: (0,) * nd) for a in pars]
    in_specs += [spec for _, spec in refs]
    out_specs = [pl.BlockSpec((bm, c), lambda i: (i, 0)) for c, _ in out_rows]
    out_specs += [pl.BlockSpec((8, c), lambda i: (0, 0)) for c in out_accs]
    out_specs += [pl.BlockSpec((bm, c), lambda i: (0, 0)) for c, _ in out_first]
    out_specs += [pl.BlockSpec((bm, c), prev) for c, _ in out_shifted]
    out_shape = [jax.ShapeDtypeStruct((nblk * bm, c), dt) for c, dt in out_rows]
    out_shape += [jax.ShapeDtypeStruct((8, c), f32) for c in out_accs]
    out_shape += [jax.ShapeDtypeStruct((bm, c), dt) for c, dt in out_first]
    out_shape += [jax.ShapeDtypeStruct(((nblk - 1) * bm, c), dt) for c, dt in out_shifted]
    res, cres = _call(body, comm, name=name, nsteps=nblk, in_specs=in_specs, out_specs=out_specs, out_shape=out_shape,
                      scratch_shapes=[], args=[*rows, *shifted, *pars, *[a for a, _ in refs]])
    parts = (res[:n_o], res[n_o:n_o + n_a], res[n_o + n_a:n_o + n_a + n_f], res[n_o + n_a + n_f:])
    return parts if comm is None else parts + (cres,)


PACK_ROWS = 4096
HALF_ROWS = PACK_ROWS // 2
R_UP, R_DOWN, R_GLU, R_OUT, R_IN, R_SPARE = 0, 1024, 2048, 2560, 3072, 3972


BIG_ROWS = 768


def _contract_rows(lp):
    return BIG_ROWS if lp % BIG_ROWS == 0 else CH


def _dw_into(name, a, b, slab, ka, a_sharded, row_blk, n_s, s0, piece_rows=2048):
    lp = a.shape[0]
    bm = _contract_rows(lp)
    steps = lp // bm

    def body(a_ref, b_ref, *rest):
        o_ref, acc = rest[-2], rest[-1]
        k = pl.program_id(1)

        @pl.when(k == 0)
        def _():
            acc[...] = jnp.zeros_like(acc)
        acc[...] += _dot(a_ref[...], b_ref[...], TN)

        @pl.when(k == steps - 1)
        def _():
            o_ref[0] = acc[...].astype(bf16)

    in_specs = [pl.BlockSpec((bm, ka), (lambda s, k: (k, s)) if a_sharded else (lambda s, k: (k, 0))),
                pl.BlockSpec((bm, D), (lambda s, k: (k, 0)) if a_sharded else (lambda s, k: (k, s)))]
    args = [a, b]
    aliases = {}
    if slab is not None:
        in_specs.append(_ANY)
        args.append(slab)
        aliases = {2: 0}
    return pl.pallas_call(
        body, name=name, grid=(n_s, steps), in_specs=in_specs,
        out_specs=pl.BlockSpec((1, ka, D), lambda s, k: (s0 + s, row_blk, 0)),
        out_shape=jax.ShapeDtypeStruct((4, piece_rows, D), bf16),
        scratch_shapes=[pltpu.VMEM((ka, D), f32)], input_output_aliases=aliases,
        compiler_params=_cp(("arbitrary", "arbitrary")))(*args)


def _dw_in_t(dproj, n0):
    lp = n0.shape[0]
    bm = _contract_rows(lp)
    steps = lp // bm
    bn = 512

    def body(a_ref, b_ref, o_ref, acc):
        k = pl.program_id(1)

        @pl.when(k == 0)
        def _():
            acc[...] = jnp.zeros_like(acc)
        acc[...] += _dot(a_ref[...], b_ref[...], TN)

        @pl.when(k == steps - 1)
        def _():
            o_ref[...] = acc[...].astype(bf16)

    return pl.pallas_call(
        body, name="dw_in", grid=(D // bn, steps),
        in_specs=[pl.BlockSpec((bm, W_PROJ), lambda j, k: (k, 0)), pl.BlockSpec((bm, bn), lambda j, k: (k, j))],
        out_specs=pl.BlockSpec((W_PROJ, bn), lambda j, k: (0, j)),
        out_shape=jax.ShapeDtypeStruct((W_PROJ, D), bf16),
        scratch_shapes=[pltpu.VMEM((W_PROJ, bn), f32)],
        compiler_params=_cp(("arbitrary", "arbitrary")))(dproj, n0)


def _head_expand():
    h = lax.broadcasted_iota(jnp.int32, (HEADS, D), 0)
    c = lax.broadcasted_iota(jnp.int32, (HEADS, D), 1)
    return jnp.where((c >> 6) == h, 1.0, 0.0).astype(bf16)


def _ssd_common(i, P, prev8, cw, cb, dtb, alog, xc=None):
    z = P[:, O_Z:O_Z + D]
    xp = P[:, O_XBC:O_XBC + D_XBC]
    dt_raw = P[:, O_DT:O_DT + HEADS]
    row = lax.broadcasted_iota(jnp.int32, (CH, 1), 0)
    if xc is None:
        row8 = lax.broadcasted_iota(jnp.int32, (8, 1), 0)
        xc = cb + cw[3:4] * xp
        for k in (1, 2, 3):
            rolled = pltpu.roll(xp, k, 0)
            fix = pltpu.roll(prev8, k, 0)
            top = jnp.where(row8 < k, fix, rolled[0:8])
            xc = xc + cw[3 - k:4 - k] * jnp.concatenate([top, rolled[8:]], axis=0)
    sg = _sigmoid(xc)
    xbc = xc * sg
    live = jnp.where(jnp.logical_or(i > 0, row >= CH - N_META), 1.0, 0.0)
    pre = dt_raw + dtb
    dt = jnp.where(pre > 20.0, pre, jnp.log(1.0 + jnp.exp(jnp.minimum(pre, 20.0)))) * live
    a = -jnp.exp(alog)
    dta = dt * a
    r_i = lax.broadcasted_iota(jnp.int32, (CH, CH), 0)
    c_i = lax.broadcasted_iota(jnp.int32, (CH, CH), 1)
    tril = r_i >= c_i
    acs = _split_dot(dta, jnp.where(tril, 1.0, 0.0).astype(bf16), None, 3, v_is_lhs=False)
    acs_t = _split_dot(dta, jnp.where(r_i <= c_i, 1.0, 0.0).astype(bf16), TN, 3)
    e = _head_expand()
    acs_e = _split_dot(acs, e, None, 3)
    dt_e = _split_dot(dt, e, None, 3)
    return dict(z=z, xp=xp, xc=xc, sg=sg, xbc=xbc, live=live, pre=pre, dt=dt, a=a, tril=tril,
                acs=acs, acs_t=acs_t, e=e, acs_e=acs_e, dt_e=dt_e)


def _lmat(c, h):
    seg = c["acs"][:, h:h + 1] - c["acs_t"][h:h + 1, :]
    return jnp.where(c["tril"], jnp.exp(jnp.minimum(seg, 0.0)), 0.0)


def _pair_masks():
    lane = lax.broadcasted_iota(jnp.int32, (1, 128), 1)
    return jnp.where(lane < HEAD_DIM, 1.0, 0.0), jnp.where(lane >= HEAD_DIM, 1.0, 0.0)


def _ssd_fwd(proj, conv_w, conv_b, dt_bias, a_log, d_ssd, g_ssd, nch, comm=None):
    def body(p_ref, cw_ref, cb_ref, dtb_ref, al_ref, d_ref, g_ref, y_ref, ys_ref, st_ref, xc_ref, prev8_ref, state_ref):
        i = pl.program_id(0)

        @pl.when(i == 0)
        def _():
            prev8_ref[...] = jnp.zeros_like(prev8_ref)
            state_ref[...] = jnp.zeros_like(state_ref)

        P = p_ref[...]
        c = _ssd_common(i, P, prev8_ref[...], cw_ref[...], cb_ref[...], dtb_ref[...], al_ref[...])
        prev8_ref[...] = c["xp"][CH - 8:CH]
        xc_ref[...] = c["xc"]
        xbc = c["xbc"]
        x = xbc[:, 0:D]
        xdt = x * c["dt_e"]
        a_last_e = c["acs_e"][CH - 1:CH, :]
        w_end = (xdt * jnp.exp(a_last_e - c["acs_e"])).astype(bf16)
        m0, m1 = _pair_masks()
        ys = []
        for g in range(2):
            bg = xbc[:, D + NSTATE * g:D + NSTATE * (g + 1)].astype(bf16)
            cg = xbc[:, D + 2 * NSTATE + NSTATE * g:D + 2 * NSTATE + NSTATE * (g + 1)].astype(bf16)
            gmat = _dot(cg, bg, NT)
            st = state_ref[g]
            st_ref[0, g] = st
            sl = slice(512 * g, 512 * (g + 1))
            y_off = _dot(cg, st.astype(bf16)) * jnp.exp(c["acs_e"][:, sl])
            contrib = _dot(bg, w_end[:, sl], TN)
            state_ref[g] = st * jnp.exp(a_last_e[:, sl]) + contrib
            yd = []
            for pr in range(4):
                h0 = 8 * g + 2 * pr
                xp2 = xdt[:, 128 * (4 * g + pr):128 * (4 * g + pr + 1)]
                ma = (gmat * _lmat(c, h0)).astype(bf16)
                mb = (gmat * _lmat(c, h0 + 1)).astype(bf16)
                yd.append(_dot(ma, (xp2 * m0).astype(bf16)) + _dot(mb, (xp2 * m1).astype(bf16)))
            ys.append(jnp.concatenate(yd, axis=1) + y_off)
        d_e = _split_dot(d_ref[...], c["e"], None, 3)
        y = jnp.concatenate(ys, axis=1) + x * d_e
        y_ref[...] = y
        yg = y * (c["z"] * _sigmoid(c["z"]))
        ys_ref[...] = _rms(yg, g_ref[...]).astype(bf16)

    full = lambda a: pl.BlockSpec(a.shape, lambda i, nd=a.ndim: (0,) * nd)
    return _call(
        body, comm, name="ssd_fwd", nsteps=nch,
        in_specs=[pl.BlockSpec((CH, W_PROJ), lambda i: (i, 0))] + [full(a) for a in (conv_w, conv_b, dt_bias, a_log, d_ssd, g_ssd)],
        out_specs=[pl.BlockSpec((CH, D), lambda i: (i, 0)), pl.BlockSpec((CH, D), lambda i: (i, 0)),
                   pl.BlockSpec((1, 2, NSTATE, 512), lambda i: (i, 0, 0, 0)), pl.BlockSpec((CH, D_XBC), lambda i: (i, 0))],
        out_shape=[jax.ShapeDtypeStruct((nch * CH, D), f32), jax.ShapeDtypeStruct((nch * CH, D), bf16),
                   jax.ShapeDtypeStruct((nch, 2, NSTATE, 512), f32), jax.ShapeDtypeStruct((nch * CH, D_XBC), f32)],
        scratch_shapes=[pltpu.VMEM((8, D_XBC), f32), pltpu.VMEM((2, NSTATE, 512), f32)],
        args=[proj, conv_w, conv_b, dt_bias, a_log, d_ssd, g_ssd])


def _ssd_bwd(proj, xc_all, y, dys, du, states, conv_w, conv_b, dt_bias, a_log, d_ssd, g_ssd, nch, comm=None):
    def body(p_ref, xc_ref, y_ref, dys_ref, du_ref, st_ref, cw_ref, cb_ref, dtb_ref, al_ref, d_ref, g_ref,
             dp_ref, dcw_ref, dcb_ref, ddtb_ref, dal_ref, dd_ref, dg_ref, nxt8_ref, dst_ref):
        step = pl.program_id(0)
        i = nch - 1 - step

        @pl.when(step == 0)
        def _():
            nxt8_ref[...] = jnp.zeros_like(nxt8_ref)
            dst_ref[...] = jnp.zeros_like(dst_ref)
            for r in (dcw_ref, dcb_ref, ddtb_ref, dal_ref, dd_ref, dg_ref):
                r[...] = jnp.zeros_like(r)

        P = p_ref[...]
        c = _ssd_common(i, P, None, cw_ref[...], cb_ref[...], dtb_ref[...], al_ref[...], xc=xc_ref[...])
        xbc, z, e = c["xbc"], c["z"], c["e"]
        x = xbc[:, 0:D]
        yv = y_ref[...]
        sz = _sigmoid(z)
        silu_z = z * sz
        dyg, dg8 = _rms_bwd(dys_ref[...], yv * silu_z, g_ref[...])
        dg_ref[...] += dg8
        dy = dyg * silu_z
        dz = dyg * yv * (sz * (1.0 + z * (1.0 - sz)))
        d_e = _split_dot(d_ref[...], e, None, 3)
        dd_ref[...] += _rsum8(dy * x)
        xdt = x * c["dt_e"]
        a_last_e = c["acs_e"][CH - 1:CH, :]
        e_end = jnp.exp(a_last_e - c["acs_e"])
        w_end = xdt * e_end
        e_acs = jnp.exp(c["acs_e"])
        dy_dec = dy * e_acs
        m0, m1 = _pair_masks()
        lane16 = lax.broadcasted_iota(jnp.int32, (1, HEADS), 1)
        row16 = lax.broadcasted_iota(jnp.int32, (HEADS, 1), 0)
        dacs = jnp.zeros((CH, HEADS), f32)
        dacs_t = jnp.zeros((HEADS, CH), f32)
        dxdt_parts, dbs, dcs, zparts, yoff_parts, dlast_parts = [], [], [], [], [], []
        for g in range(2):
            sl = slice(512 * g, 512 * (g + 1))
            bg = xbc[:, D + NSTATE * g:D + NSTATE * (g + 1)].astype(bf16)
            cg = xbc[:, D + 2 * NSTATE + NSTATE * g:D + 2 * NSTATE + NSTATE * (g + 1)].astype(bf16)
            gmat = _dot(cg, bg, NT)
            st = st_ref[0, g]
            dstn = dst_ref[g]
            dstn_b = dstn.astype(bf16)
            y_off = _dot(cg, st.astype(bf16)) * e_acs[:, sl]
            yoff_parts.append(y_off)
            bds = _dot(bg, dstn_b)
            zparts.append(w_end[:, sl] * bds)
            dlast_parts.append(jnp.sum(dstn * st, axis=0, keepdims=True) * jnp.exp(a_last_e[:, sl]))
            dg_acc = jnp.zeros((CH, CH), f32)
            dxd = []
            for pr in range(4):
                lo = 128 * (4 * g + pr)
                xp2 = xdt[:, lo:lo + 128].astype(bf16)
                dy2 = dy[:, lo:lo + 128]
                outp = jnp.zeros((CH, 128), f32)
                for hh, msk in ((0, m0), (1, m1)):
                    h = 8 * g + 2 * pr + hh
                    lm = _lmat(c, h)
                    dyh = (dy2 * msk).astype(bf16)
                    mh = (gmat * lm).astype(bf16)
                    outp = outp + _dot(mh, dyh, TN)
                    dml = _dot(dyh, xp2, NT) * lm
                    dg_acc = dg_acc + dml
                    q = dml * gmat
                    dacs = dacs + jnp.where(lane16 == h, jnp.sum(q, axis=1, keepdims=True), 0.0)
                    dacs_t = dacs_t + jnp.where(row16 == h, jnp.sum(q, axis=0, keepdims=True), 0.0)
                dxd.append(outp)
            dxdt_parts.append(jnp.concatenate(dxd, axis=1) + e_end[:, sl] * bds)
            dgb = dg_acc.astype(bf16)
            dcs.append(_dot(dgb, bg) + _dot(dy_dec[:, sl].astype(bf16), st.astype(bf16), NT))
            dbs.append(_dot(dgb, cg, TN) + _dot(w_end[:, sl].astype(bf16), dstn_b, NT))
            dst_ref[g] = dstn * jnp.exp(a_last_e[:, sl]) + _dot(cg, dy_dec[:, sl].astype(bf16), TN)
        dxdt = jnp.concatenate(dxdt_parts, axis=1)
        zfull = jnp.concatenate(zparts, axis=1)
        y_off_full = jnp.concatenate(yoff_parts, axis=1)
        dlast = jnp.concatenate(dlast_parts, axis=1)
        red = lambda v: _split_dot(v, e, NT, 2)
        eye16 = jnp.where(lax.broadcasted_iota(jnp.int32, (HEADS, HEADS), 0) == lax.broadcasted_iota(jnp.int32, (HEADS, HEADS), 1),
                          1.0, 0.0).astype(bf16)
        dacs = dacs - _split_dot(dacs_t, eye16, TN, 3)
        zred = red(zfull)
        dacs = dacs + red(dy * y_off_full) - zred
        last_term = jnp.sum(zred, axis=0, keepdims=True) + red(dlast)
        rowc = lax.broadcasted_iota(jnp.int32, (CH, 1), 0)
        dacs = dacs + jnp.where(rowc == CH - 1, last_term, 0.0)
        r_i = lax.broadcasted_iota(jnp.int32, (CH, CH), 0)
        c_i = lax.broadcasted_iota(jnp.int32, (CH, CH), 1)
        ddta = _split_dot(dacs, jnp.where(c_i >= r_i, 1.0, 0.0).astype(bf16), None, 3, v_is_lhs=False)
        ddt = ddta * c["a"] + red(dxdt * x)
        dal_ref[...] += _rsum8(ddta * c["dt"] * c["a"])
        ddt_raw = ddt * _sigmoid(c["pre"]) * c["live"]
        ddtb_ref[...] += _rsum8(ddt_raw)
        dx = dy * d_e + dxdt * c["dt_e"]
        dxbc = jnp.concatenate([dx, dbs[0], dbs[1], dcs[0], dcs[1]], axis=1)
        sg = c["sg"]
        dxc = dxbc * (sg * (1.0 + c["xc"] * (1.0 - sg)))
        dcb_ref[...] += _rsum8(dxc)
        xp = c["xp"]
        row8 = lax.broadcasted_iota(jnp.int32, (8, 1), 0)
        cw = cw_ref[...]
        dxp = cw[3:4] * dxc
        dcw = jnp.where(row8 == 3, jnp.sum(dxc * xp, axis=0, keepdims=True), 0.0)
        nxt8 = nxt8_ref[...]
        for j in (1, 2, 3):
            rolled = pltpu.roll(dxc, CH - j, 0)
            fix = pltpu.roll(nxt8, 8 - j, 0)
            bot = jnp.where(row8 >= 8 - j, fix, rolled[CH - 8:CH])
            later = jnp.concatenate([rolled[:CH - 8], bot], axis=0)
            dxp = dxp + cw[3 - j:4 - j] * later
            dcw = dcw + jnp.where(row8 == 3 - j, jnp.sum(later * xp, axis=0, keepdims=True), 0.0)
        dcw_ref[...] += dcw
        nxt8_ref[...] = dxc[0:8]
        dp_ref[:, O_Z:O_Z + D] = dz.astype(bf16)
        dp_ref[:, O_U:O_U + D] = du_ref[...].astype(bf16)
        dp_ref[:, O_XBC:O_XBC + D_XBC] = dxp.astype(bf16)
        dp_ref[:, O_DT:W_PROJ] = jnp.zeros((CH, W_PROJ - O_DT), bf16)
        dp_ref[:, O_DT:O_DT + HEADS] = ddt_raw.astype(bf16)

    full = lambda a: pl.BlockSpec(a.shape, lambda s, nd=a.ndim: (0,) * nd)
    rev = lambda s: (nch - 1 - s, 0)
    acc = lambda cdim: pl.BlockSpec((8, cdim), lambda s: (0, 0))
    return _call(
        body, comm, name="ssd_bwd", nsteps=nch,
        in_specs=[pl.BlockSpec((CH, W_PROJ), rev), pl.BlockSpec((CH, D_XBC), rev),
                  pl.BlockSpec((CH, D), rev), pl.BlockSpec((CH, D), rev), pl.BlockSpec((CH, D), rev),
                  pl.BlockSpec((1, 2, NSTATE, 512), lambda s: (nch - 1 - s, 0, 0, 0))]
        + [full(a) for a in (conv_w, conv_b, dt_bias, a_log, d_ssd, g_ssd)],
        out_specs=[pl.BlockSpec((CH, W_PROJ), rev), acc(D_XBC), acc(D_XBC), acc(HEADS), acc(HEADS), acc(D), acc(D)],
        out_shape=[jax.ShapeDtypeStruct((nch * CH, W_PROJ), bf16)]
        + [jax.ShapeDtypeStruct((8, cdim), f32) for cdim in (D_XBC, D_XBC, HEADS, HEADS, D, D)],
        scratch_shapes=[pltpu.VMEM((8, D_XBC), f32), pltpu.VMEM((2, NSTATE, 512), f32)],
        args=[proj, xc_all, y, dys, du, states, conv_w, conv_b, dt_bias, a_log, d_ssd, g_ssd])


SCAN_UNROLL = 8


def _to_slabs(slab_ref, q, mat):
    for ls in range(8):
        slab_ref[ls, pl.ds(PITCH * q, CH), :] = mat[:, 128 * ls:128 * (ls + 1)]


def _from_slabs(slab, q):
    return jnp.concatenate([slab(ls, PITCH * q) for ls in range(8)], axis=1)


def _tile(slab_ref, ls, t, lead=None):
    idx = (ls, pl.ds(t, 8, stride=PITCH), slice(None))
    return slab_ref[idx] if lead is None else slab_ref[(lead,) + idx]


def _s5_fwd(proj, bbq, ccq_t, ar, ai, d_skip, nch, comm=None):
    def body(u_ref, bb_ref, cc_ref, ar_ref, ai_ref, d_ref, s_ref, yl_ref, y5_ref, bu_ref, st_ref):
        @pl.when(pl.program_id(0) == 0)
        def _():
            st_ref[...] = jnp.zeros_like(st_ref)
        u = u_ref[...]
        ub = u.astype(bf16)
        for q in range(NQ):
            _to_slabs(bu_ref, q, _dot(ub[:, 128 * q:128 * (q + 1)], bb_ref[q]))
        ar_t = [ar_ref[:, 128 * l:128 * (l + 1)] for l in range(4)]
        ai_t = [ai_ref[:, 128 * l:128 * (l + 1)] for l in range(4)]

        def one(t, carry):
            re, im = carry
            nre, nim = [], []
            for l in range(4):
                a = ar_t[l] * re[l] - ai_t[l] * im[l] + _tile(bu_ref, l, t)
                b = ar_t[l] * im[l] + ai_t[l] * re[l] + _tile(bu_ref, l + 4, t)
                s_ref[0, l, pl.ds(t, 8, stride=PITCH), :] = a
                s_ref[0, l + 4, pl.ds(t, 8, stride=PITCH), :] = b
                nre.append(a)
                nim.append(b)
            return tuple(nre), tuple(nim)

        def step(tt, carry):
            for k in range(SCAN_UNROLL):
                carry = one(tt * SCAN_UNROLL + k, carry)
            return carry
        init = (tuple(st_ref[l] for l in range(4)), tuple(st_ref[l + 4] for l in range(4)))
        re, im = lax.fori_loop(0, CH // SCAN_UNROLL, step, init)
        for l in range(4):
            st_ref[l] = re[l]
            st_ref[l + 4] = im[l]
        ys = []
        for q in range(NQ):
            sq = _from_slabs(lambda ls, r0: s_ref[0, ls, pl.ds(r0, CH), :], q).astype(bf16)
            ys.append(_dot(sq, cc_ref[q]))
        yl = jnp.concatenate(ys, axis=1) + u * d_ref[...]
        yl_ref[...] = yl
        y5_ref[...] = (0.5 * yl * (1.0 + lax.erf(yl * (1.0 / math.sqrt(2.0))))).astype(bf16)

    const = lambda a: pl.BlockSpec(a.shape, lambda i, nd=a.ndim: (0,) * nd)
    return _call(
        body, comm, name="s5_fwd", nsteps=nch,
        in_specs=[pl.BlockSpec((CH, D), lambda i: (i, O_U // D)), const(bbq), const(ccq_t), const(ar), const(ai), const(d_skip)],
        out_specs=[pl.BlockSpec((1, 8, 8 * PITCH, 128), lambda i: (i, 0, 0, 0)),
                   pl.BlockSpec((CH, D), lambda i: (i, 0)), pl.BlockSpec((CH, D), lambda i: (i, 0))],
        out_shape=[jax.ShapeDtypeStruct((nch, 8, 8 * PITCH, 128), f32), jax.ShapeDtypeStruct((nch * CH, D), f32),
                   jax.ShapeDtypeStruct((nch * CH, D), bf16)],
        scratch_shapes=[pltpu.VMEM((8, 8 * PITCH, 128), f32), pltpu.VMEM((8, 8, 128), f32)],
        args=[proj, bbq, ccq_t, ar, ai, d_skip])


def _s5_bwd(proj, dyl, s_all, bbtq, cctq, ar, ai, d_skip, nch, comm=None):
    def body(u_ref, dy_ref, s_ref, bbt_ref, cct_ref, ar_ref, ai_ref, d_ref,
             du_ref, dcc_ref, dbb_ref, dab_ref, dd_ref, ga_ref, st_ref):
        @pl.when(pl.program_id(0) == 0)
        def _():
            st_ref[...] = jnp.zeros_like(st_ref)
            for r in (dcc_ref, dbb_ref, dab_ref, dd_ref):
                r[...] = jnp.zeros_like(r)
        u = u_ref[...]
        dyl_v = dy_ref[...]
        dd_ref[...] += _rsum8(dyl_v * u)
        ub = u.astype(bf16)
        dyb = dyl_v.astype(bf16)
        for q in range(NQ):
            _to_slabs(ga_ref, q, _dot(dyb[:, 128 * q:128 * (q + 1)], cct_ref[q]))
        ar_t = [ar_ref[:, 128 * l:128 * (l + 1)] for l in range(4)]
        ai_t = [ai_ref[:, 128 * l:128 * (l + 1)] for l in range(4)]

        def one(t, carry):
            re, im, dar, dai = carry
            nre, nim, ndar, ndai = [], [], [], []
            for l in range(4):
                sre = _tile(s_ref, l, t, lead=0)
                sim = _tile(s_ref, l + 4, t, lead=0)
                ndar.append(dar[l] + re[l] * sre + im[l] * sim)
                ndai.append(dai[l] + im[l] * sre - re[l] * sim)
                a = _tile(ga_ref, l, t) + ar_t[l] * re[l] + ai_t[l] * im[l]
                b = _tile(ga_ref, l + 4, t) - ai_t[l] * re[l] + ar_t[l] * im[l]
                ga_ref[l, pl.ds(t, 8, stride=PITCH), :] = a
                ga_ref[l + 4, pl.ds(t, 8, stride=PITCH), :] = b
                nre.append(a)
                nim.append(b)
            return tuple(nre), tuple(nim), tuple(ndar), tuple(ndai)

        def step(tt, carry):
            for k in range(SCAN_UNROLL):
                carry = one(CH - 1 - (tt * SCAN_UNROLL + k), carry)
            return carry
        four = lambda ref, o: tuple(ref[l + o] for l in range(4))
        re, im, dar, dai = lax.fori_loop(0, CH // SCAN_UNROLL, step,
                                         (four(st_ref, 0), four(st_ref, 4), four(dab_ref, 0), four(dab_ref, 4)))
        for l in range(4):
            st_ref[l], st_ref[l + 4] = re[l], im[l]
            dab_ref[l], dab_ref[l + 4] = dar[l], dai[l]
        dus = []
        for q in range(NQ):
            aq = _from_slabs(lambda ls, r0: ga_ref[ls, pl.ds(r0, CH), :], q).astype(bf16)
            sq = _from_slabs(lambda ls, r0: s_ref[0, ls, pl.ds(r0, CH), :], q).astype(bf16)
            dcc_ref[q] += _dot(dyb[:, 128 * q:128 * (q + 1)], sq, TN)
            dbb_ref[q] += _dot(ub[:, 128 * q:128 * (q + 1)], aq, TN)
            dus.append(_dot(aq, bbt_ref[q]))
        du_ref[...] = jnp.concatenate(dus, axis=1) + dyl_v * d_ref[...]

    const = lambda a: pl.BlockSpec(a.shape, lambda s, nd=a.ndim: (0,) * nd)
    rev = lambda s: (nch - 1 - s, 0)
    return _call(
        body, comm, name="s5_bwd", nsteps=nch,
        in_specs=[pl.BlockSpec((CH, D), lambda s: (nch - 1 - s, O_U // D)), pl.BlockSpec((CH, D), rev),
                  pl.BlockSpec((1, 8, 8 * PITCH, 128), lambda s: (nch - 1 - s, 0, 0, 0)),
                  const(bbtq), const(cctq), const(ar), const(ai), const(d_skip)],
        out_specs=[pl.BlockSpec((CH, D), rev), pl.BlockSpec((NQ, 128, D), lambda s: (0, 0, 0)),
                   pl.BlockSpec((NQ, 128, D), lambda s: (0, 0, 0)), pl.BlockSpec((8, 8, 128), lambda s: (0, 0, 0)),
                   pl.BlockSpec((8, D), lambda s: (0, 0))],
        out_shape=[jax.ShapeDtypeStruct((nch * CH, D), f32), jax.ShapeDtypeStruct((NQ, 128, D), f32),
                   jax.ShapeDtypeStruct((NQ, 128, D), f32), jax.ShapeDtypeStruct((8, 8, 128), f32),
                   jax.ShapeDtypeStruct((8, D), f32)],
        scratch_shapes=[pltpu.VMEM((8, 8 * PITCH, 128), f32), pltpu.VMEM((8, 8, 128), f32)],
        args=[proj, dyl, s_all, bbtq, cctq, ar, ai, d_skip])


def _s5_tables(lam_re, lam_im, log_step, b_re, b_im):
    step = jnp.exp(log_step)[:, None]
    mag = jnp.exp(lam_re * step)
    ab_re = mag * jnp.cos(lam_im * step)
    ab_im = mag * jnp.sin(lam_im * step)
    den = lam_re * lam_re + lam_im * lam_im
    coef_re = ((ab_re - 1.0) * lam_re + ab_im * lam_im) / den
    coef_im = (ab_im * lam_re - (ab_re - 1.0) * lam_im) / den
    bb_re = coef_re[..., None] * b_re - coef_im[..., None] * b_im
    bb_im = coef_re[..., None] * b_im + coef_im[..., None] * b_re
    return ab_re, ab_im, bb_re, bb_im


def _blockdiag_in(m_re, m_im):
    eye = jnp.eye(8, dtype=f32)

    def one(m):
        m = m.reshape(NQ, 8, S5_P, 16)
        return jnp.einsum("qgph,gk->qghkp", m, eye).reshape(NQ, 128, 512)
    return jnp.concatenate([one(m_re), one(m_im)], axis=2)


def _blockdiag_in_grad(dm):
    def one(x):
        x = x.reshape(NQ, 8, 16, 8, S5_P)
        return jnp.einsum("qghgp->qgph", x).reshape(NQ * 8, S5_P, 16)
    return one(dm[:, :, :512]), one(dm[:, :, 512:])


def _local_step(x2, tgt2, meta, p, w_in_t, slab):
    seq = x2.shape[0]
    nch = 1 + seq // CH
    bmb = BIG_ROWS if (nch * CH) % BIG_ROWS == 0 else CH
    nbig = nch * CH // bmb
    metablk = jnp.concatenate([jnp.zeros((CH - N_META, D), f32), meta], axis=0)
    w_full = (w_in_t, pl.BlockSpec(w_in_t.shape, lambda i: (0, 0), pipeline_mode=pl.Buffered(1)))
    h0_of = lambda i, s, q: jnp.where(i == 0, q[0], s[0])

    def in_fn(i, r, s, q, w):
        nb = _rms(h0_of(i, s, q), q[1]).astype(bf16)
        return [_dot(nb, w[0][...], NT), nb], [], [], []
    (proj, n0), _, _, _, (g_up,) = _rowwise("in_proj", in_fn, nch, shifted=[x2], pars=[metablk, p["g_mix"]], refs=[w_full],
                                            out_rows=[(W_PROJ, f32), (D, bf16)], comm=_gather_piece(slab, R_UP, 1024))
    (y, y_ssd, states, xc_all), (g_down,) = _ssd_fwd(proj, p["conv_w"], p["conv_b"], p["dt_bias"], p["a_log"], p["d_ssd"],
                                                     p["g_ssd"], nch, comm=_gather_piece(slab, R_DOWN, 1024))

    ab_re, ab_im, bb_re, bb_im = _s5_tables(p["lam_re"], p["lam_im"], p["log_step"], p["b_re"], p["b_im"])
    ar, ai = ab_re.reshape(NQ, 512), ab_im.reshape(NQ, 512)
    bbq = _blockdiag_in(bb_re, bb_im)
    ccq = _blockdiag_in(jnp.swapaxes(p["c_re"], 1, 2), -jnp.swapaxes(p["c_im"], 1, 2))
    d_skip = p["d_s5"].reshape(1, D)
    (s_all, ylin, y5), (g_go,) = _s5_fwd(proj, bbq.astype(bf16), jnp.swapaxes(ccq, 1, 2).astype(bf16), ar, ai, d_skip, nch,
                                         comm=_gather_piece(slab, R_GLU, 1024))
    whole = lambda a: (a, pl.BlockSpec(a.shape, lambda i: (0, 0, 0), pipeline_mode=pl.Buffered(1)))
    w_up, w_down = whole(g_up), whole(g_down)
    w_glu_t = (g_go, pl.BlockSpec((4, 512, D), lambda i: (0, 0, 0), pipeline_mode=pl.Buffered(1)))
    w_out = (g_go, pl.BlockSpec((4, 512, D), lambda i: (0, 1, 0), pipeline_mode=pl.Buffered(1)))

    def glu_fn(i, r, s, q, w):
        v = jnp.concatenate([_dot(r[0], w[0][k], NT) for k in range(4)], axis=1) + q[0]
        return [v, _rms(v[:, :D] * _sigmoid(v[:, D:]), q[1])], [], [], []
    (v, y_s5), _, _, _ = _rowwise("glu", glu_fn, nbig, rows=[y5], pars=[p["b_glu"], p["g_s5"]], refs=[w_glu_t],
                                  out_rows=[(2 * D, f32), (D, bf16)], bm=bmb)

    def out_fn(i, r, s, q, w):
        acc = (_dot(r[0][:, :512], w[0][0]) + _dot(r[0][:, 512:], w[0][1]) + _dot(r[1][:, :512], w[0][2])
               + _dot(r[1][:, 512:], w[0][3]))
        return [h0_of(i, s, q) + acc], [], [], []
    (h1,), _, _, _ = _rowwise("out_proj", out_fn, nch, rows=[y_ssd, y_s5], shifted=[x2], pars=[metablk], refs=[w_out],
                              out_rows=[(D, f32)])

    def up_fn(i, r, s, q, w):
        nb = _rms(r[0], q[0]).astype(bf16)
        act = [jnp.square(jnp.maximum(_dot(nb, w[0][k]), 0.0)).astype(bf16) for k in range(4)]
        return [jnp.concatenate(act, axis=1), nb], [], [], []
    (act, n1), _, _, _ = _rowwise("up_proj", up_fn, nbig, rows=[h1], pars=[p["g_mlp"]], refs=[w_up],
                                  out_rows=[(4 * D, bf16), (D, bf16)], bm=bmb)

    def down_fn(i, r, s, q, w):
        acc = _dot(r[0][:, :D], w[0][0])
        for k in range(1, 4):
            acc = acc + _dot(r[0][:, D * k:D * (k + 1)], w[0][k])
        return [r[1] + acc], [], [], []
    (h2,), _, _, _ = _rowwise("down_proj", down_fn, nbig, rows=[act, h1], refs=[w_down], out_rows=[(D, f32)], bm=bmb)

    def final_fn(i, r, s, q, w):
        live = jnp.where(i > 0, 1.0, 0.0)
        err = (_rms(r[0], q[0]) - s[0]) * live
        dh, dg8 = _rms_bwd(err * (1.0 / D), r[0], q[0])
        return [dh, dh], [_rsum8(err * err), dg8], [], []
    (dh2, dh2_b), (loss8, dgf8), _, _ = _rowwise("final", final_fn, nch, rows=[h2], shifted=[tgt2], pars=[p["g_final"]],
                                                  out_rows=[(D, f32), (D, bf16)], out_accs=[D, D])
    loss = 0.5 / D * jnp.sum(loss8)

    def down_bwd_fn(i, r, s, q, w):
        dm_ = [_dot(r[0], w[0][k], NT) * 2.0 * jnp.sqrt(r[1][:, D * k:D * (k + 1)].astype(f32)) for k in range(4)]
        return [jnp.concatenate(dm_, axis=1)], [], [], []
    (dm,), _, _, _ = _rowwise("down_bwd", down_bwd_fn, nbig, rows=[dh2_b, act], refs=[w_down], out_rows=[(4 * D, bf16)], bm=bmb)
    g_a = _dw_into("dw_down", act, dh2_b, None, 1024, True, 1, 4, 0, piece_rows=2048)

    def up_bwd_fn(i, r, s, q, w):
        acc = _dot(r[0][:, :D], w[0][0], NT)
        for k in range(1, 4):
            acc = acc + _dot(r[0][:, D * k:D * (k + 1)], w[0][k], NT)
        dh, dg8 = _rms_bwd(acc, r[1], q[0])
        dh1_ = r[2] + dh
        return [dh1_, dh1_], [dg8], [], []
    (dh1, dh1_b), (dgmlp8,), _, _ = _rowwise("up_bwd", up_bwd_fn, nbig, rows=[dm, h1, dh2], pars=[p["g_mlp"]], refs=[w_up],
                                             out_rows=[(D, f32), (D, bf16)], out_accs=[D], bm=bmb)
    g_a = _dw_into("dw_up", n1, dm, g_a, 1024, False, 0, 4, 0, piece_rows=2048)

    def out_bwd_fn(i, r, s, q, w):
        dmix = [_dot(r[0], w[0][k], NT) for k in range(4)]
        v1, v2 = r[1][:, :D], r[1][:, D:]
        s2 = _sigmoid(v2)
        dglu, dg8 = _rms_bwd(jnp.concatenate(dmix[2:], axis=1), v1 * s2, q[0])
        dv = jnp.concatenate([dglu * s2, dglu * v1 * s2 * (1.0 - s2)], axis=1)
        return [jnp.concatenate(dmix[:2], axis=1), dv], [dg8, _rsum8(dv)], [], []
    (dys, dv), (dgs58, dbglu8), _, _ = _rowwise("out_bwd", out_bwd_fn, nbig, rows=[dh1_b, v], pars=[p["g_s5"]], refs=[w_out],
                                                out_rows=[(D, f32), (2 * D, bf16)], out_accs=[D, 2 * D], bm=bmb)
    g_b = _dw_into("dw_out_a", y_ssd, dh1_b, None, 512, True, 1, 2, 0, piece_rows=1024)
    g_b = _dw_into("dw_out_b", y_s5, dh1_b, g_b, 512, True, 1, 2, 2, piece_rows=1024)

    def glu_bwd_fn(i, r, s, q, w):
        acc = _dot(r[0][:, :512], w[0][0])
        for k in range(1, 4):
            acc = acc + _dot(r[0][:, 512 * k:512 * (k + 1)], w[0][k])
        yl = r[1]
        cdf = 0.5 * (1.0 + lax.erf(yl * (1.0 / math.sqrt(2.0))))
        pdf = jnp.exp(-0.5 * yl * yl) * (1.0 / math.sqrt(2.0 * math.pi))
        return [acc * (cdf + yl * pdf)], [], [], []
    (dylin,), _, _, _ = _rowwise("glu_bwd", glu_bwd_fn, nbig, rows=[dv, ylin], refs=[w_glu_t], out_rows=[(D, f32)], bm=bmb)
    g_b = _dw_into("dw_glu", dv, y5, g_b, 512, True, 0, 4, 0, piece_rows=1024)

    (du, dcc, dbb, dab, dds5), (land_a,) = _s5_bwd(proj, dylin, s_all, jnp.swapaxes(bbq, 1, 2).astype(bf16), ccq.astype(bf16),
                                                   ar, ai, d_skip, nch, comm=_scatter_piece(g_a))
    (dproj, dcw8, dcb8, ddtb8, dal8, dd8, dgssd8), (land_b,) = _ssd_bwd(
        proj, xc_all, y, dys, du, states, p["conv_w"], p["conv_b"], p["dt_bias"], p["a_log"], p["d_ssd"], p["g_ssd"], nch,
        comm=_scatter_piece(g_b))

    gt = _dw_in_t(dproj, n0)
    gt = jnp.concatenate([gt[0:1024], gt[O_XBC:O_XBC + D_XBC], gt[O_DT:O_DT + HEADS], gt[O_U:O_U + D]], axis=0).reshape(4, 900, D)
    g_c = jnp.concatenate([gt, jnp.zeros((4, 1024 - 900, D), bf16)], axis=1)

    def in_bwd_fn(i, r, s, q, w):
        dh, dg8 = _rms_bwd(_dot(r[0], w[0][...]), h0_of(i, s, q), q[1])
        dh0 = r[1] + dh
        return [], [dg8], [dh0], [dh0]
    _, (dgmix8,), (dmeta_blk,), (grad_x,), (land_c,) = _rowwise(
        "in_bwd", in_bwd_fn, nch, rows=[dproj, dh1], shifted=[x2], pars=[metablk, p["g_mix"]], refs=[w_full], out_accs=[D],
        out_first=[(D, f32)], out_shifted=[(D, f32)], comm=_scatter_piece(g_c))

    dab_q = jnp.swapaxes(dab.reshape(2, 4, NQ, 128), 1, 2).reshape(2, NQ * 8, S5_P)
    dbb_re, dbb_im = _blockdiag_in_grad(dbb)
    dcr, dci = _blockdiag_in_grad(dcc)
    _, vjp = jax.vjp(_s5_tables, p["lam_re"], p["lam_im"], p["log_step"], p["b_re"], p["b_im"])
    dlam_re, dlam_im, dlog_step, db_re, db_im = vjp((dab_q[0], dab_q[1], dbb_re, dbb_im))

    s8 = lambda a: jnp.sum(a, axis=0, keepdims=True)
    hsum = lambda a: jnp.sum(s8(a).reshape(HEADS, HEAD_DIM), axis=1).reshape(1, HEADS)
    small = dict(
        g_mix=s8(dgmix8), conv_w=dcw8[0:4], conv_b=s8(dcb8), dt_bias=s8(ddtb8), a_log=s8(dal8),
        d_ssd=hsum(dd8), g_ssd=s8(dgssd8), lam_re=dlam_re, lam_im=dlam_im, log_step=dlog_step, b_re=db_re, b_im=db_im,
        c_re=jnp.swapaxes(dcr, 1, 2), c_im=-jnp.swapaxes(dci, 1, 2), d_s5=s8(dds5).reshape(NQ * 8, 16),
        b_glu=s8(dbglu8), g_s5=s8(dgs58), g_mlp=s8(dgmlp8), g_final=s8(dgf8).reshape(D))
    return loss, grad_x, dmeta_blk, [(g_a, land_a), (g_b, land_b), (g_c, land_c)], small


def _perm_rows_w_in(wt):
    return jnp.concatenate([wt[0:1024], wt[2576:3600], wt[1024:2560], wt[2560:2576],
                            jnp.zeros((W_PROJ - 3600, wt.shape[1]), wt.dtype)], axis=0)


def _unperm_cols_w_in(g):
    return jnp.concatenate([g[:, 0:1024], g[:, O_XBC:O_XBC + D_XBC], g[:, O_DT:O_DT + HEADS], g[:, O_U:O_U + D]], axis=1)


def _pack_shard(w_in, w_glu, w_out, w_up, w_down, spare):
    dt = w_in.dtype
    parts = [w_up, w_down, w_glu.T, w_out, w_in.T, spare,
             jnp.zeros((PACK_ROWS - R_SPARE - spare.shape[0], D), dt)]
    return jnp.concatenate(parts, axis=0)


def _allgather8(x_shard, name):
    m_per, n = x_shard.shape

    def body(x_ref, out_ref, send_sems, recv_sems, stage, local_sems):
        x, y, c = _place()
        me, sibling = (x, y, c), (x, y, 1 - c)
        chips = [(1 - x, y), (x, 1 - y), (1 - x, 1 - y)]

        def rows(px, py, pc):
            return out_ref.at[pl.ds((4 * px + 2 * py + pc) * m_per, m_per), :]

        def copy(k, block, to, src=None):
            return pltpu.make_async_remote_copy(
                src_ref=rows(*block) if src is None else src, dst_ref=rows(*block),
                send_sem=send_sems.at[k], recv_sem=recv_sems.at[k], device_id=to, device_id_type=MESH)

        load = pltpu.make_async_copy(x_ref, stage, local_sems.at[0])
        load.start()
        first = [copy(0, me, sibling, src=x_ref)]
        first += [copy(1 + j, me, (*chip, c), src=x_ref) for j, chip in enumerate(chips)]
        for cp in first:
            cp.start()
        load.wait()
        store = pltpu.make_async_copy(stage, rows(*me), local_sems.at[1])
        store.start()
        passed = [copy(4 + j, (*chip, c), sibling) for j, chip in enumerate(chips)]
        for j, chip in enumerate(chips):
            copy(1 + j, (*chip, c), me).wait_recv()
            passed[j].start()
        copy(0, sibling, me).wait_recv()
        for j, chip in enumerate(chips):
            copy(4 + j, (*chip, 1 - c), me).wait_recv()
        for cp in first + passed:
            cp.wait_send()
        store.wait()

    return pl.pallas_call(
        body, name=name, out_shape=jax.ShapeDtypeStruct((8 * m_per, n), x_shard.dtype),
        in_specs=[_ANY], out_specs=_ANY,
        scratch_shapes=[pltpu.SemaphoreType.DMA((7,)), pltpu.SemaphoreType.DMA((7,)), pltpu.VMEM((m_per, n), x_shard.dtype),
                        pltpu.SemaphoreType.DMA((2,))])(x_shard)


def _swap_sibling(r, name):
    def body(r_ref, out_ref, send_sem, recv_sem):
        x, y, c = _place()
        cp = pltpu.make_async_remote_copy(src_ref=r_ref, dst_ref=out_ref, send_sem=send_sem, recv_sem=recv_sem,
                                          device_id=(x, y, 1 - c), device_id_type=MESH)
        cp.start()
        cp.wait()

    return pl.pallas_call(
        body, name=name, out_shape=jax.ShapeDtypeStruct(r.shape, r.dtype), in_specs=[_ANY], out_specs=_ANY,
        scratch_shapes=[pltpu.SemaphoreType.DMA, pltpu.SemaphoreType.DMA])(r)


SH_CONVW, SH_META = 4 * 384, 16 * 256
SPARE_ROWS = 17

SMALL = [("g_mix", (1, 1024)), ("conv_b", (1, 1536)), ("dt_bias", (1, 16)), ("a_log", (1, 16)), ("d_ssd", (1, 16)),
         ("g_ssd", (1, 1024)), ("lam_re", (1, 64, 64)), ("lam_im", (1, 64, 64)), ("log_step", (1, 64)),
         ("b_re", (1, 64, 64, 16)), ("b_im", (1, 64, 64, 16)), ("c_re", (1, 64, 16, 64)), ("c_im", (1, 64, 16, 64)),
         ("d_s5", (1, 64, 16)), ("b_glu", (1, 2048)), ("g_s5", (1, 1024)), ("g_mlp", (1, 1024)), ("g_final", (1024,))]


def _pack_small(arrs, rows):
    flat = jnp.concatenate([a.reshape(-1).astype(f32) for a in arrs])
    return jnp.concatenate([flat, jnp.zeros((rows * D - flat.shape[0],), f32)]).reshape(rows, D)


def _unpack_small(slab, shapes):
    flat = slab.reshape(-1)
    out, o = [], 0
    for shp in shapes:
        n = math.prod(shp)
        out.append(flat[o:o + n].reshape(shp))
        o += n
    return out


def _sum8(g, rows):
    def body(g_ref, o_ref):
        acc = g_ref[0]
        for k in range(1, 8):
            acc = acc + g_ref[k]
        o_ref[...] = acc
    return pl.pallas_call(body, name="sum8", out_shape=jax.ShapeDtypeStruct((rows, D), f32),
                          compiler_params=_cp())(g.reshape(8, rows, D))


def _adam_math(w_, g_, m_, v_):
    m2 = ADAM_B1 * m_ + (1.0 - ADAM_B1) * g_
    v2 = ADAM_B2 * v_ + (1.0 - ADAM_B2) * jnp.square(g_)
    m_hat = m2 / (1.0 - ADAM_B1 ** ADAM_STEP)
    v_hat = v2 / (1.0 - ADAM_B2 ** ADAM_STEP)
    delta = -ADAM_LR * (m_hat / (jnp.sqrt(v_hat) + ADAM_EPS) + ADAM_WD * w_)
    return delta, m2, v2


def _adamw(name, w, g, m, v, bm):
    def fn(i, r, s, q, refs):
        return list(_adam_math(*r)), [], [], []
    c = w.shape[1]
    (d, m2, v2), _, _, _ = _rowwise(name, fn, w.shape[0] // bm, rows=[w, g, m, v], out_rows=[(c, f32)] * 3, bm=bm)
    return d, m2, v2


def _adamw_whole(name, w, g, m, v):
    def body(w_ref, g_ref, m_ref, v_ref, d_ref, m2_ref, v2_ref):
        d_ref[...], m2_ref[...], v2_ref[...] = _adam_math(w_ref[...], g_ref[...], m_ref[...], v_ref[...])
    return pl.pallas_call(body, name=name, out_shape=[jax.ShapeDtypeStruct(w.shape, f32)] * 3, compiler_params=_cp())(w, g, m, v)


def _sum_parts(name, own, land):
    def fn(i, r, s, q, refs):
        acc = r[0].astype(f32)
        for k in range(7):
            acc = acc + refs[0][k].astype(f32)
        return [acc], [], [], []
    (o,), _, _, _ = _rowwise(name, fn, own.shape[0] // CH, rows=[own],
                             refs=[(land, pl.BlockSpec((7, CH, D), lambda i: (0, i, 0)))], out_rows=[(D, f32)])
    return o


def kernel(x, meta_tokens, g_mix, w_in, conv_w, conv_b, dt_bias, a_log, d_ssd, g_ssd, lam_re, lam_im, log_step, b_re, b_im, c_re, c_im, d_s5, w_glu, b_glu, g_s5, w_out, g_mlp, w_up, w_down, g_final, loss_target, m_meta_tokens, m_g_mix, m_w_in, m_conv_w, m_conv_b, m_dt_bias, m_a_log, m_d_ssd, m_g_ssd, m_lam_re, m_lam_im, m_log_step, m_b_re, m_b_im, m_c_re, m_c_im, m_d_s5, m_w_glu, m_b_glu, m_g_s5, m_w_out, m_g_mlp, m_w_up, m_w_down, m_g_final, v_meta_tokens, v_g_mix, v_w_in, v_conv_w, v_conv_b, v_dt_bias, v_a_log, v_d_ssd, v_g_ssd, v_lam_re, v_lam_im, v_log_step, v_b_re, v_b_im, v_c_re, v_c_im, v_d_s5, v_w_glu, v_b_glu, v_g_s5, v_w_out, v_g_mlp, v_w_up, v_w_down, v_g_final):
    given = dict(locals())
    cx, cy, cc = _place()
    chip = 2 * cx + cy

    small_f = jnp.concatenate([conv_w.reshape(-1), meta_tokens.reshape(-1)])
    t_hi = small_f.astype(bf16)
    r_1 = small_f - t_hi.astype(f32)
    t_mid = r_1.astype(bf16)
    t_lo = (r_1 - t_mid.astype(f32)).astype(bf16)
    terms = jnp.concatenate([t_hi, t_mid, t_lo])
    spare = jnp.concatenate([terms, jnp.zeros((SPARE_ROWS * D - terms.shape[0],), bf16)]).reshape(SPARE_ROWS, D)
    slab = _pack_shard(w_in[0].astype(bf16), w_glu[0].astype(bf16), w_out[0].astype(bf16), w_up[0].astype(bf16),
                       w_down[0].astype(bf16), spare)
    my_half = lax.dynamic_slice_in_dim(slab, R_IN + cc * 512, 512, axis=0)
    gathered = _allgather8(my_half, "gather_w_in").reshape(4, 1024, D)
    w_in_t = _perm_rows_w_in(jnp.concatenate([gathered[s, 0:900] for s in range(4)], axis=0))
    n_sf = SH_CONVW + SH_META
    tr = gathered[:, 900:900 + SPARE_ROWS].reshape(4, SPARE_ROWS * D)[:, :3 * n_sf].astype(f32).reshape(4, 3, n_sf)
    sp = tr[:, 0] + tr[:, 1] + tr[:, 2]
    conv_w_full = jnp.concatenate([sp[s, :SH_CONVW].reshape(4, 384) for s in range(4)], axis=1)
    meta_full = jnp.concatenate([sp[s, SH_CONVW:].reshape(16, 256) for s in range(4)], axis=1)

    p = dict(g_mix=g_mix, conv_w=conv_w_full, conv_b=conv_b, dt_bias=dt_bias, a_log=a_log, d_ssd=d_ssd, g_ssd=g_ssd,
             lam_re=lam_re[0], lam_im=lam_im[0], log_step=log_step[0], b_re=b_re[0], b_im=b_im[0], c_re=c_re[0], c_im=c_im[0],
             d_s5=d_s5[0], b_glu=b_glu, g_s5=g_s5, g_mlp=g_mlp, g_final=g_final.reshape(1, D))
    loss_part, grad_x, dmeta_blk, pieces, g = _local_step(x[0], loss_target[0], meta_full, p, w_in_t, slab)
    grad_x = grad_x.reshape(x.shape)

    reds = []
    for k, (gp, land) in enumerate(pieces):
        half = gp.shape[1] // 2
        own = lax.dynamic_slice(gp, (chip, cc * half, 0), (1, half, D)).reshape(half, D)
        reds.append(_sum_parts("rs_sum_%d" % k, own, land))
    red = jnp.concatenate(reds, axis=0)
    other = _swap_sibling(red, "rs_share")
    first = jnp.where(cc == 0, red, other)
    second = jnp.where(cc == 0, other, red)
    g_up, g_down = first[0:1024], second[0:1024]
    g_glu, g_out = first[1024:1536].T, second[1024:1536]
    g_in_t = jnp.concatenate([first[1536:2048], second[1536:1536 + 900 - 512]], axis=0)

    small_g = [g[n] for n, _ in SMALL] + [g["conv_w"], dmeta_blk[CH - N_META:CH], loss_part.reshape(1)]
    n_small = sum(math.prod(s) for _, s in SMALL) + 4 * D_XBC + N_META * D + 1
    rows_small = -(-n_small // (8 * D)) * 8
    total = _sum8(_allgather8(_pack_small(small_g, rows_small), "gather_small"), rows_small)
    outs = _unpack_small(total, [s for _, s in SMALL] + [(4, D_XBC), (N_META, D), ()])
    gs = {n: o for (n, _), o in zip(SMALL, outs)}
    g_conv_w = lax.dynamic_slice_in_dim(outs[-3], chip * 384, 384, axis=1).reshape(conv_w.shape)
    g_meta = lax.dynamic_slice_in_dim(outs[-2], chip * 256, 256, axis=1)
    loss = outs[-1]

    grads = dict(gs, meta_tokens=g_meta, conv_w=g_conv_w, w_in=g_in_t.T.reshape(w_in.shape), w_glu=g_glu.reshape(w_glu.shape),
                 w_out=g_out.reshape(w_out.shape), w_up=g_up.reshape(w_up.shape), w_down=g_down.reshape(w_down.shape))
    delta, new_m, new_v = {}, {}, {}
    d_, m_, v_ = _adamw_whole("adamw_w_in", w_in[0].T, g_in_t, m_w_in[0].T, v_w_in[0].T)
    delta["w_in"], new_m["w_in"], new_v["w_in"] = (a.T.reshape(w_in.shape) for a in (d_, m_, v_))
    for n in ("w_glu", "w_out", "w_up", "w_down"):
        shp = given[n].shape
        two = lambda a: a.reshape(shp[1], shp[2])
        d_, m_, v_ = _adamw("adamw_" + n, two(given[n]), two(grads[n]), two(given["m_" + n]), two(given["v_" + n]), 256)
        delta[n], new_m[n], new_v[n] = d_.reshape(shp), m_.reshape(shp), v_.reshape(shp)
    for n in [n for n, _ in SMALL] + ["conv_w", "meta_tokens"]:
        shp = given[n].shape
        two = (lambda a: a.reshape(1, -1)) if len(shp) == 1 else (lambda a: a)
        d_, m_, v_ = _adamw_whole("adamw_" + n, two(given[n]), two(grads[n].reshape(shp)), two(given["m_" + n]), two(given["v_" + n]))
        delta[n], new_m[n], new_v[n] = d_.reshape(shp), m_.reshape(shp), v_.reshape(shp)

    order = ["meta_tokens", "g_mix", "w_in", "conv_w", "conv_b", "dt_bias", "a_log", "d_ssd", "g_ssd", "lam_re", "lam_im", "log_step",
             "b_re", "b_im", "c_re", "c_im", "d_s5", "w_glu", "b_glu", "g_s5", "w_out", "g_mlp", "w_up", "w_down", "g_final"]
    grads_out = [grads[n].reshape(given[n].shape) for n in order]
    return (loss, grad_x, *grads_out, *[delta[n] for n in order], *[new_m[n] for n in order], *[new_v[n] for n in order])
```

```python
import math

import jax
import jax.numpy as jnp
from jax import lax
from jax.experimental import pallas as pl
from jax.experimental.pallas import tpu as pltpu

f32 = jnp.float32
bf16 = jnp.bfloat16

D = 1024
N_META = 16
CH = 256
HEADS = 16
HEAD_DIM = 64
NSTATE = 128
D_XBC = 1536
S5_P = 64
NQ = 8
PITCH = CH + 4
EPS = 1e-5
O_Z, O_U, O_XBC, O_DT, W_PROJ = 0, 1024, 2048, 3584, 3712
VMEM_LIMIT = 60 * 1024 * 1024

ADAM_LR, ADAM_B1, ADAM_B2, ADAM_EPS, ADAM_WD, ADAM_STEP = 0.001, 0.9, 0.999, 1e-08, 0.01, 10

NT = (((1,), (1,)), ((), ()))
TN = (((0,), (0,)), ((), ()))
_ANY = pl.BlockSpec(memory_space=pl.ANY)


def _cp(sem=None):
    return pltpu.CompilerParams(dimension_semantics=sem, vmem_limit_bytes=VMEM_LIMIT)


def _sigmoid(v):
    return 1.0 / (1.0 + jnp.exp(-v))


def _rsum8(v):
    r, c = v.shape
    return jnp.sum(v.reshape(r // 8, 8, c), axis=0)


def _rms(h, g):
    r = lax.rsqrt(jnp.mean(h * h, axis=-1, keepdims=True) + EPS)
    return h * r * g


def _rms_bwd(dy, h, g):
    r = lax.rsqrt(jnp.mean(h * h, axis=-1, keepdims=True) + EPS)
    n = h * r
    dn = dy * g
    dh = r * (dn - n * jnp.mean(dn * n, axis=-1, keepdims=True))
    return dh, _rsum8(dy * n)


def _dot(a, b, dims=None):
    if dims is None:
        return jnp.dot(a, b, preferred_element_type=f32)
    return lax.dot_general(a, b, dims, preferred_element_type=f32)


def _split_dot(v, m01, dims, terms, v_is_lhs=True):
    out, r = None, v
    for _ in range(terms):
        piece = r.astype(bf16)
        o = _dot(piece, m01, dims) if v_is_lhs else _dot(m01, piece, dims)
        out = o if out is None else out + o
        r = r - piece.astype(f32)
    return out


MESH = pl.DeviceIdType.MESH


def _place():
    return lax.axis_index("x"), lax.axis_index("y"), lax.axis_index("c")


def _flip(v, f):
    return 1 - v if f else v


def _call(body, comm, *, name, nsteps, in_specs, out_specs, out_shape, scratch_shapes, args):
    n_in, n_out, n_scr = len(in_specs), len(out_specs), len(scratch_shapes)
    if comm is None:
        res = pl.pallas_call(body, name=name, grid=(nsteps,), in_specs=in_specs, out_specs=out_specs, out_shape=out_shape,
                             scratch_shapes=scratch_shapes, compiler_params=_cp(("arbitrary",)))(*args)
        return list(res), []
    c_in, c_out = len(comm["ins"]), len(comm["outs"])

    def wrapped(*refs):
        o0 = n_in + c_in
        s0 = o0 + n_out + c_out
        cparts = (refs[n_in:o0], refs[o0 + n_out:s0], refs[s0 + n_scr:])

        @pl.when(pl.program_id(0) == 0)
        def _():
            comm["start"](*cparts)
        body(*refs[:n_in], *refs[o0:o0 + n_out], *refs[s0:s0 + n_scr])

        @pl.when(pl.program_id(0) == nsteps - 1)
        def _():
            comm["finish"](*cparts)

    any_spec = pl.BlockSpec(memory_space=pl.ANY)
    res = pl.pallas_call(
        wrapped, name=name, grid=(nsteps,), in_specs=list(in_specs) + [any_spec] * c_in,
        out_specs=list(out_specs) + [any_spec] * c_out, out_shape=list(out_shape) + list(comm["outs"]),
        scratch_shapes=list(scratch_shapes) + list(comm["scratch"]),
        compiler_params=_cp(("arbitrary",)))(*args, *comm["ins"])
    return list(res[:n_out]), list(res[n_out:])


def _gather_piece(slab, r0, rows):
    half = rows // 2

    def copies(slab_ref, out_ref, send_sems, recv_sems):
        x, y, c = _place()
        cps = []
        for fx, fy in ((1, 0), (0, 1), (1, 1)):
            for fc in (0, 1):
                k = (2 * fx + fy - 1) * 2 + fc
                cps.append(pltpu.make_async_remote_copy(
                    src_ref=slab_ref.at[pl.ds(r0 + c * half, half), :], dst_ref=out_ref.at[2 * x + y, pl.ds(c * half, half), :],
                    send_sem=send_sems.at[k], recv_sem=recv_sems.at[k],
                    device_id=(_flip(x, fx), _flip(y, fy), _flip(c, fc)), device_id_type=MESH))
        return cps

    def start(ins, outs, scr):
        send_sems, recv_sems, stage, local_sems = scr
        x, y, _ = _place()
        load = pltpu.make_async_copy(ins[0].at[pl.ds(r0, rows), :], stage, local_sems.at[0])
        load.start()
        for cp in copies(ins[0], outs[0], send_sems, recv_sems):
            cp.start()
        load.wait()
        pltpu.make_async_copy(stage, outs[0].at[2 * x + y], local_sems.at[1]).start()

    def finish(ins, outs, scr):
        send_sems, recv_sems, stage, local_sems = scr
        x, y, _ = _place()
        for cp in copies(ins[0], outs[0], send_sems, recv_sems):
            cp.wait()
        pltpu.make_async_copy(stage, outs[0].at[2 * x + y], local_sems.at[1]).wait()

    return dict(ins=[slab], outs=[jax.ShapeDtypeStruct((4, rows, D), bf16)],
                scratch=[pltpu.SemaphoreType.DMA((6,)), pltpu.SemaphoreType.DMA((6,)), pltpu.VMEM((rows, D), bf16),
                         pltpu.SemaphoreType.DMA((2,))], start=start, finish=finish)


def _scatter_piece(gpiece):
    half = gpiece.shape[1] // 2

    def copies(g_ref, land_ref, send_sems, recv_sems):
        x, y, c = _place()
        cps = []
        for fx in (0, 1):
            for fy in (0, 1):
                for fc in (0, 1):
                    k = 4 * fx + 2 * fy + fc - 1
                    if k < 0:
                        continue
                    px, py, pc = _flip(x, fx), _flip(y, fy), _flip(c, fc)
                    cps.append(pltpu.make_async_remote_copy(
                        src_ref=g_ref.at[2 * px + py, pl.ds(pc * half, half), :], dst_ref=land_ref.at[k],
                        send_sem=send_sems.at[k], recv_sem=recv_sems.at[k], device_id=(px, py, pc), device_id_type=MESH))
        return cps

    def start(ins, outs, scr):
        for cp in copies(ins[0], outs[0], *scr):
            cp.start()

    def finish(ins, outs, scr):
        for cp in copies(ins[0], outs[0], *scr):
            cp.wait()

    return dict(ins=[gpiece], outs=[jax.ShapeDtypeStruct((7, half, D), gpiece.dtype)],
                scratch=[pltpu.SemaphoreType.DMA((7,)), pltpu.SemaphoreType.DMA((7,))], start=start, finish=finish)


def _gather_blocks(block):
    rows = block.shape[0]

    def mine(out_ref):
        x, y, c = _place()
        return out_ref.at[pl.ds((4 * x + 2 * y + c) * rows, rows), :]

    def copies(b_ref, out_ref, send_sems, recv_sems):
        x, y, c = _place()
        cps = []
        for fx in (0, 1):
            for fy in (0, 1):
                for fc in (0, 1):
                    k = 4 * fx + 2 * fy + fc - 1
                    if k < 0:
                        continue
                    cps.append(pltpu.make_async_remote_copy(
                        src_ref=b_ref, dst_ref=mine(out_ref), send_sem=send_sems.at[k], recv_sem=recv_sems.at[k],
                        device_id=(_flip(x, fx), _flip(y, fy), _flip(c, fc)), device_id_type=MESH))
        return cps

    def start(ins, outs, scr):
        send_sems, recv_sems, stage, local_sems = scr
        load = pltpu.make_async_copy(ins[0], stage, local_sems.at[0])
        load.start()
        for cp in copies(ins[0], outs[0], send_sems, recv_sems):
            cp.start()
        load.wait()
        pltpu.make_async_copy(stage, mine(outs[0]), local_sems.at[1]).start()

    def finish(ins, outs, scr):
        send_sems, recv_sems, stage, local_sems = scr
        for cp in copies(ins[0], outs[0], send_sems, recv_sems):
            cp.wait()
        pltpu.make_async_copy(stage, mine(outs[0]), local_sems.at[1]).wait()

    return dict(ins=[block], outs=[jax.ShapeDtypeStruct((8 * rows, D), block.dtype)],
                scratch=[pltpu.SemaphoreType.DMA((7,)), pltpu.SemaphoreType.DMA((7,)), pltpu.VMEM((rows, D), block.dtype),
                         pltpu.SemaphoreType.DMA((2,))], start=start, finish=finish)


def _both(c1, c2):
    n = (len(c1["ins"]), len(c1["outs"]), len(c1["scratch"]))

    def split(parts):
        return [p[:k] for p, k in zip(parts, n)], [p[k:] for p, k in zip(parts, n)]

    def start(*parts):
        a, b = split(parts)
        c1["start"](*a)
        c2["start"](*b)

    def finish(*parts):
        a, b = split(parts)
        c1["finish"](*a)
        c2["finish"](*b)

    return dict(ins=c1["ins"] + c2["ins"], outs=c1["outs"] + c2["outs"], scratch=c1["scratch"] + c2["scratch"],
                start=start, finish=finish)


def _rowwise(name, fn, nblk, rows=(), shifted=(), pars=(), refs=(), out_rows=(), out_accs=(), out_first=(), out_shifted=(), bm=CH,
             comm=None):
    n_r, n_s, n_p, n_w = len(rows), len(shifted), len(pars), len(refs)
    n_in = n_r + n_s + n_p + n_w
    n_o, n_a, n_f, n_so = len(out_rows), len(out_accs), len(out_first), len(out_shifted)

    def body(*all_refs):
        i = pl.program_id(0)
        ins = all_refs[:n_in]
        outs = all_refs[n_in:]
        rv = [r[...] for r in ins[:n_r]]
        sv = [r[...] for r in ins[n_r:n_r + n_s]]
        pv = [r[...] for r in ins[n_r + n_s:n_r + n_s + n_p]]
        ro, ao, fo, so = fn(i, rv, sv, pv, list(ins[n_r + n_s + n_p:]))
        for r, v in zip(outs[:n_o], ro):
            r[...] = v.astype(r.dtype)
        accs = outs[n_o:n_o + n_a]

        @pl.when(i == 0)
        def _():
            for r in accs:
                r[...] = jnp.zeros_like(r)
            for r, v in zip(outs[n_o + n_a:n_o + n_a + n_f], fo):
                r[...] = v.astype(r.dtype)
        for r, v in zip(accs, ao):
            r[...] += v
        for r, v in zip(outs[n_o + n_a + n_f:], so):
            r[...] = v.astype(r.dtype)

    prev = lambda i: (jnp.maximum(i - 1, 0), 0)
    in_specs = [pl.BlockSpec((bm, a.shape[1]), lambda i: (i, 0)) for a in rows]
    in_specs += [pl.BlockSpec((bm, a.shape[1]), prev) for a in shifted]
    in_specs += [pl.BlockSpec(a.shape, lambda i, nd=a.ndim: (0,) * nd) for a in pars]
    in_specs += [spec for _, spec in refs]
    out_specs = [pl.BlockSpec((bm, c), lambda i: (i, 0)) for c, _ in out_rows]
    out_specs += [pl.BlockSpec((8, c), lambda i: (0, 0)) for c in out_accs]
    out_specs += [pl.BlockSpec((bm, c), lambda i: (0, 0)) for c, _ in out_first]
    out_specs += [pl.BlockSpec((bm, c), prev) for c, _ in out_shifted]
    out_shape = [jax.ShapeDtypeStruct((nblk * bm, c), dt) for c, dt in out_rows]
    out_shape += [jax.ShapeDtypeStruct((8, c), f32) for c in out_accs]
    out_shape += [jax.ShapeDtypeStruct((bm, c), dt) for c, dt in out_first]
    out_shape += [jax.ShapeDtypeStruct(((nblk - 1) * bm, c), dt) for c, dt in out_shifted]
    res, cres = _call(body, comm, name=name, nsteps=nblk, in_specs=in_specs, out_specs=out_specs, out_shape=out_shape,
                      scratch_shapes=[], args=[*rows, *shifted, *pars, *[a for a, _ in refs]])
    parts = (res[:n_o], res[n_o:n_o + n_a], res[n_o + n_a:n_o + n_a + n_f], res[n_o + n_a + n_f:])
    return parts if comm is None else parts + (cres,)


PACK_ROWS = 4096
HALF_ROWS = PACK_ROWS // 2
R_UP, R_DOWN, R_GLU, R_OUT, R_IN, R_SPARE = 0, 1024, 2048, 2560, 3072, 3972


BIG_ROWS = 768


def _contract_rows(lp):
    return BIG_ROWS if lp % BIG_ROWS == 0 else CH


def _dw_into(name, a, b, slab, ka, a_sharded, row_blk, n_s, s0, piece_rows=2048):
    lp = a.shape[0]
    bm = _contract_rows(lp)
    steps = lp // bm

    def body(a_ref, b_ref, *rest):
        o_ref, acc = rest[-2], rest[-1]
        k = pl.program_id(1)

        @pl.when(k == 0)
        def _():
            acc[...] = jnp.zeros_like(acc)
        acc[...] += _dot(a_ref[...], b_ref[...], TN)

        @pl.when(k == steps - 1)
        def _():
            o_ref[0] = acc[...].astype(bf16)

    in_specs = [pl.BlockSpec((bm, ka), (lambda s, k: (k, s)) if a_sharded else (lambda s, k: (k, 0))),
                pl.BlockSpec((bm, D), (lambda s, k: (k, 0)) if a_sharded else (lambda s, k: (k, s)))]
    args = [a, b]
    aliases = {}
    if slab is not None:
        in_specs.append(_ANY)
        args.append(slab)
        aliases = {2: 0}
    return pl.pallas_call(
        body, name=name, grid=(n_s, steps), in_specs=in_specs,
        out_specs=pl.BlockSpec((1, ka, D), lambda s, k: (s0 + s, row_blk, 0)),
        out_shape=jax.ShapeDtypeStruct((4, piece_rows, D), bf16),
        scratch_shapes=[pltpu.VMEM((ka, D), f32)], input_output_aliases=aliases,
        compiler_params=_cp(("arbitrary", "arbitrary")))(*args)


def _dw_in_t(dproj, n0):
    lp = n0.shape[0]
    bm = _contract_rows(lp)
    steps = lp // bm
    bn = 512

    def body(a_ref, b_ref, o_ref, acc):
        k = pl.program_id(1)

        @pl.when(k == 0)
        def _():
            acc[...] = jnp.zeros_like(acc)
        acc[...] += _dot(a_ref[...], b_ref[...], TN)

        @pl.when(k == steps - 1)
        def _():
            o_ref[...] = acc[...].astype(bf16)

    return pl.pallas_call(
        body, name="dw_in", grid=(D // bn, steps),
        in_specs=[pl.BlockSpec((bm, W_PROJ), lambda j, k: (k, 0)), pl.BlockSpec((bm, bn), lambda j, k: (k, j))],
        out_specs=pl.BlockSpec((W_PROJ, bn), lambda j, k: (0, j)),
        out_shape=jax.ShapeDtypeStruct((W_PROJ, D), bf16),
        scratch_shapes=[pltpu.VMEM((W_PROJ, bn), f32)],
        compiler_params=_cp(("arbitrary", "arbitrary")))(dproj, n0)


def _head_expand():
    h = lax.broadcasted_iota(jnp.int32, (HEADS, D), 0)
    c = lax.broadcasted_iota(jnp.int32, (HEADS, D), 1)
    return jnp.where((c >> 6) == h, 1.0, 0.0).astype(bf16)


def _ssd_common(i, P, prev8, cw, cb, dtb, alog, xc=None):
    z = P[:, O_Z:O_Z + D]
    xp = P[:, O_XBC:O_XBC + D_XBC]
    dt_raw = P[:, O_DT:O_DT + HEADS]
    row = lax.broadcasted_iota(jnp.int32, (CH, 1), 0)
    if xc is None:
        row8 = lax.broadcasted_iota(jnp.int32, (8, 1), 0)
        xc = cb + cw[3:4] * xp
        for k in (1, 2, 3):
            rolled = pltpu.roll(xp, k, 0)
            fix = pltpu.roll(prev8, k, 0)
            top = jnp.where(row8 < k, fix, rolled[0:8])
            xc = xc + cw[3 - k:4 - k] * jnp.concatenate([top, rolled[8:]], axis=0)
    sg = _sigmoid(xc)
    xbc = xc * sg
    live = jnp.where(jnp.logical_or(i > 0, row >= CH - N_META), 1.0, 0.0)
    pre = dt_raw + dtb
    dt = jnp.where(pre > 20.0, pre, jnp.log(1.0 + jnp.exp(jnp.minimum(pre, 20.0)))) * live
    a = -jnp.exp(alog)
    dta = dt * a
    r_i = lax.broadcasted_iota(jnp.int32, (CH, CH), 0)
    c_i = lax.broadcasted_iota(jnp.int32, (CH, CH), 1)
    tril = r_i >= c_i
    acs = _split_dot(dta, jnp.where(tril, 1.0, 0.0).astype(bf16), None, 3, v_is_lhs=False)
    acs_t = _split_dot(dta, jnp.where(r_i <= c_i, 1.0, 0.0).astype(bf16), TN, 3)
    e = _head_expand()
    acs_e = _split_dot(acs, e, None, 3)
    dt_e = _split_dot(dt, e, None, 3)
    return dict(z=z, xp=xp, xc=xc, sg=sg, xbc=xbc, live=live, pre=pre, dt=dt, a=a, tril=tril,
                acs=acs, acs_t=acs_t, e=e, acs_e=acs_e, dt_e=dt_e)


def _lmat(c, h):
    seg = c["acs"][:, h:h + 1] - c["acs_t"][h:h + 1, :]
    return jnp.where(c["tril"], jnp.exp(jnp.minimum(seg, 0.0)), 0.0)


def _pair_masks():
    lane = lax.broadcasted_iota(jnp.int32, (1, 128), 1)
    return jnp.where(lane < HEAD_DIM, 1.0, 0.0), jnp.where(lane >= HEAD_DIM, 1.0, 0.0)


def _ssd_fwd(proj, conv_w, conv_b, dt_bias, a_log, d_ssd, g_ssd, nch, comm=None):
    def body(p_ref, cw_ref, cb_ref, dtb_ref, al_ref, d_ref, g_ref, y_ref, ys_ref, st_ref, xc_ref, prev8_ref, state_ref):
        i = pl.program_id(0)

        @pl.when(i == 0)
        def _():
            prev8_ref[...] = jnp.zeros_like(prev8_ref)
            state_ref[...] = jnp.zeros_like(state_ref)

        P = p_ref[...]
        c = _ssd_common(i, P, prev8_ref[...], cw_ref[...], cb_ref[...], dtb_ref[...], al_ref[...])
        prev8_ref[...] = c["xp"][CH - 8:CH]
        xc_ref[...] = c["xc"]
        xbc = c["xbc"]
        x = xbc[:, 0:D]
        xdt = x * c["dt_e"]
        a_last_e = c["acs_e"][CH - 1:CH, :]
        w_end = (xdt * jnp.exp(a_last_e - c["acs_e"])).astype(bf16)
        m0, m1 = _pair_masks()
        ys = []
        for g in range(2):
            bg = xbc[:, D + NSTATE * g:D + NSTATE * (g + 1)].astype(bf16)
            cg = xbc[:, D + 2 * NSTATE + NSTATE * g:D + 2 * NSTATE + NSTATE * (g + 1)].astype(bf16)
            gmat = _dot(cg, bg, NT)
            st = state_ref[g]
            st_ref[0, g] = st
            sl = slice(512 * g, 512 * (g + 1))
            y_off = _dot(cg, st.astype(bf16)) * jnp.exp(c["acs_e"][:, sl])
            contrib = _dot(bg, w_end[:, sl], TN)
            state_ref[g] = st * jnp.exp(a_last_e[:, sl]) + contrib
            yd = []
            for pr in range(4):
                h0 = 8 * g + 2 * pr
                xp2 = xdt[:, 128 * (4 * g + pr):128 * (4 * g + pr + 1)]
                ma = (gmat * _lmat(c, h0)).astype(bf16)
                mb = (gmat * _lmat(c, h0 + 1)).astype(bf16)
                yd.append(_dot(ma, (xp2 * m0).astype(bf16)) + _dot(mb, (xp2 * m1).astype(bf16)))
            ys.append(jnp.concatenate(yd, axis=1) + y_off)
        d_e = _split_dot(d_ref[...], c["e"], None, 3)
        y = jnp.concatenate(ys, axis=1) + x * d_e
        y_ref[...] = y
        yg = y * (c["z"] * _sigmoid(c["z"]))
        ys_ref[...] = _rms(yg, g_ref[...]).astype(bf16)

    full = lambda a: pl.BlockSpec(a.shape, lambda i, nd=a.ndim: (0,) * nd)
    return _call(
        body, comm, name="ssd_fwd", nsteps=nch,
        in_specs=[pl.BlockSpec((CH, W_PROJ), lambda i: (i, 0))] + [full(a) for a in (conv_w, conv_b, dt_bias, a_log, d_ssd, g_ssd)],
        out_specs=[pl.BlockSpec((CH, D), lambda i: (i, 0)), pl.BlockSpec((CH, D), lambda i: (i, 0)),
                   pl.BlockSpec((1, 2, NSTATE, 512), lambda i: (i, 0, 0, 0)), pl.BlockSpec((CH, D_XBC), lambda i: (i, 0))],
        out_shape=[jax.ShapeDtypeStruct((nch * CH, D), f32), jax.ShapeDtypeStruct((nch * CH, D), bf16),
                   jax.ShapeDtypeStruct((nch, 2, NSTATE, 512), f32), jax.ShapeDtypeStruct((nch * CH, D_XBC), f32)],
        scratch_shapes=[pltpu.VMEM((8, D_XBC), f32), pltpu.VMEM((2, NSTATE, 512), f32)],
        args=[proj, conv_w, conv_b, dt_bias, a_log, d_ssd, g_ssd])


def _ssd_bwd(proj, xc_all, y, dys, du, states, conv_w, conv_b, dt_bias, a_log, d_ssd, g_ssd, nch, comm=None):
    def body(p_ref, xc_ref, y_ref, dys_ref, du_ref, st_ref, cw_ref, cb_ref, dtb_ref, al_ref, d_ref, g_ref,
             dp_ref, dcw_ref, dcb_ref, ddtb_ref, dal_ref, dd_ref, dg_ref, nxt8_ref, dst_ref):
        step = pl.program_id(0)
        i = nch - 1 - step

        @pl.when(step == 0)
        def _():
            nxt8_ref[...] = jnp.zeros_like(nxt8_ref)
            dst_ref[...] = jnp.zeros_like(dst_ref)
            for r in (dcw_ref, dcb_ref, ddtb_ref, dal_ref, dd_ref, dg_ref):
                r[...] = jnp.zeros_like(r)

        P = p_ref[...]
        c = _ssd_common(i, P, None, cw_ref[...], cb_ref[...], dtb_ref[...], al_ref[...], xc=xc_ref[...])
        xbc, z, e = c["xbc"], c["z"], c["e"]
        x = xbc[:, 0:D]
        yv = y_ref[...]
        sz = _sigmoid(z)
        silu_z = z * sz
        dyg, dg8 = _rms_bwd(dys_ref[...], yv * silu_z, g_ref[...])
        dg_ref[...] += dg8
        dy = dyg * silu_z
        dz = dyg * yv * (sz * (1.0 + z * (1.0 - sz)))
        d_e = _split_dot(d_ref[...], e, None, 3)
        dd_ref[...] += _rsum8(dy * x)
        xdt = x * c["dt_e"]
        a_last_e = c["acs_e"][CH - 1:CH, :]
        e_end = jnp.exp(a_last_e - c["acs_e"])
        w_end = xdt * e_end
        e_acs = jnp.exp(c["acs_e"])
        dy_dec = dy * e_acs
        m0, m1 = _pair_masks()
        lane16 = lax.broadcasted_iota(jnp.int32, (1, HEADS), 1)
        row16 = lax.broadcasted_iota(jnp.int32, (HEADS, 1), 0)
        dacs = jnp.zeros((CH, HEADS), f32)
        dacs_t = jnp.zeros((HEADS, CH), f32)
        dxdt_parts, dbs, dcs, zparts, yoff_parts, dlast_parts = [], [], [], [], [], []
        for g in range(2):
            sl = slice(512 * g, 512 * (g + 1))
            bg = xbc[:, D + NSTATE * g:D + NSTATE * (g + 1)].astype(bf16)
            cg = xbc[:, D + 2 * NSTATE + NSTATE * g:D + 2 * NSTATE + NSTATE * (g + 1)].astype(bf16)
            gmat = _dot(cg, bg, NT)
            st = st_ref[0, g]
            dstn = dst_ref[g]
            dstn_b = dstn.astype(bf16)
            y_off = _dot(cg, st.astype(bf16)) * e_acs[:, sl]
            yoff_parts.append(y_off)
            bds = _dot(bg, dstn_b)
            zparts.append(w_end[:, sl] * bds)
            dlast_parts.append(jnp.sum(dstn * st, axis=0, keepdims=True) * jnp.exp(a_last_e[:, sl]))
            dg_acc = jnp.zeros((CH, CH), f32)
            dxd = []
            for pr in range(4):
                lo = 128 * (4 * g + pr)
                xp2 = xdt[:, lo:lo + 128].astype(bf16)
                dy2 = dy[:, lo:lo + 128]
                outp = jnp.zeros((CH, 128), f32)
                for hh, msk in ((0, m0), (1, m1)):
                    h = 8 * g + 2 * pr + hh
                    lm = _lmat(c, h)
                    dyh = (dy2 * msk).astype(bf16)
                    mh = (gmat * lm).astype(bf16)
                    outp = outp + _dot(mh, dyh, TN)
                    dml = _dot(dyh, xp2, NT) * lm
                    dg_acc = dg_acc + dml
                    q = dml * gmat
                    dacs = dacs + jnp.where(lane16 == h, jnp.sum(q, axis=1, keepdims=True), 0.0)
                    dacs_t = dacs_t + jnp.where(row16 == h, jnp.sum(q, axis=0, keepdims=True), 0.0)
                dxd.append(outp)
            dxdt_parts.append(jnp.concatenate(dxd, axis=1) + e_end[:, sl] * bds)
            dgb = dg_acc.astype(bf16)
            dcs.append(_dot(dgb, bg) + _dot(dy_dec[:, sl].astype(bf16), st.astype(bf16), NT))
            dbs.append(_dot(dgb, cg, TN) + _dot(w_end[:, sl].astype(bf16), dstn_b, NT))
            dst_ref[g] = dstn * jnp.exp(a_last_e[:, sl]) + _dot(cg, dy_dec[:, sl].astype(bf16), TN)
        dxdt = jnp.concatenate(dxdt_parts, axis=1)
        zfull = jnp.concatenate(zparts, axis=1)
        y_off_full = jnp.concatenate(yoff_parts, axis=1)
        dlast = jnp.concatenate(dlast_parts, axis=1)
        red = lambda v: _split_dot(v, e, NT, 2)
        eye16 = jnp.where(lax.broadcasted_iota(jnp.int32, (HEADS, HEADS), 0) == lax.broadcasted_iota(jnp.int32, (HEADS, HEADS), 1),
                          1.0, 0.0).astype(bf16)
        dacs = dacs - _split_dot(dacs_t, eye16, TN, 3)
        zred = red(zfull)
        dacs = dacs + red(dy * y_off_full) - zred
        last_term = jnp.sum(zred, axis=0, keepdims=True) + red(dlast)
        rowc = lax.broadcasted_iota(jnp.int32, (CH, 1), 0)
        dacs = dacs + jnp.where(rowc == CH - 1, last_term, 0.0)
        r_i = lax.broadcasted_iota(jnp.int32, (CH, CH), 0)
        c_i = lax.broadcasted_iota(jnp.int32, (CH, CH), 1)
        ddta = _split_dot(dacs, jnp.where(c_i >= r_i, 1.0, 0.0).astype(bf16), None, 3, v_is_lhs=False)
        ddt = ddta * c["a"] + red(dxdt * x)
        dal_ref[...] += _rsum8(ddta * c["dt"] * c["a"])
        ddt_raw = ddt * _sigmoid(c["pre"]) * c["live"]
        ddtb_ref[...] += _rsum8(ddt_raw)
        dx = dy * d_e + dxdt * c["dt_e"]
        dxbc = jnp.concatenate([dx, dbs[0], dbs[1], dcs[0], dcs[1]], axis=1)
        sg = c["sg"]
        dxc = dxbc * (sg * (1.0 + c["xc"] * (1.0 - sg)))
        dcb_ref[...] += _rsum8(dxc)
        xp = c["xp"]
        row8 = lax.broadcasted_iota(jnp.int32, (8, 1), 0)
        cw = cw_ref[...]
        dxp = cw[3:4] * dxc
        dcw = jnp.where(row8 == 3, jnp.sum(dxc * xp, axis=0, keepdims=True), 0.0)
        nxt8 = nxt8_ref[...]
        for j in (1, 2, 3):
            rolled = pltpu.roll(dxc, CH - j, 0)
            fix = pltpu.roll(nxt8, 8 - j, 0)
            bot = jnp.where(row8 >= 8 - j, fix, rolled[CH - 8:CH])
            later = jnp.concatenate([rolled[:CH - 8], bot], axis=0)
            dxp = dxp + cw[3 - j:4 - j] * later
            dcw = dcw + jnp.where(row8 == 3 - j, jnp.sum(later * xp, axis=0, keepdims=True), 0.0)
        dcw_ref[...] += dcw
        nxt8_ref[...] = dxc[0:8]
        dp_ref[:, O_Z:O_Z + D] = dz.astype(bf16)
        dp_ref[:, O_U:O_U + D] = du_ref[...].astype(bf16)
        dp_ref[:, O_XBC:O_XBC + D_XBC] = dxp.astype(bf16)
        dp_ref[:, O_DT:W_PROJ] = jnp.zeros((CH, W_PROJ - O_DT), bf16)
        dp_ref[:, O_DT:O_DT + HEADS] = ddt_raw.astype(bf16)

    full = lambda a: pl.BlockSpec(a.shape, lambda s, nd=a.ndim: (0,) * nd)
    rev = lambda s: (nch - 1 - s, 0)
    acc = lambda cdim: pl.BlockSpec((8, cdim), lambda s: (0, 0))
    return _call(
        body, comm, name="ssd_bwd", nsteps=nch,
        in_specs=[pl.BlockSpec((CH, W_PROJ), rev), pl.BlockSpec((CH, D_XBC), rev),
                  pl.BlockSpec((CH, D), rev), pl.BlockSpec((CH, D), rev), pl.BlockSpec((CH, D), rev),
                  pl.BlockSpec((1, 2, NSTATE, 512), lambda s: (nch - 1 - s, 0, 0, 0))]
        + [full(a) for a in (conv_w, conv_b, dt_bias, a_log, d_ssd, g_ssd)],
        out_specs=[pl.BlockSpec((CH, W_PROJ), rev), acc(D_XBC), acc(D_XBC), acc(HEADS), acc(HEADS), acc(D), acc(D)],
        out_shape=[jax.ShapeDtypeStruct((nch * CH, W_PROJ), bf16)]
        + [jax.ShapeDtypeStruct((8, cdim), f32) for cdim in (D_XBC, D_XBC, HEADS, HEADS, D, D)],
        scratch_shapes=[pltpu.VMEM((8, D_XBC), f32), pltpu.VMEM((2, NSTATE, 512), f32)],
        args=[proj, xc_all, y, dys, du, states, conv_w, conv_b, dt_bias, a_log, d_ssd, g_ssd])


SCAN_UNROLL = 8


def _to_slabs(slab_ref, q, mat):
    for ls in range(8):
        slab_ref[ls, pl.ds(PITCH * q, CH), :] = mat[:, 128 * ls:128 * (ls + 1)]


def _from_slabs(slab, q):
    return jnp.concatenate([slab(ls, PITCH * q) for ls in range(8)], axis=1)


def _tile(slab_ref, ls, t, lead=None):
    idx = (ls, pl.ds(t, 8, stride=PITCH), slice(None))
    return slab_ref[idx] if lead is None else slab_ref[(lead,) + idx]


def _s5_fwd(proj, bbq, ccq_t, ar, ai, d_skip, nch, comm=None):
    def body(u_ref, bb_ref, cc_ref, ar_ref, ai_ref, d_ref, s_ref, yl_ref, y5_ref, bu_ref, st_ref):
        @pl.when(pl.program_id(0) == 0)
        def _():
            st_ref[...] = jnp.zeros_like(st_ref)
        u = u_ref[...]
        ub = u.astype(bf16)
        for q in range(NQ):
            _to_slabs(bu_ref, q, _dot(ub[:, 128 * q:128 * (q + 1)], bb_ref[q]))
        ar_t = [ar_ref[:, 128 * l:128 * (l + 1)] for l in range(4)]
        ai_t = [ai_ref[:, 128 * l:128 * (l + 1)] for l in range(4)]

        def one(t, carry):
            re, im = carry
            nre, nim = [], []
            for l in range(4):
                a = ar_t[l] * re[l] - ai_t[l] * im[l] + _tile(bu_ref, l, t)
                b = ar_t[l] * im[l] + ai_t[l] * re[l] + _tile(bu_ref, l + 4, t)
                s_ref[0, l, pl.ds(t, 8, stride=PITCH), :] = a
                s_ref[0, l + 4, pl.ds(t, 8, stride=PITCH), :] = b
                nre.append(a)
                nim.append(b)
            return tuple(nre), tuple(nim)

        def step(tt, carry):
            for k in range(SCAN_UNROLL):
                carry = one(tt * SCAN_UNROLL + k, carry)
            return carry
        init = (tuple(st_ref[l] for l in range(4)), tuple(st_ref[l + 4] for l in range(4)))
        re, im = lax.fori_loop(0, CH // SCAN_UNROLL, step, init)
        for l in range(4):
            st_ref[l] = re[l]
            st_ref[l + 4] = im[l]
        ys = []
        for q in range(NQ):
            sq = _from_slabs(lambda ls, r0: s_ref[0, ls, pl.ds(r0, CH), :], q).astype(bf16)
            ys.append(_dot(sq, cc_ref[q]))
        yl = jnp.concatenate(ys, axis=1) + u * d_ref[...]
        yl_ref[...] = yl
        y5_ref[...] = (0.5 * yl * (1.0 + lax.erf(yl * (1.0 / math.sqrt(2.0))))).astype(bf16)

    const = lambda a: pl.BlockSpec(a.shape, lambda i, nd=a.ndim: (0,) * nd)
    return _call(
        body, comm, name="s5_fwd", nsteps=nch,
        in_specs=[pl.BlockSpec((CH, D), lambda i: (i, O_U // D)), const(bbq), const(ccq_t), const(ar), const(ai), const(d_skip)],
        out_specs=[pl.BlockSpec((1, 8, 8 * PITCH, 128), lambda i: (i, 0, 0, 0)),
                   pl.BlockSpec((CH, D), lambda i: (i, 0)), pl.BlockSpec((CH, D), lambda i: (i, 0))],
        out_shape=[jax.ShapeDtypeStruct((nch, 8, 8 * PITCH, 128), f32), jax.ShapeDtypeStruct((nch * CH, D), f32),
                   jax.ShapeDtypeStruct((nch * CH, D), bf16)],
        scratch_shapes=[pltpu.VMEM((8, 8 * PITCH, 128), f32), pltpu.VMEM((8, 8, 128), f32)],
        args=[proj, bbq, ccq_t, ar, ai, d_skip])


def _s5_bwd(proj, dyl, s_all, bbtq, cctq, ar, ai, d_skip, nch, comm=None):
    def body(u_ref, dy_ref, s_ref, bbt_ref, cct_ref, ar_ref, ai_ref, d_ref,
             du_ref, dcc_ref, dbb_ref, dab_ref, dd_ref, ga_ref, st_ref):
        @pl.when(pl.program_id(0) == 0)
        def _():
            st_ref[...] = jnp.zeros_like(st_ref)
            for r in (dcc_ref, dbb_ref, dab_ref, dd_ref):
                r[...] = jnp.zeros_like(r)
        u = u_ref[...]
        dyl_v = dy_ref[...]
        dd_ref[...] += _rsum8(dyl_v * u)
        ub = u.astype(bf16)
        dyb = dyl_v.astype(bf16)
        for q in range(NQ):
            _to_slabs(ga_ref, q, _dot(dyb[:, 128 * q:128 * (q + 1)], cct_ref[q]))
        ar_t = [ar_ref[:, 128 * l:128 * (l + 1)] for l in range(4)]
        ai_t = [ai_ref[:, 128 * l:128 * (l + 1)] for l in range(4)]

        for ls in range(8):
            ga_ref[ls, pl.ds(CH, 8, stride=PITCH), :] = st_ref[ls]

        def one(t, carry):
            re, im = carry
            nre, nim = [], []
            for l in range(4):
                a = _tile(ga_ref, l, t) + ar_t[l] * re[l] + ai_t[l] * im[l]
                b = _tile(ga_ref, l + 4, t) - ai_t[l] * re[l] + ar_t[l] * im[l]
                ga_ref[l, pl.ds(t, 8, stride=PITCH), :] = a
                ga_ref[l + 4, pl.ds(t, 8, stride=PITCH), :] = b
                nre.append(a)
                nim.append(b)
            return tuple(nre), tuple(nim)

        def step(tt, carry):
            for k in range(SCAN_UNROLL):
                carry = one(CH - 1 - (tt * SCAN_UNROLL + k), carry)
            return carry
        four = lambda ref, o: tuple(ref[l + o] for l in range(4))
        re, im = lax.fori_loop(0, CH // SCAN_UNROLL, step, (four(st_ref, 0), four(st_ref, 4)))
        for l in range(4):
            st_ref[l], st_ref[l + 4] = re[l], im[l]
        row8 = lax.broadcasted_iota(jnp.int32, (8, 1), 0)
        for l in range(4):
            dar = jnp.zeros((8, 128), f32)
            dai = jnp.zeros((8, 128), f32)
            for q in range(NQ):
                r0 = PITCH * q
                are, aim = ga_ref[l, pl.ds(r0 + 1, CH), :], ga_ref[l + 4, pl.ds(r0 + 1, CH), :]
                sre, sim = s_ref[0, l, pl.ds(r0, CH), :], s_ref[0, l + 4, pl.ds(r0, CH), :]
                dar = dar + jnp.where(row8 == q, jnp.sum(are * sre + aim * sim, axis=0, keepdims=True), 0.0)
                dai = dai + jnp.where(row8 == q, jnp.sum(aim * sre - are * sim, axis=0, keepdims=True), 0.0)
            dab_ref[l] += dar
            dab_ref[l + 4] += dai
        dus = []
        for q in range(NQ):
            aq = _from_slabs(lambda ls, r0: ga_ref[ls, pl.ds(r0, CH), :], q).astype(bf16)
            sq = _from_slabs(lambda ls, r0: s_ref[0, ls, pl.ds(r0, CH), :], q).astype(bf16)
            dcc_ref[q] += _dot(dyb[:, 128 * q:128 * (q + 1)], sq, TN)
            dbb_ref[q] += _dot(ub[:, 128 * q:128 * (q + 1)], aq, TN)
            dus.append(_dot(aq, bbt_ref[q]))
        du_ref[...] = jnp.concatenate(dus, axis=1) + dyl_v * d_ref[...]

    const = lambda a: pl.BlockSpec(a.shape, lambda s, nd=a.ndim: (0,) * nd)
    rev = lambda s: (nch - 1 - s, 0)
    return _call(
        body, comm, name="s5_bwd", nsteps=nch,
        in_specs=[pl.BlockSpec((CH, D), lambda s: (nch - 1 - s, O_U // D)), pl.BlockSpec((CH, D), rev),
                  pl.BlockSpec((1, 8, 8 * PITCH, 128), lambda s: (nch - 1 - s, 0, 0, 0)),
                  const(bbtq), const(cctq), const(ar), const(ai), const(d_skip)],
        out_specs=[pl.BlockSpec((CH, D), rev), pl.BlockSpec((NQ, 128, D), lambda s: (0, 0, 0)),
                   pl.BlockSpec((NQ, 128, D), lambda s: (0, 0, 0)), pl.BlockSpec((8, 8, 128), lambda s: (0, 0, 0)),
                   pl.BlockSpec((8, D), lambda s: (0, 0))],
        out_shape=[jax.ShapeDtypeStruct((nch * CH, D), f32), jax.ShapeDtypeStruct((NQ, 128, D), f32),
                   jax.ShapeDtypeStruct((NQ, 128, D), f32), jax.ShapeDtypeStruct((8, 8, 128), f32),
                   jax.ShapeDtypeStruct((8, D), f32)],
        scratch_shapes=[pltpu.VMEM((8, 8 * PITCH, 128), f32), pltpu.VMEM((8, 8, 128), f32)],
        args=[proj, dyl, s_all, bbtq, cctq, ar, ai, d_skip])


def _s5_tables(lam_re, lam_im, log_step, b_re, b_im):
    step = jnp.exp(log_step)[:, None]
    mag = jnp.exp(lam_re * step)
    ab_re = mag * jnp.cos(lam_im * step)
    ab_im = mag * jnp.sin(lam_im * step)
    den = lam_re * lam_re + lam_im * lam_im
    coef_re = ((ab_re - 1.0) * lam_re + ab_im * lam_im) / den
    coef_im = (ab_im * lam_re - (ab_re - 1.0) * lam_im) / den
    bb_re = coef_re[..., None] * b_re - coef_im[..., None] * b_im
    bb_im = coef_re[..., None] * b_im + coef_im[..., None] * b_re
    return ab_re, ab_im, bb_re, bb_im


def _blockdiag_in(m_re, m_im):
    eye = jnp.eye(8, dtype=f32)

    def one(m):
        m = m.reshape(NQ, 8, S5_P, 16)
        return jnp.einsum("qgph,gk->qghkp", m, eye).reshape(NQ, 128, 512)
    return jnp.concatenate([one(m_re), one(m_im)], axis=2)


def _blockdiag_in_grad(dm):
    def one(x):
        x = x.reshape(NQ, 8, 16, 8, S5_P)
        return jnp.einsum("qghgp->qgph", x).reshape(NQ * 8, S5_P, 16)
    return one(dm[:, :, :512]), one(dm[:, :, 512:])


def _local_step(x2, tgt2, meta, p, w_in_t, slab):
    seq = x2.shape[0]
    nch = 1 + seq // CH
    bmb = BIG_ROWS if (nch * CH) % BIG_ROWS == 0 else CH
    nbig = nch * CH // bmb
    metablk = jnp.concatenate([jnp.zeros((CH - N_META, D), f32), meta], axis=0)
    w_full = (w_in_t, pl.BlockSpec(w_in_t.shape, lambda i: (0, 0), pipeline_mode=pl.Buffered(1)))
    h0_of = lambda i, s, q: jnp.where(i == 0, q[0], s[0])

    def in_fn(i, r, s, q, w):
        nb = _rms(h0_of(i, s, q), q[1]).astype(bf16)
        return [_dot(nb, w[0][...], NT), nb], [], [], []
    (proj, n0), _, _, _, (g_up,) = _rowwise("in_proj", in_fn, nch, shifted=[x2], pars=[metablk, p["g_mix"]], refs=[w_full],
                                            out_rows=[(W_PROJ, f32), (D, bf16)], comm=_gather_piece(slab, R_UP, 1024))
    (y, y_ssd, states, xc_all), (g_down,) = _ssd_fwd(proj, p["conv_w"], p["conv_b"], p["dt_bias"], p["a_log"], p["d_ssd"],
                                                     p["g_ssd"], nch, comm=_gather_piece(slab, R_DOWN, 1024))

    ab_re, ab_im, bb_re, bb_im = _s5_tables(p["lam_re"], p["lam_im"], p["log_step"], p["b_re"], p["b_im"])
    ar, ai = ab_re.reshape(NQ, 512), ab_im.reshape(NQ, 512)
    bbq = _blockdiag_in(bb_re, bb_im)
    ccq = _blockdiag_in(jnp.swapaxes(p["c_re"], 1, 2), -jnp.swapaxes(p["c_im"], 1, 2))
    d_skip = p["d_s5"].reshape(1, D)
    (s_all, ylin, y5), (g_go,) = _s5_fwd(proj, bbq.astype(bf16), jnp.swapaxes(ccq, 1, 2).astype(bf16), ar, ai, d_skip, nch,
                                         comm=_gather_piece(slab, R_GLU, 1024))
    whole = lambda a: (a, pl.BlockSpec(a.shape, lambda i: (0, 0, 0), pipeline_mode=pl.Buffered(1)))
    w_up, w_down = whole(g_up), whole(g_down)
    w_glu_t = (g_go, pl.BlockSpec((4, 512, D), lambda i: (0, 0, 0), pipeline_mode=pl.Buffered(1)))
    w_out = (g_go, pl.BlockSpec((4, 512, D), lambda i: (0, 1, 0), pipeline_mode=pl.Buffered(1)))

    def glu_fn(i, r, s, q, w):
        v = jnp.concatenate([_dot(r[0], w[0][k], NT) for k in range(4)], axis=1) + q[0]
        return [v, _rms(v[:, :D] * _sigmoid(v[:, D:]), q[1])], [], [], []
    (v, y_s5), _, _, _ = _rowwise("glu", glu_fn, nbig, rows=[y5], pars=[p["b_glu"], p["g_s5"]], refs=[w_glu_t],
                                  out_rows=[(2 * D, f32), (D, bf16)], bm=bmb)

    def out_fn(i, r, s, q, w):
        acc = (_dot(r[0][:, :512], w[0][0]) + _dot(r[0][:, 512:], w[0][1]) + _dot(r[1][:, :512], w[0][2])
               + _dot(r[1][:, 512:], w[0][3]))
        return [h0_of(i, s, q) + acc], [], [], []
    (h1,), _, _, _ = _rowwise("out_proj", out_fn, nch, rows=[y_ssd, y_s5], shifted=[x2], pars=[metablk], refs=[w_out],
                              out_rows=[(D, f32)])

    def up_fn(i, r, s, q, w):
        nb = _rms(r[0], q[0]).astype(bf16)
        act = [jnp.square(jnp.maximum(_dot(nb, w[0][k]), 0.0)).astype(bf16) for k in range(4)]
        return [jnp.concatenate(act, axis=1), nb], [], [], []
    (act, n1), _, _, _ = _rowwise("up_proj", up_fn, nbig, rows=[h1], pars=[p["g_mlp"]], refs=[w_up],
                                  out_rows=[(4 * D, bf16), (D, bf16)], bm=bmb)

    def down_fn(i, r, s, q, w):
        acc = _dot(r[0][:, :D], w[0][0])
        for k in range(1, 4):
            acc = acc + _dot(r[0][:, D * k:D * (k + 1)], w[0][k])
        return [r[1] + acc], [], [], []
    (h2,), _, _, _ = _rowwise("down_proj", down_fn, nbig, rows=[act, h1], refs=[w_down], out_rows=[(D, f32)], bm=bmb)

    def final_fn(i, r, s, q, w):
        live = jnp.where(i > 0, 1.0, 0.0)
        err = (_rms(r[0], q[0]) - s[0]) * live
        dh, dg8 = _rms_bwd(err * (1.0 / D), r[0], q[0])
        return [dh, dh], [_rsum8(err * err), dg8], [], []
    (dh2, dh2_b), (loss8, dgf8), _, _ = _rowwise("final", final_fn, nch, rows=[h2], shifted=[tgt2], pars=[p["g_final"]],
                                                  out_rows=[(D, f32), (D, bf16)], out_accs=[D, D])
    loss = 0.5 / D * jnp.sum(loss8)

    def down_bwd_fn(i, r, s, q, w):
        dm_ = [_dot(r[0], w[0][k], NT) * 2.0 * jnp.sqrt(r[1][:, D * k:D * (k + 1)].astype(f32)) for k in range(4)]
        return [jnp.concatenate(dm_, axis=1)], [], [], []
    (dm,), _, _, _ = _rowwise("down_bwd", down_bwd_fn, nbig, rows=[dh2_b, act], refs=[w_down], out_rows=[(4 * D, bf16)], bm=bmb)
    g_a = _dw_into("dw_down", act, dh2_b, None, 1024, True, 1, 4, 0, piece_rows=2048)

    def up_bwd_fn(i, r, s, q, w):
        acc = _dot(r[0][:, :D], w[0][0], NT)
        for k in range(1, 4):
            acc = acc + _dot(r[0][:, D * k:D * (k + 1)], w[0][k], NT)
        dh, dg8 = _rms_bwd(acc, r[1], q[0])
        dh1_ = r[2] + dh
        return [dh1_, dh1_], [dg8], [], []
    (dh1, dh1_b), (dgmlp8,), _, _ = _rowwise("up_bwd", up_bwd_fn, nbig, rows=[dm, h1, dh2], pars=[p["g_mlp"]], refs=[w_up],
                                             out_rows=[(D, f32), (D, bf16)], out_accs=[D], bm=bmb)
    g_a = _dw_into("dw_up", n1, dm, g_a, 1024, False, 0, 4, 0, piece_rows=2048)

    def out_bwd_fn(i, r, s, q, w):
        dmix = [_dot(r[0], w[0][k], NT) for k in range(4)]
        v1, v2 = r[1][:, :D], r[1][:, D:]
        s2 = _sigmoid(v2)
        dglu, dg8 = _rms_bwd(jnp.concatenate(dmix[2:], axis=1), v1 * s2, q[0])
        dv = jnp.concatenate([dglu * s2, dglu * v1 * s2 * (1.0 - s2)], axis=1)
        return [jnp.concatenate(dmix[:2], axis=1), dv], [dg8, _rsum8(dv)], [], []
    (dys, dv), (dgs58, dbglu8), _, _ = _rowwise("out_bwd", out_bwd_fn, nbig, rows=[dh1_b, v], pars=[p["g_s5"]], refs=[w_out],
                                                out_rows=[(D, f32), (2 * D, bf16)], out_accs=[D, 2 * D], bm=bmb)
    g_b = _dw_into("dw_out_a", y_ssd, dh1_b, None, 512, True, 1, 2, 0, piece_rows=1024)
    g_b = _dw_into("dw_out_b", y_s5, dh1_b, g_b, 512, True, 1, 2, 2, piece_rows=1024)

    def glu_bwd_fn(i, r, s, q, w):
        acc = _dot(r[0][:, :512], w[0][0])
        for k in range(1, 4):
            acc = acc + _dot(r[0][:, 512 * k:512 * (k + 1)], w[0][k])
        yl = r[1]
        cdf = 0.5 * (1.0 + lax.erf(yl * (1.0 / math.sqrt(2.0))))
        pdf = jnp.exp(-0.5 * yl * yl) * (1.0 / math.sqrt(2.0 * math.pi))
        return [acc * (cdf + yl * pdf)], [], [], []
    (dylin,), _, _, _ = _rowwise("glu_bwd", glu_bwd_fn, nbig, rows=[dv, ylin], refs=[w_glu_t], out_rows=[(D, f32)], bm=bmb)
    g_b = _dw_into("dw_glu", dv, y5, g_b, 512, True, 0, 4, 0, piece_rows=1024)

    (du, dcc, dbb, dab, dds5), (land_a,) = _s5_bwd(proj, dylin, s_all, jnp.swapaxes(bbq, 1, 2).astype(bf16), ccq.astype(bf16),
                                                   ar, ai, d_skip, nch, comm=_scatter_piece(g_a))

    s8 = lambda a: jnp.sum(a, axis=0, keepdims=True)
    dab_q = jnp.swapaxes(dab.reshape(2, 4, NQ, 128), 1, 2).reshape(2, NQ * 8, S5_P)
    dbb_re, dbb_im = _blockdiag_in_grad(dbb)
    dcr, dci = _blockdiag_in_grad(dcc)
    _, vjp = jax.vjp(_s5_tables, p["lam_re"], p["lam_im"], p["log_step"], p["b_re"], p["b_im"])
    dlam_re, dlam_im, dlog_step, db_re, db_im = vjp((dab_q[0], dab_q[1], dbb_re, dbb_im))
    early = dict(lam_re=dlam_re, lam_im=dlam_im, log_step=dlog_step, b_re=db_re, b_im=db_im, c_re=jnp.swapaxes(dcr, 1, 2),
                 c_im=-jnp.swapaxes(dci, 1, 2), d_s5=s8(dds5).reshape(NQ * 8, 16), b_glu=s8(dbglu8), g_s5=s8(dgs58),
                 g_mlp=s8(dgmlp8), g_final=s8(dgf8).reshape(D))
    early_pack = _pack_small([early[n] for n in EARLY], _rows_for(EARLY))

    (dproj, dcw8, dcb8, ddtb8, dal8, dd8, dgssd8), (land_b, all_early) = _ssd_bwd(
        proj, xc_all, y, dys, du, states, p["conv_w"], p["conv_b"], p["dt_bias"], p["a_log"], p["d_ssd"], p["g_ssd"], nch,
        comm=_both(_scatter_piece(g_b), _gather_blocks(early_pack)))

    gt = _dw_in_t(dproj, n0)
    gt = jnp.concatenate([gt[0:1024], gt[O_XBC:O_XBC + D_XBC], gt[O_DT:O_DT + HEADS], gt[O_U:O_U + D]], axis=0).reshape(4, 900, D)
    g_c = jnp.concatenate([gt, jnp.zeros((4, 1024 - 900, D), bf16)], axis=1)

    def in_bwd_fn(i, r, s, q, w):
        dh, dg8 = _rms_bwd(_dot(r[0], w[0][...]), h0_of(i, s, q), q[1])
        dh0 = r[1] + dh
        return [], [dg8], [dh0], [dh0]
    _, (dgmix8,), (dmeta_blk,), (grad_x,), (land_c,) = _rowwise(
        "in_bwd", in_bwd_fn, nch, rows=[dproj, dh1], shifted=[x2], pars=[metablk, p["g_mix"]], refs=[w_full], out_accs=[D],
        out_first=[(D, f32)], out_shifted=[(D, f32)], comm=_scatter_piece(g_c))

    hsum = lambda a: jnp.sum(s8(a).reshape(HEADS, HEAD_DIM), axis=1).reshape(1, HEADS)
    late = dict(g_mix=s8(dgmix8), conv_b=s8(dcb8), dt_bias=s8(ddtb8), a_log=s8(dal8), d_ssd=hsum(dd8), g_ssd=s8(dgssd8),
                conv_w=dcw8[0:4], meta_tokens=dmeta_blk[CH - N_META:CH], loss=loss.reshape(1))
    return grad_x, [(g_a, land_a), (g_b, land_b), (g_c, land_c)], all_early, late


def _perm_rows_w_in(wt):
    return jnp.concatenate([wt[0:1024], wt[2576:3600], wt[1024:2560], wt[2560:2576],
                            jnp.zeros((W_PROJ - 3600, wt.shape[1]), wt.dtype)], axis=0)


def _unperm_cols_w_in(g):
    return jnp.concatenate([g[:, 0:1024], g[:, O_XBC:O_XBC + D_XBC], g[:, O_DT:O_DT + HEADS], g[:, O_U:O_U + D]], axis=1)


def _pack_shard(w_in, w_glu, w_out, w_up, w_down, spare):
    dt = w_in.dtype
    parts = [w_up, w_down, w_glu.T, w_out, w_in.T, spare,
             jnp.zeros((PACK_ROWS - R_SPARE - spare.shape[0], D), dt)]
    return jnp.concatenate(parts, axis=0)


def _allgather8(x_shard, name):
    m_per, n = x_shard.shape

    def body(x_ref, out_ref, send_sems, recv_sems, stage, local_sems):
        x, y, c = _place()
        me, sibling = (x, y, c), (x, y, 1 - c)
        chips = [(1 - x, y), (x, 1 - y), (1 - x, 1 - y)]

        def rows(px, py, pc):
            return out_ref.at[pl.ds((4 * px + 2 * py + pc) * m_per, m_per), :]

        def copy(k, block, to, src=None):
            return pltpu.make_async_remote_copy(
                src_ref=rows(*block) if src is None else src, dst_ref=rows(*block),
                send_sem=send_sems.at[k], recv_sem=recv_sems.at[k], device_id=to, device_id_type=MESH)

        load = pltpu.make_async_copy(x_ref, stage, local_sems.at[0])
        load.start()
        first = [copy(0, me, sibling, src=x_ref)]
        first += [copy(1 + j, me, (*chip, c), src=x_ref) for j, chip in enumerate(chips)]
        for cp in first:
            cp.start()
        load.wait()
        store = pltpu.make_async_copy(stage, rows(*me), local_sems.at[1])
        store.start()
        passed = [copy(4 + j, (*chip, c), sibling) for j, chip in enumerate(chips)]
        for j, chip in enumerate(chips):
            copy(1 + j, (*chip, c), me).wait_recv()
            passed[j].start()
        copy(0, sibling, me).wait_recv()
        for j, chip in enumerate(chips):
            copy(4 + j, (*chip, 1 - c), me).wait_recv()
        for cp in first + passed:
            cp.wait_send()
        store.wait()

    return pl.pallas_call(
        body, name=name, out_shape=jax.ShapeDtypeStruct((8 * m_per, n), x_shard.dtype),
        in_specs=[_ANY], out_specs=_ANY,
        scratch_shapes=[pltpu.SemaphoreType.DMA((7,)), pltpu.SemaphoreType.DMA((7,)), pltpu.VMEM((m_per, n), x_shard.dtype),
                        pltpu.SemaphoreType.DMA((2,))])(x_shard)


def _swap_sibling(r, name):
    def body(r_ref, out_ref, send_sem, recv_sem):
        x, y, c = _place()
        cp = pltpu.make_async_remote_copy(src_ref=r_ref, dst_ref=out_ref, send_sem=send_sem, recv_sem=recv_sem,
                                          device_id=(x, y, 1 - c), device_id_type=MESH)
        cp.start()
        cp.wait()

    return pl.pallas_call(
        body, name=name, out_shape=jax.ShapeDtypeStruct(r.shape, r.dtype), in_specs=[_ANY], out_specs=_ANY,
        scratch_shapes=[pltpu.SemaphoreType.DMA, pltpu.SemaphoreType.DMA])(r)


SH_CONVW, SH_META = 4 * 384, 16 * 256
SPARE_ROWS = 17

SMALL_SHAPES = dict(
    g_mix=(1, 1024), conv_b=(1, 1536), dt_bias=(1, 16), a_log=(1, 16), d_ssd=(1, 16), g_ssd=(1, 1024), lam_re=(1, 64, 64),
    lam_im=(1, 64, 64), log_step=(1, 64), b_re=(1, 64, 64, 16), b_im=(1, 64, 64, 16), c_re=(1, 64, 16, 64), c_im=(1, 64, 16, 64),
    d_s5=(1, 64, 16), b_glu=(1, 2048), g_s5=(1, 1024), g_mlp=(1, 1024), g_final=(1024,),
    conv_w=(4, D_XBC), meta_tokens=(N_META, D), loss=(1,))
EARLY = ["lam_re", "lam_im", "log_step", "b_re", "b_im", "c_re", "c_im", "d_s5", "b_glu", "g_s5", "g_mlp", "g_final"]
LATE = ["g_mix", "conv_b", "dt_bias", "a_log", "d_ssd", "g_ssd", "conv_w", "meta_tokens", "loss"]


def _rows_for(names):
    return -(-sum(math.prod(SMALL_SHAPES[n]) for n in names) // (8 * D)) * 8


def _pack_small(arrs, rows):
    flat = jnp.concatenate([a.reshape(-1).astype(f32) for a in arrs])
    return jnp.concatenate([flat, jnp.zeros((rows * D - flat.shape[0],), f32)]).reshape(rows, D)


def _unpack_small(slab, shapes):
    flat = slab.reshape(-1)
    out, o = [], 0
    for shp in shapes:
        n = math.prod(shp)
        out.append(flat[o:o + n].reshape(shp))
        o += n
    return out


def _sum8(g, rows, name):
    def body(g_ref, o_ref):
        acc = g_ref[0]
        for k in range(1, 8):
            acc = acc + g_ref[k]
        o_ref[...] = acc
    return pl.pallas_call(body, name=name, out_shape=jax.ShapeDtypeStruct((rows, D), f32),
                          compiler_params=_cp())(g.reshape(8, rows, D))


def _adam_math(w_, g_, m_, v_):
    m2 = ADAM_B1 * m_ + (1.0 - ADAM_B1) * g_
    v2 = ADAM_B2 * v_ + (1.0 - ADAM_B2) * jnp.square(g_)
    m_hat = m2 / (1.0 - ADAM_B1 ** ADAM_STEP)
    v_hat = v2 / (1.0 - ADAM_B2 ** ADAM_STEP)
    delta = -ADAM_LR * (m_hat / (jnp.sqrt(v_hat) + ADAM_EPS) + ADAM_WD * w_)
    return delta, m2, v2


def _adamw(name, w, g, m, v, bm):
    def fn(i, r, s, q, refs):
        return list(_adam_math(*r)), [], [], []
    c = w.shape[1]
    (d, m2, v2), _, _, _ = _rowwise(name, fn, w.shape[0] // bm, rows=[w, g, m, v], out_rows=[(c, f32)] * 3, bm=bm)
    return d, m2, v2


def _adamw_whole(name, w, g, m, v):
    def body(w_ref, g_ref, m_ref, v_ref, d_ref, m2_ref, v2_ref):
        d_ref[...], m2_ref[...], v2_ref[...] = _adam_math(w_ref[...], g_ref[...], m_ref[...], v_ref[...])
    return pl.pallas_call(body, name=name, out_shape=[jax.ShapeDtypeStruct(w.shape, f32)] * 3, compiler_params=_cp())(w, g, m, v)


def _sum_parts(name, own, land):
    def fn(i, r, s, q, refs):
        acc = r[0].astype(f32)
        for k in range(7):
            acc = acc + refs[0][k].astype(f32)
        return [acc], [], [], []
    (o,), _, _, _ = _rowwise(name, fn, own.shape[0] // CH, rows=[own],
                             refs=[(land, pl.BlockSpec((7, CH, D), lambda i: (0, i, 0)))], out_rows=[(D, f32)])
    return o


def kernel(x, meta_tokens, g_mix, w_in, conv_w, conv_b, dt_bias, a_log, d_ssd, g_ssd, lam_re, lam_im, log_step, b_re, b_im, c_re, c_im, d_s5, w_glu, b_glu, g_s5, w_out, g_mlp, w_up, w_down, g_final, loss_target, m_meta_tokens, m_g_mix, m_w_in, m_conv_w, m_conv_b, m_dt_bias, m_a_log, m_d_ssd, m_g_ssd, m_lam_re, m_lam_im, m_log_step, m_b_re, m_b_im, m_c_re, m_c_im, m_d_s5, m_w_glu, m_b_glu, m_g_s5, m_w_out, m_g_mlp, m_w_up, m_w_down, m_g_final, v_meta_tokens, v_g_mix, v_w_in, v_conv_w, v_conv_b, v_dt_bias, v_a_log, v_d_ssd, v_g_ssd, v_lam_re, v_lam_im, v_log_step, v_b_re, v_b_im, v_c_re, v_c_im, v_d_s5, v_w_glu, v_b_glu, v_g_s5, v_w_out, v_g_mlp, v_w_up, v_w_down, v_g_final):
    given = dict(locals())
    cx, cy, cc = _place()
    chip = 2 * cx + cy

    small_f = jnp.concatenate([conv_w.reshape(-1), meta_tokens.reshape(-1)])
    t_hi = small_f.astype(bf16)
    r_1 = small_f - t_hi.astype(f32)
    t_mid = r_1.astype(bf16)
    t_lo = (r_1 - t_mid.astype(f32)).astype(bf16)
    terms = jnp.concatenate([t_hi, t_mid, t_lo])
    spare = jnp.concatenate([terms, jnp.zeros((SPARE_ROWS * D - terms.shape[0],), bf16)]).reshape(SPARE_ROWS, D)
    slab = _pack_shard(w_in[0].astype(bf16), w_glu[0].astype(bf16), w_out[0].astype(bf16), w_up[0].astype(bf16),
                       w_down[0].astype(bf16), spare)
    my_half = lax.dynamic_slice_in_dim(slab, R_IN + cc * 512, 512, axis=0)
    gathered = _allgather8(my_half, "gather_w_in").reshape(4, 1024, D)
    w_in_t = _perm_rows_w_in(jnp.concatenate([gathered[s, 0:900] for s in range(4)], axis=0))
    n_sf = SH_CONVW + SH_META
    tr = gathered[:, 900:900 + SPARE_ROWS].reshape(4, SPARE_ROWS * D)[:, :3 * n_sf].astype(f32).reshape(4, 3, n_sf)
    sp = tr[:, 0] + tr[:, 1] + tr[:, 2]
    conv_w_full = jnp.concatenate([sp[s, :SH_CONVW].reshape(4, 384) for s in range(4)], axis=1)
    meta_full = jnp.concatenate([sp[s, SH_CONVW:].reshape(16, 256) for s in range(4)], axis=1)

    p = dict(g_mix=g_mix, conv_w=conv_w_full, conv_b=conv_b, dt_bias=dt_bias, a_log=a_log, d_ssd=d_ssd, g_ssd=g_ssd,
             lam_re=lam_re[0], lam_im=lam_im[0], log_step=log_step[0], b_re=b_re[0], b_im=b_im[0], c_re=c_re[0], c_im=c_im[0],
             d_s5=d_s5[0], b_glu=b_glu, g_s5=g_s5, g_mlp=g_mlp, g_final=g_final.reshape(1, D))
    grad_x, pieces, all_early, late = _local_step(x[0], loss_target[0], meta_full, p, w_in_t, slab)
    grad_x = grad_x.reshape(x.shape)

    reds = []
    for k, (gp, land) in enumerate(pieces):
        half = gp.shape[1] // 2
        own = lax.dynamic_slice(gp, (chip, cc * half, 0), (1, half, D)).reshape(half, D)
        reds.append(_sum_parts("rs_sum_%d" % k, own, land))
    red = jnp.concatenate(reds, axis=0)
    other = _swap_sibling(red, "rs_share")
    first = jnp.where(cc == 0, red, other)
    second = jnp.where(cc == 0, other, red)
    g_up, g_down = first[0:1024], second[0:1024]
    g_glu, g_out = first[1024:1536].T, second[1024:1536]
    g_in_t = jnp.concatenate([first[1536:2048], second[1536:1536 + 900 - 512]], axis=0)

    gs = dict(zip(EARLY, _unpack_small(_sum8(all_early, _rows_for(EARLY), "sum8_early"), [SMALL_SHAPES[n] for n in EARLY])))
    all_late = _allgather8(_pack_small([late[n] for n in LATE], _rows_for(LATE)), "gather_small")
    gs.update(zip(LATE, _unpack_small(_sum8(all_late, _rows_for(LATE), "sum8_late"), [SMALL_SHAPES[n] for n in LATE])))
    g_conv_w = lax.dynamic_slice_in_dim(gs.pop("conv_w"), chip * 384, 384, axis=1).reshape(conv_w.shape)
    g_meta = lax.dynamic_slice_in_dim(gs.pop("meta_tokens"), chip * 256, 256, axis=1)
    loss = gs.pop("loss").reshape(())

    grads = dict(gs, meta_tokens=g_meta, conv_w=g_conv_w, w_in=g_in_t.T.reshape(w_in.shape), w_glu=g_glu.reshape(w_glu.shape),
                 w_out=g_out.reshape(w_out.shape), w_up=g_up.reshape(w_up.shape), w_down=g_down.reshape(w_down.shape))
    delta, new_m, new_v = {}, {}, {}
    d_, m_, v_ = _adamw_whole("adamw_w_in", w_in[0].T, g_in_t, m_w_in[0].T, v_w_in[0].T)
    delta["w_in"], new_m["w_in"], new_v["w_in"] = (a.T.reshape(w_in.shape) for a in (d_, m_, v_))
    for n in ("w_glu", "w_out", "w_up", "w_down"):
        shp = given[n].shape
        two = lambda a: a.reshape(shp[1], shp[2])
        d_, m_, v_ = _adamw("adamw_" + n, two(given[n]), two(grads[n]), two(given["m_" + n]), two(given["v_" + n]), 256)
        delta[n], new_m[n], new_v[n] = d_.reshape(shp), m_.reshape(shp), v_.reshape(shp)
    for n in EARLY + LATE[:-1]:
        shp = given[n].shape
        two = (lambda a: a.reshape(1, -1)) if len(shp) == 1 else (lambda a: a)
        d_, m_, v_ = _adamw_whole("adamw_" + n, two(given[n]), two(grads[n].reshape(shp)), two(given["m_" + n]), two(given["v_" + n]))
        delta[n], new_m[n], new_v[n] = d_.reshape(shp), m_.reshape(shp), v_.reshape(shp)

    order = ["meta_tokens", "g_mix", "w_in", "conv_w", "conv_b", "dt_bias", "a_log", "d_ssd", "g_ssd", "lam_re", "lam_im", "log_step",
             "b_re", "b_im", "c_re", "c_im", "d_s5", "w_glu", "b_glu", "g_s5", "w_out", "g_mlp", "w_up", "w_down", "g_final"]
    grads_out = [grads[n].reshape(given[n].shape) for n in order]
    return (loss, grad_x, *grads_out, *[delta[n] for n in order], *[new_m[n] for n in order], *[new_v[n] for n in order])
```

```python
import math

import jax
import jax.numpy as jnp
from jax import lax
from jax.experimental import pallas as pl
from jax.experimental.pallas import tpu as pltpu

f32 = jnp.float32
bf16 = jnp.bfloat16

D = 1024
N_META = 16
CH = 256
HEADS = 16
HEAD_DIM = 64
NSTATE = 128
D_XBC = 1536
S5_P = 64
NQ = 8
PITCH = CH + 4
EPS = 1e-5
O_Z, O_U, O_XBC, O_DT, W_PROJ = 0, 1024, 2048, 3584, 3712
VMEM_LIMIT = 60 * 1024 * 1024

ADAM_LR, ADAM_B1, ADAM_B2, ADAM_EPS, ADAM_WD, ADAM_STEP = 0.001, 0.9, 0.999, 1e-08, 0.01, 10

NT = (((1,), (1,)), ((), ()))
TN = (((0,), (0,)), ((), ()))
_ANY = pl.BlockSpec(memory_space=pl.ANY)


def _cp(sem=None):
    return pltpu.CompilerParams(dimension_semantics=sem, vmem_limit_bytes=VMEM_LIMIT)


def _sigmoid(v):
    return 1.0 / (1.0 + jnp.exp(-v))


def _rsum8(v):
    r, c = v.shape
    return jnp.sum(v.reshape(r // 8, 8, c), axis=0)


def _rms(h, g):
    r = lax.rsqrt(jnp.mean(h * h, axis=-1, keepdims=True) + EPS)
    return h * r * g


def _rms_bwd(dy, h, g):
    r = lax.rsqrt(jnp.mean(h * h, axis=-1, keepdims=True) + EPS)
    n = h * r
    dn = dy * g
    dh = r * (dn - n * jnp.mean(dn * n, axis=-1, keepdims=True))
    return dh, _rsum8(dy * n)


def _dot(a, b, dims=None):
    if dims is None:
        return jnp.dot(a, b, preferred_element_type=f32)
    return lax.dot_general(a, b, dims, preferred_element_type=f32)


def _split_dot(v, m01, dims, terms, v_is_lhs=True):
    out, r = None, v
    for _ in range(terms):
        piece = r.astype(bf16)
        o = _dot(piece, m01, dims) if v_is_lhs else _dot(m01, piece, dims)
        out = o if out is None else out + o
        r = r - piece.astype(f32)
    return out


MESH = pl.DeviceIdType.MESH


def _place():
    return lax.axis_index("x"), lax.axis_index("y"), lax.axis_index("c")


def _flip(v, f):
    return 1 - v if f else v


def _call(body, comm, *, name, nsteps, in_specs, out_specs, out_shape, scratch_shapes, args):
    n_in, n_out, n_scr = len(in_specs), len(out_specs), len(scratch_shapes)
    if comm is None:
        res = pl.pallas_call(body, name=name, grid=(nsteps,), in_specs=in_specs, out_specs=out_specs, out_shape=out_shape,
                             scratch_shapes=scratch_shapes, compiler_params=_cp(("arbitrary",)))(*args)
        return list(res), []
    c_in, c_out = len(comm["ins"]), len(comm["outs"])

    def wrapped(*refs):
        o0 = n_in + c_in
        s0 = o0 + n_out + c_out
        cparts = (refs[n_in:o0], refs[o0 + n_out:s0], refs[s0 + n_scr:])

        @pl.when(pl.program_id(0) == 0)
        def _():
            comm["start"](*cparts)
        body(*refs[:n_in], *refs[o0:o0 + n_out], *refs[s0:s0 + n_scr])

        @pl.when(pl.program_id(0) == nsteps - 1)
        def _():
            comm["finish"](*cparts)

    any_spec = pl.BlockSpec(memory_space=pl.ANY)
    res = pl.pallas_call(
        wrapped, name=name, grid=(nsteps,), in_specs=list(in_specs) + [any_spec] * c_in,
        out_specs=list(out_specs) + [any_spec] * c_out, out_shape=list(out_shape) + list(comm["outs"]),
        scratch_shapes=list(scratch_shapes) + list(comm["scratch"]),
        compiler_params=_cp(("arbitrary",)))(*args, *comm["ins"])
    return list(res[:n_out]), list(res[n_out:])


def _gather_piece(slab, r0, rows):
    half = rows // 2

    def copies(slab_ref, out_ref, send_sems, recv_sems):
        x, y, c = _place()
        cps = []
        for fx, fy in ((1, 0), (0, 1), (1, 1)):
            for fc in (0, 1):
                k = (2 * fx + fy - 1) * 2 + fc
                cps.append(pltpu.make_async_remote_copy(
                    src_ref=slab_ref.at[pl.ds(r0 + c * half, half), :], dst_ref=out_ref.at[2 * x + y, pl.ds(c * half, half), :],
                    send_sem=send_sems.at[k], recv_sem=recv_sems.at[k],
                    device_id=(_flip(x, fx), _flip(y, fy), _flip(c, fc)), device_id_type=MESH))
        return cps

    def start(ins, outs, scr):
        send_sems, recv_sems, stage, local_sems = scr
        x, y, _ = _place()
        load = pltpu.make_async_copy(ins[0].at[pl.ds(r0, rows), :], stage, local_sems.at[0])
        load.start()
        for cp in copies(ins[0], outs[0], send_sems, recv_sems):
            cp.start()
        load.wait()
        pltpu.make_async_copy(stage, outs[0].at[2 * x + y], local_sems.at[1]).start()

    def finish(ins, outs, scr):
        send_sems, recv_sems, stage, local_sems = scr
        x, y, _ = _place()
        for cp in copies(ins[0], outs[0], send_sems, recv_sems):
            cp.wait()
        pltpu.make_async_copy(stage, outs[0].at[2 * x + y], local_sems.at[1]).wait()

    return dict(ins=[slab], outs=[jax.ShapeDtypeStruct((4, rows, D), bf16)],
                scratch=[pltpu.SemaphoreType.DMA((6,)), pltpu.SemaphoreType.DMA((6,)), pltpu.VMEM((rows, D), bf16),
                         pltpu.SemaphoreType.DMA((2,))], start=start, finish=finish)


def _scatter_piece(gpiece):
    half = gpiece.shape[1] // 2

    def copies(g_ref, land_ref, send_sems, recv_sems):
        x, y, c = _place()
        cps = []
        for fx in (0, 1):
            for fy in (0, 1):
                for fc in (0, 1):
                    k = 4 * fx + 2 * fy + fc - 1
                    if k < 0:
                        continue
                    px, py, pc = _flip(x, fx), _flip(y, fy), _flip(c, fc)
                    cps.append(pltpu.make_async_remote_copy(
                        src_ref=g_ref.at[2 * px + py, pl.ds(pc * half, half), :], dst_ref=land_ref.at[k],
                        send_sem=send_sems.at[k], recv_sem=recv_sems.at[k], device_id=(px, py, pc), device_id_type=MESH))
        return cps

    def start(ins, outs, scr):
        for cp in copies(ins[0], outs[0], *scr):
            cp.start()

    def finish(ins, outs, scr):
        for cp in copies(ins[0], outs[0], *scr):
            cp.wait()

    return dict(ins=[gpiece], outs=[jax.ShapeDtypeStruct((7, half, D), gpiece.dtype)],
                scratch=[pltpu.SemaphoreType.DMA((7,)), pltpu.SemaphoreType.DMA((7,))], start=start, finish=finish)


def _gather_blocks(block):
    rows = block.shape[0]

    def mine(out_ref):
        x, y, c = _place()
        return out_ref.at[pl.ds((4 * x + 2 * y + c) * rows, rows), :]

    def copies(b_ref, out_ref, send_sems, recv_sems):
        x, y, c = _place()
        cps = []
        for fx in (0, 1):
            for fy in (0, 1):
                for fc in (0, 1):
                    k = 4 * fx + 2 * fy + fc - 1
                    if k < 0:
                        continue
                    cps.append(pltpu.make_async_remote_copy(
                        src_ref=b_ref, dst_ref=mine(out_ref), send_sem=send_sems.at[k], recv_sem=recv_sems.at[k],
                        device_id=(_flip(x, fx), _flip(y, fy), _flip(c, fc)), device_id_type=MESH))
        return cps

    def start(ins, outs, scr):
        send_sems, recv_sems, stage, local_sems = scr
        load = pltpu.make_async_copy(ins[0], stage, local_sems.at[0])
        load.start()
        for cp in copies(ins[0], outs[0], send_sems, recv_sems):
            cp.start()
        load.wait()
        pltpu.make_async_copy(stage, mine(outs[0]), local_sems.at[1]).start()

    def finish(ins, outs, scr):
        send_sems, recv_sems, stage, local_sems = scr
        for cp in copies(ins[0], outs[0], send_sems, recv_sems):
            cp.wait()
        pltpu.make_async_copy(stage, mine(outs[0]), local_sems.at[1]).wait()

    return dict(ins=[block], outs=[jax.ShapeDtypeStruct((8 * rows, D), block.dtype)],
                scratch=[pltpu.SemaphoreType.DMA((7,)), pltpu.SemaphoreType.DMA((7,)), pltpu.VMEM((rows, D), block.dtype),
                         pltpu.SemaphoreType.DMA((2,))], start=start, finish=finish)


def _both(c1, c2):
    n = (len(c1["ins"]), len(c1["outs"]), len(c1["scratch"]))

    def split(parts):
        return [p[:k] for p, k in zip(parts, n)], [p[k:] for p, k in zip(parts, n)]

    def start(*parts):
        a, b = split(parts)
        c1["start"](*a)
        c2["start"](*b)

    def finish(*parts):
        a, b = split(parts)
        c1["finish"](*a)
        c2["finish"](*b)

    return dict(ins=c1["ins"] + c2["ins"], outs=c1["outs"] + c2["outs"], scratch=c1["scratch"] + c2["scratch"],
                start=start, finish=finish)


def _rowwise(name, fn, nblk, rows=(), shifted=(), pars=(), refs=(), out_rows=(), out_accs=(), bm=CH, comm=None):
    n_sub = bm // CH
    n_r, n_s, n_p, n_w = len(rows), len(shifted) * n_sub, len(pars), len(refs)
    n_in = n_r + n_s + n_p + n_w
    n_o, n_a = len(out_rows), len(out_accs)

    def body(*all_refs):
        i = pl.program_id(0)
        ins = all_refs[:n_in]
        outs = all_refs[n_in:]
        rv = [r[...] for r in ins[:n_r]]
        sub = ins[n_r:n_r + n_s]
        sv = [jnp.concatenate([r[...] for r in sub[k * n_sub:(k + 1) * n_sub]], axis=0) if n_sub > 1 else sub[k][...]
              for k in range(len(shifted))]
        pv = [r[...] for r in ins[n_r + n_s:n_r + n_s + n_p]]
        ro, ao = fn(i, rv, sv, pv, list(ins[n_r + n_s + n_p:]))
        for r, v in zip(outs[:n_o], ro):
            r[...] = v.astype(r.dtype)
        accs = outs[n_o:]

        @pl.when(i == 0)
        def _():
            for r in accs:
                r[...] = jnp.zeros_like(r)
        for r, v in zip(accs, ao):
            r[...] += v

    in_specs = [pl.BlockSpec((bm, a.shape[1]), lambda i: (i, 0)) for a in rows]
    in_specs += [pl.BlockSpec((CH, a.shape[1]), lambda i, j=j: (jnp.maximum(n_sub * i - 1 + j, 0), 0))
                 for a in shifted for j in range(n_sub)]
    in_specs += [pl.BlockSpec(a.shape, lambda i, nd=a.ndim: (0,) * nd) for a in pars]
    in_specs += [spec for _, spec in refs]
    out_specs = [pl.BlockSpec((bm, c), lambda i: (i, 0)) for c, _ in out_rows]
    out_specs += [pl.BlockSpec((8, c), lambda i: (0, 0)) for c in out_accs]
    out_shape = [jax.ShapeDtypeStruct((nblk * bm, c), dt) for c, dt in out_rows]
    out_shape += [jax.ShapeDtypeStruct((8, c), f32) for c in out_accs]
    res, cres = _call(body, comm, name=name, nsteps=nblk, in_specs=in_specs, out_specs=out_specs, out_shape=out_shape,
                      scratch_shapes=[], args=[*rows, *[a for a in shifted for _ in range(n_sub)], *pars, *[a for a, _ in refs]])
    parts = (res[:n_o], res[n_o:])
    return parts if comm is None else parts + (cres,)


PACK_ROWS = 4096
HALF_ROWS = PACK_ROWS // 2
R_UP, R_DOWN, R_GLU, R_OUT, R_IN, R_SPARE = 0, 1024, 2048, 2560, 3072, 3972


BIG_ROWS = 768
DW_ROWS = 2816


def _contract_rows(lp, big=DW_ROWS):
    for rows in (big, BIG_ROWS):
        if lp % rows == 0:
            return rows
    return CH


def _dw_into(name, a, b, slab, ka, a_sharded, row_blk, n_s, s0, piece_rows=2048, a_square=False):
    lp = a.shape[0]
    bm = _contract_rows(lp)
    steps = lp // bm

    def body(a_ref, b_ref, *rest):
        o_ref, acc = rest[-2], rest[-1]
        k = pl.program_id(1)

        @pl.when(k == 0)
        def _():
            acc[...] = jnp.zeros_like(acc)
        a_v = a_ref[...]
        acc[...] += _dot(a_v * a_v if a_square else a_v, b_ref[...], TN)

        @pl.when(k == steps - 1)
        def _():
            o_ref[0] = acc[...].astype(bf16)

    in_specs = [pl.BlockSpec((bm, ka), (lambda s, k: (k, s)) if a_sharded else (lambda s, k: (k, 0))),
                pl.BlockSpec((bm, D), (lambda s, k: (k, 0)) if a_sharded else (lambda s, k: (k, s)))]
    args = [a, b]
    aliases = {}
    if slab is not None:
        in_specs.append(_ANY)
        args.append(slab)
        aliases = {2: 0}
    return pl.pallas_call(
        body, name=name, grid=(n_s, steps), in_specs=in_specs,
        out_specs=pl.BlockSpec((1, ka, D), lambda s, k: (s0 + s, row_blk, 0)),
        out_shape=jax.ShapeDtypeStruct((4, piece_rows, D), bf16),
        scratch_shapes=[pltpu.VMEM((ka, D), f32)], input_output_aliases=aliases,
        compiler_params=_cp(("arbitrary", "arbitrary")))(*args)


def _dw_in_t(dproj, n0):
    lp = n0.shape[0]
    bm = _contract_rows(lp, BIG_ROWS)
    steps = lp // bm
    bn = 512

    def body(a_ref, b_ref, o_ref, acc):
        k = pl.program_id(1)

        @pl.when(k == 0)
        def _():
            acc[...] = jnp.zeros_like(acc)
        acc[...] += _dot(a_ref[...], b_ref[...], TN)

        @pl.when(k == steps - 1)
        def _():
            o_ref[...] = acc[...].astype(bf16)

    return pl.pallas_call(
        body, name="dw_in", grid=(D // bn, steps),
        in_specs=[pl.BlockSpec((bm, W_PROJ), lambda j, k: (k, 0)), pl.BlockSpec((bm, bn), lambda j, k: (k, j))],
        out_specs=pl.BlockSpec((W_PROJ, bn), lambda j, k: (0, j)),
        out_shape=jax.ShapeDtypeStruct((W_PROJ, D), bf16),
        scratch_shapes=[pltpu.VMEM((W_PROJ, bn), f32)],
        compiler_params=_cp(("arbitrary", "arbitrary")))(dproj, n0)


def _head_expand():
    h = lax.broadcasted_iota(jnp.int32, (HEADS, D), 0)
    c = lax.broadcasted_iota(jnp.int32, (HEADS, D), 1)
    return jnp.where((c >> 6) == h, 1.0, 0.0).astype(bf16)


def _ssd_common(i, P, prev8, cw, cb, dtb, alog, xc=None):
    z = P[:, O_Z:O_Z + D]
    xp = P[:, O_XBC:O_XBC + D_XBC]
    dt_raw = P[:, O_DT:O_DT + HEADS]
    row = lax.broadcasted_iota(jnp.int32, (CH, 1), 0)
    if xc is None:
        row8 = lax.broadcasted_iota(jnp.int32, (8, 1), 0)
        xc = cb + cw[3:4] * xp
        for k in (1, 2, 3):
            rolled = pltpu.roll(xp, k, 0)
            fix = pltpu.roll(prev8, k, 0)
            top = jnp.where(row8 < k, fix, rolled[0:8])
            xc = xc + cw[3 - k:4 - k] * jnp.concatenate([top, rolled[8:]], axis=0)
    sg = _sigmoid(xc)
    xbc = xc * sg
    live = jnp.where(jnp.logical_or(i > 0, row >= CH - N_META), 1.0, 0.0)
    pre = dt_raw + dtb
    dt = jnp.where(pre > 20.0, pre, jnp.log(1.0 + jnp.exp(jnp.minimum(pre, 20.0)))) * live
    a = -jnp.exp(alog)
    dta = dt * a
    r_i = lax.broadcasted_iota(jnp.int32, (CH, CH), 0)
    c_i = lax.broadcasted_iota(jnp.int32, (CH, CH), 1)
    tril = r_i >= c_i
    acs = _split_dot(dta, jnp.where(tril, 1.0, 0.0).astype(bf16), None, 3, v_is_lhs=False)
    acs_t = _split_dot(dta, jnp.where(r_i <= c_i, 1.0, 0.0).astype(bf16), TN, 3)
    e = _head_expand()
    acs_e = _split_dot(acs, e, None, 3)
    dt_e = _split_dot(dt, e, None, 3)
    return dict(z=z, xp=xp, xc=xc, sg=sg, xbc=xbc, live=live, pre=pre, dt=dt, a=a, tril=tril,
                acs=acs, acs_t=acs_t, e=e, acs_e=acs_e, dt_e=dt_e)


def _lmat(c, h):
    seg = c["acs"][:, h:h + 1] - c["acs_t"][h:h + 1, :]
    return jnp.where(c["tril"], jnp.exp(jnp.minimum(seg, 0.0)), 0.0)


def _pair_masks():
    lane = lax.broadcasted_iota(jnp.int32, (1, 128), 1)
    return jnp.where(lane < HEAD_DIM, 1.0, 0.0), jnp.where(lane >= HEAD_DIM, 1.0, 0.0)


def _ssd_fwd(proj, conv_w, conv_b, dt_bias, a_log, d_ssd, g_ssd, nch, comm=None):
    def body(p_ref, cw_ref, cb_ref, dtb_ref, al_ref, d_ref, g_ref, y_ref, ys_ref, st_ref, xc_ref, prev8_ref, state_ref):
        i = pl.program_id(0)

        @pl.when(i == 0)
        def _():
            prev8_ref[...] = jnp.zeros_like(prev8_ref)
            state_ref[...] = jnp.zeros_like(state_ref)

        P = p_ref[...]
        c = _ssd_common(i, P, prev8_ref[...], cw_ref[...], cb_ref[...], dtb_ref[...], al_ref[...])
        prev8_ref[...] = c["xp"][CH - 8:CH]
        xc_ref[...] = c["xc"]
        xbc = c["xbc"]
        x = xbc[:, 0:D]
        xdt = x * c["dt_e"]
        a_last_e = c["acs_e"][CH - 1:CH, :]
        w_end = (xdt * jnp.exp(a_last_e - c["acs_e"])).astype(bf16)
        m0, m1 = _pair_masks()
        ys = []
        for g in range(2):
            bg = xbc[:, D + NSTATE * g:D + NSTATE * (g + 1)].astype(bf16)
            cg = xbc[:, D + 2 * NSTATE + NSTATE * g:D + 2 * NSTATE + NSTATE * (g + 1)].astype(bf16)
            gmat = _dot(cg, bg, NT)
            st = state_ref[g]
            st_ref[0, g] = st
            sl = slice(512 * g, 512 * (g + 1))
            y_off = _dot(cg, st.astype(bf16)) * jnp.exp(c["acs_e"][:, sl])
            contrib = _dot(bg, w_end[:, sl], TN)
            state_ref[g] = st * jnp.exp(a_last_e[:, sl]) + contrib
            yd = []
            for pr in range(4):
                h0 = 8 * g + 2 * pr
                xp2 = xdt[:, 128 * (4 * g + pr):128 * (4 * g + pr + 1)]
                ma = (gmat * _lmat(c, h0)).astype(bf16)
                mb = (gmat * _lmat(c, h0 + 1)).astype(bf16)
                yd.append(_dot(ma, (xp2 * m0).astype(bf16)) + _dot(mb, (xp2 * m1).astype(bf16)))
            ys.append(jnp.concatenate(yd, axis=1) + y_off)
        d_e = _split_dot(d_ref[...], c["e"], None, 3)
        y = jnp.concatenate(ys, axis=1) + x * d_e
        y_ref[...] = y
        yg = y * (c["z"] * _sigmoid(c["z"]))
        ys_ref[...] = _rms(yg, g_ref[...]).astype(bf16)

    full = lambda a: pl.BlockSpec(a.shape, lambda i, nd=a.ndim: (0,) * nd)
    return _call(
        body, comm, name="ssd_fwd", nsteps=nch,
        in_specs=[pl.BlockSpec((CH, W_PROJ), lambda i: (i, 0))] + [full(a) for a in (conv_w, conv_b, dt_bias, a_log, d_ssd, g_ssd)],
        out_specs=[pl.BlockSpec((CH, D), lambda i: (i, 0)), pl.BlockSpec((CH, D), lambda i: (i, 0)),
                   pl.BlockSpec((1, 2, NSTATE, 512), lambda i: (i, 0, 0, 0)), pl.BlockSpec((CH, D_XBC), lambda i: (i, 0))],
        out_shape=[jax.ShapeDtypeStruct((nch * CH, D), f32), jax.ShapeDtypeStruct((nch * CH, D), bf16),
                   jax.ShapeDtypeStruct((nch, 2, NSTATE, 512), f32), jax.ShapeDtypeStruct((nch * CH, D_XBC), f32)],
        scratch_shapes=[pltpu.VMEM((8, D_XBC), f32), pltpu.VMEM((2, NSTATE, 512), f32)],
        args=[proj, conv_w, conv_b, dt_bias, a_log, d_ssd, g_ssd])


def _ssd_bwd(proj, xc_all, y, dys, du, states, conv_w, conv_b, dt_bias, a_log, d_ssd, g_ssd, nch, comm=None):
    def body(p_ref, xc_ref, y_ref, dys_ref, du_ref, st_ref, cw_ref, cb_ref, dtb_ref, al_ref, d_ref, g_ref,
             dp_ref, dcw_ref, dcb_ref, ddtb_ref, dal_ref, dd_ref, dg_ref, nxt8_ref, dst_ref):
        step = pl.program_id(0)
        i = nch - 1 - step

        @pl.when(step == 0)
        def _():
            nxt8_ref[...] = jnp.zeros_like(nxt8_ref)
            dst_ref[...] = jnp.zeros_like(dst_ref)
            for r in (dcw_ref, dcb_ref, ddtb_ref, dal_ref, dd_ref, dg_ref):
                r[...] = jnp.zeros_like(r)

        P = p_ref[...]
        c = _ssd_common(i, P, None, cw_ref[...], cb_ref[...], dtb_ref[...], al_ref[...], xc=xc_ref[...])
        xbc, z, e = c["xbc"], c["z"], c["e"]
        x = xbc[:, 0:D]
        yv = y_ref[...]
        sz = _sigmoid(z)
        silu_z = z * sz
        dyg, dg8 = _rms_bwd(dys_ref[...], yv * silu_z, g_ref[...])
        dg_ref[...] += dg8
        dy = dyg * silu_z
        dz = dyg * yv * (sz * (1.0 + z * (1.0 - sz)))
        d_e = _split_dot(d_ref[...], e, None, 3)
        dd_ref[...] += _rsum8(dy * x)
        xdt = x * c["dt_e"]
        a_last_e = c["acs_e"][CH - 1:CH, :]
        e_end = jnp.exp(a_last_e - c["acs_e"])
        w_end = xdt * e_end
        e_acs = jnp.exp(c["acs_e"])
        dy_dec = dy * e_acs
        m0, m1 = _pair_masks()
        lane16 = lax.broadcasted_iota(jnp.int32, (1, HEADS), 1)
        row16 = lax.broadcasted_iota(jnp.int32, (HEADS, 1), 0)
        dacs = jnp.zeros((CH, HEADS), f32)
        dacs_t = jnp.zeros((HEADS, CH), f32)
        dxdt_parts, dbs, dcs, zparts, yoff_parts, dlast_parts = [], [], [], [], [], []
        for g in range(2):
            sl = slice(512 * g, 512 * (g + 1))
            bg = xbc[:, D + NSTATE * g:D + NSTATE * (g + 1)].astype(bf16)
            cg = xbc[:, D + 2 * NSTATE + NSTATE * g:D + 2 * NSTATE + NSTATE * (g + 1)].astype(bf16)
            gmat = _dot(cg, bg, NT)
            st = st_ref[0, g]
            dstn = dst_ref[g]
            dstn_b = dstn.astype(bf16)
            y_off = _dot(cg, st.astype(bf16)) * e_acs[:, sl]
            yoff_parts.append(y_off)
            bds = _dot(bg, dstn_b)
            zparts.append(w_end[:, sl] * bds)
            dlast_parts.append(jnp.sum(dstn * st, axis=0, keepdims=True) * jnp.exp(a_last_e[:, sl]))
            dg_acc = jnp.zeros((CH, CH), f32)
            dxd = []
            for pr in range(4):
                lo = 128 * (4 * g + pr)
                xp2 = xdt[:, lo:lo + 128].astype(bf16)
                dy2 = dy[:, lo:lo + 128]
                outp = jnp.zeros((CH, 128), f32)
                for hh, msk in ((0, m0), (1, m1)):
                    h = 8 * g + 2 * pr + hh
                    lm = _lmat(c, h)
                    dyh = (dy2 * msk).astype(bf16)
                    mh = (gmat * lm).astype(bf16)
                    outp = outp + _dot(mh, dyh, TN)
                    dml = _dot(dyh, xp2, NT) * lm
                    dg_acc = dg_acc + dml
                    q = dml * gmat
                    dacs = dacs + jnp.where(lane16 == h, jnp.sum(q, axis=1, keepdims=True), 0.0)
                    dacs_t = dacs_t + jnp.where(row16 == h, jnp.sum(q, axis=0, keepdims=True), 0.0)
                dxd.append(outp)
            dxdt_parts.append(jnp.concatenate(dxd, axis=1) + e_end[:, sl] * bds)
            dgb = dg_acc.astype(bf16)
            dcs.append(_dot(dgb, bg) + _dot(dy_dec[:, sl].astype(bf16), st.astype(bf16), NT))
            dbs.append(_dot(dgb, cg, TN) + _dot(w_end[:, sl].astype(bf16), dstn_b, NT))
            dst_ref[g] = dstn * jnp.exp(a_last_e[:, sl]) + _dot(cg, dy_dec[:, sl].astype(bf16), TN)
        dxdt = jnp.concatenate(dxdt_parts, axis=1)
        zfull = jnp.concatenate(zparts, axis=1)
        y_off_full = jnp.concatenate(yoff_parts, axis=1)
        dlast = jnp.concatenate(dlast_parts, axis=1)
        red = lambda v: _split_dot(v, e, NT, 2)
        eye16 = jnp.where(lax.broadcasted_iota(jnp.int32, (HEADS, HEADS), 0) == lax.broadcasted_iota(jnp.int32, (HEADS, HEADS), 1),
                          1.0, 0.0).astype(bf16)
        dacs = dacs - _split_dot(dacs_t, eye16, TN, 3)
        zred = red(zfull)
        dacs = dacs + red(dy * y_off_full) - zred
        last_term = jnp.sum(zred, axis=0, keepdims=True) + red(dlast)
        rowc = lax.broadcasted_iota(jnp.int32, (CH, 1), 0)
        dacs = dacs + jnp.where(rowc == CH - 1, last_term, 0.0)
        r_i = lax.broadcasted_iota(jnp.int32, (CH, CH), 0)
        c_i = lax.broadcasted_iota(jnp.int32, (CH, CH), 1)
        ddta = _split_dot(dacs, jnp.where(c_i >= r_i, 1.0, 0.0).astype(bf16), None, 3, v_is_lhs=False)
        ddt = ddta * c["a"] + red(dxdt * x)
        dal_ref[...] += _rsum8(ddta * c["dt"] * c["a"])
        ddt_raw = ddt * _sigmoid(c["pre"]) * c["live"]
        ddtb_ref[...] += _rsum8(ddt_raw)
        dx = dy * d_e + dxdt * c["dt_e"]
        dxbc = jnp.concatenate([dx, dbs[0], dbs[1], dcs[0], dcs[1]], axis=1)
        sg = c["sg"]
        dxc = dxbc * (sg * (1.0 + c["xc"] * (1.0 - sg)))
        dcb_ref[...] += _rsum8(dxc)
        xp = c["xp"]
        row8 = lax.broadcasted_iota(jnp.int32, (8, 1), 0)
        cw = cw_ref[...]
        dxp = cw[3:4] * dxc
        dcw = jnp.where(row8 == 3, jnp.sum(dxc * xp, axis=0, keepdims=True), 0.0)
        nxt8 = nxt8_ref[...]
        for j in (1, 2, 3):
            rolled = pltpu.roll(dxc, CH - j, 0)
            fix = pltpu.roll(nxt8, 8 - j, 0)
            bot = jnp.where(row8 >= 8 - j, fix, rolled[CH - 8:CH])
            later = jnp.concatenate([rolled[:CH - 8], bot], axis=0)
            dxp = dxp + cw[3 - j:4 - j] * later
            dcw = dcw + jnp.where(row8 == 3 - j, jnp.sum(later * xp, axis=0, keepdims=True), 0.0)
        dcw_ref[...] += dcw
        nxt8_ref[...] = dxc[0:8]
        dp_ref[:, O_Z:O_Z + D] = dz.astype(bf16)
        dp_ref[:, O_U:O_U + D] = du_ref[...].astype(bf16)
        dp_ref[:, O_XBC:O_XBC + D_XBC] = dxp.astype(bf16)
        dp_ref[:, O_DT:W_PROJ] = jnp.zeros((CH, W_PROJ - O_DT), bf16)
        dp_ref[:, O_DT:O_DT + HEADS] = ddt_raw.astype(bf16)

    full = lambda a: pl.BlockSpec(a.shape, lambda s, nd=a.ndim: (0,) * nd)
    rev = lambda s: (nch - 1 - s, 0)
    acc = lambda cdim: pl.BlockSpec((8, cdim), lambda s: (0, 0))
    return _call(
        body, comm, name="ssd_bwd", nsteps=nch,
        in_specs=[pl.BlockSpec((CH, W_PROJ), rev), pl.BlockSpec((CH, D_XBC), rev),
                  pl.BlockSpec((CH, D), rev), pl.BlockSpec((CH, D), rev), pl.BlockSpec((CH, D), rev),
                  pl.BlockSpec((1, 2, NSTATE, 512), lambda s: (nch - 1 - s, 0, 0, 0))]
        + [full(a) for a in (conv_w, conv_b, dt_bias, a_log, d_ssd, g_ssd)],
        out_specs=[pl.BlockSpec((CH, W_PROJ), rev), acc(D_XBC), acc(D_XBC), acc(HEADS), acc(HEADS), acc(D), acc(D)],
        out_shape=[jax.ShapeDtypeStruct((nch * CH, W_PROJ), bf16)]
        + [jax.ShapeDtypeStruct((8, cdim), f32) for cdim in (D_XBC, D_XBC, HEADS, HEADS, D, D)],
        scratch_shapes=[pltpu.VMEM((8, D_XBC), f32), pltpu.VMEM((2, NSTATE, 512), f32)],
        args=[proj, xc_all, y, dys, du, states, conv_w, conv_b, dt_bias, a_log, d_ssd, g_ssd])


SCAN_UNROLL = 8


def _to_slabs(slab_ref, q, mat):
    for ls in range(8):
        slab_ref[ls, pl.ds(PITCH * q, CH), :] = mat[:, 128 * ls:128 * (ls + 1)]


def _from_slabs(slab, q):
    return jnp.concatenate([slab(ls, PITCH * q) for ls in range(8)], axis=1)


def _tile(slab_ref, ls, t, lead=None):
    idx = (ls, pl.ds(t, 8, stride=PITCH), slice(None))
    return slab_ref[idx] if lead is None else slab_ref[(lead,) + idx]


def _s5_fwd(proj, bbq, ccq_t, ar, ai, d_skip, nch, comm=None):
    def body(u_ref, bb_ref, cc_ref, ar_ref, ai_ref, d_ref, s_ref, yl_ref, y5_ref, bu_ref, st_ref):
        @pl.when(pl.program_id(0) == 0)
        def _():
            st_ref[...] = jnp.zeros_like(st_ref)
        u = u_ref[...]
        ub = u.astype(bf16)
        for q in range(NQ):
            _to_slabs(bu_ref, q, _dot(ub[:, 128 * q:128 * (q + 1)], bb_ref[q]))
        ar_t = [ar_ref[:, 128 * l:128 * (l + 1)] for l in range(4)]
        ai_t = [ai_ref[:, 128 * l:128 * (l + 1)] for l in range(4)]

        def one(t, carry):
            re, im = carry
            nre, nim = [], []
            for l in range(4):
                a = ar_t[l] * re[l] - ai_t[l] * im[l] + _tile(bu_ref, l, t)
                b = ar_t[l] * im[l] + ai_t[l] * re[l] + _tile(bu_ref, l + 4, t)
                s_ref[0, l, pl.ds(t, 8, stride=PITCH), :] = a
                s_ref[0, l + 4, pl.ds(t, 8, stride=PITCH), :] = b
                nre.append(a)
                nim.append(b)
            return tuple(nre), tuple(nim)

        def step(tt, carry):
            for k in range(SCAN_UNROLL):
                carry = one(tt * SCAN_UNROLL + k, carry)
            return carry
        init = (tuple(st_ref[l] for l in range(4)), tuple(st_ref[l + 4] for l in range(4)))
        re, im = lax.fori_loop(0, CH // SCAN_UNROLL, step, init)
        for l in range(4):
            st_ref[l] = re[l]
            st_ref[l + 4] = im[l]
        ys = []
        for q in range(NQ):
            sq = _from_slabs(lambda ls, r0: s_ref[0, ls, pl.ds(r0, CH), :], q).astype(bf16)
            ys.append(_dot(sq, cc_ref[q]))
        yl = jnp.concatenate(ys, axis=1) + u * d_ref[...]
        yl_ref[...] = yl
        y5_ref[...] = (0.5 * yl * (1.0 + lax.erf(yl * (1.0 / math.sqrt(2.0))))).astype(bf16)

    const = lambda a: pl.BlockSpec(a.shape, lambda i, nd=a.ndim: (0,) * nd)
    return _call(
        body, comm, name="s5_fwd", nsteps=nch,
        in_specs=[pl.BlockSpec((CH, D), lambda i: (i, O_U // D)), const(bbq), const(ccq_t), const(ar), const(ai), const(d_skip)],
        out_specs=[pl.BlockSpec((1, 8, 8 * PITCH, 128), lambda i: (i, 0, 0, 0)),
                   pl.BlockSpec((CH, D), lambda i: (i, 0)), pl.BlockSpec((CH, D), lambda i: (i, 0))],
        out_shape=[jax.ShapeDtypeStruct((nch, 8, 8 * PITCH, 128), f32), jax.ShapeDtypeStruct((nch * CH, D), f32),
                   jax.ShapeDtypeStruct((nch * CH, D), bf16)],
        scratch_shapes=[pltpu.VMEM((8, 8 * PITCH, 128), f32), pltpu.VMEM((8, 8, 128), f32)],
        args=[proj, bbq, ccq_t, ar, ai, d_skip])


def _s5_bwd(proj, dyl, s_all, bbtq, cctq, ar, ai, d_skip, nch, comm=None):
    def body(u_ref, dy_ref, s_ref, bbt_ref, cct_ref, ar_ref, ai_ref, d_ref,
             du_ref, dcc_ref, dbb_ref, dab_ref, dd_ref, ga_ref, st_ref):
        @pl.when(pl.program_id(0) == 0)
        def _():
            st_ref[...] = jnp.zeros_like(st_ref)
            for r in (dcc_ref, dbb_ref, dab_ref, dd_ref):
                r[...] = jnp.zeros_like(r)
        u = u_ref[...]
        dyl_v = dy_ref[...]
        dd_ref[...] += _rsum8(dyl_v * u)
        ub = u.astype(bf16)
        dyb = dyl_v.astype(bf16)
        for q in range(NQ):
            _to_slabs(ga_ref, q, _dot(dyb[:, 128 * q:128 * (q + 1)], cct_ref[q]))
        ar_t = [ar_ref[:, 128 * l:128 * (l + 1)] for l in range(4)]
        ai_t = [ai_ref[:, 128 * l:128 * (l + 1)] for l in range(4)]

        def one(t, carry):
            re, im, dar, dai = carry
            nre, nim, ndar, ndai = [], [], [], []
            for l in range(4):
                sre = _tile(s_ref, l, t, lead=0)
                sim = _tile(s_ref, l + 4, t, lead=0)
                ndar.append(dar[l] + re[l] * sre + im[l] * sim)
                ndai.append(dai[l] + im[l] * sre - re[l] * sim)
                a = _tile(ga_ref, l, t) + ar_t[l] * re[l] + ai_t[l] * im[l]
                b = _tile(ga_ref, l + 4, t) - ai_t[l] * re[l] + ar_t[l] * im[l]
                ga_ref[l, pl.ds(t, 8, stride=PITCH), :] = a
                ga_ref[l + 4, pl.ds(t, 8, stride=PITCH), :] = b
                nre.append(a)
                nim.append(b)
            return tuple(nre), tuple(nim), tuple(ndar), tuple(ndai)

        def step(tt, carry):
            for k in range(SCAN_UNROLL):
                carry = one(CH - 1 - (tt * SCAN_UNROLL + k), carry)
            return carry
        four = lambda ref, o: tuple(ref[l + o] for l in range(4))
        re, im, dar, dai = lax.fori_loop(0, CH // SCAN_UNROLL, step,
                                         (four(st_ref, 0), four(st_ref, 4), four(dab_ref, 0), four(dab_ref, 4)))
        for l in range(4):
            st_ref[l], st_ref[l + 4] = re[l], im[l]
            dab_ref[l], dab_ref[l + 4] = dar[l], dai[l]
        dus = []
        for q in range(NQ):
            aq = _from_slabs(lambda ls, r0: ga_ref[ls, pl.ds(r0, CH), :], q).astype(bf16)
            sq = _from_slabs(lambda ls, r0: s_ref[0, ls, pl.ds(r0, CH), :], q).astype(bf16)
            dcc_ref[q] += _dot(dyb[:, 128 * q:128 * (q + 1)], sq, TN)
            dbb_ref[q] += _dot(ub[:, 128 * q:128 * (q + 1)], aq, TN)
            dus.append(_dot(aq, bbt_ref[q]))
        du_ref[...] = jnp.concatenate(dus, axis=1) + dyl_v * d_ref[...]

    const = lambda a: pl.BlockSpec(a.shape, lambda s, nd=a.ndim: (0,) * nd)
    rev = lambda s: (nch - 1 - s, 0)
    return _call(
        body, comm, name="s5_bwd", nsteps=nch,
        in_specs=[pl.BlockSpec((CH, D), lambda s: (nch - 1 - s, O_U // D)), pl.BlockSpec((CH, D), rev),
                  pl.BlockSpec((1, 8, 8 * PITCH, 128), lambda s: (nch - 1 - s, 0, 0, 0)),
                  const(bbtq), const(cctq), const(ar), const(ai), const(d_skip)],
        out_specs=[pl.BlockSpec((CH, D), rev), pl.BlockSpec((NQ, 128, D), lambda s: (0, 0, 0)),
                   pl.BlockSpec((NQ, 128, D), lambda s: (0, 0, 0)), pl.BlockSpec((8, 8, 128), lambda s: (0, 0, 0)),
                   pl.BlockSpec((8, D), lambda s: (0, 0))],
        out_shape=[jax.ShapeDtypeStruct((nch * CH, D), f32), jax.ShapeDtypeStruct((NQ, 128, D), f32),
                   jax.ShapeDtypeStruct((NQ, 128, D), f32), jax.ShapeDtypeStruct((8, 8, 128), f32),
                   jax.ShapeDtypeStruct((8, D), f32)],
        scratch_shapes=[pltpu.VMEM((8, 8 * PITCH, 128), f32), pltpu.VMEM((8, 8, 128), f32)],
        args=[proj, dyl, s_all, bbtq, cctq, ar, ai, d_skip])


def _s5_tables(lam_re, lam_im, log_step, b_re, b_im):
    step = jnp.exp(log_step)[:, None]
    mag = jnp.exp(lam_re * step)
    ab_re = mag * jnp.cos(lam_im * step)
    ab_im = mag * jnp.sin(lam_im * step)
    den = lam_re * lam_re + lam_im * lam_im
    coef_re = ((ab_re - 1.0) * lam_re + ab_im * lam_im) / den
    coef_im = (ab_im * lam_re - (ab_re - 1.0) * lam_im) / den
    bb_re = coef_re[..., None] * b_re - coef_im[..., None] * b_im
    bb_im = coef_re[..., None] * b_im + coef_im[..., None] * b_re
    return ab_re, ab_im, bb_re, bb_im


def _blockdiag_in(m_re, m_im):
    eye = jnp.eye(8, dtype=f32)

    def one(m):
        m = m.reshape(NQ, 8, S5_P, 16)
        return jnp.einsum("qgph,gk->qghkp", m, eye).reshape(NQ, 128, 512)
    return jnp.concatenate([one(m_re), one(m_im)], axis=2)


def _blockdiag_in_grad(dm):
    def one(x):
        x = x.reshape(NQ, 8, 16, 8, S5_P)
        return jnp.einsum("qghgp->qgph", x).reshape(NQ * 8, S5_P, 16)
    return one(dm[:, :, :512]), one(dm[:, :, 512:])


def _local_step(x2, tgt2, meta, p, w_in_t, slab):
    seq = x2.shape[0]
    nch = 1 + seq // CH
    bmb = BIG_ROWS if (nch * CH) % BIG_ROWS == 0 else CH
    nbig = nch * CH // bmb
    metablk = jnp.concatenate([jnp.zeros((CH - N_META, D), f32), meta, jnp.zeros((bmb - CH, D), f32)], axis=0)
    w_full = (w_in_t, pl.BlockSpec(w_in_t.shape, lambda i: (0, 0), pipeline_mode=pl.Buffered(1)))

    def lead(i, v):
        return jnp.logical_and(i == 0, lax.broadcasted_iota(jnp.int32, (v.shape[0], 1), 0) < CH)
    h0_of = lambda i, s, q: jnp.where(lead(i, s[0]), q[0][:s[0].shape[0]], s[0])

    def in_fn(i, r, s, q, w):
        nb = _rms(h0_of(i, s, q), q[1]).astype(bf16)
        return [_dot(nb, w[0][...], NT), nb], []
    (proj, n0), _, (g_up,) = _rowwise("in_proj", in_fn, nch, shifted=[x2], pars=[metablk, p["g_mix"]], refs=[w_full],
                                      out_rows=[(W_PROJ, f32), (D, bf16)], comm=_gather_piece(slab, R_UP, 1024))
    (y, y_ssd, states, xc_all), (g_down,) = _ssd_fwd(proj, p["conv_w"], p["conv_b"], p["dt_bias"], p["a_log"], p["d_ssd"],
                                                     p["g_ssd"], nch, comm=_gather_piece(slab, R_DOWN, 1024))

    ab_re, ab_im, bb_re, bb_im = _s5_tables(p["lam_re"], p["lam_im"], p["log_step"], p["b_re"], p["b_im"])
    ar, ai = ab_re.reshape(NQ, 512), ab_im.reshape(NQ, 512)
    bbq = _blockdiag_in(bb_re, bb_im)
    ccq = _blockdiag_in(jnp.swapaxes(p["c_re"], 1, 2), -jnp.swapaxes(p["c_im"], 1, 2))
    d_skip = p["d_s5"].reshape(1, D)
    (s_all, ylin, y5), (g_go,) = _s5_fwd(proj, bbq.astype(bf16), jnp.swapaxes(ccq, 1, 2).astype(bf16), ar, ai, d_skip, nch,
                                         comm=_gather_piece(slab, R_GLU, 1024))
    whole = lambda a: (a, pl.BlockSpec(a.shape, lambda i: (0, 0, 0), pipeline_mode=pl.Buffered(1)))
    w_up, w_down = whole(g_up), whole(g_down)
    w_glu_t = (g_go, pl.BlockSpec((4, 512, D), lambda i: (0, 0, 0), pipeline_mode=pl.Buffered(1)))
    w_out = (g_go, pl.BlockSpec((4, 512, D), lambda i: (0, 1, 0), pipeline_mode=pl.Buffered(1)))

    def glu_fn(i, r, s, q, w):
        v = jnp.concatenate([_dot(r[0], w[0][k], NT) for k in range(4)], axis=1) + q[0]
        return [v, _rms(v[:, :D] * _sigmoid(v[:, D:]), q[1])], []
    (v, y_s5), _ = _rowwise("glu", glu_fn, nbig, rows=[y5], pars=[p["b_glu"], p["g_s5"]], refs=[w_glu_t],
                            out_rows=[(2 * D, f32), (D, bf16)], bm=bmb)

    def out_fn(i, r, s, q, w):
        acc = (_dot(r[0][:, :512], w[0][0]) + _dot(r[0][:, 512:], w[0][1]) + _dot(r[1][:, :512], w[0][2])
               + _dot(r[1][:, 512:], w[0][3]))
        return [h0_of(i, s, q) + acc], []
    (h1,), _ = _rowwise("out_proj", out_fn, nbig, rows=[y_ssd, y_s5], shifted=[x2], pars=[metablk], refs=[w_out],
                        out_rows=[(D, f32)], bm=bmb)

    def up_fn(i, r, s, q, w):
        nb = _rms(r[0], q[0]).astype(bf16)
        return [jnp.concatenate([jnp.maximum(_dot(nb, w[0][k]), 0.0).astype(bf16) for k in range(4)], axis=1), nb], []
    (relu_m, n1), _ = _rowwise("up_proj", up_fn, nbig, rows=[h1], pars=[p["g_mlp"]], refs=[w_up],
                               out_rows=[(4 * D, bf16), (D, bf16)], bm=bmb)

    def down_fn(i, r, s, q, w):
        acc = None
        for k in range(4):
            t = r[0][:, D * k:D * (k + 1)]
            part = _dot(t * t, w[0][k])
            acc = part if acc is None else acc + part
        return [r[1] + acc], []
    (h2,), _ = _rowwise("down_proj", down_fn, nbig, rows=[relu_m, h1], refs=[w_down], out_rows=[(D, f32)], bm=bmb)

    def final_fn(i, r, s, q, w):
        err = jnp.where(lead(i, r[0]), 0.0, _rms(r[0], q[0]) - s[0])
        dh, dg8 = _rms_bwd(err * (1.0 / D), r[0], q[0])
        return [dh, dh], [_rsum8(err * err), dg8]
    (dh2, dh2_b), (loss8, dgf8) = _rowwise("final", final_fn, nbig, rows=[h2], shifted=[tgt2], pars=[p["g_final"]],
                                           out_rows=[(D, f32), (D, bf16)], out_accs=[D, D], bm=bmb)
    loss = 0.5 / D * jnp.sum(loss8)

    def down_bwd_fn(i, r, s, q, w):
        dm_ = [_dot(r[0], w[0][k], NT) * (2.0 * r[1][:, D * k:D * (k + 1)].astype(f32)) for k in range(4)]
        return [jnp.concatenate(dm_, axis=1)], []
    (dm,), _ = _rowwise("down_bwd", down_bwd_fn, nbig, rows=[dh2_b, relu_m], refs=[w_down], out_rows=[(4 * D, bf16)], bm=bmb)
    g_a = _dw_into("dw_down", relu_m, dh2_b, None, 1024, True, 1, 4, 0, piece_rows=2048, a_square=True)

    def up_bwd_fn(i, r, s, q, w):
        acc = _dot(r[0][:, :D], w[0][0], NT)
        for k in range(1, 4):
            acc = acc + _dot(r[0][:, D * k:D * (k + 1)], w[0][k], NT)
        dh, dg8 = _rms_bwd(acc, r[1], q[0])
        dh1_ = r[2] + dh
        return [dh1_, dh1_], [dg8]
    (dh1, dh1_b), (dgmlp8,) = _rowwise("up_bwd", up_bwd_fn, nbig, rows=[dm, h1, dh2], pars=[p["g_mlp"]], refs=[w_up],
                                       out_rows=[(D, f32), (D, bf16)], out_accs=[D], bm=bmb)
    g_a = _dw_into("dw_up", n1, dm, g_a, 1024, False, 0, 4, 0, piece_rows=2048)

    def out_bwd_fn(i, r, s, q, w):
        dmix = [_dot(r[0], w[0][k], NT) for k in range(4)]
        v1, v2 = r[1][:, :D], r[1][:, D:]
        s2 = _sigmoid(v2)
        dglu, dg8 = _rms_bwd(jnp.concatenate(dmix[2:], axis=1), v1 * s2, q[0])
        dv = jnp.concatenate([dglu * s2, dglu * v1 * s2 * (1.0 - s2)], axis=1)
        return [jnp.concatenate(dmix[:2], axis=1), dv], [dg8, _rsum8(dv)]
    (dys, dv), (dgs58, dbglu8) = _rowwise("out_bwd", out_bwd_fn, nbig, rows=[dh1_b, v], pars=[p["g_s5"]], refs=[w_out],
                                          out_rows=[(D, f32), (2 * D, bf16)], out_accs=[D, 2 * D], bm=bmb)
    g_b = _dw_into("dw_out_a", y_ssd, dh1_b, None, 512, True, 1, 2, 0, piece_rows=1024)
    g_b = _dw_into("dw_out_b", y_s5, dh1_b, g_b, 512, True, 1, 2, 2, piece_rows=1024)

    def glu_bwd_fn(i, r, s, q, w):
        acc = _dot(r[0][:, :512], w[0][0])
        for k in range(1, 4):
            acc = acc + _dot(r[0][:, 512 * k:512 * (k + 1)], w[0][k])
        yl = r[1]
        cdf = 0.5 * (1.0 + lax.erf(yl * (1.0 / math.sqrt(2.0))))
        pdf = jnp.exp(-0.5 * yl * yl) * (1.0 / math.sqrt(2.0 * math.pi))
        return [acc * (cdf + yl * pdf)], []
    (dylin,), _ = _rowwise("glu_bwd", glu_bwd_fn, nbig, rows=[dv, ylin], refs=[w_glu_t], out_rows=[(D, f32)], bm=bmb)
    g_b = _dw_into("dw_glu", dv, y5, g_b, 512, True, 0, 4, 0, piece_rows=1024)

    (du, dcc, dbb, dab, dds5), (land_a,) = _s5_bwd(proj, dylin, s_all, jnp.swapaxes(bbq, 1, 2).astype(bf16), ccq.astype(bf16),
                                                   ar, ai, d_skip, nch, comm=_scatter_piece(g_a))

    s8 = lambda a: jnp.sum(a, axis=0, keepdims=True)
    dab_q = jnp.swapaxes(dab.reshape(2, 4, NQ, 128), 1, 2).reshape(2, NQ * 8, S5_P)
    dbb_re, dbb_im = _blockdiag_in_grad(dbb)
    dcr, dci = _blockdiag_in_grad(dcc)
    _, vjp = jax.vjp(_s5_tables, p["lam_re"], p["lam_im"], p["log_step"], p["b_re"], p["b_im"])
    dlam_re, dlam_im, dlog_step, db_re, db_im = vjp((dab_q[0], dab_q[1], dbb_re, dbb_im))
    early = dict(lam_re=dlam_re, lam_im=dlam_im, log_step=dlog_step, b_re=db_re, b_im=db_im, c_re=jnp.swapaxes(dcr, 1, 2),
                 c_im=-jnp.swapaxes(dci, 1, 2), d_s5=s8(dds5).reshape(NQ * 8, 16), b_glu=s8(dbglu8), g_s5=s8(dgs58),
                 g_mlp=s8(dgmlp8), g_final=s8(dgf8).reshape(D))
    early_pack = _pack_small([early[n] for n in EARLY], _rows_for(EARLY))

    (dproj, dcw8, dcb8, ddtb8, dal8, dd8, dgssd8), (land_b, all_early) = _ssd_bwd(
        proj, xc_all, y, dys, du, states, p["conv_w"], p["conv_b"], p["dt_bias"], p["a_log"], p["d_ssd"], p["g_ssd"], nch,
        comm=_both(_scatter_piece(g_b), _gather_blocks(early_pack)))

    gt = _dw_in_t(dproj, n0)
    gt = jnp.concatenate([gt[0:1024], gt[O_XBC:O_XBC + D_XBC], gt[O_DT:O_DT + HEADS], gt[O_U:O_U + D]], axis=0).reshape(4, 900, D)
    g_c = jnp.concatenate([gt, jnp.zeros((4, 1024 - 900, D), bf16)], axis=1)

    def in_bwd_fn(i, r, s, q, w):
        dh, dg8 = _rms_bwd(_dot(r[0], w[0][...]), h0_of(i, s, q), q[1])
        return [r[1] + dh], [dg8]
    (dh0,), (dgmix8,), (land_c,) = _rowwise(
        "in_bwd", in_bwd_fn, nbig, rows=[dproj, dh1], shifted=[x2], pars=[metablk, p["g_mix"]], refs=[w_full],
        out_rows=[(D, f32)], out_accs=[D], bm=bmb, comm=_scatter_piece(g_c))

    hsum = lambda a: jnp.sum(s8(a).reshape(HEADS, HEAD_DIM), axis=1).reshape(1, HEADS)
    late = dict(g_mix=s8(dgmix8), conv_b=s8(dcb8), dt_bias=s8(ddtb8), a_log=s8(dal8), d_ssd=hsum(dd8), g_ssd=s8(dgssd8),
                conv_w=dcw8[0:4], meta_tokens=dh0[CH - N_META:CH], loss=loss.reshape(1))
    return dh0, [(g_a, land_a), (g_b, land_b), (g_c, land_c)], all_early, late


def _perm_rows_w_in(wt):
    return jnp.concatenate([wt[0:1024], wt[2576:3600], wt[1024:2560], wt[2560:2576],
                            jnp.zeros((W_PROJ - 3600, wt.shape[1]), wt.dtype)], axis=0)


def _unperm_cols_w_in(g):
    return jnp.concatenate([g[:, 0:1024], g[:, O_XBC:O_XBC + D_XBC], g[:, O_DT:O_DT + HEADS], g[:, O_U:O_U + D]], axis=1)


def _pack_shard(w_in, w_glu, w_out, w_up, w_down, spare):
    dt = w_in.dtype
    parts = [w_up, w_down, w_glu.T, w_out, w_in.T, spare,
             jnp.zeros((PACK_ROWS - R_SPARE - spare.shape[0], D), dt)]
    return jnp.concatenate(parts, axis=0)


def _allgather8(x_shard, name):
    m_per, n = x_shard.shape

    def body(x_ref, out_ref, send_sems, recv_sems, stage, local_sems):
        x, y, c = _place()
        me, sibling = (x, y, c), (x, y, 1 - c)
        chips = [(1 - x, y), (x, 1 - y), (1 - x, 1 - y)]

        def rows(px, py, pc):
            return out_ref.at[pl.ds((4 * px + 2 * py + pc) * m_per, m_per), :]

        def copy(k, block, to, src=None):
            return pltpu.make_async_remote_copy(
                src_ref=rows(*block) if src is None else src, dst_ref=rows(*block),
                send_sem=send_sems.at[k], recv_sem=recv_sems.at[k], device_id=to, device_id_type=MESH)

        load = pltpu.make_async_copy(x_ref, stage, local_sems.at[0])
        load.start()
        first = [copy(0, me, sibling, src=x_ref)]
        first += [copy(1 + j, me, (*chip, c), src=x_ref) for j, chip in enumerate(chips)]
        for cp in first:
            cp.start()
        load.wait()
        store = pltpu.make_async_copy(stage, rows(*me), local_sems.at[1])
        store.start()
        passed = [copy(4 + j, (*chip, c), sibling) for j, chip in enumerate(chips)]
        for j, chip in enumerate(chips):
            copy(1 + j, (*chip, c), me).wait_recv()
            passed[j].start()
        copy(0, sibling, me).wait_recv()
        for j, chip in enumerate(chips):
            copy(4 + j, (*chip, 1 - c), me).wait_recv()
        for cp in first + passed:
            cp.wait_send()
        store.wait()

    return pl.pallas_call(
        body, name=name, out_shape=jax.ShapeDtypeStruct((8 * m_per, n), x_shard.dtype),
        in_specs=[_ANY], out_specs=_ANY,
        scratch_shapes=[pltpu.SemaphoreType.DMA((7,)), pltpu.SemaphoreType.DMA((7,)), pltpu.VMEM((m_per, n), x_shard.dtype),
                        pltpu.SemaphoreType.DMA((2,))])(x_shard)


def _swap_sibling(r, name):
    def body(r_ref, out_ref, send_sem, recv_sem):
        x, y, c = _place()
        cp = pltpu.make_async_remote_copy(src_ref=r_ref, dst_ref=out_ref, send_sem=send_sem, recv_sem=recv_sem,
                                          device_id=(x, y, 1 - c), device_id_type=MESH)
        cp.start()
        cp.wait()

    return pl.pallas_call(
        body, name=name, out_shape=jax.ShapeDtypeStruct(r.shape, r.dtype), in_specs=[_ANY], out_specs=_ANY,
        scratch_shapes=[pltpu.SemaphoreType.DMA, pltpu.SemaphoreType.DMA])(r)


SH_CONVW, SH_META = 4 * 384, 16 * 256
SPARE_ROWS = 17

SMALL_SHAPES = dict(
    g_mix=(1, 1024), conv_b=(1, 1536), dt_bias=(1, 16), a_log=(1, 16), d_ssd=(1, 16), g_ssd=(1, 1024), lam_re=(1, 64, 64),
    lam_im=(1, 64, 64), log_step=(1, 64), b_re=(1, 64, 64, 16), b_im=(1, 64, 64, 16), c_re=(1, 64, 16, 64), c_im=(1, 64, 16, 64),
    d_s5=(1, 64, 16), b_glu=(1, 2048), g_s5=(1, 1024), g_mlp=(1, 1024), g_final=(1024,),
    conv_w=(4, D_XBC), meta_tokens=(N_META, D), loss=(1,))
EARLY = ["lam_re", "lam_im", "log_step", "b_re", "b_im", "c_re", "c_im", "d_s5", "b_glu", "g_s5", "g_mlp", "g_final"]
LATE = ["g_mix", "conv_b", "dt_bias", "a_log", "d_ssd", "g_ssd", "conv_w", "meta_tokens", "loss"]


def _rows_for(names):
    return -(-sum(math.prod(SMALL_SHAPES[n]) for n in names) // (8 * D)) * 8


def _pack_small(arrs, rows):
    flat = jnp.concatenate([a.reshape(-1).astype(f32) for a in arrs])
    return jnp.concatenate([flat, jnp.zeros((rows * D - flat.shape[0],), f32)]).reshape(rows, D)


def _unpack_small(slab, shapes):
    flat = slab.reshape(-1)
    out, o = [], 0
    for shp in shapes:
        n = math.prod(shp)
        out.append(flat[o:o + n].reshape(shp))
        o += n
    return out


def _sum8(g, rows, name):
    def body(g_ref, o_ref):
        acc = g_ref[0]
        for k in range(1, 8):
            acc = acc + g_ref[k]
        o_ref[...] = acc
    return pl.pallas_call(body, name=name, out_shape=jax.ShapeDtypeStruct((rows, D), f32),
                          compiler_params=_cp())(g.reshape(8, rows, D))


def _adam_math(w_, g_, m_, v_):
    m2 = ADAM_B1 * m_ + (1.0 - ADAM_B1) * g_
    v2 = ADAM_B2 * v_ + (1.0 - ADAM_B2) * jnp.square(g_)
    m_hat = m2 / (1.0 - ADAM_B1 ** ADAM_STEP)
    v_hat = v2 / (1.0 - ADAM_B2 ** ADAM_STEP)
    delta = -ADAM_LR * (m_hat / (jnp.sqrt(v_hat) + ADAM_EPS) + ADAM_WD * w_)
    return delta, m2, v2


def _adamw(name, w, g, m, v, bm):
    def fn(i, r, s, q, refs):
        return list(_adam_math(*r)), []
    c = w.shape[1]
    (d, m2, v2), _ = _rowwise(name, fn, w.shape[0] // bm, rows=[w, g, m, v], out_rows=[(c, f32)] * 3, bm=bm)
    return d, m2, v2


def _adamw_whole(name, w, g, m, v):
    def body(w_ref, g_ref, m_ref, v_ref, d_ref, m2_ref, v2_ref):
        d_ref[...], m2_ref[...], v2_ref[...] = _adam_math(w_ref[...], g_ref[...], m_ref[...], v_ref[...])
    return pl.pallas_call(body, name=name, out_shape=[jax.ShapeDtypeStruct(w.shape, f32)] * 3, compiler_params=_cp())(w, g, m, v)


def _sum_parts(name, own, land):
    def fn(i, r, s, q, refs):
        acc = r[0].astype(f32)
        for k in range(7):
            acc = acc + refs[0][k].astype(f32)
        return [acc], []
    (o,), _ = _rowwise(name, fn, own.shape[0] // CH, rows=[own],
                       refs=[(land, pl.BlockSpec((7, CH, D), lambda i: (0, i, 0)))], out_rows=[(D, f32)])
    return o


def kernel(x, meta_tokens, g_mix, w_in, conv_w, conv_b, dt_bias, a_log, d_ssd, g_ssd, lam_re, lam_im, log_step, b_re, b_im, c_re, c_im, d_s5, w_glu, b_glu, g_s5, w_out, g_mlp, w_up, w_down, g_final, loss_target, m_meta_tokens, m_g_mix, m_w_in, m_conv_w, m_conv_b, m_dt_bias, m_a_log, m_d_ssd, m_g_ssd, m_lam_re, m_lam_im, m_log_step, m_b_re, m_b_im, m_c_re, m_c_im, m_d_s5, m_w_glu, m_b_glu, m_g_s5, m_w_out, m_g_mlp, m_w_up, m_w_down, m_g_final, v_meta_tokens, v_g_mix, v_w_in, v_conv_w, v_conv_b, v_dt_bias, v_a_log, v_d_ssd, v_g_ssd, v_lam_re, v_lam_im, v_log_step, v_b_re, v_b_im, v_c_re, v_c_im, v_d_s5, v_w_glu, v_b_glu, v_g_s5, v_w_out, v_g_mlp, v_w_up, v_w_down, v_g_final):
    given = dict(locals())
    cx, cy, cc = _place()
    chip = 2 * cx + cy

    small_f = jnp.concatenate([conv_w.reshape(-1), meta_tokens.reshape(-1)])
    t_hi = small_f.astype(bf16)
    r_1 = small_f - t_hi.astype(f32)
    t_mid = r_1.astype(bf16)
    t_lo = (r_1 - t_mid.astype(f32)).astype(bf16)
    terms = jnp.concatenate([t_hi, t_mid, t_lo])
    spare = jnp.concatenate([terms, jnp.zeros((SPARE_ROWS * D - terms.shape[0],), bf16)]).reshape(SPARE_ROWS, D)
    slab = _pack_shard(w_in[0].astype(bf16), w_glu[0].astype(bf16), w_out[0].astype(bf16), w_up[0].astype(bf16),
                       w_down[0].astype(bf16), spare)
    my_half = lax.dynamic_slice_in_dim(slab, R_IN + cc * 512, 512, axis=0)
    gathered = _allgather8(my_half, "gather_w_in").reshape(4, 1024, D)
    w_in_t = _perm_rows_w_in(jnp.concatenate([gathered[s, 0:900] for s in range(4)], axis=0))
    n_sf = SH_CONVW + SH_META
    tr = gathered[:, 900:900 + SPARE_ROWS].reshape(4, SPARE_ROWS * D)[:, :3 * n_sf].astype(f32).reshape(4, 3, n_sf)
    sp = tr[:, 0] + tr[:, 1] + tr[:, 2]
    conv_w_full = jnp.concatenate([sp[s, :SH_CONVW].reshape(4, 384) for s in range(4)], axis=1)
    meta_full = jnp.concatenate([sp[s, SH_CONVW:].reshape(16, 256) for s in range(4)], axis=1)

    p = dict(g_mix=g_mix, conv_w=conv_w_full, conv_b=conv_b, dt_bias=dt_bias, a_log=a_log, d_ssd=d_ssd, g_ssd=g_ssd,
             lam_re=lam_re[0], lam_im=lam_im[0], log_step=log_step[0], b_re=b_re[0], b_im=b_im[0], c_re=c_re[0], c_im=c_im[0],
             d_s5=d_s5[0], b_glu=b_glu, g_s5=g_s5, g_mlp=g_mlp, g_final=g_final.reshape(1, D))
    dh0, pieces, all_early, late = _local_step(x[0], loss_target[0], meta_full, p, w_in_t, slab)
    grad_x = dh0[CH:].reshape(x.shape)

    reds = []
    for k, (gp, land) in enumerate(pieces):
        half = gp.shape[1] // 2
        own = lax.dynamic_slice(gp, (chip, cc * half, 0), (1, half, D)).reshape(half, D)
        reds.append(_sum_parts("rs_sum_%d" % k, own, land))
    red = jnp.concatenate(reds, axis=0)
    other = _swap_sibling(red, "rs_share")
    first = jnp.where(cc == 0, red, other)
    second = jnp.where(cc == 0, other, red)
    g_up, g_down = first[0:1024], second[0:1024]
    g_glu, g_out = first[1024:1536].T, second[1024:1536]
    g_in_t = jnp.concatenate([first[1536:2048], second[1536:1536 + 900 - 512]], axis=0)

    gs = dict(zip(EARLY, _unpack_small(_sum8(all_early, _rows_for(EARLY), "sum8_early"), [SMALL_SHAPES[n] for n in EARLY])))
    all_late = _allgather8(_pack_small([late[n] for n in LATE], _rows_for(LATE)), "gather_small")
    gs.update(zip(LATE, _unpack_small(_sum8(all_late, _rows_for(LATE), "sum8_late"), [SMALL_SHAPES[n] for n in LATE])))
    g_conv_w = lax.dynamic_slice_in_dim(gs.pop("conv_w"), chip * 384, 384, axis=1).reshape(conv_w.shape)
    g_meta = lax.dynamic_slice_in_dim(gs.pop("meta_tokens"), chip * 256, 256, axis=1)
    loss = gs.pop("loss").reshape(())

    grads = dict(gs, meta_tokens=g_meta, conv_w=g_conv_w, w_in=g_in_t.T.reshape(w_in.shape), w_glu=g_glu.reshape(w_glu.shape),
                 w_out=g_out.reshape(w_out.shape), w_up=g_up.reshape(w_up.shape), w_down=g_down.reshape(w_down.shape))
    delta, new_m, new_v = {}, {}, {}
    d_, m_, v_ = _adamw_whole("adamw_w_in", w_in[0].T, g_in_t, m_w_in[0].T, v_w_in[0].T)
    delta["w_in"], new_m["w_in"], new_v["w_in"] = (a.T.reshape(w_in.shape) for a in (d_, m_, v_))
    for n in ("w_glu", "w_out", "w_up", "w_down"):
        shp = given[n].shape
        two = lambda a: a.reshape(shp[1], shp[2])
        d_, m_, v_ = _adamw("adamw_" + n, two(given[n]), two(grads[n]), two(given["m_" + n]), two(given["v_" + n]), 256)
        delta[n], new_m[n], new_v[n] = d_.reshape(shp), m_.reshape(shp), v_.reshape(shp)
    for n in EARLY + LATE[:-1]:
        shp = given[n].shape
        two = (lambda a: a.reshape(1, -1)) if len(shp) == 1 else (lambda a: a)
        d_, m_, v_ = _adamw_whole("adamw_" + n, two(given[n]), two(grads[n].reshape(shp)), two(given["m_" + n]), two(given["v_" + n]))
        delta[n], new_m[n], new_v[n] = d_.reshape(shp), m_.reshape(shp), v_.reshape(shp)

    order = ["meta_tokens", "g_mix", "w_in", "conv_w", "conv_b", "dt_bias", "a_log", "d_ssd", "g_ssd", "lam_re", "lam_im", "log_step",
             "b_re", "b_im", "c_re", "c_im", "d_s5", "w_glu", "b_glu", "g_s5", "w_out", "g_mlp", "w_up", "w_down", "g_final"]
    grads_out = [grads[n].reshape(given[n].shape) for n in order]
    return (loss, grad_x, *grads_out, *[delta[n] for n in order], *[new_m[n] for n in order], *[new_v[n] for n in order])
```

```python
import math

import jax
import jax.numpy as jnp
from jax import lax
from jax.experimental import pallas as pl
from jax.experimental.pallas import tpu as pltpu

f32 = jnp.float32
bf16 = jnp.bfloat16

D = 1024
N_META = 16
CH = 256
HEADS = 16
HEAD_DIM = 64
NSTATE = 128
D_XBC = 1536
S5_P = 64
NQ = 8
PITCH = CH + 4
EPS = 1e-5
O_Z, O_U, O_XBC, O_DT, W_PROJ = 0, 1024, 2048, 3584, 3712
VMEM_LIMIT = 60 * 1024 * 1024

ADAM_LR, ADAM_B1, ADAM_B2, ADAM_EPS, ADAM_WD, ADAM_STEP = 0.001, 0.9, 0.999, 1e-08, 0.01, 10

NT = (((1,), (1,)), ((), ()))
TN = (((0,), (0,)), ((), ()))
_ANY = pl.BlockSpec(memory_space=pl.ANY)


def _cp(sem=None):
    return pltpu.CompilerParams(dimension_semantics=sem, vmem_limit_bytes=VMEM_LIMIT)


def _sigmoid(v):
    return 1.0 / (1.0 + jnp.exp(-v))


def _rsum8(v):
    r, c = v.shape
    return jnp.sum(v.reshape(r // 8, 8, c), axis=0)


def _rms(h, g):
    r = lax.rsqrt(jnp.mean(h * h, axis=-1, keepdims=True) + EPS)
    return h * r * g


def _rms_bwd(dy, h, g):
    r = lax.rsqrt(jnp.mean(h * h, axis=-1, keepdims=True) + EPS)
    n = h * r
    dn = dy * g
    dh = r * (dn - n * jnp.mean(dn * n, axis=-1, keepdims=True))
    return dh, _rsum8(dy * n)


def _dot(a, b, dims=None):
    if dims is None:
        return jnp.dot(a, b, preferred_element_type=f32)
    return lax.dot_general(a, b, dims, preferred_element_type=f32)


def _split_dot(v, m01, dims, terms, v_is_lhs=True):
    out, r = None, v
    for _ in range(terms):
        piece = r.astype(bf16)
        o = _dot(piece, m01, dims) if v_is_lhs else _dot(m01, piece, dims)
        out = o if out is None else out + o
        r = r - piece.astype(f32)
    return out


MESH = pl.DeviceIdType.MESH


def _place():
    return lax.axis_index("x"), lax.axis_index("y"), lax.axis_index("c")


def _flip(v, f):
    return 1 - v if f else v


def _call(body, comm, *, name, nsteps, in_specs, out_specs, out_shape, scratch_shapes, args):
    n_in, n_out, n_scr = len(in_specs), len(out_specs), len(scratch_shapes)
    if comm is None:
        res = pl.pallas_call(body, name=name, grid=(nsteps,), in_specs=in_specs, out_specs=out_specs, out_shape=out_shape,
                             scratch_shapes=scratch_shapes, compiler_params=_cp(("arbitrary",)))(*args)
        return list(res), []
    c_in, c_out = len(comm["ins"]), len(comm["outs"])

    def wrapped(*refs):
        o0 = n_in + c_in
        s0 = o0 + n_out + c_out
        cparts = (refs[n_in:o0], refs[o0 + n_out:s0], refs[s0 + n_scr:])

        @pl.when(pl.program_id(0) == 0)
        def _():
            comm["start"](*cparts)
        body(*refs[:n_in], *refs[o0:o0 + n_out], *refs[s0:s0 + n_scr])

        @pl.when(pl.program_id(0) == nsteps - 1)
        def _():
            comm["finish"](*cparts)

    any_spec = pl.BlockSpec(memory_space=pl.ANY)
    res = pl.pallas_call(
        wrapped, name=name, grid=(nsteps,), in_specs=list(in_specs) + [any_spec] * c_in,
        out_specs=list(out_specs) + [any_spec] * c_out, out_shape=list(out_shape) + list(comm["outs"]),
        scratch_shapes=list(scratch_shapes) + list(comm["scratch"]),
        compiler_params=_cp(("arbitrary",)))(*args, *comm["ins"])
    return list(res[:n_out]), list(res[n_out:])


def _gather_piece(slab, r0, rows):
    half = rows // 2

    def copies(slab_ref, out_ref, send_sems, recv_sems):
        x, y, c = _place()
        cps = []
        for fx, fy in ((1, 0), (0, 1), (1, 1)):
            for fc in (0, 1):
                k = (2 * fx + fy - 1) * 2 + fc
                cps.append(pltpu.make_async_remote_copy(
                    src_ref=slab_ref.at[pl.ds(r0 + c * half, half), :], dst_ref=out_ref.at[2 * x + y, pl.ds(c * half, half), :],
                    send_sem=send_sems.at[k], recv_sem=recv_sems.at[k],
                    device_id=(_flip(x, fx), _flip(y, fy), _flip(c, fc)), device_id_type=MESH))
        return cps

    def start(ins, outs, scr):
        send_sems, recv_sems, stage, local_sems = scr
        x, y, _ = _place()
        load = pltpu.make_async_copy(ins[0].at[pl.ds(r0, rows), :], stage, local_sems.at[0])
        load.start()
        for cp in copies(ins[0], outs[0], send_sems, recv_sems):
            cp.start()
        load.wait()
        pltpu.make_async_copy(stage, outs[0].at[2 * x + y], local_sems.at[1]).start()

    def finish(ins, outs, scr):
        send_sems, recv_sems, stage, local_sems = scr
        x, y, _ = _place()
        for cp in copies(ins[0], outs[0], send_sems, recv_sems):
            cp.wait()
        pltpu.make_async_copy(stage, outs[0].at[2 * x + y], local_sems.at[1]).wait()

    return dict(ins=[slab], outs=[jax.ShapeDtypeStruct((4, rows, D), bf16)],
                scratch=[pltpu.SemaphoreType.DMA((6,)), pltpu.SemaphoreType.DMA((6,)), pltpu.VMEM((rows, D), bf16),
                         pltpu.SemaphoreType.DMA((2,))], start=start, finish=finish)


def _scatter_piece(gpiece):
    half = gpiece.shape[1] // 2

    def copies(g_ref, land_ref, send_sems, recv_sems):
        x, y, c = _place()
        cps = []
        for fx in (0, 1):
            for fy in (0, 1):
                for fc in (0, 1):
                    k = 4 * fx + 2 * fy + fc - 1
                    if k < 0:
                        continue
                    px, py, pc = _flip(x, fx), _flip(y, fy), _flip(c, fc)
                    cps.append(pltpu.make_async_remote_copy(
                        src_ref=g_ref.at[2 * px + py, pl.ds(pc * half, half), :], dst_ref=land_ref.at[k],
                        send_sem=send_sems.at[k], recv_sem=recv_sems.at[k], device_id=(px, py, pc), device_id_type=MESH))
        return cps

    def start(ins, outs, scr):
        for cp in copies(ins[0], outs[0], *scr):
            cp.start()

    def finish(ins, outs, scr):
        for cp in copies(ins[0], outs[0], *scr):
            cp.wait()

    return dict(ins=[gpiece], outs=[jax.ShapeDtypeStruct((7, half, D), gpiece.dtype)],
                scratch=[pltpu.SemaphoreType.DMA((7,)), pltpu.SemaphoreType.DMA((7,))], start=start, finish=finish)


def _gather_blocks(block):
    rows = block.shape[0]

    def mine(out_ref):
        x, y, c = _place()
        return out_ref.at[pl.ds((4 * x + 2 * y + c) * rows, rows), :]

    def copies(b_ref, out_ref, send_sems, recv_sems):
        x, y, c = _place()
        cps = []
        for fx in (0, 1):
            for fy in (0, 1):
                for fc in (0, 1):
                    k = 4 * fx + 2 * fy + fc - 1
                    if k < 0:
                        continue
                    cps.append(pltpu.make_async_remote_copy(
                        src_ref=b_ref, dst_ref=mine(out_ref), send_sem=send_sems.at[k], recv_sem=recv_sems.at[k],
                        device_id=(_flip(x, fx), _flip(y, fy), _flip(c, fc)), device_id_type=MESH))
        return cps

    def start(ins, outs, scr):
        send_sems, recv_sems, stage, local_sems = scr
        load = pltpu.make_async_copy(ins[0], stage, local_sems.at[0])
        load.start()
        for cp in copies(ins[0], outs[0], send_sems, recv_sems):
            cp.start()
        load.wait()
        pltpu.make_async_copy(stage, mine(outs[0]), local_sems.at[1]).start()

    def finish(ins, outs, scr):
        send_sems, recv_sems, stage, local_sems = scr
        for cp in copies(ins[0], outs[0], send_sems, recv_sems):
            cp.wait()
        pltpu.make_async_copy(stage, mine(outs[0]), local_sems.at[1]).wait()

    return dict(ins=[block], outs=[jax.ShapeDtypeStruct((8 * rows, D), block.dtype)],
                scratch=[pltpu.SemaphoreType.DMA((7,)), pltpu.SemaphoreType.DMA((7,)), pltpu.VMEM((rows, D), block.dtype),
                         pltpu.SemaphoreType.DMA((2,))], start=start, finish=finish)


def _both(c1, c2):
    n = (len(c1["ins"]), len(c1["outs"]), len(c1["scratch"]))

    def split(parts):
        return [p[:k] for p, k in zip(parts, n)], [p[k:] for p, k in zip(parts, n)]

    def start(*parts):
        a, b = split(parts)
        c1["start"](*a)
        c2["start"](*b)

    def finish(*parts):
        a, b = split(parts)
        c1["finish"](*a)
        c2["finish"](*b)

    return dict(ins=c1["ins"] + c2["ins"], outs=c1["outs"] + c2["outs"], scratch=c1["scratch"] + c2["scratch"],
                start=start, finish=finish)


def _rowwise(name, fn, nblk, rows=(), shifted=(), pars=(), refs=(), out_rows=(), out_accs=(), bm=CH, comm=None):
    n_sub = bm // CH
    n_r, n_s, n_p, n_w = len(rows), len(shifted) * n_sub, len(pars), len(refs)
    n_in = n_r + n_s + n_p + n_w
    n_o, n_a = len(out_rows), len(out_accs)

    def body(*all_refs):
        i = pl.program_id(0)
        ins = all_refs[:n_in]
        outs = all_refs[n_in:]
        rv = [r[...] for r in ins[:n_r]]
        sub = ins[n_r:n_r + n_s]
        sv = [jnp.concatenate([r[...] for r in sub[k * n_sub:(k + 1) * n_sub]], axis=0) if n_sub > 1 else sub[k][...]
              for k in range(len(shifted))]
        pv = [r[...] for r in ins[n_r + n_s:n_r + n_s + n_p]]
        ro, ao = fn(i, rv, sv, pv, list(ins[n_r + n_s + n_p:]))
        for r, v in zip(outs[:n_o], ro):
            r[...] = v.astype(r.dtype)
        accs = outs[n_o:]

        @pl.when(i == 0)
        def _():
            for r in accs:
                r[...] = jnp.zeros_like(r)
        for r, v in zip(accs, ao):
            r[...] += v

    in_specs = [pl.BlockSpec((bm, a.shape[1]), lambda i: (i, 0)) for a in rows]
    in_specs += [pl.BlockSpec((CH, a.shape[1]), lambda i, j=j: (jnp.maximum(n_sub * i - 1 + j, 0), 0))
                 for a in shifted for j in range(n_sub)]
    in_specs += [pl.BlockSpec(a.shape, lambda i, nd=a.ndim: (0,) * nd) for a in pars]
    in_specs += [spec for _, spec in refs]
    out_specs = [pl.BlockSpec((bm, c), lambda i: (i, 0)) for c, _ in out_rows]
    out_specs += [pl.BlockSpec((8, c), lambda i: (0, 0)) for c in out_accs]
    out_shape = [jax.ShapeDtypeStruct((nblk * bm, c), dt) for c, dt in out_rows]
    out_shape += [jax.ShapeDtypeStruct((8, c), f32) for c in out_accs]
    res, cres = _call(body, comm, name=name, nsteps=nblk, in_specs=in_specs, out_specs=out_specs, out_shape=out_shape,
                      scratch_shapes=[], args=[*rows, *[a for a in shifted for _ in range(n_sub)], *pars, *[a for a, _ in refs]])
    parts = (res[:n_o], res[n_o:])
    return parts if comm is None else parts + (cres,)


PACK_ROWS = 4096
HALF_ROWS = PACK_ROWS // 2
R_UP, R_DOWN, R_GLU, R_OUT, R_IN, R_SPARE = 0, 1024, 2048, 2560, 3072, 3972


BIG_ROWS = 768
DW_ROWS = 2816


def _contract_rows(lp, big=DW_ROWS):
    for rows in (big, BIG_ROWS):
        if lp % rows == 0:
            return rows
    return CH


def _dw_into(name, a, b, slab, ka, a_sharded, row_blk, n_s, s0, piece_rows=2048, a_square=False):
    lp = a.shape[0]
    bm = _contract_rows(lp)
    steps = lp // bm

    def body(a_ref, b_ref, *rest):
        o_ref, acc = rest[-2], rest[-1]
        k = pl.program_id(1)

        @pl.when(k == 0)
        def _():
            acc[...] = jnp.zeros_like(acc)
        a_v = a_ref[...]
        acc[...] += _dot(a_v * a_v if a_square else a_v, b_ref[...], TN)

        @pl.when(k == steps - 1)
        def _():
            o_ref[0] = acc[...].astype(bf16)

    in_specs = [pl.BlockSpec((bm, ka), (lambda s, k: (k, s)) if a_sharded else (lambda s, k: (k, 0))),
                pl.BlockSpec((bm, D), (lambda s, k: (k, 0)) if a_sharded else (lambda s, k: (k, s)))]
    args = [a, b]
    aliases = {}
    if slab is not None:
        in_specs.append(_ANY)
        args.append(slab)
        aliases = {2: 0}
    return pl.pallas_call(
        body, name=name, grid=(n_s, steps), in_specs=in_specs,
        out_specs=pl.BlockSpec((1, ka, D), lambda s, k: (s0 + s, row_blk, 0)),
        out_shape=jax.ShapeDtypeStruct((4, piece_rows, D), bf16),
        scratch_shapes=[pltpu.VMEM((ka, D), f32)], input_output_aliases=aliases,
        compiler_params=_cp(("arbitrary", "arbitrary")))(*args)


def _dw_in_t(dproj, n0):
    lp = n0.shape[0]
    bm = _contract_rows(lp, BIG_ROWS)
    steps = lp // bm
    bn = 512

    def body(a_ref, b_ref, o_ref, acc):
        k = pl.program_id(1)

        @pl.when(k == 0)
        def _():
            acc[...] = jnp.zeros_like(acc)
        acc[...] += _dot(a_ref[...], b_ref[...], TN)

        @pl.when(k == steps - 1)
        def _():
            o_ref[...] = acc[...].astype(bf16)

    return pl.pallas_call(
        body, name="dw_in", grid=(D // bn, steps),
        in_specs=[pl.BlockSpec((bm, W_PROJ), lambda j, k: (k, 0)), pl.BlockSpec((bm, bn), lambda j, k: (k, j))],
        out_specs=pl.BlockSpec((W_PROJ, bn), lambda j, k: (0, j)),
        out_shape=jax.ShapeDtypeStruct((W_PROJ, D), bf16),
        scratch_shapes=[pltpu.VMEM((W_PROJ, bn), f32)],
        compiler_params=_cp(("arbitrary", "arbitrary")))(dproj, n0)


def _head_expand():
    h = lax.broadcasted_iota(jnp.int32, (HEADS, D), 0)
    c = lax.broadcasted_iota(jnp.int32, (HEADS, D), 1)
    return jnp.where((c >> 6) == h, 1.0, 0.0).astype(bf16)


def _ssd_common(i, P, prev8, cw, cb, dtb, alog, xc=None):
    z = P[:, O_Z:O_Z + D]
    xp = P[:, O_XBC:O_XBC + D_XBC]
    dt_raw = P[:, O_DT:O_DT + HEADS]
    row = lax.broadcasted_iota(jnp.int32, (CH, 1), 0)
    if xc is None:
        row8 = lax.broadcasted_iota(jnp.int32, (8, 1), 0)
        xc = cb + cw[3:4] * xp
        for k in (1, 2, 3):
            rolled = pltpu.roll(xp, k, 0)
            fix = pltpu.roll(prev8, k, 0)
            top = jnp.where(row8 < k, fix, rolled[0:8])
            xc = xc + cw[3 - k:4 - k] * jnp.concatenate([top, rolled[8:]], axis=0)
    sg = _sigmoid(xc)
    xbc = xc * sg
    live = jnp.where(jnp.logical_or(i > 0, row >= CH - N_META), 1.0, 0.0)
    pre = dt_raw + dtb
    dt = jnp.where(pre > 20.0, pre, jnp.log(1.0 + jnp.exp(jnp.minimum(pre, 20.0)))) * live
    a = -jnp.exp(alog)
    dta = dt * a
    r_i = lax.broadcasted_iota(jnp.int32, (CH, CH), 0)
    c_i = lax.broadcasted_iota(jnp.int32, (CH, CH), 1)
    tril = r_i >= c_i
    acs = _split_dot(dta, jnp.where(tril, 1.0, 0.0).astype(bf16), None, 3, v_is_lhs=False)
    acs_t = _split_dot(dta, jnp.where(r_i <= c_i, 1.0, 0.0).astype(bf16), TN, 3)
    e = _head_expand()
    acs_e = _split_dot(acs, e, None, 3)
    dt_e = _split_dot(dt, e, None, 3)
    return dict(z=z, xp=xp, xc=xc, sg=sg, xbc=xbc, live=live, pre=pre, dt=dt, a=a, tril=tril,
                acs=acs, acs_t=acs_t, e=e, acs_e=acs_e, dt_e=dt_e)


def _lmat(c, h):
    seg = c["acs"][:, h:h + 1] - c["acs_t"][h:h + 1, :]
    return jnp.where(c["tril"], jnp.exp(jnp.minimum(seg, 0.0)), 0.0)


def _pair_masks():
    lane = lax.broadcasted_iota(jnp.int32, (1, 128), 1)
    return jnp.where(lane < HEAD_DIM, 1.0, 0.0), jnp.where(lane >= HEAD_DIM, 1.0, 0.0)


def _ssd_fwd(proj, conv_w, conv_b, dt_bias, a_log, d_ssd, g_ssd, nch, comm=None):
    def body(p_ref, cw_ref, cb_ref, dtb_ref, al_ref, d_ref, g_ref, y_ref, ys_ref, st_ref, xc_ref, prev8_ref, state_ref):
        i = pl.program_id(0)

        @pl.when(i == 0)
        def _():
            prev8_ref[...] = jnp.zeros_like(prev8_ref)
            state_ref[...] = jnp.zeros_like(state_ref)

        P = p_ref[...]
        c = _ssd_common(i, P, prev8_ref[...], cw_ref[...], cb_ref[...], dtb_ref[...], al_ref[...])
        prev8_ref[...] = c["xp"][CH - 8:CH]
        xc_ref[...] = c["xc"]
        xbc = c["xbc"]
        x = xbc[:, 0:D]
        xdt = x * c["dt_e"]
        a_last_e = c["acs_e"][CH - 1:CH, :]
        w_end = (xdt * jnp.exp(a_last_e - c["acs_e"])).astype(bf16)
        m0, m1 = _pair_masks()
        ys = []
        for g in range(2):
            bg = xbc[:, D + NSTATE * g:D + NSTATE * (g + 1)].astype(bf16)
            cg = xbc[:, D + 2 * NSTATE + NSTATE * g:D + 2 * NSTATE + NSTATE * (g + 1)].astype(bf16)
            gmat = _dot(cg, bg, NT)
            st = state_ref[g]
            st_ref[0, g] = st
            sl = slice(512 * g, 512 * (g + 1))
            y_off = _dot(cg, st.astype(bf16)) * jnp.exp(c["acs_e"][:, sl])
            contrib = _dot(bg, w_end[:, sl], TN)
            state_ref[g] = st * jnp.exp(a_last_e[:, sl]) + contrib
            yd = []
            for pr in range(4):
                h0 = 8 * g + 2 * pr
                xp2 = xdt[:, 128 * (4 * g + pr):128 * (4 * g + pr + 1)]
                ma = (gmat * _lmat(c, h0)).astype(bf16)
                mb = (gmat * _lmat(c, h0 + 1)).astype(bf16)
                yd.append(_dot(ma, (xp2 * m0).astype(bf16)) + _dot(mb, (xp2 * m1).astype(bf16)))
            ys.append(jnp.concatenate(yd, axis=1) + y_off)
        d_e = _split_dot(d_ref[...], c["e"], None, 3)
        y = jnp.concatenate(ys, axis=1) + x * d_e
        y_ref[...] = y
        yg = y * (c["z"] * _sigmoid(c["z"]))
        ys_ref[...] = _rms(yg, g_ref[...]).astype(bf16)

    full = lambda a: pl.BlockSpec(a.shape, lambda i, nd=a.ndim: (0,) * nd)
    return _call(
        body, comm, name="ssd_fwd", nsteps=nch,
        in_specs=[pl.BlockSpec((CH, W_PROJ), lambda i: (i, 0))] + [full(a) for a in (conv_w, conv_b, dt_bias, a_log, d_ssd, g_ssd)],
        out_specs=[pl.BlockSpec((CH, D), lambda i: (i, 0)), pl.BlockSpec((CH, D), lambda i: (i, 0)),
                   pl.BlockSpec((1, 2, NSTATE, 512), lambda i: (i, 0, 0, 0)), pl.BlockSpec((CH, D_XBC), lambda i: (i, 0))],
        out_shape=[jax.ShapeDtypeStruct((nch * CH, D), f32), jax.ShapeDtypeStruct((nch * CH, D), bf16),
                   jax.ShapeDtypeStruct((nch, 2, NSTATE, 512), f32), jax.ShapeDtypeStruct((nch * CH, D_XBC), f32)],
        scratch_shapes=[pltpu.VMEM((8, D_XBC), f32), pltpu.VMEM((2, NSTATE, 512), f32)],
        args=[proj, conv_w, conv_b, dt_bias, a_log, d_ssd, g_ssd])


def _ssd_bwd(proj, xc_all, y, dys, du, states, conv_w, conv_b, dt_bias, a_log, d_ssd, g_ssd, nch, comm=None):
    def body(p_ref, xc_ref, y_ref, dys_ref, du_ref, st_ref, cw_ref, cb_ref, dtb_ref, al_ref, d_ref, g_ref,
             dp_ref, dcw_ref, dcb_ref, ddtb_ref, dal_ref, dd_ref, dg_ref, nxt8_ref, dst_ref):
        step = pl.program_id(0)
        i = nch - 1 - step

        @pl.when(step == 0)
        def _():
            nxt8_ref[...] = jnp.zeros_like(nxt8_ref)
            dst_ref[...] = jnp.zeros_like(dst_ref)
            for r in (dcw_ref, dcb_ref, ddtb_ref, dal_ref, dd_ref, dg_ref):
                r[...] = jnp.zeros_like(r)

        P = p_ref[...]
        c = _ssd_common(i, P, None, cw_ref[...], cb_ref[...], dtb_ref[...], al_ref[...], xc=xc_ref[...])
        xbc, z, e = c["xbc"], c["z"], c["e"]
        x = xbc[:, 0:D]
        yv = y_ref[...]
        sz = _sigmoid(z)
        silu_z = z * sz
        dyg, dg8 = _rms_bwd(dys_ref[...], yv * silu_z, g_ref[...])
        dg_ref[...] += dg8
        dy = dyg * silu_z
        dz = dyg * yv * (sz * (1.0 + z * (1.0 - sz)))
        d_e = _split_dot(d_ref[...], e, None, 3)
        dd_ref[...] += _rsum8(dy * x)
        xdt = x * c["dt_e"]
        a_last_e = c["acs_e"][CH - 1:CH, :]
        e_end = jnp.exp(a_last_e - c["acs_e"])
        w_end = xdt * e_end
        e_acs = jnp.exp(c["acs_e"])
        dy_dec = dy * e_acs
        m0, m1 = _pair_masks()
        lane16 = lax.broadcasted_iota(jnp.int32, (1, HEADS), 1)
        row16 = lax.broadcasted_iota(jnp.int32, (HEADS, 1), 0)
        dacs = jnp.zeros((CH, HEADS), f32)
        dacs_t = jnp.zeros((HEADS, CH), f32)
        dxdt_parts, dbs, dcs, zparts, yoff_parts, dlast_parts = [], [], [], [], [], []
        for g in range(2):
            sl = slice(512 * g, 512 * (g + 1))
            bg = xbc[:, D + NSTATE * g:D + NSTATE * (g + 1)].astype(bf16)
            cg = xbc[:, D + 2 * NSTATE + NSTATE * g:D + 2 * NSTATE + NSTATE * (g + 1)].astype(bf16)
            gmat = _dot(cg, bg, NT)
            st = st_ref[0, g]
            dstn = dst_ref[g]
            dstn_b = dstn.astype(bf16)
            y_off = _dot(cg, st.astype(bf16)) * e_acs[:, sl]
            yoff_parts.append(y_off)
            bds = _dot(bg, dstn_b)
            zparts.append(w_end[:, sl] * bds)
            dlast_parts.append(jnp.sum(dstn * st, axis=0, keepdims=True) * jnp.exp(a_last_e[:, sl]))
            dg_acc = jnp.zeros((CH, CH), f32)
            dxd = []
            for pr in range(4):
                lo = 128 * (4 * g + pr)
                xp2 = xdt[:, lo:lo + 128].astype(bf16)
                dy2 = dy[:, lo:lo + 128]
                outp = jnp.zeros((CH, 128), f32)
                for hh, msk in ((0, m0), (1, m1)):
                    h = 8 * g + 2 * pr + hh
                    lm = _lmat(c, h)
                    dyh = (dy2 * msk).astype(bf16)
                    mh = (gmat * lm).astype(bf16)
                    outp = outp + _dot(mh, dyh, TN)
                    dml = _dot(dyh, xp2, NT) * lm
                    dg_acc = dg_acc + dml
                    q = dml * gmat
                    dacs = dacs + jnp.where(lane16 == h, jnp.sum(q, axis=1, keepdims=True), 0.0)
                    dacs_t = dacs_t + jnp.where(row16 == h, jnp.sum(q, axis=0, keepdims=True), 0.0)
                dxd.append(outp)
            dxdt_parts.append(jnp.concatenate(dxd, axis=1) + e_end[:, sl] * bds)
            dgb = dg_acc.astype(bf16)
            dcs.append(_dot(dgb, bg) + _dot(dy_dec[:, sl].astype(bf16), st.astype(bf16), NT))
            dbs.append(_dot(dgb, cg, TN) + _dot(w_end[:, sl].astype(bf16), dstn_b, NT))
            dst_ref[g] = dstn * jnp.exp(a_last_e[:, sl]) + _dot(cg, dy_dec[:, sl].astype(bf16), TN)
        dxdt = jnp.concatenate(dxdt_parts, axis=1)
        zfull = jnp.concatenate(zparts, axis=1)
        y_off_full = jnp.concatenate(yoff_parts, axis=1)
        dlast = jnp.concatenate(dlast_parts, axis=1)
        red = lambda v: _split_dot(v, e, NT, 2)
        eye16 = jnp.where(lax.broadcasted_iota(jnp.int32, (HEADS, HEADS), 0) == lax.broadcasted_iota(jnp.int32, (HEADS, HEADS), 1),
                          1.0, 0.0).astype(bf16)
        dacs = dacs - _split_dot(dacs_t, eye16, TN, 3)
        zred = red(zfull)
        dacs = dacs + red(dy * y_off_full) - zred
        last_term = jnp.sum(zred, axis=0, keepdims=True) + red(dlast)
        rowc = lax.broadcasted_iota(jnp.int32, (CH, 1), 0)
        dacs = dacs + jnp.where(rowc == CH - 1, last_term, 0.0)
        r_i = lax.broadcasted_iota(jnp.int32, (CH, CH), 0)
        c_i = lax.broadcasted_iota(jnp.int32, (CH, CH), 1)
        ddta = _split_dot(dacs, jnp.where(c_i >= r_i, 1.0, 0.0).astype(bf16), None, 3, v_is_lhs=False)
        ddt = ddta * c["a"] + red(dxdt * x)
        dal_ref[...] += _rsum8(ddta * c["dt"] * c["a"])
        ddt_raw = ddt * _sigmoid(c["pre"]) * c["live"]
        ddtb_ref[...] += _rsum8(ddt_raw)
        dx = dy * d_e + dxdt * c["dt_e"]
        dxbc = jnp.concatenate([dx, dbs[0], dbs[1], dcs[0], dcs[1]], axis=1)
        sg = c["sg"]
        dxc = dxbc * (sg * (1.0 + c["xc"] * (1.0 - sg)))
        dcb_ref[...] += _rsum8(dxc)
        xp = c["xp"]
        row8 = lax.broadcasted_iota(jnp.int32, (8, 1), 0)
        cw = cw_ref[...]
        dxp = cw[3:4] * dxc
        dcw = jnp.where(row8 == 3, jnp.sum(dxc * xp, axis=0, keepdims=True), 0.0)
        nxt8 = nxt8_ref[...]
        for j in (1, 2, 3):
            rolled = pltpu.roll(dxc, CH - j, 0)
            fix = pltpu.roll(nxt8, 8 - j, 0)
            bot = jnp.where(row8 >= 8 - j, fix, rolled[CH - 8:CH])
            later = jnp.concatenate([rolled[:CH - 8], bot], axis=0)
            dxp = dxp + cw[3 - j:4 - j] * later
            dcw = dcw + jnp.where(row8 == 3 - j, jnp.sum(later * xp, axis=0, keepdims=True), 0.0)
        dcw_ref[...] += dcw
        nxt8_ref[...] = dxc[0:8]
        dp_ref[:, O_Z:O_Z + D] = dz.astype(bf16)
        dp_ref[:, O_U:O_U + D] = du_ref[...].astype(bf16)
        dp_ref[:, O_XBC:O_XBC + D_XBC] = dxp.astype(bf16)
        dp_ref[:, O_DT:W_PROJ] = jnp.zeros((CH, W_PROJ - O_DT), bf16)
        dp_ref[:, O_DT:O_DT + HEADS] = ddt_raw.astype(bf16)

    full = lambda a: pl.BlockSpec(a.shape, lambda s, nd=a.ndim: (0,) * nd)
    rev = lambda s: (nch - 1 - s, 0)
    acc = lambda cdim: pl.BlockSpec((8, cdim), lambda s: (0, 0))
    return _call(
        body, comm, name="ssd_bwd", nsteps=nch,
        in_specs=[pl.BlockSpec((CH, W_PROJ), rev), pl.BlockSpec((CH, D_XBC), rev),
                  pl.BlockSpec((CH, D), rev), pl.BlockSpec((CH, D), rev), pl.BlockSpec((CH, D), rev),
                  pl.BlockSpec((1, 2, NSTATE, 512), lambda s: (nch - 1 - s, 0, 0, 0))]
        + [full(a) for a in (conv_w, conv_b, dt_bias, a_log, d_ssd, g_ssd)],
        out_specs=[pl.BlockSpec((CH, W_PROJ), rev), acc(D_XBC), acc(D_XBC), acc(HEADS), acc(HEADS), acc(D), acc(D)],
        out_shape=[jax.ShapeDtypeStruct((nch * CH, W_PROJ), bf16)]
        + [jax.ShapeDtypeStruct((8, cdim), f32) for cdim in (D_XBC, D_XBC, HEADS, HEADS, D, D)],
        scratch_shapes=[pltpu.VMEM((8, D_XBC), f32), pltpu.VMEM((2, NSTATE, 512), f32)],
        args=[proj, xc_all, y, dys, du, states, conv_w, conv_b, dt_bias, a_log, d_ssd, g_ssd])


SCAN_UNROLL = 8


def _to_slabs(slab_ref, q, mat):
    for ls in range(8):
        slab_ref[ls, pl.ds(PITCH * q, CH), :] = mat[:, 128 * ls:128 * (ls + 1)]


def _from_slabs(slab, q):
    return jnp.concatenate([slab(ls, PITCH * q) for ls in range(8)], axis=1)


def _tile(slab_ref, ls, t, lead=None):
    idx = (ls, pl.ds(t, 8, stride=PITCH), slice(None))
    return slab_ref[idx] if lead is None else slab_ref[(lead,) + idx]


def _s5_fwd(proj, bbq, ccq_t, ar, ai, d_skip, nch, comm=None):
    def body(u_ref, bb_ref, cc_ref, ar_ref, ai_ref, d_ref, s_ref, yl_ref, y5_ref, bu_ref, st_ref):
        @pl.when(pl.program_id(0) == 0)
        def _():
            st_ref[...] = jnp.zeros_like(st_ref)
        u = u_ref[...]
        ub = u.astype(bf16)
        for q in range(NQ):
            _to_slabs(bu_ref, q, _dot(ub[:, 128 * q:128 * (q + 1)], bb_ref[q]))
        ar_t = [ar_ref[:, 128 * l:128 * (l + 1)] for l in range(4)]
        ai_t = [ai_ref[:, 128 * l:128 * (l + 1)] for l in range(4)]

        def one(t, carry):
            re, im = carry
            nre, nim = [], []
            for l in range(4):
                a = ar_t[l] * re[l] - ai_t[l] * im[l] + _tile(bu_ref, l, t)
                b = ar_t[l] * im[l] + ai_t[l] * re[l] + _tile(bu_ref, l + 4, t)
                s_ref[0, l, pl.ds(t, 8, stride=PITCH), :] = a
                s_ref[0, l + 4, pl.ds(t, 8, stride=PITCH), :] = b
                nre.append(a)
                nim.append(b)
            return tuple(nre), tuple(nim)

        def step(tt, carry):
            for k in range(SCAN_UNROLL):
                carry = one(tt * SCAN_UNROLL + k, carry)
            return carry
        init = (tuple(st_ref[l] for l in range(4)), tuple(st_ref[l + 4] for l in range(4)))
        re, im = lax.fori_loop(0, CH // SCAN_UNROLL, step, init)
        for l in range(4):
            st_ref[l] = re[l]
            st_ref[l + 4] = im[l]
        ys = []
        for q in range(NQ):
            sq = _from_slabs(lambda ls, r0: s_ref[0, ls, pl.ds(r0, CH), :], q).astype(bf16)
            ys.append(_dot(sq, cc_ref[q]))
        yl = jnp.concatenate(ys, axis=1) + u * d_ref[...]
        yl_ref[...] = yl
        y5_ref[...] = (0.5 * yl * (1.0 + lax.erf(yl * (1.0 / math.sqrt(2.0))))).astype(bf16)

    const = lambda a: pl.BlockSpec(a.shape, lambda i, nd=a.ndim: (0,) * nd)
    return _call(
        body, comm, name="s5_fwd", nsteps=nch,
        in_specs=[pl.BlockSpec((CH, D), lambda i: (i, O_U // D)), const(bbq), const(ccq_t), const(ar), const(ai), const(d_skip)],
        out_specs=[pl.BlockSpec((1, 8, 8 * PITCH, 128), lambda i: (i, 0, 0, 0)),
                   pl.BlockSpec((CH, D), lambda i: (i, 0)), pl.BlockSpec((CH, D), lambda i: (i, 0))],
        out_shape=[jax.ShapeDtypeStruct((nch, 8, 8 * PITCH, 128), f32), jax.ShapeDtypeStruct((nch * CH, D), f32),
                   jax.ShapeDtypeStruct((nch * CH, D), bf16)],
        scratch_shapes=[pltpu.VMEM((8, 8 * PITCH, 128), f32), pltpu.VMEM((8, 8, 128), f32)],
        args=[proj, bbq, ccq_t, ar, ai, d_skip])


def _s5_bwd(proj, dyl, s_all, bbtq, cctq, ar, ai, d_skip, nch, comm=None):
    def body(u_ref, dy_ref, s_ref, bbt_ref, cct_ref, ar_ref, ai_ref, d_ref,
             du_ref, dcc_ref, dbb_ref, dab_ref, dd_ref, ga_ref, st_ref):
        @pl.when(pl.program_id(0) == 0)
        def _():
            st_ref[...] = jnp.zeros_like(st_ref)
            for r in (dcc_ref, dbb_ref, dab_ref, dd_ref):
                r[...] = jnp.zeros_like(r)
        u = u_ref[...]
        dyl_v = dy_ref[...]
        dd_ref[...] += _rsum8(dyl_v * u)
        ub = u.astype(bf16)
        dyb = dyl_v.astype(bf16)
        for q in range(NQ):
            _to_slabs(ga_ref, q, _dot(dyb[:, 128 * q:128 * (q + 1)], cct_ref[q]))
        ar_t = [ar_ref[:, 128 * l:128 * (l + 1)] for l in range(4)]
        ai_t = [ai_ref[:, 128 * l:128 * (l + 1)] for l in range(4)]

        def one(t, carry):
            re, im, dar, dai = carry
            nre, nim, ndar, ndai = [], [], [], []
            for l in range(4):
                sre = _tile(s_ref, l, t, lead=0)
                sim = _tile(s_ref, l + 4, t, lead=0)
                ndar.append(dar[l] + re[l] * sre + im[l] * sim)
                ndai.append(dai[l] + im[l] * sre - re[l] * sim)
                a = _tile(ga_ref, l, t) + ar_t[l] * re[l] + ai_t[l] * im[l]
                b = _tile(ga_ref, l + 4, t) - ai_t[l] * re[l] + ar_t[l] * im[l]
                ga_ref[l, pl.ds(t, 8, stride=PITCH), :] = a
                ga_ref[l + 4, pl.ds(t, 8, stride=PITCH), :] = b
                nre.append(a)
                nim.append(b)
            return tuple(nre), tuple(nim), tuple(ndar), tuple(ndai)

        def step(tt, carry):
            for k in range(SCAN_UNROLL):
                carry = one(CH - 1 - (tt * SCAN_UNROLL + k), carry)
            return carry
        four = lambda ref, o: tuple(ref[l + o] for l in range(4))
        re, im, dar, dai = lax.fori_loop(0, CH // SCAN_UNROLL, step,
                                         (four(st_ref, 0), four(st_ref, 4), four(dab_ref, 0), four(dab_ref, 4)))
        for l in range(4):
            st_ref[l], st_ref[l + 4] = re[l], im[l]
            dab_ref[l], dab_ref[l + 4] = dar[l], dai[l]
        dus = []
        for q in range(NQ):
            aq = _from_slabs(lambda ls, r0: ga_ref[ls, pl.ds(r0, CH), :], q).astype(bf16)
            sq = _from_slabs(lambda ls, r0: s_ref[0, ls, pl.ds(r0, CH), :], q).astype(bf16)
            dcc_ref[q] += _dot(dyb[:, 128 * q:128 * (q + 1)], sq, TN)
            dbb_ref[q] += _dot(ub[:, 128 * q:128 * (q + 1)], aq, TN)
            dus.append(_dot(aq, bbt_ref[q]))
        du_ref[...] = jnp.concatenate(dus, axis=1) + dyl_v * d_ref[...]

    const = lambda a: pl.BlockSpec(a.shape, lambda s, nd=a.ndim: (0,) * nd)
    rev = lambda s: (nch - 1 - s, 0)
    return _call(
        body, comm, name="s5_bwd", nsteps=nch,
        in_specs=[pl.BlockSpec((CH, D), lambda s: (nch - 1 - s, O_U // D)), pl.BlockSpec((CH, D), rev),
                  pl.BlockSpec((1, 8, 8 * PITCH, 128), lambda s: (nch - 1 - s, 0, 0, 0)),
                  const(bbtq), const(cctq), const(ar), const(ai), const(d_skip)],
        out_specs=[pl.BlockSpec((CH, D), rev), pl.BlockSpec((NQ, 128, D), lambda s: (0, 0, 0)),
                   pl.BlockSpec((NQ, 128, D), lambda s: (0, 0, 0)), pl.BlockSpec((8, 8, 128), lambda s: (0, 0, 0)),
                   pl.BlockSpec((8, D), lambda s: (0, 0))],
        out_shape=[jax.ShapeDtypeStruct((nch * CH, D), f32), jax.ShapeDtypeStruct((NQ, 128, D), f32),
                   jax.ShapeDtypeStruct((NQ, 128, D), f32), jax.ShapeDtypeStruct((8, 8, 128), f32),
                   jax.ShapeDtypeStruct((8, D), f32)],
        scratch_shapes=[pltpu.VMEM((8, 8 * PITCH, 128), f32), pltpu.VMEM((8, 8, 128), f32)],
        args=[proj, dyl, s_all, bbtq, cctq, ar, ai, d_skip])


def _s5_tables(lam_re, lam_im, log_step, b_re, b_im):
    step = jnp.exp(log_step)[:, None]
    mag = jnp.exp(lam_re * step)
    ab_re = mag * jnp.cos(lam_im * step)
    ab_im = mag * jnp.sin(lam_im * step)
    den = lam_re * lam_re + lam_im * lam_im
    coef_re = ((ab_re - 1.0) * lam_re + ab_im * lam_im) / den
    coef_im = (ab_im * lam_re - (ab_re - 1.0) * lam_im) / den
    bb_re = coef_re[..., None] * b_re - coef_im[..., None] * b_im
    bb_im = coef_re[..., None] * b_im + coef_im[..., None] * b_re
    return ab_re, ab_im, bb_re, bb_im


def _blockdiag_in(m_re, m_im):
    eye = jnp.eye(8, dtype=f32)

    def one(m):
        m = m.reshape(NQ, 8, S5_P, 16)
        return jnp.einsum("qgph,gk->qghkp", m, eye).reshape(NQ, 128, 512)
    return jnp.concatenate([one(m_re), one(m_im)], axis=2)


def _blockdiag_in_grad(dm):
    def one(x):
        x = x.reshape(NQ, 8, 16, 8, S5_P)
        return jnp.einsum("qghgp->qgph", x).reshape(NQ * 8, S5_P, 16)
    return one(dm[:, :, :512]), one(dm[:, :, 512:])


def _local_step(x2, tgt2, meta, p, w_in_t, slab):
    seq = x2.shape[0]
    nch = 1 + seq // CH
    bmb = BIG_ROWS if (nch * CH) % BIG_ROWS == 0 else CH
    nbig = nch * CH // bmb
    metablk = jnp.concatenate([jnp.zeros((CH - N_META, D), f32), meta, jnp.zeros((bmb - CH, D), f32)], axis=0)
    w_full = (w_in_t, pl.BlockSpec(w_in_t.shape, lambda i: (0, 0), pipeline_mode=pl.Buffered(1)))

    def lead(i, v):
        return jnp.logical_and(i == 0, lax.broadcasted_iota(jnp.int32, (v.shape[0], 1), 0) < CH)
    h0_of = lambda i, s, q: jnp.where(lead(i, s[0]), q[0][:s[0].shape[0]], s[0])

    def in_fn(i, r, s, q, w):
        nb = _rms(h0_of(i, s, q), q[1]).astype(bf16)
        return [_dot(nb, w[0][...], NT), nb], []
    (proj, n0), _, (g_up,) = _rowwise("in_proj", in_fn, nbig, shifted=[x2], pars=[metablk, p["g_mix"]], refs=[w_full],
                                      out_rows=[(W_PROJ, f32), (D, bf16)], bm=bmb, comm=_gather_piece(slab, R_UP, 1024))
    (y, y_ssd, states, xc_all), (g_down,) = _ssd_fwd(proj, p["conv_w"], p["conv_b"], p["dt_bias"], p["a_log"], p["d_ssd"],
                                                     p["g_ssd"], nch, comm=_gather_piece(slab, R_DOWN, 1024))

    ab_re, ab_im, bb_re, bb_im = _s5_tables(p["lam_re"], p["lam_im"], p["log_step"], p["b_re"], p["b_im"])
    ar, ai = ab_re.reshape(NQ, 512), ab_im.reshape(NQ, 512)
    bbq = _blockdiag_in(bb_re, bb_im)
    ccq = _blockdiag_in(jnp.swapaxes(p["c_re"], 1, 2), -jnp.swapaxes(p["c_im"], 1, 2))
    d_skip = p["d_s5"].reshape(1, D)
    (s_all, ylin, y5), (g_go,) = _s5_fwd(proj, bbq.astype(bf16), jnp.swapaxes(ccq, 1, 2).astype(bf16), ar, ai, d_skip, nch,
                                         comm=_gather_piece(slab, R_GLU, 1024))
    whole = lambda a: (a, pl.BlockSpec(a.shape, lambda i: (0, 0, 0), pipeline_mode=pl.Buffered(1)))
    w_up, w_down = whole(g_up), whole(g_down)
    w_glu_t = (g_go, pl.BlockSpec((4, 512, D), lambda i: (0, 0, 0), pipeline_mode=pl.Buffered(1)))
    w_out = (g_go, pl.BlockSpec((4, 512, D), lambda i: (0, 1, 0), pipeline_mode=pl.Buffered(1)))

    def glu_fn(i, r, s, q, w):
        v = jnp.concatenate([_dot(r[0], w[0][k], NT) for k in range(4)], axis=1) + q[0]
        return [v, _rms(v[:, :D] * _sigmoid(v[:, D:]), q[1])], []
    (v, y_s5), _ = _rowwise("glu", glu_fn, nbig, rows=[y5], pars=[p["b_glu"], p["g_s5"]], refs=[w_glu_t],
                            out_rows=[(2 * D, bf16), (D, bf16)], bm=bmb)

    def out_fn(i, r, s, q, w):
        acc = (_dot(r[0][:, :512], w[0][0]) + _dot(r[0][:, 512:], w[0][1]) + _dot(r[1][:, :512], w[0][2])
               + _dot(r[1][:, 512:], w[0][3]))
        return [h0_of(i, s, q) + acc], []
    (h1,), _ = _rowwise("out_proj", out_fn, nbig, rows=[y_ssd, y_s5], shifted=[x2], pars=[metablk], refs=[w_out],
                        out_rows=[(D, f32)], bm=bmb)

    def up_fn(i, r, s, q, w):
        nb = _rms(r[0], q[0]).astype(bf16)
        return [jnp.concatenate([jnp.maximum(_dot(nb, w[0][k]), 0.0).astype(bf16) for k in range(4)], axis=1), nb], []
    (relu_m, n1), _ = _rowwise("up_proj", up_fn, nbig, rows=[h1], pars=[p["g_mlp"]], refs=[w_up],
                               out_rows=[(4 * D, bf16), (D, bf16)], bm=bmb)

    def down_fn(i, r, s, q, w):
        acc = None
        for k in range(4):
            t = r[0][:, D * k:D * (k + 1)]
            part = _dot(t * t, w[0][k])
            acc = part if acc is None else acc + part
        return [r[1] + acc], []
    (h2,), _ = _rowwise("down_proj", down_fn, nbig, rows=[relu_m, h1], refs=[w_down], out_rows=[(D, f32)], bm=bmb)

    def final_fn(i, r, s, q, w):
        err = jnp.where(lead(i, r[0]), 0.0, _rms(r[0], q[0]) - s[0])
        dh, dg8 = _rms_bwd(err * (1.0 / D), r[0], q[0])
        return [dh, dh], [_rsum8(err * err), dg8]
    (dh2, dh2_b), (loss8, dgf8) = _rowwise("final", final_fn, nbig, rows=[h2], shifted=[tgt2], pars=[p["g_final"]],
                                           out_rows=[(D, f32), (D, bf16)], out_accs=[D, D], bm=bmb)
    loss = 0.5 / D * jnp.sum(loss8)

    def down_bwd_fn(i, r, s, q, w):
        dm_ = [_dot(r[0], w[0][k], NT) * (2.0 * r[1][:, D * k:D * (k + 1)].astype(f32)) for k in range(4)]
        return [jnp.concatenate(dm_, axis=1)], []
    (dm,), _ = _rowwise("down_bwd", down_bwd_fn, nbig, rows=[dh2_b, relu_m], refs=[w_down], out_rows=[(4 * D, bf16)], bm=bmb)
    g_a = _dw_into("dw_down", relu_m, dh2_b, None, 1024, True, 1, 4, 0, piece_rows=2048, a_square=True)

    def up_bwd_fn(i, r, s, q, w):
        acc = _dot(r[0][:, :D], w[0][0], NT)
        for k in range(1, 4):
            acc = acc + _dot(r[0][:, D * k:D * (k + 1)], w[0][k], NT)
        dh, dg8 = _rms_bwd(acc, r[1], q[0])
        dh1_ = r[2] + dh
        return [dh1_, dh1_], [dg8]
    (dh1, dh1_b), (dgmlp8,) = _rowwise("up_bwd", up_bwd_fn, nbig, rows=[dm, h1, dh2], pars=[p["g_mlp"]], refs=[w_up],
                                       out_rows=[(D, f32), (D, bf16)], out_accs=[D], bm=bmb)
    g_a = _dw_into("dw_up", n1, dm, g_a, 1024, False, 0, 4, 0, piece_rows=2048)

    def out_bwd_fn(i, r, s, q, w):
        dmix = [_dot(r[0], w[0][k], NT) for k in range(4)]
        v1, v2 = r[1][:, :D].astype(f32), r[1][:, D:].astype(f32)
        s2 = _sigmoid(v2)
        dglu, dg8 = _rms_bwd(jnp.concatenate(dmix[2:], axis=1), v1 * s2, q[0])
        dv = jnp.concatenate([dglu * s2, dglu * v1 * s2 * (1.0 - s2)], axis=1)
        return [jnp.concatenate(dmix[:2], axis=1), dv], [dg8, _rsum8(dv)]
    (dys, dv), (dgs58, dbglu8) = _rowwise("out_bwd", out_bwd_fn, nbig, rows=[dh1_b, v], pars=[p["g_s5"]], refs=[w_out],
                                          out_rows=[(D, f32), (2 * D, bf16)], out_accs=[D, 2 * D], bm=bmb)
    g_b = _dw_into("dw_out_a", y_ssd, dh1_b, None, 512, True, 1, 2, 0, piece_rows=1024)
    g_b = _dw_into("dw_out_b", y_s5, dh1_b, g_b, 512, True, 1, 2, 2, piece_rows=1024)

    def glu_bwd_fn(i, r, s, q, w):
        acc = _dot(r[0][:, :512], w[0][0])
        for k in range(1, 4):
            acc = acc + _dot(r[0][:, 512 * k:512 * (k + 1)], w[0][k])
        yl = r[1]
        cdf = 0.5 * (1.0 + lax.erf(yl * (1.0 / math.sqrt(2.0))))
        pdf = jnp.exp(-0.5 * yl * yl) * (1.0 / math.sqrt(2.0 * math.pi))
        return [acc * (cdf + yl * pdf)], []
    (dylin,), _ = _rowwise("glu_bwd", glu_bwd_fn, nbig, rows=[dv, ylin], refs=[w_glu_t], out_rows=[(D, f32)], bm=bmb)
    g_b = _dw_into("dw_glu", dv, y5, g_b, 512, True, 0, 4, 0, piece_rows=1024)

    (du, dcc, dbb, dab, dds5), (land_a,) = _s5_bwd(proj, dylin, s_all, jnp.swapaxes(bbq, 1, 2).astype(bf16), ccq.astype(bf16),
                                                   ar, ai, d_skip, nch, comm=_scatter_piece(g_a))

    s8 = lambda a: jnp.sum(a, axis=0, keepdims=True)
    dab_q = jnp.swapaxes(dab.reshape(2, 4, NQ, 128), 1, 2).reshape(2, NQ * 8, S5_P)
    dbb_re, dbb_im = _blockdiag_in_grad(dbb)
    dcr, dci = _blockdiag_in_grad(dcc)
    _, vjp = jax.vjp(_s5_tables, p["lam_re"], p["lam_im"], p["log_step"], p["b_re"], p["b_im"])
    dlam_re, dlam_im, dlog_step, db_re, db_im = vjp((dab_q[0], dab_q[1], dbb_re, dbb_im))
    early = dict(lam_re=dlam_re, lam_im=dlam_im, log_step=dlog_step, b_re=db_re, b_im=db_im, c_re=jnp.swapaxes(dcr, 1, 2),
                 c_im=-jnp.swapaxes(dci, 1, 2), d_s5=s8(dds5).reshape(NQ * 8, 16), b_glu=s8(dbglu8), g_s5=s8(dgs58),
                 g_mlp=s8(dgmlp8), g_final=s8(dgf8).reshape(D))
    early_pack = _pack_small([early[n] for n in EARLY], _rows_for(EARLY))

    (dproj, dcw8, dcb8, ddtb8, dal8, dd8, dgssd8), (land_b, all_early) = _ssd_bwd(
        proj, xc_all, y, dys, du, states, p["conv_w"], p["conv_b"], p["dt_bias"], p["a_log"], p["d_ssd"], p["g_ssd"], nch,
        comm=_both(_scatter_piece(g_b), _gather_blocks(early_pack)))

    gt = _dw_in_t(dproj, n0)
    gt = jnp.concatenate([gt[0:1024], gt[O_XBC:O_XBC + D_XBC], gt[O_DT:O_DT + HEADS], gt[O_U:O_U + D]], axis=0).reshape(4, 900, D)
    g_c = jnp.concatenate([gt, jnp.zeros((4, 1024 - 900, D), bf16)], axis=1)

    def in_bwd_fn(i, r, s, q, w):
        dh, dg8 = _rms_bwd(_dot(r[0], w[0][...]), h0_of(i, s, q), q[1])
        return [r[1] + dh], [dg8]
    (dh0,), (dgmix8,), (land_c,) = _rowwise(
        "in_bwd", in_bwd_fn, nbig, rows=[dproj, dh1], shifted=[x2], pars=[metablk, p["g_mix"]], refs=[w_full],
        out_rows=[(D, f32)], out_accs=[D], bm=bmb, comm=_scatter_piece(g_c))

    hsum = lambda a: jnp.sum(s8(a).reshape(HEADS, HEAD_DIM), axis=1).reshape(1, HEADS)
    late = dict(g_mix=s8(dgmix8), conv_b=s8(dcb8), dt_bias=s8(ddtb8), a_log=s8(dal8), d_ssd=hsum(dd8), g_ssd=s8(dgssd8),
                conv_w=dcw8[0:4], meta_tokens=dh0[CH - N_META:CH], loss=loss.reshape(1))
    return dh0, [(g_a, land_a), (g_b, land_b), (g_c, land_c)], all_early, late


def _perm_rows_w_in(wt):
    return jnp.concatenate([wt[0:1024], wt[2576:3600], wt[1024:2560], wt[2560:2576],
                            jnp.zeros((W_PROJ - 3600, wt.shape[1]), wt.dtype)], axis=0)


def _unperm_cols_w_in(g):
    return jnp.concatenate([g[:, 0:1024], g[:, O_XBC:O_XBC + D_XBC], g[:, O_DT:O_DT + HEADS], g[:, O_U:O_U + D]], axis=1)


def _pack_shard(w_in, w_glu, w_out, w_up, w_down, spare):
    dt = w_in.dtype
    parts = [w_up, w_down, w_glu.T, w_out, w_in.T, spare,
             jnp.zeros((PACK_ROWS - R_SPARE - spare.shape[0], D), dt)]
    return jnp.concatenate(parts, axis=0)


def _allgather8(x_shard, name):
    m_per, n = x_shard.shape

    def body(x_ref, out_ref, send_sems, recv_sems, stage, local_sems):
        x, y, c = _place()
        me, sibling = (x, y, c), (x, y, 1 - c)
        chips = [(1 - x, y), (x, 1 - y), (1 - x, 1 - y)]

        def rows(px, py, pc):
            return out_ref.at[pl.ds((4 * px + 2 * py + pc) * m_per, m_per), :]

        def copy(k, block, to, src=None):
            return pltpu.make_async_remote_copy(
                src_ref=rows(*block) if src is None else src, dst_ref=rows(*block),
                send_sem=send_sems.at[k], recv_sem=recv_sems.at[k], device_id=to, device_id_type=MESH)

        load = pltpu.make_async_copy(x_ref, stage, local_sems.at[0])
        load.start()
        first = [copy(0, me, sibling, src=x_ref)]
        first += [copy(1 + j, me, (*chip, c), src=x_ref) for j, chip in enumerate(chips)]
        for cp in first:
            cp.start()
        load.wait()
        store = pltpu.make_async_copy(stage, rows(*me), local_sems.at[1])
        store.start()
        passed = [copy(4 + j, (*chip, c), sibling) for j, chip in enumerate(chips)]
        for j, chip in enumerate(chips):
            copy(1 + j, (*chip, c), me).wait_recv()
            passed[j].start()
        copy(0, sibling, me).wait_recv()
        for j, chip in enumerate(chips):
            copy(4 + j, (*chip, 1 - c), me).wait_recv()
        for cp in first + passed:
            cp.wait_send()
        store.wait()

    return pl.pallas_call(
        body, name=name, out_shape=jax.ShapeDtypeStruct((8 * m_per, n), x_shard.dtype),
        in_specs=[_ANY], out_specs=_ANY,
        scratch_shapes=[pltpu.SemaphoreType.DMA((7,)), pltpu.SemaphoreType.DMA((7,)), pltpu.VMEM((m_per, n), x_shard.dtype),
                        pltpu.SemaphoreType.DMA((2,))])(x_shard)


def _swap_sibling(r, name):
    def body(r_ref, out_ref, send_sem, recv_sem):
        x, y, c = _place()
        cp = pltpu.make_async_remote_copy(src_ref=r_ref, dst_ref=out_ref, send_sem=send_sem, recv_sem=recv_sem,
                                          device_id=(x, y, 1 - c), device_id_type=MESH)
        cp.start()
        cp.wait()

    return pl.pallas_call(
        body, name=name, out_shape=jax.ShapeDtypeStruct(r.shape, r.dtype), in_specs=[_ANY], out_specs=_ANY,
        scratch_shapes=[pltpu.SemaphoreType.DMA, pltpu.SemaphoreType.DMA])(r)


SH_CONVW, SH_META = 4 * 384, 16 * 256
SPARE_ROWS = 17

SMALL_SHAPES = dict(
    g_mix=(1, 1024), conv_b=(1, 1536), dt_bias=(1, 16), a_log=(1, 16), d_ssd=(1, 16), g_ssd=(1, 1024), lam_re=(1, 64, 64),
    lam_im=(1, 64, 64), log_step=(1, 64), b_re=(1, 64, 64, 16), b_im=(1, 64, 64, 16), c_re=(1, 64, 16, 64), c_im=(1, 64, 16, 64),
    d_s5=(1, 64, 16), b_glu=(1, 2048), g_s5=(1, 1024), g_mlp=(1, 1024), g_final=(1024,),
    conv_w=(4, D_XBC), meta_tokens=(N_META, D), loss=(1,))
EARLY = ["lam_re", "lam_im", "log_step", "b_re", "b_im", "c_re", "c_im", "d_s5", "b_glu", "g_s5", "g_mlp", "g_final"]
LATE = ["g_mix", "conv_b", "dt_bias", "a_log", "d_ssd", "g_ssd", "conv_w", "meta_tokens", "loss"]


def _rows_for(names):
    return -(-sum(math.prod(SMALL_SHAPES[n]) for n in names) // (8 * D)) * 8


def _pack_small(arrs, rows):
    flat = jnp.concatenate([a.reshape(-1).astype(f32) for a in arrs])
    return jnp.concatenate([flat, jnp.zeros((rows * D - flat.shape[0],), f32)]).reshape(rows, D)


def _unpack_small(slab, shapes):
    flat = slab.reshape(-1)
    out, o = [], 0
    for shp in shapes:
        n = math.prod(shp)
        out.append(flat[o:o + n].reshape(shp))
        o += n
    return out


def _sum8(g, rows, name):
    def body(g_ref, o_ref):
        acc = g_ref[0]
        for k in range(1, 8):
            acc = acc + g_ref[k]
        o_ref[...] = acc
    return pl.pallas_call(body, name=name, out_shape=jax.ShapeDtypeStruct((rows, D), f32),
                          compiler_params=_cp())(g.reshape(8, rows, D))


def _adam_math(w_, g_, m_, v_):
    m2 = ADAM_B1 * m_ + (1.0 - ADAM_B1) * g_
    v2 = ADAM_B2 * v_ + (1.0 - ADAM_B2) * jnp.square(g_)
    m_hat = m2 / (1.0 - ADAM_B1 ** ADAM_STEP)
    v_hat = v2 / (1.0 - ADAM_B2 ** ADAM_STEP)
    delta = -ADAM_LR * (m_hat / (jnp.sqrt(v_hat) + ADAM_EPS) + ADAM_WD * w_)
    return delta, m2, v2


def _adamw(name, w, g, m, v, bm):
    def fn(i, r, s, q, refs):
        return list(_adam_math(*r)), []
    c = w.shape[1]
    (d, m2, v2), _ = _rowwise(name, fn, w.shape[0] // bm, rows=[w, g, m, v], out_rows=[(c, f32)] * 3, bm=bm)
    return d, m2, v2


def _adamw_whole(name, w, g, m, v):
    def body(w_ref, g_ref, m_ref, v_ref, d_ref, m2_ref, v2_ref):
        d_ref[...], m2_ref[...], v2_ref[...] = _adam_math(w_ref[...], g_ref[...], m_ref[...], v_ref[...])
    return pl.pallas_call(body, name=name, out_shape=[jax.ShapeDtypeStruct(w.shape, f32)] * 3, compiler_params=_cp())(w, g, m, v)


def _sum_parts(name, own, land):
    def fn(i, r, s, q, refs):
        acc = r[0].astype(f32)
        for k in range(7):
            acc = acc + refs[0][k].astype(f32)
        return [acc], []
    (o,), _ = _rowwise(name, fn, own.shape[0] // CH, rows=[own],
                       refs=[(land, pl.BlockSpec((7, CH, D), lambda i: (0, i, 0)))], out_rows=[(D, f32)])
    return o


def kernel(x, meta_tokens, g_mix, w_in, conv_w, conv_b, dt_bias, a_log, d_ssd, g_ssd, lam_re, lam_im, log_step, b_re, b_im, c_re, c_im, d_s5, w_glu, b_glu, g_s5, w_out, g_mlp, w_up, w_down, g_final, loss_target, m_meta_tokens, m_g_mix, m_w_in, m_conv_w, m_conv_b, m_dt_bias, m_a_log, m_d_ssd, m_g_ssd, m_lam_re, m_lam_im, m_log_step, m_b_re, m_b_im, m_c_re, m_c_im, m_d_s5, m_w_glu, m_b_glu, m_g_s5, m_w_out, m_g_mlp, m_w_up, m_w_down, m_g_final, v_meta_tokens, v_g_mix, v_w_in, v_conv_w, v_conv_b, v_dt_bias, v_a_log, v_d_ssd, v_g_ssd, v_lam_re, v_lam_im, v_log_step, v_b_re, v_b_im, v_c_re, v_c_im, v_d_s5, v_w_glu, v_b_glu, v_g_s5, v_w_out, v_g_mlp, v_w_up, v_w_down, v_g_final):
    given = dict(locals())
    cx, cy, cc = _place()
    chip = 2 * cx + cy

    small_f = jnp.concatenate([conv_w.reshape(-1), meta_tokens.reshape(-1)])
    t_hi = small_f.astype(bf16)
    r_1 = small_f - t_hi.astype(f32)
    t_mid = r_1.astype(bf16)
    t_lo = (r_1 - t_mid.astype(f32)).astype(bf16)
    terms = jnp.concatenate([t_hi, t_mid, t_lo])
    spare = jnp.concatenate([terms, jnp.zeros((SPARE_ROWS * D - terms.shape[0],), bf16)]).reshape(SPARE_ROWS, D)
    slab = _pack_shard(w_in[0].astype(bf16), w_glu[0].astype(bf16), w_out[0].astype(bf16), w_up[0].astype(bf16),
                       w_down[0].astype(bf16), spare)
    my_half = lax.dynamic_slice_in_dim(slab, R_IN + cc * 512, 512, axis=0)
    gathered = _allgather8(my_half, "gather_w_in").reshape(4, 1024, D)
    w_in_t = _perm_rows_w_in(jnp.concatenate([gathered[s, 0:900] for s in range(4)], axis=0))
    n_sf = SH_CONVW + SH_META
    tr = gathered[:, 900:900 + SPARE_ROWS].reshape(4, SPARE_ROWS * D)[:, :3 * n_sf].astype(f32).reshape(4, 3, n_sf)
    sp = tr[:, 0] + tr[:, 1] + tr[:, 2]
    conv_w_full = jnp.concatenate([sp[s, :SH_CONVW].reshape(4, 384) for s in range(4)], axis=1)
    meta_full = jnp.concatenate([sp[s, SH_CONVW:].reshape(16, 256) for s in range(4)], axis=1)

    p = dict(g_mix=g_mix, conv_w=conv_w_full, conv_b=conv_b, dt_bias=dt_bias, a_log=a_log, d_ssd=d_ssd, g_ssd=g_ssd,
             lam_re=lam_re[0], lam_im=lam_im[0], log_step=log_step[0], b_re=b_re[0], b_im=b_im[0], c_re=c_re[0], c_im=c_im[0],
             d_s5=d_s5[0], b_glu=b_glu, g_s5=g_s5, g_mlp=g_mlp, g_final=g_final.reshape(1, D))
    dh0, pieces, all_early, late = _local_step(x[0], loss_target[0], meta_full, p, w_in_t, slab)
    grad_x = dh0[CH:].reshape(x.shape)

    reds = []
    for k, (gp, land) in enumerate(pieces):
        half = gp.shape[1] // 2
        own = lax.dynamic_slice(gp, (chip, cc * half, 0), (1, half, D)).reshape(half, D)
        reds.append(_sum_parts("rs_sum_%d" % k, own, land))
    red = jnp.concatenate(reds, axis=0)
    other = _swap_sibling(red, "rs_share")
    first = jnp.where(cc == 0, red, other)
    second = jnp.where(cc == 0, other, red)
    g_up, g_down = first[0:1024], second[0:1024]
    g_glu, g_out = first[1024:1536].T, second[1024:1536]
    g_in_t = jnp.concatenate([first[1536:2048], second[1536:1536 + 900 - 512]], axis=0)

    gs = dict(zip(EARLY, _unpack_small(_sum8(all_early, _rows_for(EARLY), "sum8_early"), [SMALL_SHAPES[n] for n in EARLY])))
    all_late = _allgather8(_pack_small([late[n] for n in LATE], _rows_for(LATE)), "gather_small")
    gs.update(zip(LATE, _unpack_small(_sum8(all_late, _rows_for(LATE), "sum8_late"), [SMALL_SHAPES[n] for n in LATE])))
    g_conv_w = lax.dynamic_slice_in_dim(gs.pop("conv_w"), chip * 384, 384, axis=1).reshape(conv_w.shape)
    g_meta = lax.dynamic_slice_in_dim(gs.pop("meta_tokens"), chip * 256, 256, axis=1)
    loss = gs.pop("loss").reshape(())

    grads = dict(gs, meta_tokens=g_meta, conv_w=g_conv_w, w_in=g_in_t.T.reshape(w_in.shape), w_glu=g_glu.reshape(w_glu.shape),
                 w_out=g_out.reshape(w_out.shape), w_up=g_up.reshape(w_up.shape), w_down=g_down.reshape(w_down.shape))
    delta, new_m, new_v = {}, {}, {}
    d_, m_, v_ = _adamw_whole("adamw_w_in", w_in[0].T, g_in_t, m_w_in[0].T, v_w_in[0].T)
    delta["w_in"], new_m["w_in"], new_v["w_in"] = (a.T.reshape(w_in.shape) for a in (d_, m_, v_))
    for n in ("w_glu", "w_out", "w_up", "w_down"):
        shp = given[n].shape
        two = lambda a: a.reshape(shp[1], shp[2])
        d_, m_, v_ = _adamw("adamw_" + n, two(given[n]), two(grads[n]), two(given["m_" + n]), two(given["v_" + n]), 256)
        delta[n], new_m[n], new_v[n] = d_.reshape(shp), m_.reshape(shp), v_.reshape(shp)
    for n in EARLY + LATE[:-1]:
        shp = given[n].shape
        two = (lambda a: a.reshape(1, -1)) if len(shp) == 1 else (lambda a: a)
        d_, m_, v_ = _adamw_whole("adamw_" + n, two(given[n]), two(grads[n].reshape(shp)), two(given["m_" + n]), two(given["v_" + n]))
        delta[n], new_m[n], new_v[n] = d_.reshape(shp), m_.reshape(shp), v_.reshape(shp)

    order = ["meta_tokens", "g_mix", "w_in", "conv_w", "conv_b", "dt_bias", "a_log", "d_ssd", "g_ssd", "lam_re", "lam_im", "log_step",
             "b_re", "b_im", "c_re", "c_im", "d_s5", "w_glu", "b_glu", "g_s5", "w_out", "g_mlp", "w_up", "w_down", "g_final"]
    grads_out = [grads[n].reshape(given[n].shape) for n in order]
    return (loss, grad_x, *grads_out, *[delta[n] for n in order], *[new_m[n] for n in order], *[new_v[n] for n in order])
```

```python
import math

import jax
import jax.numpy as jnp
from jax import lax
from jax.experimental import pallas as pl
from jax.experimental.pallas import tpu as pltpu

f32 = jnp.float32
bf16 = jnp.bfloat16

D = 1024
N_META = 16
CH = 256
HEADS = 16
HEAD_DIM = 64
NSTATE = 128
D_XBC = 1536
S5_P = 64
NQ = 8
PITCH = CH + 4
EPS = 1e-5
O_Z, O_U, O_XBC, O_DT, W_PROJ = 0, 1024, 2048, 3584, 3712
VMEM_LIMIT = 60 * 1024 * 1024

ADAM_LR, ADAM_B1, ADAM_B2, ADAM_EPS, ADAM_WD, ADAM_STEP = 0.001, 0.9, 0.999, 1e-08, 0.01, 10

NT = (((1,), (1,)), ((), ()))
TN = (((0,), (0,)), ((), ()))
_ANY = pl.BlockSpec(memory_space=pl.ANY)


def _cp(sem=None):
    return pltpu.CompilerParams(dimension_semantics=sem, vmem_limit_bytes=VMEM_LIMIT)


def _sigmoid(v):
    return 1.0 / (1.0 + jnp.exp(-v))


def _rsum8(v):
    r, c = v.shape
    return jnp.sum(v.reshape(r // 8, 8, c), axis=0)


def _rms(h, g):
    r = lax.rsqrt(jnp.mean(h * h, axis=-1, keepdims=True) + EPS)
    return h * r * g


def _rms_bwd(dy, h, g):
    r = lax.rsqrt(jnp.mean(h * h, axis=-1, keepdims=True) + EPS)
    n = h * r
    dn = dy * g
    dh = r * (dn - n * jnp.mean(dn * n, axis=-1, keepdims=True))
    return dh, _rsum8(dy * n)


def _dot(a, b, dims=None):
    if dims is None:
        return jnp.dot(a, b, preferred_element_type=f32)
    return lax.dot_general(a, b, dims, preferred_element_type=f32)


def _split_dot(v, m01, dims, terms, v_is_lhs=True):
    out, r = None, v
    for _ in range(terms):
        piece = r.astype(bf16)
        o = _dot(piece, m01, dims) if v_is_lhs else _dot(m01, piece, dims)
        out = o if out is None else out + o
        r = r - piece.astype(f32)
    return out


MESH = pl.DeviceIdType.MESH


def _place():
    return lax.axis_index("x"), lax.axis_index("y"), lax.axis_index("c")


def _flip(v, f):
    return 1 - v if f else v


def _call(body, comm, *, name, nsteps, in_specs, out_specs, out_shape, scratch_shapes, args):
    n_in, n_out, n_scr = len(in_specs), len(out_specs), len(scratch_shapes)
    if comm is None:
        res = pl.pallas_call(body, name=name, grid=(nsteps,), in_specs=in_specs, out_specs=out_specs, out_shape=out_shape,
                             scratch_shapes=scratch_shapes, compiler_params=_cp(("arbitrary",)))(*args)
        return list(res), []
    c_in, c_out = len(comm["ins"]), len(comm["outs"])

    def wrapped(*refs):
        o0 = n_in + c_in
        s0 = o0 + n_out + c_out
        cparts = (refs[n_in:o0], refs[o0 + n_out:s0], refs[s0 + n_scr:])

        @pl.when(pl.program_id(0) == 0)
        def _():
            comm["start"](*cparts)
        if "middle" in comm:
            @pl.when(pl.program_id(0) == nsteps // 2)
            def _():
                comm["middle"](*cparts)
        body(*refs[:n_in], *refs[o0:o0 + n_out], *refs[s0:s0 + n_scr])

        @pl.when(pl.program_id(0) == nsteps - 1)
        def _():
            comm["finish"](*cparts)

    any_spec = pl.BlockSpec(memory_space=pl.ANY)
    res = pl.pallas_call(
        wrapped, name=name, grid=(nsteps,), in_specs=list(in_specs) + [any_spec] * c_in,
        out_specs=list(out_specs) + [any_spec] * c_out, out_shape=list(out_shape) + list(comm["outs"]),
        scratch_shapes=list(scratch_shapes) + list(comm["scratch"]),
        compiler_params=_cp(("arbitrary",)))(*args, *comm["ins"])
    return list(res[:n_out]), list(res[n_out:])


def _gather_piece(slab, r0, rows):
    half = rows // 2
    flips = ((1, 0), (0, 1), (1, 1))

    def first(j, slab_ref, out_ref, send_sems, recv_sems):
        x, y, c = _place()
        return pltpu.make_async_remote_copy(
            src_ref=slab_ref.at[pl.ds(r0 + c * half, half), :], dst_ref=out_ref.at[2 * x + y, pl.ds(c * half, half), :],
            send_sem=send_sems.at[j], recv_sem=recv_sems.at[j],
            device_id=(_flip(x, flips[j][0]), _flip(y, flips[j][1]), c), device_id_type=MESH)

    def passed(j, out_ref, send_sems, recv_sems):
        x, y, c = _place()
        rows_j = out_ref.at[2 * _flip(x, flips[j][0]) + _flip(y, flips[j][1]), pl.ds(c * half, half), :]
        return pltpu.make_async_remote_copy(src_ref=rows_j, dst_ref=rows_j, send_sem=send_sems.at[3 + j],
                                            recv_sem=recv_sems.at[3 + j], device_id=(x, y, 1 - c), device_id_type=MESH)

    def start(ins, outs, scr):
        send_sems, recv_sems, stage, local_sems = scr
        x, y, _ = _place()
        load = pltpu.make_async_copy(ins[0].at[pl.ds(r0, rows), :], stage, local_sems.at[0])
        load.start()
        for j in range(3):
            first(j, ins[0], outs[0], send_sems, recv_sems).start()
        load.wait()
        pltpu.make_async_copy(stage, outs[0].at[2 * x + y], local_sems.at[1]).start()

    def middle(ins, outs, scr):
        send_sems, recv_sems, _, _ = scr
        for j in range(3):
            first(j, ins[0], outs[0], send_sems, recv_sems).wait_recv()
            passed(j, outs[0], send_sems, recv_sems).start()

    def finish(ins, outs, scr):
        send_sems, recv_sems, stage, local_sems = scr
        x, y, c = _place()
        for j in range(3):
            sib = outs[0].at[2 * _flip(x, flips[j][0]) + _flip(y, flips[j][1]), pl.ds((1 - c) * half, half), :]
            pltpu.make_async_remote_copy(src_ref=sib, dst_ref=sib, send_sem=send_sems.at[3 + j], recv_sem=recv_sems.at[3 + j],
                                         device_id=(x, y, 1 - c), device_id_type=MESH).wait_recv()
        for j in range(3):
            first(j, ins[0], outs[0], send_sems, recv_sems).wait_send()
            passed(j, outs[0], send_sems, recv_sems).wait_send()
        pltpu.make_async_copy(stage, outs[0].at[2 * x + y], local_sems.at[1]).wait()

    return dict(ins=[slab], outs=[jax.ShapeDtypeStruct((4, rows, D), bf16)],
                scratch=[pltpu.SemaphoreType.DMA((6,)), pltpu.SemaphoreType.DMA((6,)), pltpu.VMEM((rows, D), bf16),
                         pltpu.SemaphoreType.DMA((2,))], start=start, middle=middle, finish=finish)


def _scatter_piece(gpiece):
    half = gpiece.shape[1] // 2

    def copies(g_ref, land_ref, send_sems, recv_sems):
        x, y, c = _place()
        cps = []
        for fx in (0, 1):
            for fy in (0, 1):
                for fc in (0, 1):
                    k = 4 * fx + 2 * fy + fc - 1
                    if k < 0:
                        continue
                    px, py, pc = _flip(x, fx), _flip(y, fy), _flip(c, fc)
                    cps.append(pltpu.make_async_remote_copy(
                        src_ref=g_ref.at[2 * px + py, pl.ds(pc * half, half), :], dst_ref=land_ref.at[k],
                        send_sem=send_sems.at[k], recv_sem=recv_sems.at[k], device_id=(px, py, pc), device_id_type=MESH))
        return cps

    def start(ins, outs, scr):
        for cp in copies(ins[0], outs[0], *scr):
            cp.start()

    def finish(ins, outs, scr):
        for cp in copies(ins[0], outs[0], *scr):
            cp.wait()

    return dict(ins=[gpiece], outs=[jax.ShapeDtypeStruct((7, half, D), gpiece.dtype)],
                scratch=[pltpu.SemaphoreType.DMA((7,)), pltpu.SemaphoreType.DMA((7,))], start=start, finish=finish)


def _gather_blocks(block):
    rows = block.shape[0]

    def mine(out_ref):
        x, y, c = _place()
        return out_ref.at[pl.ds((4 * x + 2 * y + c) * rows, rows), :]

    def copies(b_ref, out_ref, send_sems, recv_sems):
        x, y, c = _place()
        cps = []
        for fx in (0, 1):
            for fy in (0, 1):
                for fc in (0, 1):
                    k = 4 * fx + 2 * fy + fc - 1
                    if k < 0:
                        continue
                    cps.append(pltpu.make_async_remote_copy(
                        src_ref=b_ref, dst_ref=mine(out_ref), send_sem=send_sems.at[k], recv_sem=recv_sems.at[k],
                        device_id=(_flip(x, fx), _flip(y, fy), _flip(c, fc)), device_id_type=MESH))
        return cps

    def start(ins, outs, scr):
        send_sems, recv_sems, stage, local_sems = scr
        load = pltpu.make_async_copy(ins[0], stage, local_sems.at[0])
        load.start()
        for cp in copies(ins[0], outs[0], send_sems, recv_sems):
            cp.start()
        load.wait()
        pltpu.make_async_copy(stage, mine(outs[0]), local_sems.at[1]).start()

    def finish(ins, outs, scr):
        send_sems, recv_sems, stage, local_sems = scr
        for cp in copies(ins[0], outs[0], send_sems, recv_sems):
            cp.wait()
        pltpu.make_async_copy(stage, mine(outs[0]), local_sems.at[1]).wait()

    return dict(ins=[block], outs=[jax.ShapeDtypeStruct((8 * rows, D), block.dtype)],
                scratch=[pltpu.SemaphoreType.DMA((7,)), pltpu.SemaphoreType.DMA((7,)), pltpu.VMEM((rows, D), block.dtype),
                         pltpu.SemaphoreType.DMA((2,))], start=start, finish=finish)


def _both(c1, c2):
    n = (len(c1["ins"]), len(c1["outs"]), len(c1["scratch"]))

    def split(parts):
        return [p[:k] for p, k in zip(parts, n)], [p[k:] for p, k in zip(parts, n)]

    def start(*parts):
        a, b = split(parts)
        c1["start"](*a)
        c2["start"](*b)

    def finish(*parts):
        a, b = split(parts)
        c1["finish"](*a)
        c2["finish"](*b)

    both = dict(ins=c1["ins"] + c2["ins"], outs=c1["outs"] + c2["outs"], scratch=c1["scratch"] + c2["scratch"],
                start=start, finish=finish)
    if "middle" in c1 or "middle" in c2:
        def middle(*parts):
            for cm, part in zip((c1, c2), split(parts)):
                if "middle" in cm:
                    cm["middle"](*part)
        both["middle"] = middle
    return both


def _rowwise(name, fn, nblk, rows=(), shifted=(), pars=(), refs=(), out_rows=(), out_accs=(), bm=CH, comm=None):
    n_sub = bm // CH
    n_r, n_s, n_p, n_w = len(rows), len(shifted) * n_sub, len(pars), len(refs)
    n_in = n_r + n_s + n_p + n_w
    n_o, n_a = len(out_rows), len(out_accs)

    def body(*all_refs):
        i = pl.program_id(0)
        ins = all_refs[:n_in]
        outs = all_refs[n_in:]
        rv = [r[...] for r in ins[:n_r]]
        sub = ins[n_r:n_r + n_s]
        sv = [jnp.concatenate([r[...] for r in sub[k * n_sub:(k + 1) * n_sub]], axis=0) if n_sub > 1 else sub[k][...]
              for k in range(len(shifted))]
        pv = [r[...] for r in ins[n_r + n_s:n_r + n_s + n_p]]
        ro, ao = fn(i, rv, sv, pv, list(ins[n_r + n_s + n_p:]))
        for r, v in zip(outs[:n_o], ro):
            r[...] = v.astype(r.dtype)
        accs = outs[n_o:]

        @pl.when(i == 0)
        def _():
            for r in accs:
                r[...] = jnp.zeros_like(r)
        for r, v in zip(accs, ao):
            r[...] += v

    in_specs = [pl.BlockSpec((bm, a.shape[1]), lambda i: (i, 0)) for a in rows]
    in_specs += [pl.BlockSpec((CH, a.shape[1]), lambda i, j=j: (jnp.maximum(n_sub * i - 1 + j, 0), 0))
                 for a in shifted for j in range(n_sub)]
    in_specs += [pl.BlockSpec(a.shape, lambda i, nd=a.ndim: (0,) * nd) for a in pars]
    in_specs += [spec for _, spec in refs]
    out_specs = [pl.BlockSpec((bm, c), lambda i: (i, 0)) for c, _ in out_rows]
    out_specs += [pl.BlockSpec((8, c), lambda i: (0, 0)) for c in out_accs]
    out_shape = [jax.ShapeDtypeStruct((nblk * bm, c), dt) for c, dt in out_rows]
    out_shape += [jax.ShapeDtypeStruct((8, c), f32) for c in out_accs]
    res, cres = _call(body, comm, name=name, nsteps=nblk, in_specs=in_specs, out_specs=out_specs, out_shape=out_shape,
                      scratch_shapes=[], args=[*rows, *[a for a in shifted for _ in range(n_sub)], *pars, *[a for a, _ in refs]])
    parts = (res[:n_o], res[n_o:])
    return parts if comm is None else parts + (cres,)


PACK_ROWS = 4096
HALF_ROWS = PACK_ROWS // 2
R_UP, R_DOWN, R_GLU, R_OUT, R_IN, R_SPARE = 0, 1024, 2048, 2560, 3072, 3972


C_ROWS = 928
BIG_ROWS = 768
DW_ROWS = 2816


def _contract_rows(lp, big=DW_ROWS):
    for rows in (big, BIG_ROWS):
        if lp % rows == 0:
            return rows
    return CH


def _dw_into(name, a, b, slab, ka, a_sharded, row_blk, n_s, s0, piece_rows=2048, a_square=False):
    lp = a.shape[0]
    bm = _contract_rows(lp)
    steps = lp // bm

    def body(a_ref, b_ref, *rest):
        o_ref, acc = rest[-2], rest[-1]
        k = pl.program_id(1)

        @pl.when(k == 0)
        def _():
            acc[...] = jnp.zeros_like(acc)
        a_v = a_ref[...]
        acc[...] += _dot(a_v * a_v if a_square else a_v, b_ref[...], TN)

        @pl.when(k == steps - 1)
        def _():
            o_ref[0] = acc[...].astype(bf16)

    in_specs = [pl.BlockSpec((bm, ka), (lambda s, k: (k, s)) if a_sharded else (lambda s, k: (k, 0))),
                pl.BlockSpec((bm, D), (lambda s, k: (k, 0)) if a_sharded else (lambda s, k: (k, s)))]
    args = [a, b]
    aliases = {}
    if slab is not None:
        in_specs.append(_ANY)
        args.append(slab)
        aliases = {2: 0}
    return pl.pallas_call(
        body, name=name, grid=(n_s, steps), in_specs=in_specs,
        out_specs=pl.BlockSpec((1, ka, D), lambda s, k: (s0 + s, row_blk, 0)),
        out_shape=jax.ShapeDtypeStruct((4, piece_rows, D), bf16),
        scratch_shapes=[pltpu.VMEM((ka, D), f32)], input_output_aliases=aliases,
        compiler_params=_cp(("arbitrary", "arbitrary")))(*args)


def _dw_in_t(dproj, n0):
    lp = n0.shape[0]
    bm = _contract_rows(lp, BIG_ROWS)
    steps = lp // bm
    bn = 512

    def body(a_ref, b_ref, o_ref, acc):
        k = pl.program_id(1)

        @pl.when(k == 0)
        def _():
            acc[...] = jnp.zeros_like(acc)
        acc[...] += _dot(a_ref[...], b_ref[...], TN)

        @pl.when(k == steps - 1)
        def _():
            o_ref[...] = acc[...].astype(bf16)

    return pl.pallas_call(
        body, name="dw_in", grid=(D // bn, steps),
        in_specs=[pl.BlockSpec((bm, W_PROJ), lambda j, k: (k, 0)), pl.BlockSpec((bm, bn), lambda j, k: (k, j))],
        out_specs=pl.BlockSpec((W_PROJ, bn), lambda j, k: (0, j)),
        out_shape=jax.ShapeDtypeStruct((W_PROJ, D), bf16),
        scratch_shapes=[pltpu.VMEM((W_PROJ, bn), f32)],
        compiler_params=_cp(("arbitrary", "arbitrary")))(dproj, n0)


def _head_expand():
    h = lax.broadcasted_iota(jnp.int32, (HEADS, D), 0)
    c = lax.broadcasted_iota(jnp.int32, (HEADS, D), 1)
    return jnp.where((c >> 6) == h, 1.0, 0.0).astype(bf16)


def _ssd_common(i, P, prev8, cw, cb, dtb, alog, xc=None):
    z = P[:, O_Z:O_Z + D]
    xp = P[:, O_XBC:O_XBC + D_XBC]
    dt_raw = P[:, O_DT:O_DT + HEADS]
    row = lax.broadcasted_iota(jnp.int32, (CH, 1), 0)
    if xc is None:
        row8 = lax.broadcasted_iota(jnp.int32, (8, 1), 0)
        xc = cb + cw[3:4] * xp
        for k in (1, 2, 3):
            rolled = pltpu.roll(xp, k, 0)
            fix = pltpu.roll(prev8, k, 0)
            top = jnp.where(row8 < k, fix, rolled[0:8])
            xc = xc + cw[3 - k:4 - k] * jnp.concatenate([top, rolled[8:]], axis=0)
    sg = _sigmoid(xc)
    xbc = xc * sg
    live = jnp.where(jnp.logical_or(i > 0, row >= CH - N_META), 1.0, 0.0)
    pre = dt_raw + dtb
    dt = jnp.where(pre > 20.0, pre, jnp.log(1.0 + jnp.exp(jnp.minimum(pre, 20.0)))) * live
    a = -jnp.exp(alog)
    dta = dt * a
    r_i = lax.broadcasted_iota(jnp.int32, (CH, CH), 0)
    c_i = lax.broadcasted_iota(jnp.int32, (CH, CH), 1)
    tril = r_i >= c_i
    acs = _split_dot(dta, jnp.where(tril, 1.0, 0.0).astype(bf16), None, 3, v_is_lhs=False)
    acs_t = _split_dot(dta, jnp.where(r_i <= c_i, 1.0, 0.0).astype(bf16), TN, 3)
    e = _head_expand()
    acs_e = _split_dot(acs, e, None, 3)
    dt_e = _split_dot(dt, e, None, 3)
    return dict(z=z, xp=xp, xc=xc, sg=sg, xbc=xbc, live=live, pre=pre, dt=dt, a=a, tril=tril,
                acs=acs, acs_t=acs_t, e=e, acs_e=acs_e, dt_e=dt_e)


def _lmat(c, h):
    seg = c["acs"][:, h:h + 1] - c["acs_t"][h:h + 1, :]
    return jnp.where(c["tril"], jnp.exp(jnp.minimum(seg, 0.0)), 0.0)


def _pair_masks():
    lane = lax.broadcasted_iota(jnp.int32, (1, 128), 1)
    return jnp.where(lane < HEAD_DIM, 1.0, 0.0), jnp.where(lane >= HEAD_DIM, 1.0, 0.0)


def _ssd_fwd(proj, conv_w, conv_b, dt_bias, a_log, d_ssd, g_ssd, nch, comm=None):
    def body(p_ref, cw_ref, cb_ref, dtb_ref, al_ref, d_ref, g_ref, y_ref, ys_ref, st_ref, xc_ref, prev8_ref, state_ref):
        i = pl.program_id(0)

        @pl.when(i == 0)
        def _():
            prev8_ref[...] = jnp.zeros_like(prev8_ref)
            state_ref[...] = jnp.zeros_like(state_ref)

        P = p_ref[...]
        c = _ssd_common(i, P, prev8_ref[...], cw_ref[...], cb_ref[...], dtb_ref[...], al_ref[...])
        prev8_ref[...] = c["xp"][CH - 8:CH]
        xc_ref[...] = c["xc"]
        xbc = c["xbc"]
        x = xbc[:, 0:D]
        xdt = x * c["dt_e"]
        a_last_e = c["acs_e"][CH - 1:CH, :]
        w_end = (xdt * jnp.exp(a_last_e - c["acs_e"])).astype(bf16)
        m0, m1 = _pair_masks()
        ys = []
        for g in range(2):
            bg = xbc[:, D + NSTATE * g:D + NSTATE * (g + 1)].astype(bf16)
            cg = xbc[:, D + 2 * NSTATE + NSTATE * g:D + 2 * NSTATE + NSTATE * (g + 1)].astype(bf16)
            gmat = _dot(cg, bg, NT)
            st = state_ref[g]
            st_ref[0, g] = st
            sl = slice(512 * g, 512 * (g + 1))
            y_off = _dot(cg, st.astype(bf16)) * jnp.exp(c["acs_e"][:, sl])
            contrib = _dot(bg, w_end[:, sl], TN)
            state_ref[g] = st * jnp.exp(a_last_e[:, sl]) + contrib
            yd = []
            for pr in range(4):
                h0 = 8 * g + 2 * pr
                xp2 = xdt[:, 128 * (4 * g + pr):128 * (4 * g + pr + 1)]
                ma = (gmat * _lmat(c, h0)).astype(bf16)
                mb = (gmat * _lmat(c, h0 + 1)).astype(bf16)
                yd.append(_dot(ma, (xp2 * m0).astype(bf16)) + _dot(mb, (xp2 * m1).astype(bf16)))
            ys.append(jnp.concatenate(yd, axis=1) + y_off)
        d_e = _split_dot(d_ref[...], c["e"], None, 3)
        y = jnp.concatenate(ys, axis=1) + x * d_e
        y_ref[...] = y
        yg = y * (c["z"] * _sigmoid(c["z"]))
        ys_ref[...] = _rms(yg, g_ref[...]).astype(bf16)

    full = lambda a: pl.BlockSpec(a.shape, lambda i, nd=a.ndim: (0,) * nd)
    return _call(
        body, comm, name="ssd_fwd", nsteps=nch,
        in_specs=[pl.BlockSpec((CH, W_PROJ), lambda i: (i, 0))] + [full(a) for a in (conv_w, conv_b, dt_bias, a_log, d_ssd, g_ssd)],
        out_specs=[pl.BlockSpec((CH, D), lambda i: (i, 0)), pl.BlockSpec((CH, D), lambda i: (i, 0)),
                   pl.BlockSpec((1, 2, NSTATE, 512), lambda i: (i, 0, 0, 0)), pl.BlockSpec((CH, D_XBC), lambda i: (i, 0))],
        out_shape=[jax.ShapeDtypeStruct((nch * CH, D), f32), jax.ShapeDtypeStruct((nch * CH, D), bf16),
                   jax.ShapeDtypeStruct((nch, 2, NSTATE, 512), f32), jax.ShapeDtypeStruct((nch * CH, D_XBC), f32)],
        scratch_shapes=[pltpu.VMEM((8, D_XBC), f32), pltpu.VMEM((2, NSTATE, 512), f32)],
        args=[proj, conv_w, conv_b, dt_bias, a_log, d_ssd, g_ssd])


def _ssd_bwd(proj, xc_all, y, dys, du, states, conv_w, conv_b, dt_bias, a_log, d_ssd, g_ssd, nch, comm=None):
    def body(p_ref, xc_ref, y_ref, dys_ref, du_ref, st_ref, cw_ref, cb_ref, dtb_ref, al_ref, d_ref, g_ref,
             dp_ref, dcw_ref, dcb_ref, ddtb_ref, dal_ref, dd_ref, dg_ref, nxt8_ref, dst_ref):
        step = pl.program_id(0)
        i = nch - 1 - step

        @pl.when(step == 0)
        def _():
            nxt8_ref[...] = jnp.zeros_like(nxt8_ref)
            dst_ref[...] = jnp.zeros_like(dst_ref)
            for r in (dcw_ref, dcb_ref, ddtb_ref, dal_ref, dd_ref, dg_ref):
                r[...] = jnp.zeros_like(r)

        P = p_ref[...]
        c = _ssd_common(i, P, None, cw_ref[...], cb_ref[...], dtb_ref[...], al_ref[...], xc=xc_ref[...])
        xbc, z, e = c["xbc"], c["z"], c["e"]
        x = xbc[:, 0:D]
        yv = y_ref[...]
        sz = _sigmoid(z)
        silu_z = z * sz
        dyg, dg8 = _rms_bwd(dys_ref[...], yv * silu_z, g_ref[...])
        dg_ref[...] += dg8
        dy = dyg * silu_z
        dz = dyg * yv * (sz * (1.0 + z * (1.0 - sz)))
        d_e = _split_dot(d_ref[...], e, None, 3)
        dd_ref[...] += _rsum8(dy * x)
        xdt = x * c["dt_e"]
        a_last_e = c["acs_e"][CH - 1:CH, :]
        e_end = jnp.exp(a_last_e - c["acs_e"])
        w_end = xdt * e_end
        e_acs = jnp.exp(c["acs_e"])
        dy_dec = dy * e_acs
        m0, m1 = _pair_masks()
        lane16 = lax.broadcasted_iota(jnp.int32, (1, HEADS), 1)
        row16 = lax.broadcasted_iota(jnp.int32, (HEADS, 1), 0)
        dacs = jnp.zeros((CH, HEADS), f32)
        dacs_t = jnp.zeros((HEADS, CH), f32)
        dxdt_parts, dbs, dcs, zparts, yoff_parts, dlast_parts = [], [], [], [], [], []
        for g in range(2):
            sl = slice(512 * g, 512 * (g + 1))
            bg = xbc[:, D + NSTATE * g:D + NSTATE * (g + 1)].astype(bf16)
            cg = xbc[:, D + 2 * NSTATE + NSTATE * g:D + 2 * NSTATE + NSTATE * (g + 1)].astype(bf16)
            gmat = _dot(cg, bg, NT)
            st = st_ref[0, g]
            dstn = dst_ref[g]
            dstn_b = dstn.astype(bf16)
            y_off = _dot(cg, st.astype(bf16)) * e_acs[:, sl]
            yoff_parts.append(y_off)
            bds = _dot(bg, dstn_b)
            zparts.append(w_end[:, sl] * bds)
            dlast_parts.append(jnp.sum(dstn * st, axis=0, keepdims=True) * jnp.exp(a_last_e[:, sl]))
            dg_acc = jnp.zeros((CH, CH), f32)
            dxd = []
            for pr in range(4):
                lo = 128 * (4 * g + pr)
                xp2 = xdt[:, lo:lo + 128].astype(bf16)
                dy2 = dy[:, lo:lo + 128]
                outp = jnp.zeros((CH, 128), f32)
                for hh, msk in ((0, m0), (1, m1)):
                    h = 8 * g + 2 * pr + hh
                    lm = _lmat(c, h)
                    dyh = (dy2 * msk).astype(bf16)
                    mh = (gmat * lm).astype(bf16)
                    outp = outp + _dot(mh, dyh, TN)
                    dml = _dot(dyh, xp2, NT) * lm
                    dg_acc = dg_acc + dml
                    q = dml * gmat
                    dacs = dacs + jnp.where(lane16 == h, jnp.sum(q, axis=1, keepdims=True), 0.0)
                    dacs_t = dacs_t + jnp.where(row16 == h, jnp.sum(q, axis=0, keepdims=True), 0.0)
                dxd.append(outp)
            dxdt_parts.append(jnp.concatenate(dxd, axis=1) + e_end[:, sl] * bds)
            dgb = dg_acc.astype(bf16)
            dcs.append(_dot(dgb, bg) + _dot(dy_dec[:, sl].astype(bf16), st.astype(bf16), NT))
            dbs.append(_dot(dgb, cg, TN) + _dot(w_end[:, sl].astype(bf16), dstn_b, NT))
            dst_ref[g] = dstn * jnp.exp(a_last_e[:, sl]) + _dot(cg, dy_dec[:, sl].astype(bf16), TN)
        dxdt = jnp.concatenate(dxdt_parts, axis=1)
        zfull = jnp.concatenate(zparts, axis=1)
        y_off_full = jnp.concatenate(yoff_parts, axis=1)
        dlast = jnp.concatenate(dlast_parts, axis=1)
        red = lambda v: _split_dot(v, e, NT, 2)
        eye16 = jnp.where(lax.broadcasted_iota(jnp.int32, (HEADS, HEADS), 0) == lax.broadcasted_iota(jnp.int32, (HEADS, HEADS), 1),
                          1.0, 0.0).astype(bf16)
        dacs = dacs - _split_dot(dacs_t, eye16, TN, 3)
        zred = red(zfull)
        dacs = dacs + red(dy * y_off_full) - zred
        last_term = jnp.sum(zred, axis=0, keepdims=True) + red(dlast)
        rowc = lax.broadcasted_iota(jnp.int32, (CH, 1), 0)
        dacs = dacs + jnp.where(rowc == CH - 1, last_term, 0.0)
        r_i = lax.broadcasted_iota(jnp.int32, (CH, CH), 0)
        c_i = lax.broadcasted_iota(jnp.int32, (CH, CH), 1)
        ddta = _split_dot(dacs, jnp.where(c_i >= r_i, 1.0, 0.0).astype(bf16), None, 3, v_is_lhs=False)
        ddt = ddta * c["a"] + red(dxdt * x)
        dal_ref[...] += _rsum8(ddta * c["dt"] * c["a"])
        ddt_raw = ddt * _sigmoid(c["pre"]) * c["live"]
        ddtb_ref[...] += _rsum8(ddt_raw)
        dx = dy * d_e + dxdt * c["dt_e"]
        dxbc = jnp.concatenate([dx, dbs[0], dbs[1], dcs[0], dcs[1]], axis=1)
        sg = c["sg"]
        dxc = dxbc * (sg * (1.0 + c["xc"] * (1.0 - sg)))
        dcb_ref[...] += _rsum8(dxc)
        xp = c["xp"]
        row8 = lax.broadcasted_iota(jnp.int32, (8, 1), 0)
        cw = cw_ref[...]
        dxp = cw[3:4] * dxc
        dcw = jnp.where(row8 == 3, jnp.sum(dxc * xp, axis=0, keepdims=True), 0.0)
        nxt8 = nxt8_ref[...]
        for j in (1, 2, 3):
            rolled = pltpu.roll(dxc, CH - j, 0)
            fix = pltpu.roll(nxt8, 8 - j, 0)
            bot = jnp.where(row8 >= 8 - j, fix, rolled[CH - 8:CH])
            later = jnp.concatenate([rolled[:CH - 8], bot], axis=0)
            dxp = dxp + cw[3 - j:4 - j] * later
            dcw = dcw + jnp.where(row8 == 3 - j, jnp.sum(later * xp, axis=0, keepdims=True), 0.0)
        dcw_ref[...] += dcw
        nxt8_ref[...] = dxc[0:8]
        dp_ref[:, O_Z:O_Z + D] = dz.astype(bf16)
        dp_ref[:, O_U:O_U + D] = du_ref[...].astype(bf16)
        dp_ref[:, O_XBC:O_XBC + D_XBC] = dxp.astype(bf16)
        dp_ref[:, O_DT:W_PROJ] = jnp.zeros((CH, W_PROJ - O_DT), bf16)
        dp_ref[:, O_DT:O_DT + HEADS] = ddt_raw.astype(bf16)

    full = lambda a: pl.BlockSpec(a.shape, lambda s, nd=a.ndim: (0,) * nd)
    rev = lambda s: (nch - 1 - s, 0)
    acc = lambda cdim: pl.BlockSpec((8, cdim), lambda s: (0, 0))
    return _call(
        body, comm, name="ssd_bwd", nsteps=nch,
        in_specs=[pl.BlockSpec((CH, W_PROJ), rev), pl.BlockSpec((CH, D_XBC), rev),
                  pl.BlockSpec((CH, D), rev), pl.BlockSpec((CH, D), rev), pl.BlockSpec((CH, D), rev),
                  pl.BlockSpec((1, 2, NSTATE, 512), lambda s: (nch - 1 - s, 0, 0, 0))]
        + [full(a) for a in (conv_w, conv_b, dt_bias, a_log, d_ssd, g_ssd)],
        out_specs=[pl.BlockSpec((CH, W_PROJ), rev), acc(D_XBC), acc(D_XBC), acc(HEADS), acc(HEADS), acc(D), acc(D)],
        out_shape=[jax.ShapeDtypeStruct((nch * CH, W_PROJ), bf16)]
        + [jax.ShapeDtypeStruct((8, cdim), f32) for cdim in (D_XBC, D_XBC, HEADS, HEADS, D, D)],
        scratch_shapes=[pltpu.VMEM((8, D_XBC), f32), pltpu.VMEM((2, NSTATE, 512), f32)],
        args=[proj, xc_all, y, dys, du, states, conv_w, conv_b, dt_bias, a_log, d_ssd, g_ssd])


SCAN_UNROLL = 8


def _to_slabs(slab_ref, q, mat):
    for ls in range(8):
        slab_ref[ls, pl.ds(PITCH * q, CH), :] = mat[:, 128 * ls:128 * (ls + 1)]


def _from_slabs(slab, q):
    return jnp.concatenate([slab(ls, PITCH * q) for ls in range(8)], axis=1)


def _tile(slab_ref, ls, t, lead=None):
    idx = (ls, pl.ds(t, 8, stride=PITCH), slice(None))
    return slab_ref[idx] if lead is None else slab_ref[(lead,) + idx]


def _s5_fwd(proj, bbq, ccq_t, ar, ai, d_skip, nch, comm=None):
    def body(u_ref, bb_ref, cc_ref, ar_ref, ai_ref, d_ref, s_ref, yl_ref, y5_ref, bu_ref, st_ref):
        @pl.when(pl.program_id(0) == 0)
        def _():
            st_ref[...] = jnp.zeros_like(st_ref)
        u = u_ref[...]
        ub = u.astype(bf16)
        for q in range(NQ):
            _to_slabs(bu_ref, q, _dot(ub[:, 128 * q:128 * (q + 1)], bb_ref[q]))
        ar_t = [ar_ref[:, 128 * l:128 * (l + 1)] for l in range(4)]
        ai_t = [ai_ref[:, 128 * l:128 * (l + 1)] for l in range(4)]

        def one(t, carry):
            re, im = carry
            nre, nim = [], []
            for l in range(4):
                a = ar_t[l] * re[l] - ai_t[l] * im[l] + _tile(bu_ref, l, t)
                b = ar_t[l] * im[l] + ai_t[l] * re[l] + _tile(bu_ref, l + 4, t)
                s_ref[0, l, pl.ds(t, 8, stride=PITCH), :] = a
                s_ref[0, l + 4, pl.ds(t, 8, stride=PITCH), :] = b
                nre.append(a)
                nim.append(b)
            return tuple(nre), tuple(nim)

        def step(tt, carry):
            for k in range(SCAN_UNROLL):
                carry = one(tt * SCAN_UNROLL + k, carry)
            return carry
        init = (tuple(st_ref[l] for l in range(4)), tuple(st_ref[l + 4] for l in range(4)))
        re, im = lax.fori_loop(0, CH // SCAN_UNROLL, step, init)
        for l in range(4):
            st_ref[l] = re[l]
            st_ref[l + 4] = im[l]
        ys = []
        for q in range(NQ):
            sq = _from_slabs(lambda ls, r0: s_ref[0, ls, pl.ds(r0, CH), :], q).astype(bf16)
            ys.append(_dot(sq, cc_ref[q]))
        yl = jnp.concatenate(ys, axis=1) + u * d_ref[...]
        yl_ref[...] = yl
        y5_ref[...] = (0.5 * yl * (1.0 + lax.erf(yl * (1.0 / math.sqrt(2.0))))).astype(bf16)

    const = lambda a: pl.BlockSpec(a.shape, lambda i, nd=a.ndim: (0,) * nd)
    return _call(
        body, comm, name="s5_fwd", nsteps=nch,
        in_specs=[pl.BlockSpec((CH, D), lambda i: (i, O_U // D)), const(bbq), const(ccq_t), const(ar), const(ai), const(d_skip)],
        out_specs=[pl.BlockSpec((1, 8, 8 * PITCH, 128), lambda i: (i, 0, 0, 0)),
                   pl.BlockSpec((CH, D), lambda i: (i, 0)), pl.BlockSpec((CH, D), lambda i: (i, 0))],
        out_shape=[jax.ShapeDtypeStruct((nch, 8, 8 * PITCH, 128), f32), jax.ShapeDtypeStruct((nch * CH, D), f32),
                   jax.ShapeDtypeStruct((nch * CH, D), bf16)],
        scratch_shapes=[pltpu.VMEM((8, 8 * PITCH, 128), f32), pltpu.VMEM((8, 8, 128), f32)],
        args=[proj, bbq, ccq_t, ar, ai, d_skip])


def _s5_bwd(proj, dyl, s_all, bbtq, cctq, ar, ai, d_skip, nch, comm=None):
    def body(u_ref, dy_ref, s_ref, bbt_ref, cct_ref, ar_ref, ai_ref, d_ref,
             du_ref, dcc_ref, dbb_ref, dab_ref, dd_ref, ga_ref, st_ref):
        @pl.when(pl.program_id(0) == 0)
        def _():
            st_ref[...] = jnp.zeros_like(st_ref)
            for r in (dcc_ref, dbb_ref, dab_ref, dd_ref):
                r[...] = jnp.zeros_like(r)
        u = u_ref[...]
        dyl_v = dy_ref[...]
        dd_ref[...] += _rsum8(dyl_v * u)
        ub = u.astype(bf16)
        dyb = dyl_v.astype(bf16)
        for q in range(NQ):
            _to_slabs(ga_ref, q, _dot(dyb[:, 128 * q:128 * (q + 1)], cct_ref[q]))
        ar_t = [ar_ref[:, 128 * l:128 * (l + 1)] for l in range(4)]
        ai_t = [ai_ref[:, 128 * l:128 * (l + 1)] for l in range(4)]

        def one(t, carry):
            re, im, dar, dai = carry
            nre, nim, ndar, ndai = [], [], [], []
            for l in range(4):
                sre = _tile(s_ref, l, t, lead=0)
                sim = _tile(s_ref, l + 4, t, lead=0)
                ndar.append(dar[l] + re[l] * sre + im[l] * sim)
                ndai.append(dai[l] + im[l] * sre - re[l] * sim)
                a = _tile(ga_ref, l, t) + ar_t[l] * re[l] + ai_t[l] * im[l]
                b = _tile(ga_ref, l + 4, t) - ai_t[l] * re[l] + ar_t[l] * im[l]
                ga_ref[l, pl.ds(t, 8, stride=PITCH), :] = a
                ga_ref[l + 4, pl.ds(t, 8, stride=PITCH), :] = b
                nre.append(a)
                nim.append(b)
            return tuple(nre), tuple(nim), tuple(ndar), tuple(ndai)

        def step(tt, carry):
            for k in range(SCAN_UNROLL):
                carry = one(CH - 1 - (tt * SCAN_UNROLL + k), carry)
            return carry
        four = lambda ref, o: tuple(ref[l + o] for l in range(4))
        re, im, dar, dai = lax.fori_loop(0, CH // SCAN_UNROLL, step,
                                         (four(st_ref, 0), four(st_ref, 4), four(dab_ref, 0), four(dab_ref, 4)))
        for l in range(4):
            st_ref[l], st_ref[l + 4] = re[l], im[l]
            dab_ref[l], dab_ref[l + 4] = dar[l], dai[l]
        dus = []
        for q in range(NQ):
            aq = _from_slabs(lambda ls, r0: ga_ref[ls, pl.ds(r0, CH), :], q).astype(bf16)
            sq = _from_slabs(lambda ls, r0: s_ref[0, ls, pl.ds(r0, CH), :], q).astype(bf16)
            dcc_ref[q] += _dot(dyb[:, 128 * q:128 * (q + 1)], sq, TN)
            dbb_ref[q] += _dot(ub[:, 128 * q:128 * (q + 1)], aq, TN)
            dus.append(_dot(aq, bbt_ref[q]))
        du_ref[...] = jnp.concatenate(dus, axis=1) + dyl_v * d_ref[...]

    const = lambda a: pl.BlockSpec(a.shape, lambda s, nd=a.ndim: (0,) * nd)
    rev = lambda s: (nch - 1 - s, 0)
    return _call(
        body, comm, name="s5_bwd", nsteps=nch,
        in_specs=[pl.BlockSpec((CH, D), lambda s: (nch - 1 - s, O_U // D)), pl.BlockSpec((CH, D), rev),
                  pl.BlockSpec((1, 8, 8 * PITCH, 128), lambda s: (nch - 1 - s, 0, 0, 0)),
                  const(bbtq), const(cctq), const(ar), const(ai), const(d_skip)],
        out_specs=[pl.BlockSpec((CH, D), rev), pl.BlockSpec((NQ, 128, D), lambda s: (0, 0, 0)),
                   pl.BlockSpec((NQ, 128, D), lambda s: (0, 0, 0)), pl.BlockSpec((8, 8, 128), lambda s: (0, 0, 0)),
                   pl.BlockSpec((8, D), lambda s: (0, 0))],
        out_shape=[jax.ShapeDtypeStruct((nch * CH, D), f32), jax.ShapeDtypeStruct((NQ, 128, D), f32),
                   jax.ShapeDtypeStruct((NQ, 128, D), f32), jax.ShapeDtypeStruct((8, 8, 128), f32),
                   jax.ShapeDtypeStruct((8, D), f32)],
        scratch_shapes=[pltpu.VMEM((8, 8 * PITCH, 128), f32), pltpu.VMEM((8, 8, 128), f32)],
        args=[proj, dyl, s_all, bbtq, cctq, ar, ai, d_skip])


def _s5_tables(lam_re, lam_im, log_step, b_re, b_im):
    step = jnp.exp(log_step)[:, None]
    mag = jnp.exp(lam_re * step)
    ab_re = mag * jnp.cos(lam_im * step)
    ab_im = mag * jnp.sin(lam_im * step)
    den = lam_re * lam_re + lam_im * lam_im
    coef_re = ((ab_re - 1.0) * lam_re + ab_im * lam_im) / den
    coef_im = (ab_im * lam_re - (ab_re - 1.0) * lam_im) / den
    bb_re = coef_re[..., None] * b_re - coef_im[..., None] * b_im
    bb_im = coef_re[..., None] * b_im + coef_im[..., None] * b_re
    return ab_re, ab_im, bb_re, bb_im


def _blockdiag_in(m_re, m_im):
    eye = jnp.eye(8, dtype=f32)

    def one(m):
        m = m.reshape(NQ, 8, S5_P, 16)
        return jnp.einsum("qgph,gk->qghkp", m, eye).reshape(NQ, 128, 512)
    return jnp.concatenate([one(m_re), one(m_im)], axis=2)


def _blockdiag_in_grad(dm):
    def one(x):
        x = x.reshape(NQ, 8, 16, 8, S5_P)
        return jnp.einsum("qghgp->qgph", x).reshape(NQ * 8, S5_P, 16)
    return one(dm[:, :, :512]), one(dm[:, :, 512:])


def _local_step(x2, tgt2, meta, p, w_in_t, slab):
    seq = x2.shape[0]
    nch = 1 + seq // CH
    bmb = BIG_ROWS if (nch * CH) % BIG_ROWS == 0 else CH
    nbig = nch * CH // bmb
    metablk = jnp.concatenate([jnp.zeros((CH - N_META, D), f32), meta, jnp.zeros((bmb - CH, D), f32)], axis=0)
    w_full = (w_in_t, pl.BlockSpec(w_in_t.shape, lambda i: (0, 0), pipeline_mode=pl.Buffered(1)))

    def lead(i, v):
        return jnp.logical_and(i == 0, lax.broadcasted_iota(jnp.int32, (v.shape[0], 1), 0) < CH)
    h0_of = lambda i, s, q: jnp.where(lead(i, s[0]), q[0][:s[0].shape[0]], s[0])

    def in_fn(i, r, s, q, w):
        nb = _rms(h0_of(i, s, q), q[1]).astype(bf16)
        return [_dot(nb, w[0][...], NT), nb], []
    (proj, n0), _, (g_up,) = _rowwise("in_proj", in_fn, nbig, shifted=[x2], pars=[metablk, p["g_mix"]], refs=[w_full],
                                      out_rows=[(W_PROJ, f32), (D, bf16)], bm=bmb, comm=_gather_piece(slab, R_UP, 1024))
    (y, y_ssd, states, xc_all), (g_down,) = _ssd_fwd(proj, p["conv_w"], p["conv_b"], p["dt_bias"], p["a_log"], p["d_ssd"],
                                                     p["g_ssd"], nch, comm=_gather_piece(slab, R_DOWN, 1024))

    ab_re, ab_im, bb_re, bb_im = _s5_tables(p["lam_re"], p["lam_im"], p["log_step"], p["b_re"], p["b_im"])
    ar, ai = ab_re.reshape(NQ, 512), ab_im.reshape(NQ, 512)
    bbq = _blockdiag_in(bb_re, bb_im)
    ccq = _blockdiag_in(jnp.swapaxes(p["c_re"], 1, 2), -jnp.swapaxes(p["c_im"], 1, 2))
    d_skip = p["d_s5"].reshape(1, D)
    (s_all, ylin, y5), (g_go,) = _s5_fwd(proj, bbq.astype(bf16), jnp.swapaxes(ccq, 1, 2).astype(bf16), ar, ai, d_skip, nch,
                                         comm=_gather_piece(slab, R_GLU, 1024))
    whole = lambda a: (a, pl.BlockSpec(a.shape, lambda i: (0, 0, 0), pipeline_mode=pl.Buffered(1)))
    w_up, w_down = whole(g_up), whole(g_down)
    w_glu_t = (g_go, pl.BlockSpec((4, 512, D), lambda i: (0, 0, 0), pipeline_mode=pl.Buffered(1)))
    w_out = (g_go, pl.BlockSpec((4, 512, D), lambda i: (0, 1, 0), pipeline_mode=pl.Buffered(1)))

    def glu_fn(i, r, s, q, w):
        v = jnp.concatenate([_dot(r[0], w[0][k], NT) for k in range(4)], axis=1) + q[0]
        return [v, _rms(v[:, :D] * _sigmoid(v[:, D:]), q[1])], []
    (v, y_s5), _ = _rowwise("glu", glu_fn, nbig, rows=[y5], pars=[p["b_glu"], p["g_s5"]], refs=[w_glu_t],
                            out_rows=[(2 * D, bf16), (D, bf16)], bm=bmb)

    def out_fn(i, r, s, q, w):
        acc = (_dot(r[0][:, :512], w[0][0]) + _dot(r[0][:, 512:], w[0][1]) + _dot(r[1][:, :512], w[0][2])
               + _dot(r[1][:, 512:], w[0][3]))
        return [h0_of(i, s, q) + acc], []
    (h1,), _ = _rowwise("out_proj", out_fn, nbig, rows=[y_ssd, y_s5], shifted=[x2], pars=[metablk], refs=[w_out],
                        out_rows=[(D, f32)], bm=bmb)

    def up_fn(i, r, s, q, w):
        nb = _rms(r[0], q[0]).astype(bf16)
        return [jnp.concatenate([jnp.maximum(_dot(nb, w[0][k]), 0.0).astype(bf16) for k in range(4)], axis=1), nb], []
    (relu_m, n1), _ = _rowwise("up_proj", up_fn, nbig, rows=[h1], pars=[p["g_mlp"]], refs=[w_up],
                               out_rows=[(4 * D, bf16), (D, bf16)], bm=bmb)

    def down_fn(i, r, s, q, w):
        acc = None
        for k in range(4):
            t = r[0][:, D * k:D * (k + 1)]
            part = _dot(t * t, w[0][k])
            acc = part if acc is None else acc + part
        return [r[1] + acc], []
    (h2,), _ = _rowwise("down_proj", down_fn, nbig, rows=[relu_m, h1], refs=[w_down], out_rows=[(D, f32)], bm=bmb)

    def final_fn(i, r, s, q, w):
        err = jnp.where(lead(i, r[0]), 0.0, _rms(r[0], q[0]) - s[0])
        dh, dg8 = _rms_bwd(err * (1.0 / D), r[0], q[0])
        return [dh, dh], [_rsum8(err * err), dg8]
    (dh2, dh2_b), (loss8, dgf8) = _rowwise("final", final_fn, nbig, rows=[h2], shifted=[tgt2], pars=[p["g_final"]],
                                           out_rows=[(D, f32), (D, bf16)], out_accs=[D, D], bm=bmb)
    loss = 0.5 / D * jnp.sum(loss8)

    def down_bwd_fn(i, r, s, q, w):
        dm_ = [_dot(r[0], w[0][k], NT) * (2.0 * r[1][:, D * k:D * (k + 1)].astype(f32)) for k in range(4)]
        return [jnp.concatenate(dm_, axis=1)], []
    (dm,), _ = _rowwise("down_bwd", down_bwd_fn, nbig, rows=[dh2_b, relu_m], refs=[w_down], out_rows=[(4 * D, bf16)], bm=bmb)
    g_a = _dw_into("dw_down", relu_m, dh2_b, None, 1024, True, 1, 4, 0, piece_rows=2048, a_square=True)

    def up_bwd_fn(i, r, s, q, w):
        acc = _dot(r[0][:, :D], w[0][0], NT)
        for k in range(1, 4):
            acc = acc + _dot(r[0][:, D * k:D * (k + 1)], w[0][k], NT)
        dh, dg8 = _rms_bwd(acc, r[1], q[0])
        dh1_ = r[2] + dh
        return [dh1_, dh1_], [dg8]
    (dh1, dh1_b), (dgmlp8,) = _rowwise("up_bwd", up_bwd_fn, nbig, rows=[dm, h1, dh2], pars=[p["g_mlp"]], refs=[w_up],
                                       out_rows=[(D, f32), (D, bf16)], out_accs=[D], bm=bmb)
    g_a = _dw_into("dw_up", n1, dm, g_a, 1024, False, 0, 4, 0, piece_rows=2048)

    def out_bwd_fn(i, r, s, q, w):
        dmix = [_dot(r[0], w[0][k], NT) for k in range(4)]
        v1, v2 = r[1][:, :D].astype(f32), r[1][:, D:].astype(f32)
        s2 = _sigmoid(v2)
        dglu, dg8 = _rms_bwd(jnp.concatenate(dmix[2:], axis=1), v1 * s2, q[0])
        dv = jnp.concatenate([dglu * s2, dglu * v1 * s2 * (1.0 - s2)], axis=1)
        return [jnp.concatenate(dmix[:2], axis=1), dv], [dg8, _rsum8(dv)]
    (dys, dv), (dgs58, dbglu8) = _rowwise("out_bwd", out_bwd_fn, nbig, rows=[dh1_b, v], pars=[p["g_s5"]], refs=[w_out],
                                          out_rows=[(D, f32), (2 * D, bf16)], out_accs=[D, 2 * D], bm=bmb)
    g_b = _dw_into("dw_out_a", y_ssd, dh1_b, None, 512, True, 1, 2, 0, piece_rows=1024)
    g_b = _dw_into("dw_out_b", y_s5, dh1_b, g_b, 512, True, 1, 2, 2, piece_rows=1024)

    def glu_bwd_fn(i, r, s, q, w):
        acc = _dot(r[0][:, :512], w[0][0])
        for k in range(1, 4):
            acc = acc + _dot(r[0][:, 512 * k:512 * (k + 1)], w[0][k])
        yl = r[1]
        cdf = 0.5 * (1.0 + lax.erf(yl * (1.0 / math.sqrt(2.0))))
        pdf = jnp.exp(-0.5 * yl * yl) * (1.0 / math.sqrt(2.0 * math.pi))
        return [acc * (cdf + yl * pdf)], []
    (dylin,), _ = _rowwise("glu_bwd", glu_bwd_fn, nbig, rows=[dv, ylin], refs=[w_glu_t], out_rows=[(D, f32)], bm=bmb)
    g_b = _dw_into("dw_glu", dv, y5, g_b, 512, True, 0, 4, 0, piece_rows=1024)

    (du, dcc, dbb, dab, dds5), (land_a,) = _s5_bwd(proj, dylin, s_all, jnp.swapaxes(bbq, 1, 2).astype(bf16), ccq.astype(bf16),
                                                   ar, ai, d_skip, nch, comm=_scatter_piece(g_a))

    s8 = lambda a: jnp.sum(a, axis=0, keepdims=True)
    dab_q = jnp.swapaxes(dab.reshape(2, 4, NQ, 128), 1, 2).reshape(2, NQ * 8, S5_P)
    dbb_re, dbb_im = _blockdiag_in_grad(dbb)
    dcr, dci = _blockdiag_in_grad(dcc)
    _, vjp = jax.vjp(_s5_tables, p["lam_re"], p["lam_im"], p["log_step"], p["b_re"], p["b_im"])
    dlam_re, dlam_im, dlog_step, db_re, db_im = vjp((dab_q[0], dab_q[1], dbb_re, dbb_im))
    early = dict(lam_re=dlam_re, lam_im=dlam_im, log_step=dlog_step, b_re=db_re, b_im=db_im, c_re=jnp.swapaxes(dcr, 1, 2),
                 c_im=-jnp.swapaxes(dci, 1, 2), d_s5=s8(dds5).reshape(NQ * 8, 16), b_glu=s8(dbglu8), g_s5=s8(dgs58),
                 g_mlp=s8(dgmlp8), g_final=s8(dgf8).reshape(D))
    early_pack = _pack_small([early[n] for n in EARLY], _rows_for(EARLY))

    (dproj, dcw8, dcb8, ddtb8, dal8, dd8, dgssd8), (land_b, all_early) = _ssd_bwd(
        proj, xc_all, y, dys, du, states, p["conv_w"], p["conv_b"], p["dt_bias"], p["a_log"], p["d_ssd"], p["g_ssd"], nch,
        comm=_both(_scatter_piece(g_b), _gather_blocks(early_pack)))

    gt = _dw_in_t(dproj, n0)
    gt = jnp.concatenate([gt[0:1024], gt[O_XBC:O_XBC + D_XBC], gt[O_DT:O_DT + HEADS], gt[O_U:O_U + D]], axis=0).reshape(4, 900, D)
    g_c = jnp.concatenate([gt, jnp.zeros((4, C_ROWS - 900, D), bf16)], axis=1)

    def in_bwd_fn(i, r, s, q, w):
        dh, dg8 = _rms_bwd(_dot(r[0], w[0][...]), h0_of(i, s, q), q[1])
        return [r[1] + dh], [dg8]
    (dh0,), (dgmix8,), (land_c,) = _rowwise(
        "in_bwd", in_bwd_fn, nbig, rows=[dproj, dh1], shifted=[x2], pars=[metablk, p["g_mix"]], refs=[w_full],
        out_rows=[(D, f32)], out_accs=[D], bm=bmb, comm=_scatter_piece(g_c))

    hsum = lambda a: jnp.sum(s8(a).reshape(HEADS, HEAD_DIM), axis=1).reshape(1, HEADS)
    late = dict(g_mix=s8(dgmix8), conv_b=s8(dcb8), dt_bias=s8(ddtb8), a_log=s8(dal8), d_ssd=hsum(dd8), g_ssd=s8(dgssd8),
                conv_w=dcw8[0:4], meta_tokens=dh0[CH - N_META:CH], loss=loss.reshape(1))
    return dh0, [(g_a, land_a), (g_b, land_b), (g_c, land_c)], all_early, late


def _perm_rows_w_in(wt):
    return jnp.concatenate([wt[0:1024], wt[2576:3600], wt[1024:2560], wt[2560:2576],
                            jnp.zeros((W_PROJ - 3600, wt.shape[1]), wt.dtype)], axis=0)


def _unperm_cols_w_in(g):
    return jnp.concatenate([g[:, 0:1024], g[:, O_XBC:O_XBC + D_XBC], g[:, O_DT:O_DT + HEADS], g[:, O_U:O_U + D]], axis=1)


def _pack_shard(w_in, w_glu, w_out, w_up, w_down, spare):
    dt = w_in.dtype
    parts = [w_up, w_down, w_glu.T, w_out, w_in.T, spare,
             jnp.zeros((PACK_ROWS - R_SPARE - spare.shape[0], D), dt)]
    return jnp.concatenate(parts, axis=0)


def _allgather8(x_shard, name):
    m_per, n = x_shard.shape

    def body(x_ref, out_ref, send_sems, recv_sems, stage, local_sems):
        x, y, c = _place()
        me, sibling = (x, y, c), (x, y, 1 - c)
        chips = [(1 - x, y), (x, 1 - y), (1 - x, 1 - y)]

        def rows(px, py, pc):
            return out_ref.at[pl.ds((4 * px + 2 * py + pc) * m_per, m_per), :]

        def copy(k, block, to, src=None):
            return pltpu.make_async_remote_copy(
                src_ref=rows(*block) if src is None else src, dst_ref=rows(*block),
                send_sem=send_sems.at[k], recv_sem=recv_sems.at[k], device_id=to, device_id_type=MESH)

        load = pltpu.make_async_copy(x_ref, stage, local_sems.at[0])
        load.start()
        first = [copy(0, me, sibling, src=x_ref)]
        first += [copy(1 + j, me, (*chip, c), src=x_ref) for j, chip in enumerate(chips)]
        for cp in first:
            cp.start()
        load.wait()
        store = pltpu.make_async_copy(stage, rows(*me), local_sems.at[1])
        store.start()
        passed = [copy(4 + j, (*chip, c), sibling) for j, chip in enumerate(chips)]
        for j, chip in enumerate(chips):
            copy(1 + j, (*chip, c), me).wait_recv()
            passed[j].start()
        copy(0, sibling, me).wait_recv()
        for j, chip in enumerate(chips):
            copy(4 + j, (*chip, 1 - c), me).wait_recv()
        for cp in first + passed:
            cp.wait_send()
        store.wait()

    return pl.pallas_call(
        body, name=name, out_shape=jax.ShapeDtypeStruct((8 * m_per, n), x_shard.dtype),
        in_specs=[_ANY], out_specs=_ANY,
        scratch_shapes=[pltpu.SemaphoreType.DMA((7,)), pltpu.SemaphoreType.DMA((7,)), pltpu.VMEM((m_per, n), x_shard.dtype),
                        pltpu.SemaphoreType.DMA((2,))])(x_shard)


def _swap_sibling(r, name):
    def body(r_ref, out_ref, send_sem, recv_sem):
        x, y, c = _place()
        cp = pltpu.make_async_remote_copy(src_ref=r_ref, dst_ref=out_ref, send_sem=send_sem, recv_sem=recv_sem,
                                          device_id=(x, y, 1 - c), device_id_type=MESH)
        cp.start()
        cp.wait()

    return pl.pallas_call(
        body, name=name, out_shape=jax.ShapeDtypeStruct(r.shape, r.dtype), in_specs=[_ANY], out_specs=_ANY,
        scratch_shapes=[pltpu.SemaphoreType.DMA, pltpu.SemaphoreType.DMA])(r)


SH_CONVW, SH_META = 4 * 384, 16 * 256
SPARE_ROWS = 17

SMALL_SHAPES = dict(
    g_mix=(1, 1024), conv_b=(1, 1536), dt_bias=(1, 16), a_log=(1, 16), d_ssd=(1, 16), g_ssd=(1, 1024), lam_re=(1, 64, 64),
    lam_im=(1, 64, 64), log_step=(1, 64), b_re=(1, 64, 64, 16), b_im=(1, 64, 64, 16), c_re=(1, 64, 16, 64), c_im=(1, 64, 16, 64),
    d_s5=(1, 64, 16), b_glu=(1, 2048), g_s5=(1, 1024), g_mlp=(1, 1024), g_final=(1024,),
    conv_w=(4, D_XBC), meta_tokens=(N_META, D), loss=(1,))
EARLY = ["lam_re", "lam_im", "log_step", "b_re", "b_im", "c_re", "c_im", "d_s5", "b_glu", "g_s5", "g_mlp", "g_final"]
LATE = ["g_mix", "conv_b", "dt_bias", "a_log", "d_ssd", "g_ssd", "conv_w", "meta_tokens", "loss"]


def _rows_for(names):
    return -(-sum(math.prod(SMALL_SHAPES[n]) for n in names) // (8 * D)) * 8


def _pack_small(arrs, rows):
    flat = jnp.concatenate([a.reshape(-1).astype(f32) for a in arrs])
    return jnp.concatenate([flat, jnp.zeros((rows * D - flat.shape[0],), f32)]).reshape(rows, D)


def _unpack_small(slab, shapes):
    flat = slab.reshape(-1)
    out, o = [], 0
    for shp in shapes:
        n = math.prod(shp)
        out.append(flat[o:o + n].reshape(shp))
        o += n
    return out


def _sum8(g, rows, name):
    def body(g_ref, o_ref):
        acc = g_ref[0]
        for k in range(1, 8):
            acc = acc + g_ref[k]
        o_ref[...] = acc
    return pl.pallas_call(body, name=name, out_shape=jax.ShapeDtypeStruct((rows, D), f32),
                          compiler_params=_cp())(g.reshape(8, rows, D))


def _adam_math(w_, g_, m_, v_):
    m2 = ADAM_B1 * m_ + (1.0 - ADAM_B1) * g_
    v2 = ADAM_B2 * v_ + (1.0 - ADAM_B2) * jnp.square(g_)
    m_hat = m2 / (1.0 - ADAM_B1 ** ADAM_STEP)
    v_hat = v2 / (1.0 - ADAM_B2 ** ADAM_STEP)
    delta = -ADAM_LR * (m_hat / (jnp.sqrt(v_hat) + ADAM_EPS) + ADAM_WD * w_)
    return delta, m2, v2


def _adamw(name, w, g, m, v, bm):
    def fn(i, r, s, q, refs):
        return list(_adam_math(*r)), []
    c = w.shape[1]
    (d, m2, v2), _ = _rowwise(name, fn, w.shape[0] // bm, rows=[w, g, m, v], out_rows=[(c, f32)] * 3, bm=bm)
    return d, m2, v2


def _adamw_whole(name, w, g, m, v):
    def body(w_ref, g_ref, m_ref, v_ref, d_ref, m2_ref, v2_ref):
        d_ref[...], m2_ref[...], v2_ref[...] = _adam_math(w_ref[...], g_ref[...], m_ref[...], v_ref[...])
    return pl.pallas_call(body, name=name, out_shape=[jax.ShapeDtypeStruct(w.shape, f32)] * 3, compiler_params=_cp())(w, g, m, v)


def _sum_parts(name, own, land):
    def fn(i, r, s, q, refs):
        acc = r[0].astype(f32)
        for k in range(7):
            acc = acc + refs[0][k].astype(f32)
        return [acc], []
    rows = own.shape[0]
    bm = CH if rows % CH == 0 else rows
    (o,), _ = _rowwise(name, fn, rows // bm, rows=[own], refs=[(land, pl.BlockSpec((7, bm, D), lambda i: (0, i, 0)))],
                       out_rows=[(D, f32)], bm=bm)
    return o


def kernel(x, meta_tokens, g_mix, w_in, conv_w, conv_b, dt_bias, a_log, d_ssd, g_ssd, lam_re, lam_im, log_step, b_re, b_im, c_re, c_im, d_s5, w_glu, b_glu, g_s5, w_out, g_mlp, w_up, w_down, g_final, loss_target, m_meta_tokens, m_g_mix, m_w_in, m_conv_w, m_conv_b, m_dt_bias, m_a_log, m_d_ssd, m_g_ssd, m_lam_re, m_lam_im, m_log_step, m_b_re, m_b_im, m_c_re, m_c_im, m_d_s5, m_w_glu, m_b_glu, m_g_s5, m_w_out, m_g_mlp, m_w_up, m_w_down, m_g_final, v_meta_tokens, v_g_mix, v_w_in, v_conv_w, v_conv_b, v_dt_bias, v_a_log, v_d_ssd, v_g_ssd, v_lam_re, v_lam_im, v_log_step, v_b_re, v_b_im, v_c_re, v_c_im, v_d_s5, v_w_glu, v_b_glu, v_g_s5, v_w_out, v_g_mlp, v_w_up, v_w_down, v_g_final):
    given = dict(locals())
    cx, cy, cc = _place()
    chip = 2 * cx + cy

    small_f = jnp.concatenate([conv_w.reshape(-1), meta_tokens.reshape(-1)])
    t_hi = small_f.astype(bf16)
    r_1 = small_f - t_hi.astype(f32)
    t_mid = r_1.astype(bf16)
    t_lo = (r_1 - t_mid.astype(f32)).astype(bf16)
    terms = jnp.concatenate([t_hi, t_mid, t_lo])
    spare = jnp.concatenate([terms, jnp.zeros((SPARE_ROWS * D - terms.shape[0],), bf16)]).reshape(SPARE_ROWS, D)
    slab = _pack_shard(w_in[0].astype(bf16), w_glu[0].astype(bf16), w_out[0].astype(bf16), w_up[0].astype(bf16),
                       w_down[0].astype(bf16), spare)
    my_half = lax.dynamic_slice_in_dim(slab, R_IN + cc * 512, 512, axis=0)
    gathered = _allgather8(my_half, "gather_w_in").reshape(4, 1024, D)
    w_in_t = _perm_rows_w_in(jnp.concatenate([gathered[s, 0:900] for s in range(4)], axis=0))
    n_sf = SH_CONVW + SH_META
    tr = gathered[:, 900:900 + SPARE_ROWS].reshape(4, SPARE_ROWS * D)[:, :3 * n_sf].astype(f32).reshape(4, 3, n_sf)
    sp = tr[:, 0] + tr[:, 1] + tr[:, 2]
    conv_w_full = jnp.concatenate([sp[s, :SH_CONVW].reshape(4, 384) for s in range(4)], axis=1)
    meta_full = jnp.concatenate([sp[s, SH_CONVW:].reshape(16, 256) for s in range(4)], axis=1)

    p = dict(g_mix=g_mix, conv_w=conv_w_full, conv_b=conv_b, dt_bias=dt_bias, a_log=a_log, d_ssd=d_ssd, g_ssd=g_ssd,
             lam_re=lam_re[0], lam_im=lam_im[0], log_step=log_step[0], b_re=b_re[0], b_im=b_im[0], c_re=c_re[0], c_im=c_im[0],
             d_s5=d_s5[0], b_glu=b_glu, g_s5=g_s5, g_mlp=g_mlp, g_final=g_final.reshape(1, D))
    dh0, pieces, all_early, late = _local_step(x[0], loss_target[0], meta_full, p, w_in_t, slab)
    grad_x = dh0[CH:].reshape(x.shape)

    reds = []
    for k, (gp, land) in enumerate(pieces):
        half = gp.shape[1] // 2
        own = lax.dynamic_slice(gp, (chip, cc * half, 0), (1, half, D)).reshape(half, D)
        reds.append(_sum_parts("rs_sum_%d" % k, own, land))
    red = jnp.concatenate(reds, axis=0)
    other = _swap_sibling(red, "rs_share")
    first = jnp.where(cc == 0, red, other)
    second = jnp.where(cc == 0, other, red)
    g_up, g_down = first[0:1024], second[0:1024]
    g_glu, g_out = first[1024:1536].T, second[1024:1536]
    hc = C_ROWS // 2
    g_in_t = jnp.concatenate([first[1536:1536 + hc], second[1536:1536 + 900 - hc]], axis=0)

    gs = dict(zip(EARLY, _unpack_small(_sum8(all_early, _rows_for(EARLY), "sum8_early"), [SMALL_SHAPES[n] for n in EARLY])))
    all_late = _allgather8(_pack_small([late[n] for n in LATE], _rows_for(LATE)), "gather_small")
    gs.update(zip(LATE, _unpack_small(_sum8(all_late, _rows_for(LATE), "sum8_late"), [SMALL_SHAPES[n] for n in LATE])))
    g_conv_w = lax.dynamic_slice_in_dim(gs.pop("conv_w"), chip * 384, 384, axis=1).reshape(conv_w.shape)
    g_meta = lax.dynamic_slice_in_dim(gs.pop("meta_tokens"), chip * 256, 256, axis=1)
    loss = gs.pop("loss").reshape(())

    grads = dict(gs, meta_tokens=g_meta, conv_w=g_conv_w, w_in=g_in_t.T.reshape(w_in.shape), w_glu=g_glu.reshape(w_glu.shape),
                 w_out=g_out.reshape(w_out.shape), w_up=g_up.reshape(w_up.shape), w_down=g_down.reshape(w_down.shape))
    delta, new_m, new_v = {}, {}, {}
    d_, m_, v_ = _adamw_whole("adamw_w_in", w_in[0].T, g_in_t, m_w_in[0].T, v_w_in[0].T)
    delta["w_in"], new_m["w_in"], new_v["w_in"] = (a.T.reshape(w_in.shape) for a in (d_, m_, v_))
    for n in ("w_glu", "w_out", "w_up", "w_down"):
        shp = given[n].shape
        two = lambda a: a.reshape(shp[1], shp[2])
        d_, m_, v_ = _adamw("adamw_" + n, two(given[n]), two(grads[n]), two(given["m_" + n]), two(given["v_" + n]), 256)
        delta[n], new_m[n], new_v[n] = d_.reshape(shp), m_.reshape(shp), v_.reshape(shp)
    for n in EARLY + LATE[:-1]:
        shp = given[n].shape
        two = (lambda a: a.reshape(1, -1)) if len(shp) == 1 else (lambda a: a)
        d_, m_, v_ = _adamw_whole("adamw_" + n, two(given[n]), two(grads[n].reshape(shp)), two(given["m_" + n]), two(given["v_" + n]))
        delta[n], new_m[n], new_v[n] = d_.reshape(shp), m_.reshape(shp), v_.reshape(shp)

    order = ["meta_tokens", "g_mix", "w_in", "conv_w", "conv_b", "dt_bias", "a_log", "d_ssd", "g_ssd", "lam_re", "lam_im", "log_step",
             "b_re", "b_im", "c_re", "c_im", "d_s5", "w_glu", "b_glu", "g_s5", "w_out", "g_mlp", "w_up", "w_down", "g_final"]
    grads_out = [grads[n].reshape(given[n].shape) for n in order]
    return (loss, grad_x, *grads_out, *[delta[n] for n in order], *[new_m[n] for n in order], *[new_v[n] for n in order])
```

```python
import math

import jax
import jax.numpy as jnp
from jax import lax
from jax.experimental import pallas as pl
from jax.experimental.pallas import tpu as pltpu

f32 = jnp.float32
bf16 = jnp.bfloat16

D = 1024
N_META = 16
CH = 256
HEADS = 16
HEAD_DIM = 64
NSTATE = 128
D_XBC = 1536
S5_P = 64
NQ = 8
PITCH = CH + 4
EPS = 1e-5
O_Z, O_U, O_XBC, O_DT, W_PROJ = 0, 1024, 2048, 3584, 3712
VMEM_LIMIT = 60 * 1024 * 1024

ADAM_LR, ADAM_B1, ADAM_B2, ADAM_EPS, ADAM_WD, ADAM_STEP = 0.001, 0.9, 0.999, 1e-08, 0.01, 10

NT = (((1,), (1,)), ((), ()))
TN = (((0,), (0,)), ((), ()))
_ANY = pl.BlockSpec(memory_space=pl.ANY)


def _cp(sem=None):
    return pltpu.CompilerParams(dimension_semantics=sem, vmem_limit_bytes=VMEM_LIMIT)


def _sigmoid(v):
    return 1.0 / (1.0 + jnp.exp(-v))


def _rsum8(v):
    r, c = v.shape
    return jnp.sum(v.reshape(r // 8, 8, c), axis=0)


def _rms(h, g):
    r = lax.rsqrt(jnp.mean(h * h, axis=-1, keepdims=True) + EPS)
    return h * r * g


def _rms_bwd(dy, h, g):
    r = lax.rsqrt(jnp.mean(h * h, axis=-1, keepdims=True) + EPS)
    n = h * r
    dn = dy * g
    dh = r * (dn - n * jnp.mean(dn * n, axis=-1, keepdims=True))
    return dh, _rsum8(dy * n)


def _dot(a, b, dims=None):
    if dims is None:
        return jnp.dot(a, b, preferred_element_type=f32)
    return lax.dot_general(a, b, dims, preferred_element_type=f32)


def _split_dot(v, m01, dims, terms, v_is_lhs=True):
    out, r = None, v
    for _ in range(terms):
        piece = r.astype(bf16)
        o = _dot(piece, m01, dims) if v_is_lhs else _dot(m01, piece, dims)
        out = o if out is None else out + o
        r = r - piece.astype(f32)
    return out


MESH = pl.DeviceIdType.MESH


def _place():
    return lax.axis_index("x"), lax.axis_index("y"), lax.axis_index("c")


def _flip(v, f):
    return 1 - v if f else v


def _call(body, comm, *, name, nsteps, in_specs, out_specs, out_shape, scratch_shapes, args):
    n_in, n_out, n_scr = len(in_specs), len(out_specs), len(scratch_shapes)
    if comm is None:
        res = pl.pallas_call(body, name=name, grid=(nsteps,), in_specs=in_specs, out_specs=out_specs, out_shape=out_shape,
                             scratch_shapes=scratch_shapes, compiler_params=_cp(("arbitrary",)))(*args)
        return list(res), []
    c_in, c_out = len(comm["ins"]), len(comm["outs"])

    def wrapped(*refs):
        o0 = n_in + c_in
        s0 = o0 + n_out + c_out
        cparts = (refs[n_in:o0], refs[o0 + n_out:s0], refs[s0 + n_scr:])

        @pl.when(pl.program_id(0) == 0)
        def _():
            comm["start"](*cparts)
        if "middle" in comm:
            @pl.when(pl.program_id(0) == nsteps // 2)
            def _():
                comm["middle"](*cparts)
        body(*refs[:n_in], *refs[o0:o0 + n_out], *refs[s0:s0 + n_scr])

        @pl.when(pl.program_id(0) == nsteps - 1)
        def _():
            comm["finish"](*cparts)

    any_spec = pl.BlockSpec(memory_space=pl.ANY)
    res = pl.pallas_call(
        wrapped, name=name, grid=(nsteps,), in_specs=list(in_specs) + [any_spec] * c_in,
        out_specs=list(out_specs) + [any_spec] * c_out, out_shape=list(out_shape) + list(comm["outs"]),
        scratch_shapes=list(scratch_shapes) + list(comm["scratch"]),
        compiler_params=_cp(("arbitrary",)))(*args, *comm["ins"])
    return list(res[:n_out]), list(res[n_out:])


def _gather_piece(slab, r0, rows):
    half = rows // 2
    flips = ((1, 0), (0, 1), (1, 1))

    def first(j, slab_ref, out_ref, send_sems, recv_sems):
        x, y, c = _place()
        return pltpu.make_async_remote_copy(
            src_ref=slab_ref.at[pl.ds(r0 + c * half, half), :], dst_ref=out_ref.at[2 * x + y, pl.ds(c * half, half), :],
            send_sem=send_sems.at[j], recv_sem=recv_sems.at[j],
            device_id=(_flip(x, flips[j][0]), _flip(y, flips[j][1]), c), device_id_type=MESH)

    def passed(j, out_ref, send_sems, recv_sems):
        x, y, c = _place()
        rows_j = out_ref.at[2 * _flip(x, flips[j][0]) + _flip(y, flips[j][1]), pl.ds(c * half, half), :]
        return pltpu.make_async_remote_copy(src_ref=rows_j, dst_ref=rows_j, send_sem=send_sems.at[3 + j],
                                            recv_sem=recv_sems.at[3 + j], device_id=(x, y, 1 - c), device_id_type=MESH)

    def start(ins, outs, scr):
        send_sems, recv_sems, stage, local_sems = scr
        x, y, _ = _place()
        load = pltpu.make_async_copy(ins[0].at[pl.ds(r0, rows), :], stage, local_sems.at[0])
        load.start()
        for j in range(3):
            first(j, ins[0], outs[0], send_sems, recv_sems).start()
        load.wait()
        pltpu.make_async_copy(stage, outs[0].at[2 * x + y], local_sems.at[1]).start()

    def middle(ins, outs, scr):
        send_sems, recv_sems, _, _ = scr
        for j in range(3):
            first(j, ins[0], outs[0], send_sems, recv_sems).wait_recv()
            passed(j, outs[0], send_sems, recv_sems).start()

    def finish(ins, outs, scr):
        send_sems, recv_sems, stage, local_sems = scr
        x, y, c = _place()
        for j in range(3):
            sib = outs[0].at[2 * _flip(x, flips[j][0]) + _flip(y, flips[j][1]), pl.ds((1 - c) * half, half), :]
            pltpu.make_async_remote_copy(src_ref=sib, dst_ref=sib, send_sem=send_sems.at[3 + j], recv_sem=recv_sems.at[3 + j],
                                         device_id=(x, y, 1 - c), device_id_type=MESH).wait_recv()
        for j in range(3):
            first(j, ins[0], outs[0], send_sems, recv_sems).wait_send()
            passed(j, outs[0], send_sems, recv_sems).wait_send()
        pltpu.make_async_copy(stage, outs[0].at[2 * x + y], local_sems.at[1]).wait()

    return dict(ins=[slab], outs=[jax.ShapeDtypeStruct((4, rows, D), bf16)],
                scratch=[pltpu.SemaphoreType.DMA((6,)), pltpu.SemaphoreType.DMA((6,)), pltpu.VMEM((rows, D), bf16),
                         pltpu.SemaphoreType.DMA((2,))], start=start, middle=middle, finish=finish)


def _scatter_piece(gpiece):
    half = gpiece.shape[1] // 2

    def copies(g_ref, land_ref, send_sems, recv_sems):
        x, y, c = _place()
        cps = []
        for fx in (0, 1):
            for fy in (0, 1):
                for fc in (0, 1):
                    k = 4 * fx + 2 * fy + fc - 1
                    if k < 0:
                        continue
                    px, py, pc = _flip(x, fx), _flip(y, fy), _flip(c, fc)
                    cps.append(pltpu.make_async_remote_copy(
                        src_ref=g_ref.at[2 * px + py, pl.ds(pc * half, half), :], dst_ref=land_ref.at[k],
                        send_sem=send_sems.at[k], recv_sem=recv_sems.at[k], device_id=(px, py, pc), device_id_type=MESH))
        return cps

    def start(ins, outs, scr):
        for cp in copies(ins[0], outs[0], *scr):
            cp.start()

    def finish(ins, outs, scr):
        for cp in copies(ins[0], outs[0], *scr):
            cp.wait()

    return dict(ins=[gpiece], outs=[jax.ShapeDtypeStruct((7, half, D), gpiece.dtype)],
                scratch=[pltpu.SemaphoreType.DMA((7,)), pltpu.SemaphoreType.DMA((7,))], start=start, finish=finish)


def _gather_blocks(block):
    rows = block.shape[0]

    def mine(out_ref):
        x, y, c = _place()
        return out_ref.at[pl.ds((4 * x + 2 * y + c) * rows, rows), :]

    def copies(b_ref, out_ref, send_sems, recv_sems):
        x, y, c = _place()
        cps = []
        for fx in (0, 1):
            for fy in (0, 1):
                for fc in (0, 1):
                    k = 4 * fx + 2 * fy + fc - 1
                    if k < 0:
                        continue
                    cps.append(pltpu.make_async_remote_copy(
                        src_ref=b_ref, dst_ref=mine(out_ref), send_sem=send_sems.at[k], recv_sem=recv_sems.at[k],
                        device_id=(_flip(x, fx), _flip(y, fy), _flip(c, fc)), device_id_type=MESH))
        return cps

    def start(ins, outs, scr):
        send_sems, recv_sems, stage, local_sems = scr
        load = pltpu.make_async_copy(ins[0], stage, local_sems.at[0])
        load.start()
        for cp in copies(ins[0], outs[0], send_sems, recv_sems):
            cp.start()
        load.wait()
        pltpu.make_async_copy(stage, mine(outs[0]), local_sems.at[1]).start()

    def finish(ins, outs, scr):
        send_sems, recv_sems, stage, local_sems = scr
        for cp in copies(ins[0], outs[0], send_sems, recv_sems):
            cp.wait()
        pltpu.make_async_copy(stage, mine(outs[0]), local_sems.at[1]).wait()

    return dict(ins=[block], outs=[jax.ShapeDtypeStruct((8 * rows, D), block.dtype)],
                scratch=[pltpu.SemaphoreType.DMA((7,)), pltpu.SemaphoreType.DMA((7,)), pltpu.VMEM((rows, D), block.dtype),
                         pltpu.SemaphoreType.DMA((2,))], start=start, finish=finish)


def _both(c1, c2):
    n = (len(c1["ins"]), len(c1["outs"]), len(c1["scratch"]))

    def split(parts):
        return [p[:k] for p, k in zip(parts, n)], [p[k:] for p, k in zip(parts, n)]

    def start(*parts):
        a, b = split(parts)
        c1["start"](*a)
        c2["start"](*b)

    def finish(*parts):
        a, b = split(parts)
        c1["finish"](*a)
        c2["finish"](*b)

    both = dict(ins=c1["ins"] + c2["ins"], outs=c1["outs"] + c2["outs"], scratch=c1["scratch"] + c2["scratch"],
                start=start, finish=finish)
    if "middle" in c1 or "middle" in c2:
        def middle(*parts):
            for cm, part in zip((c1, c2), split(parts)):
                if "middle" in cm:
                    cm["middle"](*part)
        both["middle"] = middle
    return both


def _rowwise(name, fn, nblk, rows=(), shifted=(), pars=(), refs=(), out_rows=(), out_accs=(), bm=CH, comm=None):
    n_sub = bm // CH
    n_r, n_s, n_p, n_w = len(rows), len(shifted) * n_sub, len(pars), len(refs)
    n_in = n_r + n_s + n_p + n_w
    n_o, n_a = len(out_rows), len(out_accs)

    def body(*all_refs):
        i = pl.program_id(0)
        ins = all_refs[:n_in]
        outs = all_refs[n_in:]
        rv = [r[...] for r in ins[:n_r]]
        sub = ins[n_r:n_r + n_s]
        sv = [jnp.concatenate([r[...] for r in sub[k * n_sub:(k + 1) * n_sub]], axis=0) if n_sub > 1 else sub[k][...]
              for k in range(len(shifted))]
        pv = [r[...] for r in ins[n_r + n_s:n_r + n_s + n_p]]
        ro, ao = fn(i, rv, sv, pv, list(ins[n_r + n_s + n_p:]))
        for r, v in zip(outs[:n_o], ro):
            r[...] = v.astype(r.dtype)
        accs = outs[n_o:]

        @pl.when(i == 0)
        def _():
            for r in accs:
                r[...] = jnp.zeros_like(r)
        for r, v in zip(accs, ao):
            r[...] += v

    in_specs = [pl.BlockSpec((bm, a.shape[1]), lambda i: (i, 0)) for a in rows]
    in_specs += [pl.BlockSpec((CH, a.shape[1]), lambda i, j=j: (jnp.maximum(n_sub * i - 1 + j, 0), 0))
                 for a in shifted for j in range(n_sub)]
    in_specs += [pl.BlockSpec(a.shape, lambda i, nd=a.ndim: (0,) * nd) for a in pars]
    in_specs += [spec for _, spec in refs]
    out_specs = [pl.BlockSpec((bm, c), lambda i: (i, 0)) for c, _ in out_rows]
    out_specs += [pl.BlockSpec((8, c), lambda i: (0, 0)) for c in out_accs]
    out_shape = [jax.ShapeDtypeStruct((nblk * bm, c), dt) for c, dt in out_rows]
    out_shape += [jax.ShapeDtypeStruct((8, c), f32) for c in out_accs]
    res, cres = _call(body, comm, name=name, nsteps=nblk, in_specs=in_specs, out_specs=out_specs, out_shape=out_shape,
                      scratch_shapes=[], args=[*rows, *[a for a in shifted for _ in range(n_sub)], *pars, *[a for a, _ in refs]])
    parts = (res[:n_o], res[n_o:])
    return parts if comm is None else parts + (cres,)


PACK_ROWS = 4096
HALF_ROWS = PACK_ROWS // 2
R_UP, R_DOWN, R_GLU, R_OUT, R_IN, R_SPARE = 0, 1024, 2048, 2560, 3072, 3972


C_ROWS = 928
BIG_ROWS = 768
DW_ROWS = 2816


def _contract_rows(lp, big=DW_ROWS):
    for rows in (big, BIG_ROWS):
        if lp % rows == 0:
            return rows
    return CH


def _dw_into(name, a, b, slab, ka, a_sharded, row_blk, n_s, s0, piece_rows=2048, a_square=False):
    lp = a.shape[0]
    bm = _contract_rows(lp)
    steps = lp // bm

    def body(a_ref, b_ref, *rest):
        o_ref, acc = rest[-2], rest[-1]
        k = pl.program_id(1)

        @pl.when(k == 0)
        def _():
            acc[...] = jnp.zeros_like(acc)
        a_v = a_ref[...]
        acc[...] += _dot(a_v * a_v if a_square else a_v, b_ref[...], TN)

        @pl.when(k == steps - 1)
        def _():
            o_ref[0] = acc[...].astype(bf16)

    in_specs = [pl.BlockSpec((bm, ka), (lambda s, k: (k, s)) if a_sharded else (lambda s, k: (k, 0))),
                pl.BlockSpec((bm, D), (lambda s, k: (k, 0)) if a_sharded else (lambda s, k: (k, s)))]
    args = [a, b]
    aliases = {}
    if slab is not None:
        in_specs.append(_ANY)
        args.append(slab)
        aliases = {2: 0}
    return pl.pallas_call(
        body, name=name, grid=(n_s, steps), in_specs=in_specs,
        out_specs=pl.BlockSpec((1, ka, D), lambda s, k: (s0 + s, row_blk, 0)),
        out_shape=jax.ShapeDtypeStruct((4, piece_rows, D), bf16),
        scratch_shapes=[pltpu.VMEM((ka, D), f32)], input_output_aliases=aliases,
        compiler_params=_cp(("arbitrary", "arbitrary")))(*args)


def _dw_in_t(dproj, n0):
    lp = n0.shape[0]
    bm = _contract_rows(lp, BIG_ROWS)
    steps = lp // bm
    bn = 512

    def body(a_ref, b_ref, o_ref, acc):
        k = pl.program_id(1)

        @pl.when(k == 0)
        def _():
            acc[...] = jnp.zeros_like(acc)
        acc[...] += _dot(a_ref[...], b_ref[...], TN)

        @pl.when(k == steps - 1)
        def _():
            o_ref[...] = acc[...].astype(bf16)

    return pl.pallas_call(
        body, name="dw_in", grid=(D // bn, steps),
        in_specs=[pl.BlockSpec((bm, W_PROJ), lambda j, k: (k, 0)), pl.BlockSpec((bm, bn), lambda j, k: (k, j))],
        out_specs=pl.BlockSpec((W_PROJ, bn), lambda j, k: (0, j)),
        out_shape=jax.ShapeDtypeStruct((W_PROJ, D), bf16),
        scratch_shapes=[pltpu.VMEM((W_PROJ, bn), f32)],
        compiler_params=_cp(("arbitrary", "arbitrary")))(dproj, n0)


def _head_expand():
    h = lax.broadcasted_iota(jnp.int32, (HEADS, D), 0)
    c = lax.broadcasted_iota(jnp.int32, (HEADS, D), 1)
    return jnp.where((c >> 6) == h, 1.0, 0.0).astype(bf16)


def _ssd_common(i, P, dtr, prev8, cw, cb, dtb, alog, xc=None):
    z = P[:, O_Z:O_Z + D].astype(f32)
    xp = P[:, O_XBC:O_XBC + D_XBC].astype(f32)
    dt_raw = dtr[:, 0:HEADS]
    row = lax.broadcasted_iota(jnp.int32, (CH, 1), 0)
    if xc is None:
        row8 = lax.broadcasted_iota(jnp.int32, (8, 1), 0)
        xc = cb + cw[3:4] * xp
        for k in (1, 2, 3):
            rolled = pltpu.roll(xp, k, 0)
            fix = pltpu.roll(prev8, k, 0)
            top = jnp.where(row8 < k, fix, rolled[0:8])
            xc = xc + cw[3 - k:4 - k] * jnp.concatenate([top, rolled[8:]], axis=0)
    sg = _sigmoid(xc)
    xbc = xc * sg
    live = jnp.where(jnp.logical_or(i > 0, row >= CH - N_META), 1.0, 0.0)
    pre = dt_raw + dtb
    dt = jnp.where(pre > 20.0, pre, jnp.log(1.0 + jnp.exp(jnp.minimum(pre, 20.0)))) * live
    a = -jnp.exp(alog)
    dta = dt * a
    r_i = lax.broadcasted_iota(jnp.int32, (CH, CH), 0)
    c_i = lax.broadcasted_iota(jnp.int32, (CH, CH), 1)
    tril = r_i >= c_i
    acs = _split_dot(dta, jnp.where(tril, 1.0, 0.0).astype(bf16), None, 3, v_is_lhs=False)
    acs_t = _split_dot(dta, jnp.where(r_i <= c_i, 1.0, 0.0).astype(bf16), TN, 3)
    e = _head_expand()
    acs_e = _split_dot(acs, e, None, 3)
    dt_e = _split_dot(dt, e, None, 3)
    return dict(z=z, xp=xp, xc=xc, sg=sg, xbc=xbc, live=live, pre=pre, dt=dt, a=a, tril=tril,
                acs=acs, acs_t=acs_t, e=e, acs_e=acs_e, dt_e=dt_e)


def _lmat(c, h):
    seg = c["acs"][:, h:h + 1] - c["acs_t"][h:h + 1, :]
    return jnp.where(c["tril"], jnp.exp(jnp.minimum(seg, 0.0)), 0.0)


def _pair_masks():
    lane = lax.broadcasted_iota(jnp.int32, (1, 128), 1)
    return jnp.where(lane < HEAD_DIM, 1.0, 0.0), jnp.where(lane >= HEAD_DIM, 1.0, 0.0)


def _ssd_fwd(proj, dtr, conv_w, conv_b, dt_bias, a_log, d_ssd, g_ssd, nch, comm=None):
    def body(p_ref, dtr_ref, cw_ref, cb_ref, dtb_ref, al_ref, d_ref, g_ref, y_ref, ys_ref, st_ref, xc_ref, prev8_ref, state_ref):
        i = pl.program_id(0)

        @pl.when(i == 0)
        def _():
            prev8_ref[...] = jnp.zeros_like(prev8_ref)
            state_ref[...] = jnp.zeros_like(state_ref)

        P = p_ref[...]
        c = _ssd_common(i, P, dtr_ref[...], prev8_ref[...], cw_ref[...], cb_ref[...], dtb_ref[...], al_ref[...])
        prev8_ref[...] = c["xp"][CH - 8:CH]
        xc_ref[...] = c["xc"]
        xbc = c["xbc"]
        x = xbc[:, 0:D]
        xdt = x * c["dt_e"]
        a_last_e = c["acs_e"][CH - 1:CH, :]
        w_end = (xdt * jnp.exp(a_last_e - c["acs_e"])).astype(bf16)
        m0, m1 = _pair_masks()
        ys = []
        for g in range(2):
            bg = xbc[:, D + NSTATE * g:D + NSTATE * (g + 1)].astype(bf16)
            cg = xbc[:, D + 2 * NSTATE + NSTATE * g:D + 2 * NSTATE + NSTATE * (g + 1)].astype(bf16)
            gmat = _dot(cg, bg, NT)
            st = state_ref[g]
            st_ref[0, g] = st
            sl = slice(512 * g, 512 * (g + 1))
            y_off = _dot(cg, st.astype(bf16)) * jnp.exp(c["acs_e"][:, sl])
            contrib = _dot(bg, w_end[:, sl], TN)
            state_ref[g] = st * jnp.exp(a_last_e[:, sl]) + contrib
            yd = []
            for pr in range(4):
                h0 = 8 * g + 2 * pr
                xp2 = xdt[:, 128 * (4 * g + pr):128 * (4 * g + pr + 1)]
                ma = (gmat * _lmat(c, h0)).astype(bf16)
                mb = (gmat * _lmat(c, h0 + 1)).astype(bf16)
                yd.append(_dot(ma, (xp2 * m0).astype(bf16)) + _dot(mb, (xp2 * m1).astype(bf16)))
            ys.append(jnp.concatenate(yd, axis=1) + y_off)
        d_e = _split_dot(d_ref[...], c["e"], None, 3)
        y = jnp.concatenate(ys, axis=1) + x * d_e
        y_ref[...] = y
        yg = y * (c["z"] * _sigmoid(c["z"]))
        ys_ref[...] = _rms(yg, g_ref[...]).astype(bf16)

    full = lambda a: pl.BlockSpec(a.shape, lambda i, nd=a.ndim: (0,) * nd)
    return _call(
        body, comm, name="ssd_fwd", nsteps=nch,
        in_specs=[pl.BlockSpec((CH, W_PROJ), lambda i: (i, 0)), pl.BlockSpec((CH, 128), lambda i: (i, 0))]
        + [full(a) for a in (conv_w, conv_b, dt_bias, a_log, d_ssd, g_ssd)],
        out_specs=[pl.BlockSpec((CH, D), lambda i: (i, 0)), pl.BlockSpec((CH, D), lambda i: (i, 0)),
                   pl.BlockSpec((1, 2, NSTATE, 512), lambda i: (i, 0, 0, 0)), pl.BlockSpec((CH, D_XBC), lambda i: (i, 0))],
        out_shape=[jax.ShapeDtypeStruct((nch * CH, D), f32), jax.ShapeDtypeStruct((nch * CH, D), bf16),
                   jax.ShapeDtypeStruct((nch, 2, NSTATE, 512), f32), jax.ShapeDtypeStruct((nch * CH, D_XBC), f32)],
        scratch_shapes=[pltpu.VMEM((8, D_XBC), f32), pltpu.VMEM((2, NSTATE, 512), f32)],
        args=[proj, dtr, conv_w, conv_b, dt_bias, a_log, d_ssd, g_ssd])


def _ssd_bwd(proj, dtr, xc_all, y, dys, du, states, conv_w, conv_b, dt_bias, a_log, d_ssd, g_ssd, nch, comm=None):
    def body(p_ref, dtr_ref, xc_ref, y_ref, dys_ref, du_ref, st_ref, cw_ref, cb_ref, dtb_ref, al_ref, d_ref, g_ref,
             dp_ref, dcw_ref, dcb_ref, ddtb_ref, dal_ref, dd_ref, dg_ref, nxt8_ref, dst_ref):
        step = pl.program_id(0)
        i = nch - 1 - step

        @pl.when(step == 0)
        def _():
            nxt8_ref[...] = jnp.zeros_like(nxt8_ref)
            dst_ref[...] = jnp.zeros_like(dst_ref)
            for r in (dcw_ref, dcb_ref, ddtb_ref, dal_ref, dd_ref, dg_ref):
                r[...] = jnp.zeros_like(r)

        P = p_ref[...]
        c = _ssd_common(i, P, dtr_ref[...], None, cw_ref[...], cb_ref[...], dtb_ref[...], al_ref[...], xc=xc_ref[...])
        xbc, z, e = c["xbc"], c["z"], c["e"]
        x = xbc[:, 0:D]
        yv = y_ref[...]
        sz = _sigmoid(z)
        silu_z = z * sz
        dyg, dg8 = _rms_bwd(dys_ref[...], yv * silu_z, g_ref[...])
        dg_ref[...] += dg8
        dy = dyg * silu_z
        dz = dyg * yv * (sz * (1.0 + z * (1.0 - sz)))
        d_e = _split_dot(d_ref[...], e, None, 3)
        dd_ref[...] += _rsum8(dy * x)
        xdt = x * c["dt_e"]
        a_last_e = c["acs_e"][CH - 1:CH, :]
        e_end = jnp.exp(a_last_e - c["acs_e"])
        w_end = xdt * e_end
        e_acs = jnp.exp(c["acs_e"])
        dy_dec = dy * e_acs
        m0, m1 = _pair_masks()
        lane16 = lax.broadcasted_iota(jnp.int32, (1, HEADS), 1)
        row16 = lax.broadcasted_iota(jnp.int32, (HEADS, 1), 0)
        dacs = jnp.zeros((CH, HEADS), f32)
        dacs_t = jnp.zeros((HEADS, CH), f32)
        dxdt_parts, dbs, dcs, zparts, yoff_parts, dlast_parts = [], [], [], [], [], []
        for g in range(2):
            sl = slice(512 * g, 512 * (g + 1))
            bg = xbc[:, D + NSTATE * g:D + NSTATE * (g + 1)].astype(bf16)
            cg = xbc[:, D + 2 * NSTATE + NSTATE * g:D + 2 * NSTATE + NSTATE * (g + 1)].astype(bf16)
            gmat = _dot(cg, bg, NT)
            st = st_ref[0, g]
            dstn = dst_ref[g]
            dstn_b = dstn.astype(bf16)
            y_off = _dot(cg, st.astype(bf16)) * e_acs[:, sl]
            yoff_parts.append(y_off)
            bds = _dot(bg, dstn_b)
            zparts.append(w_end[:, sl] * bds)
            dlast_parts.append(jnp.sum(dstn * st, axis=0, keepdims=True) * jnp.exp(a_last_e[:, sl]))
            dg_acc = jnp.zeros((CH, CH), f32)
            dxd = []
            for pr in range(4):
                lo = 128 * (4 * g + pr)
                xp2 = xdt[:, lo:lo + 128].astype(bf16)
                dy2 = dy[:, lo:lo + 128]
                outp = jnp.zeros((CH, 128), f32)
                for hh, msk in ((0, m0), (1, m1)):
                    h = 8 * g + 2 * pr + hh
                    lm = _lmat(c, h)
                    dyh = (dy2 * msk).astype(bf16)
                    mh = (gmat * lm).astype(bf16)
                    outp = outp + _dot(mh, dyh, TN)
                    dml = _dot(dyh, xp2, NT) * lm
                    dg_acc = dg_acc + dml
                    q = dml * gmat
                    dacs = dacs + jnp.where(lane16 == h, jnp.sum(q, axis=1, keepdims=True), 0.0)
                    dacs_t = dacs_t + jnp.where(row16 == h, jnp.sum(q, axis=0, keepdims=True), 0.0)
                dxd.append(outp)
            dxdt_parts.append(jnp.concatenate(dxd, axis=1) + e_end[:, sl] * bds)
            dgb = dg_acc.astype(bf16)
            dcs.append(_dot(dgb, bg) + _dot(dy_dec[:, sl].astype(bf16), st.astype(bf16), NT))
            dbs.append(_dot(dgb, cg, TN) + _dot(w_end[:, sl].astype(bf16), dstn_b, NT))
            dst_ref[g] = dstn * jnp.exp(a_last_e[:, sl]) + _dot(cg, dy_dec[:, sl].astype(bf16), TN)
        dxdt = jnp.concatenate(dxdt_parts, axis=1)
        zfull = jnp.concatenate(zparts, axis=1)
        y_off_full = jnp.concatenate(yoff_parts, axis=1)
        dlast = jnp.concatenate(dlast_parts, axis=1)
        red = lambda v: _split_dot(v, e, NT, 2)
        eye16 = jnp.where(lax.broadcasted_iota(jnp.int32, (HEADS, HEADS), 0) == lax.broadcasted_iota(jnp.int32, (HEADS, HEADS), 1),
                          1.0, 0.0).astype(bf16)
        dacs = dacs - _split_dot(dacs_t, eye16, TN, 3)
        zred = red(zfull)
        dacs = dacs + red(dy * y_off_full) - zred
        last_term = jnp.sum(zred, axis=0, keepdims=True) + red(dlast)
        rowc = lax.broadcasted_iota(jnp.int32, (CH, 1), 0)
        dacs = dacs + jnp.where(rowc == CH - 1, last_term, 0.0)
        r_i = lax.broadcasted_iota(jnp.int32, (CH, CH), 0)
        c_i = lax.broadcasted_iota(jnp.int32, (CH, CH), 1)
        ddta = _split_dot(dacs, jnp.where(c_i >= r_i, 1.0, 0.0).astype(bf16), None, 3, v_is_lhs=False)
        ddt = ddta * c["a"] + red(dxdt * x)
        dal_ref[...] += _rsum8(ddta * c["dt"] * c["a"])
        ddt_raw = ddt * _sigmoid(c["pre"]) * c["live"]
        ddtb_ref[...] += _rsum8(ddt_raw)
        dx = dy * d_e + dxdt * c["dt_e"]
        dxbc = jnp.concatenate([dx, dbs[0], dbs[1], dcs[0], dcs[1]], axis=1)
        sg = c["sg"]
        dxc = dxbc * (sg * (1.0 + c["xc"] * (1.0 - sg)))
        dcb_ref[...] += _rsum8(dxc)
        xp = c["xp"]
        row8 = lax.broadcasted_iota(jnp.int32, (8, 1), 0)
        cw = cw_ref[...]
        dxp = cw[3:4] * dxc
        dcw = jnp.where(row8 == 3, jnp.sum(dxc * xp, axis=0, keepdims=True), 0.0)
        nxt8 = nxt8_ref[...]
        for j in (1, 2, 3):
            rolled = pltpu.roll(dxc, CH - j, 0)
            fix = pltpu.roll(nxt8, 8 - j, 0)
            bot = jnp.where(row8 >= 8 - j, fix, rolled[CH - 8:CH])
            later = jnp.concatenate([rolled[:CH - 8], bot], axis=0)
            dxp = dxp + cw[3 - j:4 - j] * later
            dcw = dcw + jnp.where(row8 == 3 - j, jnp.sum(later * xp, axis=0, keepdims=True), 0.0)
        dcw_ref[...] += dcw
        nxt8_ref[...] = dxc[0:8]
        dp_ref[:, O_Z:O_Z + D] = dz.astype(bf16)
        dp_ref[:, O_U:O_U + D] = du_ref[...].astype(bf16)
        dp_ref[:, O_XBC:O_XBC + D_XBC] = dxp.astype(bf16)
        dp_ref[:, O_DT:W_PROJ] = jnp.zeros((CH, W_PROJ - O_DT), bf16)
        dp_ref[:, O_DT:O_DT + HEADS] = ddt_raw.astype(bf16)

    full = lambda a: pl.BlockSpec(a.shape, lambda s, nd=a.ndim: (0,) * nd)
    rev = lambda s: (nch - 1 - s, 0)
    acc = lambda cdim: pl.BlockSpec((8, cdim), lambda s: (0, 0))
    return _call(
        body, comm, name="ssd_bwd", nsteps=nch,
        in_specs=[pl.BlockSpec((CH, W_PROJ), rev), pl.BlockSpec((CH, 128), rev), pl.BlockSpec((CH, D_XBC), rev),
                  pl.BlockSpec((CH, D), rev), pl.BlockSpec((CH, D), rev), pl.BlockSpec((CH, D), rev),
                  pl.BlockSpec((1, 2, NSTATE, 512), lambda s: (nch - 1 - s, 0, 0, 0))]
        + [full(a) for a in (conv_w, conv_b, dt_bias, a_log, d_ssd, g_ssd)],
        out_specs=[pl.BlockSpec((CH, W_PROJ), rev), acc(D_XBC), acc(D_XBC), acc(HEADS), acc(HEADS), acc(D), acc(D)],
        out_shape=[jax.ShapeDtypeStruct((nch * CH, W_PROJ), bf16)]
        + [jax.ShapeDtypeStruct((8, cdim), f32) for cdim in (D_XBC, D_XBC, HEADS, HEADS, D, D)],
        scratch_shapes=[pltpu.VMEM((8, D_XBC), f32), pltpu.VMEM((2, NSTATE, 512), f32)],
        args=[proj, dtr, xc_all, y, dys, du, states, conv_w, conv_b, dt_bias, a_log, d_ssd, g_ssd])


SCAN_UNROLL = 8


def _to_slabs(slab_ref, q, mat):
    for ls in range(8):
        slab_ref[ls, pl.ds(PITCH * q, CH), :] = mat[:, 128 * ls:128 * (ls + 1)]


def _from_slabs(slab, q):
    return jnp.concatenate([slab(ls, PITCH * q) for ls in range(8)], axis=1)


def _tile(slab_ref, ls, t, lead=None):
    idx = (ls, pl.ds(t, 8, stride=PITCH), slice(None))
    return slab_ref[idx] if lead is None else slab_ref[(lead,) + idx]


def _s5_fwd(proj, bbq, ccq, ar, ai, d_skip, nch, comm=None):
    def body(u_ref, bb_ref, cc_ref, ar_ref, ai_ref, d_ref, s_ref, yl_ref, y5_ref, bu_ref, st_ref):
        @pl.when(pl.program_id(0) == 0)
        def _():
            st_ref[...] = jnp.zeros_like(st_ref)
        ub = u_ref[...]
        u = ub.astype(f32)
        for q in range(NQ):
            _to_slabs(bu_ref, q, _dot(ub[:, 128 * q:128 * (q + 1)], bb_ref[q]))
        ar_t = [ar_ref[:, 128 * l:128 * (l + 1)] for l in range(4)]
        ai_t = [ai_ref[:, 128 * l:128 * (l + 1)] for l in range(4)]

        def one(t, carry):
            re, im = carry
            nre, nim = [], []
            for l in range(4):
                a = ar_t[l] * re[l] - ai_t[l] * im[l] + _tile(bu_ref, l, t)
                b = ar_t[l] * im[l] + ai_t[l] * re[l] + _tile(bu_ref, l + 4, t)
                s_ref[0, l, pl.ds(t, 8, stride=PITCH), :] = a
                s_ref[0, l + 4, pl.ds(t, 8, stride=PITCH), :] = b
                nre.append(a)
                nim.append(b)
            return tuple(nre), tuple(nim)

        def step(tt, carry):
            for k in range(SCAN_UNROLL):
                carry = one(tt * SCAN_UNROLL + k, carry)
            return carry
        init = (tuple(st_ref[l] for l in range(4)), tuple(st_ref[l + 4] for l in range(4)))
        re, im = lax.fori_loop(0, CH // SCAN_UNROLL, step, init)
        for l in range(4):
            st_ref[l] = re[l]
            st_ref[l + 4] = im[l]
        ys = []
        for q in range(NQ):
            sq = _from_slabs(lambda ls, r0: s_ref[0, ls, pl.ds(r0, CH), :], q).astype(bf16)
            ys.append(_dot(sq, cc_ref[q], NT))
        yl = jnp.concatenate(ys, axis=1) + u * d_ref[...]
        yl_ref[...] = yl
        y5_ref[...] = (0.5 * yl * (1.0 + lax.erf(yl * (1.0 / math.sqrt(2.0))))).astype(bf16)

    const = lambda a: pl.BlockSpec(a.shape, lambda i, nd=a.ndim: (0,) * nd)
    return _call(
        body, comm, name="s5_fwd", nsteps=nch,
        in_specs=[pl.BlockSpec((CH, D), lambda i: (i, O_U // D)), const(bbq), const(ccq), const(ar), const(ai), const(d_skip)],
        out_specs=[pl.BlockSpec((1, 8, 8 * PITCH, 128), lambda i: (i, 0, 0, 0)),
                   pl.BlockSpec((CH, D), lambda i: (i, 0)), pl.BlockSpec((CH, D), lambda i: (i, 0))],
        out_shape=[jax.ShapeDtypeStruct((nch, 8, 8 * PITCH, 128), f32), jax.ShapeDtypeStruct((nch * CH, D), f32),
                   jax.ShapeDtypeStruct((nch * CH, D), bf16)],
        scratch_shapes=[pltpu.VMEM((8, 8 * PITCH, 128), f32), pltpu.VMEM((8, 8, 128), f32)],
        args=[proj, bbq, ccq, ar, ai, d_skip])


def _s5_bwd(proj, dyl, s_all, bbq, ccq, ar, ai, d_skip, nch, comm=None):
    def body(u_ref, dy_ref, s_ref, bbt_ref, cct_ref, ar_ref, ai_ref, d_ref,
             du_ref, dcc_ref, dbb_ref, dab_ref, dd_ref, ga_ref, st_ref):
        @pl.when(pl.program_id(0) == 0)
        def _():
            st_ref[...] = jnp.zeros_like(st_ref)
            for r in (dcc_ref, dbb_ref, dab_ref, dd_ref):
                r[...] = jnp.zeros_like(r)
        ub = u_ref[...]
        dyl_v = dy_ref[...]
        dd_ref[...] += _rsum8(dyl_v * ub.astype(f32))
        dyb = dyl_v.astype(bf16)
        for q in range(NQ):
            _to_slabs(ga_ref, q, _dot(dyb[:, 128 * q:128 * (q + 1)], cct_ref[q]))
        ar_t = [ar_ref[:, 128 * l:128 * (l + 1)] for l in range(4)]
        ai_t = [ai_ref[:, 128 * l:128 * (l + 1)] for l in range(4)]

        def one(t, carry):
            re, im, dar, dai = carry
            nre, nim, ndar, ndai = [], [], [], []
            for l in range(4):
                sre = _tile(s_ref, l, t, lead=0)
                sim = _tile(s_ref, l + 4, t, lead=0)
                ndar.append(dar[l] + re[l] * sre + im[l] * sim)
                ndai.append(dai[l] + im[l] * sre - re[l] * sim)
                a = _tile(ga_ref, l, t) + ar_t[l] * re[l] + ai_t[l] * im[l]
                b = _tile(ga_ref, l + 4, t) - ai_t[l] * re[l] + ar_t[l] * im[l]
                ga_ref[l, pl.ds(t, 8, stride=PITCH), :] = a
                ga_ref[l + 4, pl.ds(t, 8, stride=PITCH), :] = b
                nre.append(a)
                nim.append(b)
            return tuple(nre), tuple(nim), tuple(ndar), tuple(ndai)

        def step(tt, carry):
            for k in range(SCAN_UNROLL):
                carry = one(CH - 1 - (tt * SCAN_UNROLL + k), carry)
            return carry
        four = lambda ref, o: tuple(ref[l + o] for l in range(4))
        re, im, dar, dai = lax.fori_loop(0, CH // SCAN_UNROLL, step,
                                         (four(st_ref, 0), four(st_ref, 4), four(dab_ref, 0), four(dab_ref, 4)))
        for l in range(4):
            st_ref[l], st_ref[l + 4] = re[l], im[l]
            dab_ref[l], dab_ref[l + 4] = dar[l], dai[l]
        dus = []
        for q in range(NQ):
            aq = _from_slabs(lambda ls, r0: ga_ref[ls, pl.ds(r0, CH), :], q).astype(bf16)
            sq = _from_slabs(lambda ls, r0: s_ref[0, ls, pl.ds(r0, CH), :], q).astype(bf16)
            dcc_ref[q] += _dot(dyb[:, 128 * q:128 * (q + 1)], sq, TN)
            dbb_ref[q] += _dot(ub[:, 128 * q:128 * (q + 1)], aq, TN)
            dus.append(_dot(aq, bbt_ref[q], NT))
        du_ref[...] = jnp.concatenate(dus, axis=1) + dyl_v * d_ref[...]

    const = lambda a: pl.BlockSpec(a.shape, lambda s, nd=a.ndim: (0,) * nd)
    rev = lambda s: (nch - 1 - s, 0)
    return _call(
        body, comm, name="s5_bwd", nsteps=nch,
        in_specs=[pl.BlockSpec((CH, D), lambda s: (nch - 1 - s, O_U // D)), pl.BlockSpec((CH, D), rev),
                  pl.BlockSpec((1, 8, 8 * PITCH, 128), lambda s: (nch - 1 - s, 0, 0, 0)),
                  const(bbq), const(ccq), const(ar), const(ai), const(d_skip)],
        out_specs=[pl.BlockSpec((CH, D), rev), pl.BlockSpec((NQ, 128, D), lambda s: (0, 0, 0)),
                   pl.BlockSpec((NQ, 128, D), lambda s: (0, 0, 0)), pl.BlockSpec((8, 8, 128), lambda s: (0, 0, 0)),
                   pl.BlockSpec((8, D), lambda s: (0, 0))],
        out_shape=[jax.ShapeDtypeStruct((nch * CH, D), f32), jax.ShapeDtypeStruct((NQ, 128, D), f32),
                   jax.ShapeDtypeStruct((NQ, 128, D), f32), jax.ShapeDtypeStruct((8, 8, 128), f32),
                   jax.ShapeDtypeStruct((8, D), f32)],
        scratch_shapes=[pltpu.VMEM((8, 8 * PITCH, 128), f32), pltpu.VMEM((8, 8, 128), f32)],
        args=[proj, dyl, s_all, bbq, ccq, ar, ai, d_skip])


def _s5_tables(lam_re, lam_im, log_step, b_re, b_im):
    step = jnp.exp(log_step)[:, None]
    mag = jnp.exp(lam_re * step)
    ab_re = mag * jnp.cos(lam_im * step)
    ab_im = mag * jnp.sin(lam_im * step)
    den = lam_re * lam_re + lam_im * lam_im
    coef_re = ((ab_re - 1.0) * lam_re + ab_im * lam_im) / den
    coef_im = (ab_im * lam_re - (ab_re - 1.0) * lam_im) / den
    bb_re = coef_re[..., None] * b_re - coef_im[..., None] * b_im
    bb_im = coef_re[..., None] * b_im + coef_im[..., None] * b_re
    return ab_re, ab_im, bb_re, bb_im


def _blockdiag_in(m_re, m_im):
    eye = jnp.eye(8, dtype=f32)

    def one(m):
        m = m.reshape(NQ, 8, S5_P, 16)
        return jnp.einsum("qgph,gk->qghkp", m, eye).reshape(NQ, 128, 512)
    return jnp.concatenate([one(m_re), one(m_im)], axis=2)


def _blockdiag_in_grad(dm):
    def one(x):
        x = x.reshape(NQ, 8, 16, 8, S5_P)
        return jnp.einsum("qghgp->qgph", x).reshape(NQ * 8, S5_P, 16)
    return one(dm[:, :, :512]), one(dm[:, :, 512:])


def _local_step(x2, tgt2, meta, p, w_in_t, slab):
    seq = x2.shape[0]
    nch = 1 + seq // CH
    bmb = BIG_ROWS if (nch * CH) % BIG_ROWS == 0 else CH
    nbig = nch * CH // bmb
    metablk = jnp.concatenate([jnp.zeros((CH - N_META, D), f32), meta, jnp.zeros((bmb - CH, D), f32)], axis=0)
    w_full = (w_in_t, pl.BlockSpec(w_in_t.shape, lambda i: (0, 0), pipeline_mode=pl.Buffered(1)))

    def lead(i, v):
        return jnp.logical_and(i == 0, lax.broadcasted_iota(jnp.int32, (v.shape[0], 1), 0) < CH)
    h0_of = lambda i, s, q: jnp.where(lead(i, s[0]), q[0][:s[0].shape[0]], s[0])

    def in_fn(i, r, s, q, w):
        nb = _rms(h0_of(i, s, q), q[1]).astype(bf16)
        acc = _dot(nb, w[0][...], NT)
        return [acc, acc[:, O_DT:O_DT + 128], nb], []
    (proj, dtr, n0), _, (g_up,) = _rowwise("in_proj", in_fn, nbig, shifted=[x2], pars=[metablk, p["g_mix"]], refs=[w_full],
                                           out_rows=[(W_PROJ, bf16), (128, f32), (D, bf16)], bm=bmb,
                                           comm=_gather_piece(slab, R_UP, 1024))
    (y, y_ssd, states, xc_all), (g_down,) = _ssd_fwd(proj, dtr, p["conv_w"], p["conv_b"], p["dt_bias"], p["a_log"], p["d_ssd"],
                                                     p["g_ssd"], nch, comm=_gather_piece(slab, R_DOWN, 1024))

    ab_re, ab_im, bb_re, bb_im = _s5_tables(p["lam_re"], p["lam_im"], p["log_step"], p["b_re"], p["b_im"])
    ar, ai = ab_re.reshape(NQ, 512), ab_im.reshape(NQ, 512)
    bbq = _blockdiag_in(bb_re, bb_im)
    ccq = _blockdiag_in(jnp.swapaxes(p["c_re"], 1, 2), -jnp.swapaxes(p["c_im"], 1, 2))
    d_skip = p["d_s5"].reshape(1, D)
    bbq_b, ccq_b = bbq.astype(bf16), ccq.astype(bf16)
    (s_all, ylin, y5), (g_go,) = _s5_fwd(proj, bbq_b, ccq_b, ar, ai, d_skip, nch, comm=_gather_piece(slab, R_GLU, 1024))
    whole = lambda a: (a, pl.BlockSpec(a.shape, lambda i: (0, 0, 0), pipeline_mode=pl.Buffered(1)))
    w_up, w_down = whole(g_up), whole(g_down)
    w_glu_t = (g_go, pl.BlockSpec((4, 512, D), lambda i: (0, 0, 0), pipeline_mode=pl.Buffered(1)))
    w_out = (g_go, pl.BlockSpec((4, 512, D), lambda i: (0, 1, 0), pipeline_mode=pl.Buffered(1)))

    def glu_fn(i, r, s, q, w):
        v = jnp.concatenate([_dot(r[0], w[0][k], NT) for k in range(4)], axis=1) + q[0]
        return [v, _rms(v[:, :D] * _sigmoid(v[:, D:]), q[1])], []
    (v, y_s5), _ = _rowwise("glu", glu_fn, nbig, rows=[y5], pars=[p["b_glu"], p["g_s5"]], refs=[w_glu_t],
                            out_rows=[(2 * D, bf16), (D, bf16)], bm=bmb)

    def out_fn(i, r, s, q, w):
        acc = (_dot(r[0][:, :512], w[0][0]) + _dot(r[0][:, 512:], w[0][1]) + _dot(r[1][:, :512], w[0][2])
               + _dot(r[1][:, 512:], w[0][3]))
        return [h0_of(i, s, q) + acc], []
    (h1,), _ = _rowwise("out_proj", out_fn, nbig, rows=[y_ssd, y_s5], shifted=[x2], pars=[metablk], refs=[w_out],
                        out_rows=[(D, f32)], bm=bmb)

    def up_fn(i, r, s, q, w):
        nb = _rms(r[0], q[0]).astype(bf16)
        return [jnp.concatenate([jnp.maximum(_dot(nb, w[0][k]), 0.0).astype(bf16) for k in range(4)], axis=1), nb], []
    (relu_m, n1), _ = _rowwise("up_proj", up_fn, nbig, rows=[h1], pars=[p["g_mlp"]], refs=[w_up],
                               out_rows=[(4 * D, bf16), (D, bf16)], bm=bmb)

    def down_fn(i, r, s, q, w):
        acc = None
        for k in range(4):
            t = r[0][:, D * k:D * (k + 1)]
            part = _dot(t * t, w[0][k])
            acc = part if acc is None else acc + part
        return [r[1] + acc], []
    (h2,), _ = _rowwise("down_proj", down_fn, nbig, rows=[relu_m, h1], refs=[w_down], out_rows=[(D, f32)], bm=bmb)

    def final_fn(i, r, s, q, w):
        err = jnp.where(lead(i, r[0]), 0.0, _rms(r[0], q[0]) - s[0])
        dh, dg8 = _rms_bwd(err * (1.0 / D), r[0], q[0])
        return [dh, dh], [_rsum8(err * err), dg8]
    (dh2, dh2_b), (loss8, dgf8) = _rowwise("final", final_fn, nbig, rows=[h2], shifted=[tgt2], pars=[p["g_final"]],
                                           out_rows=[(D, f32), (D, bf16)], out_accs=[D, D], bm=bmb)
    loss = 0.5 / D * jnp.sum(loss8)

    def down_bwd_fn(i, r, s, q, w):
        dm_ = [_dot(r[0], w[0][k], NT) * (2.0 * r[1][:, D * k:D * (k + 1)].astype(f32)) for k in range(4)]
        return [jnp.concatenate(dm_, axis=1)], []
    (dm,), _ = _rowwise("down_bwd", down_bwd_fn, nbig, rows=[dh2_b, relu_m], refs=[w_down], out_rows=[(4 * D, bf16)], bm=bmb)
    g_a = _dw_into("dw_down", relu_m, dh2_b, None, 1024, True, 1, 4, 0, piece_rows=2048, a_square=True)

    def up_bwd_fn(i, r, s, q, w):
        acc = _dot(r[0][:, :D], w[0][0], NT)
        for k in range(1, 4):
            acc = acc + _dot(r[0][:, D * k:D * (k + 1)], w[0][k], NT)
        dh, dg8 = _rms_bwd(acc, r[1], q[0])
        dh1_ = r[2] + dh
        return [dh1_, dh1_], [dg8]
    (dh1, dh1_b), (dgmlp8,) = _rowwise("up_bwd", up_bwd_fn, nbig, rows=[dm, h1, dh2], pars=[p["g_mlp"]], refs=[w_up],
                                       out_rows=[(D, f32), (D, bf16)], out_accs=[D], bm=bmb)
    g_a = _dw_into("dw_up", n1, dm, g_a, 1024, False, 0, 4, 0, piece_rows=2048)

    def out_bwd_fn(i, r, s, q, w):
        dmix = [_dot(r[0], w[0][k], NT) for k in range(4)]
        v1, v2 = r[1][:, :D].astype(f32), r[1][:, D:].astype(f32)
        s2 = _sigmoid(v2)
        dglu, dg8 = _rms_bwd(jnp.concatenate(dmix[2:], axis=1), v1 * s2, q[0])
        dv = jnp.concatenate([dglu * s2, dglu * v1 * s2 * (1.0 - s2)], axis=1)
        return [jnp.concatenate(dmix[:2], axis=1), dv], [dg8, _rsum8(dv)]
    (dys, dv), (dgs58, dbglu8) = _rowwise("out_bwd", out_bwd_fn, nbig, rows=[dh1_b, v], pars=[p["g_s5"]], refs=[w_out],
                                          out_rows=[(D, f32), (2 * D, bf16)], out_accs=[D, 2 * D], bm=bmb)
    g_b = _dw_into("dw_out_a", y_ssd, dh1_b, None, 512, True, 1, 2, 0, piece_rows=1024)
    g_b = _dw_into("dw_out_b", y_s5, dh1_b, g_b, 512, True, 1, 2, 2, piece_rows=1024)

    def glu_bwd_fn(i, r, s, q, w):
        acc = _dot(r[0][:, :512], w[0][0])
        for k in range(1, 4):
            acc = acc + _dot(r[0][:, 512 * k:512 * (k + 1)], w[0][k])
        yl = r[1]
        cdf = 0.5 * (1.0 + lax.erf(yl * (1.0 / math.sqrt(2.0))))
        pdf = jnp.exp(-0.5 * yl * yl) * (1.0 / math.sqrt(2.0 * math.pi))
        return [acc * (cdf + yl * pdf)], []
    (dylin,), _ = _rowwise("glu_bwd", glu_bwd_fn, nbig, rows=[dv, ylin], refs=[w_glu_t], out_rows=[(D, f32)], bm=bmb)
    g_b = _dw_into("dw_glu", dv, y5, g_b, 512, True, 0, 4, 0, piece_rows=1024)

    (du, dcc, dbb, dab, dds5), (land_a,) = _s5_bwd(proj, dylin, s_all, bbq_b, ccq_b, ar, ai, d_skip, nch,
                                                   comm=_scatter_piece(g_a))

    s8 = lambda a: jnp.sum(a, axis=0, keepdims=True)
    dab_q = jnp.swapaxes(dab.reshape(2, 4, NQ, 128), 1, 2).reshape(2, NQ * 8, S5_P)
    dbb_re, dbb_im = _blockdiag_in_grad(dbb)
    dcr, dci = _blockdiag_in_grad(dcc)
    _, vjp = jax.vjp(_s5_tables, p["lam_re"], p["lam_im"], p["log_step"], p["b_re"], p["b_im"])
    dlam_re, dlam_im, dlog_step, db_re, db_im = vjp((dab_q[0], dab_q[1], dbb_re, dbb_im))
    early = dict(lam_re=dlam_re, lam_im=dlam_im, log_step=dlog_step, b_re=db_re, b_im=db_im, c_re=jnp.swapaxes(dcr, 1, 2),
                 c_im=-jnp.swapaxes(dci, 1, 2), d_s5=s8(dds5).reshape(NQ * 8, 16), b_glu=s8(dbglu8), g_s5=s8(dgs58),
                 g_mlp=s8(dgmlp8), g_final=s8(dgf8).reshape(D))
    early_pack = _pack_small([early[n] for n in EARLY], _rows_for(EARLY))

    (dproj, dcw8, dcb8, ddtb8, dal8, dd8, dgssd8), (land_b, all_early) = _ssd_bwd(
        proj, dtr, xc_all, y, dys, du, states, p["conv_w"], p["conv_b"], p["dt_bias"], p["a_log"], p["d_ssd"], p["g_ssd"], nch,
        comm=_both(_scatter_piece(g_b), _gather_blocks(early_pack)))

    gt = _dw_in_t(dproj, n0)
    gt = jnp.concatenate([gt[0:1024], gt[O_XBC:O_XBC + D_XBC], gt[O_DT:O_DT + HEADS], gt[O_U:O_U + D]], axis=0).reshape(4, 900, D)
    g_c = jnp.concatenate([gt, jnp.zeros((4, C_ROWS - 900, D), bf16)], axis=1)

    def in_bwd_fn(i, r, s, q, w):
        dh, dg8 = _rms_bwd(_dot(r[0], w[0][...]), h0_of(i, s, q), q[1])
        return [r[1] + dh], [dg8]
    (dh0,), (dgmix8,), (land_c,) = _rowwise(
        "in_bwd", in_bwd_fn, nbig, rows=[dproj, dh1], shifted=[x2], pars=[metablk, p["g_mix"]], refs=[w_full],
        out_rows=[(D, f32)], out_accs=[D], bm=bmb, comm=_scatter_piece(g_c))

    hsum = lambda a: jnp.sum(s8(a).reshape(HEADS, HEAD_DIM), axis=1).reshape(1, HEADS)
    late = dict(g_mix=s8(dgmix8), conv_b=s8(dcb8), dt_bias=s8(ddtb8), a_log=s8(dal8), d_ssd=hsum(dd8), g_ssd=s8(dgssd8),
                conv_w=dcw8[0:4], meta_tokens=dh0[CH - N_META:CH], loss=loss.reshape(1))
    return dh0, [(g_a, land_a), (g_b, land_b), (g_c, land_c)], all_early, late


def _perm_rows_w_in(wt):
    return jnp.concatenate([wt[0:1024], wt[2576:3600], wt[1024:2560], wt[2560:2576],
                            jnp.zeros((W_PROJ - 3600, wt.shape[1]), wt.dtype)], axis=0)


def _unperm_cols_w_in(g):
    return jnp.concatenate([g[:, 0:1024], g[:, O_XBC:O_XBC + D_XBC], g[:, O_DT:O_DT + HEADS], g[:, O_U:O_U + D]], axis=1)


def _pack_shard(w_in, w_glu, w_out, w_up, w_down, spare):
    dt = w_in.dtype
    parts = [w_up, w_down, w_glu.T, w_out, w_in.T, spare,
             jnp.zeros((PACK_ROWS - R_SPARE - spare.shape[0], D), dt)]
    return jnp.concatenate(parts, axis=0)


def _allgather8(x_shard, name):
    m_per, n = x_shard.shape

    def body(x_ref, out_ref, send_sems, recv_sems, stage, local_sems):
        x, y, c = _place()
        me, sibling = (x, y, c), (x, y, 1 - c)
        chips = [(1 - x, y), (x, 1 - y), (1 - x, 1 - y)]

        def rows(px, py, pc):
            return out_ref.at[pl.ds((4 * px + 2 * py + pc) * m_per, m_per), :]

        def copy(k, block, to, src=None):
            return pltpu.make_async_remote_copy(
                src_ref=rows(*block) if src is None else src, dst_ref=rows(*block),
                send_sem=send_sems.at[k], recv_sem=recv_sems.at[k], device_id=to, device_id_type=MESH)

        load = pltpu.make_async_copy(x_ref, stage, local_sems.at[0])
        load.start()
        first = [copy(0, me, sibling, src=x_ref)]
        first += [copy(1 + j, me, (*chip, c), src=x_ref) for j, chip in enumerate(chips)]
        for cp in first:
            cp.start()
        load.wait()
        store = pltpu.make_async_copy(stage, rows(*me), local_sems.at[1])
        store.start()
        passed = [copy(4 + j, (*chip, c), sibling) for j, chip in enumerate(chips)]
        for j, chip in enumerate(chips):
            copy(1 + j, (*chip, c), me).wait_recv()
            passed[j].start()
        copy(0, sibling, me).wait_recv()
        for j, chip in enumerate(chips):
            copy(4 + j, (*chip, 1 - c), me).wait_recv()
        for cp in first + passed:
            cp.wait_send()
        store.wait()

    return pl.pallas_call(
        body, name=name, out_shape=jax.ShapeDtypeStruct((8 * m_per, n), x_shard.dtype),
        in_specs=[_ANY], out_specs=_ANY,
        scratch_shapes=[pltpu.SemaphoreType.DMA((7,)), pltpu.SemaphoreType.DMA((7,)), pltpu.VMEM((m_per, n), x_shard.dtype),
                        pltpu.SemaphoreType.DMA((2,))])(x_shard)


def _swap_sibling(r, name):
    def body(r_ref, out_ref, send_sem, recv_sem):
        x, y, c = _place()
        cp = pltpu.make_async_remote_copy(src_ref=r_ref, dst_ref=out_ref, send_sem=send_sem, recv_sem=recv_sem,
                                          device_id=(x, y, 1 - c), device_id_type=MESH)
        cp.start()
        cp.wait()

    return pl.pallas_call(
        body, name=name, out_shape=jax.ShapeDtypeStruct(r.shape, r.dtype), in_specs=[_ANY], out_specs=_ANY,
        scratch_shapes=[pltpu.SemaphoreType.DMA, pltpu.SemaphoreType.DMA])(r)


SH_CONVW, SH_META = 4 * 384, 16 * 256
SPARE_ROWS = 17

SMALL_SHAPES = dict(
    g_mix=(1, 1024), conv_b=(1, 1536), dt_bias=(1, 16), a_log=(1, 16), d_ssd=(1, 16), g_ssd=(1, 1024), lam_re=(1, 64, 64),
    lam_im=(1, 64, 64), log_step=(1, 64), b_re=(1, 64, 64, 16), b_im=(1, 64, 64, 16), c_re=(1, 64, 16, 64), c_im=(1, 64, 16, 64),
    d_s5=(1, 64, 16), b_glu=(1, 2048), g_s5=(1, 1024), g_mlp=(1, 1024), g_final=(1024,),
    conv_w=(4, D_XBC), meta_tokens=(N_META, D), loss=(1,))
EARLY = ["lam_re", "lam_im", "log_step", "b_re", "b_im", "c_re", "c_im", "d_s5", "b_glu", "g_s5", "g_mlp", "g_final"]
LATE = ["g_mix", "conv_b", "dt_bias", "a_log", "d_ssd", "g_ssd", "conv_w", "meta_tokens", "loss"]


def _rows_for(names):
    return -(-sum(math.prod(SMALL_SHAPES[n]) for n in names) // (8 * D)) * 8


def _pack_small(arrs, rows):
    flat = jnp.concatenate([a.reshape(-1).astype(f32) for a in arrs])
    return jnp.concatenate([flat, jnp.zeros((rows * D - flat.shape[0],), f32)]).reshape(rows, D)


def _unpack_small(slab, shapes):
    flat = slab.reshape(-1)
    out, o = [], 0
    for shp in shapes:
        n = math.prod(shp)
        out.append(flat[o:o + n].reshape(shp))
        o += n
    return out


def _sum8(g, rows, name):
    def body(g_ref, o_ref):
        acc = g_ref[0]
        for k in range(1, 8):
            acc = acc + g_ref[k]
        o_ref[...] = acc
    return pl.pallas_call(body, name=name, out_shape=jax.ShapeDtypeStruct((rows, D), f32),
                          compiler_params=_cp())(g.reshape(8, rows, D))


def _adam_math(w_, g_, m_, v_):
    m2 = ADAM_B1 * m_ + (1.0 - ADAM_B1) * g_
    v2 = ADAM_B2 * v_ + (1.0 - ADAM_B2) * jnp.square(g_)
    m_hat = m2 / (1.0 - ADAM_B1 ** ADAM_STEP)
    v_hat = v2 / (1.0 - ADAM_B2 ** ADAM_STEP)
    delta = -ADAM_LR * (m_hat / (jnp.sqrt(v_hat) + ADAM_EPS) + ADAM_WD * w_)
    return delta, m2, v2


def _adamw(name, w, g, m, v, bm):
    def fn(i, r, s, q, refs):
        return list(_adam_math(*r)), []
    c = w.shape[1]
    (d, m2, v2), _ = _rowwise(name, fn, w.shape[0] // bm, rows=[w, g, m, v], out_rows=[(c, f32)] * 3, bm=bm)
    return d, m2, v2


def _adamw_whole(name, w, g, m, v):
    def body(w_ref, g_ref, m_ref, v_ref, d_ref, m2_ref, v2_ref):
        d_ref[...], m2_ref[...], v2_ref[...] = _adam_math(w_ref[...], g_ref[...], m_ref[...], v_ref[...])
    return pl.pallas_call(body, name=name, out_shape=[jax.ShapeDtypeStruct(w.shape, f32)] * 3, compiler_params=_cp())(w, g, m, v)


def _sum_parts(name, own, land):
    def fn(i, r, s, q, refs):
        acc = r[0].astype(f32)
        for k in range(7):
            acc = acc + refs[0][k].astype(f32)
        return [acc], []
    rows = own.shape[0]
    bm = CH if rows % CH == 0 else rows
    (o,), _ = _rowwise(name, fn, rows // bm, rows=[own], refs=[(land, pl.BlockSpec((7, bm, D), lambda i: (0, i, 0)))],
                       out_rows=[(D, f32)], bm=bm)
    return o


def kernel(x, meta_tokens, g_mix, w_in, conv_w, conv_b, dt_bias, a_log, d_ssd, g_ssd, lam_re, lam_im, log_step, b_re, b_im, c_re, c_im, d_s5, w_glu, b_glu, g_s5, w_out, g_mlp, w_up, w_down, g_final, loss_target, m_meta_tokens, m_g_mix, m_w_in, m_conv_w, m_conv_b, m_dt_bias, m_a_log, m_d_ssd, m_g_ssd, m_lam_re, m_lam_im, m_log_step, m_b_re, m_b_im, m_c_re, m_c_im, m_d_s5, m_w_glu, m_b_glu, m_g_s5, m_w_out, m_g_mlp, m_w_up, m_w_down, m_g_final, v_meta_tokens, v_g_mix, v_w_in, v_conv_w, v_conv_b, v_dt_bias, v_a_log, v_d_ssd, v_g_ssd, v_lam_re, v_lam_im, v_log_step, v_b_re, v_b_im, v_c_re, v_c_im, v_d_s5, v_w_glu, v_b_glu, v_g_s5, v_w_out, v_g_mlp, v_w_up, v_w_down, v_g_final):
    given = dict(locals())
    cx, cy, cc = _place()
    chip = 2 * cx + cy

    small_f = jnp.concatenate([conv_w.reshape(-1), meta_tokens.reshape(-1)])
    t_hi = small_f.astype(bf16)
    r_1 = small_f - t_hi.astype(f32)
    t_mid = r_1.astype(bf16)
    t_lo = (r_1 - t_mid.astype(f32)).astype(bf16)
    terms = jnp.concatenate([t_hi, t_mid, t_lo])
    spare = jnp.concatenate([terms, jnp.zeros((SPARE_ROWS * D - terms.shape[0],), bf16)]).reshape(SPARE_ROWS, D)
    slab = _pack_shard(w_in[0].astype(bf16), w_glu[0].astype(bf16), w_out[0].astype(bf16), w_up[0].astype(bf16),
                       w_down[0].astype(bf16), spare)
    my_half = lax.dynamic_slice_in_dim(slab, R_IN + cc * 512, 512, axis=0)
    gathered = _allgather8(my_half, "gather_w_in").reshape(4, 1024, D)
    w_in_t = _perm_rows_w_in(jnp.concatenate([gathered[s, 0:900] for s in range(4)], axis=0))
    n_sf = SH_CONVW + SH_META
    tr = gathered[:, 900:900 + SPARE_ROWS].reshape(4, SPARE_ROWS * D)[:, :3 * n_sf].astype(f32).reshape(4, 3, n_sf)
    sp = tr[:, 0] + tr[:, 1] + tr[:, 2]
    conv_w_full = jnp.concatenate([sp[s, :SH_CONVW].reshape(4, 384) for s in range(4)], axis=1)
    meta_full = jnp.concatenate([sp[s, SH_CONVW:].reshape(16, 256) for s in range(4)], axis=1)

    p = dict(g_mix=g_mix, conv_w=conv_w_full, conv_b=conv_b, dt_bias=dt_bias, a_log=a_log, d_ssd=d_ssd, g_ssd=g_ssd,
             lam_re=lam_re[0], lam_im=lam_im[0], log_step=log_step[0], b_re=b_re[0], b_im=b_im[0], c_re=c_re[0], c_im=c_im[0],
             d_s5=d_s5[0], b_glu=b_glu, g_s5=g_s5, g_mlp=g_mlp, g_final=g_final.reshape(1, D))
    dh0, pieces, all_early, late = _local_step(x[0], loss_target[0], meta_full, p, w_in_t, slab)
    grad_x = dh0[CH:].reshape(x.shape)

    reds = []
    for k, (gp, land) in enumerate(pieces):
        half = gp.shape[1] // 2
        own = lax.dynamic_slice(gp, (chip, cc * half, 0), (1, half, D)).reshape(half, D)
        reds.append(_sum_parts("rs_sum_%d" % k, own, land))
    red = jnp.concatenate(reds, axis=0)
    other = _swap_sibling(red, "rs_share")
    first = jnp.where(cc == 0, red, other)
    second = jnp.where(cc == 0, other, red)
    g_up, g_down = first[0:1024], second[0:1024]
    g_glu, g_out = first[1024:1536].T, second[1024:1536]
    hc = C_ROWS // 2
    g_in_t = jnp.concatenate([first[1536:1536 + hc], second[1536:1536 + 900 - hc]], axis=0)

    gs = dict(zip(EARLY, _unpack_small(_sum8(all_early, _rows_for(EARLY), "sum8_early"), [SMALL_SHAPES[n] for n in EARLY])))
    all_late = _allgather8(_pack_small([late[n] for n in LATE], _rows_for(LATE)), "gather_small")
    gs.update(zip(LATE, _unpack_small(_sum8(all_late, _rows_for(LATE), "sum8_late"), [SMALL_SHAPES[n] for n in LATE])))
    g_conv_w = lax.dynamic_slice_in_dim(gs.pop("conv_w"), chip * 384, 384, axis=1).reshape(conv_w.shape)
    g_meta = lax.dynamic_slice_in_dim(gs.pop("meta_tokens"), chip * 256, 256, axis=1)
    loss = gs.pop("loss").reshape(())

    grads = dict(gs, meta_tokens=g_meta, conv_w=g_conv_w, w_in=g_in_t.T.reshape(w_in.shape), w_glu=g_glu.reshape(w_glu.shape),
                 w_out=g_out.reshape(w_out.shape), w_up=g_up.reshape(w_up.shape), w_down=g_down.reshape(w_down.shape))
    delta, new_m, new_v = {}, {}, {}
    d_, m_, v_ = _adamw_whole("adamw_w_in", w_in[0].T, g_in_t, m_w_in[0].T, v_w_in[0].T)
    delta["w_in"], new_m["w_in"], new_v["w_in"] = (a.T.reshape(w_in.shape) for a in (d_, m_, v_))
    for n in ("w_glu", "w_out", "w_up", "w_down"):
        shp = given[n].shape
        two = lambda a: a.reshape(shp[1], shp[2])
        d_, m_, v_ = _adamw("adamw_" + n, two(given[n]), two(grads[n]), two(given["m_" + n]), two(given["v_" + n]), 256)
        delta[n], new_m[n], new_v[n] = d_.reshape(shp), m_.reshape(shp), v_.reshape(shp)
    for n in EARLY + LATE[:-1]:
        shp = given[n].shape
        if len(shp) == 4 and shp[-1] == 16:
            two = back = lambda a: jnp.swapaxes(a, -1, -2)
        else:
            two = (lambda a: a.reshape(1, -1)) if len(shp) == 1 else (lambda a: a)
            back = lambda a: a.reshape(shp)
        d_, m_, v_ = _adamw_whole("adamw_" + n, two(given[n]), two(grads[n].reshape(shp)), two(given["m_" + n]), two(given["v_" + n]))
        delta[n], new_m[n], new_v[n] = back(d_), back(m_), back(v_)

    order = ["meta_tokens", "g_mix", "w_in", "conv_w", "conv_b", "dt_bias", "a_log", "d_ssd", "g_ssd", "lam_re", "lam_im", "log_step",
             "b_re", "b_im", "c_re", "c_im", "d_s5", "w_glu", "b_glu", "g_s5", "w_out", "g_mlp", "w_up", "w_down", "g_final"]
    grads_out = [grads[n].reshape(given[n].shape) for n in order]
    return (loss, grad_x, *grads_out, *[delta[n] for n in order], *[new_m[n] for n in order], *[new_v[n] for n in order])
```

```python
import math

import jax
import jax.numpy as jnp
from jax import lax
from jax.experimental import pallas as pl
from jax.experimental.pallas import tpu as pltpu

f32 = jnp.float32
bf16 = jnp.bfloat16

D = 1024
N_META = 16
CH = 256
HEADS = 16
HEAD_DIM = 64
NSTATE = 128
D_XBC = 1536
S5_P = 64
NQ = 8
PITCH = CH + 4
EPS = 1e-5
O_Z, O_U, O_XBC, O_DT, W_PROJ = 0, 1024, 2048, 3584, 3712
VMEM_LIMIT = 60 * 1024 * 1024

ADAM_LR, ADAM_B1, ADAM_B2, ADAM_EPS, ADAM_WD, ADAM_STEP = 0.001, 0.9, 0.999, 1e-08, 0.01, 10

NT = (((1,), (1,)), ((), ()))
TN = (((0,), (0,)), ((), ()))
_ANY = pl.BlockSpec(memory_space=pl.ANY)


def _cp(sem=None):
    return pltpu.CompilerParams(dimension_semantics=sem, vmem_limit_bytes=VMEM_LIMIT)


def _sigmoid(v):
    return 1.0 / (1.0 + jnp.exp(-v))


def _rsum8(v):
    r, c = v.shape
    return jnp.sum(v.reshape(r // 8, 8, c), axis=0)


def _rms(h, g):
    r = lax.rsqrt(jnp.mean(h * h, axis=-1, keepdims=True) + EPS)
    return h * r * g


def _rms_bwd(dy, h, g):
    r = lax.rsqrt(jnp.mean(h * h, axis=-1, keepdims=True) + EPS)
    n = h * r
    dn = dy * g
    dh = r * (dn - n * jnp.mean(dn * n, axis=-1, keepdims=True))
    return dh, _rsum8(dy * n)


def _dot(a, b, dims=None):
    if dims is None:
        return jnp.dot(a, b, preferred_element_type=f32)
    return lax.dot_general(a, b, dims, preferred_element_type=f32)


def _split_dot(v, m01, dims, terms, v_is_lhs=True):
    out, r = None, v
    for _ in range(terms):
        piece = r.astype(bf16)
        o = _dot(piece, m01, dims) if v_is_lhs else _dot(m01, piece, dims)
        out = o if out is None else out + o
        r = r - piece.astype(f32)
    return out


MESH = pl.DeviceIdType.MESH


def _place():
    return lax.axis_index("x"), lax.axis_index("y"), lax.axis_index("c")


def _flip(v, f):
    return 1 - v if f else v


def _call(body, comm, *, name, nsteps, in_specs, out_specs, out_shape, scratch_shapes, args):
    n_in, n_out, n_scr = len(in_specs), len(out_specs), len(scratch_shapes)
    if comm is None:
        res = pl.pallas_call(body, name=name, grid=(nsteps,), in_specs=in_specs, out_specs=out_specs, out_shape=out_shape,
                             scratch_shapes=scratch_shapes, compiler_params=_cp(("arbitrary",)))(*args)
        return list(res), []
    c_in, c_out = len(comm["ins"]), len(comm["outs"])

    def wrapped(*refs):
        o0 = n_in + c_in
        s0 = o0 + n_out + c_out
        cparts = (refs[n_in:o0], refs[o0 + n_out:s0], refs[s0 + n_scr:])

        @pl.when(pl.program_id(0) == 0)
        def _():
            comm["start"](*cparts)
        if "middle" in comm:
            @pl.when(pl.program_id(0) == (3 * nsteps) // 4)
            def _():
                comm["middle"](*cparts)
        body(*refs[:n_in], *refs[o0:o0 + n_out], *refs[s0:s0 + n_scr])

        @pl.when(pl.program_id(0) == nsteps - 1)
        def _():
            comm["finish"](*cparts)

    any_spec = pl.BlockSpec(memory_space=pl.ANY)
    res = pl.pallas_call(
        wrapped, name=name, grid=(nsteps,), in_specs=list(in_specs) + [any_spec] * c_in,
        out_specs=list(out_specs) + [any_spec] * c_out, out_shape=list(out_shape) + list(comm["outs"]),
        scratch_shapes=list(scratch_shapes) + list(comm["scratch"]),
        compiler_params=_cp(("arbitrary",)))(*args, *comm["ins"])
    return list(res[:n_out]), list(res[n_out:])


def _gather_piece(slab, r0, rows):
    half = rows // 2
    flips = ((1, 0), (0, 1), (1, 1))

    def first(j, slab_ref, out_ref, send_sems, recv_sems):
        x, y, c = _place()
        return pltpu.make_async_remote_copy(
            src_ref=slab_ref.at[pl.ds(r0 + c * half, half), :], dst_ref=out_ref.at[2 * x + y, pl.ds(c * half, half), :],
            send_sem=send_sems.at[j], recv_sem=recv_sems.at[j],
            device_id=(_flip(x, flips[j][0]), _flip(y, flips[j][1]), c), device_id_type=MESH)

    def passed(j, out_ref, send_sems, recv_sems):
        x, y, c = _place()
        rows_j = out_ref.at[2 * _flip(x, flips[j][0]) + _flip(y, flips[j][1]), pl.ds(c * half, half), :]
        return pltpu.make_async_remote_copy(src_ref=rows_j, dst_ref=rows_j, send_sem=send_sems.at[3 + j],
                                            recv_sem=recv_sems.at[3 + j], device_id=(x, y, 1 - c), device_id_type=MESH)

    def start(ins, outs, scr):
        send_sems, recv_sems, stage, local_sems = scr
        x, y, _ = _place()
        load = pltpu.make_async_copy(ins[0].at[pl.ds(r0, rows), :], stage, local_sems.at[0])
        load.start()
        for j in range(3):
            first(j, ins[0], outs[0], send_sems, recv_sems).start()
        load.wait()
        pltpu.make_async_copy(stage, outs[0].at[2 * x + y], local_sems.at[1]).start()

    def middle(ins, outs, scr):
        send_sems, recv_sems, _, _ = scr
        for j in range(3):
            first(j, ins[0], outs[0], send_sems, recv_sems).wait_recv()
            passed(j, outs[0], send_sems, recv_sems).start()

    def finish(ins, outs, scr):
        send_sems, recv_sems, stage, local_sems = scr
        x, y, c = _place()
        for j in range(3):
            sib = outs[0].at[2 * _flip(x, flips[j][0]) + _flip(y, flips[j][1]), pl.ds((1 - c) * half, half), :]
            pltpu.make_async_remote_copy(src_ref=sib, dst_ref=sib, send_sem=send_sems.at[3 + j], recv_sem=recv_sems.at[3 + j],
                                         device_id=(x, y, 1 - c), device_id_type=MESH).wait_recv()
        for j in range(3):
            first(j, ins[0], outs[0], send_sems, recv_sems).wait_send()
            passed(j, outs[0], send_sems, recv_sems).wait_send()
        pltpu.make_async_copy(stage, outs[0].at[2 * x + y], local_sems.at[1]).wait()

    return dict(ins=[slab], outs=[jax.ShapeDtypeStruct((4, rows, D), bf16)],
                scratch=[pltpu.SemaphoreType.DMA((6,)), pltpu.SemaphoreType.DMA((6,)), pltpu.VMEM((rows, D), bf16),
                         pltpu.SemaphoreType.DMA((2,))], start=start, middle=middle, finish=finish)


def _scatter_piece(gpiece):
    half = gpiece.shape[1] // 2

    def copies(g_ref, land_ref, send_sems, recv_sems):
        x, y, c = _place()
        cps = []
        for fx in (0, 1):
            for fy in (0, 1):
                for fc in (0, 1):
                    k = 4 * fx + 2 * fy + fc - 1
                    if k < 0:
                        continue
                    px, py, pc = _flip(x, fx), _flip(y, fy), _flip(c, fc)
                    cps.append(pltpu.make_async_remote_copy(
                        src_ref=g_ref.at[2 * px + py, pl.ds(pc * half, half), :], dst_ref=land_ref.at[k],
                        send_sem=send_sems.at[k], recv_sem=recv_sems.at[k], device_id=(px, py, pc), device_id_type=MESH))
        return cps

    def start(ins, outs, scr):
        for cp in copies(ins[0], outs[0], *scr):
            cp.start()

    def finish(ins, outs, scr):
        for cp in copies(ins[0], outs[0], *scr):
            cp.wait()

    return dict(ins=[gpiece], outs=[jax.ShapeDtypeStruct((7, half, D), gpiece.dtype)],
                scratch=[pltpu.SemaphoreType.DMA((7,)), pltpu.SemaphoreType.DMA((7,))], start=start, finish=finish)


def _gather_blocks(block):
    rows = block.shape[0]

    def mine(out_ref):
        x, y, c = _place()
        return out_ref.at[pl.ds((4 * x + 2 * y + c) * rows, rows), :]

    def copies(b_ref, out_ref, send_sems, recv_sems):
        x, y, c = _place()
        cps = []
        for fx in (0, 1):
            for fy in (0, 1):
                for fc in (0, 1):
                    k = 4 * fx + 2 * fy + fc - 1
                    if k < 0:
                        continue
                    cps.append(pltpu.make_async_remote_copy(
                        src_ref=b_ref, dst_ref=mine(out_ref), send_sem=send_sems.at[k], recv_sem=recv_sems.at[k],
                        device_id=(_flip(x, fx), _flip(y, fy), _flip(c, fc)), device_id_type=MESH))
        return cps

    def start(ins, outs, scr):
        send_sems, recv_sems, stage, local_sems = scr
        load = pltpu.make_async_copy(ins[0], stage, local_sems.at[0])
        load.start()
        for cp in copies(ins[0], outs[0], send_sems, recv_sems):
            cp.start()
        load.wait()
        pltpu.make_async_copy(stage, mine(outs[0]), local_sems.at[1]).start()

    def finish(ins, outs, scr):
        send_sems, recv_sems, stage, local_sems = scr
        for cp in copies(ins[0], outs[0], send_sems, recv_sems):
            cp.wait()
        pltpu.make_async_copy(stage, mine(outs[0]), local_sems.at[1]).wait()

    return dict(ins=[block], outs=[jax.ShapeDtypeStruct((8 * rows, D), block.dtype)],
                scratch=[pltpu.SemaphoreType.DMA((7,)), pltpu.SemaphoreType.DMA((7,)), pltpu.VMEM((rows, D), block.dtype),
                         pltpu.SemaphoreType.DMA((2,))], start=start, finish=finish)


def _both(c1, c2):
    n = (len(c1["ins"]), len(c1["outs"]), len(c1["scratch"]))

    def split(parts):
        return [p[:k] for p, k in zip(parts, n)], [p[k:] for p, k in zip(parts, n)]

    def start(*parts):
        a, b = split(parts)
        c1["start"](*a)
        c2["start"](*b)

    def finish(*parts):
        a, b = split(parts)
        c1["finish"](*a)
        c2["finish"](*b)

    both = dict(ins=c1["ins"] + c2["ins"], outs=c1["outs"] + c2["outs"], scratch=c1["scratch"] + c2["scratch"],
                start=start, finish=finish)
    if "middle" in c1 or "middle" in c2:
        def middle(*parts):
            for cm, part in zip((c1, c2), split(parts)):
                if "middle" in cm:
                    cm["middle"](*part)
        both["middle"] = middle
    return both


def _rowwise(name, fn, nblk, rows=(), shifted=(), pars=(), refs=(), out_rows=(), out_accs=(), bm=CH, comm=None):
    n_sub = bm // CH
    n_r, n_s, n_p, n_w = len(rows), len(shifted) * n_sub, len(pars), len(refs)
    n_in = n_r + n_s + n_p + n_w
    n_o, n_a = len(out_rows), len(out_accs)

    def body(*all_refs):
        i = pl.program_id(0)
        ins = all_refs[:n_in]
        outs = all_refs[n_in:]
        rv = [r[...] for r in ins[:n_r]]
        sub = ins[n_r:n_r + n_s]
        sv = [jnp.concatenate([r[...] for r in sub[k * n_sub:(k + 1) * n_sub]], axis=0) if n_sub > 1 else sub[k][...]
              for k in range(len(shifted))]
        pv = [r[...] for r in ins[n_r + n_s:n_r + n_s + n_p]]
        ro, ao = fn(i, rv, sv, pv, list(ins[n_r + n_s + n_p:]))
        for r, v in zip(outs[:n_o], ro):
            r[...] = v.astype(r.dtype)
        accs = outs[n_o:]

        @pl.when(i == 0)
        def _():
            for r in accs:
                r[...] = jnp.zeros_like(r)
        for r, v in zip(accs, ao):
            r[...] += v

    in_specs = [pl.BlockSpec((bm, a.shape[1]), lambda i: (i, 0)) for a in rows]
    in_specs += [pl.BlockSpec((CH, a.shape[1]), lambda i, j=j: (jnp.maximum(n_sub * i - 1 + j, 0), 0))
                 for a in shifted for j in range(n_sub)]
    in_specs += [pl.BlockSpec(a.shape, lambda i, nd=a.ndim: (0,) * nd) for a in pars]
    in_specs += [spec for _, spec in refs]
    out_specs = [pl.BlockSpec((bm, c), lambda i: (i, 0)) for c, _ in out_rows]
    out_specs += [pl.BlockSpec((8, c), lambda i: (0, 0)) for c in out_accs]
    out_shape = [jax.ShapeDtypeStruct((nblk * bm, c), dt) for c, dt in out_rows]
    out_shape += [jax.ShapeDtypeStruct((8, c), f32) for c in out_accs]
    res, cres = _call(body, comm, name=name, nsteps=nblk, in_specs=in_specs, out_specs=out_specs, out_shape=out_shape,
                      scratch_shapes=[], args=[*rows, *[a for a in shifted for _ in range(n_sub)], *pars, *[a for a, _ in refs]])
    parts = (res[:n_o], res[n_o:])
    return parts if comm is None else parts + (cres,)


PACK_ROWS = 4096
HALF_ROWS = PACK_ROWS // 2
R_UP, R_DOWN, R_GLU, R_OUT, R_IN, R_SPARE = 0, 1024, 2048, 2560, 3072, 3972


C_ROWS = 928
BIG_ROWS = 768
DW_ROWS = 2816


def _contract_rows(lp, big=DW_ROWS):
    for rows in (big, BIG_ROWS):
        if lp % rows == 0:
            return rows
    return CH


def _dw_into(name, a, b, slab, ka, a_sharded, row_blk, n_s, s0, piece_rows=2048, a_square=False):
    lp = a.shape[0]
    bm = _contract_rows(lp)
    steps = lp // bm

    def body(a_ref, b_ref, *rest):
        o_ref, acc = rest[-2], rest[-1]
        k = pl.program_id(1)

        @pl.when(k == 0)
        def _():
            acc[...] = jnp.zeros_like(acc)
        a_v = a_ref[...]
        acc[...] += _dot(a_v * a_v if a_square else a_v, b_ref[...], TN)

        @pl.when(k == steps - 1)
        def _():
            o_ref[0] = acc[...].astype(bf16)

    in_specs = [pl.BlockSpec((bm, ka), (lambda s, k: (k, s)) if a_sharded else (lambda s, k: (k, 0))),
                pl.BlockSpec((bm, D), (lambda s, k: (k, 0)) if a_sharded else (lambda s, k: (k, s)))]
    args = [a, b]
    aliases = {}
    if slab is not None:
        in_specs.append(_ANY)
        args.append(slab)
        aliases = {2: 0}
    return pl.pallas_call(
        body, name=name, grid=(n_s, steps), in_specs=in_specs,
        out_specs=pl.BlockSpec((1, ka, D), lambda s, k: (s0 + s, row_blk, 0)),
        out_shape=jax.ShapeDtypeStruct((4, piece_rows, D), bf16),
        scratch_shapes=[pltpu.VMEM((ka, D), f32)], input_output_aliases=aliases,
        compiler_params=_cp(("arbitrary", "arbitrary")))(*args)


def _dw_in_t(dproj, n0):
    lp = n0.shape[0]
    bm = _contract_rows(lp, BIG_ROWS)
    steps = lp // bm
    bn = 512

    def body(a_ref, b_ref, o_ref, acc):
        k = pl.program_id(1)

        @pl.when(k == 0)
        def _():
            acc[...] = jnp.zeros_like(acc)
        acc[...] += _dot(a_ref[...], b_ref[...], TN)

        @pl.when(k == steps - 1)
        def _():
            o_ref[...] = acc[...].astype(bf16)

    return pl.pallas_call(
        body, name="dw_in", grid=(D // bn, steps),
        in_specs=[pl.BlockSpec((bm, W_PROJ), lambda j, k: (k, 0)), pl.BlockSpec((bm, bn), lambda j, k: (k, j))],
        out_specs=pl.BlockSpec((W_PROJ, bn), lambda j, k: (0, j)),
        out_shape=jax.ShapeDtypeStruct((W_PROJ, D), bf16),
        scratch_shapes=[pltpu.VMEM((W_PROJ, bn), f32)],
        compiler_params=_cp(("arbitrary", "arbitrary")))(dproj, n0)


def _head_expand():
    h = lax.broadcasted_iota(jnp.int32, (HEADS, D), 0)
    c = lax.broadcasted_iota(jnp.int32, (HEADS, D), 1)
    return jnp.where((c >> 6) == h, 1.0, 0.0).astype(bf16)


def _ssd_common(i, P, prev8, cw, cb, dtb, alog, xc=None):
    z = P[:, O_Z:O_Z + D]
    xp = P[:, O_XBC:O_XBC + D_XBC]
    dt_raw = P[:, O_DT:O_DT + HEADS]
    row = lax.broadcasted_iota(jnp.int32, (CH, 1), 0)
    if xc is None:
        row8 = lax.broadcasted_iota(jnp.int32, (8, 1), 0)
        xc = cb + cw[3:4] * xp
        for k in (1, 2, 3):
            rolled = pltpu.roll(xp, k, 0)
            fix = pltpu.roll(prev8, k, 0)
            top = jnp.where(row8 < k, fix, rolled[0:8])
            xc = xc + cw[3 - k:4 - k] * jnp.concatenate([top, rolled[8:]], axis=0)
    sg = _sigmoid(xc)
    xbc = xc * sg
    live = jnp.where(jnp.logical_or(i > 0, row >= CH - N_META), 1.0, 0.0)
    pre = dt_raw + dtb
    dt = jnp.where(pre > 20.0, pre, jnp.log(1.0 + jnp.exp(jnp.minimum(pre, 20.0)))) * live
    a = -jnp.exp(alog)
    dta = dt * a
    r_i = lax.broadcasted_iota(jnp.int32, (CH, CH), 0)
    c_i = lax.broadcasted_iota(jnp.int32, (CH, CH), 1)
    tril = r_i >= c_i
    acs = _split_dot(dta, jnp.where(tril, 1.0, 0.0).astype(bf16), None, 3, v_is_lhs=False)
    acs_t = _split_dot(dta, jnp.where(r_i <= c_i, 1.0, 0.0).astype(bf16), TN, 3)
    e = _head_expand()
    acs_e = _split_dot(acs, e, None, 3)
    dt_e = _split_dot(dt, e, None, 3)
    return dict(z=z, xp=xp, xc=xc, sg=sg, xbc=xbc, live=live, pre=pre, dt=dt, a=a, tril=tril,
                acs=acs, acs_t=acs_t, e=e, acs_e=acs_e, dt_e=dt_e)


def _lmat(c, h):
    seg = c["acs"][:, h:h + 1] - c["acs_t"][h:h + 1, :]
    return jnp.where(c["tril"], jnp.exp(jnp.minimum(seg, 0.0)), 0.0)


def _pair_masks():
    lane = lax.broadcasted_iota(jnp.int32, (1, 128), 1)
    return jnp.where(lane < HEAD_DIM, 1.0, 0.0), jnp.where(lane >= HEAD_DIM, 1.0, 0.0)


def _ssd_fwd(proj, conv_w, conv_b, dt_bias, a_log, d_ssd, g_ssd, nch, comm=None):
    def body(p_ref, cw_ref, cb_ref, dtb_ref, al_ref, d_ref, g_ref, y_ref, ys_ref, st_ref, xc_ref, prev8_ref, state_ref):
        i = pl.program_id(0)

        @pl.when(i == 0)
        def _():
            prev8_ref[...] = jnp.zeros_like(prev8_ref)
            state_ref[...] = jnp.zeros_like(state_ref)

        P = p_ref[...]
        c = _ssd_common(i, P, prev8_ref[...], cw_ref[...], cb_ref[...], dtb_ref[...], al_ref[...])
        prev8_ref[...] = c["xp"][CH - 8:CH]
        xc_ref[...] = c["xc"]
        xbc = c["xbc"]
        x = xbc[:, 0:D]
        xdt = x * c["dt_e"]
        a_last_e = c["acs_e"][CH - 1:CH, :]
        w_end = (xdt * jnp.exp(a_last_e - c["acs_e"])).astype(bf16)
        m0, m1 = _pair_masks()
        ys = []
        for g in range(2):
            bg = xbc[:, D + NSTATE * g:D + NSTATE * (g + 1)].astype(bf16)
            cg = xbc[:, D + 2 * NSTATE + NSTATE * g:D + 2 * NSTATE + NSTATE * (g + 1)].astype(bf16)
            gmat = _dot(cg, bg, NT)
            st = state_ref[g]
            st_ref[0, g] = st
            sl = slice(512 * g, 512 * (g + 1))
            y_off = _dot(cg, st.astype(bf16)) * jnp.exp(c["acs_e"][:, sl])
            contrib = _dot(bg, w_end[:, sl], TN)
            state_ref[g] = st * jnp.exp(a_last_e[:, sl]) + contrib
            yd = []
            for pr in range(4):
                h0 = 8 * g + 2 * pr
                xp2 = xdt[:, 128 * (4 * g + pr):128 * (4 * g + pr + 1)]
                ma = (gmat * _lmat(c, h0)).astype(bf16)
                mb = (gmat * _lmat(c, h0 + 1)).astype(bf16)
                yd.append(_dot(ma, (xp2 * m0).astype(bf16)) + _dot(mb, (xp2 * m1).astype(bf16)))
            ys.append(jnp.concatenate(yd, axis=1) + y_off)
        d_e = _split_dot(d_ref[...], c["e"], None, 3)
        y = jnp.concatenate(ys, axis=1) + x * d_e
        y_ref[...] = y
        yg = y * (c["z"] * _sigmoid(c["z"]))
        ys_ref[...] = _rms(yg, g_ref[...]).astype(bf16)

    full = lambda a: pl.BlockSpec(a.shape, lambda i, nd=a.ndim: (0,) * nd)
    return _call(
        body, comm, name="ssd_fwd", nsteps=nch,
        in_specs=[pl.BlockSpec((CH, W_PROJ), lambda i: (i, 0))] + [full(a) for a in (conv_w, conv_b, dt_bias, a_log, d_ssd, g_ssd)],
        out_specs=[pl.BlockSpec((CH, D), lambda i: (i, 0)), pl.BlockSpec((CH, D), lambda i: (i, 0)),
                   pl.BlockSpec((1, 2, NSTATE, 512), lambda i: (i, 0, 0, 0)), pl.BlockSpec((CH, D_XBC), lambda i: (i, 0))],
        out_shape=[jax.ShapeDtypeStruct((nch * CH, D), f32), jax.ShapeDtypeStruct((nch * CH, D), bf16),
                   jax.ShapeDtypeStruct((nch, 2, NSTATE, 512), f32), jax.ShapeDtypeStruct((nch * CH, D_XBC), f32)],
        scratch_shapes=[pltpu.VMEM((8, D_XBC), f32), pltpu.VMEM((2, NSTATE, 512), f32)],
        args=[proj, conv_w, conv_b, dt_bias, a_log, d_ssd, g_ssd])


def _ssd_bwd(proj, xc_all, y, dys, du, states, conv_w, conv_b, dt_bias, a_log, d_ssd, g_ssd, nch, comm=None):
    def body(p_ref, xc_ref, y_ref, dys_ref, du_ref, st_ref, cw_ref, cb_ref, dtb_ref, al_ref, d_ref, g_ref,
             dp_ref, dcw_ref, dcb_ref, ddtb_ref, dal_ref, dd_ref, dg_ref, nxt8_ref, dst_ref):
        step = pl.program_id(0)
        i = nch - 1 - step

        @pl.when(step == 0)
        def _():
            nxt8_ref[...] = jnp.zeros_like(nxt8_ref)
            dst_ref[...] = jnp.zeros_like(dst_ref)
            for r in (dcw_ref, dcb_ref, ddtb_ref, dal_ref, dd_ref, dg_ref):
                r[...] = jnp.zeros_like(r)

        P = p_ref[...]
        c = _ssd_common(i, P, None, cw_ref[...], cb_ref[...], dtb_ref[...], al_ref[...], xc=xc_ref[...])
        xbc, z, e = c["xbc"], c["z"], c["e"]
        x = xbc[:, 0:D]
        yv = y_ref[...]
        sz = _sigmoid(z)
        silu_z = z * sz
        dyg, dg8 = _rms_bwd(dys_ref[...], yv * silu_z, g_ref[...])
        dg_ref[...] += dg8
        dy = dyg * silu_z
        dz = dyg * yv * (sz * (1.0 + z * (1.0 - sz)))
        d_e = _split_dot(d_ref[...], e, None, 3)
        dd_ref[...] += _rsum8(dy * x)
        xdt = x * c["dt_e"]
        a_last_e = c["acs_e"][CH - 1:CH, :]
        e_end = jnp.exp(a_last_e - c["acs_e"])
        w_end = xdt * e_end
        e_acs = jnp.exp(c["acs_e"])
        dy_dec = dy * e_acs
        m0, m1 = _pair_masks()
        lane16 = lax.broadcasted_iota(jnp.int32, (1, HEADS), 1)
        row16 = lax.broadcasted_iota(jnp.int32, (HEADS, 1), 0)
        dacs = jnp.zeros((CH, HEADS), f32)
        dacs_t = jnp.zeros((HEADS, CH), f32)
        dxdt_parts, dbs, dcs, zparts, yoff_parts, dlast_parts = [], [], [], [], [], []
        for g in range(2):
            sl = slice(512 * g, 512 * (g + 1))
            bg = xbc[:, D + NSTATE * g:D + NSTATE * (g + 1)].astype(bf16)
            cg = xbc[:, D + 2 * NSTATE + NSTATE * g:D + 2 * NSTATE + NSTATE * (g + 1)].astype(bf16)
            gmat = _dot(cg, bg, NT)
            st = st_ref[0, g]
            dstn = dst_ref[g]
            dstn_b = dstn.astype(bf16)
            y_off = _dot(cg, st.astype(bf16)) * e_acs[:, sl]
            yoff_parts.append(y_off)
            bds = _dot(bg, dstn_b)
            zparts.append(w_end[:, sl] * bds)
            dlast_parts.append(jnp.sum(dstn * st, axis=0, keepdims=True) * jnp.exp(a_last_e[:, sl]))
            dg_acc = jnp.zeros((CH, CH), f32)
            dxd = []
            for pr in range(4):
                lo = 128 * (4 * g + pr)
                xp2 = xdt[:, lo:lo + 128].astype(bf16)
                dy2 = dy[:, lo:lo + 128]
                outp = jnp.zeros((CH, 128), f32)
                for hh, msk in ((0, m0), (1, m1)):
                    h = 8 * g + 2 * pr + hh
                    lm = _lmat(c, h)
                    dyh = (dy2 * msk).astype(bf16)
                    mh = (gmat * lm).astype(bf16)
                    outp = outp + _dot(mh, dyh, TN)
                    dml = _dot(dyh, xp2, NT) * lm
                    dg_acc = dg_acc + dml
                    q = dml * gmat
                    dacs = dacs + jnp.where(lane16 == h, jnp.sum(q, axis=1, keepdims=True), 0.0)
                    dacs_t = dacs_t + jnp.where(row16 == h, jnp.sum(q, axis=0, keepdims=True), 0.0)
                dxd.append(outp)
            dxdt_parts.append(jnp.concatenate(dxd, axis=1) + e_end[:, sl] * bds)
            dgb = dg_acc.astype(bf16)
            dcs.append(_dot(dgb, bg) + _dot(dy_dec[:, sl].astype(bf16), st.astype(bf16), NT))
            dbs.append(_dot(dgb, cg, TN) + _dot(w_end[:, sl].astype(bf16), dstn_b, NT))
            dst_ref[g] = dstn * jnp.exp(a_last_e[:, sl]) + _dot(cg, dy_dec[:, sl].astype(bf16), TN)
        dxdt = jnp.concatenate(dxdt_parts, axis=1)
        zfull = jnp.concatenate(zparts, axis=1)
        y_off_full = jnp.concatenate(yoff_parts, axis=1)
        dlast = jnp.concatenate(dlast_parts, axis=1)
        red = lambda v: _split_dot(v, e, NT, 2)
        eye16 = jnp.where(lax.broadcasted_iota(jnp.int32, (HEADS, HEADS), 0) == lax.broadcasted_iota(jnp.int32, (HEADS, HEADS), 1),
                          1.0, 0.0).astype(bf16)
        dacs = dacs - _split_dot(dacs_t, eye16, TN, 3)
        zred = red(zfull)
        dacs = dacs + red(dy * y_off_full) - zred
        last_term = jnp.sum(zred, axis=0, keepdims=True) + red(dlast)
        rowc = lax.broadcasted_iota(jnp.int32, (CH, 1), 0)
        dacs = dacs + jnp.where(rowc == CH - 1, last_term, 0.0)
        r_i = lax.broadcasted_iota(jnp.int32, (CH, CH), 0)
        c_i = lax.broadcasted_iota(jnp.int32, (CH, CH), 1)
        ddta = _split_dot(dacs, jnp.where(c_i >= r_i, 1.0, 0.0).astype(bf16), None, 3, v_is_lhs=False)
        ddt = ddta * c["a"] + red(dxdt * x)
        dal_ref[...] += _rsum8(ddta * c["dt"] * c["a"])
        ddt_raw = ddt * _sigmoid(c["pre"]) * c["live"]
        ddtb_ref[...] += _rsum8(ddt_raw)
        dx = dy * d_e + dxdt * c["dt_e"]
        dxbc = jnp.concatenate([dx, dbs[0], dbs[1], dcs[0], dcs[1]], axis=1)
        sg = c["sg"]
        dxc = dxbc * (sg * (1.0 + c["xc"] * (1.0 - sg)))
        dcb_ref[...] += _rsum8(dxc)
        xp = c["xp"]
        row8 = lax.broadcasted_iota(jnp.int32, (8, 1), 0)
        cw = cw_ref[...]
        dxp = cw[3:4] * dxc
        dcw = jnp.where(row8 == 3, jnp.sum(dxc * xp, axis=0, keepdims=True), 0.0)
        nxt8 = nxt8_ref[...]
        for j in (1, 2, 3):
            rolled = pltpu.roll(dxc, CH - j, 0)
            fix = pltpu.roll(nxt8, 8 - j, 0)
            bot = jnp.where(row8 >= 8 - j, fix, rolled[CH - 8:CH])
            later = jnp.concatenate([rolled[:CH - 8], bot], axis=0)
            dxp = dxp + cw[3 - j:4 - j] * later
            dcw = dcw + jnp.where(row8 == 3 - j, jnp.sum(later * xp, axis=0, keepdims=True), 0.0)
        dcw_ref[...] += dcw
        nxt8_ref[...] = dxc[0:8]
        dp_ref[:, O_Z:O_Z + D] = dz.astype(bf16)
        dp_ref[:, O_U:O_U + D] = du_ref[...].astype(bf16)
        dp_ref[:, O_XBC:O_XBC + D_XBC] = dxp.astype(bf16)
        dp_ref[:, O_DT:W_PROJ] = jnp.zeros((CH, W_PROJ - O_DT), bf16)
        dp_ref[:, O_DT:O_DT + HEADS] = ddt_raw.astype(bf16)

    full = lambda a: pl.BlockSpec(a.shape, lambda s, nd=a.ndim: (0,) * nd)
    rev = lambda s: (nch - 1 - s, 0)
    acc = lambda cdim: pl.BlockSpec((8, cdim), lambda s: (0, 0))
    return _call(
        body, comm, name="ssd_bwd", nsteps=nch,
        in_specs=[pl.BlockSpec((CH, W_PROJ), rev), pl.BlockSpec((CH, D_XBC), rev),
                  pl.BlockSpec((CH, D), rev), pl.BlockSpec((CH, D), rev), pl.BlockSpec((CH, D), rev),
                  pl.BlockSpec((1, 2, NSTATE, 512), lambda s: (nch - 1 - s, 0, 0, 0))]
        + [full(a) for a in (conv_w, conv_b, dt_bias, a_log, d_ssd, g_ssd)],
        out_specs=[pl.BlockSpec((CH, W_PROJ), rev), acc(D_XBC), acc(D_XBC), acc(HEADS), acc(HEADS), acc(D), acc(D)],
        out_shape=[jax.ShapeDtypeStruct((nch * CH, W_PROJ), bf16)]
        + [jax.ShapeDtypeStruct((8, cdim), f32) for cdim in (D_XBC, D_XBC, HEADS, HEADS, D, D)],
        scratch_shapes=[pltpu.VMEM((8, D_XBC), f32), pltpu.VMEM((2, NSTATE, 512), f32)],
        args=[proj, xc_all, y, dys, du, states, conv_w, conv_b, dt_bias, a_log, d_ssd, g_ssd])


SCAN_UNROLL = 8


def _to_slabs(slab_ref, q, mat):
    for ls in range(8):
        slab_ref[ls, pl.ds(PITCH * q, CH), :] = mat[:, 128 * ls:128 * (ls + 1)]


def _from_slabs(slab, q):
    return jnp.concatenate([slab(ls, PITCH * q) for ls in range(8)], axis=1)


def _tile(slab_ref, ls, t, lead=None):
    idx = (ls, pl.ds(t, 8, stride=PITCH), slice(None))
    return slab_ref[idx] if lead is None else slab_ref[(lead,) + idx]


def _s5_fwd(proj, bbq, ccq, ar, ai, d_skip, nch, comm=None):
    def body(u_ref, bb_ref, cc_ref, ar_ref, ai_ref, d_ref, s_ref, yl_ref, y5_ref, bu_ref, st_ref):
        @pl.when(pl.program_id(0) == 0)
        def _():
            st_ref[...] = jnp.zeros_like(st_ref)
        u = u_ref[...]
        ub = u.astype(bf16)
        for q in range(NQ):
            _to_slabs(bu_ref, q, _dot(ub[:, 128 * q:128 * (q + 1)], bb_ref[q]))
        ar_t = [ar_ref[:, 128 * l:128 * (l + 1)] for l in range(4)]
        ai_t = [ai_ref[:, 128 * l:128 * (l + 1)] for l in range(4)]

        def one(t, carry):
            re, im = carry
            nre, nim = [], []
            for l in range(4):
                a = ar_t[l] * re[l] - ai_t[l] * im[l] + _tile(bu_ref, l, t)
                b = ar_t[l] * im[l] + ai_t[l] * re[l] + _tile(bu_ref, l + 4, t)
                s_ref[0, l, pl.ds(t, 8, stride=PITCH), :] = a
                s_ref[0, l + 4, pl.ds(t, 8, stride=PITCH), :] = b
                nre.append(a)
                nim.append(b)
            return tuple(nre), tuple(nim)

        def step(tt, carry):
            for k in range(SCAN_UNROLL):
                carry = one(tt * SCAN_UNROLL + k, carry)
            return carry
        init = (tuple(st_ref[l] for l in range(4)), tuple(st_ref[l + 4] for l in range(4)))
        re, im = lax.fori_loop(0, CH // SCAN_UNROLL, step, init)
        for l in range(4):
            st_ref[l] = re[l]
            st_ref[l + 4] = im[l]
        ys = []
        for q in range(NQ):
            sq = _from_slabs(lambda ls, r0: s_ref[0, ls, pl.ds(r0, CH), :], q).astype(bf16)
            ys.append(_dot(sq, cc_ref[q], NT))
        yl = jnp.concatenate(ys, axis=1) + u * d_ref[...]
        yl_ref[...] = yl
        y5_ref[...] = (0.5 * yl * (1.0 + lax.erf(yl * (1.0 / math.sqrt(2.0))))).astype(bf16)

    const = lambda a: pl.BlockSpec(a.shape, lambda i, nd=a.ndim: (0,) * nd)
    return _call(
        body, comm, name="s5_fwd", nsteps=nch,
        in_specs=[pl.BlockSpec((CH, D), lambda i: (i, O_U // D)), const(bbq), const(ccq), const(ar), const(ai), const(d_skip)],
        out_specs=[pl.BlockSpec((1, 8, 8 * PITCH, 128), lambda i: (i, 0, 0, 0)),
                   pl.BlockSpec((CH, D), lambda i: (i, 0)), pl.BlockSpec((CH, D), lambda i: (i, 0))],
        out_shape=[jax.ShapeDtypeStruct((nch, 8, 8 * PITCH, 128), f32), jax.ShapeDtypeStruct((nch * CH, D), f32),
                   jax.ShapeDtypeStruct((nch * CH, D), bf16)],
        scratch_shapes=[pltpu.VMEM((8, 8 * PITCH, 128), f32), pltpu.VMEM((8, 8, 128), f32)],
        args=[proj, bbq, ccq, ar, ai, d_skip])


def _s5_bwd(proj, dyl, s_all, bbq, ccq, ar, ai, d_skip, nch, comm=None):
    def body(u_ref, dy_ref, s_ref, bbt_ref, cct_ref, ar_ref, ai_ref, d_ref,
             du_ref, dcc_ref, dbb_ref, dab_ref, dd_ref, ga_ref, st_ref):
        @pl.when(pl.program_id(0) == 0)
        def _():
            st_ref[...] = jnp.zeros_like(st_ref)
            for r in (dcc_ref, dbb_ref, dab_ref, dd_ref):
                r[...] = jnp.zeros_like(r)
        u = u_ref[...]
        dyl_v = dy_ref[...]
        dd_ref[...] += _rsum8(dyl_v * u)
        ub = u.astype(bf16)
        dyb = dyl_v.astype(bf16)
        for q in range(NQ):
            _to_slabs(ga_ref, q, _dot(dyb[:, 128 * q:128 * (q + 1)], cct_ref[q]))
        ar_t = [ar_ref[:, 128 * l:128 * (l + 1)] for l in range(4)]
        ai_t = [ai_ref[:, 128 * l:128 * (l + 1)] for l in range(4)]

        def one(t, carry):
            re, im, dar, dai = carry
            nre, nim, ndar, ndai = [], [], [], []
            for l in range(4):
                sre = _tile(s_ref, l, t, lead=0)
                sim = _tile(s_ref, l + 4, t, lead=0)
                ndar.append(dar[l] + re[l] * sre + im[l] * sim)
                ndai.append(dai[l] + im[l] * sre - re[l] * sim)
                a = _tile(ga_ref, l, t) + ar_t[l] * re[l] + ai_t[l] * im[l]
                b = _tile(ga_ref, l + 4, t) - ai_t[l] * re[l] + ar_t[l] * im[l]
                ga_ref[l, pl.ds(t, 8, stride=PITCH), :] = a
                ga_ref[l + 4, pl.ds(t, 8, stride=PITCH), :] = b
                nre.append(a)
                nim.append(b)
            return tuple(nre), tuple(nim), tuple(ndar), tuple(ndai)

        def step(tt, carry):
            for k in range(SCAN_UNROLL):
                carry = one(CH - 1 - (tt * SCAN_UNROLL + k), carry)
            return carry
        four = lambda ref, o: tuple(ref[l + o] for l in range(4))
        re, im, dar, dai = lax.fori_loop(0, CH // SCAN_UNROLL, step,
                                         (four(st_ref, 0), four(st_ref, 4), four(dab_ref, 0), four(dab_ref, 4)))
        for l in range(4):
            st_ref[l], st_ref[l + 4] = re[l], im[l]
            dab_ref[l], dab_ref[l + 4] = dar[l], dai[l]
        dus = []
        for q in range(NQ):
            aq = _from_slabs(lambda ls, r0: ga_ref[ls, pl.ds(r0, CH), :], q).astype(bf16)
            sq = _from_slabs(lambda ls, r0: s_ref[0, ls, pl.ds(r0, CH), :], q).astype(bf16)
            dcc_ref[q] += _dot(dyb[:, 128 * q:128 * (q + 1)], sq, TN)
            dbb_ref[q] += _dot(ub[:, 128 * q:128 * (q + 1)], aq, TN)
            dus.append(_dot(aq, bbt_ref[q], NT))
        du_ref[...] = jnp.concatenate(dus, axis=1) + dyl_v * d_ref[...]

    const = lambda a: pl.BlockSpec(a.shape, lambda s, nd=a.ndim: (0,) * nd)
    rev = lambda s: (nch - 1 - s, 0)
    return _call(
        body, comm, name="s5_bwd", nsteps=nch,
        in_specs=[pl.BlockSpec((CH, D), lambda s: (nch - 1 - s, O_U // D)), pl.BlockSpec((CH, D), rev),
                  pl.BlockSpec((1, 8, 8 * PITCH, 128), lambda s: (nch - 1 - s, 0, 0, 0)),
                  const(bbq), const(ccq), const(ar), const(ai), const(d_skip)],
        out_specs=[pl.BlockSpec((CH, D), rev), pl.BlockSpec((NQ, 128, D), lambda s: (0, 0, 0)),
                   pl.BlockSpec((NQ, 128, D), lambda s: (0, 0, 0)), pl.BlockSpec((8, 8, 128), lambda s: (0, 0, 0)),
                   pl.BlockSpec((8, D), lambda s: (0, 0))],
        out_shape=[jax.ShapeDtypeStruct((nch * CH, D), f32), jax.ShapeDtypeStruct((NQ, 128, D), f32),
                   jax.ShapeDtypeStruct((NQ, 128, D), f32), jax.ShapeDtypeStruct((8, 8, 128), f32),
                   jax.ShapeDtypeStruct((8, D), f32)],
        scratch_shapes=[pltpu.VMEM((8, 8 * PITCH, 128), f32), pltpu.VMEM((8, 8, 128), f32)],
        args=[proj, dyl, s_all, bbq, ccq, ar, ai, d_skip])


def _s5_tables(lam_re, lam_im, log_step, b_re, b_im):
    step = jnp.exp(log_step)[:, None]
    mag = jnp.exp(lam_re * step)
    ab_re = mag * jnp.cos(lam_im * step)
    ab_im = mag * jnp.sin(lam_im * step)
    den = lam_re * lam_re + lam_im * lam_im
    coef_re = ((ab_re - 1.0) * lam_re + ab_im * lam_im) / den
    coef_im = (ab_im * lam_re - (ab_re - 1.0) * lam_im) / den
    bb_re = coef_re[..., None] * b_re - coef_im[..., None] * b_im
    bb_im = coef_re[..., None] * b_im + coef_im[..., None] * b_re
    return ab_re, ab_im, bb_re, bb_im


def _blockdiag_in(m_re, m_im):
    eye = jnp.eye(8, dtype=f32)

    def one(m):
        m = m.reshape(NQ, 8, S5_P, 16)
        return jnp.einsum("qgph,gk->qghkp", m, eye).reshape(NQ, 128, 512)
    return jnp.concatenate([one(m_re), one(m_im)], axis=2)


def _blockdiag_in_grad(dm):
    def one(x):
        x = x.reshape(NQ, 8, 16, 8, S5_P)
        return jnp.einsum("qghgp->qgph", x).reshape(NQ * 8, S5_P, 16)
    return one(dm[:, :, :512]), one(dm[:, :, 512:])


def _local_step(x2, tgt2, meta, p, w_in_t, slab):
    seq = x2.shape[0]
    nch = 1 + seq // CH
    bmb = BIG_ROWS if (nch * CH) % BIG_ROWS == 0 else CH
    nbig = nch * CH // bmb
    metablk = jnp.concatenate([jnp.zeros((CH - N_META, D), f32), meta, jnp.zeros((bmb - CH, D), f32)], axis=0)
    w_full = (w_in_t, pl.BlockSpec(w_in_t.shape, lambda i: (0, 0), pipeline_mode=pl.Buffered(1)))

    def lead(i, v):
        return jnp.logical_and(i == 0, lax.broadcasted_iota(jnp.int32, (v.shape[0], 1), 0) < CH)
    h0_of = lambda i, s, q: jnp.where(lead(i, s[0]), q[0][:s[0].shape[0]], s[0])

    def in_fn(i, r, s, q, w):
        nb = _rms(h0_of(i, s, q), q[1]).astype(bf16)
        return [_dot(nb, w[0][...], NT), nb], []
    (proj, n0), _, (g_up,) = _rowwise("in_proj", in_fn, nbig, shifted=[x2], pars=[metablk, p["g_mix"]], refs=[w_full],
                                      out_rows=[(W_PROJ, f32), (D, bf16)], bm=bmb, comm=_gather_piece(slab, R_UP, 1024))
    (y, y_ssd, states, xc_all), (g_down,) = _ssd_fwd(proj, p["conv_w"], p["conv_b"], p["dt_bias"], p["a_log"], p["d_ssd"],
                                                     p["g_ssd"], nch, comm=_gather_piece(slab, R_DOWN, 1024))

    ab_re, ab_im, bb_re, bb_im = _s5_tables(p["lam_re"], p["lam_im"], p["log_step"], p["b_re"], p["b_im"])
    ar, ai = ab_re.reshape(NQ, 512), ab_im.reshape(NQ, 512)
    bbq = _blockdiag_in(bb_re, bb_im)
    ccq = _blockdiag_in(jnp.swapaxes(p["c_re"], 1, 2), -jnp.swapaxes(p["c_im"], 1, 2))
    d_skip = p["d_s5"].reshape(1, D)
    bbq_b, ccq_b = bbq.astype(bf16), ccq.astype(bf16)
    (s_all, ylin, y5), (g_go,) = _s5_fwd(proj, bbq_b, ccq_b, ar, ai, d_skip, nch, comm=_gather_piece(slab, R_GLU, 1024))
    whole = lambda a: (a, pl.BlockSpec(a.shape, lambda i: (0, 0, 0), pipeline_mode=pl.Buffered(1)))
    w_up, w_down = whole(g_up), whole(g_down)
    w_glu_t = (g_go, pl.BlockSpec((4, 512, D), lambda i: (0, 0, 0), pipeline_mode=pl.Buffered(1)))
    w_out = (g_go, pl.BlockSpec((4, 512, D), lambda i: (0, 1, 0), pipeline_mode=pl.Buffered(1)))

    def glu_fn(i, r, s, q, w):
        v = jnp.concatenate([_dot(r[0], w[0][k], NT) for k in range(4)], axis=1) + q[0]
        return [v, _rms(v[:, :D] * _sigmoid(v[:, D:]), q[1])], []
    (v, y_s5), _ = _rowwise("glu", glu_fn, nbig, rows=[y5], pars=[p["b_glu"], p["g_s5"]], refs=[w_glu_t],
                            out_rows=[(2 * D, bf16), (D, bf16)], bm=bmb)

    def out_fn(i, r, s, q, w):
        acc = (_dot(r[0][:, :512], w[0][0]) + _dot(r[0][:, 512:], w[0][1]) + _dot(r[1][:, :512], w[0][2])
               + _dot(r[1][:, 512:], w[0][3]))
        return [h0_of(i, s, q) + acc], []
    (h1,), _ = _rowwise("out_proj", out_fn, nbig, rows=[y_ssd, y_s5], shifted=[x2], pars=[metablk], refs=[w_out],
                        out_rows=[(D, f32)], bm=bmb)

    def up_fn(i, r, s, q, w):
        nb = _rms(r[0], q[0]).astype(bf16)
        return [jnp.concatenate([jnp.maximum(_dot(nb, w[0][k]), 0.0).astype(bf16) for k in range(4)], axis=1), nb], []
    (relu_m, n1), _ = _rowwise("up_proj", up_fn, nbig, rows=[h1], pars=[p["g_mlp"]], refs=[w_up],
                               out_rows=[(4 * D, bf16), (D, bf16)], bm=bmb)

    def down_fn(i, r, s, q, w):
        acc = None
        for k in range(4):
            t = r[0][:, D * k:D * (k + 1)]
            part = _dot(t * t, w[0][k])
            acc = part if acc is None else acc + part
        return [r[1] + acc], []
    (h2,), _ = _rowwise("down_proj", down_fn, nbig, rows=[relu_m, h1], refs=[w_down], out_rows=[(D, f32)], bm=bmb)

    def final_fn(i, r, s, q, w):
        err = jnp.where(lead(i, r[0]), 0.0, _rms(r[0], q[0]) - s[0])
        dh, dg8 = _rms_bwd(err * (1.0 / D), r[0], q[0])
        return [dh, dh], [_rsum8(err * err), dg8]
    (dh2, dh2_b), (loss8, dgf8) = _rowwise("final", final_fn, nbig, rows=[h2], shifted=[tgt2], pars=[p["g_final"]],
                                           out_rows=[(D, f32), (D, bf16)], out_accs=[D, D], bm=bmb)
    loss = 0.5 / D * jnp.sum(loss8)

    def down_bwd_fn(i, r, s, q, w):
        dm_ = [_dot(r[0], w[0][k], NT) * (2.0 * r[1][:, D * k:D * (k + 1)].astype(f32)) for k in range(4)]
        return [jnp.concatenate(dm_, axis=1)], []
    (dm,), _ = _rowwise("down_bwd", down_bwd_fn, nbig, rows=[dh2_b, relu_m], refs=[w_down], out_rows=[(4 * D, bf16)], bm=bmb)
    g_a = _dw_into("dw_down", relu_m, dh2_b, None, 1024, True, 1, 4, 0, piece_rows=2048, a_square=True)

    def up_bwd_fn(i, r, s, q, w):
        acc = _dot(r[0][:, :D], w[0][0], NT)
        for k in range(1, 4):
            acc = acc + _dot(r[0][:, D * k:D * (k + 1)], w[0][k], NT)
        dh, dg8 = _rms_bwd(acc, r[1], q[0])
        dh1_ = r[2] + dh
        return [dh1_, dh1_], [dg8]
    (dh1, dh1_b), (dgmlp8,) = _rowwise("up_bwd", up_bwd_fn, nbig, rows=[dm, h1, dh2], pars=[p["g_mlp"]], refs=[w_up],
                                       out_rows=[(D, f32), (D, bf16)], out_accs=[D], bm=bmb)
    g_a = _dw_into("dw_up", n1, dm, g_a, 1024, False, 0, 4, 0, piece_rows=2048)

    def out_bwd_fn(i, r, s, q, w):
        dmix = [_dot(r[0], w[0][k], NT) for k in range(4)]
        v1, v2 = r[1][:, :D].astype(f32), r[1][:, D:].astype(f32)
        s2 = _sigmoid(v2)
        dglu, dg8 = _rms_bwd(jnp.concatenate(dmix[2:], axis=1), v1 * s2, q[0])
        dv = jnp.concatenate([dglu * s2, dglu * v1 * s2 * (1.0 - s2)], axis=1)
        return [jnp.concatenate(dmix[:2], axis=1), dv], [dg8, _rsum8(dv)]
    (dys, dv), (dgs58, dbglu8) = _rowwise("out_bwd", out_bwd_fn, nbig, rows=[dh1_b, v], pars=[p["g_s5"]], refs=[w_out],
                                          out_rows=[(D, f32), (2 * D, bf16)], out_accs=[D, 2 * D], bm=bmb)
    g_b = _dw_into("dw_out_a", y_ssd, dh1_b, None, 512, True, 1, 2, 0, piece_rows=1024)
    g_b = _dw_into("dw_out_b", y_s5, dh1_b, g_b, 512, True, 1, 2, 2, piece_rows=1024)

    def glu_bwd_fn(i, r, s, q, w):
        acc = _dot(r[0][:, :512], w[0][0])
        for k in range(1, 4):
            acc = acc + _dot(r[0][:, 512 * k:512 * (k + 1)], w[0][k])
        yl = r[1]
        cdf = 0.5 * (1.0 + lax.erf(yl * (1.0 / math.sqrt(2.0))))
        pdf = jnp.exp(-0.5 * yl * yl) * (1.0 / math.sqrt(2.0 * math.pi))
        return [acc * (cdf + yl * pdf)], []
    (dylin,), _ = _rowwise("glu_bwd", glu_bwd_fn, nbig, rows=[dv, ylin], refs=[w_glu_t], out_rows=[(D, f32)], bm=bmb)
    g_b = _dw_into("dw_glu", dv, y5, g_b, 512, True, 0, 4, 0, piece_rows=1024)

    (du, dcc, dbb, dab, dds5), (land_a,) = _s5_bwd(proj, dylin, s_all, bbq_b, ccq_b, ar, ai, d_skip, nch,
                                                   comm=_scatter_piece(g_a))

    s8 = lambda a: jnp.sum(a, axis=0, keepdims=True)
    dab_q = jnp.swapaxes(dab.reshape(2, 4, NQ, 128), 1, 2).reshape(2, NQ * 8, S5_P)
    dbb_re, dbb_im = _blockdiag_in_grad(dbb)
    dcr, dci = _blockdiag_in_grad(dcc)
    _, vjp = jax.vjp(_s5_tables, p["lam_re"], p["lam_im"], p["log_step"], p["b_re"], p["b_im"])
    dlam_re, dlam_im, dlog_step, db_re, db_im = vjp((dab_q[0], dab_q[1], dbb_re, dbb_im))
    early = dict(lam_re=dlam_re, lam_im=dlam_im, log_step=dlog_step, b_re=db_re, b_im=db_im, c_re=jnp.swapaxes(dcr, 1, 2),
                 c_im=-jnp.swapaxes(dci, 1, 2), d_s5=s8(dds5).reshape(NQ * 8, 16), b_glu=s8(dbglu8), g_s5=s8(dgs58),
                 g_mlp=s8(dgmlp8), g_final=s8(dgf8).reshape(D))
    early_pack = _pack_small([early[n] for n in EARLY], _rows_for(EARLY))

    (dproj, dcw8, dcb8, ddtb8, dal8, dd8, dgssd8), (land_b, all_early) = _ssd_bwd(
        proj, xc_all, y, dys, du, states, p["conv_w"], p["conv_b"], p["dt_bias"], p["a_log"], p["d_ssd"], p["g_ssd"], nch,
        comm=_both(_scatter_piece(g_b), _gather_blocks(early_pack)))

    gt = _dw_in_t(dproj, n0)
    gt = jnp.concatenate([gt[0:1024], gt[O_XBC:O_XBC + D_XBC], gt[O_DT:O_DT + HEADS], gt[O_U:O_U + D]], axis=0).reshape(4, 900, D)
    g_c = jnp.concatenate([gt, jnp.zeros((4, C_ROWS - 900, D), bf16)], axis=1)

    def in_bwd_fn(i, r, s, q, w):
        dh, dg8 = _rms_bwd(_dot(r[0], w[0][...]), h0_of(i, s, q), q[1])
        return [r[1] + dh], [dg8]
    (dh0,), (dgmix8,), (land_c,) = _rowwise(
        "in_bwd", in_bwd_fn, nbig, rows=[dproj, dh1], shifted=[x2], pars=[metablk, p["g_mix"]], refs=[w_full],
        out_rows=[(D, f32)], out_accs=[D], bm=bmb, comm=_scatter_piece(g_c))

    hsum = lambda a: jnp.sum(s8(a).reshape(HEADS, HEAD_DIM), axis=1).reshape(1, HEADS)
    late = dict(g_mix=s8(dgmix8), conv_b=s8(dcb8), dt_bias=s8(ddtb8), a_log=s8(dal8), d_ssd=hsum(dd8), g_ssd=s8(dgssd8),
                conv_w=dcw8[0:4], meta_tokens=dh0[CH - N_META:CH], loss=loss.reshape(1))
    return dh0, [(g_a, land_a), (g_b, land_b), (g_c, land_c)], all_early, late


def _perm_rows_w_in(wt):
    return jnp.concatenate([wt[0:1024], wt[2576:3600], wt[1024:2560], wt[2560:2576],
                            jnp.zeros((W_PROJ - 3600, wt.shape[1]), wt.dtype)], axis=0)


def _unperm_cols_w_in(g):
    return jnp.concatenate([g[:, 0:1024], g[:, O_XBC:O_XBC + D_XBC], g[:, O_DT:O_DT + HEADS], g[:, O_U:O_U + D]], axis=1)


def _pack_shard(w_in, w_glu, w_out, w_up, w_down, spare):
    dt = w_in.dtype
    parts = [w_up, w_down, w_glu.T, w_out, w_in.T, spare,
             jnp.zeros((PACK_ROWS - R_SPARE - spare.shape[0], D), dt)]
    return jnp.concatenate(parts, axis=0)


def _allgather8(x_shard, name):
    m_per, n = x_shard.shape

    def body(x_ref, out_ref, send_sems, recv_sems, stage, local_sems):
        x, y, c = _place()
        me, sibling = (x, y, c), (x, y, 1 - c)
        chips = [(1 - x, y), (x, 1 - y), (1 - x, 1 - y)]

        def rows(px, py, pc):
            return out_ref.at[pl.ds((4 * px + 2 * py + pc) * m_per, m_per), :]

        def copy(k, block, to, src=None):
            return pltpu.make_async_remote_copy(
                src_ref=rows(*block) if src is None else src, dst_ref=rows(*block),
                send_sem=send_sems.at[k], recv_sem=recv_sems.at[k], device_id=to, device_id_type=MESH)

        load = pltpu.make_async_copy(x_ref, stage, local_sems.at[0])
        load.start()
        first = [copy(0, me, sibling, src=x_ref)]
        first += [copy(1 + j, me, (*chip, c), src=x_ref) for j, chip in enumerate(chips)]
        for cp in first:
            cp.start()
        load.wait()
        store = pltpu.make_async_copy(stage, rows(*me), local_sems.at[1])
        store.start()
        passed = [copy(4 + j, (*chip, c), sibling) for j, chip in enumerate(chips)]
        for j, chip in enumerate(chips):
            copy(1 + j, (*chip, c), me).wait_recv()
            passed[j].start()
        copy(0, sibling, me).wait_recv()
        for j, chip in enumerate(chips):
            copy(4 + j, (*chip, 1 - c), me).wait_recv()
        for cp in first + passed:
            cp.wait_send()
        store.wait()

    return pl.pallas_call(
        body, name=name, out_shape=jax.ShapeDtypeStruct((8 * m_per, n), x_shard.dtype),
        in_specs=[_ANY], out_specs=_ANY,
        scratch_shapes=[pltpu.SemaphoreType.DMA((7,)), pltpu.SemaphoreType.DMA((7,)), pltpu.VMEM((m_per, n), x_shard.dtype),
                        pltpu.SemaphoreType.DMA((2,))])(x_shard)


def _swap_sibling(r, name):
    def body(r_ref, out_ref, send_sem, recv_sem):
        x, y, c = _place()
        cp = pltpu.make_async_remote_copy(src_ref=r_ref, dst_ref=out_ref, send_sem=send_sem, recv_sem=recv_sem,
                                          device_id=(x, y, 1 - c), device_id_type=MESH)
        cp.start()
        cp.wait()

    return pl.pallas_call(
        body, name=name, out_shape=jax.ShapeDtypeStruct(r.shape, r.dtype), in_specs=[_ANY], out_specs=_ANY,
        scratch_shapes=[pltpu.SemaphoreType.DMA, pltpu.SemaphoreType.DMA])(r)


SH_CONVW, SH_META = 4 * 384, 16 * 256
SPARE_ROWS = 17

SMALL_SHAPES = dict(
    g_mix=(1, 1024), conv_b=(1, 1536), dt_bias=(1, 16), a_log=(1, 16), d_ssd=(1, 16), g_ssd=(1, 1024), lam_re=(1, 64, 64),
    lam_im=(1, 64, 64), log_step=(1, 64), b_re=(1, 64, 64, 16), b_im=(1, 64, 64, 16), c_re=(1, 64, 16, 64), c_im=(1, 64, 16, 64),
    d_s5=(1, 64, 16), b_glu=(1, 2048), g_s5=(1, 1024), g_mlp=(1, 1024), g_final=(1024,),
    conv_w=(4, D_XBC), meta_tokens=(N_META, D), loss=(1,))
EARLY = ["lam_re", "lam_im", "log_step", "b_re", "b_im", "c_re", "c_im", "d_s5", "b_glu", "g_s5", "g_mlp", "g_final"]
LATE = ["g_mix", "conv_b", "dt_bias", "a_log", "d_ssd", "g_ssd", "conv_w", "meta_tokens", "loss"]


def _rows_for(names):
    return -(-sum(math.prod(SMALL_SHAPES[n]) for n in names) // (8 * D)) * 8


def _pack_small(arrs, rows):
    flat = jnp.concatenate([a.reshape(-1).astype(f32) for a in arrs])
    return jnp.concatenate([flat, jnp.zeros((rows * D - flat.shape[0],), f32)]).reshape(rows, D)


def _unpack_small(slab, shapes):
    flat = slab.reshape(-1)
    out, o = [], 0
    for shp in shapes:
        n = math.prod(shp)
        out.append(flat[o:o + n].reshape(shp))
        o += n
    return out


def _sum8(g, rows, name):
    def body(g_ref, o_ref):
        acc = g_ref[0]
        for k in range(1, 8):
            acc = acc + g_ref[k]
        o_ref[...] = acc
    return pl.pallas_call(body, name=name, out_shape=jax.ShapeDtypeStruct((rows, D), f32),
                          compiler_params=_cp())(g.reshape(8, rows, D))


def _adam_math(w_, g_, m_, v_):
    m2 = ADAM_B1 * m_ + (1.0 - ADAM_B1) * g_
    v2 = ADAM_B2 * v_ + (1.0 - ADAM_B2) * jnp.square(g_)
    m_hat = m2 / (1.0 - ADAM_B1 ** ADAM_STEP)
    v_hat = v2 / (1.0 - ADAM_B2 ** ADAM_STEP)
    delta = -ADAM_LR * (m_hat / (jnp.sqrt(v_hat) + ADAM_EPS) + ADAM_WD * w_)
    return delta, m2, v2


def _adamw(name, w, g, m, v, bm):
    def fn(i, r, s, q, refs):
        return list(_adam_math(*r)), []
    c = w.shape[1]
    (d, m2, v2), _ = _rowwise(name, fn, w.shape[0] // bm, rows=[w, g, m, v], out_rows=[(c, f32)] * 3, bm=bm)
    return d, m2, v2


def _adamw_whole(name, w, g, m, v):
    def body(w_ref, g_ref, m_ref, v_ref, d_ref, m2_ref, v2_ref):
        d_ref[...], m2_ref[...], v2_ref[...] = _adam_math(w_ref[...], g_ref[...], m_ref[...], v_ref[...])
    return pl.pallas_call(body, name=name, out_shape=[jax.ShapeDtypeStruct(w.shape, f32)] * 3, compiler_params=_cp())(w, g, m, v)


def _sum_parts(name, own, land):
    def fn(i, r, s, q, refs):
        acc = r[0].astype(f32)
        for k in range(7):
            acc = acc + refs[0][k].astype(f32)
        return [acc], []
    rows = own.shape[0]
    bm = CH if rows % CH == 0 else rows
    (o,), _ = _rowwise(name, fn, rows // bm, rows=[own], refs=[(land, pl.BlockSpec((7, bm, D), lambda i: (0, i, 0)))],
                       out_rows=[(D, f32)], bm=bm)
    return o


def kernel(x, meta_tokens, g_mix, w_in, conv_w, conv_b, dt_bias, a_log, d_ssd, g_ssd, lam_re, lam_im, log_step, b_re, b_im, c_re, c_im, d_s5, w_glu, b_glu, g_s5, w_out, g_mlp, w_up, w_down, g_final, loss_target, m_meta_tokens, m_g_mix, m_w_in, m_conv_w, m_conv_b, m_dt_bias, m_a_log, m_d_ssd, m_g_ssd, m_lam_re, m_lam_im, m_log_step, m_b_re, m_b_im, m_c_re, m_c_im, m_d_s5, m_w_glu, m_b_glu, m_g_s5, m_w_out, m_g_mlp, m_w_up, m_w_down, m_g_final, v_meta_tokens, v_g_mix, v_w_in, v_conv_w, v_conv_b, v_dt_bias, v_a_log, v_d_ssd, v_g_ssd, v_lam_re, v_lam_im, v_log_step, v_b_re, v_b_im, v_c_re, v_c_im, v_d_s5, v_w_glu, v_b_glu, v_g_s5, v_w_out, v_g_mlp, v_w_up, v_w_down, v_g_final):
    given = dict(locals())
    cx, cy, cc = _place()
    chip = 2 * cx + cy

    small_f = jnp.concatenate([conv_w.reshape(-1), meta_tokens.reshape(-1)])
    t_hi = small_f.astype(bf16)
    r_1 = small_f - t_hi.astype(f32)
    t_mid = r_1.astype(bf16)
    t_lo = (r_1 - t_mid.astype(f32)).astype(bf16)
    terms = jnp.concatenate([t_hi, t_mid, t_lo])
    spare = jnp.concatenate([terms, jnp.zeros((SPARE_ROWS * D - terms.shape[0],), bf16)]).reshape(SPARE_ROWS, D)
    slab = _pack_shard(w_in[0].astype(bf16), w_glu[0].astype(bf16), w_out[0].astype(bf16), w_up[0].astype(bf16),
                       w_down[0].astype(bf16), spare)
    my_half = lax.dynamic_slice_in_dim(slab, R_IN + cc * 512, 512, axis=0)
    gathered = _allgather8(my_half, "gather_w_in").reshape(4, 1024, D)
    w_in_t = _perm_rows_w_in(jnp.concatenate([gathered[s, 0:900] for s in range(4)], axis=0))
    n_sf = SH_CONVW + SH_META
    tr = gathered[:, 900:900 + SPARE_ROWS].reshape(4, SPARE_ROWS * D)[:, :3 * n_sf].astype(f32).reshape(4, 3, n_sf)
    sp = tr[:, 0] + tr[:, 1] + tr[:, 2]
    conv_w_full = jnp.concatenate([sp[s, :SH_CONVW].reshape(4, 384) for s in range(4)], axis=1)
    meta_full = jnp.concatenate([sp[s, SH_CONVW:].reshape(16, 256) for s in range(4)], axis=1)

    p = dict(g_mix=g_mix, conv_w=conv_w_full, conv_b=conv_b, dt_bias=dt_bias, a_log=a_log, d_ssd=d_ssd, g_ssd=g_ssd,
             lam_re=lam_re[0], lam_im=lam_im[0], log_step=log_step[0], b_re=b_re[0], b_im=b_im[0], c_re=c_re[0], c_im=c_im[0],
             d_s5=d_s5[0], b_glu=b_glu, g_s5=g_s5, g_mlp=g_mlp, g_final=g_final.reshape(1, D))
    dh0, pieces, all_early, late = _local_step(x[0], loss_target[0], meta_full, p, w_in_t, slab)
    grad_x = dh0[CH:].reshape(x.shape)

    reds = []
    for k, (gp, land) in enumerate(pieces):
        half = gp.shape[1] // 2
        own = lax.dynamic_slice(gp, (chip, cc * half, 0), (1, half, D)).reshape(half, D)
        reds.append(_sum_parts("rs_sum_%d" % k, own, land))
    red = jnp.concatenate(reds, axis=0)
    other = _swap_sibling(red, "rs_share")
    first = jnp.where(cc == 0, red, other)
    second = jnp.where(cc == 0, other, red)
    g_up, g_down = first[0:1024], second[0:1024]
    g_glu, g_out = first[1024:1536].T, second[1024:1536]
    hc = C_ROWS // 2
    g_in_t = jnp.concatenate([first[1536:1536 + hc], second[1536:1536 + 900 - hc]], axis=0)

    gs = dict(zip(EARLY, _unpack_small(_sum8(all_early, _rows_for(EARLY), "sum8_early"), [SMALL_SHAPES[n] for n in EARLY])))
    all_late = _allgather8(_pack_small([late[n] for n in LATE], _rows_for(LATE)), "gather_small")
    gs.update(zip(LATE, _unpack_small(_sum8(all_late, _rows_for(LATE), "sum8_late"), [SMALL_SHAPES[n] for n in LATE])))
    g_conv_w = lax.dynamic_slice_in_dim(gs.pop("conv_w"), chip * 384, 384, axis=1).reshape(conv_w.shape)
    g_meta = lax.dynamic_slice_in_dim(gs.pop("meta_tokens"), chip * 256, 256, axis=1)
    loss = gs.pop("loss").reshape(())

    grads = dict(gs, meta_tokens=g_meta, conv_w=g_conv_w, w_in=g_in_t.T.reshape(w_in.shape), w_glu=g_glu.reshape(w_glu.shape),
                 w_out=g_out.reshape(w_out.shape), w_up=g_up.reshape(w_up.shape), w_down=g_down.reshape(w_down.shape))
    delta, new_m, new_v = {}, {}, {}
    d_, m_, v_ = _adamw_whole("adamw_w_in", w_in[0].T, g_in_t, m_w_in[0].T, v_w_in[0].T)
    delta["w_in"], new_m["w_in"], new_v["w_in"] = (a.T.reshape(w_in.shape) for a in (d_, m_, v_))
    for n in ("w_glu", "w_out", "w_up", "w_down"):
        shp = given[n].shape
        two = lambda a: a.reshape(shp[1], shp[2])
        d_, m_, v_ = _adamw("adamw_" + n, two(given[n]), two(grads[n]), two(given["m_" + n]), two(given["v_" + n]), 256)
        delta[n], new_m[n], new_v[n] = d_.reshape(shp), m_.reshape(shp), v_.reshape(shp)
    for n in EARLY + LATE[:-1]:
        shp = given[n].shape
        if len(shp) == 4 and shp[-1] == 16:
            two = back = lambda a: jnp.swapaxes(a, -1, -2)
        else:
            two = (lambda a: a.reshape(1, -1)) if len(shp) == 1 else (lambda a: a)
            back = lambda a: a.reshape(shp)
        d_, m_, v_ = _adamw_whole("adamw_" + n, two(given[n]), two(grads[n].reshape(shp)), two(given["m_" + n]), two(given["v_" + n]))
        delta[n], new_m[n], new_v[n] = back(d_), back(m_), back(v_)

    order = ["meta_tokens", "g_mix", "w_in", "conv_w", "conv_b", "dt_bias", "a_log", "d_ssd", "g_ssd", "lam_re", "lam_im", "log_step",
             "b_re", "b_im", "c_re", "c_im", "d_s5", "w_glu", "b_glu", "g_s5", "w_out", "g_mlp", "w_up", "w_down", "g_final"]
    grads_out = [grads[n].reshape(given[n].shape) for n in order]
    return (loss, grad_x, *grads_out, *[delta[n] for n in order], *[new_m[n] for n in order], *[new_v[n] for n in order])
```

```python
import math

import jax
import jax.numpy as jnp
from jax import lax
from jax.experimental import pallas as pl
from jax.experimental.pallas import tpu as pltpu

f32 = jnp.float32
bf16 = jnp.bfloat16

D = 1024
N_META = 16
CH = 256
HEADS = 16
HEAD_DIM = 64
NSTATE = 128
D_XBC = 1536
S5_P = 64
NQ = 8
PITCH = CH + 4
EPS = 1e-5
O_Z, O_U, O_XBC, O_DT, W_PROJ = 0, 1024, 2048, 3584, 3712
VMEM_LIMIT = 60 * 1024 * 1024

ADAM_LR, ADAM_B1, ADAM_B2, ADAM_EPS, ADAM_WD, ADAM_STEP = 0.001, 0.9, 0.999, 1e-08, 0.01, 10

NT = (((1,), (1,)), ((), ()))
TN = (((0,), (0,)), ((), ()))
_ANY = pl.BlockSpec(memory_space=pl.ANY)


def _cp(sem=None):
    return pltpu.CompilerParams(dimension_semantics=sem, vmem_limit_bytes=VMEM_LIMIT)


def _sigmoid(v):
    return 1.0 / (1.0 + jnp.exp(-v))


def _rsum8(v):
    r, c = v.shape
    return jnp.sum(v.reshape(r // 8, 8, c), axis=0)


def _rms(h, g):
    r = lax.rsqrt(jnp.mean(h * h, axis=-1, keepdims=True) + EPS)
    return h * r * g


def _rms_bwd(dy, h, g):
    r = lax.rsqrt(jnp.mean(h * h, axis=-1, keepdims=True) + EPS)
    n = h * r
    dn = dy * g
    dh = r * (dn - n * jnp.mean(dn * n, axis=-1, keepdims=True))
    return dh, _rsum8(dy * n)


def _dot(a, b, dims=None):
    if dims is None:
        return jnp.dot(a, b, preferred_element_type=f32)
    return lax.dot_general(a, b, dims, preferred_element_type=f32)


def _split_dot(v, m01, dims, terms, v_is_lhs=True):
    out, r = None, v
    for _ in range(terms):
        piece = r.astype(bf16)
        o = _dot(piece, m01, dims) if v_is_lhs else _dot(m01, piece, dims)
        out = o if out is None else out + o
        r = r - piece.astype(f32)
    return out


MESH = pl.DeviceIdType.MESH


def _place():
    return lax.axis_index("x"), lax.axis_index("y"), lax.axis_index("c")


def _flip(v, f):
    return 1 - v if f else v


def _call(body, comm, *, name, nsteps, in_specs, out_specs, out_shape, scratch_shapes, args):
    n_in, n_out, n_scr = len(in_specs), len(out_specs), len(scratch_shapes)
    if comm is None:
        res = pl.pallas_call(body, name=name, grid=(nsteps,), in_specs=in_specs, out_specs=out_specs, out_shape=out_shape,
                             scratch_shapes=scratch_shapes, compiler_params=_cp(("arbitrary",)))(*args)
        return list(res), []
    c_in, c_out = len(comm["ins"]), len(comm["outs"])

    def wrapped(*refs):
        o0 = n_in + c_in
        s0 = o0 + n_out + c_out
        cparts = (refs[n_in:o0], refs[o0 + n_out:s0], refs[s0 + n_scr:])

        @pl.when(pl.program_id(0) == 0)
        def _():
            comm["start"](*cparts)
        if "middle" in comm:
            @pl.when(pl.program_id(0) == (3 * nsteps) // 4)
            def _():
                comm["middle"](*cparts)
        body(*refs[:n_in], *refs[o0:o0 + n_out], *refs[s0:s0 + n_scr])

        @pl.when(pl.program_id(0) == nsteps - 1)
        def _():
            comm["finish"](*cparts)

    any_spec = pl.BlockSpec(memory_space=pl.ANY)
    res = pl.pallas_call(
        wrapped, name=name, grid=(nsteps,), in_specs=list(in_specs) + [any_spec] * c_in,
        out_specs=list(out_specs) + [any_spec] * c_out, out_shape=list(out_shape) + list(comm["outs"]),
        scratch_shapes=list(scratch_shapes) + list(comm["scratch"]),
        compiler_params=_cp(("arbitrary",)))(*args, *comm["ins"])
    return list(res[:n_out]), list(res[n_out:])


def _gather_piece(slab):
    r0, rows = 0, slab.shape[0]
    half = rows // 2
    flips = ((1, 0), (0, 1), (1, 1))

    def first(j, slab_ref, out_ref, send_sems, recv_sems):
        x, y, c = _place()
        return pltpu.make_async_remote_copy(
            src_ref=slab_ref.at[pl.ds(r0 + c * half, half), :], dst_ref=out_ref.at[2 * x + y, pl.ds(c * half, half), :],
            send_sem=send_sems.at[j], recv_sem=recv_sems.at[j],
            device_id=(_flip(x, flips[j][0]), _flip(y, flips[j][1]), c), device_id_type=MESH)

    def passed(j, out_ref, send_sems, recv_sems):
        x, y, c = _place()
        rows_j = out_ref.at[2 * _flip(x, flips[j][0]) + _flip(y, flips[j][1]), pl.ds(c * half, half), :]
        return pltpu.make_async_remote_copy(src_ref=rows_j, dst_ref=rows_j, send_sem=send_sems.at[3 + j],
                                            recv_sem=recv_sems.at[3 + j], device_id=(x, y, 1 - c), device_id_type=MESH)

    def start(ins, outs, scr):
        send_sems, recv_sems, stage, local_sems = scr
        x, y, _ = _place()
        load = pltpu.make_async_copy(ins[0].at[pl.ds(r0, rows), :], stage, local_sems.at[0])
        load.start()
        for j in range(3):
            first(j, ins[0], outs[0], send_sems, recv_sems).start()
        load.wait()
        pltpu.make_async_copy(stage, outs[0].at[2 * x + y], local_sems.at[1]).start()

    def middle(ins, outs, scr):
        send_sems, recv_sems, _, _ = scr
        for j in range(3):
            first(j, ins[0], outs[0], send_sems, recv_sems).wait_recv()
            passed(j, outs[0], send_sems, recv_sems).start()

    def finish(ins, outs, scr):
        send_sems, recv_sems, stage, local_sems = scr
        x, y, c = _place()
        for j in range(3):
            sib = outs[0].at[2 * _flip(x, flips[j][0]) + _flip(y, flips[j][1]), pl.ds((1 - c) * half, half), :]
            pltpu.make_async_remote_copy(src_ref=sib, dst_ref=sib, send_sem=send_sems.at[3 + j], recv_sem=recv_sems.at[3 + j],
                                         device_id=(x, y, 1 - c), device_id_type=MESH).wait_recv()
        for j in range(3):
            first(j, ins[0], outs[0], send_sems, recv_sems).wait_send()
            passed(j, outs[0], send_sems, recv_sems).wait_send()
        pltpu.make_async_copy(stage, outs[0].at[2 * x + y], local_sems.at[1]).wait()

    return dict(ins=[slab], outs=[jax.ShapeDtypeStruct((4, rows, D), bf16)],
                scratch=[pltpu.SemaphoreType.DMA((6,)), pltpu.SemaphoreType.DMA((6,)), pltpu.VMEM((rows, D), bf16),
                         pltpu.SemaphoreType.DMA((2,))], start=start, middle=middle, finish=finish)


def _scatter_piece(gpiece):
    half = gpiece.shape[1] // 2

    def copies(g_ref, land_ref, send_sems, recv_sems):
        x, y, c = _place()
        cps = []
        for fx in (0, 1):
            for fy in (0, 1):
                for fc in (0, 1):
                    k = 4 * fx + 2 * fy + fc - 1
                    if k < 0:
                        continue
                    px, py, pc = _flip(x, fx), _flip(y, fy), _flip(c, fc)
                    cps.append(pltpu.make_async_remote_copy(
                        src_ref=g_ref.at[2 * px + py, pl.ds(pc * half, half), :], dst_ref=land_ref.at[k],
                        send_sem=send_sems.at[k], recv_sem=recv_sems.at[k], device_id=(px, py, pc), device_id_type=MESH))
        return cps

    def start(ins, outs, scr):
        for cp in copies(ins[0], outs[0], *scr):
            cp.start()

    def finish(ins, outs, scr):
        for cp in copies(ins[0], outs[0], *scr):
            cp.wait()

    return dict(ins=[gpiece], outs=[jax.ShapeDtypeStruct((7, half, D), gpiece.dtype)],
                scratch=[pltpu.SemaphoreType.DMA((7,)), pltpu.SemaphoreType.DMA((7,))], start=start, finish=finish)


def _gather_blocks(block):
    rows = block.shape[0]

    def mine(out_ref):
        x, y, c = _place()
        return out_ref.at[pl.ds((4 * x + 2 * y + c) * rows, rows), :]

    def copies(b_ref, out_ref, send_sems, recv_sems):
        x, y, c = _place()
        cps = []
        for fx in (0, 1):
            for fy in (0, 1):
                for fc in (0, 1):
                    k = 4 * fx + 2 * fy + fc - 1
                    if k < 0:
                        continue
                    cps.append(pltpu.make_async_remote_copy(
                        src_ref=b_ref, dst_ref=mine(out_ref), send_sem=send_sems.at[k], recv_sem=recv_sems.at[k],
                        device_id=(_flip(x, fx), _flip(y, fy), _flip(c, fc)), device_id_type=MESH))
        return cps

    def start(ins, outs, scr):
        send_sems, recv_sems, stage, local_sems = scr
        load = pltpu.make_async_copy(ins[0], stage, local_sems.at[0])
        load.start()
        for cp in copies(ins[0], outs[0], send_sems, recv_sems):
            cp.start()
        load.wait()
        pltpu.make_async_copy(stage, mine(outs[0]), local_sems.at[1]).start()

    def finish(ins, outs, scr):
        send_sems, recv_sems, stage, local_sems = scr
        for cp in copies(ins[0], outs[0], send_sems, recv_sems):
            cp.wait()
        pltpu.make_async_copy(stage, mine(outs[0]), local_sems.at[1]).wait()

    return dict(ins=[block], outs=[jax.ShapeDtypeStruct((8 * rows, D), block.dtype)],
                scratch=[pltpu.SemaphoreType.DMA((7,)), pltpu.SemaphoreType.DMA((7,)), pltpu.VMEM((rows, D), block.dtype),
                         pltpu.SemaphoreType.DMA((2,))], start=start, finish=finish)


def _both(c1, c2):
    n = (len(c1["ins"]), len(c1["outs"]), len(c1["scratch"]))

    def split(parts):
        return [p[:k] for p, k in zip(parts, n)], [p[k:] for p, k in zip(parts, n)]

    def start(*parts):
        a, b = split(parts)
        c1["start"](*a)
        c2["start"](*b)

    def finish(*parts):
        a, b = split(parts)
        c1["finish"](*a)
        c2["finish"](*b)

    both = dict(ins=c1["ins"] + c2["ins"], outs=c1["outs"] + c2["outs"], scratch=c1["scratch"] + c2["scratch"],
                start=start, finish=finish)
    if "middle" in c1 or "middle" in c2:
        def middle(*parts):
            for cm, part in zip((c1, c2), split(parts)):
                if "middle" in cm:
                    cm["middle"](*part)
        both["middle"] = middle
    return both


def _rowwise(name, fn, nblk, rows=(), shifted=(), pars=(), refs=(), out_rows=(), out_accs=(), bm=CH, comm=None):
    n_sub = bm // CH
    n_r, n_s, n_p, n_w = len(rows), len(shifted) * n_sub, len(pars), len(refs)
    n_in = n_r + n_s + n_p + n_w
    n_o, n_a = len(out_rows), len(out_accs)

    def body(*all_refs):
        i = pl.program_id(0)
        ins = all_refs[:n_in]
        outs = all_refs[n_in:]
        rv = [r[...] for r in ins[:n_r]]
        sub = ins[n_r:n_r + n_s]
        sv = [jnp.concatenate([r[...] for r in sub[k * n_sub:(k + 1) * n_sub]], axis=0) if n_sub > 1 else sub[k][...]
              for k in range(len(shifted))]
        pv = [r[...] for r in ins[n_r + n_s:n_r + n_s + n_p]]
        ro, ao = fn(i, rv, sv, pv, list(ins[n_r + n_s + n_p:]))
        for r, v in zip(outs[:n_o], ro):
            r[...] = v.astype(r.dtype)
        accs = outs[n_o:]

        @pl.when(i == 0)
        def _():
            for r in accs:
                r[...] = jnp.zeros_like(r)
        for r, v in zip(accs, ao):
            r[...] += v

    in_specs = [pl.BlockSpec((bm, a.shape[1]), lambda i: (i, 0)) for a in rows]
    in_specs += [pl.BlockSpec((CH, a.shape[1]), lambda i, j=j: (jnp.maximum(n_sub * i - 1 + j, 0), 0))
                 for a in shifted for j in range(n_sub)]
    in_specs += [pl.BlockSpec(a.shape, lambda i, nd=a.ndim: (0,) * nd) for a in pars]
    in_specs += [spec for _, spec in refs]
    out_specs = [pl.BlockSpec((bm, c), lambda i: (i, 0)) for c, _ in out_rows]
    out_specs += [pl.BlockSpec((8, c), lambda i: (0, 0)) for c in out_accs]
    out_shape = [jax.ShapeDtypeStruct((nblk * bm, c), dt) for c, dt in out_rows]
    out_shape += [jax.ShapeDtypeStruct((8, c), f32) for c in out_accs]
    res, cres = _call(body, comm, name=name, nsteps=nblk, in_specs=in_specs, out_specs=out_specs, out_shape=out_shape,
                      scratch_shapes=[], args=[*rows, *[a for a in shifted for _ in range(n_sub)], *pars, *[a for a, _ in refs]])
    parts = (res[:n_o], res[n_o:])
    return parts if comm is None else parts + (cres,)


C_ROWS = 928
BIG_ROWS = 768
DW_ROWS = 2816


def _contract_rows(lp, big=DW_ROWS):
    for rows in (big, BIG_ROWS):
        if lp % rows == 0:
            return rows
    return CH


def _dw_into(name, a, b, slab, ka, a_sharded, row_blk, n_s, s0, piece_rows=2048, a_square=False):
    lp = a.shape[0]
    bm = _contract_rows(lp)
    steps = lp // bm

    def body(a_ref, b_ref, *rest):
        o_ref, acc = rest[-2], rest[-1]
        k = pl.program_id(1)

        @pl.when(k == 0)
        def _():
            acc[...] = jnp.zeros_like(acc)
        a_v = a_ref[...]
        acc[...] += _dot(a_v * a_v if a_square else a_v, b_ref[...], TN)

        @pl.when(k == steps - 1)
        def _():
            o_ref[0] = acc[...].astype(bf16)

    in_specs = [pl.BlockSpec((bm, ka), (lambda s, k: (k, s)) if a_sharded else (lambda s, k: (k, 0))),
                pl.BlockSpec((bm, D), (lambda s, k: (k, 0)) if a_sharded else (lambda s, k: (k, s)))]
    args = [a, b]
    aliases = {}
    if slab is not None:
        in_specs.append(_ANY)
        args.append(slab)
        aliases = {2: 0}
    return pl.pallas_call(
        body, name=name, grid=(n_s, steps), in_specs=in_specs,
        out_specs=pl.BlockSpec((1, ka, D), lambda s, k: (s0 + s, row_blk, 0)),
        out_shape=jax.ShapeDtypeStruct((4, piece_rows, D), bf16),
        scratch_shapes=[pltpu.VMEM((ka, D), f32)], input_output_aliases=aliases,
        compiler_params=_cp(("arbitrary", "arbitrary")))(*args)


def _dw_in_t(dproj, n0):
    lp = n0.shape[0]
    bm = _contract_rows(lp, BIG_ROWS)
    steps = lp // bm
    bn = 512

    def body(a_ref, b_ref, o_ref, acc):
        k = pl.program_id(1)

        @pl.when(k == 0)
        def _():
            acc[...] = jnp.zeros_like(acc)
        acc[...] += _dot(a_ref[...], b_ref[...], TN)

        @pl.when(k == steps - 1)
        def _():
            o_ref[...] = acc[...].astype(bf16)

    return pl.pallas_call(
        body, name="dw_in", grid=(D // bn, steps),
        in_specs=[pl.BlockSpec((bm, W_PROJ), lambda j, k: (k, 0)), pl.BlockSpec((bm, bn), lambda j, k: (k, j))],
        out_specs=pl.BlockSpec((W_PROJ, bn), lambda j, k: (0, j)),
        out_shape=jax.ShapeDtypeStruct((W_PROJ, D), bf16),
        scratch_shapes=[pltpu.VMEM((W_PROJ, bn), f32)],
        compiler_params=_cp(("arbitrary", "arbitrary")))(dproj, n0)


def _head_expand():
    h = lax.broadcasted_iota(jnp.int32, (HEADS, D), 0)
    c = lax.broadcasted_iota(jnp.int32, (HEADS, D), 1)
    return jnp.where((c >> 6) == h, 1.0, 0.0).astype(bf16)


def _ssd_common(i, P, prev8, cw, cb, dtb, alog, xc=None):
    z = P[:, O_Z:O_Z + D]
    xp = P[:, O_XBC:O_XBC + D_XBC]
    dt_raw = P[:, O_DT:O_DT + HEADS]
    row = lax.broadcasted_iota(jnp.int32, (CH, 1), 0)
    if xc is None:
        row8 = lax.broadcasted_iota(jnp.int32, (8, 1), 0)
        xc = cb + cw[3:4] * xp
        for k in (1, 2, 3):
            rolled = pltpu.roll(xp, k, 0)
            fix = pltpu.roll(prev8, k, 0)
            top = jnp.where(row8 < k, fix, rolled[0:8])
            xc = xc + cw[3 - k:4 - k] * jnp.concatenate([top, rolled[8:]], axis=0)
    sg = _sigmoid(xc)
    xbc = xc * sg
    live = jnp.where(jnp.logical_or(i > 0, row >= CH - N_META), 1.0, 0.0)
    pre = dt_raw + dtb
    dt = jnp.where(pre > 20.0, pre, jnp.log(1.0 + jnp.exp(jnp.minimum(pre, 20.0)))) * live
    a = -jnp.exp(alog)
    dta = dt * a
    r_i = lax.broadcasted_iota(jnp.int32, (CH, CH), 0)
    c_i = lax.broadcasted_iota(jnp.int32, (CH, CH), 1)
    tril = r_i >= c_i
    acs = _split_dot(dta, jnp.where(tril, 1.0, 0.0).astype(bf16), None, 3, v_is_lhs=False)
    acs_t = _split_dot(dta, jnp.where(r_i <= c_i, 1.0, 0.0).astype(bf16), TN, 3)
    e = _head_expand()
    acs_e = _split_dot(acs, e, None, 3)
    dt_e = _split_dot(dt, e, None, 3)
    return dict(z=z, xp=xp, xc=xc, sg=sg, xbc=xbc, live=live, pre=pre, dt=dt, a=a, tril=tril,
                acs=acs, acs_t=acs_t, e=e, acs_e=acs_e, dt_e=dt_e)


def _lmat(c, h):
    seg = c["acs"][:, h:h + 1] - c["acs_t"][h:h + 1, :]
    return jnp.where(c["tril"], jnp.exp(jnp.minimum(seg, 0.0)), 0.0)


def _pair_masks():
    lane = lax.broadcasted_iota(jnp.int32, (1, 128), 1)
    return jnp.where(lane < HEAD_DIM, 1.0, 0.0), jnp.where(lane >= HEAD_DIM, 1.0, 0.0)


def _ssd_fwd(proj, conv_w, conv_b, dt_bias, a_log, d_ssd, g_ssd, nch, comm=None):
    def body(p_ref, cw_ref, cb_ref, dtb_ref, al_ref, d_ref, g_ref, y_ref, ys_ref, st_ref, xc_ref, prev8_ref, state_ref):
        i = pl.program_id(0)

        @pl.when(i == 0)
        def _():
            prev8_ref[...] = jnp.zeros_like(prev8_ref)
            state_ref[...] = jnp.zeros_like(state_ref)

        P = p_ref[...]
        c = _ssd_common(i, P, prev8_ref[...], cw_ref[...], cb_ref[...], dtb_ref[...], al_ref[...])
        prev8_ref[...] = c["xp"][CH - 8:CH]
        xc_ref[...] = c["xc"]
        xbc = c["xbc"]
        x = xbc[:, 0:D]
        xdt = x * c["dt_e"]
        a_last_e = c["acs_e"][CH - 1:CH, :]
        w_end = (xdt * jnp.exp(a_last_e - c["acs_e"])).astype(bf16)
        m0, m1 = _pair_masks()
        ys = []
        for g in range(2):
            bg = xbc[:, D + NSTATE * g:D + NSTATE * (g + 1)].astype(bf16)
            cg = xbc[:, D + 2 * NSTATE + NSTATE * g:D + 2 * NSTATE + NSTATE * (g + 1)].astype(bf16)
            gmat = _dot(cg, bg, NT)
            st = state_ref[g]
            st_ref[0, g] = st
            sl = slice(512 * g, 512 * (g + 1))
            y_off = _dot(cg, st.astype(bf16)) * jnp.exp(c["acs_e"][:, sl])
            contrib = _dot(bg, w_end[:, sl], TN)
            state_ref[g] = st * jnp.exp(a_last_e[:, sl]) + contrib
            yd = []
            for pr in range(4):
                h0 = 8 * g + 2 * pr
                xp2 = xdt[:, 128 * (4 * g + pr):128 * (4 * g + pr + 1)]
                ma = (gmat * _lmat(c, h0)).astype(bf16)
                mb = (gmat * _lmat(c, h0 + 1)).astype(bf16)
                yd.append(_dot(ma, (xp2 * m0).astype(bf16)) + _dot(mb, (xp2 * m1).astype(bf16)))
            ys.append(jnp.concatenate(yd, axis=1) + y_off)
        d_e = _split_dot(d_ref[...], c["e"], None, 3)
        y = jnp.concatenate(ys, axis=1) + x * d_e
        y_ref[...] = y
        yg = y * (c["z"] * _sigmoid(c["z"]))
        ys_ref[...] = _rms(yg, g_ref[...]).astype(bf16)

    full = lambda a: pl.BlockSpec(a.shape, lambda i, nd=a.ndim: (0,) * nd)
    return _call(
        body, comm, name="ssd_fwd", nsteps=nch,
        in_specs=[pl.BlockSpec((CH, W_PROJ), lambda i: (i, 0))] + [full(a) for a in (conv_w, conv_b, dt_bias, a_log, d_ssd, g_ssd)],
        out_specs=[pl.BlockSpec((CH, D), lambda i: (i, 0)), pl.BlockSpec((CH, D), lambda i: (i, 0)),
                   pl.BlockSpec((1, 2, NSTATE, 512), lambda i: (i, 0, 0, 0)), pl.BlockSpec((CH, D_XBC), lambda i: (i, 0))],
        out_shape=[jax.ShapeDtypeStruct((nch * CH, D), f32), jax.ShapeDtypeStruct((nch * CH, D), bf16),
                   jax.ShapeDtypeStruct((nch, 2, NSTATE, 512), f32), jax.ShapeDtypeStruct((nch * CH, D_XBC), f32)],
        scratch_shapes=[pltpu.VMEM((8, D_XBC), f32), pltpu.VMEM((2, NSTATE, 512), f32)],
        args=[proj, conv_w, conv_b, dt_bias, a_log, d_ssd, g_ssd])


def _ssd_bwd(proj, xc_all, y, dys, du, states, conv_w, conv_b, dt_bias, a_log, d_ssd, g_ssd, nch, comm=None):
    def body(p_ref, xc_ref, y_ref, dys_ref, du_ref, st_ref, cw_ref, cb_ref, dtb_ref, al_ref, d_ref, g_ref,
             dp_ref, dcw_ref, dcb_ref, ddtb_ref, dal_ref, dd_ref, dg_ref, nxt8_ref, dst_ref):
        step = pl.program_id(0)
        i = nch - 1 - step

        @pl.when(step == 0)
        def _():
            nxt8_ref[...] = jnp.zeros_like(nxt8_ref)
            dst_ref[...] = jnp.zeros_like(dst_ref)
            for r in (dcw_ref, dcb_ref, ddtb_ref, dal_ref, dd_ref, dg_ref):
                r[...] = jnp.zeros_like(r)

        P = p_ref[...]
        c = _ssd_common(i, P, None, cw_ref[...], cb_ref[...], dtb_ref[...], al_ref[...], xc=xc_ref[...])
        xbc, z, e = c["xbc"], c["z"], c["e"]
        x = xbc[:, 0:D]
        yv = y_ref[...]
        sz = _sigmoid(z)
        silu_z = z * sz
        dyg, dg8 = _rms_bwd(dys_ref[...], yv * silu_z, g_ref[...])
        dg_ref[...] += dg8
        dy = dyg * silu_z
        dz = dyg * yv * (sz * (1.0 + z * (1.0 - sz)))
        d_e = _split_dot(d_ref[...], e, None, 3)
        dd_ref[...] += _rsum8(dy * x)
        xdt = x * c["dt_e"]
        a_last_e = c["acs_e"][CH - 1:CH, :]
        e_end = jnp.exp(a_last_e - c["acs_e"])
        w_end = xdt * e_end
        e_acs = jnp.exp(c["acs_e"])
        dy_dec = dy * e_acs
        m0, m1 = _pair_masks()
        lane16 = lax.broadcasted_iota(jnp.int32, (1, HEADS), 1)
        row16 = lax.broadcasted_iota(jnp.int32, (HEADS, 1), 0)
        dacs = jnp.zeros((CH, HEADS), f32)
        dacs_t = jnp.zeros((HEADS, CH), f32)
        dxdt_parts, dbs, dcs, zparts, yoff_parts, dlast_parts = [], [], [], [], [], []
        for g in range(2):
            sl = slice(512 * g, 512 * (g + 1))
            bg = xbc[:, D + NSTATE * g:D + NSTATE * (g + 1)].astype(bf16)
            cg = xbc[:, D + 2 * NSTATE + NSTATE * g:D + 2 * NSTATE + NSTATE * (g + 1)].astype(bf16)
            gmat = _dot(cg, bg, NT)
            st = st_ref[0, g]
            dstn = dst_ref[g]
            dstn_b = dstn.astype(bf16)
            y_off = _dot(cg, st.astype(bf16)) * e_acs[:, sl]
            yoff_parts.append(y_off)
            bds = _dot(bg, dstn_b)
            zparts.append(w_end[:, sl] * bds)
            dlast_parts.append(jnp.sum(dstn * st, axis=0, keepdims=True) * jnp.exp(a_last_e[:, sl]))
            dg_acc = jnp.zeros((CH, CH), f32)
            dxd = []
            for pr in range(4):
                lo = 128 * (4 * g + pr)
                xp2 = xdt[:, lo:lo + 128].astype(bf16)
                dy2 = dy[:, lo:lo + 128]
                outp = jnp.zeros((CH, 128), f32)
                for hh, msk in ((0, m0), (1, m1)):
                    h = 8 * g + 2 * pr + hh
                    lm = _lmat(c, h)
                    dyh = (dy2 * msk).astype(bf16)
                    mh = (gmat * lm).astype(bf16)
                    outp = outp + _dot(mh, dyh, TN)
                    dml = _dot(dyh, xp2, NT) * lm
                    dg_acc = dg_acc + dml
                    q = dml * gmat
                    dacs = dacs + jnp.where(lane16 == h, jnp.sum(q, axis=1, keepdims=True), 0.0)
                    dacs_t = dacs_t + jnp.where(row16 == h, jnp.sum(q, axis=0, keepdims=True), 0.0)
                dxd.append(outp)
            dxdt_parts.append(jnp.concatenate(dxd, axis=1) + e_end[:, sl] * bds)
            dgb = dg_acc.astype(bf16)
            dcs.append(_dot(dgb, bg) + _dot(dy_dec[:, sl].astype(bf16), st.astype(bf16), NT))
            dbs.append(_dot(dgb, cg, TN) + _dot(w_end[:, sl].astype(bf16), dstn_b, NT))
            dst_ref[g] = dstn * jnp.exp(a_last_e[:, sl]) + _dot(cg, dy_dec[:, sl].astype(bf16), TN)
        dxdt = jnp.concatenate(dxdt_parts, axis=1)
        zfull = jnp.concatenate(zparts, axis=1)
        y_off_full = jnp.concatenate(yoff_parts, axis=1)
        dlast = jnp.concatenate(dlast_parts, axis=1)
        red = lambda v: _split_dot(v, e, NT, 2)
        eye16 = jnp.where(lax.broadcasted_iota(jnp.int32, (HEADS, HEADS), 0) == lax.broadcasted_iota(jnp.int32, (HEADS, HEADS), 1),
                          1.0, 0.0).astype(bf16)
        dacs = dacs - _split_dot(dacs_t, eye16, TN, 3)
        zred = red(zfull)
        dacs = dacs + red(dy * y_off_full) - zred
        last_term = jnp.sum(zred, axis=0, keepdims=True) + red(dlast)
        rowc = lax.broadcasted_iota(jnp.int32, (CH, 1), 0)
        dacs = dacs + jnp.where(rowc == CH - 1, last_term, 0.0)
        r_i = lax.broadcasted_iota(jnp.int32, (CH, CH), 0)
        c_i = lax.broadcasted_iota(jnp.int32, (CH, CH), 1)
        ddta = _split_dot(dacs, jnp.where(c_i >= r_i, 1.0, 0.0).astype(bf16), None, 3, v_is_lhs=False)
        ddt = ddta * c["a"] + red(dxdt * x)
        dal_ref[...] += _rsum8(ddta * c["dt"] * c["a"])
        ddt_raw = ddt * _sigmoid(c["pre"]) * c["live"]
        ddtb_ref[...] += _rsum8(ddt_raw)
        dx = dy * d_e + dxdt * c["dt_e"]
        dxbc = jnp.concatenate([dx, dbs[0], dbs[1], dcs[0], dcs[1]], axis=1)
        sg = c["sg"]
        dxc = dxbc * (sg * (1.0 + c["xc"] * (1.0 - sg)))
        dcb_ref[...] += _rsum8(dxc)
        xp = c["xp"]
        row8 = lax.broadcasted_iota(jnp.int32, (8, 1), 0)
        cw = cw_ref[...]
        dxp = cw[3:4] * dxc
        dcw = jnp.where(row8 == 3, jnp.sum(dxc * xp, axis=0, keepdims=True), 0.0)
        nxt8 = nxt8_ref[...]
        for j in (1, 2, 3):
            rolled = pltpu.roll(dxc, CH - j, 0)
            fix = pltpu.roll(nxt8, 8 - j, 0)
            bot = jnp.where(row8 >= 8 - j, fix, rolled[CH - 8:CH])
            later = jnp.concatenate([rolled[:CH - 8], bot], axis=0)
            dxp = dxp + cw[3 - j:4 - j] * later
            dcw = dcw + jnp.where(row8 == 3 - j, jnp.sum(later * xp, axis=0, keepdims=True), 0.0)
        dcw_ref[...] += dcw
        nxt8_ref[...] = dxc[0:8]
        dp_ref[:, O_Z:O_Z + D] = dz.astype(bf16)
        dp_ref[:, O_U:O_U + D] = du_ref[...].astype(bf16)
        dp_ref[:, O_XBC:O_XBC + D_XBC] = dxp.astype(bf16)
        dp_ref[:, O_DT:W_PROJ] = jnp.zeros((CH, W_PROJ - O_DT), bf16)
        dp_ref[:, O_DT:O_DT + HEADS] = ddt_raw.astype(bf16)

    full = lambda a: pl.BlockSpec(a.shape, lambda s, nd=a.ndim: (0,) * nd)
    rev = lambda s: (nch - 1 - s, 0)
    acc = lambda cdim: pl.BlockSpec((8, cdim), lambda s: (0, 0))
    return _call(
        body, comm, name="ssd_bwd", nsteps=nch,
        in_specs=[pl.BlockSpec((CH, W_PROJ), rev), pl.BlockSpec((CH, D_XBC), rev),
                  pl.BlockSpec((CH, D), rev), pl.BlockSpec((CH, D), rev), pl.BlockSpec((CH, D), rev),
                  pl.BlockSpec((1, 2, NSTATE, 512), lambda s: (nch - 1 - s, 0, 0, 0))]
        + [full(a) for a in (conv_w, conv_b, dt_bias, a_log, d_ssd, g_ssd)],
        out_specs=[pl.BlockSpec((CH, W_PROJ), rev), acc(D_XBC), acc(D_XBC), acc(HEADS), acc(HEADS), acc(D), acc(D)],
        out_shape=[jax.ShapeDtypeStruct((nch * CH, W_PROJ), bf16)]
        + [jax.ShapeDtypeStruct((8, cdim), f32) for cdim in (D_XBC, D_XBC, HEADS, HEADS, D, D)],
        scratch_shapes=[pltpu.VMEM((8, D_XBC), f32), pltpu.VMEM((2, NSTATE, 512), f32)],
        args=[proj, xc_all, y, dys, du, states, conv_w, conv_b, dt_bias, a_log, d_ssd, g_ssd])


SCAN_UNROLL = 8


def _to_slabs(slab_ref, q, mat):
    for ls in range(8):
        slab_ref[ls, pl.ds(PITCH * q, CH), :] = mat[:, 128 * ls:128 * (ls + 1)]


def _from_slabs(slab, q):
    return jnp.concatenate([slab(ls, PITCH * q) for ls in range(8)], axis=1)


def _tile(slab_ref, ls, t, lead=None):
    idx = (ls, pl.ds(t, 8, stride=PITCH), slice(None))
    return slab_ref[idx] if lead is None else slab_ref[(lead,) + idx]


def _s5_fwd(proj, bbq, ccq, ar, ai, d_skip, nch, comm=None):
    def body(u_ref, bb_ref, cc_ref, ar_ref, ai_ref, d_ref, s_ref, yl_ref, y5_ref, bu_ref, st_ref):
        @pl.when(pl.program_id(0) == 0)
        def _():
            st_ref[...] = jnp.zeros_like(st_ref)
        u = u_ref[...]
        ub = u.astype(bf16)
        for q in range(NQ):
            _to_slabs(bu_ref, q, _dot(ub[:, 128 * q:128 * (q + 1)], bb_ref[q]))
        ar_t = [ar_ref[:, 128 * l:128 * (l + 1)] for l in range(4)]
        ai_t = [ai_ref[:, 128 * l:128 * (l + 1)] for l in range(4)]

        def one(t, carry):
            re, im = carry
            nre, nim = [], []
            for l in range(4):
                a = ar_t[l] * re[l] - ai_t[l] * im[l] + _tile(bu_ref, l, t)
                b = ar_t[l] * im[l] + ai_t[l] * re[l] + _tile(bu_ref, l + 4, t)
                s_ref[0, l, pl.ds(t, 8, stride=PITCH), :] = a
                s_ref[0, l + 4, pl.ds(t, 8, stride=PITCH), :] = b
                nre.append(a)
                nim.append(b)
            return tuple(nre), tuple(nim)

        def step(tt, carry):
            for k in range(SCAN_UNROLL):
                carry = one(tt * SCAN_UNROLL + k, carry)
            return carry
        init = (tuple(st_ref[l] for l in range(4)), tuple(st_ref[l + 4] for l in range(4)))
        re, im = lax.fori_loop(0, CH // SCAN_UNROLL, step, init)
        for l in range(4):
            st_ref[l] = re[l]
            st_ref[l + 4] = im[l]
        ys = []
        for q in range(NQ):
            sq = _from_slabs(lambda ls, r0: s_ref[0, ls, pl.ds(r0, CH), :], q).astype(bf16)
            ys.append(_dot(sq, cc_ref[q], NT))
        yl = jnp.concatenate(ys, axis=1) + u * d_ref[...]
        yl_ref[...] = yl
        y5_ref[...] = (0.5 * yl * (1.0 + lax.erf(yl * (1.0 / math.sqrt(2.0))))).astype(bf16)

    const = lambda a: pl.BlockSpec(a.shape, lambda i, nd=a.ndim: (0,) * nd)
    return _call(
        body, comm, name="s5_fwd", nsteps=nch,
        in_specs=[pl.BlockSpec((CH, D), lambda i: (i, O_U // D)), const(bbq), const(ccq), const(ar), const(ai), const(d_skip)],
        out_specs=[pl.BlockSpec((1, 8, 8 * PITCH, 128), lambda i: (i, 0, 0, 0)),
                   pl.BlockSpec((CH, D), lambda i: (i, 0)), pl.BlockSpec((CH, D), lambda i: (i, 0))],
        out_shape=[jax.ShapeDtypeStruct((nch, 8, 8 * PITCH, 128), f32), jax.ShapeDtypeStruct((nch * CH, D), f32),
                   jax.ShapeDtypeStruct((nch * CH, D), bf16)],
        scratch_shapes=[pltpu.VMEM((8, 8 * PITCH, 128), f32), pltpu.VMEM((8, 8, 128), f32)],
        args=[proj, bbq, ccq, ar, ai, d_skip])


def _s5_bwd(proj, dyl, s_all, bbq, ccq, ar, ai, d_skip, nch, comm=None):
    def body(u_ref, dy_ref, s_ref, bbt_ref, cct_ref, ar_ref, ai_ref, d_ref,
             du_ref, dcc_ref, dbb_ref, dab_ref, dd_ref, ga_ref, st_ref):
        @pl.when(pl.program_id(0) == 0)
        def _():
            st_ref[...] = jnp.zeros_like(st_ref)
            for r in (dcc_ref, dbb_ref, dab_ref, dd_ref):
                r[...] = jnp.zeros_like(r)
        u = u_ref[...]
        dyl_v = dy_ref[...]
        dd_ref[...] += _rsum8(dyl_v * u)
        ub = u.astype(bf16)
        dyb = dyl_v.astype(bf16)
        for q in range(NQ):
            _to_slabs(ga_ref, q, _dot(dyb[:, 128 * q:128 * (q + 1)], cct_ref[q]))
        ar_t = [ar_ref[:, 128 * l:128 * (l + 1)] for l in range(4)]
        ai_t = [ai_ref[:, 128 * l:128 * (l + 1)] for l in range(4)]

        def one(t, carry):
            re, im, dar, dai = carry
            nre, nim, ndar, ndai = [], [], [], []
            for l in range(4):
                sre = _tile(s_ref, l, t, lead=0)
                sim = _tile(s_ref, l + 4, t, lead=0)
                ndar.append(dar[l] + re[l] * sre + im[l] * sim)
                ndai.append(dai[l] + im[l] * sre - re[l] * sim)
                a = _tile(ga_ref, l, t) + ar_t[l] * re[l] + ai_t[l] * im[l]
                b = _tile(ga_ref, l + 4, t) - ai_t[l] * re[l] + ar_t[l] * im[l]
                ga_ref[l, pl.ds(t, 8, stride=PITCH), :] = a
                ga_ref[l + 4, pl.ds(t, 8, stride=PITCH), :] = b
                nre.append(a)
                nim.append(b)
            return tuple(nre), tuple(nim), tuple(ndar), tuple(ndai)

        def step(tt, carry):
            for k in range(SCAN_UNROLL):
                carry = one(CH - 1 - (tt * SCAN_UNROLL + k), carry)
            return carry
        four = lambda ref, o: tuple(ref[l + o] for l in range(4))
        re, im, dar, dai = lax.fori_loop(0, CH // SCAN_UNROLL, step,
                                         (four(st_ref, 0), four(st_ref, 4), four(dab_ref, 0), four(dab_ref, 4)))
        for l in range(4):
            st_ref[l], st_ref[l + 4] = re[l], im[l]
            dab_ref[l], dab_ref[l + 4] = dar[l], dai[l]
        dus = []
        for q in range(NQ):
            aq = _from_slabs(lambda ls, r0: ga_ref[ls, pl.ds(r0, CH), :], q).astype(bf16)
            sq = _from_slabs(lambda ls, r0: s_ref[0, ls, pl.ds(r0, CH), :], q).astype(bf16)
            dcc_ref[q] += _dot(dyb[:, 128 * q:128 * (q + 1)], sq, TN)
            dbb_ref[q] += _dot(ub[:, 128 * q:128 * (q + 1)], aq, TN)
            dus.append(_dot(aq, bbt_ref[q], NT))
        du_ref[...] = jnp.concatenate(dus, axis=1) + dyl_v * d_ref[...]

    const = lambda a: pl.BlockSpec(a.shape, lambda s, nd=a.ndim: (0,) * nd)
    rev = lambda s: (nch - 1 - s, 0)
    return _call(
        body, comm, name="s5_bwd", nsteps=nch,
        in_specs=[pl.BlockSpec((CH, D), lambda s: (nch - 1 - s, O_U // D)), pl.BlockSpec((CH, D), rev),
                  pl.BlockSpec((1, 8, 8 * PITCH, 128), lambda s: (nch - 1 - s, 0, 0, 0)),
                  const(bbq), const(ccq), const(ar), const(ai), const(d_skip)],
        out_specs=[pl.BlockSpec((CH, D), rev), pl.BlockSpec((NQ, 128, D), lambda s: (0, 0, 0)),
                   pl.BlockSpec((NQ, 128, D), lambda s: (0, 0, 0)), pl.BlockSpec((8, 8, 128), lambda s: (0, 0, 0)),
                   pl.BlockSpec((8, D), lambda s: (0, 0))],
        out_shape=[jax.ShapeDtypeStruct((nch * CH, D), f32), jax.ShapeDtypeStruct((NQ, 128, D), f32),
                   jax.ShapeDtypeStruct((NQ, 128, D), f32), jax.ShapeDtypeStruct((8, 8, 128), f32),
                   jax.ShapeDtypeStruct((8, D), f32)],
        scratch_shapes=[pltpu.VMEM((8, 8 * PITCH, 128), f32), pltpu.VMEM((8, 8, 128), f32)],
        args=[proj, dyl, s_all, bbq, ccq, ar, ai, d_skip])


def _s5_tables(lam_re, lam_im, log_step, b_re, b_im):
    step = jnp.exp(log_step)[:, None]
    mag = jnp.exp(lam_re * step)
    ab_re = mag * jnp.cos(lam_im * step)
    ab_im = mag * jnp.sin(lam_im * step)
    den = lam_re * lam_re + lam_im * lam_im
    coef_re = ((ab_re - 1.0) * lam_re + ab_im * lam_im) / den
    coef_im = (ab_im * lam_re - (ab_re - 1.0) * lam_im) / den
    bb_re = coef_re[..., None] * b_re - coef_im[..., None] * b_im
    bb_im = coef_re[..., None] * b_im + coef_im[..., None] * b_re
    return ab_re, ab_im, bb_re, bb_im


def _blockdiag_in(m_re, m_im):
    eye = jnp.eye(8, dtype=f32)

    def one(m):
        m = m.reshape(NQ, 8, S5_P, 16)
        return jnp.einsum("qgph,gk->qghkp", m, eye).reshape(NQ, 128, 512)
    return jnp.concatenate([one(m_re), one(m_im)], axis=2)


def _blockdiag_in_grad(dm):
    def one(x):
        x = x.reshape(NQ, 8, 16, 8, S5_P)
        return jnp.einsum("qghgp->qgph", x).reshape(NQ * 8, S5_P, 16)
    return one(dm[:, :, :512]), one(dm[:, :, 512:])


def _local_step(x2, tgt2, meta, p, w_in_t, shards):
    seq = x2.shape[0]
    nch = 1 + seq // CH
    bmb = BIG_ROWS if (nch * CH) % BIG_ROWS == 0 else CH
    nbig = nch * CH // bmb
    metablk = jnp.concatenate([jnp.zeros((CH - N_META, D), f32), meta, jnp.zeros((bmb - CH, D), f32)], axis=0)
    w_full = (w_in_t, pl.BlockSpec(w_in_t.shape, lambda i: (0, 0), pipeline_mode=pl.Buffered(1)))

    def lead(i, v):
        return jnp.logical_and(i == 0, lax.broadcasted_iota(jnp.int32, (v.shape[0], 1), 0) < CH)
    h0_of = lambda i, s, q: jnp.where(lead(i, s[0]), q[0][:s[0].shape[0]], s[0])

    def in_fn(i, r, s, q, w):
        nb = _rms(h0_of(i, s, q), q[1]).astype(bf16)
        return [_dot(nb, w[0][...], NT), nb], []
    (proj, n0), _, (g_up,) = _rowwise("in_proj", in_fn, nbig, shifted=[x2], pars=[metablk, p["g_mix"]], refs=[w_full],
                                      out_rows=[(W_PROJ, f32), (D, bf16)], bm=bmb, comm=_gather_piece(shards[0]))
    (y, y_ssd, states, xc_all), (g_down,) = _ssd_fwd(proj, p["conv_w"], p["conv_b"], p["dt_bias"], p["a_log"], p["d_ssd"],
                                                     p["g_ssd"], nch, comm=_gather_piece(shards[1]))

    ab_re, ab_im, bb_re, bb_im = _s5_tables(p["lam_re"], p["lam_im"], p["log_step"], p["b_re"], p["b_im"])
    ar, ai = ab_re.reshape(NQ, 512), ab_im.reshape(NQ, 512)
    bbq = _blockdiag_in(bb_re, bb_im)
    ccq = _blockdiag_in(jnp.swapaxes(p["c_re"], 1, 2), -jnp.swapaxes(p["c_im"], 1, 2))
    d_skip = p["d_s5"].reshape(1, D)
    bbq_b, ccq_b = bbq.astype(bf16), ccq.astype(bf16)
    (s_all, ylin, y5), (g_go,) = _s5_fwd(proj, bbq_b, ccq_b, ar, ai, d_skip, nch, comm=_gather_piece(shards[2]))
    whole = lambda a: (a, pl.BlockSpec(a.shape, lambda i: (0, 0, 0), pipeline_mode=pl.Buffered(1)))
    w_up, w_down = whole(g_up), whole(g_down)
    w_glu_t = (g_go, pl.BlockSpec((4, 512, D), lambda i: (0, 0, 0), pipeline_mode=pl.Buffered(1)))
    w_out = (g_go, pl.BlockSpec((4, 512, D), lambda i: (0, 1, 0), pipeline_mode=pl.Buffered(1)))

    def glu_fn(i, r, s, q, w):
        v = jnp.concatenate([_dot(r[0], w[0][k], NT) for k in range(4)], axis=1) + q[0]
        return [v, _rms(v[:, :D] * _sigmoid(v[:, D:]), q[1])], []
    (v, y_s5), _ = _rowwise("glu", glu_fn, nbig, rows=[y5], pars=[p["b_glu"], p["g_s5"]], refs=[w_glu_t],
                            out_rows=[(2 * D, bf16), (D, bf16)], bm=bmb)

    def out_fn(i, r, s, q, w):
        acc = (_dot(r[0][:, :512], w[0][0]) + _dot(r[0][:, 512:], w[0][1]) + _dot(r[1][:, :512], w[0][2])
               + _dot(r[1][:, 512:], w[0][3]))
        return [h0_of(i, s, q) + acc], []
    (h1,), _ = _rowwise("out_proj", out_fn, nbig, rows=[y_ssd, y_s5], shifted=[x2], pars=[metablk], refs=[w_out],
                        out_rows=[(D, f32)], bm=bmb)

    def up_fn(i, r, s, q, w):
        nb = _rms(r[0], q[0]).astype(bf16)
        return [jnp.concatenate([jnp.maximum(_dot(nb, w[0][k]), 0.0).astype(bf16) for k in range(4)], axis=1), nb], []
    (relu_m, n1), _ = _rowwise("up_proj", up_fn, nbig, rows=[h1], pars=[p["g_mlp"]], refs=[w_up],
                               out_rows=[(4 * D, bf16), (D, bf16)], bm=bmb)

    def down_fn(i, r, s, q, w):
        acc = None
        for k in range(4):
            t = r[0][:, D * k:D * (k + 1)]
            part = _dot(t * t, w[0][k])
            acc = part if acc is None else acc + part
        return [r[1] + acc], []
    (h2,), _ = _rowwise("down_proj", down_fn, nbig, rows=[relu_m, h1], refs=[w_down], out_rows=[(D, f32)], bm=bmb)

    def final_fn(i, r, s, q, w):
        err = jnp.where(lead(i, r[0]), 0.0, _rms(r[0], q[0]) - s[0])
        dh, dg8 = _rms_bwd(err * (1.0 / D), r[0], q[0])
        return [dh, dh], [_rsum8(err * err), dg8]
    (dh2, dh2_b), (loss8, dgf8) = _rowwise("final", final_fn, nbig, rows=[h2], shifted=[tgt2], pars=[p["g_final"]],
                                           out_rows=[(D, f32), (D, bf16)], out_accs=[D, D], bm=bmb)
    loss = 0.5 / D * jnp.sum(loss8)

    def down_bwd_fn(i, r, s, q, w):
        dm_ = [_dot(r[0], w[0][k], NT) * (2.0 * r[1][:, D * k:D * (k + 1)].astype(f32)) for k in range(4)]
        return [jnp.concatenate(dm_, axis=1)], []
    (dm,), _ = _rowwise("down_bwd", down_bwd_fn, nbig, rows=[dh2_b, relu_m], refs=[w_down], out_rows=[(4 * D, bf16)], bm=bmb)
    g_a = _dw_into("dw_down", relu_m, dh2_b, None, 1024, True, 1, 4, 0, piece_rows=2048, a_square=True)

    def up_bwd_fn(i, r, s, q, w):
        acc = _dot(r[0][:, :D], w[0][0], NT)
        for k in range(1, 4):
            acc = acc + _dot(r[0][:, D * k:D * (k + 1)], w[0][k], NT)
        dh, dg8 = _rms_bwd(acc, r[1], q[0])
        dh1_ = r[2] + dh
        return [dh1_, dh1_], [dg8]
    (dh1, dh1_b), (dgmlp8,) = _rowwise("up_bwd", up_bwd_fn, nbig, rows=[dm, h1, dh2], pars=[p["g_mlp"]], refs=[w_up],
                                       out_rows=[(D, f32), (D, bf16)], out_accs=[D], bm=bmb)
    g_a = _dw_into("dw_up", n1, dm, g_a, 1024, False, 0, 4, 0, piece_rows=2048)

    def out_bwd_fn(i, r, s, q, w):
        dmix = [_dot(r[0], w[0][k], NT) for k in range(4)]
        v1, v2 = r[1][:, :D].astype(f32), r[1][:, D:].astype(f32)
        s2 = _sigmoid(v2)
        dglu, dg8 = _rms_bwd(jnp.concatenate(dmix[2:], axis=1), v1 * s2, q[0])
        dv = jnp.concatenate([dglu * s2, dglu * v1 * s2 * (1.0 - s2)], axis=1)
        return [jnp.concatenate(dmix[:2], axis=1), dv], [dg8, _rsum8(dv)]
    (dys, dv), (dgs58, dbglu8) = _rowwise("out_bwd", out_bwd_fn, nbig, rows=[dh1_b, v], pars=[p["g_s5"]], refs=[w_out],
                                          out_rows=[(D, f32), (2 * D, bf16)], out_accs=[D, 2 * D], bm=bmb)
    g_b = _dw_into("dw_out_a", y_ssd, dh1_b, None, 512, True, 1, 2, 0, piece_rows=1024)
    g_b = _dw_into("dw_out_b", y_s5, dh1_b, g_b, 512, True, 1, 2, 2, piece_rows=1024)

    def glu_bwd_fn(i, r, s, q, w):
        acc = _dot(r[0][:, :512], w[0][0])
        for k in range(1, 4):
            acc = acc + _dot(r[0][:, 512 * k:512 * (k + 1)], w[0][k])
        yl = r[1]
        cdf = 0.5 * (1.0 + lax.erf(yl * (1.0 / math.sqrt(2.0))))
        pdf = jnp.exp(-0.5 * yl * yl) * (1.0 / math.sqrt(2.0 * math.pi))
        return [acc * (cdf + yl * pdf)], []
    (dylin,), _ = _rowwise("glu_bwd", glu_bwd_fn, nbig, rows=[dv, ylin], refs=[w_glu_t], out_rows=[(D, f32)], bm=bmb)
    g_b = _dw_into("dw_glu", dv, y5, g_b, 512, True, 0, 4, 0, piece_rows=1024)

    (du, dcc, dbb, dab, dds5), (land_a,) = _s5_bwd(proj, dylin, s_all, bbq_b, ccq_b, ar, ai, d_skip, nch,
                                                   comm=_scatter_piece(g_a))

    s8 = lambda a: jnp.sum(a, axis=0, keepdims=True)
    dab_q = jnp.swapaxes(dab.reshape(2, 4, NQ, 128), 1, 2).reshape(2, NQ * 8, S5_P)
    dbb_re, dbb_im = _blockdiag_in_grad(dbb)
    dcr, dci = _blockdiag_in_grad(dcc)
    _, vjp = jax.vjp(_s5_tables, p["lam_re"], p["lam_im"], p["log_step"], p["b_re"], p["b_im"])
    dlam_re, dlam_im, dlog_step, db_re, db_im = vjp((dab_q[0], dab_q[1], dbb_re, dbb_im))
    early = dict(lam_re=dlam_re, lam_im=dlam_im, log_step=dlog_step, b_re=db_re, b_im=db_im, c_re=jnp.swapaxes(dcr, 1, 2),
                 c_im=-jnp.swapaxes(dci, 1, 2), d_s5=s8(dds5).reshape(NQ * 8, 16), b_glu=s8(dbglu8), g_s5=s8(dgs58),
                 g_mlp=s8(dgmlp8), g_final=s8(dgf8).reshape(D))
    early_pack = _pack_small([early[n] for n in EARLY], _rows_for(EARLY))

    (dproj, dcw8, dcb8, ddtb8, dal8, dd8, dgssd8), (land_b, all_early) = _ssd_bwd(
        proj, xc_all, y, dys, du, states, p["conv_w"], p["conv_b"], p["dt_bias"], p["a_log"], p["d_ssd"], p["g_ssd"], nch,
        comm=_both(_scatter_piece(g_b), _gather_blocks(early_pack)))

    gt = _dw_in_t(dproj, n0)
    gt = jnp.concatenate([gt[0:1024], gt[O_XBC:O_XBC + D_XBC], gt[O_DT:O_DT + HEADS], gt[O_U:O_U + D]], axis=0).reshape(4, 900, D)
    g_c = jnp.concatenate([gt, jnp.zeros((4, C_ROWS - 900, D), bf16)], axis=1)

    def in_bwd_fn(i, r, s, q, w):
        dh, dg8 = _rms_bwd(_dot(r[0], w[0][...]), h0_of(i, s, q), q[1])
        return [r[1] + dh], [dg8]
    (dh0,), (dgmix8,), (land_c,) = _rowwise(
        "in_bwd", in_bwd_fn, nbig, rows=[dproj, dh1], shifted=[x2], pars=[metablk, p["g_mix"]], refs=[w_full],
        out_rows=[(D, f32)], out_accs=[D], bm=bmb, comm=_scatter_piece(g_c))

    hsum = lambda a: jnp.sum(s8(a).reshape(HEADS, HEAD_DIM), axis=1).reshape(1, HEADS)
    late = dict(g_mix=s8(dgmix8), conv_b=s8(dcb8), dt_bias=s8(ddtb8), a_log=s8(dal8), d_ssd=hsum(dd8), g_ssd=s8(dgssd8),
                conv_w=dcw8[0:4], meta_tokens=dh0[CH - N_META:CH], loss=loss.reshape(1))
    return dh0, [(g_a, land_a), (g_b, land_b), (g_c, land_c)], all_early, late


def _perm_rows_w_in(wt):
    return jnp.concatenate([wt[0:1024], wt[2576:3600], wt[1024:2560], wt[2560:2576],
                            jnp.zeros((W_PROJ - 3600, wt.shape[1]), wt.dtype)], axis=0)


def _allgather8(x_shard, name):
    m_per, n = x_shard.shape

    def body(x_ref, out_ref, send_sems, recv_sems, stage, local_sems):
        x, y, c = _place()
        me, sibling = (x, y, c), (x, y, 1 - c)
        chips = [(1 - x, y), (x, 1 - y), (1 - x, 1 - y)]

        def rows(px, py, pc):
            return out_ref.at[pl.ds((4 * px + 2 * py + pc) * m_per, m_per), :]

        def copy(k, block, to, src=None):
            return pltpu.make_async_remote_copy(
                src_ref=rows(*block) if src is None else src, dst_ref=rows(*block),
                send_sem=send_sems.at[k], recv_sem=recv_sems.at[k], device_id=to, device_id_type=MESH)

        load = pltpu.make_async_copy(x_ref, stage, local_sems.at[0])
        load.start()
        first = [copy(0, me, sibling, src=x_ref)]
        first += [copy(1 + j, me, (*chip, c), src=x_ref) for j, chip in enumerate(chips)]
        for cp in first:
            cp.start()
        load.wait()
        store = pltpu.make_async_copy(stage, rows(*me), local_sems.at[1])
        store.start()
        passed = [copy(4 + j, (*chip, c), sibling) for j, chip in enumerate(chips)]
        for j, chip in enumerate(chips):
            copy(1 + j, (*chip, c), me).wait_recv()
            passed[j].start()
        copy(0, sibling, me).wait_recv()
        for j, chip in enumerate(chips):
            copy(4 + j, (*chip, 1 - c), me).wait_recv()
        for cp in first + passed:
            cp.wait_send()
        store.wait()

    return pl.pallas_call(
        body, name=name, out_shape=jax.ShapeDtypeStruct((8 * m_per, n), x_shard.dtype),
        in_specs=[_ANY], out_specs=_ANY,
        scratch_shapes=[pltpu.SemaphoreType.DMA((7,)), pltpu.SemaphoreType.DMA((7,)), pltpu.VMEM((m_per, n), x_shard.dtype),
                        pltpu.SemaphoreType.DMA((2,))])(x_shard)


def _swap_sibling(parts, name):
    n = len(parts)

    def body(*refs):
        send_sems, recv_sems = refs[2 * n:]
        x, y, c = _place()
        cps = [pltpu.make_async_remote_copy(src_ref=refs[k], dst_ref=refs[n + k], send_sem=send_sems.at[k],
                                            recv_sem=recv_sems.at[k], device_id=(x, y, 1 - c), device_id_type=MESH)
               for k in range(n)]
        for cp in cps:
            cp.start()
        for cp in cps:
            cp.wait()

    return pl.pallas_call(
        body, name=name, out_shape=[jax.ShapeDtypeStruct(r.shape, r.dtype) for r in parts], in_specs=[_ANY] * n,
        out_specs=[_ANY] * n, scratch_shapes=[pltpu.SemaphoreType.DMA((n,)), pltpu.SemaphoreType.DMA((n,))])(*parts)


SH_CONVW, SH_META = 4 * 384, 16 * 256
SPARE_ROWS = 17

SMALL_SHAPES = dict(
    g_mix=(1, 1024), conv_b=(1, 1536), dt_bias=(1, 16), a_log=(1, 16), d_ssd=(1, 16), g_ssd=(1, 1024), lam_re=(1, 64, 64),
    lam_im=(1, 64, 64), log_step=(1, 64), b_re=(1, 64, 64, 16), b_im=(1, 64, 64, 16), c_re=(1, 64, 16, 64), c_im=(1, 64, 16, 64),
    d_s5=(1, 64, 16), b_glu=(1, 2048), g_s5=(1, 1024), g_mlp=(1, 1024), g_final=(1024,),
    conv_w=(4, D_XBC), meta_tokens=(N_META, D), loss=(1,))
EARLY = ["lam_re", "lam_im", "log_step", "b_re", "b_im", "c_re", "c_im", "d_s5", "b_glu", "g_s5", "g_mlp", "g_final"]
LATE = ["g_mix", "conv_b", "dt_bias", "a_log", "d_ssd", "g_ssd", "conv_w", "meta_tokens", "loss"]


def _rows_for(names):
    return -(-sum(math.prod(SMALL_SHAPES[n]) for n in names) // (8 * D)) * 8


def _pack_small(arrs, rows):
    flat = jnp.concatenate([a.reshape(-1).astype(f32) for a in arrs])
    return jnp.concatenate([flat, jnp.zeros((rows * D - flat.shape[0],), f32)]).reshape(rows, D)


def _unpack_small(slab, shapes):
    flat = slab.reshape(-1)
    out, o = [], 0
    for shp in shapes:
        n = math.prod(shp)
        out.append(flat[o:o + n].reshape(shp))
        o += n
    return out


def _sum8(g, rows, name):
    def body(g_ref, o_ref):
        acc = g_ref[0]
        for k in range(1, 8):
            acc = acc + g_ref[k]
        o_ref[...] = acc
    return pl.pallas_call(body, name=name, out_shape=jax.ShapeDtypeStruct((rows, D), f32),
                          compiler_params=_cp())(g.reshape(8, rows, D))


def _adam_math(w_, g_, m_, v_):
    m2 = ADAM_B1 * m_ + (1.0 - ADAM_B1) * g_
    v2 = ADAM_B2 * v_ + (1.0 - ADAM_B2) * jnp.square(g_)
    m_hat = m2 / (1.0 - ADAM_B1 ** ADAM_STEP)
    v_hat = v2 / (1.0 - ADAM_B2 ** ADAM_STEP)
    delta = -ADAM_LR * (m_hat / (jnp.sqrt(v_hat) + ADAM_EPS) + ADAM_WD * w_)
    return delta, m2, v2


def _adamw(name, w, g, m, v, bm):
    def fn(i, r, s, q, refs):
        return list(_adam_math(*r)), []
    c = w.shape[1]
    (d, m2, v2), _ = _rowwise(name, fn, w.shape[0] // bm, rows=[w, g, m, v], out_rows=[(c, f32)] * 3, bm=bm)
    return d, m2, v2


def _adamw_whole(name, w, g, m, v):
    def body(w_ref, g_ref, m_ref, v_ref, d_ref, m2_ref, v2_ref):
        d_ref[...], m2_ref[...], v2_ref[...] = _adam_math(w_ref[...], g_ref[...], m_ref[...], v_ref[...])
    return pl.pallas_call(body, name=name, out_shape=[jax.ShapeDtypeStruct(w.shape, f32)] * 3, compiler_params=_cp())(w, g, m, v)


def _sum_parts(name, own, land):
    def fn(i, r, s, q, refs):
        acc = r[0].astype(f32)
        for k in range(7):
            acc = acc + refs[0][k].astype(f32)
        return [acc], []
    rows = own.shape[0]
    bm = CH if rows % CH == 0 else rows
    (o,), _ = _rowwise(name, fn, rows // bm, rows=[own], refs=[(land, pl.BlockSpec((7, bm, D), lambda i: (0, i, 0)))],
                       out_rows=[(D, f32)], bm=bm)
    return o


def kernel(x, meta_tokens, g_mix, w_in, conv_w, conv_b, dt_bias, a_log, d_ssd, g_ssd, lam_re, lam_im, log_step, b_re, b_im, c_re, c_im, d_s5, w_glu, b_glu, g_s5, w_out, g_mlp, w_up, w_down, g_final, loss_target, m_meta_tokens, m_g_mix, m_w_in, m_conv_w, m_conv_b, m_dt_bias, m_a_log, m_d_ssd, m_g_ssd, m_lam_re, m_lam_im, m_log_step, m_b_re, m_b_im, m_c_re, m_c_im, m_d_s5, m_w_glu, m_b_glu, m_g_s5, m_w_out, m_g_mlp, m_w_up, m_w_down, m_g_final, v_meta_tokens, v_g_mix, v_w_in, v_conv_w, v_conv_b, v_dt_bias, v_a_log, v_d_ssd, v_g_ssd, v_lam_re, v_lam_im, v_log_step, v_b_re, v_b_im, v_c_re, v_c_im, v_d_s5, v_w_glu, v_b_glu, v_g_s5, v_w_out, v_g_mlp, v_w_up, v_w_down, v_g_final):
    given = dict(locals())
    cx, cy, cc = _place()
    chip = 2 * cx + cy

    small_f = jnp.concatenate([conv_w.reshape(-1), meta_tokens.reshape(-1)])
    t_hi = small_f.astype(bf16)
    r_1 = small_f - t_hi.astype(f32)
    t_mid = r_1.astype(bf16)
    t_lo = (r_1 - t_mid.astype(f32)).astype(bf16)
    terms = jnp.concatenate([t_hi, t_mid, t_lo])
    spare = jnp.concatenate([terms, jnp.zeros((SPARE_ROWS * D - terms.shape[0],), bf16)]).reshape(SPARE_ROWS, D)
    shards = (w_up[0].astype(bf16), w_down[0].astype(bf16),
              jnp.concatenate([w_glu[0].T.astype(bf16), w_out[0].astype(bf16)], axis=0))
    in_rows = jnp.concatenate([w_in[0].T.astype(bf16), spare, jnp.zeros((1024 - 900 - SPARE_ROWS, D), bf16)], axis=0)
    my_half = lax.dynamic_slice_in_dim(in_rows, cc * 512, 512, axis=0)
    gathered = _allgather8(my_half, "gather_w_in").reshape(4, 1024, D)
    w_in_t = _perm_rows_w_in(jnp.concatenate([gathered[s, 0:900] for s in range(4)], axis=0))
    n_sf = SH_CONVW + SH_META
    tr = gathered[:, 900:900 + SPARE_ROWS].reshape(4, SPARE_ROWS * D)[:, :3 * n_sf].astype(f32).reshape(4, 3, n_sf)
    sp = tr[:, 0] + tr[:, 1] + tr[:, 2]
    conv_w_full = jnp.concatenate([sp[s, :SH_CONVW].reshape(4, 384) for s in range(4)], axis=1)
    meta_full = jnp.concatenate([sp[s, SH_CONVW:].reshape(16, 256) for s in range(4)], axis=1)

    p = dict(g_mix=g_mix, conv_w=conv_w_full, conv_b=conv_b, dt_bias=dt_bias, a_log=a_log, d_ssd=d_ssd, g_ssd=g_ssd,
             lam_re=lam_re[0], lam_im=lam_im[0], log_step=log_step[0], b_re=b_re[0], b_im=b_im[0], c_re=c_re[0], c_im=c_im[0],
             d_s5=d_s5[0], b_glu=b_glu, g_s5=g_s5, g_mlp=g_mlp, g_final=g_final.reshape(1, D))
    dh0, pieces, all_early, late = _local_step(x[0], loss_target[0], meta_full, p, w_in_t, shards)
    grad_x = dh0[CH:].reshape(x.shape)

    reds = []
    for k, (gp, land) in enumerate(pieces):
        half = gp.shape[1] // 2
        own = lax.dynamic_slice(gp, (chip, cc * half, 0), (1, half, D)).reshape(half, D)
        reds.append(_sum_parts("rs_sum_%d" % k, own, land))
    others = _swap_sibling(reds, "rs_share")
    lower = [jnp.where(cc == 0, r, o) for r, o in zip(reds, others)]
    upper = [jnp.where(cc == 0, o, r) for r, o in zip(reds, others)]
    g_up, g_down = lower[0], upper[0]
    g_glu, g_out = lower[1].T, upper[1]
    g_in_t = jnp.concatenate([lower[2], upper[2][:900 - C_ROWS // 2]], axis=0)

    gs = dict(zip(EARLY, _unpack_small(_sum8(all_early, _rows_for(EARLY), "sum8_early"), [SMALL_SHAPES[n] for n in EARLY])))
    all_late = _allgather8(_pack_small([late[n] for n in LATE], _rows_for(LATE)), "gather_small")
    gs.update(zip(LATE, _unpack_small(_sum8(all_late, _rows_for(LATE), "sum8_late"), [SMALL_SHAPES[n] for n in LATE])))
    g_conv_w = lax.dynamic_slice_in_dim(gs.pop("conv_w"), chip * 384, 384, axis=1).reshape(conv_w.shape)
    g_meta = lax.dynamic_slice_in_dim(gs.pop("meta_tokens"), chip * 256, 256, axis=1)
    loss = gs.pop("loss").reshape(())

    grads = dict(gs, meta_tokens=g_meta, conv_w=g_conv_w, w_in=g_in_t.T.reshape(w_in.shape), w_glu=g_glu.reshape(w_glu.shape),
                 w_out=g_out.reshape(w_out.shape), w_up=g_up.reshape(w_up.shape), w_down=g_down.reshape(w_down.shape))
    delta, new_m, new_v = {}, {}, {}
    d_, m_, v_ = _adamw_whole("adamw_w_in", w_in[0].T, g_in_t, m_w_in[0].T, v_w_in[0].T)
    delta["w_in"], new_m["w_in"], new_v["w_in"] = (a.T.reshape(w_in.shape) for a in (d_, m_, v_))
    for n in ("w_glu", "w_out", "w_up", "w_down"):
        shp = given[n].shape
        two = lambda a: a.reshape(shp[1], shp[2])
        d_, m_, v_ = _adamw("adamw_" + n, two(given[n]), two(grads[n]), two(given["m_" + n]), two(given["v_" + n]), 256)
        delta[n], new_m[n], new_v[n] = d_.reshape(shp), m_.reshape(shp), v_.reshape(shp)
    for n in EARLY + LATE[:-1]:
        shp = given[n].shape
        if len(shp) == 4 and shp[-1] == 16:
            two = back = lambda a: jnp.swapaxes(a, -1, -2)
        else:
            two = (lambda a: a.reshape(1, -1)) if len(shp) == 1 else (lambda a: a)
            back = lambda a: a.reshape(shp)
        d_, m_, v_ = _adamw_whole("adamw_" + n, two(given[n]), two(grads[n].reshape(shp)), two(given["m_" + n]), two(given["v_" + n]))
        delta[n], new_m[n], new_v[n] = back(d_), back(m_), back(v_)

    order = ["meta_tokens", "g_mix", "w_in", "conv_w", "conv_b", "dt_bias", "a_log", "d_ssd", "g_ssd", "lam_re", "lam_im", "log_step",
             "b_re", "b_im", "c_re", "c_im", "d_s5", "w_glu", "b_glu", "g_s5", "w_out", "g_mlp", "w_up", "w_down", "g_final"]
    grads_out = [grads[n].reshape(given[n].shape) for n in order]
    return (loss, grad_x, *grads_out, *[delta[n] for n in order], *[new_m[n] for n in order], *[new_v[n] for n in order])
```

```python
import math

import jax
import jax.numpy as jnp
from jax import lax
from jax.experimental import pallas as pl
from jax.experimental.pallas import tpu as pltpu

f32 = jnp.float32
bf16 = jnp.bfloat16

D = 1024
N_META = 16
CH = 256
HEADS = 16
HEAD_DIM = 64
NSTATE = 128
D_XBC = 1536
S5_P = 64
NQ = 8
PITCH = CH + 4
EPS = 1e-5
O_Z, O_U, O_XBC, O_DT, W_PROJ = 0, 1024, 2048, 3584, 3712
VMEM_LIMIT = 60 * 1024 * 1024

ADAM_LR, ADAM_B1, ADAM_B2, ADAM_EPS, ADAM_WD, ADAM_STEP = 0.001, 0.9, 0.999, 1e-08, 0.01, 10

NT = (((1,), (1,)), ((), ()))
TN = (((0,), (0,)), ((), ()))
_ANY = pl.BlockSpec(memory_space=pl.ANY)


def _cp(sem=None):
    return pltpu.CompilerParams(dimension_semantics=sem, vmem_limit_bytes=VMEM_LIMIT)


def _sigmoid(v):
    return 1.0 / (1.0 + jnp.exp(-v))


def _rsum8(v):
    r, c = v.shape
    return jnp.sum(v.reshape(r // 8, 8, c), axis=0)


def _rms(h, g):
    r = lax.rsqrt(jnp.mean(h * h, axis=-1, keepdims=True) + EPS)
    return h * r * g


def _rms_bwd(dy, h, g):
    r = lax.rsqrt(jnp.mean(h * h, axis=-1, keepdims=True) + EPS)
    n = h * r
    dn = dy * g
    dh = r * (dn - n * jnp.mean(dn * n, axis=-1, keepdims=True))
    return dh, _rsum8(dy * n)


def _dot(a, b, dims=None):
    if dims is None:
        return jnp.dot(a, b, preferred_element_type=f32)
    return lax.dot_general(a, b, dims, preferred_element_type=f32)


def _split_dot(v, m01, dims, terms, v_is_lhs=True):
    out, r = None, v
    for _ in range(terms):
        piece = r.astype(bf16)
        o = _dot(piece, m01, dims) if v_is_lhs else _dot(m01, piece, dims)
        out = o if out is None else out + o
        r = r - piece.astype(f32)
    return out


MESH = pl.DeviceIdType.MESH


def _place():
    return lax.axis_index("x"), lax.axis_index("y"), lax.axis_index("c")


def _flip(v, f):
    return 1 - v if f else v


def _call(body, comm, *, name, nsteps, in_specs, out_specs, out_shape, scratch_shapes, args):
    n_in, n_out, n_scr = len(in_specs), len(out_specs), len(scratch_shapes)
    if comm is None:
        res = pl.pallas_call(body, name=name, grid=(nsteps,), in_specs=in_specs, out_specs=out_specs, out_shape=out_shape,
                             scratch_shapes=scratch_shapes, compiler_params=_cp(("arbitrary",)))(*args)
        return list(res), []
    c_in, c_out = len(comm["ins"]), len(comm["outs"])

    def wrapped(*refs):
        o0 = n_in + c_in
        s0 = o0 + n_out + c_out
        cparts = (refs[n_in:o0], refs[o0 + n_out:s0], refs[s0 + n_scr:])

        @pl.when(pl.program_id(0) == 0)
        def _():
            comm["start"](*cparts)
        if "middle" in comm:
            @pl.when(pl.program_id(0) == (3 * nsteps) // 4)
            def _():
                comm["middle"](*cparts)
        body(*refs[:n_in], *refs[o0:o0 + n_out], *refs[s0:s0 + n_scr])

        @pl.when(pl.program_id(0) == nsteps - 1)
        def _():
            comm["finish"](*cparts)

    any_spec = pl.BlockSpec(memory_space=pl.ANY)
    res = pl.pallas_call(
        wrapped, name=name, grid=(nsteps,), in_specs=list(in_specs) + [any_spec] * c_in,
        out_specs=list(out_specs) + [any_spec] * c_out, out_shape=list(out_shape) + list(comm["outs"]),
        scratch_shapes=list(scratch_shapes) + list(comm["scratch"]),
        compiler_params=_cp(("arbitrary",)))(*args, *comm["ins"])
    return list(res[:n_out]), list(res[n_out:])


def _gather_piece(slab):
    r0, rows = 0, slab.shape[0]
    half = rows // 2
    flips = ((1, 0), (0, 1), (1, 1))

    def first(j, slab_ref, out_ref, send_sems, recv_sems):
        x, y, c = _place()
        return pltpu.make_async_remote_copy(
            src_ref=slab_ref.at[pl.ds(r0 + c * half, half), :], dst_ref=out_ref.at[2 * x + y, pl.ds(c * half, half), :],
            send_sem=send_sems.at[j], recv_sem=recv_sems.at[j],
            device_id=(_flip(x, flips[j][0]), _flip(y, flips[j][1]), c), device_id_type=MESH)

    def passed(j, out_ref, send_sems, recv_sems):
        x, y, c = _place()
        rows_j = out_ref.at[2 * _flip(x, flips[j][0]) + _flip(y, flips[j][1]), pl.ds(c * half, half), :]
        return pltpu.make_async_remote_copy(src_ref=rows_j, dst_ref=rows_j, send_sem=send_sems.at[3 + j],
                                            recv_sem=recv_sems.at[3 + j], device_id=(x, y, 1 - c), device_id_type=MESH)

    def start(ins, outs, scr):
        send_sems, recv_sems, stage, local_sems = scr
        x, y, _ = _place()
        load = pltpu.make_async_copy(ins[0].at[pl.ds(r0, rows), :], stage, local_sems.at[0])
        load.start()
        for j in range(3):
            first(j, ins[0], outs[0], send_sems, recv_sems).start()
        load.wait()
        pltpu.make_async_copy(stage, outs[0].at[2 * x + y], local_sems.at[1]).start()

    def middle(ins, outs, scr):
        send_sems, recv_sems, _, _ = scr
        for j in range(3):
            first(j, ins[0], outs[0], send_sems, recv_sems).wait_recv()
            passed(j, outs[0], send_sems, recv_sems).start()

    def finish(ins, outs, scr):
        send_sems, recv_sems, stage, local_sems = scr
        x, y, c = _place()
        for j in range(3):
            sib = outs[0].at[2 * _flip(x, flips[j][0]) + _flip(y, flips[j][1]), pl.ds((1 - c) * half, half), :]
            pltpu.make_async_remote_copy(src_ref=sib, dst_ref=sib, send_sem=send_sems.at[3 + j], recv_sem=recv_sems.at[3 + j],
                                         device_id=(x, y, 1 - c), device_id_type=MESH).wait_recv()
        for j in range(3):
            first(j, ins[0], outs[0], send_sems, recv_sems).wait_send()
            passed(j, outs[0], send_sems, recv_sems).wait_send()
        pltpu.make_async_copy(stage, outs[0].at[2 * x + y], local_sems.at[1]).wait()

    return dict(ins=[slab], outs=[jax.ShapeDtypeStruct((4, rows, D), bf16)],
                scratch=[pltpu.SemaphoreType.DMA((6,)), pltpu.SemaphoreType.DMA((6,)), pltpu.VMEM((rows, D), bf16),
                         pltpu.SemaphoreType.DMA((2,))], start=start, middle=middle, finish=finish)


def _scatter_piece(gpiece):
    half = gpiece.shape[1] // 2

    def copies(g_ref, land_ref, send_sems, recv_sems):
        x, y, c = _place()
        cps = []
        for fx in (0, 1):
            for fy in (0, 1):
                for fc in (0, 1):
                    k = 4 * fx + 2 * fy + fc - 1
                    if k < 0:
                        continue
                    px, py, pc = _flip(x, fx), _flip(y, fy), _flip(c, fc)
                    cps.append(pltpu.make_async_remote_copy(
                        src_ref=g_ref.at[2 * px + py, pl.ds(pc * half, half), :], dst_ref=land_ref.at[k],
                        send_sem=send_sems.at[k], recv_sem=recv_sems.at[k], device_id=(px, py, pc), device_id_type=MESH))
        return cps

    def start(ins, outs, scr):
        for cp in copies(ins[0], outs[0], *scr):
            cp.start()

    def finish(ins, outs, scr):
        for cp in copies(ins[0], outs[0], *scr):
            cp.wait()

    return dict(ins=[gpiece], outs=[jax.ShapeDtypeStruct((7, half, D), gpiece.dtype)],
                scratch=[pltpu.SemaphoreType.DMA((7,)), pltpu.SemaphoreType.DMA((7,))], start=start, finish=finish)


def _gather_blocks(block):
    rows = block.shape[0]

    def mine(out_ref):
        x, y, c = _place()
        return out_ref.at[pl.ds((4 * x + 2 * y + c) * rows, rows), :]

    def copies(b_ref, out_ref, send_sems, recv_sems):
        x, y, c = _place()
        cps = []
        for fx in (0, 1):
            for fy in (0, 1):
                for fc in (0, 1):
                    k = 4 * fx + 2 * fy + fc - 1
                    if k < 0:
                        continue
                    cps.append(pltpu.make_async_remote_copy(
                        src_ref=b_ref, dst_ref=mine(out_ref), send_sem=send_sems.at[k], recv_sem=recv_sems.at[k],
                        device_id=(_flip(x, fx), _flip(y, fy), _flip(c, fc)), device_id_type=MESH))
        return cps

    def start(ins, outs, scr):
        send_sems, recv_sems, stage, local_sems = scr
        load = pltpu.make_async_copy(ins[0], stage, local_sems.at[0])
        load.start()
        for cp in copies(ins[0], outs[0], send_sems, recv_sems):
            cp.start()
        load.wait()
        pltpu.make_async_copy(stage, mine(outs[0]), local_sems.at[1]).start()

    def finish(ins, outs, scr):
        send_sems, recv_sems, stage, local_sems = scr
        for cp in copies(ins[0], outs[0], send_sems, recv_sems):
            cp.wait()
        pltpu.make_async_copy(stage, mine(outs[0]), local_sems.at[1]).wait()

    return dict(ins=[block], outs=[jax.ShapeDtypeStruct((8 * rows, D), block.dtype)],
                scratch=[pltpu.SemaphoreType.DMA((7,)), pltpu.SemaphoreType.DMA((7,)), pltpu.VMEM((rows, D), block.dtype),
                         pltpu.SemaphoreType.DMA((2,))], start=start, finish=finish)


def _both(c1, c2):
    n = (len(c1["ins"]), len(c1["outs"]), len(c1["scratch"]))

    def split(parts):
        return [p[:k] for p, k in zip(parts, n)], [p[k:] for p, k in zip(parts, n)]

    def start(*parts):
        a, b = split(parts)
        c1["start"](*a)
        c2["start"](*b)

    def finish(*parts):
        a, b = split(parts)
        c1["finish"](*a)
        c2["finish"](*b)

    both = dict(ins=c1["ins"] + c2["ins"], outs=c1["outs"] + c2["outs"], scratch=c1["scratch"] + c2["scratch"],
                start=start, finish=finish)
    if "middle" in c1 or "middle" in c2:
        def middle(*parts):
            for cm, part in zip((c1, c2), split(parts)):
                if "middle" in cm:
                    cm["middle"](*part)
        both["middle"] = middle
    return both


def _rowwise(name, fn, nblk, rows=(), shifted=(), pars=(), refs=(), out_rows=(), out_accs=(), bm=CH, comm=None):
    n_sub = bm // CH
    n_r, n_s, n_p, n_w = len(rows), len(shifted) * n_sub, len(pars), len(refs)
    n_in = n_r + n_s + n_p + n_w
    n_o, n_a = len(out_rows), len(out_accs)

    def body(*all_refs):
        i = pl.program_id(0)
        ins = all_refs[:n_in]
        outs = all_refs[n_in:]
        rv = [r[...] for r in ins[:n_r]]
        sub = ins[n_r:n_r + n_s]
        sv = [jnp.concatenate([r[...] for r in sub[k * n_sub:(k + 1) * n_sub]], axis=0) if n_sub > 1 else sub[k][...]
              for k in range(len(shifted))]
        pv = [r[...] for r in ins[n_r + n_s:n_r + n_s + n_p]]
        ro, ao = fn(i, rv, sv, pv, list(ins[n_r + n_s + n_p:]))
        for r, v in zip(outs[:n_o], ro):
            r[...] = v.astype(r.dtype)
        accs = outs[n_o:]

        @pl.when(i == 0)
        def _():
            for r in accs:
                r[...] = jnp.zeros_like(r)
        for r, v in zip(accs, ao):
            r[...] += v

    in_specs = [pl.BlockSpec((bm, a.shape[1]), lambda i: (i, 0)) for a in rows]
    in_specs += [pl.BlockSpec((CH, a.shape[1]), lambda i, j=j: (jnp.maximum(n_sub * i - 1 + j, 0), 0))
                 for a in shifted for j in range(n_sub)]
    in_specs += [pl.BlockSpec(a.shape, lambda i, nd=a.ndim: (0,) * nd) for a in pars]
    in_specs += [spec for _, spec in refs]
    drop = [len(o) > 2 for o in out_rows]
    assert not any(drop) or bm == CH
    out_specs = [pl.BlockSpec((bm, o[0]), (lambda i: (jnp.maximum(i - 1, 0), 0)) if d else (lambda i: (i, 0)))
                 for o, d in zip(out_rows, drop)]
    out_specs += [pl.BlockSpec((8, c), lambda i: (0, 0)) for c in out_accs]
    out_shape = [jax.ShapeDtypeStruct((nblk * bm - (CH if d else 0), o[0]), o[1]) for o, d in zip(out_rows, drop)]
    out_shape += [jax.ShapeDtypeStruct((8, c), f32) for c in out_accs]
    res, cres = _call(body, comm, name=name, nsteps=nblk, in_specs=in_specs, out_specs=out_specs, out_shape=out_shape,
                      scratch_shapes=[], args=[*rows, *[a for a in shifted for _ in range(n_sub)], *pars, *[a for a, _ in refs]])
    parts = (res[:n_o], res[n_o:])
    return parts if comm is None else parts + (cres,)


C_ROWS = 928
BIG_ROWS = 768
DW_ROWS = 2816


def _contract_rows(lp, big=DW_ROWS):
    for rows in (big, BIG_ROWS):
        if lp % rows == 0:
            return rows
    return CH


def _dw_into(name, a, b, slab, ka, a_sharded, row_blk, n_s, s0, piece_rows=2048, a_square=False):
    lp = a.shape[0]
    bm = _contract_rows(lp)
    steps = lp // bm

    def body(a_ref, b_ref, *rest):
        o_ref, acc = rest[-2], rest[-1]
        k = pl.program_id(1)

        @pl.when(k == 0)
        def _():
            acc[...] = jnp.zeros_like(acc)
        a_v = a_ref[...]
        acc[...] += _dot(a_v * a_v if a_square else a_v, b_ref[...], TN)

        @pl.when(k == steps - 1)
        def _():
            o_ref[0] = acc[...].astype(bf16)

    in_specs = [pl.BlockSpec((bm, ka), (lambda s, k: (k, s)) if a_sharded else (lambda s, k: (k, 0))),
                pl.BlockSpec((bm, D), (lambda s, k: (k, 0)) if a_sharded else (lambda s, k: (k, s)))]
    args = [a, b]
    aliases = {}
    if slab is not None:
        in_specs.append(_ANY)
        args.append(slab)
        aliases = {2: 0}
    return pl.pallas_call(
        body, name=name, grid=(n_s, steps), in_specs=in_specs,
        out_specs=pl.BlockSpec((1, ka, D), lambda s, k: (s0 + s, row_blk, 0)),
        out_shape=jax.ShapeDtypeStruct((4, piece_rows, D), bf16),
        scratch_shapes=[pltpu.VMEM((ka, D), f32)], input_output_aliases=aliases,
        compiler_params=_cp(("arbitrary", "arbitrary")))(*args)


def _dw_in_t(dproj, n0):
    lp = n0.shape[0]
    bm = _contract_rows(lp, BIG_ROWS)
    steps = lp // bm
    bn = 512

    def body(a_ref, b_ref, o_ref, acc):
        k = pl.program_id(1)

        @pl.when(k == 0)
        def _():
            acc[...] = jnp.zeros_like(acc)
        acc[...] += _dot(a_ref[...], b_ref[...], TN)

        @pl.when(k == steps - 1)
        def _():
            o_ref[...] = acc[...].astype(bf16)

    return pl.pallas_call(
        body, name="dw_in", grid=(D // bn, steps),
        in_specs=[pl.BlockSpec((bm, W_PROJ), lambda j, k: (k, 0)), pl.BlockSpec((bm, bn), lambda j, k: (k, j))],
        out_specs=pl.BlockSpec((W_PROJ, bn), lambda j, k: (0, j)),
        out_shape=jax.ShapeDtypeStruct((W_PROJ, D), bf16),
        scratch_shapes=[pltpu.VMEM((W_PROJ, bn), f32)],
        compiler_params=_cp(("arbitrary", "arbitrary")))(dproj, n0)


def _head_expand():
    h = lax.broadcasted_iota(jnp.int32, (HEADS, D), 0)
    c = lax.broadcasted_iota(jnp.int32, (HEADS, D), 1)
    return jnp.where((c >> 6) == h, 1.0, 0.0).astype(bf16)


def _ssd_common(i, P, prev8, cw, cb, dtb, alog, xc=None):
    z = P[:, O_Z:O_Z + D]
    xp = P[:, O_XBC:O_XBC + D_XBC]
    dt_raw = P[:, O_DT:O_DT + HEADS]
    row = lax.broadcasted_iota(jnp.int32, (CH, 1), 0)
    if xc is None:
        row8 = lax.broadcasted_iota(jnp.int32, (8, 1), 0)
        xc = cb + cw[3:4] * xp
        for k in (1, 2, 3):
            rolled = pltpu.roll(xp, k, 0)
            fix = pltpu.roll(prev8, k, 0)
            top = jnp.where(row8 < k, fix, rolled[0:8])
            xc = xc + cw[3 - k:4 - k] * jnp.concatenate([top, rolled[8:]], axis=0)
    sg = _sigmoid(xc)
    xbc = xc * sg
    live = jnp.where(jnp.logical_or(i > 0, row >= CH - N_META), 1.0, 0.0)
    pre = dt_raw + dtb
    dt = jnp.where(pre > 20.0, pre, jnp.log(1.0 + jnp.exp(jnp.minimum(pre, 20.0)))) * live
    a = -jnp.exp(alog)
    dta = dt * a
    r_i = lax.broadcasted_iota(jnp.int32, (CH, CH), 0)
    c_i = lax.broadcasted_iota(jnp.int32, (CH, CH), 1)
    tril = r_i >= c_i
    acs = _split_dot(dta, jnp.where(tril, 1.0, 0.0).astype(bf16), None, 3, v_is_lhs=False)
    acs_t = _split_dot(dta, jnp.where(r_i <= c_i, 1.0, 0.0).astype(bf16), TN, 3)
    e = _head_expand()
    acs_e = _split_dot(acs, e, None, 3)
    dt_e = _split_dot(dt, e, None, 3)
    return dict(z=z, xp=xp, xc=xc, sg=sg, xbc=xbc, live=live, pre=pre, dt=dt, a=a, tril=tril,
                acs=acs, acs_t=acs_t, e=e, acs_e=acs_e, dt_e=dt_e)


def _lmat(c, h):
    seg = c["acs"][:, h:h + 1] - c["acs_t"][h:h + 1, :]
    return jnp.where(c["tril"], jnp.exp(jnp.minimum(seg, 0.0)), 0.0)


def _pair_masks():
    lane = lax.broadcasted_iota(jnp.int32, (1, 128), 1)
    return jnp.where(lane < HEAD_DIM, 1.0, 0.0), jnp.where(lane >= HEAD_DIM, 1.0, 0.0)


def _ssd_fwd(proj, conv_w, conv_b, dt_bias, a_log, d_ssd, g_ssd, nch, comm=None):
    def body(p_ref, cw_ref, cb_ref, dtb_ref, al_ref, d_ref, g_ref, y_ref, ys_ref, st_ref, xc_ref, prev8_ref, state_ref):
        i = pl.program_id(0)

        @pl.when(i == 0)
        def _():
            prev8_ref[...] = jnp.zeros_like(prev8_ref)
            state_ref[...] = jnp.zeros_like(state_ref)

        P = p_ref[...]
        c = _ssd_common(i, P, prev8_ref[...], cw_ref[...], cb_ref[...], dtb_ref[...], al_ref[...])
        prev8_ref[...] = c["xp"][CH - 8:CH]
        xc_ref[...] = c["xc"]
        xbc = c["xbc"]
        x = xbc[:, 0:D]
        xdt = x * c["dt_e"]
        a_last_e = c["acs_e"][CH - 1:CH, :]
        w_end = (xdt * jnp.exp(a_last_e - c["acs_e"])).astype(bf16)
        m0, m1 = _pair_masks()
        ys = []
        for g in range(2):
            bg = xbc[:, D + NSTATE * g:D + NSTATE * (g + 1)].astype(bf16)
            cg = xbc[:, D + 2 * NSTATE + NSTATE * g:D + 2 * NSTATE + NSTATE * (g + 1)].astype(bf16)
            gmat = _dot(cg, bg, NT)
            st = state_ref[g]
            st_ref[0, g] = st
            sl = slice(512 * g, 512 * (g + 1))
            y_off = _dot(cg, st.astype(bf16)) * jnp.exp(c["acs_e"][:, sl])
            contrib = _dot(bg, w_end[:, sl], TN)
            state_ref[g] = st * jnp.exp(a_last_e[:, sl]) + contrib
            yd = []
            for pr in range(4):
                h0 = 8 * g + 2 * pr
                xp2 = xdt[:, 128 * (4 * g + pr):128 * (4 * g + pr + 1)]
                ma = (gmat * _lmat(c, h0)).astype(bf16)
                mb = (gmat * _lmat(c, h0 + 1)).astype(bf16)
                yd.append(_dot(ma, (xp2 * m0).astype(bf16)) + _dot(mb, (xp2 * m1).astype(bf16)))
            ys.append(jnp.concatenate(yd, axis=1) + y_off)
        d_e = _split_dot(d_ref[...], c["e"], None, 3)
        y = jnp.concatenate(ys, axis=1) + x * d_e
        y_ref[...] = y
        yg = y * (c["z"] * _sigmoid(c["z"]))
        ys_ref[...] = _rms(yg, g_ref[...]).astype(bf16)

    full = lambda a: pl.BlockSpec(a.shape, lambda i, nd=a.ndim: (0,) * nd)
    return _call(
        body, comm, name="ssd_fwd", nsteps=nch,
        in_specs=[pl.BlockSpec((CH, W_PROJ), lambda i: (i, 0))] + [full(a) for a in (conv_w, conv_b, dt_bias, a_log, d_ssd, g_ssd)],
        out_specs=[pl.BlockSpec((CH, D), lambda i: (i, 0)), pl.BlockSpec((CH, D), lambda i: (i, 0)),
                   pl.BlockSpec((1, 2, NSTATE, 512), lambda i: (i, 0, 0, 0)), pl.BlockSpec((CH, D_XBC), lambda i: (i, 0))],
        out_shape=[jax.ShapeDtypeStruct((nch * CH, D), f32), jax.ShapeDtypeStruct((nch * CH, D), bf16),
                   jax.ShapeDtypeStruct((nch, 2, NSTATE, 512), f32), jax.ShapeDtypeStruct((nch * CH, D_XBC), f32)],
        scratch_shapes=[pltpu.VMEM((8, D_XBC), f32), pltpu.VMEM((2, NSTATE, 512), f32)],
        args=[proj, conv_w, conv_b, dt_bias, a_log, d_ssd, g_ssd])


def _ssd_bwd(proj, xc_all, y, dys, du, states, conv_w, conv_b, dt_bias, a_log, d_ssd, g_ssd, nch, comm=None):
    def body(p_ref, xc_ref, y_ref, dys_ref, du_ref, st_ref, cw_ref, cb_ref, dtb_ref, al_ref, d_ref, g_ref,
             dp_ref, dcw_ref, dcb_ref, ddtb_ref, dal_ref, dd_ref, dg_ref, nxt8_ref, dst_ref):
        step = pl.program_id(0)
        i = nch - 1 - step

        @pl.when(step == 0)
        def _():
            nxt8_ref[...] = jnp.zeros_like(nxt8_ref)
            dst_ref[...] = jnp.zeros_like(dst_ref)
            for r in (dcw_ref, dcb_ref, ddtb_ref, dal_ref, dd_ref, dg_ref):
                r[...] = jnp.zeros_like(r)

        P = p_ref[...]
        c = _ssd_common(i, P, None, cw_ref[...], cb_ref[...], dtb_ref[...], al_ref[...], xc=xc_ref[...])
        xbc, z, e = c["xbc"], c["z"], c["e"]
        x = xbc[:, 0:D]
        yv = y_ref[...]
        sz = _sigmoid(z)
        silu_z = z * sz
        dyg, dg8 = _rms_bwd(dys_ref[...], yv * silu_z, g_ref[...])
        dg_ref[...] += dg8
        dy = dyg * silu_z
        dz = dyg * yv * (sz * (1.0 + z * (1.0 - sz)))
        d_e = _split_dot(d_ref[...], e, None, 3)
        dd_ref[...] += _rsum8(dy * x)
        xdt = x * c["dt_e"]
        a_last_e = c["acs_e"][CH - 1:CH, :]
        e_end = jnp.exp(a_last_e - c["acs_e"])
        w_end = xdt * e_end
        e_acs = jnp.exp(c["acs_e"])
        dy_dec = dy * e_acs
        m0, m1 = _pair_masks()
        lane16 = lax.broadcasted_iota(jnp.int32, (1, HEADS), 1)
        row16 = lax.broadcasted_iota(jnp.int32, (HEADS, 1), 0)
        dacs = jnp.zeros((CH, HEADS), f32)
        dacs_t = jnp.zeros((HEADS, CH), f32)
        dxdt_parts, dbs, dcs, zparts, yoff_parts, dlast_parts = [], [], [], [], [], []
        for g in range(2):
            sl = slice(512 * g, 512 * (g + 1))
            bg = xbc[:, D + NSTATE * g:D + NSTATE * (g + 1)].astype(bf16)
            cg = xbc[:, D + 2 * NSTATE + NSTATE * g:D + 2 * NSTATE + NSTATE * (g + 1)].astype(bf16)
            gmat = _dot(cg, bg, NT)
            st = st_ref[0, g]
            dstn = dst_ref[g]
            dstn_b = dstn.astype(bf16)
            y_off = _dot(cg, st.astype(bf16)) * e_acs[:, sl]
            yoff_parts.append(y_off)
            bds = _dot(bg, dstn_b)
            zparts.append(w_end[:, sl] * bds)
            dlast_parts.append(jnp.sum(dstn * st, axis=0, keepdims=True) * jnp.exp(a_last_e[:, sl]))
            dg_acc = jnp.zeros((CH, CH), f32)
            dxd = []
            for pr in range(4):
                lo = 128 * (4 * g + pr)
                xp2 = xdt[:, lo:lo + 128].astype(bf16)
                dy2 = dy[:, lo:lo + 128]
                outp = jnp.zeros((CH, 128), f32)
                for hh, msk in ((0, m0), (1, m1)):
                    h = 8 * g + 2 * pr + hh
                    lm = _lmat(c, h)
                    dyh = (dy2 * msk).astype(bf16)
                    mh = (gmat * lm).astype(bf16)
                    outp = outp + _dot(mh, dyh, TN)
                    dml = _dot(dyh, xp2, NT) * lm
                    dg_acc = dg_acc + dml
                    q = dml * gmat
                    dacs = dacs + jnp.where(lane16 == h, jnp.sum(q, axis=1, keepdims=True), 0.0)
                    dacs_t = dacs_t + jnp.where(row16 == h, jnp.sum(q, axis=0, keepdims=True), 0.0)
                dxd.append(outp)
            dxdt_parts.append(jnp.concatenate(dxd, axis=1) + e_end[:, sl] * bds)
            dgb = dg_acc.astype(bf16)
            dcs.append(_dot(dgb, bg) + _dot(dy_dec[:, sl].astype(bf16), st.astype(bf16), NT))
            dbs.append(_dot(dgb, cg, TN) + _dot(w_end[:, sl].astype(bf16), dstn_b, NT))
            dst_ref[g] = dstn * jnp.exp(a_last_e[:, sl]) + _dot(cg, dy_dec[:, sl].astype(bf16), TN)
        dxdt = jnp.concatenate(dxdt_parts, axis=1)
        zfull = jnp.concatenate(zparts, axis=1)
        y_off_full = jnp.concatenate(yoff_parts, axis=1)
        dlast = jnp.concatenate(dlast_parts, axis=1)
        red = lambda v: _split_dot(v, e, NT, 2)
        eye16 = jnp.where(lax.broadcasted_iota(jnp.int32, (HEADS, HEADS), 0) == lax.broadcasted_iota(jnp.int32, (HEADS, HEADS), 1),
                          1.0, 0.0).astype(bf16)
        dacs = dacs - _split_dot(dacs_t, eye16, TN, 3)
        zred = red(zfull)
        dacs = dacs + red(dy * y_off_full) - zred
        last_term = jnp.sum(zred, axis=0, keepdims=True) + red(dlast)
        rowc = lax.broadcasted_iota(jnp.int32, (CH, 1), 0)
        dacs = dacs + jnp.where(rowc == CH - 1, last_term, 0.0)
        r_i = lax.broadcasted_iota(jnp.int32, (CH, CH), 0)
        c_i = lax.broadcasted_iota(jnp.int32, (CH, CH), 1)
        ddta = _split_dot(dacs, jnp.where(c_i >= r_i, 1.0, 0.0).astype(bf16), None, 3, v_is_lhs=False)
        ddt = ddta * c["a"] + red(dxdt * x)
        dal_ref[...] += _rsum8(ddta * c["dt"] * c["a"])
        ddt_raw = ddt * _sigmoid(c["pre"]) * c["live"]
        ddtb_ref[...] += _rsum8(ddt_raw)
        dx = dy * d_e + dxdt * c["dt_e"]
        dxbc = jnp.concatenate([dx, dbs[0], dbs[1], dcs[0], dcs[1]], axis=1)
        sg = c["sg"]
        dxc = dxbc * (sg * (1.0 + c["xc"] * (1.0 - sg)))
        dcb_ref[...] += _rsum8(dxc)
        xp = c["xp"]
        row8 = lax.broadcasted_iota(jnp.int32, (8, 1), 0)
        cw = cw_ref[...]
        dxp = cw[3:4] * dxc
        dcw = jnp.where(row8 == 3, jnp.sum(dxc * xp, axis=0, keepdims=True), 0.0)
        nxt8 = nxt8_ref[...]
        for j in (1, 2, 3):
            rolled = pltpu.roll(dxc, CH - j, 0)
            fix = pltpu.roll(nxt8, 8 - j, 0)
            bot = jnp.where(row8 >= 8 - j, fix, rolled[CH - 8:CH])
            later = jnp.concatenate([rolled[:CH - 8], bot], axis=0)
            dxp = dxp + cw[3 - j:4 - j] * later
            dcw = dcw + jnp.where(row8 == 3 - j, jnp.sum(later * xp, axis=0, keepdims=True), 0.0)
        dcw_ref[...] += dcw
        nxt8_ref[...] = dxc[0:8]
        dp_ref[:, O_Z:O_Z + D] = dz.astype(bf16)
        dp_ref[:, O_U:O_U + D] = du_ref[...].astype(bf16)
        dp_ref[:, O_XBC:O_XBC + D_XBC] = dxp.astype(bf16)
        dp_ref[:, O_DT:W_PROJ] = jnp.zeros((CH, W_PROJ - O_DT), bf16)
        dp_ref[:, O_DT:O_DT + HEADS] = ddt_raw.astype(bf16)

    full = lambda a: pl.BlockSpec(a.shape, lambda s, nd=a.ndim: (0,) * nd)
    rev = lambda s: (nch - 1 - s, 0)
    acc = lambda cdim: pl.BlockSpec((8, cdim), lambda s: (0, 0))
    return _call(
        body, comm, name="ssd_bwd", nsteps=nch,
        in_specs=[pl.BlockSpec((CH, W_PROJ), rev), pl.BlockSpec((CH, D_XBC), rev),
                  pl.BlockSpec((CH, D), rev), pl.BlockSpec((CH, D), rev), pl.BlockSpec((CH, D), rev),
                  pl.BlockSpec((1, 2, NSTATE, 512), lambda s: (nch - 1 - s, 0, 0, 0))]
        + [full(a) for a in (conv_w, conv_b, dt_bias, a_log, d_ssd, g_ssd)],
        out_specs=[pl.BlockSpec((CH, W_PROJ), rev), acc(D_XBC), acc(D_XBC), acc(HEADS), acc(HEADS), acc(D), acc(D)],
        out_shape=[jax.ShapeDtypeStruct((nch * CH, W_PROJ), bf16)]
        + [jax.ShapeDtypeStruct((8, cdim), f32) for cdim in (D_XBC, D_XBC, HEADS, HEADS, D, D)],
        scratch_shapes=[pltpu.VMEM((8, D_XBC), f32), pltpu.VMEM((2, NSTATE, 512), f32)],
        args=[proj, xc_all, y, dys, du, states, conv_w, conv_b, dt_bias, a_log, d_ssd, g_ssd])


SCAN_UNROLL = 8


def _to_slabs(slab_ref, q, mat):
    for ls in range(8):
        slab_ref[ls, pl.ds(PITCH * q, CH), :] = mat[:, 128 * ls:128 * (ls + 1)]


def _from_slabs(slab, q):
    return jnp.concatenate([slab(ls, PITCH * q) for ls in range(8)], axis=1)


def _tile(slab_ref, ls, t, lead=None):
    idx = (ls, pl.ds(t, 8, stride=PITCH), slice(None))
    return slab_ref[idx] if lead is None else slab_ref[(lead,) + idx]


def _s5_fwd(proj, bbq, ccq, ar, ai, d_skip, nch, comm=None):
    def body(u_ref, bb_ref, cc_ref, ar_ref, ai_ref, d_ref, s_ref, yl_ref, y5_ref, bu_ref, st_ref):
        @pl.when(pl.program_id(0) == 0)
        def _():
            st_ref[...] = jnp.zeros_like(st_ref)
        u = u_ref[...]
        ub = u.astype(bf16)
        for q in range(NQ):
            _to_slabs(bu_ref, q, _dot(ub[:, 128 * q:128 * (q + 1)], bb_ref[q]))
        ar_t = [ar_ref[:, 128 * l:128 * (l + 1)] for l in range(4)]
        ai_t = [ai_ref[:, 128 * l:128 * (l + 1)] for l in range(4)]

        def one(t, carry):
            re, im = carry
            nre, nim = [], []
            for l in range(4):
                a = ar_t[l] * re[l] - ai_t[l] * im[l] + _tile(bu_ref, l, t)
                b = ar_t[l] * im[l] + ai_t[l] * re[l] + _tile(bu_ref, l + 4, t)
                s_ref[0, l, pl.ds(t, 8, stride=PITCH), :] = a
                s_ref[0, l + 4, pl.ds(t, 8, stride=PITCH), :] = b
                nre.append(a)
                nim.append(b)
            return tuple(nre), tuple(nim)

        def step(tt, carry):
            for k in range(SCAN_UNROLL):
                carry = one(tt * SCAN_UNROLL + k, carry)
            return carry
        init = (tuple(st_ref[l] for l in range(4)), tuple(st_ref[l + 4] for l in range(4)))
        re, im = lax.fori_loop(0, CH // SCAN_UNROLL, step, init)
        for l in range(4):
            st_ref[l] = re[l]
            st_ref[l + 4] = im[l]
        ys = []
        for q in range(NQ):
            sq = _from_slabs(lambda ls, r0: s_ref[0, ls, pl.ds(r0, CH), :], q).astype(bf16)
            ys.append(_dot(sq, cc_ref[q], NT))
        yl = jnp.concatenate(ys, axis=1) + u * d_ref[...]
        yl_ref[...] = yl
        y5_ref[...] = (0.5 * yl * (1.0 + lax.erf(yl * (1.0 / math.sqrt(2.0))))).astype(bf16)

    const = lambda a: pl.BlockSpec(a.shape, lambda i, nd=a.ndim: (0,) * nd)
    return _call(
        body, comm, name="s5_fwd", nsteps=nch,
        in_specs=[pl.BlockSpec((CH, D), lambda i: (i, O_U // D)), const(bbq), const(ccq), const(ar), const(ai), const(d_skip)],
        out_specs=[pl.BlockSpec((1, 8, 8 * PITCH, 128), lambda i: (i, 0, 0, 0)),
                   pl.BlockSpec((CH, D), lambda i: (i, 0)), pl.BlockSpec((CH, D), lambda i: (i, 0))],
        out_shape=[jax.ShapeDtypeStruct((nch, 8, 8 * PITCH, 128), f32), jax.ShapeDtypeStruct((nch * CH, D), f32),
                   jax.ShapeDtypeStruct((nch * CH, D), bf16)],
        scratch_shapes=[pltpu.VMEM((8, 8 * PITCH, 128), f32), pltpu.VMEM((8, 8, 128), f32)],
        args=[proj, bbq, ccq, ar, ai, d_skip])


def _s5_bwd(proj, dyl, s_all, bbq, ccq, ar, ai, d_skip, nch, comm=None):
    def body(u_ref, dy_ref, s_ref, bbt_ref, cct_ref, ar_ref, ai_ref, d_ref,
             du_ref, dcc_ref, dbb_ref, dab_ref, dd_ref, ga_ref, st_ref):
        @pl.when(pl.program_id(0) == 0)
        def _():
            st_ref[...] = jnp.zeros_like(st_ref)
            for r in (dcc_ref, dbb_ref, dab_ref, dd_ref):
                r[...] = jnp.zeros_like(r)
        u = u_ref[...]
        dyl_v = dy_ref[...]
        dd_ref[...] += _rsum8(dyl_v * u)
        ub = u.astype(bf16)
        dyb = dyl_v.astype(bf16)
        for q in range(NQ):
            _to_slabs(ga_ref, q, _dot(dyb[:, 128 * q:128 * (q + 1)], cct_ref[q]))
        ar_t = [ar_ref[:, 128 * l:128 * (l + 1)] for l in range(4)]
        ai_t = [ai_ref[:, 128 * l:128 * (l + 1)] for l in range(4)]

        def one(t, carry):
            re, im, dar, dai = carry
            nre, nim, ndar, ndai = [], [], [], []
            for l in range(4):
                sre = _tile(s_ref, l, t, lead=0)
                sim = _tile(s_ref, l + 4, t, lead=0)
                ndar.append(dar[l] + re[l] * sre + im[l] * sim)
                ndai.append(dai[l] + im[l] * sre - re[l] * sim)
                a = _tile(ga_ref, l, t) + ar_t[l] * re[l] + ai_t[l] * im[l]
                b = _tile(ga_ref, l + 4, t) - ai_t[l] * re[l] + ar_t[l] * im[l]
                ga_ref[l, pl.ds(t, 8, stride=PITCH), :] = a
                ga_ref[l + 4, pl.ds(t, 8, stride=PITCH), :] = b
                nre.append(a)
                nim.append(b)
            return tuple(nre), tuple(nim), tuple(ndar), tuple(ndai)

        def step(tt, carry):
            for k in range(SCAN_UNROLL):
                carry = one(CH - 1 - (tt * SCAN_UNROLL + k), carry)
            return carry
        four = lambda ref, o: tuple(ref[l + o] for l in range(4))
        re, im, dar, dai = lax.fori_loop(0, CH // SCAN_UNROLL, step,
                                         (four(st_ref, 0), four(st_ref, 4), four(dab_ref, 0), four(dab_ref, 4)))
        for l in range(4):
            st_ref[l], st_ref[l + 4] = re[l], im[l]
            dab_ref[l], dab_ref[l + 4] = dar[l], dai[l]
        dus = []
        for q in range(NQ):
            aq = _from_slabs(lambda ls, r0: ga_ref[ls, pl.ds(r0, CH), :], q).astype(bf16)
            sq = _from_slabs(lambda ls, r0: s_ref[0, ls, pl.ds(r0, CH), :], q).astype(bf16)
            dcc_ref[q] += _dot(dyb[:, 128 * q:128 * (q + 1)], sq, TN)
            dbb_ref[q] += _dot(ub[:, 128 * q:128 * (q + 1)], aq, TN)
            dus.append(_dot(aq, bbt_ref[q], NT))
        du_ref[...] = jnp.concatenate(dus, axis=1) + dyl_v * d_ref[...]

    const = lambda a: pl.BlockSpec(a.shape, lambda s, nd=a.ndim: (0,) * nd)
    rev = lambda s: (nch - 1 - s, 0)
    return _call(
        body, comm, name="s5_bwd", nsteps=nch,
        in_specs=[pl.BlockSpec((CH, D), lambda s: (nch - 1 - s, O_U // D)), pl.BlockSpec((CH, D), rev),
                  pl.BlockSpec((1, 8, 8 * PITCH, 128), lambda s: (nch - 1 - s, 0, 0, 0)),
                  const(bbq), const(ccq), const(ar), const(ai), const(d_skip)],
        out_specs=[pl.BlockSpec((CH, D), rev), pl.BlockSpec((NQ, 128, D), lambda s: (0, 0, 0)),
                   pl.BlockSpec((NQ, 128, D), lambda s: (0, 0, 0)), pl.BlockSpec((8, 8, 128), lambda s: (0, 0, 0)),
                   pl.BlockSpec((8, D), lambda s: (0, 0))],
        out_shape=[jax.ShapeDtypeStruct((nch * CH, D), f32), jax.ShapeDtypeStruct((NQ, 128, D), f32),
                   jax.ShapeDtypeStruct((NQ, 128, D), f32), jax.ShapeDtypeStruct((8, 8, 128), f32),
                   jax.ShapeDtypeStruct((8, D), f32)],
        scratch_shapes=[pltpu.VMEM((8, 8 * PITCH, 128), f32), pltpu.VMEM((8, 8, 128), f32)],
        args=[proj, dyl, s_all, bbq, ccq, ar, ai, d_skip])


def _s5_tables(lam_re, lam_im, log_step, b_re, b_im):
    step = jnp.exp(log_step)[:, None]
    mag = jnp.exp(lam_re * step)
    ab_re = mag * jnp.cos(lam_im * step)
    ab_im = mag * jnp.sin(lam_im * step)
    den = lam_re * lam_re + lam_im * lam_im
    coef_re = ((ab_re - 1.0) * lam_re + ab_im * lam_im) / den
    coef_im = (ab_im * lam_re - (ab_re - 1.0) * lam_im) / den
    bb_re = coef_re[..., None] * b_re - coef_im[..., None] * b_im
    bb_im = coef_re[..., None] * b_im + coef_im[..., None] * b_re
    return ab_re, ab_im, bb_re, bb_im


def _blockdiag_in(m_re, m_im):
    eye = jnp.eye(8, dtype=f32)

    def one(m):
        m = m.reshape(NQ, 8, S5_P, 16)
        return jnp.einsum("qgph,gk->qghkp", m, eye).reshape(NQ, 128, 512)
    return jnp.concatenate([one(m_re), one(m_im)], axis=2)


def _blockdiag_in_grad(dm):
    def one(x):
        x = x.reshape(NQ, 8, 16, 8, S5_P)
        return jnp.einsum("qghgp->qgph", x).reshape(NQ * 8, S5_P, 16)
    return one(dm[:, :, :512]), one(dm[:, :, 512:])


def _local_step(x2, tgt2, meta, p, w_in_t, shards):
    seq = x2.shape[0]
    nch = 1 + seq // CH
    bmb = BIG_ROWS if (nch * CH) % BIG_ROWS == 0 else CH
    nbig = nch * CH // bmb
    metablk = jnp.concatenate([jnp.zeros((CH - N_META, D), f32), meta, jnp.zeros((bmb - CH, D), f32)], axis=0)
    w_full = (w_in_t, pl.BlockSpec(w_in_t.shape, lambda i: (0, 0), pipeline_mode=pl.Buffered(1)))

    def lead(i, v):
        return jnp.logical_and(i == 0, lax.broadcasted_iota(jnp.int32, (v.shape[0], 1), 0) < CH)
    h0_of = lambda i, s, q: jnp.where(lead(i, s[0]), q[0][:s[0].shape[0]], s[0])

    def in_fn(i, r, s, q, w):
        nb = _rms(h0_of(i, s, q), q[1]).astype(bf16)
        return [_dot(nb, w[0][...], NT), nb], []
    (proj, n0), _, (g_up,) = _rowwise("in_proj", in_fn, nbig, shifted=[x2], pars=[metablk, p["g_mix"]], refs=[w_full],
                                      out_rows=[(W_PROJ, f32), (D, bf16)], bm=bmb, comm=_gather_piece(shards[0]))
    (y, y_ssd, states, xc_all), (g_down,) = _ssd_fwd(proj, p["conv_w"], p["conv_b"], p["dt_bias"], p["a_log"], p["d_ssd"],
                                                     p["g_ssd"], nch, comm=_gather_piece(shards[1]))

    ab_re, ab_im, bb_re, bb_im = _s5_tables(p["lam_re"], p["lam_im"], p["log_step"], p["b_re"], p["b_im"])
    ar, ai = ab_re.reshape(NQ, 512), ab_im.reshape(NQ, 512)
    bbq = _blockdiag_in(bb_re, bb_im)
    ccq = _blockdiag_in(jnp.swapaxes(p["c_re"], 1, 2), -jnp.swapaxes(p["c_im"], 1, 2))
    d_skip = p["d_s5"].reshape(1, D)
    bbq_b, ccq_b = bbq.astype(bf16), ccq.astype(bf16)
    (s_all, ylin, y5), (g_go,) = _s5_fwd(proj, bbq_b, ccq_b, ar, ai, d_skip, nch, comm=_gather_piece(shards[2]))
    whole = lambda a: (a, pl.BlockSpec(a.shape, lambda i: (0, 0, 0), pipeline_mode=pl.Buffered(1)))
    w_up, w_down = whole(g_up), whole(g_down)
    w_glu_t = (g_go, pl.BlockSpec((4, 512, D), lambda i: (0, 0, 0), pipeline_mode=pl.Buffered(1)))
    w_out = (g_go, pl.BlockSpec((4, 512, D), lambda i: (0, 1, 0), pipeline_mode=pl.Buffered(1)))

    def glu_fn(i, r, s, q, w):
        v = jnp.concatenate([_dot(r[0], w[0][k], NT) for k in range(4)], axis=1) + q[0]
        return [v, _rms(v[:, :D] * _sigmoid(v[:, D:]), q[1])], []
    (v, y_s5), _ = _rowwise("glu", glu_fn, nbig, rows=[y5], pars=[p["b_glu"], p["g_s5"]], refs=[w_glu_t],
                            out_rows=[(2 * D, bf16), (D, bf16)], bm=bmb)

    def out_fn(i, r, s, q, w):
        acc = (_dot(r[0][:, :512], w[0][0]) + _dot(r[0][:, 512:], w[0][1]) + _dot(r[1][:, :512], w[0][2])
               + _dot(r[1][:, 512:], w[0][3]))
        return [h0_of(i, s, q) + acc], []
    (h1,), _ = _rowwise("out_proj", out_fn, nbig, rows=[y_ssd, y_s5], shifted=[x2], pars=[metablk], refs=[w_out],
                        out_rows=[(D, f32)], bm=bmb)

    def up_fn(i, r, s, q, w):
        nb = _rms(r[0], q[0]).astype(bf16)
        return [jnp.concatenate([jnp.maximum(_dot(nb, w[0][k]), 0.0).astype(bf16) for k in range(4)], axis=1), nb], []
    (relu_m, n1), _ = _rowwise("up_proj", up_fn, nbig, rows=[h1], pars=[p["g_mlp"]], refs=[w_up],
                               out_rows=[(4 * D, bf16), (D, bf16)], bm=bmb)

    def down_fn(i, r, s, q, w):
        acc = None
        for k in range(4):
            t = r[0][:, D * k:D * (k + 1)]
            part = _dot(t * t, w[0][k])
            acc = part if acc is None else acc + part
        return [r[1] + acc], []
    (h2,), _ = _rowwise("down_proj", down_fn, nbig, rows=[relu_m, h1], refs=[w_down], out_rows=[(D, f32)], bm=bmb)

    def final_fn(i, r, s, q, w):
        err = jnp.where(lead(i, r[0]), 0.0, _rms(r[0], q[0]) - s[0])
        dh, dg8 = _rms_bwd(err * (1.0 / D), r[0], q[0])
        return [dh, dh], [_rsum8(err * err), dg8]
    (dh2, dh2_b), (loss8, dgf8) = _rowwise("final", final_fn, nbig, rows=[h2], shifted=[tgt2], pars=[p["g_final"]],
                                           out_rows=[(D, f32), (D, bf16)], out_accs=[D, D], bm=bmb)
    loss = 0.5 / D * jnp.sum(loss8)

    def down_bwd_fn(i, r, s, q, w):
        dm_ = [_dot(r[0], w[0][k], NT) * (2.0 * r[1][:, D * k:D * (k + 1)].astype(f32)) for k in range(4)]
        return [jnp.concatenate(dm_, axis=1)], []
    (dm,), _ = _rowwise("down_bwd", down_bwd_fn, nbig, rows=[dh2_b, relu_m], refs=[w_down], out_rows=[(4 * D, bf16)], bm=bmb)
    g_a = _dw_into("dw_down", relu_m, dh2_b, None, 1024, True, 1, 4, 0, piece_rows=2048, a_square=True)

    def up_bwd_fn(i, r, s, q, w):
        acc = _dot(r[0][:, :D], w[0][0], NT)
        for k in range(1, 4):
            acc = acc + _dot(r[0][:, D * k:D * (k + 1)], w[0][k], NT)
        dh, dg8 = _rms_bwd(acc, r[1], q[0])
        dh1_ = r[2] + dh
        return [dh1_, dh1_], [dg8]
    (dh1, dh1_b), (dgmlp8,) = _rowwise("up_bwd", up_bwd_fn, nbig, rows=[dm, h1, dh2], pars=[p["g_mlp"]], refs=[w_up],
                                       out_rows=[(D, f32), (D, bf16)], out_accs=[D], bm=bmb)
    g_a = _dw_into("dw_up", n1, dm, g_a, 1024, False, 0, 4, 0, piece_rows=2048)

    def out_bwd_fn(i, r, s, q, w):
        dmix = [_dot(r[0], w[0][k], NT) for k in range(4)]
        v1, v2 = r[1][:, :D].astype(f32), r[1][:, D:].astype(f32)
        s2 = _sigmoid(v2)
        dglu, dg8 = _rms_bwd(jnp.concatenate(dmix[2:], axis=1), v1 * s2, q[0])
        dv = jnp.concatenate([dglu * s2, dglu * v1 * s2 * (1.0 - s2)], axis=1)
        return [jnp.concatenate(dmix[:2], axis=1), dv], [dg8, _rsum8(dv)]
    (dys, dv), (dgs58, dbglu8) = _rowwise("out_bwd", out_bwd_fn, nbig, rows=[dh1_b, v], pars=[p["g_s5"]], refs=[w_out],
                                          out_rows=[(D, f32), (2 * D, bf16)], out_accs=[D, 2 * D], bm=bmb)
    g_b = _dw_into("dw_out_a", y_ssd, dh1_b, None, 512, True, 1, 2, 0, piece_rows=1024)
    g_b = _dw_into("dw_out_b", y_s5, dh1_b, g_b, 512, True, 1, 2, 2, piece_rows=1024)

    def glu_bwd_fn(i, r, s, q, w):
        acc = _dot(r[0][:, :512], w[0][0])
        for k in range(1, 4):
            acc = acc + _dot(r[0][:, 512 * k:512 * (k + 1)], w[0][k])
        yl = r[1]
        cdf = 0.5 * (1.0 + lax.erf(yl * (1.0 / math.sqrt(2.0))))
        pdf = jnp.exp(-0.5 * yl * yl) * (1.0 / math.sqrt(2.0 * math.pi))
        return [acc * (cdf + yl * pdf)], []
    (dylin,), _ = _rowwise("glu_bwd", glu_bwd_fn, nbig, rows=[dv, ylin], refs=[w_glu_t], out_rows=[(D, f32)], bm=bmb)
    g_b = _dw_into("dw_glu", dv, y5, g_b, 512, True, 0, 4, 0, piece_rows=1024)

    (du, dcc, dbb, dab, dds5), (land_a,) = _s5_bwd(proj, dylin, s_all, bbq_b, ccq_b, ar, ai, d_skip, nch,
                                                   comm=_scatter_piece(g_a))

    s8 = lambda a: jnp.sum(a, axis=0, keepdims=True)
    dab_q = jnp.swapaxes(dab.reshape(2, 4, NQ, 128), 1, 2).reshape(2, NQ * 8, S5_P)
    dbb_re, dbb_im = _blockdiag_in_grad(dbb)
    dcr, dci = _blockdiag_in_grad(dcc)
    _, vjp = jax.vjp(_s5_tables, p["lam_re"], p["lam_im"], p["log_step"], p["b_re"], p["b_im"])
    dlam_re, dlam_im, dlog_step, db_re, db_im = vjp((dab_q[0], dab_q[1], dbb_re, dbb_im))
    early = dict(lam_re=dlam_re, lam_im=dlam_im, log_step=dlog_step, b_re=db_re, b_im=db_im, c_re=jnp.swapaxes(dcr, 1, 2),
                 c_im=-jnp.swapaxes(dci, 1, 2), d_s5=s8(dds5).reshape(NQ * 8, 16), b_glu=s8(dbglu8), g_s5=s8(dgs58),
                 g_mlp=s8(dgmlp8), g_final=s8(dgf8).reshape(D))
    early_pack = _pack_small([early[n] for n in EARLY], _rows_for(EARLY))

    (dproj, dcw8, dcb8, ddtb8, dal8, dd8, dgssd8), (land_b, all_early) = _ssd_bwd(
        proj, xc_all, y, dys, du, states, p["conv_w"], p["conv_b"], p["dt_bias"], p["a_log"], p["d_ssd"], p["g_ssd"], nch,
        comm=_both(_scatter_piece(g_b), _gather_blocks(early_pack)))

    gt = _dw_in_t(dproj, n0)
    gt = jnp.concatenate([gt[0:1024], gt[O_XBC:O_XBC + D_XBC], gt[O_DT:O_DT + HEADS], gt[O_U:O_U + D]], axis=0).reshape(4, 900, D)
    g_c = jnp.concatenate([gt, jnp.zeros((4, C_ROWS - 900, D), bf16)], axis=1)

    def in_bwd_fn(i, r, s, q, w):
        dh, dg8 = _rms_bwd(_dot(r[0], w[0][...]), h0_of(i, s, q), q[1])
        dh0 = r[1] + dh
        dmeta = jnp.where(i == 0, dh0[CH - N_META:], 0.0)
        return [dh0], [dg8, dmeta[:8], dmeta[8:]]
    (grad_x,), (dgmix8, dmeta_a, dmeta_b), (land_c,) = _rowwise(
        "in_bwd", in_bwd_fn, nch, rows=[dproj, dh1], shifted=[x2], pars=[metablk, p["g_mix"]], refs=[w_full],
        out_rows=[(D, f32, "shifted")], out_accs=[D, D, D], bm=CH, comm=_scatter_piece(g_c))

    hsum = lambda a: jnp.sum(s8(a).reshape(HEADS, HEAD_DIM), axis=1).reshape(1, HEADS)
    late = dict(g_mix=s8(dgmix8), conv_b=s8(dcb8), dt_bias=s8(ddtb8), a_log=s8(dal8), d_ssd=hsum(dd8), g_ssd=s8(dgssd8),
                conv_w=dcw8[0:4], meta_tokens=jnp.concatenate([dmeta_a, dmeta_b], axis=0), loss=loss.reshape(1))
    return grad_x, [(g_a, land_a), (g_b, land_b), (g_c, land_c)], all_early, late


def _perm_rows_w_in(wt):
    return jnp.concatenate([wt[0:1024], wt[2576:3600], wt[1024:2560], wt[2560:2576],
                            jnp.zeros((W_PROJ - 3600, wt.shape[1]), wt.dtype)], axis=0)


def _allgather8(x_shard, name):
    m_per, n = x_shard.shape

    def body(x_ref, out_ref, send_sems, recv_sems, stage, local_sems):
        x, y, c = _place()
        me, sibling = (x, y, c), (x, y, 1 - c)
        chips = [(1 - x, y), (x, 1 - y), (1 - x, 1 - y)]

        def rows(px, py, pc):
            return out_ref.at[pl.ds((4 * px + 2 * py + pc) * m_per, m_per), :]

        def copy(k, block, to, src=None):
            return pltpu.make_async_remote_copy(
                src_ref=rows(*block) if src is None else src, dst_ref=rows(*block),
                send_sem=send_sems.at[k], recv_sem=recv_sems.at[k], device_id=to, device_id_type=MESH)

        load = pltpu.make_async_copy(x_ref, stage, local_sems.at[0])
        load.start()
        first = [copy(0, me, sibling, src=x_ref)]
        first += [copy(1 + j, me, (*chip, c), src=x_ref) for j, chip in enumerate(chips)]
        for cp in first:
            cp.start()
        load.wait()
        store = pltpu.make_async_copy(stage, rows(*me), local_sems.at[1])
        store.start()
        passed = [copy(4 + j, (*chip, c), sibling) for j, chip in enumerate(chips)]
        for j, chip in enumerate(chips):
            copy(1 + j, (*chip, c), me).wait_recv()
            passed[j].start()
        copy(0, sibling, me).wait_recv()
        for j, chip in enumerate(chips):
            copy(4 + j, (*chip, 1 - c), me).wait_recv()
        for cp in first + passed:
            cp.wait_send()
        store.wait()

    return pl.pallas_call(
        body, name=name, out_shape=jax.ShapeDtypeStruct((8 * m_per, n), x_shard.dtype),
        in_specs=[_ANY], out_specs=_ANY,
        scratch_shapes=[pltpu.SemaphoreType.DMA((7,)), pltpu.SemaphoreType.DMA((7,)), pltpu.VMEM((m_per, n), x_shard.dtype),
                        pltpu.SemaphoreType.DMA((2,))])(x_shard)


def _swap_sibling(parts, name):
    n = len(parts)

    def body(*refs):
        send_sems, recv_sems = refs[2 * n:]
        x, y, c = _place()
        cps = [pltpu.make_async_remote_copy(src_ref=refs[k], dst_ref=refs[n + k], send_sem=send_sems.at[k],
                                            recv_sem=recv_sems.at[k], device_id=(x, y, 1 - c), device_id_type=MESH)
               for k in range(n)]
        for cp in cps:
            cp.start()
        for cp in cps:
            cp.wait()

    return pl.pallas_call(
        body, name=name, out_shape=[jax.ShapeDtypeStruct(r.shape, r.dtype) for r in parts], in_specs=[_ANY] * n,
        out_specs=[_ANY] * n, scratch_shapes=[pltpu.SemaphoreType.DMA((n,)), pltpu.SemaphoreType.DMA((n,))])(*parts)


SH_CONVW, SH_META = 4 * 384, 16 * 256
SPARE_ROWS = 17

SMALL_SHAPES = dict(
    g_mix=(1, 1024), conv_b=(1, 1536), dt_bias=(1, 16), a_log=(1, 16), d_ssd=(1, 16), g_ssd=(1, 1024), lam_re=(1, 64, 64),
    lam_im=(1, 64, 64), log_step=(1, 64), b_re=(1, 64, 64, 16), b_im=(1, 64, 64, 16), c_re=(1, 64, 16, 64), c_im=(1, 64, 16, 64),
    d_s5=(1, 64, 16), b_glu=(1, 2048), g_s5=(1, 1024), g_mlp=(1, 1024), g_final=(1024,),
    conv_w=(4, D_XBC), meta_tokens=(N_META, D), loss=(1,))
EARLY = ["lam_re", "lam_im", "log_step", "b_re", "b_im", "c_re", "c_im", "d_s5", "b_glu", "g_s5", "g_mlp", "g_final"]
LATE = ["g_mix", "conv_b", "dt_bias", "a_log", "d_ssd", "g_ssd", "conv_w", "meta_tokens", "loss"]


def _rows_for(names):
    return -(-sum(math.prod(SMALL_SHAPES[n]) for n in names) // (8 * D)) * 8


def _pack_small(arrs, rows):
    flat = jnp.concatenate([a.reshape(-1).astype(f32) for a in arrs])
    return jnp.concatenate([flat, jnp.zeros((rows * D - flat.shape[0],), f32)]).reshape(rows, D)


def _unpack_small(slab, shapes):
    flat = slab.reshape(-1)
    out, o = [], 0
    for shp in shapes:
        n = math.prod(shp)
        out.append(flat[o:o + n].reshape(shp))
        o += n
    return out


def _sum8(g, rows, name):
    def body(g_ref, o_ref):
        acc = g_ref[0]
        for k in range(1, 8):
            acc = acc + g_ref[k]
        o_ref[...] = acc
    return pl.pallas_call(body, name=name, out_shape=jax.ShapeDtypeStruct((rows, D), f32),
                          compiler_params=_cp())(g.reshape(8, rows, D))


def _adam_math(w_, g_, m_, v_):
    m2 = ADAM_B1 * m_ + (1.0 - ADAM_B1) * g_
    v2 = ADAM_B2 * v_ + (1.0 - ADAM_B2) * jnp.square(g_)
    m_hat = m2 / (1.0 - ADAM_B1 ** ADAM_STEP)
    v_hat = v2 / (1.0 - ADAM_B2 ** ADAM_STEP)
    delta = -ADAM_LR * (m_hat / (jnp.sqrt(v_hat) + ADAM_EPS) + ADAM_WD * w_)
    return delta, m2, v2


def _adamw(name, w, g, m, v, bm):
    def fn(i, r, s, q, refs):
        return list(_adam_math(*r)), []
    c = w.shape[1]
    (d, m2, v2), _ = _rowwise(name, fn, w.shape[0] // bm, rows=[w, g, m, v], out_rows=[(c, f32)] * 3, bm=bm)
    return d, m2, v2


def _adamw_whole(name, w, g, m, v):
    def body(w_ref, g_ref, m_ref, v_ref, d_ref, m2_ref, v2_ref):
        d_ref[...], m2_ref[...], v2_ref[...] = _adam_math(w_ref[...], g_ref[...], m_ref[...], v_ref[...])
    return pl.pallas_call(body, name=name, out_shape=[jax.ShapeDtypeStruct(w.shape, f32)] * 3, compiler_params=_cp())(w, g, m, v)


def _sum_parts(name, own, land):
    def fn(i, r, s, q, refs):
        acc = r[0].astype(f32)
        for k in range(7):
            acc = acc + refs[0][k].astype(f32)
        return [acc], []
    rows = own.shape[0]
    bm = CH if rows % CH == 0 else rows
    (o,), _ = _rowwise(name, fn, rows // bm, rows=[own], refs=[(land, pl.BlockSpec((7, bm, D), lambda i: (0, i, 0)))],
                       out_rows=[(D, f32)], bm=bm)
    return o


def kernel(x, meta_tokens, g_mix, w_in, conv_w, conv_b, dt_bias, a_log, d_ssd, g_ssd, lam_re, lam_im, log_step, b_re, b_im, c_re, c_im, d_s5, w_glu, b_glu, g_s5, w_out, g_mlp, w_up, w_down, g_final, loss_target, m_meta_tokens, m_g_mix, m_w_in, m_conv_w, m_conv_b, m_dt_bias, m_a_log, m_d_ssd, m_g_ssd, m_lam_re, m_lam_im, m_log_step, m_b_re, m_b_im, m_c_re, m_c_im, m_d_s5, m_w_glu, m_b_glu, m_g_s5, m_w_out, m_g_mlp, m_w_up, m_w_down, m_g_final, v_meta_tokens, v_g_mix, v_w_in, v_conv_w, v_conv_b, v_dt_bias, v_a_log, v_d_ssd, v_g_ssd, v_lam_re, v_lam_im, v_log_step, v_b_re, v_b_im, v_c_re, v_c_im, v_d_s5, v_w_glu, v_b_glu, v_g_s5, v_w_out, v_g_mlp, v_w_up, v_w_down, v_g_final):
    given = dict(locals())
    cx, cy, cc = _place()
    chip = 2 * cx + cy

    small_f = jnp.concatenate([conv_w.reshape(-1), meta_tokens.reshape(-1)])
    t_hi = small_f.astype(bf16)
    r_1 = small_f - t_hi.astype(f32)
    t_mid = r_1.astype(bf16)
    t_lo = (r_1 - t_mid.astype(f32)).astype(bf16)
    terms = jnp.concatenate([t_hi, t_mid, t_lo])
    spare = jnp.concatenate([terms, jnp.zeros((SPARE_ROWS * D - terms.shape[0],), bf16)]).reshape(SPARE_ROWS, D)
    shards = (w_up[0].astype(bf16), w_down[0].astype(bf16),
              jnp.concatenate([w_glu[0].T.astype(bf16), w_out[0].astype(bf16)], axis=0))
    in_rows = jnp.concatenate([w_in[0].T.astype(bf16), spare, jnp.zeros((1024 - 900 - SPARE_ROWS, D), bf16)], axis=0)
    my_half = lax.dynamic_slice_in_dim(in_rows, cc * 512, 512, axis=0)
    gathered = _allgather8(my_half, "gather_w_in").reshape(4, 1024, D)
    w_in_t = _perm_rows_w_in(jnp.concatenate([gathered[s, 0:900] for s in range(4)], axis=0))
    n_sf = SH_CONVW + SH_META
    tr = gathered[:, 900:900 + SPARE_ROWS].reshape(4, SPARE_ROWS * D)[:, :3 * n_sf].astype(f32).reshape(4, 3, n_sf)
    sp = tr[:, 0] + tr[:, 1] + tr[:, 2]
    conv_w_full = jnp.concatenate([sp[s, :SH_CONVW].reshape(4, 384) for s in range(4)], axis=1)
    meta_full = jnp.concatenate([sp[s, SH_CONVW:].reshape(16, 256) for s in range(4)], axis=1)

    p = dict(g_mix=g_mix, conv_w=conv_w_full, conv_b=conv_b, dt_bias=dt_bias, a_log=a_log, d_ssd=d_ssd, g_ssd=g_ssd,
             lam_re=lam_re[0], lam_im=lam_im[0], log_step=log_step[0], b_re=b_re[0], b_im=b_im[0], c_re=c_re[0], c_im=c_im[0],
             d_s5=d_s5[0], b_glu=b_glu, g_s5=g_s5, g_mlp=g_mlp, g_final=g_final.reshape(1, D))
    grad_x, pieces, all_early, late = _local_step(x[0], loss_target[0], meta_full, p, w_in_t, shards)
    grad_x = grad_x.reshape(x.shape)

    reds = []
    for k, (gp, land) in enumerate(pieces):
        half = gp.shape[1] // 2
        own = lax.dynamic_slice(gp, (chip, cc * half, 0), (1, half, D)).reshape(half, D)
        reds.append(_sum_parts("rs_sum_%d" % k, own, land))
    others = _swap_sibling(reds, "rs_share")
    lower = [jnp.where(cc == 0, r, o) for r, o in zip(reds, others)]
    upper = [jnp.where(cc == 0, o, r) for r, o in zip(reds, others)]
    g_up, g_down = lower[0], upper[0]
    g_glu, g_out = lower[1].T, upper[1]
    g_in_t = jnp.concatenate([lower[2], upper[2][:900 - C_ROWS // 2]], axis=0)

    gs = dict(zip(EARLY, _unpack_small(_sum8(all_early, _rows_for(EARLY), "sum8_early"), [SMALL_SHAPES[n] for n in EARLY])))
    all_late = _allgather8(_pack_small([late[n] for n in LATE], _rows_for(LATE)), "gather_small")
    gs.update(zip(LATE, _unpack_small(_sum8(all_late, _rows_for(LATE), "sum8_late"), [SMALL_SHAPES[n] for n in LATE])))
    g_conv_w = lax.dynamic_slice_in_dim(gs.pop("conv_w"), chip * 384, 384, axis=1).reshape(conv_w.shape)
    g_meta = lax.dynamic_slice_in_dim(gs.pop("meta_tokens"), chip * 256, 256, axis=1)
    loss = gs.pop("loss").reshape(())

    grads = dict(gs, meta_tokens=g_meta, conv_w=g_conv_w, w_in=g_in_t.T.reshape(w_in.shape), w_glu=g_glu.reshape(w_glu.shape),
                 w_out=g_out.reshape(w_out.shape), w_up=g_up.reshape(w_up.shape), w_down=g_down.reshape(w_down.shape))
    delta, new_m, new_v = {}, {}, {}
    d_, m_, v_ = _adamw_whole("adamw_w_in", w_in[0].T, g_in_t, m_w_in[0].T, v_w_in[0].T)
    delta["w_in"], new_m["w_in"], new_v["w_in"] = (a.T.reshape(w_in.shape) for a in (d_, m_, v_))
    for n in ("w_glu", "w_out", "w_up", "w_down"):
        shp = given[n].shape
        two = lambda a: a.reshape(shp[1], shp[2])
        d_, m_, v_ = _adamw("adamw_" + n, two(given[n]), two(grads[n]), two(given["m_" + n]), two(given["v_" + n]), 256)
        delta[n], new_m[n], new_v[n] = d_.reshape(shp), m_.reshape(shp), v_.reshape(shp)
    for n in EARLY + LATE[:-1]:
        shp = given[n].shape
        if len(shp) == 4 and shp[-1] == 16:
            two = back = lambda a: jnp.swapaxes(a, -1, -2)
        else:
            two = (lambda a: a.reshape(1, -1)) if len(shp) == 1 else (lambda a: a)
            back = lambda a: a.reshape(shp)
        d_, m_, v_ = _adamw_whole("adamw_" + n, two(given[n]), two(grads[n].reshape(shp)), two(given["m_" + n]), two(given["v_" + n]))
        delta[n], new_m[n], new_v[n] = back(d_), back(m_), back(v_)

    order = ["meta_tokens", "g_mix", "w_in", "conv_w", "conv_b", "dt_bias", "a_log", "d_ssd", "g_ssd", "lam_re", "lam_im", "log_step",
             "b_re", "b_im", "c_re", "c_im", "d_s5", "w_glu", "b_glu", "g_s5", "w_out", "g_mlp", "w_up", "w_down", "g_final"]
    grads_out = [grads[n].reshape(given[n].shape) for n in order]
    return (loss, grad_x, *grads_out, *[delta[n] for n in order], *[new_m[n] for n in order], *[new_v[n] for n in order])
```

```python
import math

import jax
import jax.numpy as jnp
from jax import lax
from jax.experimental import pallas as pl
from jax.experimental.pallas import tpu as pltpu

f32 = jnp.float32
bf16 = jnp.bfloat16

D = 1024
N_META = 16
CH = 256
HEADS = 16
HEAD_DIM = 64
NSTATE = 128
D_XBC = 1536
S5_P = 64
NQ = 8
PITCH = CH + 4
EPS = 1e-5
O_Z, O_U, O_XBC, O_DT, W_PROJ = 0, 1024, 2048, 3584, 3712
VMEM_LIMIT = 60 * 1024 * 1024

ADAM_LR, ADAM_B1, ADAM_B2, ADAM_EPS, ADAM_WD, ADAM_STEP = 0.001, 0.9, 0.999, 1e-08, 0.01, 10

NT = (((1,), (1,)), ((), ()))
TN = (((0,), (0,)), ((), ()))
_ANY = pl.BlockSpec(memory_space=pl.ANY)


def _cp(sem=None):
    return pltpu.CompilerParams(dimension_semantics=sem, vmem_limit_bytes=VMEM_LIMIT)


def _sigmoid(v):
    return 1.0 / (1.0 + jnp.exp(-v))


def _rsum8(v):
    r, c = v.shape
    return jnp.sum(v.reshape(r // 8, 8, c), axis=0)


def _rms(h, g):
    r = lax.rsqrt(jnp.mean(h * h, axis=-1, keepdims=True) + EPS)
    return h * r * g


def _rms_bwd(dy, h, g):
    r = lax.rsqrt(jnp.mean(h * h, axis=-1, keepdims=True) + EPS)
    n = h * r
    dn = dy * g
    dh = r * (dn - n * jnp.mean(dn * n, axis=-1, keepdims=True))
    return dh, _rsum8(dy * n)


def _dot(a, b, dims=None):
    if dims is None:
        return jnp.dot(a, b, preferred_element_type=f32)
    return lax.dot_general(a, b, dims, preferred_element_type=f32)


def _split_dot(v, m01, dims, terms, v_is_lhs=True):
    out, r = None, v
    for _ in range(terms):
        piece = r.astype(bf16)
        o = _dot(piece, m01, dims) if v_is_lhs else _dot(m01, piece, dims)
        out = o if out is None else out + o
        r = r - piece.astype(f32)
    return out


MESH = pl.DeviceIdType.MESH


def _place():
    return lax.axis_index("x"), lax.axis_index("y"), lax.axis_index("c")


def _flip(v, f):
    return 1 - v if f else v


def _call(body, comm, *, name, nsteps, in_specs, out_specs, out_shape, scratch_shapes, args):
    n_in, n_out, n_scr = len(in_specs), len(out_specs), len(scratch_shapes)
    if comm is None:
        res = pl.pallas_call(body, name=name, grid=(nsteps,), in_specs=in_specs, out_specs=out_specs, out_shape=out_shape,
                             scratch_shapes=scratch_shapes, compiler_params=_cp(("arbitrary",)))(*args)
        return list(res), []
    c_in, c_out = len(comm["ins"]), len(comm["outs"])

    def wrapped(*refs):
        o0 = n_in + c_in
        s0 = o0 + n_out + c_out
        cparts = (refs[n_in:o0], refs[o0 + n_out:s0], refs[s0 + n_scr:])

        @pl.when(pl.program_id(0) == 0)
        def _():
            comm["start"](*cparts)
        if "middle" in comm:
            @pl.when(pl.program_id(0) == (3 * nsteps) // 4)
            def _():
                comm["middle"](*cparts)
        body(*refs[:n_in], *refs[o0:o0 + n_out], *refs[s0:s0 + n_scr])

        @pl.when(pl.program_id(0) == nsteps - 1)
        def _():
            comm["finish"](*cparts)

    any_spec = pl.BlockSpec(memory_space=pl.ANY)
    res = pl.pallas_call(
        wrapped, name=name, grid=(nsteps,), in_specs=list(in_specs) + [any_spec] * c_in,
        out_specs=list(out_specs) + [any_spec] * c_out, out_shape=list(out_shape) + list(comm["outs"]),
        scratch_shapes=list(scratch_shapes) + list(comm["scratch"]),
        compiler_params=_cp(("arbitrary",)))(*args, *comm["ins"])
    return list(res[:n_out]), list(res[n_out:])


def _gather_piece(slab):
    r0, rows = 0, slab.shape[0]
    half = rows // 2
    flips = ((1, 0), (0, 1), (1, 1))

    def first(j, slab_ref, out_ref, send_sems, recv_sems):
        x, y, c = _place()
        return pltpu.make_async_remote_copy(
            src_ref=slab_ref.at[pl.ds(r0 + c * half, half), :], dst_ref=out_ref.at[2 * x + y, pl.ds(c * half, half), :],
            send_sem=send_sems.at[j], recv_sem=recv_sems.at[j],
            device_id=(_flip(x, flips[j][0]), _flip(y, flips[j][1]), c), device_id_type=MESH)

    def passed(j, out_ref, send_sems, recv_sems):
        x, y, c = _place()
        rows_j = out_ref.at[2 * _flip(x, flips[j][0]) + _flip(y, flips[j][1]), pl.ds(c * half, half), :]
        return pltpu.make_async_remote_copy(src_ref=rows_j, dst_ref=rows_j, send_sem=send_sems.at[3 + j],
                                            recv_sem=recv_sems.at[3 + j], device_id=(x, y, 1 - c), device_id_type=MESH)

    def start(ins, outs, scr):
        send_sems, recv_sems, stage, local_sems = scr
        x, y, _ = _place()
        load = pltpu.make_async_copy(ins[0].at[pl.ds(r0, rows), :], stage, local_sems.at[0])
        load.start()
        for j in range(3):
            first(j, ins[0], outs[0], send_sems, recv_sems).start()
        load.wait()
        pltpu.make_async_copy(stage, outs[0].at[2 * x + y], local_sems.at[1]).start()

    def middle(ins, outs, scr):
        send_sems, recv_sems, _, _ = scr
        for j in range(3):
            first(j, ins[0], outs[0], send_sems, recv_sems).wait_recv()
            passed(j, outs[0], send_sems, recv_sems).start()

    def finish(ins, outs, scr):
        send_sems, recv_sems, stage, local_sems = scr
        x, y, c = _place()
        for j in range(3):
            sib = outs[0].at[2 * _flip(x, flips[j][0]) + _flip(y, flips[j][1]), pl.ds((1 - c) * half, half), :]
            pltpu.make_async_remote_copy(src_ref=sib, dst_ref=sib, send_sem=send_sems.at[3 + j], recv_sem=recv_sems.at[3 + j],
                                         device_id=(x, y, 1 - c), device_id_type=MESH).wait_recv()
        for j in range(3):
            first(j, ins[0], outs[0], send_sems, recv_sems).wait_send()
            passed(j, outs[0], send_sems, recv_sems).wait_send()
        pltpu.make_async_copy(stage, outs[0].at[2 * x + y], local_sems.at[1]).wait()

    return dict(ins=[slab], outs=[jax.ShapeDtypeStruct((4, rows, D), bf16)],
                scratch=[pltpu.SemaphoreType.DMA((6,)), pltpu.SemaphoreType.DMA((6,)), pltpu.VMEM((rows, D), bf16),
                         pltpu.SemaphoreType.DMA((2,))], start=start, middle=middle, finish=finish)


def _scatter_piece(gpiece):
    half = gpiece.shape[1] // 2

    def copies(g_ref, land_ref, send_sems, recv_sems):
        x, y, c = _place()
        cps = []
        for fx in (0, 1):
            for fy in (0, 1):
                for fc in (0, 1):
                    k = 4 * fx + 2 * fy + fc - 1
                    if k < 0:
                        continue
                    px, py, pc = _flip(x, fx), _flip(y, fy), _flip(c, fc)
                    cps.append(pltpu.make_async_remote_copy(
                        src_ref=g_ref.at[2 * px + py, pl.ds(pc * half, half), :], dst_ref=land_ref.at[k],
                        send_sem=send_sems.at[k], recv_sem=recv_sems.at[k], device_id=(px, py, pc), device_id_type=MESH))
        return cps

    def start(ins, outs, scr):
        for cp in copies(ins[0], outs[0], *scr):
            cp.start()

    def finish(ins, outs, scr):
        for cp in copies(ins[0], outs[0], *scr):
            cp.wait()

    return dict(ins=[gpiece], outs=[jax.ShapeDtypeStruct((7, half, D), gpiece.dtype)],
                scratch=[pltpu.SemaphoreType.DMA((7,)), pltpu.SemaphoreType.DMA((7,))], start=start, finish=finish)


def _gather_blocks(block):
    rows = block.shape[0]

    def mine(out_ref):
        x, y, c = _place()
        return out_ref.at[pl.ds((4 * x + 2 * y + c) * rows, rows), :]

    def copies(b_ref, out_ref, send_sems, recv_sems):
        x, y, c = _place()
        cps = []
        for fx in (0, 1):
            for fy in (0, 1):
                for fc in (0, 1):
                    k = 4 * fx + 2 * fy + fc - 1
                    if k < 0:
                        continue
                    cps.append(pltpu.make_async_remote_copy(
                        src_ref=b_ref, dst_ref=mine(out_ref), send_sem=send_sems.at[k], recv_sem=recv_sems.at[k],
                        device_id=(_flip(x, fx), _flip(y, fy), _flip(c, fc)), device_id_type=MESH))
        return cps

    def start(ins, outs, scr):
        send_sems, recv_sems, stage, local_sems = scr
        load = pltpu.make_async_copy(ins[0], stage, local_sems.at[0])
        load.start()
        for cp in copies(ins[0], outs[0], send_sems, recv_sems):
            cp.start()
        load.wait()
        pltpu.make_async_copy(stage, mine(outs[0]), local_sems.at[1]).start()

    def finish(ins, outs, scr):
        send_sems, recv_sems, stage, local_sems = scr
        for cp in copies(ins[0], outs[0], send_sems, recv_sems):
            cp.wait()
        pltpu.make_async_copy(stage, mine(outs[0]), local_sems.at[1]).wait()

    return dict(ins=[block], outs=[jax.ShapeDtypeStruct((8 * rows, D), block.dtype)],
                scratch=[pltpu.SemaphoreType.DMA((7,)), pltpu.SemaphoreType.DMA((7,)), pltpu.VMEM((rows, D), block.dtype),
                         pltpu.SemaphoreType.DMA((2,))], start=start, finish=finish)


def _both(c1, c2):
    n = (len(c1["ins"]), len(c1["outs"]), len(c1["scratch"]))

    def split(parts):
        return [p[:k] for p, k in zip(parts, n)], [p[k:] for p, k in zip(parts, n)]

    def start(*parts):
        a, b = split(parts)
        c1["start"](*a)
        c2["start"](*b)

    def finish(*parts):
        a, b = split(parts)
        c1["finish"](*a)
        c2["finish"](*b)

    both = dict(ins=c1["ins"] + c2["ins"], outs=c1["outs"] + c2["outs"], scratch=c1["scratch"] + c2["scratch"],
                start=start, finish=finish)
    if "middle" in c1 or "middle" in c2:
        def middle(*parts):
            for cm, part in zip((c1, c2), split(parts)):
                if "middle" in cm:
                    cm["middle"](*part)
        both["middle"] = middle
    return both


def _rowwise(name, fn, nblk, rows=(), shifted=(), pars=(), refs=(), out_rows=(), out_accs=(), bm=CH, comm=None):
    n_sub = bm // CH
    n_r, n_s, n_p, n_w = len(rows), len(shifted) * n_sub, len(pars), len(refs)
    n_in = n_r + n_s + n_p + n_w
    n_o, n_a = len(out_rows), len(out_accs)

    def body(*all_refs):
        i = pl.program_id(0)
        ins = all_refs[:n_in]
        outs = all_refs[n_in:]
        rv = [r[...] for r in ins[:n_r]]
        sub = ins[n_r:n_r + n_s]
        sv = [jnp.concatenate([r[...] for r in sub[k * n_sub:(k + 1) * n_sub]], axis=0) if n_sub > 1 else sub[k][...]
              for k in range(len(shifted))]
        pv = [r[...] for r in ins[n_r + n_s:n_r + n_s + n_p]]
        ro, ao = fn(i, rv, sv, pv, list(ins[n_r + n_s + n_p:]))
        for r, v in zip(outs[:n_o], ro):
            r[...] = v.astype(r.dtype)
        accs = outs[n_o:]

        @pl.when(i == 0)
        def _():
            for r in accs:
                r[...] = jnp.zeros_like(r)
        for r, v in zip(accs, ao):
            r[...] += v

    in_specs = [pl.BlockSpec((bm, a.shape[1]), lambda i: (i, 0)) for a in rows]
    in_specs += [pl.BlockSpec((CH, a.shape[1]), lambda i, j=j: (jnp.maximum(n_sub * i - 1 + j, 0), 0))
                 for a in shifted for j in range(n_sub)]
    in_specs += [pl.BlockSpec(a.shape, lambda i, nd=a.ndim: (0,) * nd) for a in pars]
    in_specs += [spec for _, spec in refs]
    drop = [len(o) > 2 for o in out_rows]
    assert not any(drop) or bm == CH
    out_specs = [pl.BlockSpec((bm, o[0]), (lambda i: (jnp.maximum(i - 1, 0), 0)) if d else (lambda i: (i, 0)))
                 for o, d in zip(out_rows, drop)]
    out_specs += [pl.BlockSpec((8, c), lambda i: (0, 0)) for c in out_accs]
    out_shape = [jax.ShapeDtypeStruct((nblk * bm - (CH if d else 0), o[0]), o[1]) for o, d in zip(out_rows, drop)]
    out_shape += [jax.ShapeDtypeStruct((8, c), f32) for c in out_accs]
    res, cres = _call(body, comm, name=name, nsteps=nblk, in_specs=in_specs, out_specs=out_specs, out_shape=out_shape,
                      scratch_shapes=[], args=[*rows, *[a for a in shifted for _ in range(n_sub)], *pars, *[a for a, _ in refs]])
    parts = (res[:n_o], res[n_o:])
    return parts if comm is None else parts + (cres,)


C_ROWS = 928
C_WINDOW = 912
BIG_ROWS = 768
DW_ROWS = 2816


def _contract_rows(lp, big=DW_ROWS):
    for rows in (big, BIG_ROWS):
        if lp % rows == 0:
            return rows
    return CH


def _dw_into(name, a, b, slab, ka, a_sharded, row_blk, n_s, s0, piece_rows=2048, a_square=False):
    lp = a.shape[0]
    bm = _contract_rows(lp)
    steps = lp // bm

    def body(a_ref, b_ref, *rest):
        o_ref, acc = rest[-2], rest[-1]
        k = pl.program_id(1)

        @pl.when(k == 0)
        def _():
            acc[...] = jnp.zeros_like(acc)
        a_v = a_ref[...]
        acc[...] += _dot(a_v * a_v if a_square else a_v, b_ref[...], TN)

        @pl.when(k == steps - 1)
        def _():
            o_ref[0] = acc[...].astype(bf16)

    in_specs = [pl.BlockSpec((bm, ka), (lambda s, k: (k, s)) if a_sharded else (lambda s, k: (k, 0))),
                pl.BlockSpec((bm, D), (lambda s, k: (k, 0)) if a_sharded else (lambda s, k: (k, s)))]
    args = [a, b]
    aliases = {}
    if slab is not None:
        in_specs.append(_ANY)
        args.append(slab)
        aliases = {2: 0}
    return pl.pallas_call(
        body, name=name, grid=(n_s, steps), in_specs=in_specs,
        out_specs=pl.BlockSpec((1, ka, D), lambda s, k: (s0 + s, row_blk, 0)),
        out_shape=jax.ShapeDtypeStruct((4, piece_rows, D), bf16),
        scratch_shapes=[pltpu.VMEM((ka, D), f32)], input_output_aliases=aliases,
        compiler_params=_cp(("arbitrary", "arbitrary")))(*args)


def _dw_in_t(dproj, n0):
    lp = n0.shape[0]
    bm = _contract_rows(lp, BIG_ROWS)
    steps = lp // bm
    bn = 512

    def body(a_ref, b_ref, o_ref, acc):
        k = pl.program_id(1)

        @pl.when(k == 0)
        def _():
            acc[...] = jnp.zeros_like(acc)
        acc[...] += _dot(a_ref[...], b_ref[...], TN)

        @pl.when(k == steps - 1)
        def _():
            o_ref[...] = acc[...].astype(bf16)

    return pl.pallas_call(
        body, name="dw_in", grid=(D // bn, steps),
        in_specs=[pl.BlockSpec((bm, W_PROJ), lambda j, k: (k, 0)), pl.BlockSpec((bm, bn), lambda j, k: (k, j))],
        out_specs=pl.BlockSpec((W_PROJ, bn), lambda j, k: (0, j)),
        out_shape=jax.ShapeDtypeStruct((W_PROJ, D), bf16),
        scratch_shapes=[pltpu.VMEM((W_PROJ, bn), f32)],
        compiler_params=_cp(("arbitrary", "arbitrary")))(dproj, n0)


def _head_expand():
    h = lax.broadcasted_iota(jnp.int32, (HEADS, D), 0)
    c = lax.broadcasted_iota(jnp.int32, (HEADS, D), 1)
    return jnp.where((c >> 6) == h, 1.0, 0.0).astype(bf16)


def _ssd_common(i, P, prev8, cw, cb, dtb, alog, xc=None):
    z = P[:, O_Z:O_Z + D]
    xp = P[:, O_XBC:O_XBC + D_XBC]
    dt_raw = P[:, O_DT:O_DT + HEADS]
    row = lax.broadcasted_iota(jnp.int32, (CH, 1), 0)
    if xc is None:
        row8 = lax.broadcasted_iota(jnp.int32, (8, 1), 0)
        xc = cb + cw[3:4] * xp
        for k in (1, 2, 3):
            rolled = pltpu.roll(xp, k, 0)
            fix = pltpu.roll(prev8, k, 0)
            top = jnp.where(row8 < k, fix, rolled[0:8])
            xc = xc + cw[3 - k:4 - k] * jnp.concatenate([top, rolled[8:]], axis=0)
    sg = _sigmoid(xc)
    xbc = xc * sg
    live = jnp.where(jnp.logical_or(i > 0, row >= CH - N_META), 1.0, 0.0)
    pre = dt_raw + dtb
    dt = jnp.where(pre > 20.0, pre, jnp.log(1.0 + jnp.exp(jnp.minimum(pre, 20.0)))) * live
    a = -jnp.exp(alog)
    dta = dt * a
    r_i = lax.broadcasted_iota(jnp.int32, (CH, CH), 0)
    c_i = lax.broadcasted_iota(jnp.int32, (CH, CH), 1)
    tril = r_i >= c_i
    acs = _split_dot(dta, jnp.where(tril, 1.0, 0.0).astype(bf16), None, 3, v_is_lhs=False)
    acs_t = _split_dot(dta, jnp.where(r_i <= c_i, 1.0, 0.0).astype(bf16), TN, 3)
    e = _head_expand()
    acs_e = _split_dot(acs, e, None, 3)
    dt_e = _split_dot(dt, e, None, 3)
    return dict(z=z, xp=xp, xc=xc, sg=sg, xbc=xbc, live=live, pre=pre, dt=dt, a=a, tril=tril,
                acs=acs, acs_t=acs_t, e=e, acs_e=acs_e, dt_e=dt_e)


def _lmat(c, h):
    seg = c["acs"][:, h:h + 1] - c["acs_t"][h:h + 1, :]
    return jnp.where(c["tril"], jnp.exp(jnp.minimum(seg, 0.0)), 0.0)


def _pair_masks():
    lane = lax.broadcasted_iota(jnp.int32, (1, 128), 1)
    return jnp.where(lane < HEAD_DIM, 1.0, 0.0), jnp.where(lane >= HEAD_DIM, 1.0, 0.0)


def _ssd_fwd(proj, conv_w, conv_b, dt_bias, a_log, d_ssd, g_ssd, nch, comm=None):
    def body(p_ref, cw_ref, cb_ref, dtb_ref, al_ref, d_ref, g_ref, y_ref, ys_ref, st_ref, xc_ref, prev8_ref, state_ref):
        i = pl.program_id(0)

        @pl.when(i == 0)
        def _():
            prev8_ref[...] = jnp.zeros_like(prev8_ref)
            state_ref[...] = jnp.zeros_like(state_ref)

        P = p_ref[...]
        c = _ssd_common(i, P, prev8_ref[...], cw_ref[...], cb_ref[...], dtb_ref[...], al_ref[...])
        prev8_ref[...] = c["xp"][CH - 8:CH]
        xc_ref[...] = c["xc"]
        xbc = c["xbc"]
        x = xbc[:, 0:D]
        xdt = x * c["dt_e"]
        a_last_e = c["acs_e"][CH - 1:CH, :]
        w_end = (xdt * jnp.exp(a_last_e - c["acs_e"])).astype(bf16)
        m0, m1 = _pair_masks()
        ys = []
        for g in range(2):
            bg = xbc[:, D + NSTATE * g:D + NSTATE * (g + 1)].astype(bf16)
            cg = xbc[:, D + 2 * NSTATE + NSTATE * g:D + 2 * NSTATE + NSTATE * (g + 1)].astype(bf16)
            gmat = _dot(cg, bg, NT)
            st = state_ref[g]
            st_ref[0, g] = st
            sl = slice(512 * g, 512 * (g + 1))
            y_off = _dot(cg, st.astype(bf16)) * jnp.exp(c["acs_e"][:, sl])
            contrib = _dot(bg, w_end[:, sl], TN)
            state_ref[g] = st * jnp.exp(a_last_e[:, sl]) + contrib
            yd = []
            for pr in range(4):
                h0 = 8 * g + 2 * pr
                xp2 = xdt[:, 128 * (4 * g + pr):128 * (4 * g + pr + 1)]
                ma = (gmat * _lmat(c, h0)).astype(bf16)
                mb = (gmat * _lmat(c, h0 + 1)).astype(bf16)
                yd.append(_dot(ma, (xp2 * m0).astype(bf16)) + _dot(mb, (xp2 * m1).astype(bf16)))
            ys.append(jnp.concatenate(yd, axis=1) + y_off)
        d_e = _split_dot(d_ref[...], c["e"], None, 3)
        y = jnp.concatenate(ys, axis=1) + x * d_e
        y_ref[...] = y
        yg = y * (c["z"] * _sigmoid(c["z"]))
        ys_ref[...] = _rms(yg, g_ref[...]).astype(bf16)

    full = lambda a: pl.BlockSpec(a.shape, lambda i, nd=a.ndim: (0,) * nd)
    return _call(
        body, comm, name="ssd_fwd", nsteps=nch,
        in_specs=[pl.BlockSpec((CH, W_PROJ), lambda i: (i, 0))] + [full(a) for a in (conv_w, conv_b, dt_bias, a_log, d_ssd, g_ssd)],
        out_specs=[pl.BlockSpec((CH, D), lambda i: (i, 0)), pl.BlockSpec((CH, D), lambda i: (i, 0)),
                   pl.BlockSpec((1, 2, NSTATE, 512), lambda i: (i, 0, 0, 0)), pl.BlockSpec((CH, D_XBC), lambda i: (i, 0))],
        out_shape=[jax.ShapeDtypeStruct((nch * CH, D), f32), jax.ShapeDtypeStruct((nch * CH, D), bf16),
                   jax.ShapeDtypeStruct((nch, 2, NSTATE, 512), f32), jax.ShapeDtypeStruct((nch * CH, D_XBC), f32)],
        scratch_shapes=[pltpu.VMEM((8, D_XBC), f32), pltpu.VMEM((2, NSTATE, 512), f32)],
        args=[proj, conv_w, conv_b, dt_bias, a_log, d_ssd, g_ssd])


def _ssd_bwd(proj, xc_all, y, dys, du, states, conv_w, conv_b, dt_bias, a_log, d_ssd, g_ssd, nch, comm=None):
    def body(p_ref, xc_ref, y_ref, dys_ref, du_ref, st_ref, cw_ref, cb_ref, dtb_ref, al_ref, d_ref, g_ref,
             dp_ref, dcw_ref, dcb_ref, ddtb_ref, dal_ref, dd_ref, dg_ref, nxt8_ref, dst_ref):
        step = pl.program_id(0)
        i = nch - 1 - step

        @pl.when(step == 0)
        def _():
            nxt8_ref[...] = jnp.zeros_like(nxt8_ref)
            dst_ref[...] = jnp.zeros_like(dst_ref)
            for r in (dcw_ref, dcb_ref, ddtb_ref, dal_ref, dd_ref, dg_ref):
                r[...] = jnp.zeros_like(r)

        P = p_ref[...]
        c = _ssd_common(i, P, None, cw_ref[...], cb_ref[...], dtb_ref[...], al_ref[...], xc=xc_ref[...])
        xbc, z, e = c["xbc"], c["z"], c["e"]
        x = xbc[:, 0:D]
        yv = y_ref[...]
        sz = _sigmoid(z)
        silu_z = z * sz
        dyg, dg8 = _rms_bwd(dys_ref[...], yv * silu_z, g_ref[...])
        dg_ref[...] += dg8
        dy = dyg * silu_z
        dz = dyg * yv * (sz * (1.0 + z * (1.0 - sz)))
        d_e = _split_dot(d_ref[...], e, None, 3)
        dd_ref[...] += _rsum8(dy * x)
        xdt = x * c["dt_e"]
        a_last_e = c["acs_e"][CH - 1:CH, :]
        e_end = jnp.exp(a_last_e - c["acs_e"])
        w_end = xdt * e_end
        e_acs = jnp.exp(c["acs_e"])
        dy_dec = dy * e_acs
        m0, m1 = _pair_masks()
        lane16 = lax.broadcasted_iota(jnp.int32, (1, HEADS), 1)
        row16 = lax.broadcasted_iota(jnp.int32, (HEADS, 1), 0)
        dacs = jnp.zeros((CH, HEADS), f32)
        dacs_t = jnp.zeros((HEADS, CH), f32)
        dxdt_parts, dbs, dcs, zparts, yoff_parts, dlast_parts = [], [], [], [], [], []
        for g in range(2):
            sl = slice(512 * g, 512 * (g + 1))
            bg = xbc[:, D + NSTATE * g:D + NSTATE * (g + 1)].astype(bf16)
            cg = xbc[:, D + 2 * NSTATE + NSTATE * g:D + 2 * NSTATE + NSTATE * (g + 1)].astype(bf16)
            gmat = _dot(cg, bg, NT)
            st = st_ref[0, g]
            dstn = dst_ref[g]
            dstn_b = dstn.astype(bf16)
            y_off = _dot(cg, st.astype(bf16)) * e_acs[:, sl]
            yoff_parts.append(y_off)
            bds = _dot(bg, dstn_b)
            zparts.append(w_end[:, sl] * bds)
            dlast_parts.append(jnp.sum(dstn * st, axis=0, keepdims=True) * jnp.exp(a_last_e[:, sl]))
            dg_acc = jnp.zeros((CH, CH), f32)
            dxd = []
            for pr in range(4):
                lo = 128 * (4 * g + pr)
                xp2 = xdt[:, lo:lo + 128].astype(bf16)
                dy2 = dy[:, lo:lo + 128]
                outp = jnp.zeros((CH, 128), f32)
                for hh, msk in ((0, m0), (1, m1)):
                    h = 8 * g + 2 * pr + hh
                    lm = _lmat(c, h)
                    dyh = (dy2 * msk).astype(bf16)
                    mh = (gmat * lm).astype(bf16)
                    outp = outp + _dot(mh, dyh, TN)
                    dml = _dot(dyh, xp2, NT) * lm
                    dg_acc = dg_acc + dml
                    q = dml * gmat
                    dacs = dacs + jnp.where(lane16 == h, jnp.sum(q, axis=1, keepdims=True), 0.0)
                    dacs_t = dacs_t + jnp.where(row16 == h, jnp.sum(q, axis=0, keepdims=True), 0.0)
                dxd.append(outp)
            dxdt_parts.append(jnp.concatenate(dxd, axis=1) + e_end[:, sl] * bds)
            dgb = dg_acc.astype(bf16)
            dcs.append(_dot(dgb, bg) + _dot(dy_dec[:, sl].astype(bf16), st.astype(bf16), NT))
            dbs.append(_dot(dgb, cg, TN) + _dot(w_end[:, sl].astype(bf16), dstn_b, NT))
            dst_ref[g] = dstn * jnp.exp(a_last_e[:, sl]) + _dot(cg, dy_dec[:, sl].astype(bf16), TN)
        dxdt = jnp.concatenate(dxdt_parts, axis=1)
        zfull = jnp.concatenate(zparts, axis=1)
        y_off_full = jnp.concatenate(yoff_parts, axis=1)
        dlast = jnp.concatenate(dlast_parts, axis=1)
        red = lambda v: _split_dot(v, e, NT, 2)
        eye16 = jnp.where(lax.broadcasted_iota(jnp.int32, (HEADS, HEADS), 0) == lax.broadcasted_iota(jnp.int32, (HEADS, HEADS), 1),
                          1.0, 0.0).astype(bf16)
        dacs = dacs - _split_dot(dacs_t, eye16, TN, 3)
        zred = red(zfull)
        dacs = dacs + red(dy * y_off_full) - zred
        last_term = jnp.sum(zred, axis=0, keepdims=True) + red(dlast)
        rowc = lax.broadcasted_iota(jnp.int32, (CH, 1), 0)
        dacs = dacs + jnp.where(rowc == CH - 1, last_term, 0.0)
        r_i = lax.broadcasted_iota(jnp.int32, (CH, CH), 0)
        c_i = lax.broadcasted_iota(jnp.int32, (CH, CH), 1)
        ddta = _split_dot(dacs, jnp.where(c_i >= r_i, 1.0, 0.0).astype(bf16), None, 3, v_is_lhs=False)
        ddt = ddta * c["a"] + red(dxdt * x)
        dal_ref[...] += _rsum8(ddta * c["dt"] * c["a"])
        ddt_raw = ddt * _sigmoid(c["pre"]) * c["live"]
        ddtb_ref[...] += _rsum8(ddt_raw)
        dx = dy * d_e + dxdt * c["dt_e"]
        dxbc = jnp.concatenate([dx, dbs[0], dbs[1], dcs[0], dcs[1]], axis=1)
        sg = c["sg"]
        dxc = dxbc * (sg * (1.0 + c["xc"] * (1.0 - sg)))
        dcb_ref[...] += _rsum8(dxc)
        xp = c["xp"]
        row8 = lax.broadcasted_iota(jnp.int32, (8, 1), 0)
        cw = cw_ref[...]
        dxp = cw[3:4] * dxc
        dcw = jnp.where(row8 == 3, jnp.sum(dxc * xp, axis=0, keepdims=True), 0.0)
        nxt8 = nxt8_ref[...]
        for j in (1, 2, 3):
            rolled = pltpu.roll(dxc, CH - j, 0)
            fix = pltpu.roll(nxt8, 8 - j, 0)
            bot = jnp.where(row8 >= 8 - j, fix, rolled[CH - 8:CH])
            later = jnp.concatenate([rolled[:CH - 8], bot], axis=0)
            dxp = dxp + cw[3 - j:4 - j] * later
            dcw = dcw + jnp.where(row8 == 3 - j, jnp.sum(later * xp, axis=0, keepdims=True), 0.0)
        dcw_ref[...] += dcw
        nxt8_ref[...] = dxc[0:8]
        dp_ref[:, O_Z:O_Z + D] = dz.astype(bf16)
        dp_ref[:, O_U:O_U + D] = du_ref[...].astype(bf16)
        dp_ref[:, O_XBC:O_XBC + D_XBC] = dxp.astype(bf16)
        dp_ref[:, O_DT:W_PROJ] = jnp.zeros((CH, W_PROJ - O_DT), bf16)
        dp_ref[:, O_DT:O_DT + HEADS] = ddt_raw.astype(bf16)

    full = lambda a: pl.BlockSpec(a.shape, lambda s, nd=a.ndim: (0,) * nd)
    rev = lambda s: (nch - 1 - s, 0)
    acc = lambda cdim: pl.BlockSpec((8, cdim), lambda s: (0, 0))
    return _call(
        body, comm, name="ssd_bwd", nsteps=nch,
        in_specs=[pl.BlockSpec((CH, W_PROJ), rev), pl.BlockSpec((CH, D_XBC), rev),
                  pl.BlockSpec((CH, D), rev), pl.BlockSpec((CH, D), rev), pl.BlockSpec((CH, D), rev),
                  pl.BlockSpec((1, 2, NSTATE, 512), lambda s: (nch - 1 - s, 0, 0, 0))]
        + [full(a) for a in (conv_w, conv_b, dt_bias, a_log, d_ssd, g_ssd)],
        out_specs=[pl.BlockSpec((CH, W_PROJ), rev), acc(D_XBC), acc(D_XBC), acc(HEADS), acc(HEADS), acc(D), acc(D)],
        out_shape=[jax.ShapeDtypeStruct((nch * CH, W_PROJ), bf16)]
        + [jax.ShapeDtypeStruct((8, cdim), f32) for cdim in (D_XBC, D_XBC, HEADS, HEADS, D, D)],
        scratch_shapes=[pltpu.VMEM((8, D_XBC), f32), pltpu.VMEM((2, NSTATE, 512), f32)],
        args=[proj, xc_all, y, dys, du, states, conv_w, conv_b, dt_bias, a_log, d_ssd, g_ssd])


SCAN_UNROLL = 8


def _to_slabs(slab_ref, q, mat):
    for ls in range(8):
        slab_ref[ls, pl.ds(PITCH * q, CH), :] = mat[:, 128 * ls:128 * (ls + 1)]


def _from_slabs(slab, q):
    return jnp.concatenate([slab(ls, PITCH * q) for ls in range(8)], axis=1)


def _tile(slab_ref, ls, t, lead=None):
    idx = (ls, pl.ds(t, 8, stride=PITCH), slice(None))
    return slab_ref[idx] if lead is None else slab_ref[(lead,) + idx]


def _s5_fwd(proj, bbq, ccq, ar, ai, d_skip, nch, comm=None):
    def body(u_ref, bb_ref, cc_ref, ar_ref, ai_ref, d_ref, s_ref, yl_ref, y5_ref, bu_ref, st_ref):
        @pl.when(pl.program_id(0) == 0)
        def _():
            st_ref[...] = jnp.zeros_like(st_ref)
        u = u_ref[...]
        ub = u.astype(bf16)
        for q in range(NQ):
            _to_slabs(bu_ref, q, _dot(ub[:, 128 * q:128 * (q + 1)], bb_ref[q]))
        ar_t = [ar_ref[:, 128 * l:128 * (l + 1)] for l in range(4)]
        ai_t = [ai_ref[:, 128 * l:128 * (l + 1)] for l in range(4)]

        def one(t, carry):
            re, im = carry
            nre, nim = [], []
            for l in range(4):
                a = ar_t[l] * re[l] - ai_t[l] * im[l] + _tile(bu_ref, l, t)
                b = ar_t[l] * im[l] + ai_t[l] * re[l] + _tile(bu_ref, l + 4, t)
                s_ref[0, l, pl.ds(t, 8, stride=PITCH), :] = a
                s_ref[0, l + 4, pl.ds(t, 8, stride=PITCH), :] = b
                nre.append(a)
                nim.append(b)
            return tuple(nre), tuple(nim)

        def step(tt, carry):
            for k in range(SCAN_UNROLL):
                carry = one(tt * SCAN_UNROLL + k, carry)
            return carry
        init = (tuple(st_ref[l] for l in range(4)), tuple(st_ref[l + 4] for l in range(4)))
        re, im = lax.fori_loop(0, CH // SCAN_UNROLL, step, init)
        for l in range(4):
            st_ref[l] = re[l]
            st_ref[l + 4] = im[l]
        ys = []
        for q in range(NQ):
            sq = _from_slabs(lambda ls, r0: s_ref[0, ls, pl.ds(r0, CH), :], q).astype(bf16)
            ys.append(_dot(sq, cc_ref[q], NT))
        yl = jnp.concatenate(ys, axis=1) + u * d_ref[...]
        yl_ref[...] = yl
        y5_ref[...] = (0.5 * yl * (1.0 + lax.erf(yl * (1.0 / math.sqrt(2.0))))).astype(bf16)

    const = lambda a: pl.BlockSpec(a.shape, lambda i, nd=a.ndim: (0,) * nd)
    return _call(
        body, comm, name="s5_fwd", nsteps=nch,
        in_specs=[pl.BlockSpec((CH, D), lambda i: (i, O_U // D)), const(bbq), const(ccq), const(ar), const(ai), const(d_skip)],
        out_specs=[pl.BlockSpec((1, 8, 8 * PITCH, 128), lambda i: (i, 0, 0, 0)),
                   pl.BlockSpec((CH, D), lambda i: (i, 0)), pl.BlockSpec((CH, D), lambda i: (i, 0))],
        out_shape=[jax.ShapeDtypeStruct((nch, 8, 8 * PITCH, 128), f32), jax.ShapeDtypeStruct((nch * CH, D), f32),
                   jax.ShapeDtypeStruct((nch * CH, D), bf16)],
        scratch_shapes=[pltpu.VMEM((8, 8 * PITCH, 128), f32), pltpu.VMEM((8, 8, 128), f32)],
        args=[proj, bbq, ccq, ar, ai, d_skip])


def _s5_bwd(proj, dyl, s_all, bbq, ccq, ar, ai, d_skip, nch, comm=None):
    def body(u_ref, dy_ref, s_ref, bbt_ref, cct_ref, ar_ref, ai_ref, d_ref,
             du_ref, dcc_ref, dbb_ref, dab_ref, dd_ref, ga_ref, st_ref):
        @pl.when(pl.program_id(0) == 0)
        def _():
            st_ref[...] = jnp.zeros_like(st_ref)
            for r in (dcc_ref, dbb_ref, dab_ref, dd_ref):
                r[...] = jnp.zeros_like(r)
        u = u_ref[...]
        dyl_v = dy_ref[...]
        dd_ref[...] += _rsum8(dyl_v * u)
        ub = u.astype(bf16)
        dyb = dyl_v.astype(bf16)
        for q in range(NQ):
            _to_slabs(ga_ref, q, _dot(dyb[:, 128 * q:128 * (q + 1)], cct_ref[q]))
        ar_t = [ar_ref[:, 128 * l:128 * (l + 1)] for l in range(4)]
        ai_t = [ai_ref[:, 128 * l:128 * (l + 1)] for l in range(4)]

        def one(t, carry):
            re, im, dar, dai = carry
            nre, nim, ndar, ndai = [], [], [], []
            for l in range(4):
                sre = _tile(s_ref, l, t, lead=0)
                sim = _tile(s_ref, l + 4, t, lead=0)
                ndar.append(dar[l] + re[l] * sre + im[l] * sim)
                ndai.append(dai[l] + im[l] * sre - re[l] * sim)
                a = _tile(ga_ref, l, t) + ar_t[l] * re[l] + ai_t[l] * im[l]
                b = _tile(ga_ref, l + 4, t) - ai_t[l] * re[l] + ar_t[l] * im[l]
                ga_ref[l, pl.ds(t, 8, stride=PITCH), :] = a
                ga_ref[l + 4, pl.ds(t, 8, stride=PITCH), :] = b
                nre.append(a)
                nim.append(b)
            return tuple(nre), tuple(nim), tuple(ndar), tuple(ndai)

        def step(tt, carry):
            for k in range(SCAN_UNROLL):
                carry = one(CH - 1 - (tt * SCAN_UNROLL + k), carry)
            return carry
        four = lambda ref, o: tuple(ref[l + o] for l in range(4))
        re, im, dar, dai = lax.fori_loop(0, CH // SCAN_UNROLL, step,
                                         (four(st_ref, 0), four(st_ref, 4), four(dab_ref, 0), four(dab_ref, 4)))
        for l in range(4):
            st_ref[l], st_ref[l + 4] = re[l], im[l]
            dab_ref[l], dab_ref[l + 4] = dar[l], dai[l]
        dus = []
        for q in range(NQ):
            aq = _from_slabs(lambda ls, r0: ga_ref[ls, pl.ds(r0, CH), :], q).astype(bf16)
            sq = _from_slabs(lambda ls, r0: s_ref[0, ls, pl.ds(r0, CH), :], q).astype(bf16)
            dcc_ref[q] += _dot(dyb[:, 128 * q:128 * (q + 1)], sq, TN)
            dbb_ref[q] += _dot(ub[:, 128 * q:128 * (q + 1)], aq, TN)
            dus.append(_dot(aq, bbt_ref[q], NT))
        du_ref[...] = jnp.concatenate(dus, axis=1) + dyl_v * d_ref[...]

    const = lambda a: pl.BlockSpec(a.shape, lambda s, nd=a.ndim: (0,) * nd)
    rev = lambda s: (nch - 1 - s, 0)
    return _call(
        body, comm, name="s5_bwd", nsteps=nch,
        in_specs=[pl.BlockSpec((CH, D), lambda s: (nch - 1 - s, O_U // D)), pl.BlockSpec((CH, D), rev),
                  pl.BlockSpec((1, 8, 8 * PITCH, 128), lambda s: (nch - 1 - s, 0, 0, 0)),
                  const(bbq), const(ccq), const(ar), const(ai), const(d_skip)],
        out_specs=[pl.BlockSpec((CH, D), rev), pl.BlockSpec((NQ, 128, D), lambda s: (0, 0, 0)),
                   pl.BlockSpec((NQ, 128, D), lambda s: (0, 0, 0)), pl.BlockSpec((8, 8, 128), lambda s: (0, 0, 0)),
                   pl.BlockSpec((8, D), lambda s: (0, 0))],
        out_shape=[jax.ShapeDtypeStruct((nch * CH, D), f32), jax.ShapeDtypeStruct((NQ, 128, D), f32),
                   jax.ShapeDtypeStruct((NQ, 128, D), f32), jax.ShapeDtypeStruct((8, 8, 128), f32),
                   jax.ShapeDtypeStruct((8, D), f32)],
        scratch_shapes=[pltpu.VMEM((8, 8 * PITCH, 128), f32), pltpu.VMEM((8, 8, 128), f32)],
        args=[proj, dyl, s_all, bbq, ccq, ar, ai, d_skip])


def _s5_tables(lam_re, lam_im, log_step, b_re, b_im):
    step = jnp.exp(log_step)[:, None]
    mag = jnp.exp(lam_re * step)
    ab_re = mag * jnp.cos(lam_im * step)
    ab_im = mag * jnp.sin(lam_im * step)
    den = lam_re * lam_re + lam_im * lam_im
    coef_re = ((ab_re - 1.0) * lam_re + ab_im * lam_im) / den
    coef_im = (ab_im * lam_re - (ab_re - 1.0) * lam_im) / den
    bb_re = coef_re[..., None] * b_re - coef_im[..., None] * b_im
    bb_im = coef_re[..., None] * b_im + coef_im[..., None] * b_re
    return ab_re, ab_im, bb_re, bb_im


def _blockdiag_in(m_re, m_im):
    eye = jnp.eye(8, dtype=f32)

    def one(m):
        m = m.reshape(NQ, 8, S5_P, 16)
        return jnp.einsum("qgph,gk->qghkp", m, eye).reshape(NQ, 128, 512)
    return jnp.concatenate([one(m_re), one(m_im)], axis=2)


def _blockdiag_in_grad(dm):
    def one(x):
        x = x.reshape(NQ, 8, 16, 8, S5_P)
        return jnp.einsum("qghgp->qgph", x).reshape(NQ * 8, S5_P, 16)
    return one(dm[:, :, :512]), one(dm[:, :, 512:])


def _local_step(x2, tgt2, meta, p, w_in_t, shards):
    seq = x2.shape[0]
    nch = 1 + seq // CH
    bmb = BIG_ROWS if (nch * CH) % BIG_ROWS == 0 else CH
    nbig = nch * CH // bmb
    metablk = jnp.concatenate([jnp.zeros((CH - N_META, D), f32), meta, jnp.zeros((bmb - CH, D), f32)], axis=0)
    w_full = (w_in_t, pl.BlockSpec(w_in_t.shape, lambda i: (0, 0), pipeline_mode=pl.Buffered(1)))

    def lead(i, v):
        return jnp.logical_and(i == 0, lax.broadcasted_iota(jnp.int32, (v.shape[0], 1), 0) < CH)
    h0_of = lambda i, s, q: jnp.where(lead(i, s[0]), q[0][:s[0].shape[0]], s[0])

    def in_fn(i, r, s, q, w):
        nb = _rms(h0_of(i, s, q), q[1]).astype(bf16)
        return [_dot(nb, w[0][...], NT), nb], []
    (proj, n0), _, (g_up,) = _rowwise("in_proj", in_fn, nbig, shifted=[x2], pars=[metablk, p["g_mix"]], refs=[w_full],
                                      out_rows=[(W_PROJ, f32), (D, bf16)], bm=bmb, comm=_gather_piece(shards[0]))
    (y, y_ssd, states, xc_all), (g_down,) = _ssd_fwd(proj, p["conv_w"], p["conv_b"], p["dt_bias"], p["a_log"], p["d_ssd"],
                                                     p["g_ssd"], nch, comm=_gather_piece(shards[1]))

    ab_re, ab_im, bb_re, bb_im = _s5_tables(p["lam_re"], p["lam_im"], p["log_step"], p["b_re"], p["b_im"])
    ar, ai = ab_re.reshape(NQ, 512), ab_im.reshape(NQ, 512)
    bbq = _blockdiag_in(bb_re, bb_im)
    ccq = _blockdiag_in(jnp.swapaxes(p["c_re"], 1, 2), -jnp.swapaxes(p["c_im"], 1, 2))
    d_skip = p["d_s5"].reshape(1, D)
    bbq_b, ccq_b = bbq.astype(bf16), ccq.astype(bf16)
    (s_all, ylin, y5), (g_go,) = _s5_fwd(proj, bbq_b, ccq_b, ar, ai, d_skip, nch, comm=_gather_piece(shards[2]))
    whole = lambda a: (a, pl.BlockSpec(a.shape, lambda i: (0, 0, 0), pipeline_mode=pl.Buffered(1)))
    w_up, w_down = whole(g_up), whole(g_down)
    w_glu_t = (g_go, pl.BlockSpec((4, 512, D), lambda i: (0, 0, 0), pipeline_mode=pl.Buffered(1)))
    w_out = (g_go, pl.BlockSpec((4, 512, D), lambda i: (0, 1, 0), pipeline_mode=pl.Buffered(1)))

    def glu_fn(i, r, s, q, w):
        v = jnp.concatenate([_dot(r[0], w[0][k], NT) for k in range(4)], axis=1) + q[0]
        return [v, _rms(v[:, :D] * _sigmoid(v[:, D:]), q[1])], []
    (v, y_s5), _ = _rowwise("glu", glu_fn, nbig, rows=[y5], pars=[p["b_glu"], p["g_s5"]], refs=[w_glu_t],
                            out_rows=[(2 * D, bf16), (D, bf16)], bm=bmb)

    def out_fn(i, r, s, q, w):
        acc = (_dot(r[0][:, :512], w[0][0]) + _dot(r[0][:, 512:], w[0][1]) + _dot(r[1][:, :512], w[0][2])
               + _dot(r[1][:, 512:], w[0][3]))
        return [h0_of(i, s, q) + acc], []
    (h1,), _ = _rowwise("out_proj", out_fn, nbig, rows=[y_ssd, y_s5], shifted=[x2], pars=[metablk], refs=[w_out],
                        out_rows=[(D, f32)], bm=bmb)

    def up_fn(i, r, s, q, w):
        nb = _rms(r[0], q[0]).astype(bf16)
        return [jnp.concatenate([jnp.maximum(_dot(nb, w[0][k]), 0.0).astype(bf16) for k in range(4)], axis=1), nb], []
    (relu_m, n1), _ = _rowwise("up_proj", up_fn, nbig, rows=[h1], pars=[p["g_mlp"]], refs=[w_up],
                               out_rows=[(4 * D, bf16), (D, bf16)], bm=bmb)

    def down_fn(i, r, s, q, w):
        acc = None
        for k in range(4):
            t = r[0][:, D * k:D * (k + 1)]
            part = _dot(t * t, w[0][k])
            acc = part if acc is None else acc + part
        return [r[1] + acc], []
    (h2,), _ = _rowwise("down_proj", down_fn, nbig, rows=[relu_m, h1], refs=[w_down], out_rows=[(D, f32)], bm=bmb)

    def final_fn(i, r, s, q, w):
        err = jnp.where(lead(i, r[0]), 0.0, _rms(r[0], q[0]) - s[0])
        dh, dg8 = _rms_bwd(err * (1.0 / D), r[0], q[0])
        return [dh, dh], [_rsum8(err * err), dg8]
    (dh2, dh2_b), (loss8, dgf8) = _rowwise("final", final_fn, nbig, rows=[h2], shifted=[tgt2], pars=[p["g_final"]],
                                           out_rows=[(D, f32), (D, bf16)], out_accs=[D, D], bm=bmb)
    loss = 0.5 / D * jnp.sum(loss8)

    def down_bwd_fn(i, r, s, q, w):
        dm_ = [_dot(r[0], w[0][k], NT) * (2.0 * r[1][:, D * k:D * (k + 1)].astype(f32)) for k in range(4)]
        return [jnp.concatenate(dm_, axis=1)], []
    (dm,), _ = _rowwise("down_bwd", down_bwd_fn, nbig, rows=[dh2_b, relu_m], refs=[w_down], out_rows=[(4 * D, bf16)], bm=bmb)
    g_a = _dw_into("dw_down", relu_m, dh2_b, None, 1024, True, 1, 4, 0, piece_rows=2048, a_square=True)

    def up_bwd_fn(i, r, s, q, w):
        acc = _dot(r[0][:, :D], w[0][0], NT)
        for k in range(1, 4):
            acc = acc + _dot(r[0][:, D * k:D * (k + 1)], w[0][k], NT)
        dh, dg8 = _rms_bwd(acc, r[1], q[0])
        dh1_ = r[2] + dh
        return [dh1_, dh1_], [dg8]
    (dh1, dh1_b), (dgmlp8,) = _rowwise("up_bwd", up_bwd_fn, nbig, rows=[dm, h1, dh2], pars=[p["g_mlp"]], refs=[w_up],
                                       out_rows=[(D, f32), (D, bf16)], out_accs=[D], bm=bmb)
    g_a = _dw_into("dw_up", n1, dm, g_a, 1024, False, 0, 4, 0, piece_rows=2048)

    def out_bwd_fn(i, r, s, q, w):
        dmix = [_dot(r[0], w[0][k], NT) for k in range(4)]
        v1, v2 = r[1][:, :D].astype(f32), r[1][:, D:].astype(f32)
        s2 = _sigmoid(v2)
        dglu, dg8 = _rms_bwd(jnp.concatenate(dmix[2:], axis=1), v1 * s2, q[0])
        dv = jnp.concatenate([dglu * s2, dglu * v1 * s2 * (1.0 - s2)], axis=1)
        return [jnp.concatenate(dmix[:2], axis=1), dv], [dg8, _rsum8(dv)]
    (dys, dv), (dgs58, dbglu8) = _rowwise("out_bwd", out_bwd_fn, nbig, rows=[dh1_b, v], pars=[p["g_s5"]], refs=[w_out],
                                          out_rows=[(D, f32), (2 * D, bf16)], out_accs=[D, 2 * D], bm=bmb)
    g_b = _dw_into("dw_out_a", y_ssd, dh1_b, None, 512, True, 1, 2, 0, piece_rows=1024)
    g_b = _dw_into("dw_out_b", y_s5, dh1_b, g_b, 512, True, 1, 2, 2, piece_rows=1024)

    def glu_bwd_fn(i, r, s, q, w):
        acc = _dot(r[0][:, :512], w[0][0])
        for k in range(1, 4):
            acc = acc + _dot(r[0][:, 512 * k:512 * (k + 1)], w[0][k])
        yl = r[1]
        cdf = 0.5 * (1.0 + lax.erf(yl * (1.0 / math.sqrt(2.0))))
        pdf = jnp.exp(-0.5 * yl * yl) * (1.0 / math.sqrt(2.0 * math.pi))
        return [acc * (cdf + yl * pdf)], []
    (dylin,), _ = _rowwise("glu_bwd", glu_bwd_fn, nbig, rows=[dv, ylin], refs=[w_glu_t], out_rows=[(D, f32)], bm=bmb)
    g_b = _dw_into("dw_glu", dv, y5, g_b, 512, True, 0, 4, 0, piece_rows=1024)

    (du, dcc, dbb, dab, dds5), (land_a,) = _s5_bwd(proj, dylin, s_all, bbq_b, ccq_b, ar, ai, d_skip, nch,
                                                   comm=_scatter_piece(g_a))

    s8 = lambda a: jnp.sum(a, axis=0, keepdims=True)
    dab_q = jnp.swapaxes(dab.reshape(2, 4, NQ, 128), 1, 2).reshape(2, NQ * 8, S5_P)
    dbb_re, dbb_im = _blockdiag_in_grad(dbb)
    dcr, dci = _blockdiag_in_grad(dcc)
    _, vjp = jax.vjp(_s5_tables, p["lam_re"], p["lam_im"], p["log_step"], p["b_re"], p["b_im"])
    dlam_re, dlam_im, dlog_step, db_re, db_im = vjp((dab_q[0], dab_q[1], dbb_re, dbb_im))
    early = dict(lam_re=dlam_re, lam_im=dlam_im, log_step=dlog_step, b_re=db_re, b_im=db_im, c_re=jnp.swapaxes(dcr, 1, 2),
                 c_im=-jnp.swapaxes(dci, 1, 2), d_s5=s8(dds5).reshape(NQ * 8, 16), b_glu=s8(dbglu8), g_s5=s8(dgs58),
                 g_mlp=s8(dgmlp8), g_final=s8(dgf8).reshape(D))
    early_pack = _pack_small([early[n] for n in EARLY], _rows_for(EARLY))

    (dproj, dcw8, dcb8, ddtb8, dal8, dd8, dgssd8), (land_b, all_early) = _ssd_bwd(
        proj, xc_all, y, dys, du, states, p["conv_w"], p["conv_b"], p["dt_bias"], p["a_log"], p["d_ssd"], p["g_ssd"], nch,
        comm=_both(_scatter_piece(g_b), _gather_blocks(early_pack)))

    gt = _dw_in_t(dproj, n0)
    windows = [[(0, 912)], [(896, 1024), (O_XBC, O_XBC + 784)], [(O_XBC + 768, O_DT + HEADS), (O_U, O_U + 128)],
               [(O_U + 112, O_U + D)]]
    assert all(sum(b - a for a, b in w) == C_WINDOW for w in windows)
    g_c = jnp.concatenate([part for w in windows for part in
                           [gt[a:b] for a, b in w] + [jnp.zeros((C_ROWS - C_WINDOW, D), bf16)]], axis=0).reshape(4, C_ROWS, D)

    def in_bwd_fn(i, r, s, q, w):
        dh, dg8 = _rms_bwd(_dot(r[0], w[0][...]), h0_of(i, s, q), q[1])
        dh0 = r[1] + dh
        dmeta = jnp.where(i == 0, dh0[CH - N_META:], 0.0)
        return [dh0], [dg8, dmeta[:8], dmeta[8:]]
    (grad_x,), (dgmix8, dmeta_a, dmeta_b), (land_c,) = _rowwise(
        "in_bwd", in_bwd_fn, nch, rows=[dproj, dh1], shifted=[x2], pars=[metablk, p["g_mix"]], refs=[w_full],
        out_rows=[(D, f32, "shifted")], out_accs=[D, D, D], bm=CH, comm=_scatter_piece(g_c))

    hsum = lambda a: jnp.sum(s8(a).reshape(HEADS, HEAD_DIM), axis=1).reshape(1, HEADS)
    late = dict(g_mix=s8(dgmix8), conv_b=s8(dcb8), dt_bias=s8(ddtb8), a_log=s8(dal8), d_ssd=hsum(dd8), g_ssd=s8(dgssd8),
                conv_w=dcw8[0:4], meta_tokens=jnp.concatenate([dmeta_a, dmeta_b], axis=0), loss=loss.reshape(1))
    return grad_x, [(g_a, land_a), (g_b, land_b), (g_c, land_c)], all_early, late


def _perm_rows_w_in(wt):
    return jnp.concatenate([wt[0:1024], wt[2576:3600], wt[1024:2560], wt[2560:2576],
                            jnp.zeros((W_PROJ - 3600, wt.shape[1]), wt.dtype)], axis=0)


def _allgather8(x_shard, name):
    m_per, n = x_shard.shape

    def body(x_ref, out_ref, send_sems, recv_sems, stage, local_sems):
        x, y, c = _place()
        me, sibling = (x, y, c), (x, y, 1 - c)
        chips = [(1 - x, y), (x, 1 - y), (1 - x, 1 - y)]

        def rows(px, py, pc):
            return out_ref.at[pl.ds((4 * px + 2 * py + pc) * m_per, m_per), :]

        def copy(k, block, to, src=None):
            return pltpu.make_async_remote_copy(
                src_ref=rows(*block) if src is None else src, dst_ref=rows(*block),
                send_sem=send_sems.at[k], recv_sem=recv_sems.at[k], device_id=to, device_id_type=MESH)

        load = pltpu.make_async_copy(x_ref, stage, local_sems.at[0])
        load.start()
        first = [copy(0, me, sibling, src=x_ref)]
        first += [copy(1 + j, me, (*chip, c), src=x_ref) for j, chip in enumerate(chips)]
        for cp in first:
            cp.start()
        load.wait()
        store = pltpu.make_async_copy(stage, rows(*me), local_sems.at[1])
        store.start()
        passed = [copy(4 + j, (*chip, c), sibling) for j, chip in enumerate(chips)]
        for j, chip in enumerate(chips):
            copy(1 + j, (*chip, c), me).wait_recv()
            passed[j].start()
        copy(0, sibling, me).wait_recv()
        for j, chip in enumerate(chips):
            copy(4 + j, (*chip, 1 - c), me).wait_recv()
        for cp in first + passed:
            cp.wait_send()
        store.wait()

    return pl.pallas_call(
        body, name=name, out_shape=jax.ShapeDtypeStruct((8 * m_per, n), x_shard.dtype),
        in_specs=[_ANY], out_specs=_ANY,
        scratch_shapes=[pltpu.SemaphoreType.DMA((7,)), pltpu.SemaphoreType.DMA((7,)), pltpu.VMEM((m_per, n), x_shard.dtype),
                        pltpu.SemaphoreType.DMA((2,))])(x_shard)


def _swap_sibling(parts, name):
    n = len(parts)

    def body(*refs):
        send_sems, recv_sems = refs[2 * n:]
        x, y, c = _place()
        cps = [pltpu.make_async_remote_copy(src_ref=refs[k], dst_ref=refs[n + k], send_sem=send_sems.at[k],
                                            recv_sem=recv_sems.at[k], device_id=(x, y, 1 - c), device_id_type=MESH)
               for k in range(n)]
        for cp in cps:
            cp.start()
        for cp in cps:
            cp.wait()

    return pl.pallas_call(
        body, name=name, out_shape=[jax.ShapeDtypeStruct(r.shape, r.dtype) for r in parts], in_specs=[_ANY] * n,
        out_specs=[_ANY] * n, scratch_shapes=[pltpu.SemaphoreType.DMA((n,)), pltpu.SemaphoreType.DMA((n,))])(*parts)


SH_CONVW, SH_META = 4 * 384, 16 * 256
SPARE_ROWS = 17

SMALL_SHAPES = dict(
    g_mix=(1, 1024), conv_b=(1, 1536), dt_bias=(1, 16), a_log=(1, 16), d_ssd=(1, 16), g_ssd=(1, 1024), lam_re=(1, 64, 64),
    lam_im=(1, 64, 64), log_step=(1, 64), b_re=(1, 64, 64, 16), b_im=(1, 64, 64, 16), c_re=(1, 64, 16, 64), c_im=(1, 64, 16, 64),
    d_s5=(1, 64, 16), b_glu=(1, 2048), g_s5=(1, 1024), g_mlp=(1, 1024), g_final=(1024,),
    conv_w=(4, D_XBC), meta_tokens=(N_META, D), loss=(1,))
EARLY = ["lam_re", "lam_im", "log_step", "b_re", "b_im", "c_re", "c_im", "d_s5", "b_glu", "g_s5", "g_mlp", "g_final"]
LATE = ["g_mix", "conv_b", "dt_bias", "a_log", "d_ssd", "g_ssd", "conv_w", "meta_tokens", "loss"]


def _rows_for(names):
    return -(-sum(math.prod(SMALL_SHAPES[n]) for n in names) // (8 * D)) * 8


def _pack_small(arrs, rows):
    flat = jnp.concatenate([a.reshape(-1).astype(f32) for a in arrs])
    return jnp.concatenate([flat, jnp.zeros((rows * D - flat.shape[0],), f32)]).reshape(rows, D)


def _unpack_small(slab, shapes):
    flat = slab.reshape(-1)
    out, o = [], 0
    for shp in shapes:
        n = math.prod(shp)
        out.append(flat[o:o + n].reshape(shp))
        o += n
    return out


def _sum8(g, rows, name):
    def body(g_ref, o_ref):
        acc = g_ref[0]
        for k in range(1, 8):
            acc = acc + g_ref[k]
        o_ref[...] = acc
    return pl.pallas_call(body, name=name, out_shape=jax.ShapeDtypeStruct((rows, D), f32),
                          compiler_params=_cp())(g.reshape(8, rows, D))


def _adam_math(w_, g_, m_, v_):
    m2 = ADAM_B1 * m_ + (1.0 - ADAM_B1) * g_
    v2 = ADAM_B2 * v_ + (1.0 - ADAM_B2) * jnp.square(g_)
    m_hat = m2 / (1.0 - ADAM_B1 ** ADAM_STEP)
    v_hat = v2 / (1.0 - ADAM_B2 ** ADAM_STEP)
    delta = -ADAM_LR * (m_hat / (jnp.sqrt(v_hat) + ADAM_EPS) + ADAM_WD * w_)
    return delta, m2, v2


def _adamw(name, w, g, m, v, bm):
    def fn(i, r, s, q, refs):
        return list(_adam_math(*r)), []
    c = w.shape[1]
    (d, m2, v2), _ = _rowwise(name, fn, w.shape[0] // bm, rows=[w, g, m, v], out_rows=[(c, f32)] * 3, bm=bm)
    return d, m2, v2


def _adamw_whole(name, w, g, m, v):
    def body(w_ref, g_ref, m_ref, v_ref, d_ref, m2_ref, v2_ref):
        d_ref[...], m2_ref[...], v2_ref[...] = _adam_math(w_ref[...], g_ref[...], m_ref[...], v_ref[...])
    return pl.pallas_call(body, name=name, out_shape=[jax.ShapeDtypeStruct(w.shape, f32)] * 3, compiler_params=_cp())(w, g, m, v)


def _sum_parts(name, own, land):
    def fn(i, r, s, q, refs):
        acc = r[0].astype(f32)
        for k in range(7):
            acc = acc + refs[0][k].astype(f32)
        return [acc], []
    rows = own.shape[0]
    bm = CH if rows % CH == 0 else rows
    (o,), _ = _rowwise(name, fn, rows // bm, rows=[own], refs=[(land, pl.BlockSpec((7, bm, D), lambda i: (0, i, 0)))],
                       out_rows=[(D, f32)], bm=bm)
    return o


def kernel(x, meta_tokens, g_mix, w_in, conv_w, conv_b, dt_bias, a_log, d_ssd, g_ssd, lam_re, lam_im, log_step, b_re, b_im, c_re, c_im, d_s5, w_glu, b_glu, g_s5, w_out, g_mlp, w_up, w_down, g_final, loss_target, m_meta_tokens, m_g_mix, m_w_in, m_conv_w, m_conv_b, m_dt_bias, m_a_log, m_d_ssd, m_g_ssd, m_lam_re, m_lam_im, m_log_step, m_b_re, m_b_im, m_c_re, m_c_im, m_d_s5, m_w_glu, m_b_glu, m_g_s5, m_w_out, m_g_mlp, m_w_up, m_w_down, m_g_final, v_meta_tokens, v_g_mix, v_w_in, v_conv_w, v_conv_b, v_dt_bias, v_a_log, v_d_ssd, v_g_ssd, v_lam_re, v_lam_im, v_log_step, v_b_re, v_b_im, v_c_re, v_c_im, v_d_s5, v_w_glu, v_b_glu, v_g_s5, v_w_out, v_g_mlp, v_w_up, v_w_down, v_g_final):
    given = dict(locals())
    cx, cy, cc = _place()
    chip = 2 * cx + cy

    small_f = jnp.concatenate([conv_w.reshape(-1), meta_tokens.reshape(-1)])
    t_hi = small_f.astype(bf16)
    r_1 = small_f - t_hi.astype(f32)
    t_mid = r_1.astype(bf16)
    t_lo = (r_1 - t_mid.astype(f32)).astype(bf16)
    terms = jnp.concatenate([t_hi, t_mid, t_lo])
    spare = jnp.concatenate([terms, jnp.zeros((SPARE_ROWS * D - terms.shape[0],), bf16)]).reshape(SPARE_ROWS, D)
    shards = (w_up[0].astype(bf16), w_down[0].astype(bf16),
              jnp.concatenate([w_glu[0].T.astype(bf16), w_out[0].astype(bf16)], axis=0))
    in_rows = jnp.concatenate([w_in[0].T.astype(bf16), spare, jnp.zeros((1024 - 900 - SPARE_ROWS, D), bf16)], axis=0)
    my_half = lax.dynamic_slice_in_dim(in_rows, cc * 512, 512, axis=0)
    gathered = _allgather8(my_half, "gather_w_in").reshape(4, 1024, D)
    w_in_t = _perm_rows_w_in(jnp.concatenate([gathered[s, 0:900] for s in range(4)], axis=0))
    n_sf = SH_CONVW + SH_META
    tr = gathered[:, 900:900 + SPARE_ROWS].reshape(4, SPARE_ROWS * D)[:, :3 * n_sf].astype(f32).reshape(4, 3, n_sf)
    sp = tr[:, 0] + tr[:, 1] + tr[:, 2]
    conv_w_full = jnp.concatenate([sp[s, :SH_CONVW].reshape(4, 384) for s in range(4)], axis=1)
    meta_full = jnp.concatenate([sp[s, SH_CONVW:].reshape(16, 256) for s in range(4)], axis=1)

    p = dict(g_mix=g_mix, conv_w=conv_w_full, conv_b=conv_b, dt_bias=dt_bias, a_log=a_log, d_ssd=d_ssd, g_ssd=g_ssd,
             lam_re=lam_re[0], lam_im=lam_im[0], log_step=log_step[0], b_re=b_re[0], b_im=b_im[0], c_re=c_re[0], c_im=c_im[0],
             d_s5=d_s5[0], b_glu=b_glu, g_s5=g_s5, g_mlp=g_mlp, g_final=g_final.reshape(1, D))
    grad_x, pieces, all_early, late = _local_step(x[0], loss_target[0], meta_full, p, w_in_t, shards)
    grad_x = grad_x.reshape(x.shape)

    reds = []
    for k, (gp, land) in enumerate(pieces):
        half = gp.shape[1] // 2
        own = lax.dynamic_slice(gp, (chip, cc * half, 0), (1, half, D)).reshape(half, D)
        reds.append(_sum_parts("rs_sum_%d" % k, own, land))
    others = _swap_sibling(reds, "rs_share")
    lower = [jnp.where(cc == 0, r, o) for r, o in zip(reds, others)]
    upper = [jnp.where(cc == 0, o, r) for r, o in zip(reds, others)]
    g_up, g_down = lower[0], upper[0]
    g_glu, g_out = lower[1].T, upper[1]
    g_in_t = lax.dynamic_slice_in_dim(jnp.concatenate([lower[2], upper[2]], axis=0), 4 * chip, 900, axis=0)

    gs = dict(zip(EARLY, _unpack_small(_sum8(all_early, _rows_for(EARLY), "sum8_early"), [SMALL_SHAPES[n] for n in EARLY])))
    all_late = _allgather8(_pack_small([late[n] for n in LATE], _rows_for(LATE)), "gather_small")
    gs.update(zip(LATE, _unpack_small(_sum8(all_late, _rows_for(LATE), "sum8_late"), [SMALL_SHAPES[n] for n in LATE])))
    g_conv_w = lax.dynamic_slice_in_dim(gs.pop("conv_w"), chip * 384, 384, axis=1).reshape(conv_w.shape)
    g_meta = lax.dynamic_slice_in_dim(gs.pop("meta_tokens"), chip * 256, 256, axis=1)
    loss = gs.pop("loss").reshape(())

    grads = dict(gs, meta_tokens=g_meta, conv_w=g_conv_w, w_in=g_in_t.T.reshape(w_in.shape), w_glu=g_glu.reshape(w_glu.shape),
                 w_out=g_out.reshape(w_out.shape), w_up=g_up.reshape(w_up.shape), w_down=g_down.reshape(w_down.shape))
    delta, new_m, new_v = {}, {}, {}
    d_, m_, v_ = _adamw_whole("adamw_w_in", w_in[0].T, g_in_t, m_w_in[0].T, v_w_in[0].T)
    delta["w_in"], new_m["w_in"], new_v["w_in"] = (a.T.reshape(w_in.shape) for a in (d_, m_, v_))
    for n in ("w_glu", "w_out", "w_up", "w_down"):
        shp = given[n].shape
        two = lambda a: a.reshape(shp[1], shp[2])
        d_, m_, v_ = _adamw("adamw_" + n, two(given[n]), two(grads[n]), two(given["m_" + n]), two(given["v_" + n]), 256)
        delta[n], new_m[n], new_v[n] = d_.reshape(shp), m_.reshape(shp), v_.reshape(shp)
    for n in EARLY + LATE[:-1]:
        shp = given[n].shape
        if len(shp) == 4 and shp[-1] == 16:
            two = back = lambda a: jnp.swapaxes(a, -1, -2)
        else:
            two = (lambda a: a.reshape(1, -1)) if len(shp) == 1 else (lambda a: a)
            back = lambda a: a.reshape(shp)
        d_, m_, v_ = _adamw_whole("adamw_" + n, two(given[n]), two(grads[n].reshape(shp)), two(given["m_" + n]), two(given["v_" + n]))
        delta[n], new_m[n], new_v[n] = back(d_), back(m_), back(v_)

    order = ["meta_tokens", "g_mix", "w_in", "conv_w", "conv_b", "dt_bias", "a_log", "d_ssd", "g_ssd", "lam_re", "lam_im", "log_step",
             "b_re", "b_im", "c_re", "c_im", "d_s5", "w_glu", "b_glu", "g_s5", "w_out", "g_mlp", "w_up", "w_down", "g_final"]
    grads_out = [grads[n].reshape(given[n].shape) for n in order]
    return (loss, grad_x, *grads_out, *[delta[n] for n in order], *[new_m[n] for n in order], *[new_v[n] for n in order])
```

```python
import math

import jax
import jax.numpy as jnp
from jax import lax
from jax.experimental import pallas as pl
from jax.experimental.pallas import tpu as pltpu

f32 = jnp.float32
bf16 = jnp.bfloat16

D = 1024
N_META = 16
CH = 256
HEADS = 16
HEAD_DIM = 64
NSTATE = 128
D_XBC = 1536
S5_P = 64
NQ = 8
PITCH = CH + 4
EPS = 1e-5
O_Z, O_U, O_XBC, O_DT, W_PROJ = 0, 1024, 2048, 3584, 3712
VMEM_LIMIT = 60 * 1024 * 1024

ADAM_LR, ADAM_B1, ADAM_B2, ADAM_EPS, ADAM_WD, ADAM_STEP = 0.001, 0.9, 0.999, 1e-08, 0.01, 10

NT = (((1,), (1,)), ((), ()))
TN = (((0,), (0,)), ((), ()))
_ANY = pl.BlockSpec(memory_space=pl.ANY)


def _cp(sem=None):
    return pltpu.CompilerParams(dimension_semantics=sem, vmem_limit_bytes=VMEM_LIMIT)


def _sigmoid(v):
    return 1.0 / (1.0 + jnp.exp(-v))


def _rsum8(v):
    r, c = v.shape
    return jnp.sum(v.reshape(r // 8, 8, c), axis=0)


def _rms(h, g):
    r = lax.rsqrt(jnp.mean(h * h, axis=-1, keepdims=True) + EPS)
    return h * r * g


def _rms_bwd(dy, h, g):
    r = lax.rsqrt(jnp.mean(h * h, axis=-1, keepdims=True) + EPS)
    n = h * r
    dn = dy * g
    dh = r * (dn - n * jnp.mean(dn * n, axis=-1, keepdims=True))
    return dh, _rsum8(dy * n)


def _dot(a, b, dims=None):
    if dims is None:
        return jnp.dot(a, b, preferred_element_type=f32)
    return lax.dot_general(a, b, dims, preferred_element_type=f32)


def _split_dot(v, m01, dims, terms, v_is_lhs=True):
    out, r = None, v
    for _ in range(terms):
        piece = r.astype(bf16)
        o = _dot(piece, m01, dims) if v_is_lhs else _dot(m01, piece, dims)
        out = o if out is None else out + o
        r = r - piece.astype(f32)
    return out


MESH = pl.DeviceIdType.MESH


def _place():
    return lax.axis_index("x"), lax.axis_index("y"), lax.axis_index("c")


def _flip(v, f):
    return 1 - v if f else v


def _call(body, comm, *, name, nsteps, in_specs, out_specs, out_shape, scratch_shapes, args):
    n_in, n_out, n_scr = len(in_specs), len(out_specs), len(scratch_shapes)
    if comm is None:
        res = pl.pallas_call(body, name=name, grid=(nsteps,), in_specs=in_specs, out_specs=out_specs, out_shape=out_shape,
                             scratch_shapes=scratch_shapes, compiler_params=_cp(("arbitrary",)))(*args)
        return list(res), []
    c_in, c_out = len(comm["ins"]), len(comm["outs"])

    def wrapped(*refs):
        o0 = n_in + c_in
        s0 = o0 + n_out + c_out
        cparts = (refs[n_in:o0], refs[o0 + n_out:s0], refs[s0 + n_scr:])

        @pl.when(pl.program_id(0) == 0)
        def _():
            comm["start"](*cparts)
        if "middle" in comm:
            @pl.when(pl.program_id(0) == (3 * nsteps) // 4)
            def _():
                comm["middle"](*cparts)
        body(*refs[:n_in], *refs[o0:o0 + n_out], *refs[s0:s0 + n_scr])

        @pl.when(pl.program_id(0) == nsteps - 1)
        def _():
            comm["finish"](*cparts)

    any_spec = pl.BlockSpec(memory_space=pl.ANY)
    res = pl.pallas_call(
        wrapped, name=name, grid=(nsteps,), in_specs=list(in_specs) + [any_spec] * c_in,
        out_specs=list(out_specs) + [any_spec] * c_out, out_shape=list(out_shape) + list(comm["outs"]),
        scratch_shapes=list(scratch_shapes) + list(comm["scratch"]),
        compiler_params=_cp(("arbitrary",)))(*args, *comm["ins"])
    return list(res[:n_out]), list(res[n_out:])


def _gather_piece(slab):
    r0, rows = 0, slab.shape[0]
    half = rows // 2
    flips = ((1, 0), (0, 1), (1, 1))

    def first(j, slab_ref, out_ref, send_sems, recv_sems):
        x, y, c = _place()
        return pltpu.make_async_remote_copy(
            src_ref=slab_ref.at[pl.ds(r0 + c * half, half), :], dst_ref=out_ref.at[2 * x + y, pl.ds(c * half, half), :],
            send_sem=send_sems.at[j], recv_sem=recv_sems.at[j],
            device_id=(_flip(x, flips[j][0]), _flip(y, flips[j][1]), c), device_id_type=MESH)

    def passed(j, out_ref, send_sems, recv_sems):
        x, y, c = _place()
        rows_j = out_ref.at[2 * _flip(x, flips[j][0]) + _flip(y, flips[j][1]), pl.ds(c * half, half), :]
        return pltpu.make_async_remote_copy(src_ref=rows_j, dst_ref=rows_j, send_sem=send_sems.at[3 + j],
                                            recv_sem=recv_sems.at[3 + j], device_id=(x, y, 1 - c), device_id_type=MESH)

    def start(ins, outs, scr):
        send_sems, recv_sems, stage, local_sems = scr
        x, y, _ = _place()
        load = pltpu.make_async_copy(ins[0].at[pl.ds(r0, rows), :], stage, local_sems.at[0])
        load.start()
        for j in range(3):
            first(j, ins[0], outs[0], send_sems, recv_sems).start()
        load.wait()
        pltpu.make_async_copy(stage, outs[0].at[2 * x + y], local_sems.at[1]).start()

    def middle(ins, outs, scr):
        send_sems, recv_sems, _, _ = scr
        for j in range(3):
            first(j, ins[0], outs[0], send_sems, recv_sems).wait_recv()
            passed(j, outs[0], send_sems, recv_sems).start()

    def finish(ins, outs, scr):
        send_sems, recv_sems, stage, local_sems = scr
        x, y, c = _place()
        for j in range(3):
            sib = outs[0].at[2 * _flip(x, flips[j][0]) + _flip(y, flips[j][1]), pl.ds((1 - c) * half, half), :]
            pltpu.make_async_remote_copy(src_ref=sib, dst_ref=sib, send_sem=send_sems.at[3 + j], recv_sem=recv_sems.at[3 + j],
                                         device_id=(x, y, 1 - c), device_id_type=MESH).wait_recv()
        for j in range(3):
            first(j, ins[0], outs[0], send_sems, recv_sems).wait_send()
            passed(j, outs[0], send_sems, recv_sems).wait_send()
        pltpu.make_async_copy(stage, outs[0].at[2 * x + y], local_sems.at[1]).wait()

    return dict(ins=[slab], outs=[jax.ShapeDtypeStruct((4, rows, D), bf16)],
                scratch=[pltpu.SemaphoreType.DMA((6,)), pltpu.SemaphoreType.DMA((6,)), pltpu.VMEM((rows, D), bf16),
                         pltpu.SemaphoreType.DMA((2,))], start=start, middle=middle, finish=finish)


def _scatter_piece(gpiece):
    half = gpiece.shape[1] // 2

    def copies(g_ref, land_ref, send_sems, recv_sems):
        x, y, c = _place()
        cps = []
        for fx in (0, 1):
            for fy in (0, 1):
                for fc in (0, 1):
                    k = 4 * fx + 2 * fy + fc - 1
                    if k < 0:
                        continue
                    px, py, pc = _flip(x, fx), _flip(y, fy), _flip(c, fc)
                    cps.append(pltpu.make_async_remote_copy(
                        src_ref=g_ref.at[2 * px + py, pl.ds(pc * half, half), :], dst_ref=land_ref.at[k],
                        send_sem=send_sems.at[k], recv_sem=recv_sems.at[k], device_id=(px, py, pc), device_id_type=MESH))
        return cps

    def start(ins, outs, scr):
        for cp in copies(ins[0], outs[0], *scr):
            cp.start()

    def finish(ins, outs, scr):
        for cp in copies(ins[0], outs[0], *scr):
            cp.wait()

    return dict(ins=[gpiece], outs=[jax.ShapeDtypeStruct((7, half, D), gpiece.dtype)],
                scratch=[pltpu.SemaphoreType.DMA((7,)), pltpu.SemaphoreType.DMA((7,))], start=start, finish=finish)


def _gather_blocks(block):
    rows = block.shape[0]

    def mine(out_ref):
        x, y, c = _place()
        return out_ref.at[pl.ds((4 * x + 2 * y + c) * rows, rows), :]

    def copies(b_ref, out_ref, send_sems, recv_sems):
        x, y, c = _place()
        cps = []
        for fx in (0, 1):
            for fy in (0, 1):
                for fc in (0, 1):
                    k = 4 * fx + 2 * fy + fc - 1
                    if k < 0:
                        continue
                    cps.append(pltpu.make_async_remote_copy(
                        src_ref=b_ref, dst_ref=mine(out_ref), send_sem=send_sems.at[k], recv_sem=recv_sems.at[k],
                        device_id=(_flip(x, fx), _flip(y, fy), _flip(c, fc)), device_id_type=MESH))
        return cps

    def start(ins, outs, scr):
        send_sems, recv_sems, stage, local_sems = scr
        load = pltpu.make_async_copy(ins[0], stage, local_sems.at[0])
        load.start()
        for cp in copies(ins[0], outs[0], send_sems, recv_sems):
            cp.start()
        load.wait()
        pltpu.make_async_copy(stage, mine(outs[0]), local_sems.at[1]).start()

    def finish(ins, outs, scr):
        send_sems, recv_sems, stage, local_sems = scr
        for cp in copies(ins[0], outs[0], send_sems, recv_sems):
            cp.wait()
        pltpu.make_async_copy(stage, mine(outs[0]), local_sems.at[1]).wait()

    return dict(ins=[block], outs=[jax.ShapeDtypeStruct((8 * rows, D), block.dtype)],
                scratch=[pltpu.SemaphoreType.DMA((7,)), pltpu.SemaphoreType.DMA((7,)), pltpu.VMEM((rows, D), block.dtype),
                         pltpu.SemaphoreType.DMA((2,))], start=start, finish=finish)


def _both(c1, c2):
    n = (len(c1["ins"]), len(c1["outs"]), len(c1["scratch"]))

    def split(parts):
        return [p[:k] for p, k in zip(parts, n)], [p[k:] for p, k in zip(parts, n)]

    def start(*parts):
        a, b = split(parts)
        c1["start"](*a)
        c2["start"](*b)

    def finish(*parts):
        a, b = split(parts)
        c1["finish"](*a)
        c2["finish"](*b)

    both = dict(ins=c1["ins"] + c2["ins"], outs=c1["outs"] + c2["outs"], scratch=c1["scratch"] + c2["scratch"],
                start=start, finish=finish)
    if "middle" in c1 or "middle" in c2:
        def middle(*parts):
            for cm, part in zip((c1, c2), split(parts)):
                if "middle" in cm:
                    cm["middle"](*part)
        both["middle"] = middle
    return both


def _rowwise(name, fn, nblk, rows=(), shifted=(), pars=(), refs=(), out_rows=(), out_accs=(), bm=CH, comm=None):
    n_sub = bm // CH
    n_r, n_s, n_p, n_w = len(rows), len(shifted) * n_sub, len(pars), len(refs)
    n_in = n_r + n_s + n_p + n_w
    n_o, n_a = len(out_rows), len(out_accs)

    def body(*all_refs):
        i = pl.program_id(0)
        ins = all_refs[:n_in]
        outs = all_refs[n_in:]
        rv = [r[...] for r in ins[:n_r]]
        sub = ins[n_r:n_r + n_s]
        sv = [jnp.concatenate([r[...] for r in sub[k * n_sub:(k + 1) * n_sub]], axis=0) if n_sub > 1 else sub[k][...]
              for k in range(len(shifted))]
        pv = [r[...] for r in ins[n_r + n_s:n_r + n_s + n_p]]
        ro, ao = fn(i, rv, sv, pv, list(ins[n_r + n_s + n_p:]))
        for r, v in zip(outs[:n_o], ro):
            r[...] = v.astype(r.dtype)
        accs = outs[n_o:]

        @pl.when(i == 0)
        def _():
            for r in accs:
                r[...] = jnp.zeros_like(r)
        for r, v in zip(accs, ao):
            r[...] += v

    in_specs = [pl.BlockSpec((bm, a.shape[1]), lambda i: (i, 0)) for a in rows]
    in_specs += [pl.BlockSpec((CH, a.shape[1]), lambda i, j=j: (jnp.maximum(n_sub * i - 1 + j, 0), 0))
                 for a in shifted for j in range(n_sub)]
    in_specs += [pl.BlockSpec(a.shape, lambda i, nd=a.ndim: (0,) * nd) for a in pars]
    in_specs += [spec for _, spec in refs]
    drop = [len(o) > 2 for o in out_rows]
    assert not any(drop) or bm == CH
    out_specs = [pl.BlockSpec((bm, o[0]), (lambda i: (jnp.maximum(i - 1, 0), 0)) if d else (lambda i: (i, 0)))
                 for o, d in zip(out_rows, drop)]
    out_specs += [pl.BlockSpec((8, c), lambda i: (0, 0)) for c in out_accs]
    out_shape = [jax.ShapeDtypeStruct((nblk * bm - (CH if d else 0), o[0]), o[1]) for o, d in zip(out_rows, drop)]
    out_shape += [jax.ShapeDtypeStruct((8, c), f32) for c in out_accs]
    res, cres = _call(body, comm, name=name, nsteps=nblk, in_specs=in_specs, out_specs=out_specs, out_shape=out_shape,
                      scratch_shapes=[], args=[*rows, *[a for a in shifted for _ in range(n_sub)], *pars, *[a for a, _ in refs]])
    parts = (res[:n_o], res[n_o:])
    return parts if comm is None else parts + (cres,)


C_ROWS = 928
C_WINDOW = 912
BIG_ROWS = 768
DW_ROWS = 2816


def _contract_rows(lp, big=DW_ROWS):
    for rows in (big, BIG_ROWS):
        if lp % rows == 0:
            return rows
    return CH


def _dw_into(name, a, b, slab, ka, a_sharded, row_blk, n_s, s0, piece_rows=2048, a_square=False):
    lp = a.shape[0]
    bm = _contract_rows(lp)
    steps = lp // bm

    def body(a_ref, b_ref, *rest):
        o_ref, acc = rest[-2], rest[-1]
        k = pl.program_id(1)

        @pl.when(k == 0)
        def _():
            acc[...] = jnp.zeros_like(acc)
        a_v = a_ref[...]
        acc[...] += _dot(a_v * a_v if a_square else a_v, b_ref[...], TN)

        @pl.when(k == steps - 1)
        def _():
            o_ref[0] = acc[...].astype(bf16)

    in_specs = [pl.BlockSpec((bm, ka), (lambda s, k: (k, s)) if a_sharded else (lambda s, k: (k, 0))),
                pl.BlockSpec((bm, D), (lambda s, k: (k, 0)) if a_sharded else (lambda s, k: (k, s)))]
    args = [a, b]
    aliases = {}
    if slab is not None:
        in_specs.append(_ANY)
        args.append(slab)
        aliases = {2: 0}
    return pl.pallas_call(
        body, name=name, grid=(n_s, steps), in_specs=in_specs,
        out_specs=pl.BlockSpec((1, ka, D), lambda s, k: (s0 + s, row_blk, 0)),
        out_shape=jax.ShapeDtypeStruct((4, piece_rows, D), bf16),
        scratch_shapes=[pltpu.VMEM((ka, D), f32)], input_output_aliases=aliases,
        compiler_params=_cp(("arbitrary", "arbitrary")))(*args)


def _dw_in_t(dproj, n0):
    lp = n0.shape[0]
    bm = _contract_rows(lp, BIG_ROWS)
    steps = lp // bm
    bn = 512

    def body(a_ref, b_ref, o_ref, acc):
        k = pl.program_id(1)

        @pl.when(k == 0)
        def _():
            acc[...] = jnp.zeros_like(acc)
        acc[...] += _dot(a_ref[...], b_ref[...], TN)

        @pl.when(k == steps - 1)
        def _():
            o_ref[...] = acc[...].astype(bf16)

    return pl.pallas_call(
        body, name="dw_in", grid=(D // bn, steps),
        in_specs=[pl.BlockSpec((bm, W_PROJ), lambda j, k: (k, 0)), pl.BlockSpec((bm, bn), lambda j, k: (k, j))],
        out_specs=pl.BlockSpec((W_PROJ, bn), lambda j, k: (0, j)),
        out_shape=jax.ShapeDtypeStruct((W_PROJ, D), bf16),
        scratch_shapes=[pltpu.VMEM((W_PROJ, bn), f32)],
        compiler_params=_cp(("arbitrary", "arbitrary")))(dproj, n0)


def _head_expand():
    h = lax.broadcasted_iota(jnp.int32, (HEADS, D), 0)
    c = lax.broadcasted_iota(jnp.int32, (HEADS, D), 1)
    return jnp.where((c >> 6) == h, 1.0, 0.0).astype(bf16)


def _ssd_common(i, P, prev8, cw, cb, dtb, alog, xc=None):
    z = P[:, O_Z:O_Z + D]
    xp = P[:, O_XBC:O_XBC + D_XBC]
    dt_raw = P[:, O_DT:O_DT + HEADS]
    row = lax.broadcasted_iota(jnp.int32, (CH, 1), 0)
    if xc is None:
        row8 = lax.broadcasted_iota(jnp.int32, (8, 1), 0)
        xc = cb + cw[3:4] * xp
        for k in (1, 2, 3):
            rolled = pltpu.roll(xp, k, 0)
            fix = pltpu.roll(prev8, k, 0)
            top = jnp.where(row8 < k, fix, rolled[0:8])
            xc = xc + cw[3 - k:4 - k] * jnp.concatenate([top, rolled[8:]], axis=0)
    sg = _sigmoid(xc)
    xbc = xc * sg
    live = jnp.where(jnp.logical_or(i > 0, row >= CH - N_META), 1.0, 0.0)
    pre = dt_raw + dtb
    dt = jnp.where(pre > 20.0, pre, jnp.log(1.0 + jnp.exp(jnp.minimum(pre, 20.0)))) * live
    a = -jnp.exp(alog)
    dta = dt * a
    r_i = lax.broadcasted_iota(jnp.int32, (CH, CH), 0)
    c_i = lax.broadcasted_iota(jnp.int32, (CH, CH), 1)
    tril = r_i >= c_i
    acs = _split_dot(dta, jnp.where(tril, 1.0, 0.0).astype(bf16), None, 3, v_is_lhs=False)
    acs_t = _split_dot(dta, jnp.where(r_i <= c_i, 1.0, 0.0).astype(bf16), TN, 3)
    e = _head_expand()
    acs_e = _split_dot(acs, e, None, 3)
    dt_e = _split_dot(dt, e, None, 3)
    return dict(z=z, xp=xp, xc=xc, sg=sg, xbc=xbc, live=live, pre=pre, dt=dt, a=a, tril=tril,
                acs=acs, acs_t=acs_t, e=e, acs_e=acs_e, dt_e=dt_e)


def _lmat(c, h):
    seg = c["acs"][:, h:h + 1] - c["acs_t"][h:h + 1, :]
    return jnp.where(c["tril"], jnp.exp(jnp.minimum(seg, 0.0)), 0.0)


def _pair_masks():
    lane = lax.broadcasted_iota(jnp.int32, (1, 128), 1)
    return jnp.where(lane < HEAD_DIM, 1.0, 0.0), jnp.where(lane >= HEAD_DIM, 1.0, 0.0)


def _ssd_fwd(proj, conv_w, conv_b, dt_bias, a_log, d_ssd, g_ssd, nch, comm=None):
    def body(p_ref, cw_ref, cb_ref, dtb_ref, al_ref, d_ref, g_ref, y_ref, ys_ref, st_ref, xc_ref, prev8_ref, state_ref):
        i = pl.program_id(0)

        @pl.when(i == 0)
        def _():
            prev8_ref[...] = jnp.zeros_like(prev8_ref)
            state_ref[...] = jnp.zeros_like(state_ref)

        P = p_ref[...]
        c = _ssd_common(i, P, prev8_ref[...], cw_ref[...], cb_ref[...], dtb_ref[...], al_ref[...])
        prev8_ref[...] = c["xp"][CH - 8:CH]
        xc_ref[...] = c["xc"]
        xbc = c["xbc"]
        x = xbc[:, 0:D]
        xdt = x * c["dt_e"]
        a_last_e = c["acs_e"][CH - 1:CH, :]
        w_end = (xdt * jnp.exp(a_last_e - c["acs_e"])).astype(bf16)
        m0, m1 = _pair_masks()
        ys = []
        for g in range(2):
            bg = xbc[:, D + NSTATE * g:D + NSTATE * (g + 1)].astype(bf16)
            cg = xbc[:, D + 2 * NSTATE + NSTATE * g:D + 2 * NSTATE + NSTATE * (g + 1)].astype(bf16)
            gmat = _dot(cg, bg, NT)
            st = state_ref[g]
            st_ref[0, g] = st
            sl = slice(512 * g, 512 * (g + 1))
            y_off = _dot(cg, st.astype(bf16)) * jnp.exp(c["acs_e"][:, sl])
            contrib = _dot(bg, w_end[:, sl], TN)
            state_ref[g] = st * jnp.exp(a_last_e[:, sl]) + contrib
            yd = []
            for pr in range(4):
                h0 = 8 * g + 2 * pr
                xp2 = xdt[:, 128 * (4 * g + pr):128 * (4 * g + pr + 1)]
                ma = (gmat * _lmat(c, h0)).astype(bf16)
                mb = (gmat * _lmat(c, h0 + 1)).astype(bf16)
                yd.append(_dot(ma, (xp2 * m0).astype(bf16)) + _dot(mb, (xp2 * m1).astype(bf16)))
            ys.append(jnp.concatenate(yd, axis=1) + y_off)
        d_e = _split_dot(d_ref[...], c["e"], None, 3)
        y = jnp.concatenate(ys, axis=1) + x * d_e
        y_ref[...] = y
        yg = y * (c["z"] * _sigmoid(c["z"]))
        ys_ref[...] = _rms(yg, g_ref[...]).astype(bf16)

    full = lambda a: pl.BlockSpec(a.shape, lambda i, nd=a.ndim: (0,) * nd)
    return _call(
        body, comm, name="ssd_fwd", nsteps=nch,
        in_specs=[pl.BlockSpec((CH, W_PROJ), lambda i: (i, 0))] + [full(a) for a in (conv_w, conv_b, dt_bias, a_log, d_ssd, g_ssd)],
        out_specs=[pl.BlockSpec((CH, D), lambda i: (i, 0)), pl.BlockSpec((CH, D), lambda i: (i, 0)),
                   pl.BlockSpec((1, 2, NSTATE, 512), lambda i: (i, 0, 0, 0)), pl.BlockSpec((CH, D_XBC), lambda i: (i, 0))],
        out_shape=[jax.ShapeDtypeStruct((nch * CH, D), f32), jax.ShapeDtypeStruct((nch * CH, D), bf16),
                   jax.ShapeDtypeStruct((nch, 2, NSTATE, 512), f32), jax.ShapeDtypeStruct((nch * CH, D_XBC), f32)],
        scratch_shapes=[pltpu.VMEM((8, D_XBC), f32), pltpu.VMEM((2, NSTATE, 512), f32)],
        args=[proj, conv_w, conv_b, dt_bias, a_log, d_ssd, g_ssd])


def _ssd_bwd(proj, xc_all, y, dys, du, states, conv_w, conv_b, dt_bias, a_log, d_ssd, g_ssd, nch, comm=None):
    def body(p_ref, xc_ref, y_ref, dys_ref, du_ref, st_ref, cw_ref, cb_ref, dtb_ref, al_ref, d_ref, g_ref,
             dp_ref, dcw_ref, dcb_ref, ddtb_ref, dal_ref, dd_ref, dg_ref, nxt8_ref, dst_ref):
        step = pl.program_id(0)
        i = nch - 1 - step

        @pl.when(step == 0)
        def _():
            nxt8_ref[...] = jnp.zeros_like(nxt8_ref)
            dst_ref[...] = jnp.zeros_like(dst_ref)
            for r in (dcw_ref, dcb_ref, ddtb_ref, dal_ref, dd_ref, dg_ref):
                r[...] = jnp.zeros_like(r)

        P = p_ref[...]
        c = _ssd_common(i, P, None, cw_ref[...], cb_ref[...], dtb_ref[...], al_ref[...], xc=xc_ref[...])
        xbc, z, e = c["xbc"], c["z"], c["e"]
        x = xbc[:, 0:D]
        yv = y_ref[...]
        sz = _sigmoid(z)
        silu_z = z * sz
        dyg, dg8 = _rms_bwd(dys_ref[...], yv * silu_z, g_ref[...])
        dg_ref[...] += dg8
        dy = dyg * silu_z
        dz = dyg * yv * (sz * (1.0 + z * (1.0 - sz)))
        d_e = _split_dot(d_ref[...], e, None, 3)
        dd_ref[...] += _rsum8(dy * x)
        xdt = x * c["dt_e"]
        a_last_e = c["acs_e"][CH - 1:CH, :]
        e_end = jnp.exp(a_last_e - c["acs_e"])
        w_end = xdt * e_end
        e_acs = jnp.exp(c["acs_e"])
        dy_dec = dy * e_acs
        m0, m1 = _pair_masks()
        lane16 = lax.broadcasted_iota(jnp.int32, (1, HEADS), 1)
        row16 = lax.broadcasted_iota(jnp.int32, (HEADS, 1), 0)
        dacs = jnp.zeros((CH, HEADS), f32)
        dacs_t = jnp.zeros((HEADS, CH), f32)
        dxdt_parts, dbs, dcs, zparts, yoff_parts, dlast_parts = [], [], [], [], [], []
        for g in range(2):
            sl = slice(512 * g, 512 * (g + 1))
            bg = xbc[:, D + NSTATE * g:D + NSTATE * (g + 1)].astype(bf16)
            cg = xbc[:, D + 2 * NSTATE + NSTATE * g:D + 2 * NSTATE + NSTATE * (g + 1)].astype(bf16)
            gmat = _dot(cg, bg, NT)
            st = st_ref[0, g]
            dstn = dst_ref[g]
            dstn_b = dstn.astype(bf16)
            y_off = _dot(cg, st.astype(bf16)) * e_acs[:, sl]
            yoff_parts.append(y_off)
            bds = _dot(bg, dstn_b)
            zparts.append(w_end[:, sl] * bds)
            dlast_parts.append(jnp.sum(dstn * st, axis=0, keepdims=True) * jnp.exp(a_last_e[:, sl]))
            dg_acc = jnp.zeros((CH, CH), f32)
            dxd = []
            for pr in range(4):
                lo = 128 * (4 * g + pr)
                xp2 = xdt[:, lo:lo + 128].astype(bf16)
                dy2 = dy[:, lo:lo + 128]
                outp = jnp.zeros((CH, 128), f32)
                for hh, msk in ((0, m0), (1, m1)):
                    h = 8 * g + 2 * pr + hh
                    lm = _lmat(c, h)
                    dyh = (dy2 * msk).astype(bf16)
                    mh = (gmat * lm).astype(bf16)
                    outp = outp + _dot(mh, dyh, TN)
                    dml = _dot(dyh, xp2, NT) * lm
                    dg_acc = dg_acc + dml
                    q = dml * gmat
                    dacs = dacs + jnp.where(lane16 == h, jnp.sum(q, axis=1, keepdims=True), 0.0)
                    dacs_t = dacs_t + jnp.where(row16 == h, jnp.sum(q, axis=0, keepdims=True), 0.0)
                dxd.append(outp)
            dxdt_parts.append(jnp.concatenate(dxd, axis=1) + e_end[:, sl] * bds)
            dgb = dg_acc.astype(bf16)
            dcs.append(_dot(dgb, bg) + _dot(dy_dec[:, sl].astype(bf16), st.astype(bf16), NT))
            dbs.append(_dot(dgb, cg, TN) + _dot(w_end[:, sl].astype(bf16), dstn_b, NT))
            dst_ref[g] = dstn * jnp.exp(a_last_e[:, sl]) + _dot(cg, dy_dec[:, sl].astype(bf16), TN)
        dxdt = jnp.concatenate(dxdt_parts, axis=1)
        zfull = jnp.concatenate(zparts, axis=1)
        y_off_full = jnp.concatenate(yoff_parts, axis=1)
        dlast = jnp.concatenate(dlast_parts, axis=1)
        red = lambda v: _split_dot(v, e, NT, 2)
        eye16 = jnp.where(lax.broadcasted_iota(jnp.int32, (HEADS, HEADS), 0) == lax.broadcasted_iota(jnp.int32, (HEADS, HEADS), 1),
                          1.0, 0.0).astype(bf16)
        dacs = dacs - _split_dot(dacs_t, eye16, TN, 3)
        zred = red(zfull)
        dacs = dacs + red(dy * y_off_full) - zred
        last_term = jnp.sum(zred, axis=0, keepdims=True) + red(dlast)
        rowc = lax.broadcasted_iota(jnp.int32, (CH, 1), 0)
        dacs = dacs + jnp.where(rowc == CH - 1, last_term, 0.0)
        r_i = lax.broadcasted_iota(jnp.int32, (CH, CH), 0)
        c_i = lax.broadcasted_iota(jnp.int32, (CH, CH), 1)
        ddta = _split_dot(dacs, jnp.where(c_i >= r_i, 1.0, 0.0).astype(bf16), None, 3, v_is_lhs=False)
        ddt = ddta * c["a"] + red(dxdt * x)
        dal_ref[...] += _rsum8(ddta * c["dt"] * c["a"])
        ddt_raw = ddt * _sigmoid(c["pre"]) * c["live"]
        ddtb_ref[...] += _rsum8(ddt_raw)
        dx = dy * d_e + dxdt * c["dt_e"]
        dxbc = jnp.concatenate([dx, dbs[0], dbs[1], dcs[0], dcs[1]], axis=1)
        sg = c["sg"]
        dxc = dxbc * (sg * (1.0 + c["xc"] * (1.0 - sg)))
        dcb_ref[...] += _rsum8(dxc)
        xp = c["xp"]
        row8 = lax.broadcasted_iota(jnp.int32, (8, 1), 0)
        cw = cw_ref[...]
        dxp = cw[3:4] * dxc
        dcw = jnp.where(row8 == 3, jnp.sum(dxc * xp, axis=0, keepdims=True), 0.0)
        nxt8 = nxt8_ref[...]
        for j in (1, 2, 3):
            rolled = pltpu.roll(dxc, CH - j, 0)
            fix = pltpu.roll(nxt8, 8 - j, 0)
            bot = jnp.where(row8 >= 8 - j, fix, rolled[CH - 8:CH])
            later = jnp.concatenate([rolled[:CH - 8], bot], axis=0)
            dxp = dxp + cw[3 - j:4 - j] * later
            dcw = dcw + jnp.where(row8 == 3 - j, jnp.sum(later * xp, axis=0, keepdims=True), 0.0)
        dcw_ref[...] += dcw
        nxt8_ref[...] = dxc[0:8]
        dp_ref[:, O_Z:O_Z + D] = dz.astype(bf16)
        dp_ref[:, O_U:O_U + D] = du_ref[...].astype(bf16)
        dp_ref[:, O_XBC:O_XBC + D_XBC] = dxp.astype(bf16)
        dp_ref[:, O_DT:W_PROJ] = jnp.zeros((CH, W_PROJ - O_DT), bf16)
        dp_ref[:, O_DT:O_DT + HEADS] = ddt_raw.astype(bf16)

    full = lambda a: pl.BlockSpec(a.shape, lambda s, nd=a.ndim: (0,) * nd)
    rev = lambda s: (nch - 1 - s, 0)
    acc = lambda cdim: pl.BlockSpec((8, cdim), lambda s: (0, 0))
    return _call(
        body, comm, name="ssd_bwd", nsteps=nch,
        in_specs=[pl.BlockSpec((CH, W_PROJ), rev), pl.BlockSpec((CH, D_XBC), rev),
                  pl.BlockSpec((CH, D), rev), pl.BlockSpec((CH, D), rev), pl.BlockSpec((CH, D), rev),
                  pl.BlockSpec((1, 2, NSTATE, 512), lambda s: (nch - 1 - s, 0, 0, 0))]
        + [full(a) for a in (conv_w, conv_b, dt_bias, a_log, d_ssd, g_ssd)],
        out_specs=[pl.BlockSpec((CH, W_PROJ), rev), acc(D_XBC), acc(D_XBC), acc(HEADS), acc(HEADS), acc(D), acc(D)],
        out_shape=[jax.ShapeDtypeStruct((nch * CH, W_PROJ), bf16)]
        + [jax.ShapeDtypeStruct((8, cdim), f32) for cdim in (D_XBC, D_XBC, HEADS, HEADS, D, D)],
        scratch_shapes=[pltpu.VMEM((8, D_XBC), f32), pltpu.VMEM((2, NSTATE, 512), f32)],
        args=[proj, xc_all, y, dys, du, states, conv_w, conv_b, dt_bias, a_log, d_ssd, g_ssd])


SCAN_UNROLL = 8


def _to_slabs(slab_ref, q, mat):
    for ls in range(8):
        slab_ref[ls, pl.ds(PITCH * q, CH), :] = mat[:, 128 * ls:128 * (ls + 1)]


def _from_slabs(slab, q):
    return jnp.concatenate([slab(ls, PITCH * q) for ls in range(8)], axis=1)


def _tile(slab_ref, ls, t, lead=None):
    idx = (ls, pl.ds(t, 8, stride=PITCH), slice(None))
    return slab_ref[idx] if lead is None else slab_ref[(lead,) + idx]


def _s5_fwd(proj, bbq, ccq, ar, ai, d_skip, nch, comm=None):
    def body(u_ref, bb_ref, cc_ref, ar_ref, ai_ref, d_ref, s_ref, yl_ref, y5_ref, bu_ref, st_ref):
        @pl.when(pl.program_id(0) == 0)
        def _():
            st_ref[...] = jnp.zeros_like(st_ref)
        u = u_ref[...]
        ub = u.astype(bf16)
        for q in range(NQ):
            _to_slabs(bu_ref, q, _dot(ub[:, 128 * q:128 * (q + 1)], bb_ref[q]))
        ar_t = [ar_ref[:, 128 * l:128 * (l + 1)] for l in range(4)]
        ai_t = [ai_ref[:, 128 * l:128 * (l + 1)] for l in range(4)]

        def one(t, carry):
            re, im = carry
            nre, nim = [], []
            for l in range(4):
                a = ar_t[l] * re[l] - ai_t[l] * im[l] + _tile(bu_ref, l, t)
                b = ar_t[l] * im[l] + ai_t[l] * re[l] + _tile(bu_ref, l + 4, t)
                s_ref[0, l, pl.ds(t, 8, stride=PITCH), :] = a
                s_ref[0, l + 4, pl.ds(t, 8, stride=PITCH), :] = b
                nre.append(a)
                nim.append(b)
            return tuple(nre), tuple(nim)

        def step(tt, carry):
            for k in range(SCAN_UNROLL):
                carry = one(tt * SCAN_UNROLL + k, carry)
            return carry
        init = (tuple(st_ref[l] for l in range(4)), tuple(st_ref[l + 4] for l in range(4)))
        re, im = lax.fori_loop(0, CH // SCAN_UNROLL, step, init)
        for l in range(4):
            st_ref[l] = re[l]
            st_ref[l + 4] = im[l]
        ys = []
        for q in range(NQ):
            sq = _from_slabs(lambda ls, r0: s_ref[0, ls, pl.ds(r0, CH), :], q).astype(bf16)
            ys.append(_dot(sq, cc_ref[q], NT))
        yl = jnp.concatenate(ys, axis=1) + u * d_ref[...]
        yl_ref[...] = yl
        y5_ref[...] = (0.5 * yl * (1.0 + lax.erf(yl * (1.0 / math.sqrt(2.0))))).astype(bf16)

    const = lambda a: pl.BlockSpec(a.shape, lambda i, nd=a.ndim: (0,) * nd)
    return _call(
        body, comm, name="s5_fwd", nsteps=nch,
        in_specs=[pl.BlockSpec((CH, D), lambda i: (i, O_U // D)), const(bbq), const(ccq), const(ar), const(ai), const(d_skip)],
        out_specs=[pl.BlockSpec((1, 8, 8 * PITCH, 128), lambda i: (i, 0, 0, 0)),
                   pl.BlockSpec((CH, D), lambda i: (i, 0)), pl.BlockSpec((CH, D), lambda i: (i, 0))],
        out_shape=[jax.ShapeDtypeStruct((nch, 8, 8 * PITCH, 128), f32), jax.ShapeDtypeStruct((nch * CH, D), f32),
                   jax.ShapeDtypeStruct((nch * CH, D), bf16)],
        scratch_shapes=[pltpu.VMEM((8, 8 * PITCH, 128), f32), pltpu.VMEM((8, 8, 128), f32)],
        args=[proj, bbq, ccq, ar, ai, d_skip])


def _s5_bwd(proj, dyl, s_all, bbq, ccq, ar, ai, d_skip, nch, comm=None):
    def body(u_ref, dy_ref, s_ref, bbt_ref, cct_ref, ar_ref, ai_ref, d_ref,
             du_ref, dcc_ref, dbb_ref, dab_ref, dd_ref, ga_ref, st_ref):
        @pl.when(pl.program_id(0) == 0)
        def _():
            st_ref[...] = jnp.zeros_like(st_ref)
            for r in (dcc_ref, dbb_ref, dab_ref, dd_ref):
                r[...] = jnp.zeros_like(r)
        u = u_ref[...]
        dyl_v = dy_ref[...]
        dd_ref[...] += _rsum8(dyl_v * u)
        ub = u.astype(bf16)
        dyb = dyl_v.astype(bf16)
        for q in range(NQ):
            _to_slabs(ga_ref, q, _dot(dyb[:, 128 * q:128 * (q + 1)], cct_ref[q]))
        ar_t = [ar_ref[:, 128 * l:128 * (l + 1)] for l in range(4)]
        ai_t = [ai_ref[:, 128 * l:128 * (l + 1)] for l in range(4)]

        def one(t, carry):
            re, im, dar, dai = carry
            nre, nim, ndar, ndai = [], [], [], []
            for l in range(4):
                sre = _tile(s_ref, l, t, lead=0)
                sim = _tile(s_ref, l + 4, t, lead=0)
                ndar.append(dar[l] + re[l] * sre + im[l] * sim)
                ndai.append(dai[l] + im[l] * sre - re[l] * sim)
                a = _tile(ga_ref, l, t) + ar_t[l] * re[l] + ai_t[l] * im[l]
                b = _tile(ga_ref, l + 4, t) - ai_t[l] * re[l] + ar_t[l] * im[l]
                ga_ref[l, pl.ds(t, 8, stride=PITCH), :] = a
                ga_ref[l + 4, pl.ds(t, 8, stride=PITCH), :] = b
                nre.append(a)
                nim.append(b)
            return tuple(nre), tuple(nim), tuple(ndar), tuple(ndai)

        def step(tt, carry):
            for k in range(SCAN_UNROLL):
                carry = one(CH - 1 - (tt * SCAN_UNROLL + k), carry)
            return carry
        four = lambda ref, o: tuple(ref[l + o] for l in range(4))
        re, im, dar, dai = lax.fori_loop(0, CH // SCAN_UNROLL, step,
                                         (four(st_ref, 0), four(st_ref, 4), four(dab_ref, 0), four(dab_ref, 4)))
        for l in range(4):
            st_ref[l], st_ref[l + 4] = re[l], im[l]
            dab_ref[l], dab_ref[l + 4] = dar[l], dai[l]
        dus = []
        for q in range(NQ):
            aq = _from_slabs(lambda ls, r0: ga_ref[ls, pl.ds(r0, CH), :], q).astype(bf16)
            sq = _from_slabs(lambda ls, r0: s_ref[0, ls, pl.ds(r0, CH), :], q).astype(bf16)
            dcc_ref[q] += _dot(dyb[:, 128 * q:128 * (q + 1)], sq, TN)
            dbb_ref[q] += _dot(ub[:, 128 * q:128 * (q + 1)], aq, TN)
            dus.append(_dot(aq, bbt_ref[q], NT))
        du_ref[...] = jnp.concatenate(dus, axis=1) + dyl_v * d_ref[...]

    const = lambda a: pl.BlockSpec(a.shape, lambda s, nd=a.ndim: (0,) * nd)
    rev = lambda s: (nch - 1 - s, 0)
    return _call(
        body, comm, name="s5_bwd", nsteps=nch,
        in_specs=[pl.BlockSpec((CH, D), lambda s: (nch - 1 - s, O_U // D)), pl.BlockSpec((CH, D), rev),
                  pl.BlockSpec((1, 8, 8 * PITCH, 128), lambda s: (nch - 1 - s, 0, 0, 0)),
                  const(bbq), const(ccq), const(ar), const(ai), const(d_skip)],
        out_specs=[pl.BlockSpec((CH, D), rev), pl.BlockSpec((NQ, 128, D), lambda s: (0, 0, 0)),
                   pl.BlockSpec((NQ, 128, D), lambda s: (0, 0, 0)), pl.BlockSpec((8, 8, 128), lambda s: (0, 0, 0)),
                   pl.BlockSpec((8, D), lambda s: (0, 0))],
        out_shape=[jax.ShapeDtypeStruct((nch * CH, D), f32), jax.ShapeDtypeStruct((NQ, 128, D), f32),
                   jax.ShapeDtypeStruct((NQ, 128, D), f32), jax.ShapeDtypeStruct((8, 8, 128), f32),
                   jax.ShapeDtypeStruct((8, D), f32)],
        scratch_shapes=[pltpu.VMEM((8, 8 * PITCH, 128), f32), pltpu.VMEM((8, 8, 128), f32)],
        args=[proj, dyl, s_all, bbq, ccq, ar, ai, d_skip])


def _s5_tables(lam_re, lam_im, log_step, b_re, b_im):
    step = jnp.exp(log_step)[:, None]
    mag = jnp.exp(lam_re * step)
    ab_re = mag * jnp.cos(lam_im * step)
    ab_im = mag * jnp.sin(lam_im * step)
    den = lam_re * lam_re + lam_im * lam_im
    coef_re = ((ab_re - 1.0) * lam_re + ab_im * lam_im) / den
    coef_im = (ab_im * lam_re - (ab_re - 1.0) * lam_im) / den
    bb_re = coef_re[..., None] * b_re - coef_im[..., None] * b_im
    bb_im = coef_re[..., None] * b_im + coef_im[..., None] * b_re
    return ab_re, ab_im, bb_re, bb_im


def _blockdiag_in(m_re, m_im):
    eye = jnp.eye(8, dtype=f32)

    def one(m):
        m = m.reshape(NQ, 8, S5_P, 16)
        return jnp.einsum("qgph,gk->qghkp", m, eye).reshape(NQ, 128, 512)
    return jnp.concatenate([one(m_re), one(m_im)], axis=2)


def _blockdiag_in_grad(dm):
    def one(off):
        blocks = jnp.stack([dm[:, 16 * g:16 * (g + 1), off + S5_P * g:off + S5_P * (g + 1)] for g in range(8)], axis=1)
        return jnp.swapaxes(blocks, 2, 3).reshape(NQ * 8, S5_P, 16)
    return one(0), one(512)


def _local_step(x2, tgt2, meta, p, w_in_t, shards):
    seq = x2.shape[0]
    nch = 1 + seq // CH
    bmb = BIG_ROWS if (nch * CH) % BIG_ROWS == 0 else CH
    nbig = nch * CH // bmb
    metablk = jnp.concatenate([jnp.zeros((CH - N_META, D), f32), meta, jnp.zeros((bmb - CH, D), f32)], axis=0)
    w_full = (w_in_t, pl.BlockSpec(w_in_t.shape, lambda i: (0, 0), pipeline_mode=pl.Buffered(1)))

    def lead(i, v):
        return jnp.logical_and(i == 0, lax.broadcasted_iota(jnp.int32, (v.shape[0], 1), 0) < CH)
    h0_of = lambda i, s, q: jnp.where(lead(i, s[0]), q[0][:s[0].shape[0]], s[0])

    def in_fn(i, r, s, q, w):
        nb = _rms(h0_of(i, s, q), q[1]).astype(bf16)
        return [_dot(nb, w[0][...], NT), nb], []
    (proj, n0), _, (g_up,) = _rowwise("in_proj", in_fn, nbig, shifted=[x2], pars=[metablk, p["g_mix"]], refs=[w_full],
                                      out_rows=[(W_PROJ, f32), (D, bf16)], bm=bmb, comm=_gather_piece(shards[0]))
    (y, y_ssd, states, xc_all), (g_down,) = _ssd_fwd(proj, p["conv_w"], p["conv_b"], p["dt_bias"], p["a_log"], p["d_ssd"],
                                                     p["g_ssd"], nch, comm=_gather_piece(shards[1]))

    ab_re, ab_im, bb_re, bb_im = _s5_tables(p["lam_re"], p["lam_im"], p["log_step"], p["b_re"], p["b_im"])
    ar, ai = ab_re.reshape(NQ, 512), ab_im.reshape(NQ, 512)
    bbq = _blockdiag_in(bb_re, bb_im)
    ccq = _blockdiag_in(jnp.swapaxes(p["c_re"], 1, 2), -jnp.swapaxes(p["c_im"], 1, 2))
    d_skip = p["d_s5"].reshape(1, D)
    bbq_b, ccq_b = bbq.astype(bf16), ccq.astype(bf16)
    (s_all, ylin, y5), (g_go,) = _s5_fwd(proj, bbq_b, ccq_b, ar, ai, d_skip, nch, comm=_gather_piece(shards[2]))
    whole = lambda a: (a, pl.BlockSpec(a.shape, lambda i: (0, 0, 0), pipeline_mode=pl.Buffered(1)))
    w_up, w_down = whole(g_up), whole(g_down)
    w_glu_t = (g_go, pl.BlockSpec((4, 512, D), lambda i: (0, 0, 0), pipeline_mode=pl.Buffered(1)))
    w_out = (g_go, pl.BlockSpec((4, 512, D), lambda i: (0, 1, 0), pipeline_mode=pl.Buffered(1)))

    def glu_fn(i, r, s, q, w):
        v = jnp.concatenate([_dot(r[0], w[0][k], NT) for k in range(4)], axis=1) + q[0]
        return [v, _rms(v[:, :D] * _sigmoid(v[:, D:]), q[1])], []
    (v, y_s5), _ = _rowwise("glu", glu_fn, nbig, rows=[y5], pars=[p["b_glu"], p["g_s5"]], refs=[w_glu_t],
                            out_rows=[(2 * D, bf16), (D, bf16)], bm=bmb)

    def out_fn(i, r, s, q, w):
        acc = (_dot(r[0][:, :512], w[0][0]) + _dot(r[0][:, 512:], w[0][1]) + _dot(r[1][:, :512], w[0][2])
               + _dot(r[1][:, 512:], w[0][3]))
        return [h0_of(i, s, q) + acc], []
    (h1,), _ = _rowwise("out_proj", out_fn, nbig, rows=[y_ssd, y_s5], shifted=[x2], pars=[metablk], refs=[w_out],
                        out_rows=[(D, f32)], bm=bmb)

    def up_fn(i, r, s, q, w):
        nb = _rms(r[0], q[0]).astype(bf16)
        return [jnp.concatenate([jnp.maximum(_dot(nb, w[0][k]), 0.0).astype(bf16) for k in range(4)], axis=1), nb], []
    (relu_m, n1), _ = _rowwise("up_proj", up_fn, nbig, rows=[h1], pars=[p["g_mlp"]], refs=[w_up],
                               out_rows=[(4 * D, bf16), (D, bf16)], bm=bmb)

    def down_fn(i, r, s, q, w):
        acc = None
        for k in range(4):
            t = r[0][:, D * k:D * (k + 1)]
            part = _dot(t * t, w[0][k])
            acc = part if acc is None else acc + part
        return [r[1] + acc], []
    (h2,), _ = _rowwise("down_proj", down_fn, nbig, rows=[relu_m, h1], refs=[w_down], out_rows=[(D, f32)], bm=bmb)

    def final_fn(i, r, s, q, w):
        err = jnp.where(lead(i, r[0]), 0.0, _rms(r[0], q[0]) - s[0])
        dh, dg8 = _rms_bwd(err * (1.0 / D), r[0], q[0])
        return [dh, dh], [_rsum8(err * err), dg8]
    (dh2, dh2_b), (loss8, dgf8) = _rowwise("final", final_fn, nbig, rows=[h2], shifted=[tgt2], pars=[p["g_final"]],
                                           out_rows=[(D, f32), (D, bf16)], out_accs=[D, D], bm=bmb)
    loss = 0.5 / D * jnp.sum(loss8)

    def down_bwd_fn(i, r, s, q, w):
        dm_ = [_dot(r[0], w[0][k], NT) * (2.0 * r[1][:, D * k:D * (k + 1)].astype(f32)) for k in range(4)]
        return [jnp.concatenate(dm_, axis=1)], []
    (dm,), _ = _rowwise("down_bwd", down_bwd_fn, nbig, rows=[dh2_b, relu_m], refs=[w_down], out_rows=[(4 * D, bf16)], bm=bmb)
    g_a = _dw_into("dw_down", relu_m, dh2_b, None, 1024, True, 1, 4, 0, piece_rows=2048, a_square=True)

    def up_bwd_fn(i, r, s, q, w):
        acc = _dot(r[0][:, :D], w[0][0], NT)
        for k in range(1, 4):
            acc = acc + _dot(r[0][:, D * k:D * (k + 1)], w[0][k], NT)
        dh, dg8 = _rms_bwd(acc, r[1], q[0])
        dh1_ = r[2] + dh
        return [dh1_, dh1_], [dg8]
    (dh1, dh1_b), (dgmlp8,) = _rowwise("up_bwd", up_bwd_fn, nbig, rows=[dm, h1, dh2], pars=[p["g_mlp"]], refs=[w_up],
                                       out_rows=[(D, f32), (D, bf16)], out_accs=[D], bm=bmb)
    g_a = _dw_into("dw_up", n1, dm, g_a, 1024, False, 0, 4, 0, piece_rows=2048)

    def out_bwd_fn(i, r, s, q, w):
        dmix = [_dot(r[0], w[0][k], NT) for k in range(4)]
        v1, v2 = r[1][:, :D].astype(f32), r[1][:, D:].astype(f32)
        s2 = _sigmoid(v2)
        dglu, dg8 = _rms_bwd(jnp.concatenate(dmix[2:], axis=1), v1 * s2, q[0])
        dv = jnp.concatenate([dglu * s2, dglu * v1 * s2 * (1.0 - s2)], axis=1)
        return [jnp.concatenate(dmix[:2], axis=1), dv], [dg8, _rsum8(dv)]
    (dys, dv), (dgs58, dbglu8) = _rowwise("out_bwd", out_bwd_fn, nbig, rows=[dh1_b, v], pars=[p["g_s5"]], refs=[w_out],
                                          out_rows=[(D, f32), (2 * D, bf16)], out_accs=[D, 2 * D], bm=bmb)
    g_b = _dw_into("dw_out_a", y_ssd, dh1_b, None, 512, True, 1, 2, 0, piece_rows=1024)
    g_b = _dw_into("dw_out_b", y_s5, dh1_b, g_b, 512, True, 1, 2, 2, piece_rows=1024)

    def glu_bwd_fn(i, r, s, q, w):
        acc = _dot(r[0][:, :512], w[0][0])
        for k in range(1, 4):
            acc = acc + _dot(r[0][:, 512 * k:512 * (k + 1)], w[0][k])
        yl = r[1]
        cdf = 0.5 * (1.0 + lax.erf(yl * (1.0 / math.sqrt(2.0))))
        pdf = jnp.exp(-0.5 * yl * yl) * (1.0 / math.sqrt(2.0 * math.pi))
        return [acc * (cdf + yl * pdf)], []
    (dylin,), _ = _rowwise("glu_bwd", glu_bwd_fn, nbig, rows=[dv, ylin], refs=[w_glu_t], out_rows=[(D, f32)], bm=bmb)
    g_b = _dw_into("dw_glu", dv, y5, g_b, 512, True, 0, 4, 0, piece_rows=1024)

    (du, dcc, dbb, dab, dds5), (land_a,) = _s5_bwd(proj, dylin, s_all, bbq_b, ccq_b, ar, ai, d_skip, nch,
                                                   comm=_scatter_piece(g_a))

    s8 = lambda a: jnp.sum(a, axis=0, keepdims=True)
    dab_q = jnp.swapaxes(dab.reshape(2, 4, NQ, 128), 1, 2).reshape(2, NQ * 8, S5_P)
    dbb_re, dbb_im = _blockdiag_in_grad(dbb)
    dcr, dci = _blockdiag_in_grad(dcc)
    _, vjp = jax.vjp(_s5_tables, p["lam_re"], p["lam_im"], p["log_step"], p["b_re"], p["b_im"])
    dlam_re, dlam_im, dlog_step, db_re, db_im = vjp((dab_q[0], dab_q[1], dbb_re, dbb_im))
    early = dict(lam_re=dlam_re, lam_im=dlam_im, log_step=dlog_step, b_re=db_re, b_im=db_im, c_re=jnp.swapaxes(dcr, 1, 2),
                 c_im=-jnp.swapaxes(dci, 1, 2), d_s5=s8(dds5).reshape(NQ * 8, 16), b_glu=s8(dbglu8), g_s5=s8(dgs58),
                 g_mlp=s8(dgmlp8), g_final=s8(dgf8).reshape(D))
    early_pack = _pack_small([early[n] for n in EARLY], _rows_for(EARLY))

    (dproj, dcw8, dcb8, ddtb8, dal8, dd8, dgssd8), (land_b, all_early) = _ssd_bwd(
        proj, xc_all, y, dys, du, states, p["conv_w"], p["conv_b"], p["dt_bias"], p["a_log"], p["d_ssd"], p["g_ssd"], nch,
        comm=_both(_scatter_piece(g_b), _gather_blocks(early_pack)))

    gt = _dw_in_t(dproj, n0)
    windows = [[(0, 912)], [(896, 1024), (O_XBC, O_XBC + 784)], [(O_XBC + 768, O_DT + HEADS), (O_U, O_U + 128)],
               [(O_U + 112, O_U + D)]]
    assert all(sum(b - a for a, b in w) == C_WINDOW for w in windows)
    g_c = jnp.concatenate([part for w in windows for part in
                           [gt[a:b] for a, b in w] + [jnp.zeros((C_ROWS - C_WINDOW, D), bf16)]], axis=0).reshape(4, C_ROWS, D)

    def in_bwd_fn(i, r, s, q, w):
        dh, dg8 = _rms_bwd(_dot(r[0], w[0][...]), h0_of(i, s, q), q[1])
        dh0 = r[1] + dh
        dmeta = jnp.where(i == 0, dh0[CH - N_META:], 0.0)
        return [dh0], [dg8, dmeta[:8], dmeta[8:]]
    (grad_x,), (dgmix8, dmeta_a, dmeta_b), (land_c,) = _rowwise(
        "in_bwd", in_bwd_fn, nch, rows=[dproj, dh1], shifted=[x2], pars=[metablk, p["g_mix"]], refs=[w_full],
        out_rows=[(D, f32, "shifted")], out_accs=[D, D, D], bm=CH, comm=_scatter_piece(g_c))

    hsum = lambda a: jnp.sum(s8(a).reshape(HEADS, HEAD_DIM), axis=1).reshape(1, HEADS)
    late = dict(g_mix=s8(dgmix8), conv_b=s8(dcb8), dt_bias=s8(ddtb8), a_log=s8(dal8), d_ssd=hsum(dd8), g_ssd=s8(dgssd8),
                conv_w=dcw8[0:4], meta_tokens=jnp.concatenate([dmeta_a, dmeta_b], axis=0), loss=loss.reshape(1))
    return grad_x, [(g_a, land_a), (g_b, land_b), (g_c, land_c)], all_early, late


def _perm_rows_w_in(wt):
    return jnp.concatenate([wt[0:1024], wt[2576:3600], wt[1024:2560], wt[2560:2576],
                            jnp.zeros((W_PROJ - 3600, wt.shape[1]), wt.dtype)], axis=0)


def _allgather8(x_shard, name):
    m_per, n = x_shard.shape

    def body(x_ref, out_ref, send_sems, recv_sems, stage, local_sems):
        x, y, c = _place()
        me, sibling = (x, y, c), (x, y, 1 - c)
        chips = [(1 - x, y), (x, 1 - y), (1 - x, 1 - y)]

        def rows(px, py, pc):
            return out_ref.at[pl.ds((4 * px + 2 * py + pc) * m_per, m_per), :]

        def copy(k, block, to, src=None):
            return pltpu.make_async_remote_copy(
                src_ref=rows(*block) if src is None else src, dst_ref=rows(*block),
                send_sem=send_sems.at[k], recv_sem=recv_sems.at[k], device_id=to, device_id_type=MESH)

        load = pltpu.make_async_copy(x_ref, stage, local_sems.at[0])
        load.start()
        first = [copy(0, me, sibling, src=x_ref)]
        first += [copy(1 + j, me, (*chip, c), src=x_ref) for j, chip in enumerate(chips)]
        for cp in first:
            cp.start()
        load.wait()
        store = pltpu.make_async_copy(stage, rows(*me), local_sems.at[1])
        store.start()
        passed = [copy(4 + j, (*chip, c), sibling) for j, chip in enumerate(chips)]
        for j, chip in enumerate(chips):
            copy(1 + j, (*chip, c), me).wait_recv()
            passed[j].start()
        copy(0, sibling, me).wait_recv()
        for j, chip in enumerate(chips):
            copy(4 + j, (*chip, 1 - c), me).wait_recv()
        for cp in first + passed:
            cp.wait_send()
        store.wait()

    return pl.pallas_call(
        body, name=name, out_shape=jax.ShapeDtypeStruct((8 * m_per, n), x_shard.dtype),
        in_specs=[_ANY], out_specs=_ANY,
        scratch_shapes=[pltpu.SemaphoreType.DMA((7,)), pltpu.SemaphoreType.DMA((7,)), pltpu.VMEM((m_per, n), x_shard.dtype),
                        pltpu.SemaphoreType.DMA((2,))])(x_shard)


def _swap_sibling(parts, name):
    n = len(parts)

    def body(*refs):
        send_sems, recv_sems = refs[2 * n:]
        x, y, c = _place()
        cps = [pltpu.make_async_remote_copy(src_ref=refs[k], dst_ref=refs[n + k], send_sem=send_sems.at[k],
                                            recv_sem=recv_sems.at[k], device_id=(x, y, 1 - c), device_id_type=MESH)
               for k in range(n)]
        for cp in cps:
            cp.start()
        for cp in cps:
            cp.wait()

    return pl.pallas_call(
        body, name=name, out_shape=[jax.ShapeDtypeStruct(r.shape, r.dtype) for r in parts], in_specs=[_ANY] * n,
        out_specs=[_ANY] * n, scratch_shapes=[pltpu.SemaphoreType.DMA((n,)), pltpu.SemaphoreType.DMA((n,))])(*parts)


SH_CONVW, SH_META = 4 * 384, 16 * 256
SPARE_ROWS = 17

SMALL_SHAPES = dict(
    g_mix=(1, 1024), conv_b=(1, 1536), dt_bias=(1, 16), a_log=(1, 16), d_ssd=(1, 16), g_ssd=(1, 1024), lam_re=(1, 64, 64),
    lam_im=(1, 64, 64), log_step=(1, 64), b_re=(1, 64, 64, 16), b_im=(1, 64, 64, 16), c_re=(1, 64, 16, 64), c_im=(1, 64, 16, 64),
    d_s5=(1, 64, 16), b_glu=(1, 2048), g_s5=(1, 1024), g_mlp=(1, 1024), g_final=(1024,),
    conv_w=(4, D_XBC), meta_tokens=(N_META, D), loss=(1,))
EARLY = ["lam_re", "lam_im", "log_step", "b_re", "b_im", "c_re", "c_im", "d_s5", "b_glu", "g_s5", "g_mlp", "g_final"]
LATE = ["g_mix", "conv_b", "dt_bias", "a_log", "d_ssd", "g_ssd", "conv_w", "meta_tokens", "loss"]


def _rows_for(names):
    return -(-sum(math.prod(SMALL_SHAPES[n]) for n in names) // (8 * D)) * 8


def _pack_small(arrs, rows):
    flat = jnp.concatenate([a.reshape(-1).astype(f32) for a in arrs])
    return jnp.concatenate([flat, jnp.zeros((rows * D - flat.shape[0],), f32)]).reshape(rows, D)


def _unpack_small(slab, shapes):
    flat = slab.reshape(-1)
    out, o = [], 0
    for shp in shapes:
        n = math.prod(shp)
        out.append(flat[o:o + n].reshape(shp))
        o += n
    return out


def _sum8(g, rows, name):
    def body(g_ref, o_ref):
        acc = g_ref[0]
        for k in range(1, 8):
            acc = acc + g_ref[k]
        o_ref[...] = acc
    return pl.pallas_call(body, name=name, out_shape=jax.ShapeDtypeStruct((rows, D), f32),
                          compiler_params=_cp())(g.reshape(8, rows, D))


def _adam_math(w_, g_, m_, v_):
    m2 = ADAM_B1 * m_ + (1.0 - ADAM_B1) * g_
    v2 = ADAM_B2 * v_ + (1.0 - ADAM_B2) * jnp.square(g_)
    m_hat = m2 / (1.0 - ADAM_B1 ** ADAM_STEP)
    v_hat = v2 / (1.0 - ADAM_B2 ** ADAM_STEP)
    delta = -ADAM_LR * (m_hat / (jnp.sqrt(v_hat) + ADAM_EPS) + ADAM_WD * w_)
    return delta, m2, v2


def _adamw(name, w, g, m, v, bm):
    def fn(i, r, s, q, refs):
        return list(_adam_math(*r)), []
    c = w.shape[1]
    (d, m2, v2), _ = _rowwise(name, fn, w.shape[0] // bm, rows=[w, g, m, v], out_rows=[(c, f32)] * 3, bm=bm)
    return d, m2, v2


def _adamw_whole(name, w, g, m, v):
    def body(w_ref, g_ref, m_ref, v_ref, d_ref, m2_ref, v2_ref):
        d_ref[...], m2_ref[...], v2_ref[...] = _adam_math(w_ref[...], g_ref[...], m_ref[...], v_ref[...])
    return pl.pallas_call(body, name=name, out_shape=[jax.ShapeDtypeStruct(w.shape, f32)] * 3, compiler_params=_cp())(w, g, m, v)


def _sum_parts(name, own, land):
    def fn(i, r, s, q, refs):
        acc = r[0].astype(f32)
        for k in range(7):
            acc = acc + refs[0][k].astype(f32)
        return [acc], []
    rows = own.shape[0]
    bm = CH if rows % CH == 0 else rows
    (o,), _ = _rowwise(name, fn, rows // bm, rows=[own], refs=[(land, pl.BlockSpec((7, bm, D), lambda i: (0, i, 0)))],
                       out_rows=[(D, f32)], bm=bm)
    return o


def kernel(x, meta_tokens, g_mix, w_in, conv_w, conv_b, dt_bias, a_log, d_ssd, g_ssd, lam_re, lam_im, log_step, b_re, b_im, c_re, c_im, d_s5, w_glu, b_glu, g_s5, w_out, g_mlp, w_up, w_down, g_final, loss_target, m_meta_tokens, m_g_mix, m_w_in, m_conv_w, m_conv_b, m_dt_bias, m_a_log, m_d_ssd, m_g_ssd, m_lam_re, m_lam_im, m_log_step, m_b_re, m_b_im, m_c_re, m_c_im, m_d_s5, m_w_glu, m_b_glu, m_g_s5, m_w_out, m_g_mlp, m_w_up, m_w_down, m_g_final, v_meta_tokens, v_g_mix, v_w_in, v_conv_w, v_conv_b, v_dt_bias, v_a_log, v_d_ssd, v_g_ssd, v_lam_re, v_lam_im, v_log_step, v_b_re, v_b_im, v_c_re, v_c_im, v_d_s5, v_w_glu, v_b_glu, v_g_s5, v_w_out, v_g_mlp, v_w_up, v_w_down, v_g_final):
    given = dict(locals())
    cx, cy, cc = _place()
    chip = 2 * cx + cy

    small_f = jnp.concatenate([conv_w.reshape(-1), meta_tokens.reshape(-1)])
    t_hi = small_f.astype(bf16)
    r_1 = small_f - t_hi.astype(f32)
    t_mid = r_1.astype(bf16)
    t_lo = (r_1 - t_mid.astype(f32)).astype(bf16)
    terms = jnp.concatenate([t_hi, t_mid, t_lo])
    spare = jnp.concatenate([terms, jnp.zeros((SPARE_ROWS * D - terms.shape[0],), bf16)]).reshape(SPARE_ROWS, D)
    shards = (w_up[0].astype(bf16), w_down[0].astype(bf16),
              jnp.concatenate([w_glu[0].T.astype(bf16), w_out[0].astype(bf16)], axis=0))
    in_rows = jnp.concatenate([w_in[0].T.astype(bf16), spare, jnp.zeros((1024 - 900 - SPARE_ROWS, D), bf16)], axis=0)
    my_half = lax.dynamic_slice_in_dim(in_rows, cc * 512, 512, axis=0)
    gathered = _allgather8(my_half, "gather_w_in").reshape(4, 1024, D)
    w_in_t = _perm_rows_w_in(jnp.concatenate([gathered[s, 0:900] for s in range(4)], axis=0))
    n_sf = SH_CONVW + SH_META
    tr = gathered[:, 900:900 + SPARE_ROWS].reshape(4, SPARE_ROWS * D)[:, :3 * n_sf].astype(f32).reshape(4, 3, n_sf)
    sp = tr[:, 0] + tr[:, 1] + tr[:, 2]
    conv_w_full = jnp.concatenate([sp[s, :SH_CONVW].reshape(4, 384) for s in range(4)], axis=1)
    meta_full = jnp.concatenate([sp[s, SH_CONVW:].reshape(16, 256) for s in range(4)], axis=1)

    p = dict(g_mix=g_mix, conv_w=conv_w_full, conv_b=conv_b, dt_bias=dt_bias, a_log=a_log, d_ssd=d_ssd, g_ssd=g_ssd,
             lam_re=lam_re[0], lam_im=lam_im[0], log_step=log_step[0], b_re=b_re[0], b_im=b_im[0], c_re=c_re[0], c_im=c_im[0],
             d_s5=d_s5[0], b_glu=b_glu, g_s5=g_s5, g_mlp=g_mlp, g_final=g_final.reshape(1, D))
    grad_x, pieces, all_early, late = _local_step(x[0], loss_target[0], meta_full, p, w_in_t, shards)
    grad_x = grad_x.reshape(x.shape)

    reds = []
    for k, (gp, land) in enumerate(pieces):
        half = gp.shape[1] // 2
        own = lax.dynamic_slice(gp, (chip, cc * half, 0), (1, half, D)).reshape(half, D)
        reds.append(_sum_parts("rs_sum_%d" % k, own, land))
    others = _swap_sibling(reds, "rs_share")
    lower = [jnp.where(cc == 0, r, o) for r, o in zip(reds, others)]
    upper = [jnp.where(cc == 0, o, r) for r, o in zip(reds, others)]
    g_up, g_down = lower[0], upper[0]
    g_glu, g_out = lower[1].T, upper[1]
    g_in_t = lax.dynamic_slice_in_dim(jnp.concatenate([lower[2], upper[2]], axis=0), 4 * chip, 900, axis=0)

    gs = dict(zip(EARLY, _unpack_small(_sum8(all_early, _rows_for(EARLY), "sum8_early"), [SMALL_SHAPES[n] for n in EARLY])))
    all_late = _allgather8(_pack_small([late[n] for n in LATE], _rows_for(LATE)), "gather_small")
    gs.update(zip(LATE, _unpack_small(_sum8(all_late, _rows_for(LATE), "sum8_late"), [SMALL_SHAPES[n] for n in LATE])))
    g_conv_w = lax.dynamic_slice_in_dim(gs.pop("conv_w"), chip * 384, 384, axis=1).reshape(conv_w.shape)
    g_meta = lax.dynamic_slice_in_dim(gs.pop("meta_tokens"), chip * 256, 256, axis=1)
    loss = gs.pop("loss").reshape(())

    grads = dict(gs, meta_tokens=g_meta, conv_w=g_conv_w, w_in=g_in_t.T.reshape(w_in.shape), w_glu=g_glu.reshape(w_glu.shape),
                 w_out=g_out.reshape(w_out.shape), w_up=g_up.reshape(w_up.shape), w_down=g_down.reshape(w_down.shape))
    delta, new_m, new_v = {}, {}, {}
    d_, m_, v_ = _adamw_whole("adamw_w_in", w_in[0].T, g_in_t, m_w_in[0].T, v_w_in[0].T)
    delta["w_in"], new_m["w_in"], new_v["w_in"] = (a.T.reshape(w_in.shape) for a in (d_, m_, v_))
    for n in ("w_glu", "w_out", "w_up", "w_down"):
        shp = given[n].shape
        two = lambda a: a.reshape(shp[1], shp[2])
        d_, m_, v_ = _adamw("adamw_" + n, two(given[n]), two(grads[n]), two(given["m_" + n]), two(given["v_" + n]), 256)
        delta[n], new_m[n], new_v[n] = d_.reshape(shp), m_.reshape(shp), v_.reshape(shp)
    for n in EARLY + LATE[:-1]:
        shp = given[n].shape
        if len(shp) == 4 and shp[-1] == 16:
            two = back = lambda a: jnp.swapaxes(a, -1, -2)
        else:
            two = (lambda a: a.reshape(1, -1)) if len(shp) == 1 else (lambda a: a)
            back = lambda a: a.reshape(shp)
        d_, m_, v_ = _adamw_whole("adamw_" + n, two(given[n]), two(grads[n].reshape(shp)), two(given["m_" + n]), two(given["v_" + n]))
        delta[n], new_m[n], new_v[n] = back(d_), back(m_), back(v_)

    order = ["meta_tokens", "g_mix", "w_in", "conv_w", "conv_b", "dt_bias", "a_log", "d_ssd", "g_ssd", "lam_re", "lam_im", "log_step",
             "b_re", "b_im", "c_re", "c_im", "d_s5", "w_glu", "b_glu", "g_s5", "w_out", "g_mlp", "w_up", "w_down", "g_final"]
    grads_out = [grads[n].reshape(given[n].shape) for n in order]
    return (loss, grad_x, *grads_out, *[delta[n] for n in order], *[new_m[n] for n in order], *[new_v[n] for n in order])
```

```python
import math

import jax
import jax.numpy as jnp
from jax import lax
from jax.experimental import pallas as pl
from jax.experimental.pallas import tpu as pltpu

f32 = jnp.float32
bf16 = jnp.bfloat16

D = 1024
N_META = 16
CH = 256
HEADS = 16
HEAD_DIM = 64
NSTATE = 128
D_XBC = 1536
S5_P = 64
NQ = 8
PITCH = CH + 4
EPS = 1e-5
O_Z, O_U, O_XBC, O_DT, W_PROJ = 0, 1024, 2048, 3584, 3712
VMEM_LIMIT = 60 * 1024 * 1024

ADAM_LR, ADAM_B1, ADAM_B2, ADAM_EPS, ADAM_WD, ADAM_STEP = 0.001, 0.9, 0.999, 1e-08, 0.01, 10

NT = (((1,), (1,)), ((), ()))
TN = (((0,), (0,)), ((), ()))
_ANY = pl.BlockSpec(memory_space=pl.ANY)


def _cp(sem=None):
    return pltpu.CompilerParams(dimension_semantics=sem, vmem_limit_bytes=VMEM_LIMIT)


def _sigmoid(v):
    return 1.0 / (1.0 + jnp.exp(-v))


def _rsum8(v):
    r, c = v.shape
    return jnp.sum(v.reshape(r // 8, 8, c), axis=0)


def _rms(h, g):
    r = lax.rsqrt(jnp.mean(h * h, axis=-1, keepdims=True) + EPS)
    return h * r * g


def _rms_bwd(dy, h, g):
    r = lax.rsqrt(jnp.mean(h * h, axis=-1, keepdims=True) + EPS)
    n = h * r
    dn = dy * g
    dh = r * (dn - n * jnp.mean(dn * n, axis=-1, keepdims=True))
    return dh, _rsum8(dy * n)


def _dot(a, b, dims=None):
    if dims is None:
        return jnp.dot(a, b, preferred_element_type=f32)
    return lax.dot_general(a, b, dims, preferred_element_type=f32)


def _split_dot(v, m01, dims, terms, v_is_lhs=True):
    out, r = None, v
    for _ in range(terms):
        piece = r.astype(bf16)
        o = _dot(piece, m01, dims) if v_is_lhs else _dot(m01, piece, dims)
        out = o if out is None else out + o
        r = r - piece.astype(f32)
    return out


MESH = pl.DeviceIdType.MESH


def _place():
    return lax.axis_index("x"), lax.axis_index("y"), lax.axis_index("c")


def _flip(v, f):
    return 1 - v if f else v


def _call(body, comm, *, name, nsteps, in_specs, out_specs, out_shape, scratch_shapes, args):
    n_in, n_out, n_scr = len(in_specs), len(out_specs), len(scratch_shapes)
    if comm is None:
        res = pl.pallas_call(body, name=name, grid=(nsteps,), in_specs=in_specs, out_specs=out_specs, out_shape=out_shape,
                             scratch_shapes=scratch_shapes, compiler_params=_cp(("arbitrary",)))(*args)
        return list(res), []
    c_in, c_out = len(comm["ins"]), len(comm["outs"])

    def wrapped(*refs):
        o0 = n_in + c_in
        s0 = o0 + n_out + c_out
        cparts = (refs[n_in:o0], refs[o0 + n_out:s0], refs[s0 + n_scr:])

        @pl.when(pl.program_id(0) == 0)
        def _():
            comm["start"](*cparts)
        if "middle" in comm:
            @pl.when(pl.program_id(0) == (3 * nsteps) // 4)
            def _():
                comm["middle"](*cparts)
        body(*refs[:n_in], *refs[o0:o0 + n_out], *refs[s0:s0 + n_scr])

        @pl.when(pl.program_id(0) == nsteps - 1)
        def _():
            comm["finish"](*cparts)

    any_spec = pl.BlockSpec(memory_space=pl.ANY)
    res = pl.pallas_call(
        wrapped, name=name, grid=(nsteps,), in_specs=list(in_specs) + [any_spec] * c_in,
        out_specs=list(out_specs) + [any_spec] * c_out, out_shape=list(out_shape) + list(comm["outs"]),
        scratch_shapes=list(scratch_shapes) + list(comm["scratch"]),
        compiler_params=_cp(("arbitrary",)))(*args, *comm["ins"])
    return list(res[:n_out]), list(res[n_out:])


def _gather_piece(slab):
    r0, rows = 0, slab.shape[0]
    half = rows // 2
    flips = ((1, 0), (0, 1), (1, 1))

    def first(j, slab_ref, out_ref, send_sems, recv_sems):
        x, y, c = _place()
        return pltpu.make_async_remote_copy(
            src_ref=slab_ref.at[pl.ds(r0 + c * half, half), :], dst_ref=out_ref.at[2 * x + y, pl.ds(c * half, half), :],
            send_sem=send_sems.at[j], recv_sem=recv_sems.at[j],
            device_id=(_flip(x, flips[j][0]), _flip(y, flips[j][1]), c), device_id_type=MESH)

    def passed(j, out_ref, send_sems, recv_sems):
        x, y, c = _place()
        rows_j = out_ref.at[2 * _flip(x, flips[j][0]) + _flip(y, flips[j][1]), pl.ds(c * half, half), :]
        return pltpu.make_async_remote_copy(src_ref=rows_j, dst_ref=rows_j, send_sem=send_sems.at[3 + j],
                                            recv_sem=recv_sems.at[3 + j], device_id=(x, y, 1 - c), device_id_type=MESH)

    def start(ins, outs, scr):
        send_sems, recv_sems, stage, local_sems = scr
        x, y, _ = _place()
        load = pltpu.make_async_copy(ins[0].at[pl.ds(r0, rows), :], stage, local_sems.at[0])
        load.start()
        for j in range(3):
            first(j, ins[0], outs[0], send_sems, recv_sems).start()
        load.wait()
        pltpu.make_async_copy(stage, outs[0].at[2 * x + y], local_sems.at[1]).start()

    def middle(ins, outs, scr):
        send_sems, recv_sems, _, _ = scr
        for j in range(3):
            first(j, ins[0], outs[0], send_sems, recv_sems).wait_recv()
            passed(j, outs[0], send_sems, recv_sems).start()

    def finish(ins, outs, scr):
        send_sems, recv_sems, stage, local_sems = scr
        x, y, c = _place()
        for j in range(3):
            sib = outs[0].at[2 * _flip(x, flips[j][0]) + _flip(y, flips[j][1]), pl.ds((1 - c) * half, half), :]
            pltpu.make_async_remote_copy(src_ref=sib, dst_ref=sib, send_sem=send_sems.at[3 + j], recv_sem=recv_sems.at[3 + j],
                                         device_id=(x, y, 1 - c), device_id_type=MESH).wait_recv()
        for j in range(3):
            first(j, ins[0], outs[0], send_sems, recv_sems).wait_send()
            passed(j, outs[0], send_sems, recv_sems).wait_send()
        pltpu.make_async_copy(stage, outs[0].at[2 * x + y], local_sems.at[1]).wait()

    return dict(ins=[slab], outs=[jax.ShapeDtypeStruct((4, rows, D), bf16)],
                scratch=[pltpu.SemaphoreType.DMA((6,)), pltpu.SemaphoreType.DMA((6,)), pltpu.VMEM((rows, D), bf16),
                         pltpu.SemaphoreType.DMA((2,))], start=start, middle=middle, finish=finish)


def _scatter_piece(gpiece):
    half = gpiece.shape[1] // 2

    def copies(g_ref, land_ref, send_sems, recv_sems):
        x, y, c = _place()
        cps = []
        for fx in (0, 1):
            for fy in (0, 1):
                for fc in (0, 1):
                    k = 4 * fx + 2 * fy + fc - 1
                    if k < 0:
                        continue
                    px, py, pc = _flip(x, fx), _flip(y, fy), _flip(c, fc)
                    cps.append(pltpu.make_async_remote_copy(
                        src_ref=g_ref.at[2 * px + py, pl.ds(pc * half, half), :], dst_ref=land_ref.at[k],
                        send_sem=send_sems.at[k], recv_sem=recv_sems.at[k], device_id=(px, py, pc), device_id_type=MESH))
        return cps

    def start(ins, outs, scr):
        for cp in copies(ins[0], outs[0], *scr):
            cp.start()

    def finish(ins, outs, scr):
        for cp in copies(ins[0], outs[0], *scr):
            cp.wait()

    return dict(ins=[gpiece], outs=[jax.ShapeDtypeStruct((7, half, D), gpiece.dtype)],
                scratch=[pltpu.SemaphoreType.DMA((7,)), pltpu.SemaphoreType.DMA((7,))], start=start, finish=finish)


def _gather_blocks(block):
    rows = block.shape[0]

    def mine(out_ref):
        x, y, c = _place()
        return out_ref.at[pl.ds((4 * x + 2 * y + c) * rows, rows), :]

    def copies(b_ref, out_ref, send_sems, recv_sems):
        x, y, c = _place()
        cps = []
        for fx in (0, 1):
            for fy in (0, 1):
                for fc in (0, 1):
                    k = 4 * fx + 2 * fy + fc - 1
                    if k < 0:
                        continue
                    cps.append(pltpu.make_async_remote_copy(
                        src_ref=b_ref, dst_ref=mine(out_ref), send_sem=send_sems.at[k], recv_sem=recv_sems.at[k],
                        device_id=(_flip(x, fx), _flip(y, fy), _flip(c, fc)), device_id_type=MESH))
        return cps

    def start(ins, outs, scr):
        send_sems, recv_sems, stage, local_sems = scr
        load = pltpu.make_async_copy(ins[0], stage, local_sems.at[0])
        load.start()
        for cp in copies(ins[0], outs[0], send_sems, recv_sems):
            cp.start()
        load.wait()
        pltpu.make_async_copy(stage, mine(outs[0]), local_sems.at[1]).start()

    def finish(ins, outs, scr):
        send_sems, recv_sems, stage, local_sems = scr
        for cp in copies(ins[0], outs[0], send_sems, recv_sems):
            cp.wait()
        pltpu.make_async_copy(stage, mine(outs[0]), local_sems.at[1]).wait()

    return dict(ins=[block], outs=[jax.ShapeDtypeStruct((8 * rows, D), block.dtype)],
                scratch=[pltpu.SemaphoreType.DMA((7,)), pltpu.SemaphoreType.DMA((7,)), pltpu.VMEM((rows, D), block.dtype),
                         pltpu.SemaphoreType.DMA((2,))], start=start, finish=finish)


def _both(c1, c2):
    n = (len(c1["ins"]), len(c1["outs"]), len(c1["scratch"]))

    def split(parts):
        return [p[:k] for p, k in zip(parts, n)], [p[k:] for p, k in zip(parts, n)]

    def start(*parts):
        a, b = split(parts)
        c1["start"](*a)
        c2["start"](*b)

    def finish(*parts):
        a, b = split(parts)
        c1["finish"](*a)
        c2["finish"](*b)

    both = dict(ins=c1["ins"] + c2["ins"], outs=c1["outs"] + c2["outs"], scratch=c1["scratch"] + c2["scratch"],
                start=start, finish=finish)
    if "middle" in c1 or "middle" in c2:
        def middle(*parts):
            for cm, part in zip((c1, c2), split(parts)):
                if "middle" in cm:
                    cm["middle"](*part)
        both["middle"] = middle
    return both


def _rowwise(name, fn, nblk, rows=(), shifted=(), pars=(), refs=(), out_rows=(), out_accs=(), bm=CH, comm=None):
    n_sub = bm // CH
    n_r, n_s, n_p, n_w = len(rows), len(shifted) * n_sub, len(pars), len(refs)
    n_in = n_r + n_s + n_p + n_w
    n_o, n_a = len(out_rows), len(out_accs)

    def body(*all_refs):
        i = pl.program_id(0)
        ins = all_refs[:n_in]
        outs = all_refs[n_in:]
        rv = [r[...] for r in ins[:n_r]]
        sub = ins[n_r:n_r + n_s]
        sv = [jnp.concatenate([r[...] for r in sub[k * n_sub:(k + 1) * n_sub]], axis=0) if n_sub > 1 else sub[k][...]
              for k in range(len(shifted))]
        pv = [r[...] for r in ins[n_r + n_s:n_r + n_s + n_p]]
        ro, ao = fn(i, rv, sv, pv, list(ins[n_r + n_s + n_p:]))
        for r, v in zip(outs[:n_o], ro):
            r[...] = v.astype(r.dtype)
        accs = outs[n_o:]

        @pl.when(i == 0)
        def _():
            for r in accs:
                r[...] = jnp.zeros_like(r)
        for r, v in zip(accs, ao):
            r[...] += v

    in_specs = [pl.BlockSpec((bm, a.shape[1]), lambda i: (i, 0)) for a in rows]
    in_specs += [pl.BlockSpec((CH, a.shape[1]), lambda i, j=j: (jnp.maximum(n_sub * i - 1 + j, 0), 0))
                 for a in shifted for j in range(n_sub)]
    in_specs += [pl.BlockSpec(a.shape, lambda i, nd=a.ndim: (0,) * nd) for a in pars]
    in_specs += [spec for _, spec in refs]
    drop = [len(o) > 2 for o in out_rows]
    assert not any(drop) or bm == CH
    out_specs = [pl.BlockSpec((bm, o[0]), (lambda i: (jnp.maximum(i - 1, 0), 0)) if d else (lambda i: (i, 0)))
                 for o, d in zip(out_rows, drop)]
    out_specs += [pl.BlockSpec((8, c), lambda i: (0, 0)) for c in out_accs]
    out_shape = [jax.ShapeDtypeStruct((nblk * bm - (CH if d else 0), o[0]), o[1]) for o, d in zip(out_rows, drop)]
    out_shape += [jax.ShapeDtypeStruct((8, c), f32) for c in out_accs]
    res, cres = _call(body, comm, name=name, nsteps=nblk, in_specs=in_specs, out_specs=out_specs, out_shape=out_shape,
                      scratch_shapes=[], args=[*rows, *[a for a in shifted for _ in range(n_sub)], *pars, *[a for a, _ in refs]])
    parts = (res[:n_o], res[n_o:])
    return parts if comm is None else parts + (cres,)


C_ROWS = 928
C_WINDOW = 912
BIG_ROWS = 768
DW_ROWS = 2816


def _contract_rows(lp, big=DW_ROWS):
    for rows in (big, BIG_ROWS):
        if lp % rows == 0:
            return rows
    return CH


def _dw_into(name, a, b, slab, ka, a_sharded, row_blk, n_s, s0, piece_rows=2048, a_square=False):
    lp = a.shape[0]
    bm = _contract_rows(lp)
    steps = lp // bm

    def body(a_ref, b_ref, *rest):
        o_ref, acc = rest[-2], rest[-1]
        k = pl.program_id(1)

        @pl.when(k == 0)
        def _():
            acc[...] = jnp.zeros_like(acc)
        a_v = a_ref[...]
        acc[...] += _dot(a_v * a_v if a_square else a_v, b_ref[...], TN)

        @pl.when(k == steps - 1)
        def _():
            o_ref[0] = acc[...].astype(bf16)

    in_specs = [pl.BlockSpec((bm, ka), (lambda s, k: (k, s)) if a_sharded else (lambda s, k: (k, 0))),
                pl.BlockSpec((bm, D), (lambda s, k: (k, 0)) if a_sharded else (lambda s, k: (k, s)))]
    args = [a, b]
    aliases = {}
    if slab is not None:
        in_specs.append(_ANY)
        args.append(slab)
        aliases = {2: 0}
    return pl.pallas_call(
        body, name=name, grid=(n_s, steps), in_specs=in_specs,
        out_specs=pl.BlockSpec((1, ka, D), lambda s, k: (s0 + s, row_blk, 0)),
        out_shape=jax.ShapeDtypeStruct((4, piece_rows, D), bf16),
        scratch_shapes=[pltpu.VMEM((ka, D), f32)], input_output_aliases=aliases,
        compiler_params=_cp(("arbitrary", "arbitrary")))(*args)


def _dw_in_t(dproj, n0):
    lp = n0.shape[0]
    bm = _contract_rows(lp, BIG_ROWS)
    steps = lp // bm
    bn = 512

    def body(a_ref, b_ref, o_ref, acc):
        k = pl.program_id(1)

        @pl.when(k == 0)
        def _():
            acc[...] = jnp.zeros_like(acc)
        acc[...] += _dot(a_ref[...], b_ref[...], TN)

        @pl.when(k == steps - 1)
        def _():
            o_ref[...] = acc[...].astype(bf16)

    return pl.pallas_call(
        body, name="dw_in", grid=(D // bn, steps),
        in_specs=[pl.BlockSpec((bm, W_PROJ), lambda j, k: (k, 0)), pl.BlockSpec((bm, bn), lambda j, k: (k, j))],
        out_specs=pl.BlockSpec((W_PROJ, bn), lambda j, k: (0, j)),
        out_shape=jax.ShapeDtypeStruct((W_PROJ, D), bf16),
        scratch_shapes=[pltpu.VMEM((W_PROJ, bn), f32)],
        compiler_params=_cp(("arbitrary", "arbitrary")))(dproj, n0)


def _head_expand():
    h = lax.broadcasted_iota(jnp.int32, (HEADS, D), 0)
    c = lax.broadcasted_iota(jnp.int32, (HEADS, D), 1)
    return jnp.where((c >> 6) == h, 1.0, 0.0).astype(bf16)


def _ssd_common(i, P, prev8, cw, cb, dtb, alog, xc=None):
    z = P[:, O_Z:O_Z + D]
    xp = P[:, O_XBC:O_XBC + D_XBC]
    dt_raw = P[:, O_DT:O_DT + HEADS]
    row = lax.broadcasted_iota(jnp.int32, (CH, 1), 0)
    if xc is None:
        row8 = lax.broadcasted_iota(jnp.int32, (8, 1), 0)
        xc = cb + cw[3:4] * xp
        for k in (1, 2, 3):
            rolled = pltpu.roll(xp, k, 0)
            fix = pltpu.roll(prev8, k, 0)
            top = jnp.where(row8 < k, fix, rolled[0:8])
            xc = xc + cw[3 - k:4 - k] * jnp.concatenate([top, rolled[8:]], axis=0)
    sg = _sigmoid(xc)
    xbc = xc * sg
    live = jnp.where(jnp.logical_or(i > 0, row >= CH - N_META), 1.0, 0.0)
    pre = dt_raw + dtb
    dt = jnp.where(pre > 20.0, pre, jnp.log(1.0 + jnp.exp(jnp.minimum(pre, 20.0)))) * live
    a = -jnp.exp(alog)
    dta = dt * a
    r_i = lax.broadcasted_iota(jnp.int32, (CH, CH), 0)
    c_i = lax.broadcasted_iota(jnp.int32, (CH, CH), 1)
    tril = r_i >= c_i
    acs = _split_dot(dta, jnp.where(tril, 1.0, 0.0).astype(bf16), None, 3, v_is_lhs=False)
    acs_t = _split_dot(dta, jnp.where(r_i <= c_i, 1.0, 0.0).astype(bf16), TN, 3)
    e = _head_expand()
    acs_e = _split_dot(acs, e, None, 3)
    dt_e = _split_dot(dt, e, None, 3)
    return dict(z=z, xp=xp, xc=xc, sg=sg, xbc=xbc, live=live, pre=pre, dt=dt, a=a, tril=tril,
                acs=acs, acs_t=acs_t, e=e, acs_e=acs_e, dt_e=dt_e)


def _lmat(c, h):
    seg = c["acs"][:, h:h + 1] - c["acs_t"][h:h + 1, :]
    return jnp.where(c["tril"], jnp.exp(jnp.minimum(seg, 0.0)), 0.0)


def _pair_masks():
    lane = lax.broadcasted_iota(jnp.int32, (1, 128), 1)
    return jnp.where(lane < HEAD_DIM, 1.0, 0.0), jnp.where(lane >= HEAD_DIM, 1.0, 0.0)


def _ssd_fwd(proj, conv_w, conv_b, dt_bias, a_log, d_ssd, g_ssd, nch, comm=None):
    def body(p_ref, cw_ref, cb_ref, dtb_ref, al_ref, d_ref, g_ref, y_ref, ys_ref, st_ref, xc_ref, prev8_ref, state_ref):
        i = pl.program_id(0)

        @pl.when(i == 0)
        def _():
            prev8_ref[...] = jnp.zeros_like(prev8_ref)
            state_ref[...] = jnp.zeros_like(state_ref)

        P = p_ref[...]
        c = _ssd_common(i, P, prev8_ref[...], cw_ref[...], cb_ref[...], dtb_ref[...], al_ref[...])
        prev8_ref[...] = c["xp"][CH - 8:CH]
        xc_ref[...] = c["xc"]
        xbc = c["xbc"]
        x = xbc[:, 0:D]
        xdt = x * c["dt_e"]
        a_last_e = c["acs_e"][CH - 1:CH, :]
        w_end = (xdt * jnp.exp(a_last_e - c["acs_e"])).astype(bf16)
        m0, m1 = _pair_masks()
        ys = []
        for g in range(2):
            bg = xbc[:, D + NSTATE * g:D + NSTATE * (g + 1)].astype(bf16)
            cg = xbc[:, D + 2 * NSTATE + NSTATE * g:D + 2 * NSTATE + NSTATE * (g + 1)].astype(bf16)
            gmat = _dot(cg, bg, NT)
            st = state_ref[g]
            st_ref[0, g] = st
            sl = slice(512 * g, 512 * (g + 1))
            y_off = _dot(cg, st.astype(bf16)) * jnp.exp(c["acs_e"][:, sl])
            contrib = _dot(bg, w_end[:, sl], TN)
            state_ref[g] = st * jnp.exp(a_last_e[:, sl]) + contrib
            yd = []
            for pr in range(4):
                h0 = 8 * g + 2 * pr
                xp2 = xdt[:, 128 * (4 * g + pr):128 * (4 * g + pr + 1)]
                ma = (gmat * _lmat(c, h0)).astype(bf16)
                mb = (gmat * _lmat(c, h0 + 1)).astype(bf16)
                yd.append(_dot(ma, (xp2 * m0).astype(bf16)) + _dot(mb, (xp2 * m1).astype(bf16)))
            ys.append(jnp.concatenate(yd, axis=1) + y_off)
        d_e = _split_dot(d_ref[...], c["e"], None, 3)
        y = jnp.concatenate(ys, axis=1) + x * d_e
        y_ref[...] = y
        yg = y * (c["z"] * _sigmoid(c["z"]))
        ys_ref[...] = _rms(yg, g_ref[...]).astype(bf16)

    full = lambda a: pl.BlockSpec(a.shape, lambda i, nd=a.ndim: (0,) * nd)
    return _call(
        body, comm, name="ssd_fwd", nsteps=nch,
        in_specs=[pl.BlockSpec((CH, W_PROJ), lambda i: (i, 0))] + [full(a) for a in (conv_w, conv_b, dt_bias, a_log, d_ssd, g_ssd)],
        out_specs=[pl.BlockSpec((CH, D), lambda i: (i, 0)), pl.BlockSpec((CH, D), lambda i: (i, 0)),
                   pl.BlockSpec((1, 2, NSTATE, 512), lambda i: (i, 0, 0, 0)), pl.BlockSpec((CH, D_XBC), lambda i: (i, 0))],
        out_shape=[jax.ShapeDtypeStruct((nch * CH, D), f32), jax.ShapeDtypeStruct((nch * CH, D), bf16),
                   jax.ShapeDtypeStruct((nch, 2, NSTATE, 512), f32), jax.ShapeDtypeStruct((nch * CH, D_XBC), f32)],
        scratch_shapes=[pltpu.VMEM((8, D_XBC), f32), pltpu.VMEM((2, NSTATE, 512), f32)],
        args=[proj, conv_w, conv_b, dt_bias, a_log, d_ssd, g_ssd])


def _ssd_bwd(proj, xc_all, y, dys, du, states, conv_w, conv_b, dt_bias, a_log, d_ssd, g_ssd, nch, comm=None):
    def body(p_ref, xc_ref, y_ref, dys_ref, du_ref, st_ref, cw_ref, cb_ref, dtb_ref, al_ref, d_ref, g_ref,
             dp_ref, dcw_ref, dcb_ref, ddtb_ref, dal_ref, dd_ref, dg_ref, nxt8_ref, dst_ref):
        step = pl.program_id(0)
        i = nch - 1 - step

        @pl.when(step == 0)
        def _():
            nxt8_ref[...] = jnp.zeros_like(nxt8_ref)
            dst_ref[...] = jnp.zeros_like(dst_ref)
            for r in (dcw_ref, dcb_ref, ddtb_ref, dal_ref, dd_ref, dg_ref):
                r[...] = jnp.zeros_like(r)

        P = p_ref[...]
        c = _ssd_common(i, P, None, cw_ref[...], cb_ref[...], dtb_ref[...], al_ref[...], xc=xc_ref[...])
        xbc, z, e = c["xbc"], c["z"], c["e"]
        x = xbc[:, 0:D]
        yv = y_ref[...]
        sz = _sigmoid(z)
        silu_z = z * sz
        dyg, dg8 = _rms_bwd(dys_ref[...], yv * silu_z, g_ref[...])
        dg_ref[...] += dg8
        dy = dyg * silu_z
        dz = dyg * yv * (sz * (1.0 + z * (1.0 - sz)))
        d_e = _split_dot(d_ref[...], e, None, 3)
        dd_ref[...] += _rsum8(dy * x)
        xdt = x * c["dt_e"]
        a_last_e = c["acs_e"][CH - 1:CH, :]
        e_end = jnp.exp(a_last_e - c["acs_e"])
        w_end = xdt * e_end
        e_acs = jnp.exp(c["acs_e"])
        dy_dec = dy * e_acs
        m0, m1 = _pair_masks()
        lane16 = lax.broadcasted_iota(jnp.int32, (1, HEADS), 1)
        row16 = lax.broadcasted_iota(jnp.int32, (HEADS, 1), 0)
        dacs = jnp.zeros((CH, HEADS), f32)
        dacs_t = jnp.zeros((HEADS, CH), f32)
        dxdt_parts, dbs, dcs, zparts, yoff_parts, dlast_parts = [], [], [], [], [], []
        for g in range(2):
            sl = slice(512 * g, 512 * (g + 1))
            bg = xbc[:, D + NSTATE * g:D + NSTATE * (g + 1)].astype(bf16)
            cg = xbc[:, D + 2 * NSTATE + NSTATE * g:D + 2 * NSTATE + NSTATE * (g + 1)].astype(bf16)
            gmat = _dot(cg, bg, NT)
            st = st_ref[0, g]
            dstn = dst_ref[g]
            dstn_b = dstn.astype(bf16)
            y_off = _dot(cg, st.astype(bf16)) * e_acs[:, sl]
            yoff_parts.append(y_off)
            bds = _dot(bg, dstn_b)
            zparts.append(w_end[:, sl] * bds)
            dlast_parts.append(jnp.sum(dstn * st, axis=0, keepdims=True) * jnp.exp(a_last_e[:, sl]))
            dg_acc = jnp.zeros((CH, CH), f32)
            dxd = []
            for pr in range(4):
                lo = 128 * (4 * g + pr)
                xp2 = xdt[:, lo:lo + 128].astype(bf16)
                dy2 = dy[:, lo:lo + 128]
                outp = jnp.zeros((CH, 128), f32)
                for hh, msk in ((0, m0), (1, m1)):
                    h = 8 * g + 2 * pr + hh
                    lm = _lmat(c, h)
                    dyh = (dy2 * msk).astype(bf16)
                    mh = (gmat * lm).astype(bf16)
                    outp = outp + _dot(mh, dyh, TN)
                    dml = _dot(dyh, xp2, NT) * lm
                    dg_acc = dg_acc + dml
                    q = dml * gmat
                    dacs = dacs + jnp.where(lane16 == h, jnp.sum(q, axis=1, keepdims=True), 0.0)
                    dacs_t = dacs_t + jnp.where(row16 == h, jnp.sum(q, axis=0, keepdims=True), 0.0)
                dxd.append(outp)
            dxdt_parts.append(jnp.concatenate(dxd, axis=1) + e_end[:, sl] * bds)
            dgb = dg_acc.astype(bf16)
            dcs.append(_dot(dgb, bg) + _dot(dy_dec[:, sl].astype(bf16), st.astype(bf16), NT))
            dbs.append(_dot(dgb, cg, TN) + _dot(w_end[:, sl].astype(bf16), dstn_b, NT))
            dst_ref[g] = dstn * jnp.exp(a_last_e[:, sl]) + _dot(cg, dy_dec[:, sl].astype(bf16), TN)
        dxdt = jnp.concatenate(dxdt_parts, axis=1)
        zfull = jnp.concatenate(zparts, axis=1)
        y_off_full = jnp.concatenate(yoff_parts, axis=1)
        dlast = jnp.concatenate(dlast_parts, axis=1)
        red = lambda v: _split_dot(v, e, NT, 2)
        eye16 = jnp.where(lax.broadcasted_iota(jnp.int32, (HEADS, HEADS), 0) == lax.broadcasted_iota(jnp.int32, (HEADS, HEADS), 1),
                          1.0, 0.0).astype(bf16)
        dacs = dacs - _split_dot(dacs_t, eye16, TN, 3)
        zred = red(zfull)
        dacs = dacs + red(dy * y_off_full) - zred
        last_term = jnp.sum(zred, axis=0, keepdims=True) + red(dlast)
        rowc = lax.broadcasted_iota(jnp.int32, (CH, 1), 0)
        dacs = dacs + jnp.where(rowc == CH - 1, last_term, 0.0)
        r_i = lax.broadcasted_iota(jnp.int32, (CH, CH), 0)
        c_i = lax.broadcasted_iota(jnp.int32, (CH, CH), 1)
        ddta = _split_dot(dacs, jnp.where(c_i >= r_i, 1.0, 0.0).astype(bf16), None, 3, v_is_lhs=False)
        ddt = ddta * c["a"] + red(dxdt * x)
        dal_ref[...] += _rsum8(ddta * c["dt"] * c["a"])
        ddt_raw = ddt * _sigmoid(c["pre"]) * c["live"]
        ddtb_ref[...] += _rsum8(ddt_raw)
        dx = dy * d_e + dxdt * c["dt_e"]
        dxbc = jnp.concatenate([dx, dbs[0], dbs[1], dcs[0], dcs[1]], axis=1)
        sg = c["sg"]
        dxc = dxbc * (sg * (1.0 + c["xc"] * (1.0 - sg)))
        dcb_ref[...] += _rsum8(dxc)
        xp = c["xp"]
        row8 = lax.broadcasted_iota(jnp.int32, (8, 1), 0)
        cw = cw_ref[...]
        dxp = cw[3:4] * dxc
        dcw = jnp.where(row8 == 3, jnp.sum(dxc * xp, axis=0, keepdims=True), 0.0)
        nxt8 = nxt8_ref[...]
        for j in (1, 2, 3):
            rolled = pltpu.roll(dxc, CH - j, 0)
            fix = pltpu.roll(nxt8, 8 - j, 0)
            bot = jnp.where(row8 >= 8 - j, fix, rolled[CH - 8:CH])
            later = jnp.concatenate([rolled[:CH - 8], bot], axis=0)
            dxp = dxp + cw[3 - j:4 - j] * later
            dcw = dcw + jnp.where(row8 == 3 - j, jnp.sum(later * xp, axis=0, keepdims=True), 0.0)
        dcw_ref[...] += dcw
        nxt8_ref[...] = dxc[0:8]
        dp_ref[:, O_Z:O_Z + D] = dz.astype(bf16)
        dp_ref[:, O_U:O_U + D] = du_ref[...].astype(bf16)
        dp_ref[:, O_XBC:O_XBC + D_XBC] = dxp.astype(bf16)
        dp_ref[:, O_DT:W_PROJ] = jnp.zeros((CH, W_PROJ - O_DT), bf16)
        dp_ref[:, O_DT:O_DT + HEADS] = ddt_raw.astype(bf16)

    full = lambda a: pl.BlockSpec(a.shape, lambda s, nd=a.ndim: (0,) * nd)
    rev = lambda s: (nch - 1 - s, 0)
    acc = lambda cdim: pl.BlockSpec((8, cdim), lambda s: (0, 0))
    return _call(
        body, comm, name="ssd_bwd", nsteps=nch,
        in_specs=[pl.BlockSpec((CH, W_PROJ), rev), pl.BlockSpec((CH, D_XBC), rev),
                  pl.BlockSpec((CH, D), rev), pl.BlockSpec((CH, D), rev), pl.BlockSpec((CH, D), rev),
                  pl.BlockSpec((1, 2, NSTATE, 512), lambda s: (nch - 1 - s, 0, 0, 0))]
        + [full(a) for a in (conv_w, conv_b, dt_bias, a_log, d_ssd, g_ssd)],
        out_specs=[pl.BlockSpec((CH, W_PROJ), rev), acc(D_XBC), acc(D_XBC), acc(HEADS), acc(HEADS), acc(D), acc(D)],
        out_shape=[jax.ShapeDtypeStruct((nch * CH, W_PROJ), bf16)]
        + [jax.ShapeDtypeStruct((8, cdim), f32) for cdim in (D_XBC, D_XBC, HEADS, HEADS, D, D)],
        scratch_shapes=[pltpu.VMEM((8, D_XBC), f32), pltpu.VMEM((2, NSTATE, 512), f32)],
        args=[proj, xc_all, y, dys, du, states, conv_w, conv_b, dt_bias, a_log, d_ssd, g_ssd])


SCAN_UNROLL = 8


def _to_slabs(slab_ref, q, mat):
    for ls in range(8):
        slab_ref[ls, pl.ds(PITCH * q, CH), :] = mat[:, 128 * ls:128 * (ls + 1)]


def _from_slabs(slab, q):
    return jnp.concatenate([slab(ls, PITCH * q) for ls in range(8)], axis=1)


def _tile(slab_ref, ls, t, lead=None):
    idx = (ls, pl.ds(t, 8, stride=PITCH), slice(None))
    return slab_ref[idx] if lead is None else slab_ref[(lead,) + idx]


def _s5_fwd(proj, bbq, ccq, ar, ai, d_skip, nch, comm=None):
    def body(u_ref, bb_ref, cc_ref, ar_ref, ai_ref, d_ref, s_ref, yl_ref, y5_ref, bu_ref, st_ref):
        @pl.when(pl.program_id(0) == 0)
        def _():
            st_ref[...] = jnp.zeros_like(st_ref)
        u = u_ref[...]
        ub = u.astype(bf16)
        for q in range(NQ):
            _to_slabs(bu_ref, q, _dot(ub[:, 128 * q:128 * (q + 1)], bb_ref[q]))
        ar_t = [ar_ref[:, 128 * l:128 * (l + 1)] for l in range(4)]
        ai_t = [ai_ref[:, 128 * l:128 * (l + 1)] for l in range(4)]

        def one(t, carry):
            re, im = carry
            nre, nim = [], []
            for l in range(4):
                a = ar_t[l] * re[l] - ai_t[l] * im[l] + _tile(bu_ref, l, t)
                b = ar_t[l] * im[l] + ai_t[l] * re[l] + _tile(bu_ref, l + 4, t)
                s_ref[0, l, pl.ds(t, 8, stride=PITCH), :] = a
                s_ref[0, l + 4, pl.ds(t, 8, stride=PITCH), :] = b
                nre.append(a)
                nim.append(b)
            return tuple(nre), tuple(nim)

        def step(tt, carry):
            for k in range(SCAN_UNROLL):
                carry = one(tt * SCAN_UNROLL + k, carry)
            return carry
        init = (tuple(st_ref[l] for l in range(4)), tuple(st_ref[l + 4] for l in range(4)))
        re, im = lax.fori_loop(0, CH // SCAN_UNROLL, step, init)
        for l in range(4):
            st_ref[l] = re[l]
            st_ref[l + 4] = im[l]
        ys = []
        for q in range(NQ):
            sq = _from_slabs(lambda ls, r0: s_ref[0, ls, pl.ds(r0, CH), :], q).astype(bf16)
            ys.append(_dot(sq, cc_ref[q], NT))
        yl = jnp.concatenate(ys, axis=1) + u * d_ref[...]
        yl_ref[...] = yl
        y5_ref[...] = (0.5 * yl * (1.0 + lax.erf(yl * (1.0 / math.sqrt(2.0))))).astype(bf16)

    const = lambda a: pl.BlockSpec(a.shape, lambda i, nd=a.ndim: (0,) * nd)
    return _call(
        body, comm, name="s5_fwd", nsteps=nch,
        in_specs=[pl.BlockSpec((CH, D), lambda i: (i, O_U // D)), const(bbq), const(ccq), const(ar), const(ai), const(d_skip)],
        out_specs=[pl.BlockSpec((1, 8, 8 * PITCH, 128), lambda i: (i, 0, 0, 0)),
                   pl.BlockSpec((CH, D), lambda i: (i, 0)), pl.BlockSpec((CH, D), lambda i: (i, 0))],
        out_shape=[jax.ShapeDtypeStruct((nch, 8, 8 * PITCH, 128), f32), jax.ShapeDtypeStruct((nch * CH, D), f32),
                   jax.ShapeDtypeStruct((nch * CH, D), bf16)],
        scratch_shapes=[pltpu.VMEM((8, 8 * PITCH, 128), f32), pltpu.VMEM((8, 8, 128), f32)],
        args=[proj, bbq, ccq, ar, ai, d_skip])


def _s5_bwd(proj, dyl, s_all, bbq, ccq, ar, ai, d_skip, nch, comm=None):
    def body(u_ref, dy_ref, s_ref, bbt_ref, cct_ref, ar_ref, ai_ref, d_ref,
             du_ref, dcc_ref, dbb_ref, dab_ref, dd_ref, ga_ref, st_ref):
        @pl.when(pl.program_id(0) == 0)
        def _():
            st_ref[...] = jnp.zeros_like(st_ref)
            for r in (dcc_ref, dbb_ref, dab_ref, dd_ref):
                r[...] = jnp.zeros_like(r)
        u = u_ref[...]
        dyl_v = dy_ref[...]
        dd_ref[...] += _rsum8(dyl_v * u)
        ub = u.astype(bf16)
        dyb = dyl_v.astype(bf16)
        for q in range(NQ):
            _to_slabs(ga_ref, q, _dot(dyb[:, 128 * q:128 * (q + 1)], cct_ref[q]))
        ar_t = [ar_ref[:, 128 * l:128 * (l + 1)] for l in range(4)]
        ai_t = [ai_ref[:, 128 * l:128 * (l + 1)] for l in range(4)]

        def one(t, carry):
            re, im, dar, dai = carry
            nre, nim, ndar, ndai = [], [], [], []
            for l in range(4):
                sre = _tile(s_ref, l, t, lead=0)
                sim = _tile(s_ref, l + 4, t, lead=0)
                ndar.append(dar[l] + re[l] * sre + im[l] * sim)
                ndai.append(dai[l] + im[l] * sre - re[l] * sim)
                a = _tile(ga_ref, l, t) + ar_t[l] * re[l] + ai_t[l] * im[l]
                b = _tile(ga_ref, l + 4, t) - ai_t[l] * re[l] + ar_t[l] * im[l]
                ga_ref[l, pl.ds(t, 8, stride=PITCH), :] = a
                ga_ref[l + 4, pl.ds(t, 8, stride=PITCH), :] = b
                nre.append(a)
                nim.append(b)
            return tuple(nre), tuple(nim), tuple(ndar), tuple(ndai)

        def step(tt, carry):
            for k in range(SCAN_UNROLL):
                carry = one(CH - 1 - (tt * SCAN_UNROLL + k), carry)
            return carry
        four = lambda ref, o: tuple(ref[l + o] for l in range(4))
        re, im, dar, dai = lax.fori_loop(0, CH // SCAN_UNROLL, step,
                                         (four(st_ref, 0), four(st_ref, 4), four(dab_ref, 0), four(dab_ref, 4)))
        for l in range(4):
            st_ref[l], st_ref[l + 4] = re[l], im[l]
            dab_ref[l], dab_ref[l + 4] = dar[l], dai[l]
        dus = []
        for q in range(NQ):
            aq = _from_slabs(lambda ls, r0: ga_ref[ls, pl.ds(r0, CH), :], q).astype(bf16)
            sq = _from_slabs(lambda ls, r0: s_ref[0, ls, pl.ds(r0, CH), :], q).astype(bf16)
            dcc_ref[q] += _dot(dyb[:, 128 * q:128 * (q + 1)], sq, TN)
            dbb_ref[q] += _dot(ub[:, 128 * q:128 * (q + 1)], aq, TN)
            dus.append(_dot(aq, bbt_ref[q], NT))
        du_ref[...] = jnp.concatenate(dus, axis=1) + dyl_v * d_ref[...]

    const = lambda a: pl.BlockSpec(a.shape, lambda s, nd=a.ndim: (0,) * nd)
    rev = lambda s: (nch - 1 - s, 0)
    return _call(
        body, comm, name="s5_bwd", nsteps=nch,
        in_specs=[pl.BlockSpec((CH, D), lambda s: (nch - 1 - s, O_U // D)), pl.BlockSpec((CH, D), rev),
                  pl.BlockSpec((1, 8, 8 * PITCH, 128), lambda s: (nch - 1 - s, 0, 0, 0)),
                  const(bbq), const(ccq), const(ar), const(ai), const(d_skip)],
        out_specs=[pl.BlockSpec((CH, D), rev), pl.BlockSpec((NQ, 128, D), lambda s: (0, 0, 0)),
                   pl.BlockSpec((NQ, 128, D), lambda s: (0, 0, 0)), pl.BlockSpec((8, 8, 128), lambda s: (0, 0, 0)),
                   pl.BlockSpec((8, D), lambda s: (0, 0))],
        out_shape=[jax.ShapeDtypeStruct((nch * CH, D), f32), jax.ShapeDtypeStruct((NQ, 128, D), f32),
                   jax.ShapeDtypeStruct((NQ, 128, D), f32), jax.ShapeDtypeStruct((8, 8, 128), f32),
                   jax.ShapeDtypeStruct((8, D), f32)],
        scratch_shapes=[pltpu.VMEM((8, 8 * PITCH, 128), f32), pltpu.VMEM((8, 8, 128), f32)],
        args=[proj, dyl, s_all, bbq, ccq, ar, ai, d_skip])


def _s5_tables(lam_re, lam_im, log_step, b_re, b_im):
    step = jnp.exp(log_step)[:, None]
    mag = jnp.exp(lam_re * step)
    ab_re = mag * jnp.cos(lam_im * step)
    ab_im = mag * jnp.sin(lam_im * step)
    den = lam_re * lam_re + lam_im * lam_im
    coef_re = ((ab_re - 1.0) * lam_re + ab_im * lam_im) / den
    coef_im = (ab_im * lam_re - (ab_re - 1.0) * lam_im) / den
    bb_re = coef_re[..., None] * b_re - coef_im[..., None] * b_im
    bb_im = coef_re[..., None] * b_im + coef_im[..., None] * b_re
    return ab_re, ab_im, bb_re, bb_im


def _blockdiag_in(m_re, m_im):
    eye = jnp.eye(8, dtype=f32)

    def one(m):
        m = m.reshape(NQ, 8, S5_P, 16)
        return jnp.einsum("qgph,gk->qghkp", m, eye).reshape(NQ, 128, 512)
    return jnp.concatenate([one(m_re), one(m_im)], axis=2)


def _blockdiag_in_grad(dm):
    rows = dm.reshape(NQ, 8, 16, 1024)
    shape = (8, 1, 1024)
    diag = (lax.broadcasted_iota(jnp.int32, shape, 2) // S5_P) % 8 == lax.broadcasted_iota(jnp.int32, shape, 0)
    z = jnp.sum(jnp.where(diag, rows, 0.0), axis=1)

    def one(part):
        return jnp.transpose(part.reshape(NQ, 16, 8, S5_P), (0, 2, 3, 1)).reshape(NQ * 8, S5_P, 16)
    return one(z[:, :, :512]), one(z[:, :, 512:])


def _local_step(x2, tgt2, meta, p, w_in_t, shards):
    seq = x2.shape[0]
    nch = 1 + seq // CH
    bmb = BIG_ROWS if (nch * CH) % BIG_ROWS == 0 else CH
    nbig = nch * CH // bmb
    metablk = jnp.concatenate([jnp.zeros((CH - N_META, D), f32), meta, jnp.zeros((bmb - CH, D), f32)], axis=0)
    w_full = (w_in_t, pl.BlockSpec(w_in_t.shape, lambda i: (0, 0), pipeline_mode=pl.Buffered(1)))

    def lead(i, v):
        return jnp.logical_and(i == 0, lax.broadcasted_iota(jnp.int32, (v.shape[0], 1), 0) < CH)
    h0_of = lambda i, s, q: jnp.where(lead(i, s[0]), q[0][:s[0].shape[0]], s[0])

    def in_fn(i, r, s, q, w):
        nb = _rms(h0_of(i, s, q), q[1]).astype(bf16)
        return [_dot(nb, w[0][...], NT), nb], []
    (proj, n0), _, (g_up,) = _rowwise("in_proj", in_fn, nbig, shifted=[x2], pars=[metablk, p["g_mix"]], refs=[w_full],
                                      out_rows=[(W_PROJ, f32), (D, bf16)], bm=bmb, comm=_gather_piece(shards[0]))
    (y, y_ssd, states, xc_all), (g_down,) = _ssd_fwd(proj, p["conv_w"], p["conv_b"], p["dt_bias"], p["a_log"], p["d_ssd"],
                                                     p["g_ssd"], nch, comm=_gather_piece(shards[1]))

    ab_re, ab_im, bb_re, bb_im = _s5_tables(p["lam_re"], p["lam_im"], p["log_step"], p["b_re"], p["b_im"])
    ar, ai = ab_re.reshape(NQ, 512), ab_im.reshape(NQ, 512)
    bbq = _blockdiag_in(bb_re, bb_im)
    ccq = _blockdiag_in(jnp.swapaxes(p["c_re"], 1, 2), -jnp.swapaxes(p["c_im"], 1, 2))
    d_skip = p["d_s5"].reshape(1, D)
    bbq_b, ccq_b = bbq.astype(bf16), ccq.astype(bf16)
    (s_all, ylin, y5), (g_go,) = _s5_fwd(proj, bbq_b, ccq_b, ar, ai, d_skip, nch, comm=_gather_piece(shards[2]))
    whole = lambda a: (a, pl.BlockSpec(a.shape, lambda i: (0, 0, 0), pipeline_mode=pl.Buffered(1)))
    w_up, w_down = whole(g_up), whole(g_down)
    w_glu_t = (g_go, pl.BlockSpec((4, 512, D), lambda i: (0, 0, 0), pipeline_mode=pl.Buffered(1)))
    w_out = (g_go, pl.BlockSpec((4, 512, D), lambda i: (0, 1, 0), pipeline_mode=pl.Buffered(1)))

    def glu_fn(i, r, s, q, w):
        v = jnp.concatenate([_dot(r[0], w[0][k], NT) for k in range(4)], axis=1) + q[0]
        return [v, _rms(v[:, :D] * _sigmoid(v[:, D:]), q[1])], []
    (v, y_s5), _ = _rowwise("glu", glu_fn, nbig, rows=[y5], pars=[p["b_glu"], p["g_s5"]], refs=[w_glu_t],
                            out_rows=[(2 * D, bf16), (D, bf16)], bm=bmb)

    def out_fn(i, r, s, q, w):
        acc = (_dot(r[0][:, :512], w[0][0]) + _dot(r[0][:, 512:], w[0][1]) + _dot(r[1][:, :512], w[0][2])
               + _dot(r[1][:, 512:], w[0][3]))
        return [h0_of(i, s, q) + acc], []
    (h1,), _ = _rowwise("out_proj", out_fn, nbig, rows=[y_ssd, y_s5], shifted=[x2], pars=[metablk], refs=[w_out],
                        out_rows=[(D, f32)], bm=bmb)

    def up_fn(i, r, s, q, w):
        nb = _rms(r[0], q[0]).astype(bf16)
        return [jnp.concatenate([jnp.maximum(_dot(nb, w[0][k]), 0.0).astype(bf16) for k in range(4)], axis=1), nb], []
    (relu_m, n1), _ = _rowwise("up_proj", up_fn, nbig, rows=[h1], pars=[p["g_mlp"]], refs=[w_up],
                               out_rows=[(4 * D, bf16), (D, bf16)], bm=bmb)

    def down_fn(i, r, s, q, w):
        acc = None
        for k in range(4):
            t = r[0][:, D * k:D * (k + 1)]
            part = _dot(t * t, w[0][k])
            acc = part if acc is None else acc + part
        return [r[1] + acc], []
    (h2,), _ = _rowwise("down_proj", down_fn, nbig, rows=[relu_m, h1], refs=[w_down], out_rows=[(D, f32)], bm=bmb)

    def final_fn(i, r, s, q, w):
        err = jnp.where(lead(i, r[0]), 0.0, _rms(r[0], q[0]) - s[0])
        dh, dg8 = _rms_bwd(err * (1.0 / D), r[0], q[0])
        return [dh, dh], [_rsum8(err * err), dg8]
    (dh2, dh2_b), (loss8, dgf8) = _rowwise("final", final_fn, nbig, rows=[h2], shifted=[tgt2], pars=[p["g_final"]],
                                           out_rows=[(D, f32), (D, bf16)], out_accs=[D, D], bm=bmb)
    loss = 0.5 / D * jnp.sum(loss8)

    def down_bwd_fn(i, r, s, q, w):
        dm_ = [_dot(r[0], w[0][k], NT) * (2.0 * r[1][:, D * k:D * (k + 1)].astype(f32)) for k in range(4)]
        return [jnp.concatenate(dm_, axis=1)], []
    (dm,), _ = _rowwise("down_bwd", down_bwd_fn, nbig, rows=[dh2_b, relu_m], refs=[w_down], out_rows=[(4 * D, bf16)], bm=bmb)
    g_a = _dw_into("dw_down", relu_m, dh2_b, None, 1024, True, 1, 4, 0, piece_rows=2048, a_square=True)

    def up_bwd_fn(i, r, s, q, w):
        acc = _dot(r[0][:, :D], w[0][0], NT)
        for k in range(1, 4):
            acc = acc + _dot(r[0][:, D * k:D * (k + 1)], w[0][k], NT)
        dh, dg8 = _rms_bwd(acc, r[1], q[0])
        dh1_ = r[2] + dh
        return [dh1_, dh1_], [dg8]
    (dh1, dh1_b), (dgmlp8,) = _rowwise("up_bwd", up_bwd_fn, nbig, rows=[dm, h1, dh2], pars=[p["g_mlp"]], refs=[w_up],
                                       out_rows=[(D, f32), (D, bf16)], out_accs=[D], bm=bmb)
    g_a = _dw_into("dw_up", n1, dm, g_a, 1024, False, 0, 4, 0, piece_rows=2048)

    def out_bwd_fn(i, r, s, q, w):
        dmix = [_dot(r[0], w[0][k], NT) for k in range(4)]
        v1, v2 = r[1][:, :D].astype(f32), r[1][:, D:].astype(f32)
        s2 = _sigmoid(v2)
        dglu, dg8 = _rms_bwd(jnp.concatenate(dmix[2:], axis=1), v1 * s2, q[0])
        dv = jnp.concatenate([dglu * s2, dglu * v1 * s2 * (1.0 - s2)], axis=1)
        return [jnp.concatenate(dmix[:2], axis=1), dv], [dg8, _rsum8(dv)]
    (dys, dv), (dgs58, dbglu8) = _rowwise("out_bwd", out_bwd_fn, nbig, rows=[dh1_b, v], pars=[p["g_s5"]], refs=[w_out],
                                          out_rows=[(D, f32), (2 * D, bf16)], out_accs=[D, 2 * D], bm=bmb)
    g_b = _dw_into("dw_out_a", y_ssd, dh1_b, None, 512, True, 1, 2, 0, piece_rows=1024)
    g_b = _dw_into("dw_out_b", y_s5, dh1_b, g_b, 512, True, 1, 2, 2, piece_rows=1024)

    def glu_bwd_fn(i, r, s, q, w):
        acc = _dot(r[0][:, :512], w[0][0])
        for k in range(1, 4):
            acc = acc + _dot(r[0][:, 512 * k:512 * (k + 1)], w[0][k])
        yl = r[1]
        cdf = 0.5 * (1.0 + lax.erf(yl * (1.0 / math.sqrt(2.0))))
        pdf = jnp.exp(-0.5 * yl * yl) * (1.0 / math.sqrt(2.0 * math.pi))
        return [acc * (cdf + yl * pdf)], []
    (dylin,), _ = _rowwise("glu_bwd", glu_bwd_fn, nbig, rows=[dv, ylin], refs=[w_glu_t], out_rows=[(D, f32)], bm=bmb)
    g_b = _dw_into("dw_glu", dv, y5, g_b, 512, True, 0, 4, 0, piece_rows=1024)

    (du, dcc, dbb, dab, dds5), (land_a,) = _s5_bwd(proj, dylin, s_all, bbq_b, ccq_b, ar, ai, d_skip, nch,
                                                   comm=_scatter_piece(g_a))

    s8 = lambda a: jnp.sum(a, axis=0, keepdims=True)
    dab_q = jnp.swapaxes(dab.reshape(2, 4, NQ, 128), 1, 2).reshape(2, NQ * 8, S5_P)
    dbb_re, dbb_im = _blockdiag_in_grad(dbb)
    dcr, dci = _blockdiag_in_grad(dcc)
    _, vjp = jax.vjp(_s5_tables, p["lam_re"], p["lam_im"], p["log_step"], p["b_re"], p["b_im"])
    dlam_re, dlam_im, dlog_step, db_re, db_im = vjp((dab_q[0], dab_q[1], dbb_re, dbb_im))
    early = dict(lam_re=dlam_re, lam_im=dlam_im, log_step=dlog_step, b_re=db_re, b_im=db_im, c_re=jnp.swapaxes(dcr, 1, 2),
                 c_im=-jnp.swapaxes(dci, 1, 2), d_s5=s8(dds5).reshape(NQ * 8, 16), b_glu=s8(dbglu8), g_s5=s8(dgs58),
                 g_mlp=s8(dgmlp8), g_final=s8(dgf8).reshape(D))
    early_pack = _pack_small([early[n] for n in EARLY], _rows_for(EARLY))

    (dproj, dcw8, dcb8, ddtb8, dal8, dd8, dgssd8), (land_b, all_early) = _ssd_bwd(
        proj, xc_all, y, dys, du, states, p["conv_w"], p["conv_b"], p["dt_bias"], p["a_log"], p["d_ssd"], p["g_ssd"], nch,
        comm=_both(_scatter_piece(g_b), _gather_blocks(early_pack)))

    gt = _dw_in_t(dproj, n0)
    windows = [[(0, 912)], [(896, 1024), (O_XBC, O_XBC + 784)], [(O_XBC + 768, O_DT + HEADS), (O_U, O_U + 128)],
               [(O_U + 112, O_U + D)]]
    assert all(sum(b - a for a, b in w) == C_WINDOW for w in windows)
    g_c = jnp.concatenate([part for w in windows for part in
                           [gt[a:b] for a, b in w] + [jnp.zeros((C_ROWS - C_WINDOW, D), bf16)]], axis=0).reshape(4, C_ROWS, D)

    def in_bwd_fn(i, r, s, q, w):
        dh, dg8 = _rms_bwd(_dot(r[0], w[0][...]), h0_of(i, s, q), q[1])
        dh0 = r[1] + dh
        dmeta = jnp.where(i == 0, dh0[CH - N_META:], 0.0)
        return [dh0], [dg8, dmeta[:8], dmeta[8:]]
    (grad_x,), (dgmix8, dmeta_a, dmeta_b), (land_c,) = _rowwise(
        "in_bwd", in_bwd_fn, nch, rows=[dproj, dh1], shifted=[x2], pars=[metablk, p["g_mix"]], refs=[w_full],
        out_rows=[(D, f32, "shifted")], out_accs=[D, D, D], bm=CH, comm=_scatter_piece(g_c))

    hsum = lambda a: jnp.sum(s8(a).reshape(HEADS, HEAD_DIM), axis=1).reshape(1, HEADS)
    late = dict(g_mix=s8(dgmix8), conv_b=s8(dcb8), dt_bias=s8(ddtb8), a_log=s8(dal8), d_ssd=hsum(dd8), g_ssd=s8(dgssd8),
                conv_w=dcw8[0:4], meta_tokens=jnp.concatenate([dmeta_a, dmeta_b], axis=0), loss=loss.reshape(1))
    return grad_x, [(g_a, land_a), (g_b, land_b), (g_c, land_c)], all_early, late


def _perm_rows_w_in(wt):
    return jnp.concatenate([wt[0:1024], wt[2576:3600], wt[1024:2560], wt[2560:2576],
                            jnp.zeros((W_PROJ - 3600, wt.shape[1]), wt.dtype)], axis=0)


def _allgather8(x_shard, name):
    m_per, n = x_shard.shape

    def body(x_ref, out_ref, send_sems, recv_sems, stage, local_sems):
        x, y, c = _place()
        me, sibling = (x, y, c), (x, y, 1 - c)
        chips = [(1 - x, y), (x, 1 - y), (1 - x, 1 - y)]

        def rows(px, py, pc):
            return out_ref.at[pl.ds((4 * px + 2 * py + pc) * m_per, m_per), :]

        def copy(k, block, to, src=None):
            return pltpu.make_async_remote_copy(
                src_ref=rows(*block) if src is None else src, dst_ref=rows(*block),
                send_sem=send_sems.at[k], recv_sem=recv_sems.at[k], device_id=to, device_id_type=MESH)

        load = pltpu.make_async_copy(x_ref, stage, local_sems.at[0])
        load.start()
        first = [copy(0, me, sibling, src=x_ref)]
        first += [copy(1 + j, me, (*chip, c), src=x_ref) for j, chip in enumerate(chips)]
        for cp in first:
            cp.start()
        load.wait()
        store = pltpu.make_async_copy(stage, rows(*me), local_sems.at[1])
        store.start()
        passed = [copy(4 + j, (*chip, c), sibling) for j, chip in enumerate(chips)]
        for j, chip in enumerate(chips):
            copy(1 + j, (*chip, c), me).wait_recv()
            passed[j].start()
        copy(0, sibling, me).wait_recv()
        for j, chip in enumerate(chips):
            copy(4 + j, (*chip, 1 - c), me).wait_recv()
        for cp in first + passed:
            cp.wait_send()
        store.wait()

    return pl.pallas_call(
        body, name=name, out_shape=jax.ShapeDtypeStruct((8 * m_per, n), x_shard.dtype),
        in_specs=[_ANY], out_specs=_ANY,
        scratch_shapes=[pltpu.SemaphoreType.DMA((7,)), pltpu.SemaphoreType.DMA((7,)), pltpu.VMEM((m_per, n), x_shard.dtype),
                        pltpu.SemaphoreType.DMA((2,))])(x_shard)


def _swap_sibling(parts, name):
    n = len(parts)

    def body(*refs):
        send_sems, recv_sems = refs[2 * n:]
        x, y, c = _place()
        cps = [pltpu.make_async_remote_copy(src_ref=refs[k], dst_ref=refs[n + k], send_sem=send_sems.at[k],
                                            recv_sem=recv_sems.at[k], device_id=(x, y, 1 - c), device_id_type=MESH)
               for k in range(n)]
        for cp in cps:
            cp.start()
        for cp in cps:
            cp.wait()

    return pl.pallas_call(
        body, name=name, out_shape=[jax.ShapeDtypeStruct(r.shape, r.dtype) for r in parts], in_specs=[_ANY] * n,
        out_specs=[_ANY] * n, scratch_shapes=[pltpu.SemaphoreType.DMA((n,)), pltpu.SemaphoreType.DMA((n,))])(*parts)


SH_CONVW, SH_META = 4 * 384, 16 * 256
SPARE_ROWS = 17

SMALL_SHAPES = dict(
    g_mix=(1, 1024), conv_b=(1, 1536), dt_bias=(1, 16), a_log=(1, 16), d_ssd=(1, 16), g_ssd=(1, 1024), lam_re=(1, 64, 64),
    lam_im=(1, 64, 64), log_step=(1, 64), b_re=(1, 64, 64, 16), b_im=(1, 64, 64, 16), c_re=(1, 64, 16, 64), c_im=(1, 64, 16, 64),
    d_s5=(1, 64, 16), b_glu=(1, 2048), g_s5=(1, 1024), g_mlp=(1, 1024), g_final=(1024,),
    conv_w=(4, D_XBC), meta_tokens=(N_META, D), loss=(1,))
EARLY = ["lam_re", "lam_im", "log_step", "b_re", "b_im", "c_re", "c_im", "d_s5", "b_glu", "g_s5", "g_mlp", "g_final"]
LATE = ["g_mix", "conv_b", "dt_bias", "a_log", "d_ssd", "g_ssd", "conv_w", "meta_tokens", "loss"]


def _rows_for(names):
    return -(-sum(math.prod(SMALL_SHAPES[n]) for n in names) // (8 * D)) * 8


def _pack_small(arrs, rows):
    flat = jnp.concatenate([a.reshape(-1).astype(f32) for a in arrs])
    return jnp.concatenate([flat, jnp.zeros((rows * D - flat.shape[0],), f32)]).reshape(rows, D)


def _unpack_small(slab, shapes):
    flat = slab.reshape(-1)
    out, o = [], 0
    for shp in shapes:
        n = math.prod(shp)
        out.append(flat[o:o + n].reshape(shp))
        o += n
    return out


def _sum8(g, rows, name):
    def body(g_ref, o_ref):
        acc = g_ref[0]
        for k in range(1, 8):
            acc = acc + g_ref[k]
        o_ref[...] = acc
    return pl.pallas_call(body, name=name, out_shape=jax.ShapeDtypeStruct((rows, D), f32),
                          compiler_params=_cp())(g.reshape(8, rows, D))


def _adam_math(w_, g_, m_, v_):
    m2 = ADAM_B1 * m_ + (1.0 - ADAM_B1) * g_
    v2 = ADAM_B2 * v_ + (1.0 - ADAM_B2) * jnp.square(g_)
    m_hat = m2 / (1.0 - ADAM_B1 ** ADAM_STEP)
    v_hat = v2 / (1.0 - ADAM_B2 ** ADAM_STEP)
    delta = -ADAM_LR * (m_hat / (jnp.sqrt(v_hat) + ADAM_EPS) + ADAM_WD * w_)
    return delta, m2, v2


def _adamw(name, w, g, m, v, bm):
    def fn(i, r, s, q, refs):
        return list(_adam_math(*r)), []
    c = w.shape[1]
    (d, m2, v2), _ = _rowwise(name, fn, w.shape[0] // bm, rows=[w, g, m, v], out_rows=[(c, f32)] * 3, bm=bm)
    return d, m2, v2


def _adamw_whole(name, w, g, m, v):
    def body(w_ref, g_ref, m_ref, v_ref, d_ref, m2_ref, v2_ref):
        d_ref[...], m2_ref[...], v2_ref[...] = _adam_math(w_ref[...], g_ref[...], m_ref[...], v_ref[...])
    return pl.pallas_call(body, name=name, out_shape=[jax.ShapeDtypeStruct(w.shape, f32)] * 3, compiler_params=_cp())(w, g, m, v)


def _sum_parts(name, own, land):
    def fn(i, r, s, q, refs):
        acc = r[0].astype(f32)
        for k in range(7):
            acc = acc + refs[0][k].astype(f32)
        return [acc], []
    rows = own.shape[0]
    bm = CH if rows % CH == 0 else rows
    (o,), _ = _rowwise(name, fn, rows // bm, rows=[own], refs=[(land, pl.BlockSpec((7, bm, D), lambda i: (0, i, 0)))],
                       out_rows=[(D, f32)], bm=bm)
    return o


def kernel(x, meta_tokens, g_mix, w_in, conv_w, conv_b, dt_bias, a_log, d_ssd, g_ssd, lam_re, lam_im, log_step, b_re, b_im, c_re, c_im, d_s5, w_glu, b_glu, g_s5, w_out, g_mlp, w_up, w_down, g_final, loss_target, m_meta_tokens, m_g_mix, m_w_in, m_conv_w, m_conv_b, m_dt_bias, m_a_log, m_d_ssd, m_g_ssd, m_lam_re, m_lam_im, m_log_step, m_b_re, m_b_im, m_c_re, m_c_im, m_d_s5, m_w_glu, m_b_glu, m_g_s5, m_w_out, m_g_mlp, m_w_up, m_w_down, m_g_final, v_meta_tokens, v_g_mix, v_w_in, v_conv_w, v_conv_b, v_dt_bias, v_a_log, v_d_ssd, v_g_ssd, v_lam_re, v_lam_im, v_log_step, v_b_re, v_b_im, v_c_re, v_c_im, v_d_s5, v_w_glu, v_b_glu, v_g_s5, v_w_out, v_g_mlp, v_w_up, v_w_down, v_g_final):
    given = dict(locals())
    cx, cy, cc = _place()
    chip = 2 * cx + cy

    small_f = jnp.concatenate([conv_w.reshape(-1), meta_tokens.reshape(-1)])
    t_hi = small_f.astype(bf16)
    r_1 = small_f - t_hi.astype(f32)
    t_mid = r_1.astype(bf16)
    t_lo = (r_1 - t_mid.astype(f32)).astype(bf16)
    terms = jnp.concatenate([t_hi, t_mid, t_lo])
    spare = jnp.concatenate([terms, jnp.zeros((SPARE_ROWS * D - terms.shape[0],), bf16)]).reshape(SPARE_ROWS, D)
    shards = (w_up[0].astype(bf16), w_down[0].astype(bf16),
              jnp.concatenate([w_glu[0].T.astype(bf16), w_out[0].astype(bf16)], axis=0))
    in_rows = jnp.concatenate([w_in[0].T.astype(bf16), spare, jnp.zeros((1024 - 900 - SPARE_ROWS, D), bf16)], axis=0)
    my_half = lax.dynamic_slice_in_dim(in_rows, cc * 512, 512, axis=0)
    gathered = _allgather8(my_half, "gather_w_in").reshape(4, 1024, D)
    w_in_t = _perm_rows_w_in(jnp.concatenate([gathered[s, 0:900] for s in range(4)], axis=0))
    n_sf = SH_CONVW + SH_META
    tr = gathered[:, 900:900 + SPARE_ROWS].reshape(4, SPARE_ROWS * D)[:, :3 * n_sf].astype(f32).reshape(4, 3, n_sf)
    sp = tr[:, 0] + tr[:, 1] + tr[:, 2]
    conv_w_full = jnp.concatenate([sp[s, :SH_CONVW].reshape(4, 384) for s in range(4)], axis=1)
    meta_full = jnp.concatenate([sp[s, SH_CONVW:].reshape(16, 256) for s in range(4)], axis=1)

    p = dict(g_mix=g_mix, conv_w=conv_w_full, conv_b=conv_b, dt_bias=dt_bias, a_log=a_log, d_ssd=d_ssd, g_ssd=g_ssd,
             lam_re=lam_re[0], lam_im=lam_im[0], log_step=log_step[0], b_re=b_re[0], b_im=b_im[0], c_re=c_re[0], c_im=c_im[0],
             d_s5=d_s5[0], b_glu=b_glu, g_s5=g_s5, g_mlp=g_mlp, g_final=g_final.reshape(1, D))
    grad_x, pieces, all_early, late = _local_step(x[0], loss_target[0], meta_full, p, w_in_t, shards)
    grad_x = grad_x.reshape(x.shape)

    reds = []
    for k, (gp, land) in enumerate(pieces):
        half = gp.shape[1] // 2
        own = lax.dynamic_slice(gp, (chip, cc * half, 0), (1, half, D)).reshape(half, D)
        reds.append(_sum_parts("rs_sum_%d" % k, own, land))
    others = _swap_sibling(reds, "rs_share")
    lower = [jnp.where(cc == 0, r, o) for r, o in zip(reds, others)]
    upper = [jnp.where(cc == 0, o, r) for r, o in zip(reds, others)]
    g_up, g_down = lower[0], upper[0]
    g_glu, g_out = lower[1].T, upper[1]
    g_in_t = lax.dynamic_slice_in_dim(jnp.concatenate([lower[2], upper[2]], axis=0), 4 * chip, 900, axis=0)

    gs = dict(zip(EARLY, _unpack_small(_sum8(all_early, _rows_for(EARLY), "sum8_early"), [SMALL_SHAPES[n] for n in EARLY])))
    all_late = _allgather8(_pack_small([late[n] for n in LATE], _rows_for(LATE)), "gather_small")
    gs.update(zip(LATE, _unpack_small(_sum8(all_late, _rows_for(LATE), "sum8_late"), [SMALL_SHAPES[n] for n in LATE])))
    g_conv_w = lax.dynamic_slice_in_dim(gs.pop("conv_w"), chip * 384, 384, axis=1).reshape(conv_w.shape)
    g_meta = lax.dynamic_slice_in_dim(gs.pop("meta_tokens"), chip * 256, 256, axis=1)
    loss = gs.pop("loss").reshape(())

    grads = dict(gs, meta_tokens=g_meta, conv_w=g_conv_w, w_in=g_in_t.T.reshape(w_in.shape), w_glu=g_glu.reshape(w_glu.shape),
                 w_out=g_out.reshape(w_out.shape), w_up=g_up.reshape(w_up.shape), w_down=g_down.reshape(w_down.shape))
    delta, new_m, new_v = {}, {}, {}
    d_, m_, v_ = _adamw_whole("adamw_w_in", w_in[0].T, g_in_t, m_w_in[0].T, v_w_in[0].T)
    delta["w_in"], new_m["w_in"], new_v["w_in"] = (a.T.reshape(w_in.shape) for a in (d_, m_, v_))
    for n in ("w_glu", "w_out", "w_up", "w_down"):
        shp = given[n].shape
        two = lambda a: a.reshape(shp[1], shp[2])
        d_, m_, v_ = _adamw("adamw_" + n, two(given[n]), two(grads[n]), two(given["m_" + n]), two(given["v_" + n]), 256)
        delta[n], new_m[n], new_v[n] = d_.reshape(shp), m_.reshape(shp), v_.reshape(shp)
    for n in EARLY + LATE[:-1]:
        shp = given[n].shape
        if len(shp) == 4 and shp[-1] == 16:
            two = back = lambda a: jnp.swapaxes(a, -1, -2)
        else:
            two = (lambda a: a.reshape(1, -1)) if len(shp) == 1 else (lambda a: a)
            back = lambda a: a.reshape(shp)
        d_, m_, v_ = _adamw_whole("adamw_" + n, two(given[n]), two(grads[n].reshape(shp)), two(given["m_" + n]), two(given["v_" + n]))
        delta[n], new_m[n], new_v[n] = back(d_), back(m_), back(v_)

    order = ["meta_tokens", "g_mix", "w_in", "conv_w", "conv_b", "dt_bias", "a_log", "d_ssd", "g_ssd", "lam_re", "lam_im", "log_step",
             "b_re", "b_im", "c_re", "c_im", "d_s5", "w_glu", "b_glu", "g_s5", "w_out", "g_mlp", "w_up", "w_down", "g_final"]
    grads_out = [grads[n].reshape(given[n].shape) for n in order]
    return (loss, grad_x, *grads_out, *[delta[n] for n in order], *[new_m[n] for n in order], *[new_v[n] for n in order])
```

```python
import math

import jax
import jax.numpy as jnp
from jax import lax
from jax.experimental import pallas as pl
from jax.experimental.pallas import tpu as pltpu

f32 = jnp.float32
bf16 = jnp.bfloat16

D = 1024
N_META = 16
CH = 256
HEADS = 16
HEAD_DIM = 64
NSTATE = 128
D_XBC = 1536
S5_P = 64
NQ = 8
PITCH = CH + 4
EPS = 1e-5
O_Z, O_U, O_XBC, O_DT, W_PROJ = 0, 1024, 2048, 3584, 3712
VMEM_LIMIT = 60 * 1024 * 1024

ADAM_LR, ADAM_B1, ADAM_B2, ADAM_EPS, ADAM_WD, ADAM_STEP = 0.001, 0.9, 0.999, 1e-08, 0.01, 10

NT = (((1,), (1,)), ((), ()))
TN = (((0,), (0,)), ((), ()))
_ANY = pl.BlockSpec(memory_space=pl.ANY)


def _cp(sem=None):
    return pltpu.CompilerParams(dimension_semantics=sem, vmem_limit_bytes=VMEM_LIMIT)


def _sigmoid(v):
    return 1.0 / (1.0 + jnp.exp(-v))


def _rsum8(v):
    r, c = v.shape
    return jnp.sum(v.reshape(r // 8, 8, c), axis=0)


def _rms(h, g):
    r = lax.rsqrt(jnp.mean(h * h, axis=-1, keepdims=True) + EPS)
    return h * r * g


def _rms_bwd(dy, h, g):
    r = lax.rsqrt(jnp.mean(h * h, axis=-1, keepdims=True) + EPS)
    n = h * r
    dn = dy * g
    dh = r * (dn - n * jnp.mean(dn * n, axis=-1, keepdims=True))
    return dh, _rsum8(dy * n)


def _dot(a, b, dims=None):
    if dims is None:
        return jnp.dot(a, b, preferred_element_type=f32)
    return lax.dot_general(a, b, dims, preferred_element_type=f32)


def _split_dot(v, m01, dims, terms, v_is_lhs=True):
    out, r = None, v
    for _ in range(terms):
        piece = r.astype(bf16)
        o = _dot(piece, m01, dims) if v_is_lhs else _dot(m01, piece, dims)
        out = o if out is None else out + o
        r = r - piece.astype(f32)
    return out


MESH = pl.DeviceIdType.MESH


def _place():
    return lax.axis_index("x"), lax.axis_index("y"), lax.axis_index("c")


def _flip(v, f):
    return 1 - v if f else v


def _call(body, comm, *, name, nsteps, in_specs, out_specs, out_shape, scratch_shapes, args):
    n_in, n_out, n_scr = len(in_specs), len(out_specs), len(scratch_shapes)
    if comm is None:
        res = pl.pallas_call(body, name=name, grid=(nsteps,), in_specs=in_specs, out_specs=out_specs, out_shape=out_shape,
                             scratch_shapes=scratch_shapes, compiler_params=_cp(("arbitrary",)))(*args)
        return list(res), []
    c_in, c_out = len(comm["ins"]), len(comm["outs"])

    def wrapped(*refs):
        o0 = n_in + c_in
        s0 = o0 + n_out + c_out
        cparts = (refs[n_in:o0], refs[o0 + n_out:s0], refs[s0 + n_scr:])

        @pl.when(pl.program_id(0) == 0)
        def _():
            comm["start"](*cparts)
        if "middle" in comm:
            @pl.when(pl.program_id(0) == (3 * nsteps) // 4)
            def _():
                comm["middle"](*cparts)
        body(*refs[:n_in], *refs[o0:o0 + n_out], *refs[s0:s0 + n_scr])

        @pl.when(pl.program_id(0) == nsteps - 1)
        def _():
            comm["finish"](*cparts)

    any_spec = pl.BlockSpec(memory_space=pl.ANY)
    res = pl.pallas_call(
        wrapped, name=name, grid=(nsteps,), in_specs=list(in_specs) + [any_spec] * c_in,
        out_specs=list(out_specs) + [any_spec] * c_out, out_shape=list(out_shape) + list(comm["outs"]),
        scratch_shapes=list(scratch_shapes) + list(comm["scratch"]),
        compiler_params=_cp(("arbitrary",)))(*args, *comm["ins"])
    return list(res[:n_out]), list(res[n_out:])


def _gather_piece(slab):
    r0, rows = 0, slab.shape[0]
    half = rows // 2
    flips = ((1, 0), (0, 1), (1, 1))

    def first(j, slab_ref, out_ref, send_sems, recv_sems):
        x, y, c = _place()
        return pltpu.make_async_remote_copy(
            src_ref=slab_ref.at[pl.ds(r0 + c * half, half), :], dst_ref=out_ref.at[2 * x + y, pl.ds(c * half, half), :],
            send_sem=send_sems.at[j], recv_sem=recv_sems.at[j],
            device_id=(_flip(x, flips[j][0]), _flip(y, flips[j][1]), c), device_id_type=MESH)

    def passed(j, out_ref, send_sems, recv_sems):
        x, y, c = _place()
        rows_j = out_ref.at[2 * _flip(x, flips[j][0]) + _flip(y, flips[j][1]), pl.ds(c * half, half), :]
        return pltpu.make_async_remote_copy(src_ref=rows_j, dst_ref=rows_j, send_sem=send_sems.at[3 + j],
                                            recv_sem=recv_sems.at[3 + j], device_id=(x, y, 1 - c), device_id_type=MESH)

    def start(ins, outs, scr):
        send_sems, recv_sems, stage, local_sems = scr
        x, y, _ = _place()
        load = pltpu.make_async_copy(ins[0].at[pl.ds(r0, rows), :], stage, local_sems.at[0])
        load.start()
        for j in range(3):
            first(j, ins[0], outs[0], send_sems, recv_sems).start()
        load.wait()
        pltpu.make_async_copy(stage, outs[0].at[2 * x + y], local_sems.at[1]).start()

    def middle(ins, outs, scr):
        send_sems, recv_sems, _, _ = scr
        for j in range(3):
            first(j, ins[0], outs[0], send_sems, recv_sems).wait_recv()
            passed(j, outs[0], send_sems, recv_sems).start()

    def finish(ins, outs, scr):
        send_sems, recv_sems, stage, local_sems = scr
        x, y, c = _place()
        for j in range(3):
            sib = outs[0].at[2 * _flip(x, flips[j][0]) + _flip(y, flips[j][1]), pl.ds((1 - c) * half, half), :]
            pltpu.make_async_remote_copy(src_ref=sib, dst_ref=sib, send_sem=send_sems.at[3 + j], recv_sem=recv_sems.at[3 + j],
                                         device_id=(x, y, 1 - c), device_id_type=MESH).wait_recv()
        for j in range(3):
            first(j, ins[0], outs[0], send_sems, recv_sems).wait_send()
            passed(j, outs[0], send_sems, recv_sems).wait_send()
        pltpu.make_async_copy(stage, outs[0].at[2 * x + y], local_sems.at[1]).wait()

    return dict(ins=[slab], outs=[jax.ShapeDtypeStruct((4, rows, D), bf16)],
                scratch=[pltpu.SemaphoreType.DMA((6,)), pltpu.SemaphoreType.DMA((6,)), pltpu.VMEM((rows, D), bf16),
                         pltpu.SemaphoreType.DMA((2,))], start=start, middle=middle, finish=finish)


def _scatter_piece(gpiece):
    half = gpiece.shape[1] // 2

    def copies(g_ref, land_ref, send_sems, recv_sems):
        x, y, c = _place()
        cps = []
        for fx in (0, 1):
            for fy in (0, 1):
                for fc in (0, 1):
                    k = 4 * fx + 2 * fy + fc - 1
                    if k < 0:
                        continue
                    px, py, pc = _flip(x, fx), _flip(y, fy), _flip(c, fc)
                    cps.append(pltpu.make_async_remote_copy(
                        src_ref=g_ref.at[2 * px + py, pl.ds(pc * half, half), :], dst_ref=land_ref.at[k],
                        send_sem=send_sems.at[k], recv_sem=recv_sems.at[k], device_id=(px, py, pc), device_id_type=MESH))
        return cps

    def start(ins, outs, scr):
        for cp in copies(ins[0], outs[0], *scr):
            cp.start()

    def finish(ins, outs, scr):
        for cp in copies(ins[0], outs[0], *scr):
            cp.wait()

    return dict(ins=[gpiece], outs=[jax.ShapeDtypeStruct((7, half, D), gpiece.dtype)],
                scratch=[pltpu.SemaphoreType.DMA((7,)), pltpu.SemaphoreType.DMA((7,))], start=start, finish=finish)


def _gather_blocks(block):
    rows = block.shape[0]

    def mine(out_ref):
        x, y, c = _place()
        return out_ref.at[pl.ds((4 * x + 2 * y + c) * rows, rows), :]

    def copies(b_ref, out_ref, send_sems, recv_sems):
        x, y, c = _place()
        cps = []
        for fx in (0, 1):
            for fy in (0, 1):
                for fc in (0, 1):
                    k = 4 * fx + 2 * fy + fc - 1
                    if k < 0:
                        continue
                    cps.append(pltpu.make_async_remote_copy(
                        src_ref=b_ref, dst_ref=mine(out_ref), send_sem=send_sems.at[k], recv_sem=recv_sems.at[k],
                        device_id=(_flip(x, fx), _flip(y, fy), _flip(c, fc)), device_id_type=MESH))
        return cps

    def start(ins, outs, scr):
        send_sems, recv_sems, stage, local_sems = scr
        load = pltpu.make_async_copy(ins[0], stage, local_sems.at[0])
        load.start()
        for cp in copies(ins[0], outs[0], send_sems, recv_sems):
            cp.start()
        load.wait()
        pltpu.make_async_copy(stage, mine(outs[0]), local_sems.at[1]).start()

    def finish(ins, outs, scr):
        send_sems, recv_sems, stage, local_sems = scr
        for cp in copies(ins[0], outs[0], send_sems, recv_sems):
            cp.wait()
        pltpu.make_async_copy(stage, mine(outs[0]), local_sems.at[1]).wait()

    return dict(ins=[block], outs=[jax.ShapeDtypeStruct((8 * rows, D), block.dtype)],
                scratch=[pltpu.SemaphoreType.DMA((7,)), pltpu.SemaphoreType.DMA((7,)), pltpu.VMEM((rows, D), block.dtype),
                         pltpu.SemaphoreType.DMA((2,))], start=start, finish=finish)


def _both(c1, c2):
    n = (len(c1["ins"]), len(c1["outs"]), len(c1["scratch"]))

    def split(parts):
        return [p[:k] for p, k in zip(parts, n)], [p[k:] for p, k in zip(parts, n)]

    def start(*parts):
        a, b = split(parts)
        c1["start"](*a)
        c2["start"](*b)

    def finish(*parts):
        a, b = split(parts)
        c1["finish"](*a)
        c2["finish"](*b)

    both = dict(ins=c1["ins"] + c2["ins"], outs=c1["outs"] + c2["outs"], scratch=c1["scratch"] + c2["scratch"],
                start=start, finish=finish)
    if "middle" in c1 or "middle" in c2:
        def middle(*parts):
            for cm, part in zip((c1, c2), split(parts)):
                if "middle" in cm:
                    cm["middle"](*part)
        both["middle"] = middle
    return both


def _rowwise(name, fn, nblk, rows=(), shifted=(), pars=(), refs=(), out_rows=(), out_accs=(), bm=CH, comm=None):
    n_sub = bm // CH
    n_r, n_s, n_p, n_w = len(rows), len(shifted) * n_sub, len(pars), len(refs)
    n_in = n_r + n_s + n_p + n_w
    n_o, n_a = len(out_rows), len(out_accs)

    def body(*all_refs):
        i = pl.program_id(0)
        ins = all_refs[:n_in]
        outs = all_refs[n_in:]
        rv = [r[...] for r in ins[:n_r]]
        sub = ins[n_r:n_r + n_s]
        sv = [jnp.concatenate([r[...] for r in sub[k * n_sub:(k + 1) * n_sub]], axis=0) if n_sub > 1 else sub[k][...]
              for k in range(len(shifted))]
        pv = [r[...] for r in ins[n_r + n_s:n_r + n_s + n_p]]
        ro, ao = fn(i, rv, sv, pv, list(ins[n_r + n_s + n_p:]))
        for r, v in zip(outs[:n_o], ro):
            r[...] = v.astype(r.dtype)
        accs = outs[n_o:]

        @pl.when(i == 0)
        def _():
            for r in accs:
                r[...] = jnp.zeros_like(r)
        for r, v in zip(accs, ao):
            r[...] += v

    in_specs = [pl.BlockSpec((bm, a.shape[1]), lambda i: (i, 0)) for a in rows]
    in_specs += [pl.BlockSpec((CH, a.shape[1]), lambda i, j=j: (jnp.maximum(n_sub * i - 1 + j, 0), 0))
                 for a in shifted for j in range(n_sub)]
    in_specs += [pl.BlockSpec(a.shape, lambda i, nd=a.ndim: (0,) * nd) for a in pars]
    in_specs += [spec for _, spec in refs]
    drop = [len(o) > 2 for o in out_rows]
    assert not any(drop) or bm == CH
    out_specs = [pl.BlockSpec((bm, o[0]), (lambda i: (jnp.maximum(i - 1, 0), 0)) if d else (lambda i: (i, 0)))
                 for o, d in zip(out_rows, drop)]
    out_specs += [pl.BlockSpec((8, c), lambda i: (0, 0)) for c in out_accs]
    out_shape = [jax.ShapeDtypeStruct((nblk * bm - (CH if d else 0), o[0]), o[1]) for o, d in zip(out_rows, drop)]
    out_shape += [jax.ShapeDtypeStruct((8, c), f32) for c in out_accs]
    res, cres = _call(body, comm, name=name, nsteps=nblk, in_specs=in_specs, out_specs=out_specs, out_shape=out_shape,
                      scratch_shapes=[], args=[*rows, *[a for a in shifted for _ in range(n_sub)], *pars, *[a for a, _ in refs]])
    parts = (res[:n_o], res[n_o:])
    return parts if comm is None else parts + (cres,)


C_ROWS = 928
C_WINDOW = 912
BIG_ROWS = 768
DW_ROWS = 2816


def _contract_rows(lp, big=DW_ROWS):
    for rows in (big, BIG_ROWS):
        if lp % rows == 0:
            return rows
    return CH


def _dw_into(name, a, b, slab, ka, a_sharded, row_blk, n_s, s0, piece_rows=2048, a_square=False):
    lp = a.shape[0]
    bm = _contract_rows(lp)
    steps = lp // bm

    def body(a_ref, b_ref, *rest):
        o_ref, acc = rest[-2], rest[-1]
        k = pl.program_id(1)

        @pl.when(k == 0)
        def _():
            acc[...] = jnp.zeros_like(acc)
        a_v = a_ref[...]
        acc[...] += _dot(a_v * a_v if a_square else a_v, b_ref[...], TN)

        @pl.when(k == steps - 1)
        def _():
            o_ref[0] = acc[...].astype(bf16)

    in_specs = [pl.BlockSpec((bm, ka), (lambda s, k: (k, s)) if a_sharded else (lambda s, k: (k, 0))),
                pl.BlockSpec((bm, D), (lambda s, k: (k, 0)) if a_sharded else (lambda s, k: (k, s)))]
    args = [a, b]
    aliases = {}
    if slab is not None:
        in_specs.append(_ANY)
        args.append(slab)
        aliases = {2: 0}
    return pl.pallas_call(
        body, name=name, grid=(n_s, steps), in_specs=in_specs,
        out_specs=pl.BlockSpec((1, ka, D), lambda s, k: (s0 + s, row_blk, 0)),
        out_shape=jax.ShapeDtypeStruct((4, piece_rows, D), bf16),
        scratch_shapes=[pltpu.VMEM((ka, D), f32)], input_output_aliases=aliases,
        compiler_params=_cp(("arbitrary", "arbitrary")))(*args)


def _dw_in_t(dproj, n0):
    lp = n0.shape[0]
    bm = _contract_rows(lp, BIG_ROWS)
    steps = lp // bm
    bn = 512

    def body(a_ref, b_ref, o_ref, acc):
        k = pl.program_id(1)

        @pl.when(k == 0)
        def _():
            acc[...] = jnp.zeros_like(acc)
        acc[...] += _dot(a_ref[...], b_ref[...], TN)

        @pl.when(k == steps - 1)
        def _():
            o_ref[...] = acc[...].astype(bf16)

    return pl.pallas_call(
        body, name="dw_in", grid=(D // bn, steps),
        in_specs=[pl.BlockSpec((bm, W_PROJ), lambda j, k: (k, 0)), pl.BlockSpec((bm, bn), lambda j, k: (k, j))],
        out_specs=pl.BlockSpec((W_PROJ, bn), lambda j, k: (0, j)),
        out_shape=jax.ShapeDtypeStruct((W_PROJ, D), bf16),
        scratch_shapes=[pltpu.VMEM((W_PROJ, bn), f32)],
        compiler_params=_cp(("arbitrary", "arbitrary")))(dproj, n0)


def _head_expand():
    h = lax.broadcasted_iota(jnp.int32, (HEADS, D), 0)
    c = lax.broadcasted_iota(jnp.int32, (HEADS, D), 1)
    return jnp.where((c >> 6) == h, 1.0, 0.0).astype(bf16)


def _ssd_common(i, P, prev8, cw, cb, dtb, alog, xc=None):
    z = P[:, O_Z:O_Z + D]
    xp = P[:, O_XBC:O_XBC + D_XBC]
    dt_raw = P[:, O_DT:O_DT + HEADS]
    row = lax.broadcasted_iota(jnp.int32, (CH, 1), 0)
    if xc is None:
        row8 = lax.broadcasted_iota(jnp.int32, (8, 1), 0)
        xc = cb + cw[3:4] * xp
        for k in (1, 2, 3):
            rolled = pltpu.roll(xp, k, 0)
            fix = pltpu.roll(prev8, k, 0)
            top = jnp.where(row8 < k, fix, rolled[0:8])
            xc = xc + cw[3 - k:4 - k] * jnp.concatenate([top, rolled[8:]], axis=0)
    sg = _sigmoid(xc)
    xbc = xc * sg
    live = jnp.where(jnp.logical_or(i > 0, row >= CH - N_META), 1.0, 0.0)
    pre = dt_raw + dtb
    dt = jnp.where(pre > 20.0, pre, jnp.log(1.0 + jnp.exp(jnp.minimum(pre, 20.0)))) * live
    a = -jnp.exp(alog)
    dta = dt * a
    r_i = lax.broadcasted_iota(jnp.int32, (CH, CH), 0)
    c_i = lax.broadcasted_iota(jnp.int32, (CH, CH), 1)
    tril = r_i >= c_i
    acs = _split_dot(dta, jnp.where(tril, 1.0, 0.0).astype(bf16), None, 3, v_is_lhs=False)
    acs_t = _split_dot(dta, jnp.where(r_i <= c_i, 1.0, 0.0).astype(bf16), TN, 3)
    e = _head_expand()
    acs_e = _split_dot(acs, e, None, 3)
    dt_e = _split_dot(dt, e, None, 3)
    return dict(z=z, xp=xp, xc=xc, sg=sg, xbc=xbc, live=live, pre=pre, dt=dt, a=a, tril=tril,
                acs=acs, acs_t=acs_t, e=e, acs_e=acs_e, dt_e=dt_e)


def _lmat(c, h):
    seg = c["acs"][:, h:h + 1] - c["acs_t"][h:h + 1, :]
    return jnp.where(c["tril"], jnp.exp(jnp.minimum(seg, 0.0)), 0.0)


def _pair_masks():
    lane = lax.broadcasted_iota(jnp.int32, (1, 128), 1)
    return jnp.where(lane < HEAD_DIM, 1.0, 0.0), jnp.where(lane >= HEAD_DIM, 1.0, 0.0)


def _ssd_fwd(proj, conv_w, conv_b, dt_bias, a_log, d_ssd, g_ssd, nch, comm=None):
    def body(p_ref, cw_ref, cb_ref, dtb_ref, al_ref, d_ref, g_ref, y_ref, ys_ref, st_ref, xc_ref, prev8_ref, state_ref):
        i = pl.program_id(0)

        @pl.when(i == 0)
        def _():
            prev8_ref[...] = jnp.zeros_like(prev8_ref)
            state_ref[...] = jnp.zeros_like(state_ref)

        P = p_ref[...]
        c = _ssd_common(i, P, prev8_ref[...], cw_ref[...], cb_ref[...], dtb_ref[...], al_ref[...])
        prev8_ref[...] = c["xp"][CH - 8:CH]
        xc_ref[...] = c["xc"]
        xbc = c["xbc"]
        x = xbc[:, 0:D]
        xdt = x * c["dt_e"]
        a_last_e = c["acs_e"][CH - 1:CH, :]
        w_end = (xdt * jnp.exp(a_last_e - c["acs_e"])).astype(bf16)
        m0, m1 = _pair_masks()
        ys = []
        for g in range(2):
            bg = xbc[:, D + NSTATE * g:D + NSTATE * (g + 1)].astype(bf16)
            cg = xbc[:, D + 2 * NSTATE + NSTATE * g:D + 2 * NSTATE + NSTATE * (g + 1)].astype(bf16)
            gmat = _dot(cg, bg, NT)
            st = state_ref[g]
            st_ref[0, g] = st
            sl = slice(512 * g, 512 * (g + 1))
            y_off = _dot(cg, st.astype(bf16)) * jnp.exp(c["acs_e"][:, sl])
            contrib = _dot(bg, w_end[:, sl], TN)
            state_ref[g] = st * jnp.exp(a_last_e[:, sl]) + contrib
            yd = []
            for pr in range(4):
                h0 = 8 * g + 2 * pr
                xp2 = xdt[:, 128 * (4 * g + pr):128 * (4 * g + pr + 1)]
                ma = (gmat * _lmat(c, h0)).astype(bf16)
                mb = (gmat * _lmat(c, h0 + 1)).astype(bf16)
                yd.append(_dot(ma, (xp2 * m0).astype(bf16)) + _dot(mb, (xp2 * m1).astype(bf16)))
            ys.append(jnp.concatenate(yd, axis=1) + y_off)
        d_e = _split_dot(d_ref[...], c["e"], None, 3)
        y = jnp.concatenate(ys, axis=1) + x * d_e
        y_ref[...] = y
        yg = y * (c["z"] * _sigmoid(c["z"]))
        ys_ref[...] = _rms(yg, g_ref[...]).astype(bf16)

    full = lambda a: pl.BlockSpec(a.shape, lambda i, nd=a.ndim: (0,) * nd)
    return _call(
        body, comm, name="ssd_fwd", nsteps=nch,
        in_specs=[pl.BlockSpec((CH, W_PROJ), lambda i: (i, 0))] + [full(a) for a in (conv_w, conv_b, dt_bias, a_log, d_ssd, g_ssd)],
        out_specs=[pl.BlockSpec((CH, D), lambda i: (i, 0)), pl.BlockSpec((CH, D), lambda i: (i, 0)),
                   pl.BlockSpec((1, 2, NSTATE, 512), lambda i: (i, 0, 0, 0)), pl.BlockSpec((CH, D_XBC), lambda i: (i, 0))],
        out_shape=[jax.ShapeDtypeStruct((nch * CH, D), f32), jax.ShapeDtypeStruct((nch * CH, D), bf16),
                   jax.ShapeDtypeStruct((nch, 2, NSTATE, 512), f32), jax.ShapeDtypeStruct((nch * CH, D_XBC), f32)],
        scratch_shapes=[pltpu.VMEM((8, D_XBC), f32), pltpu.VMEM((2, NSTATE, 512), f32)],
        args=[proj, conv_w, conv_b, dt_bias, a_log, d_ssd, g_ssd])


def _ssd_bwd(proj, xc_all, y, dys, du, states, conv_w, conv_b, dt_bias, a_log, d_ssd, g_ssd, nch, comm=None):
    def body(p_ref, xc_ref, y_ref, dys_ref, du_ref, st_ref, cw_ref, cb_ref, dtb_ref, al_ref, d_ref, g_ref,
             dp_ref, dcw_ref, dcb_ref, ddtb_ref, dal_ref, dd_ref, dg_ref, nxt8_ref, dst_ref):
        step = pl.program_id(0)
        i = nch - 1 - step

        @pl.when(step == 0)
        def _():
            nxt8_ref[...] = jnp.zeros_like(nxt8_ref)
            dst_ref[...] = jnp.zeros_like(dst_ref)
            for r in (dcw_ref, dcb_ref, ddtb_ref, dal_ref, dd_ref, dg_ref):
                r[...] = jnp.zeros_like(r)

        P = p_ref[...]
        c = _ssd_common(i, P, None, cw_ref[...], cb_ref[...], dtb_ref[...], al_ref[...], xc=xc_ref[...])
        xbc, z, e = c["xbc"], c["z"], c["e"]
        x = xbc[:, 0:D]
        yv = y_ref[...]
        sz = _sigmoid(z)
        silu_z = z * sz
        dyg, dg8 = _rms_bwd(dys_ref[...], yv * silu_z, g_ref[...])
        dg_ref[...] += dg8
        dy = dyg * silu_z
        dz = dyg * yv * (sz * (1.0 + z * (1.0 - sz)))
        d_e = _split_dot(d_ref[...], e, None, 3)
        dd_ref[...] += _rsum8(dy * x)
        xdt = x * c["dt_e"]
        a_last_e = c["acs_e"][CH - 1:CH, :]
        e_end = jnp.exp(a_last_e - c["acs_e"])
        w_end = xdt * e_end
        e_acs = jnp.exp(c["acs_e"])
        dy_dec = dy * e_acs
        m0, m1 = _pair_masks()
        lane16 = lax.broadcasted_iota(jnp.int32, (1, HEADS), 1)
        row16 = lax.broadcasted_iota(jnp.int32, (HEADS, 1), 0)
        dacs = jnp.zeros((CH, HEADS), f32)
        dacs_t = jnp.zeros((HEADS, CH), f32)
        dxdt_parts, dbs, dcs, zparts, yoff_parts, dlast_parts = [], [], [], [], [], []
        for g in range(2):
            sl = slice(512 * g, 512 * (g + 1))
            bg = xbc[:, D + NSTATE * g:D + NSTATE * (g + 1)].astype(bf16)
            cg = xbc[:, D + 2 * NSTATE + NSTATE * g:D + 2 * NSTATE + NSTATE * (g + 1)].astype(bf16)
            gmat = _dot(cg, bg, NT)
            st = st_ref[0, g]
            dstn = dst_ref[g]
            dstn_b = dstn.astype(bf16)
            y_off = _dot(cg, st.astype(bf16)) * e_acs[:, sl]
            yoff_parts.append(y_off)
            bds = _dot(bg, dstn_b)
            zparts.append(w_end[:, sl] * bds)
            dlast_parts.append(jnp.sum(dstn * st, axis=0, keepdims=True) * jnp.exp(a_last_e[:, sl]))
            dg_acc = jnp.zeros((CH, CH), f32)
            dxd = []
            for pr in range(4):
                lo = 128 * (4 * g + pr)
                xp2 = xdt[:, lo:lo + 128].astype(bf16)
                dy2 = dy[:, lo:lo + 128]
                outp = jnp.zeros((CH, 128), f32)
                for hh, msk in ((0, m0), (1, m1)):
                    h = 8 * g + 2 * pr + hh
                    lm = _lmat(c, h)
                    dyh = (dy2 * msk).astype(bf16)
                    mh = (gmat * lm).astype(bf16)
                    outp = outp + _dot(mh, dyh, TN)
                    dml = _dot(dyh, xp2, NT) * lm
                    dg_acc = dg_acc + dml
                    q = dml * gmat
                    dacs = dacs + jnp.where(lane16 == h, jnp.sum(q, axis=1, keepdims=True), 0.0)
                    dacs_t = dacs_t + jnp.where(row16 == h, jnp.sum(q, axis=0, keepdims=True), 0.0)
                dxd.append(outp)
            dxdt_parts.append(jnp.concatenate(dxd, axis=1) + e_end[:, sl] * bds)
            dgb = dg_acc.astype(bf16)
            dcs.append(_dot(dgb, bg) + _dot(dy_dec[:, sl].astype(bf16), st.astype(bf16), NT))
            dbs.append(_dot(dgb, cg, TN) + _dot(w_end[:, sl].astype(bf16), dstn_b, NT))
            dst_ref[g] = dstn * jnp.exp(a_last_e[:, sl]) + _dot(cg, dy_dec[:, sl].astype(bf16), TN)
        dxdt = jnp.concatenate(dxdt_parts, axis=1)
        zfull = jnp.concatenate(zparts, axis=1)
        y_off_full = jnp.concatenate(yoff_parts, axis=1)
        dlast = jnp.concatenate(dlast_parts, axis=1)
        red = lambda v: _split_dot(v, e, NT, 2)
        eye16 = jnp.where(lax.broadcasted_iota(jnp.int32, (HEADS, HEADS), 0) == lax.broadcasted_iota(jnp.int32, (HEADS, HEADS), 1),
                          1.0, 0.0).astype(bf16)
        dacs = dacs - _split_dot(dacs_t, eye16, TN, 3)
        zred = red(zfull)
        dacs = dacs + red(dy * y_off_full) - zred
        last_term = jnp.sum(zred, axis=0, keepdims=True) + red(dlast)
        rowc = lax.broadcasted_iota(jnp.int32, (CH, 1), 0)
        dacs = dacs + jnp.where(rowc == CH - 1, last_term, 0.0)
        r_i = lax.broadcasted_iota(jnp.int32, (CH, CH), 0)
        c_i = lax.broadcasted_iota(jnp.int32, (CH, CH), 1)
        ddta = _split_dot(dacs, jnp.where(c_i >= r_i, 1.0, 0.0).astype(bf16), None, 3, v_is_lhs=False)
        ddt = ddta * c["a"] + red(dxdt * x)
        dal_ref[...] += _rsum8(ddta * c["dt"] * c["a"])
        ddt_raw = ddt * _sigmoid(c["pre"]) * c["live"]
        ddtb_ref[...] += _rsum8(ddt_raw)
        dx = dy * d_e + dxdt * c["dt_e"]
        dxbc = jnp.concatenate([dx, dbs[0], dbs[1], dcs[0], dcs[1]], axis=1)
        sg = c["sg"]
        dxc = dxbc * (sg * (1.0 + c["xc"] * (1.0 - sg)))
        dcb_ref[...] += _rsum8(dxc)
        xp = c["xp"]
        row8 = lax.broadcasted_iota(jnp.int32, (8, 1), 0)
        cw = cw_ref[...]
        dxp = cw[3:4] * dxc
        dcw = jnp.where(row8 == 3, jnp.sum(dxc * xp, axis=0, keepdims=True), 0.0)
        nxt8 = nxt8_ref[...]
        for j in (1, 2, 3):
            rolled = pltpu.roll(dxc, CH - j, 0)
            fix = pltpu.roll(nxt8, 8 - j, 0)
            bot = jnp.where(row8 >= 8 - j, fix, rolled[CH - 8:CH])
            later = jnp.concatenate([rolled[:CH - 8], bot], axis=0)
            dxp = dxp + cw[3 - j:4 - j] * later
            dcw = dcw + jnp.where(row8 == 3 - j, jnp.sum(later * xp, axis=0, keepdims=True), 0.0)
        dcw_ref[...] += dcw
        nxt8_ref[...] = dxc[0:8]
        dp_ref[:, O_Z:O_Z + D] = dz.astype(bf16)
        dp_ref[:, O_U:O_U + D] = du_ref[...].astype(bf16)
        dp_ref[:, O_XBC:O_XBC + D_XBC] = dxp.astype(bf16)
        dp_ref[:, O_DT:W_PROJ] = jnp.zeros((CH, W_PROJ - O_DT), bf16)
        dp_ref[:, O_DT:O_DT + HEADS] = ddt_raw.astype(bf16)

    full = lambda a: pl.BlockSpec(a.shape, lambda s, nd=a.ndim: (0,) * nd)
    rev = lambda s: (nch - 1 - s, 0)
    acc = lambda cdim: pl.BlockSpec((8, cdim), lambda s: (0, 0))
    return _call(
        body, comm, name="ssd_bwd", nsteps=nch,
        in_specs=[pl.BlockSpec((CH, W_PROJ), rev), pl.BlockSpec((CH, D_XBC), rev),
                  pl.BlockSpec((CH, D), rev), pl.BlockSpec((CH, D), rev), pl.BlockSpec((CH, D), rev),
                  pl.BlockSpec((1, 2, NSTATE, 512), lambda s: (nch - 1 - s, 0, 0, 0))]
        + [full(a) for a in (conv_w, conv_b, dt_bias, a_log, d_ssd, g_ssd)],
        out_specs=[pl.BlockSpec((CH, W_PROJ), rev), acc(D_XBC), acc(D_XBC), acc(HEADS), acc(HEADS), acc(D), acc(D)],
        out_shape=[jax.ShapeDtypeStruct((nch * CH, W_PROJ), bf16)]
        + [jax.ShapeDtypeStruct((8, cdim), f32) for cdim in (D_XBC, D_XBC, HEADS, HEADS, D, D)],
        scratch_shapes=[pltpu.VMEM((8, D_XBC), f32), pltpu.VMEM((2, NSTATE, 512), f32)],
        args=[proj, xc_all, y, dys, du, states, conv_w, conv_b, dt_bias, a_log, d_ssd, g_ssd])


SCAN_UNROLL = 8


def _to_slabs(slab_ref, q, mat):
    for ls in range(8):
        slab_ref[ls, pl.ds(PITCH * q, CH), :] = mat[:, 128 * ls:128 * (ls + 1)]


def _from_slabs(slab, q):
    return jnp.concatenate([slab(ls, PITCH * q) for ls in range(8)], axis=1)


def _tile(slab_ref, ls, t, lead=None):
    idx = (ls, pl.ds(t, 8, stride=PITCH), slice(None))
    return slab_ref[idx] if lead is None else slab_ref[(lead,) + idx]


def _s5_fwd(proj, bbq, ccq, ar, ai, d_skip, nch, comm=None):
    def body(u_ref, bb_ref, cc_ref, ar_ref, ai_ref, d_ref, s_ref, yl_ref, y5_ref, bu_ref, st_ref):
        @pl.when(pl.program_id(0) == 0)
        def _():
            st_ref[...] = jnp.zeros_like(st_ref)
        u = u_ref[...]
        ub = u.astype(bf16)
        for q in range(NQ):
            _to_slabs(bu_ref, q, _dot(ub[:, 128 * q:128 * (q + 1)], bb_ref[q]))
        ar_t = [ar_ref[:, 128 * l:128 * (l + 1)] for l in range(4)]
        ai_t = [ai_ref[:, 128 * l:128 * (l + 1)] for l in range(4)]

        def one(t, carry):
            re, im = carry
            nre, nim = [], []
            for l in range(4):
                a = ar_t[l] * re[l] - ai_t[l] * im[l] + _tile(bu_ref, l, t)
                b = ar_t[l] * im[l] + ai_t[l] * re[l] + _tile(bu_ref, l + 4, t)
                s_ref[0, l, pl.ds(t, 8, stride=PITCH), :] = a
                s_ref[0, l + 4, pl.ds(t, 8, stride=PITCH), :] = b
                nre.append(a)
                nim.append(b)
            return tuple(nre), tuple(nim)

        def step(tt, carry):
            for k in range(SCAN_UNROLL):
                carry = one(tt * SCAN_UNROLL + k, carry)
            return carry
        init = (tuple(st_ref[l] for l in range(4)), tuple(st_ref[l + 4] for l in range(4)))
        re, im = lax.fori_loop(0, CH // SCAN_UNROLL, step, init)
        for l in range(4):
            st_ref[l] = re[l]
            st_ref[l + 4] = im[l]
        ys = []
        for q in range(NQ):
            sq = _from_slabs(lambda ls, r0: s_ref[0, ls, pl.ds(r0, CH), :], q).astype(bf16)
            ys.append(_dot(sq, cc_ref[q], NT))
        yl = jnp.concatenate(ys, axis=1) + u * d_ref[...]
        yl_ref[...] = yl
        y5_ref[...] = (0.5 * yl * (1.0 + lax.erf(yl * (1.0 / math.sqrt(2.0))))).astype(bf16)

    const = lambda a: pl.BlockSpec(a.shape, lambda i, nd=a.ndim: (0,) * nd)
    return _call(
        body, comm, name="s5_fwd", nsteps=nch,
        in_specs=[pl.BlockSpec((CH, D), lambda i: (i, O_U // D)), const(bbq), const(ccq), const(ar), const(ai), const(d_skip)],
        out_specs=[pl.BlockSpec((1, 8, 8 * PITCH, 128), lambda i: (i, 0, 0, 0)),
                   pl.BlockSpec((CH, D), lambda i: (i, 0)), pl.BlockSpec((CH, D), lambda i: (i, 0))],
        out_shape=[jax.ShapeDtypeStruct((nch, 8, 8 * PITCH, 128), f32), jax.ShapeDtypeStruct((nch * CH, D), f32),
                   jax.ShapeDtypeStruct((nch * CH, D), bf16)],
        scratch_shapes=[pltpu.VMEM((8, 8 * PITCH, 128), f32), pltpu.VMEM((8, 8, 128), f32)],
        args=[proj, bbq, ccq, ar, ai, d_skip])


def _s5_bwd(proj, dyl, s_all, bbq, ccq, ar, ai, d_skip, nch, comm=None):
    def body(u_ref, dy_ref, s_ref, bbt_ref, cct_ref, ar_ref, ai_ref, d_ref,
             du_ref, dcc_ref, dbb_ref, dab_ref, dd_ref, ga_ref, st_ref):
        @pl.when(pl.program_id(0) == 0)
        def _():
            st_ref[...] = jnp.zeros_like(st_ref)
            for r in (dcc_ref, dbb_ref, dab_ref, dd_ref):
                r[...] = jnp.zeros_like(r)
        u = u_ref[...]
        dyl_v = dy_ref[...]
        dd_ref[...] += _rsum8(dyl_v * u)
        ub = u.astype(bf16)
        dyb = dyl_v.astype(bf16)
        for q in range(NQ):
            _to_slabs(ga_ref, q, _dot(dyb[:, 128 * q:128 * (q + 1)], cct_ref[q]))
        ar_t = [ar_ref[:, 128 * l:128 * (l + 1)] for l in range(4)]
        ai_t = [ai_ref[:, 128 * l:128 * (l + 1)] for l in range(4)]

        def one(t, carry):
            re, im, dar, dai = carry
            nre, nim, ndar, ndai = [], [], [], []
            for l in range(4):
                sre = _tile(s_ref, l, t, lead=0)
                sim = _tile(s_ref, l + 4, t, lead=0)
                ndar.append(dar[l] + re[l] * sre + im[l] * sim)
                ndai.append(dai[l] + im[l] * sre - re[l] * sim)
                a = _tile(ga_ref, l, t) + ar_t[l] * re[l] + ai_t[l] * im[l]
                b = _tile(ga_ref, l + 4, t) - ai_t[l] * re[l] + ar_t[l] * im[l]
                ga_ref[l, pl.ds(t, 8, stride=PITCH), :] = a
                ga_ref[l + 4, pl.ds(t, 8, stride=PITCH), :] = b
                nre.append(a)
                nim.append(b)
            return tuple(nre), tuple(nim), tuple(ndar), tuple(ndai)

        def step(tt, carry):
            for k in range(SCAN_UNROLL):
                carry = one(CH - 1 - (tt * SCAN_UNROLL + k), carry)
            return carry
        four = lambda ref, o: tuple(ref[l + o] for l in range(4))
        re, im, dar, dai = lax.fori_loop(0, CH // SCAN_UNROLL, step,
                                         (four(st_ref, 0), four(st_ref, 4), four(dab_ref, 0), four(dab_ref, 4)))
        for l in range(4):
            st_ref[l], st_ref[l + 4] = re[l], im[l]
            dab_ref[l], dab_ref[l + 4] = dar[l], dai[l]
        dus = []
        for q in range(NQ):
            aq = _from_slabs(lambda ls, r0: ga_ref[ls, pl.ds(r0, CH), :], q).astype(bf16)
            sq = _from_slabs(lambda ls, r0: s_ref[0, ls, pl.ds(r0, CH), :], q).astype(bf16)
            dcc_ref[q] += _dot(dyb[:, 128 * q:128 * (q + 1)], sq, TN)
            dbb_ref[q] += _dot(ub[:, 128 * q:128 * (q + 1)], aq, TN)
            dus.append(_dot(aq, bbt_ref[q], NT))
        du_ref[...] = jnp.concatenate(dus, axis=1) + dyl_v * d_ref[...]

    const = lambda a: pl.BlockSpec(a.shape, lambda s, nd=a.ndim: (0,) * nd)
    rev = lambda s: (nch - 1 - s, 0)
    return _call(
        body, comm, name="s5_bwd", nsteps=nch,
        in_specs=[pl.BlockSpec((CH, D), lambda s: (nch - 1 - s, O_U // D)), pl.BlockSpec((CH, D), rev),
                  pl.BlockSpec((1, 8, 8 * PITCH, 128), lambda s: (nch - 1 - s, 0, 0, 0)),
                  const(bbq), const(ccq), const(ar), const(ai), const(d_skip)],
        out_specs=[pl.BlockSpec((CH, D), rev), pl.BlockSpec((NQ, 128, D), lambda s: (0, 0, 0)),
                   pl.BlockSpec((NQ, 128, D), lambda s: (0, 0, 0)), pl.BlockSpec((8, 8, 128), lambda s: (0, 0, 0)),
                   pl.BlockSpec((8, D), lambda s: (0, 0))],
        out_shape=[jax.ShapeDtypeStruct((nch * CH, D), f32), jax.ShapeDtypeStruct((NQ, 128, D), f32),
                   jax.ShapeDtypeStruct((NQ, 128, D), f32), jax.ShapeDtypeStruct((8, 8, 128), f32),
                   jax.ShapeDtypeStruct((8, D), f32)],
        scratch_shapes=[pltpu.VMEM((8, 8 * PITCH, 128), f32), pltpu.VMEM((8, 8, 128), f32)],
        args=[proj, dyl, s_all, bbq, ccq, ar, ai, d_skip])


def _s5_tables(lam_re, lam_im, log_step, b_re, b_im):
    step = jnp.exp(log_step)[:, None]
    mag = jnp.exp(lam_re * step)
    ab_re = mag * jnp.cos(lam_im * step)
    ab_im = mag * jnp.sin(lam_im * step)
    den = lam_re * lam_re + lam_im * lam_im
    coef_re = ((ab_re - 1.0) * lam_re + ab_im * lam_im) / den
    coef_im = (ab_im * lam_re - (ab_re - 1.0) * lam_im) / den
    bb_re = coef_re[..., None] * b_re - coef_im[..., None] * b_im
    bb_im = coef_re[..., None] * b_im + coef_im[..., None] * b_re
    return ab_re, ab_im, bb_re, bb_im


def _blockdiag_in(m_re, m_im):
    eye = jnp.eye(8, dtype=f32)

    def one(m):
        m = m.reshape(NQ, 8, S5_P, 16)
        return jnp.einsum("qgph,gk->qghkp", m, eye).reshape(NQ, 128, 512)
    return jnp.concatenate([one(m_re), one(m_im)], axis=2)


def _blockdiag_in_grad(dm):
    rows = dm.reshape(NQ, 8, 16, 1024)
    shape = (8, 1, 1024)
    diag = (lax.broadcasted_iota(jnp.int32, shape, 2) // S5_P) % 8 == lax.broadcasted_iota(jnp.int32, shape, 0)
    z = jnp.sum(jnp.where(diag, rows, 0.0), axis=1)

    def one(part):
        return jnp.transpose(part.reshape(NQ, 16, 8, S5_P), (0, 2, 3, 1)).reshape(NQ * 8, S5_P, 16)
    return one(z[:, :, :512]), one(z[:, :, 512:])


def _local_step(x2, tgt2, meta, p, w_in_t, shards):
    seq = x2.shape[0]
    nch = 1 + seq // CH
    bmb = BIG_ROWS if (nch * CH) % BIG_ROWS == 0 else CH
    nbig = nch * CH // bmb
    metablk = jnp.concatenate([jnp.zeros((CH - N_META, D), f32), meta, jnp.zeros((bmb - CH, D), f32)], axis=0)
    w_full = (w_in_t, pl.BlockSpec(w_in_t.shape, lambda i: (0, 0), pipeline_mode=pl.Buffered(1)))

    def lead(i, v):
        return jnp.logical_and(i == 0, lax.broadcasted_iota(jnp.int32, (v.shape[0], 1), 0) < CH)
    h0_of = lambda i, s, q: jnp.where(lead(i, s[0]), q[0][:s[0].shape[0]], s[0])

    def in_fn(i, r, s, q, w):
        nb = _rms(h0_of(i, s, q), q[1]).astype(bf16)
        return [_dot(nb, w[0][...], NT), nb], []
    (proj, n0), _, (g_up,) = _rowwise("in_proj", in_fn, nbig, shifted=[x2], pars=[metablk, p["g_mix"]], refs=[w_full],
                                      out_rows=[(W_PROJ, f32), (D, bf16)], bm=bmb, comm=_gather_piece(shards[0]))
    (y, y_ssd, states, xc_all), (g_down,) = _ssd_fwd(proj, p["conv_w"], p["conv_b"], p["dt_bias"], p["a_log"], p["d_ssd"],
                                                     p["g_ssd"], nch, comm=_gather_piece(shards[1]))

    ab_re, ab_im, bb_re, bb_im = _s5_tables(p["lam_re"], p["lam_im"], p["log_step"], p["b_re"], p["b_im"])
    ar, ai = ab_re.reshape(NQ, 512), ab_im.reshape(NQ, 512)
    bbq = _blockdiag_in(bb_re, bb_im)
    ccq = _blockdiag_in(jnp.swapaxes(p["c_re"], 1, 2), -jnp.swapaxes(p["c_im"], 1, 2))
    d_skip = p["d_s5"].reshape(1, D)
    bbq_b, ccq_b = bbq.astype(bf16), ccq.astype(bf16)
    (s_all, ylin, y5), (g_go,) = _s5_fwd(proj, bbq_b, ccq_b, ar, ai, d_skip, nch, comm=_gather_piece(shards[2]))
    whole = lambda a: (a, pl.BlockSpec(a.shape, lambda i: (0, 0, 0), pipeline_mode=pl.Buffered(1)))
    w_up, w_down = whole(g_up), whole(g_down)
    w_glu_t = (g_go, pl.BlockSpec((4, 512, D), lambda i: (0, 0, 0), pipeline_mode=pl.Buffered(1)))
    w_out = (g_go, pl.BlockSpec((4, 512, D), lambda i: (0, 1, 0), pipeline_mode=pl.Buffered(1)))

    def glu_fn(i, r, s, q, w):
        v = jnp.concatenate([_dot(r[0], w[0][k], NT) for k in range(4)], axis=1) + q[0]
        return [v, _rms(v[:, :D] * _sigmoid(v[:, D:]), q[1])], []
    (v, y_s5), _ = _rowwise("glu", glu_fn, nbig, rows=[y5], pars=[p["b_glu"], p["g_s5"]], refs=[w_glu_t],
                            out_rows=[(2 * D, bf16), (D, bf16)], bm=bmb)

    def out_fn(i, r, s, q, w):
        acc = (_dot(r[0][:, :512], w[0][0]) + _dot(r[0][:, 512:], w[0][1]) + _dot(r[1][:, :512], w[0][2])
               + _dot(r[1][:, 512:], w[0][3]))
        return [h0_of(i, s, q) + acc], []
    (h1,), _ = _rowwise("out_proj", out_fn, nbig, rows=[y_ssd, y_s5], shifted=[x2], pars=[metablk], refs=[w_out],
                        out_rows=[(D, f32)], bm=bmb)

    def up_fn(i, r, s, q, w):
        nb = _rms(r[0], q[0]).astype(bf16)
        return [jnp.concatenate([jnp.maximum(_dot(nb, w[0][k]), 0.0).astype(bf16) for k in range(4)], axis=1), nb], []
    (relu_m, n1), _ = _rowwise("up_proj", up_fn, nbig, rows=[h1], pars=[p["g_mlp"]], refs=[w_up],
                               out_rows=[(4 * D, bf16), (D, bf16)], bm=bmb)

    def down_fn(i, r, s, q, w):
        acc = None
        for k in range(4):
            t = r[0][:, D * k:D * (k + 1)]
            part = _dot(t * t, w[0][k])
            acc = part if acc is None else acc + part
        return [r[1] + acc], []
    (h2,), _ = _rowwise("down_proj", down_fn, nbig, rows=[relu_m, h1], refs=[w_down], out_rows=[(D, f32)], bm=bmb)

    def final_fn(i, r, s, q, w):
        err = jnp.where(lead(i, r[0]), 0.0, _rms(r[0], q[0]) - s[0])
        dh, dg8 = _rms_bwd(err * (1.0 / D), r[0], q[0])
        return [dh, dh], [_rsum8(err * err), dg8]
    (dh2, dh2_b), (loss8, dgf8) = _rowwise("final", final_fn, nbig, rows=[h2], shifted=[tgt2], pars=[p["g_final"]],
                                           out_rows=[(D, f32), (D, bf16)], out_accs=[D, D], bm=bmb)
    loss = 0.5 / D * jnp.sum(loss8)

    def down_bwd_fn(i, r, s, q, w):
        dm_ = [_dot(r[0], w[0][k], NT) * (2.0 * r[1][:, D * k:D * (k + 1)].astype(f32)) for k in range(4)]
        return [jnp.concatenate(dm_, axis=1)], []
    (dm,), _ = _rowwise("down_bwd", down_bwd_fn, nbig, rows=[dh2_b, relu_m], refs=[w_down], out_rows=[(4 * D, bf16)], bm=bmb)
    g_a = _dw_into("dw_down", relu_m, dh2_b, None, 1024, True, 1, 4, 0, piece_rows=2048, a_square=True)

    def up_bwd_fn(i, r, s, q, w):
        acc = _dot(r[0][:, :D], w[0][0], NT)
        for k in range(1, 4):
            acc = acc + _dot(r[0][:, D * k:D * (k + 1)], w[0][k], NT)
        dh, dg8 = _rms_bwd(acc, r[1], q[0])
        dh1_ = r[2] + dh
        return [dh1_, dh1_], [dg8]
    (dh1, dh1_b), (dgmlp8,) = _rowwise("up_bwd", up_bwd_fn, nbig, rows=[dm, h1, dh2], pars=[p["g_mlp"]], refs=[w_up],
                                       out_rows=[(D, f32), (D, bf16)], out_accs=[D], bm=bmb)
    g_a = _dw_into("dw_up", n1, dm, g_a, 1024, False, 0, 4, 0, piece_rows=2048)

    def out_bwd_fn(i, r, s, q, w):
        dmix = [_dot(r[0], w[0][k], NT) for k in range(4)]
        v1, v2 = r[1][:, :D].astype(f32), r[1][:, D:].astype(f32)
        s2 = _sigmoid(v2)
        dglu, dg8 = _rms_bwd(jnp.concatenate(dmix[2:], axis=1), v1 * s2, q[0])
        dv = jnp.concatenate([dglu * s2, dglu * v1 * s2 * (1.0 - s2)], axis=1)
        return [jnp.concatenate(dmix[:2], axis=1), dv], [dg8, _rsum8(dv)]
    (dys, dv), (dgs58, dbglu8) = _rowwise("out_bwd", out_bwd_fn, nbig, rows=[dh1_b, v], pars=[p["g_s5"]], refs=[w_out],
                                          out_rows=[(D, f32), (2 * D, bf16)], out_accs=[D, 2 * D], bm=bmb)
    g_b = _dw_into("dw_out_a", y_ssd, dh1_b, None, 512, True, 1, 2, 0, piece_rows=1024)
    g_b = _dw_into("dw_out_b", y_s5, dh1_b, g_b, 512, True, 1, 2, 2, piece_rows=1024)

    def glu_bwd_fn(i, r, s, q, w):
        acc = _dot(r[0][:, :512], w[0][0])
        for k in range(1, 4):
            acc = acc + _dot(r[0][:, 512 * k:512 * (k + 1)], w[0][k])
        yl = r[1]
        cdf = 0.5 * (1.0 + lax.erf(yl * (1.0 / math.sqrt(2.0))))
        pdf = jnp.exp(-0.5 * yl * yl) * (1.0 / math.sqrt(2.0 * math.pi))
        return [acc * (cdf + yl * pdf)], []
    (dylin,), _ = _rowwise("glu_bwd", glu_bwd_fn, nbig, rows=[dv, ylin], refs=[w_glu_t], out_rows=[(D, f32)], bm=bmb)
    g_b = _dw_into("dw_glu", dv, y5, g_b, 512, True, 0, 4, 0, piece_rows=1024)

    (du, dcc, dbb, dab, dds5), (land_a,) = _s5_bwd(proj, dylin, s_all, bbq_b, ccq_b, ar, ai, d_skip, nch,
                                                   comm=_scatter_piece(g_a))

    s8 = lambda a: jnp.sum(a, axis=0, keepdims=True)
    dab_q = jnp.swapaxes(dab.reshape(2, 4, NQ, 128), 1, 2).reshape(2, NQ * 8, S5_P)
    dbb_re, dbb_im = _blockdiag_in_grad(dbb)
    dcr, dci = _blockdiag_in_grad(dcc)
    _, vjp = jax.vjp(_s5_tables, p["lam_re"], p["lam_im"], p["log_step"], p["b_re"], p["b_im"])
    dlam_re, dlam_im, dlog_step, db_re, db_im = vjp((dab_q[0], dab_q[1], dbb_re, dbb_im))
    early = dict(lam_re=dlam_re, lam_im=dlam_im, log_step=dlog_step, b_re=db_re, b_im=db_im, c_re=jnp.swapaxes(dcr, 1, 2),
                 c_im=-jnp.swapaxes(dci, 1, 2), d_s5=s8(dds5).reshape(NQ * 8, 16), b_glu=s8(dbglu8), g_s5=s8(dgs58),
                 g_mlp=s8(dgmlp8), g_final=s8(dgf8).reshape(D))
    early_pack = _pack_small([early[n] for n in EARLY], _rows_for(EARLY))

    (dproj, dcw8, dcb8, ddtb8, dal8, dd8, dgssd8), (land_b, all_early) = _ssd_bwd(
        proj, xc_all, y, dys, du, states, p["conv_w"], p["conv_b"], p["dt_bias"], p["a_log"], p["d_ssd"], p["g_ssd"], nch,
        comm=_both(_scatter_piece(g_b), _gather_blocks(early_pack)))

    gt = _dw_in_t(dproj, n0)
    windows = [[(0, 912)], [(896, 1024), (O_XBC, O_XBC + 784)], [(O_XBC + 768, O_DT + HEADS), (O_U, O_U + 128)],
               [(O_U + 112, O_U + D)]]
    assert all(sum(b - a for a, b in w) == C_WINDOW for w in windows)
    g_c = jnp.concatenate([part for w in windows for part in
                           [gt[a:b] for a, b in w] + [jnp.zeros((C_ROWS - C_WINDOW, D), bf16)]], axis=0).reshape(4, C_ROWS, D)

    def in_bwd_fn(i, r, s, q, w):
        dh, dg8 = _rms_bwd(_dot(r[0], w[0][...]), h0_of(i, s, q), q[1])
        dh0 = r[1] + dh
        dmeta = jnp.where(i == 0, dh0[CH - N_META:], 0.0)
        return [dh0], [dg8, dmeta[:8], dmeta[8:]]
    (grad_x,), (dgmix8, dmeta_a, dmeta_b), (land_c,) = _rowwise(
        "in_bwd", in_bwd_fn, nch, rows=[dproj, dh1], shifted=[x2], pars=[metablk, p["g_mix"]], refs=[w_full],
        out_rows=[(D, f32, "shifted")], out_accs=[D, D, D], bm=CH, comm=_scatter_piece(g_c))

    hsum = lambda a: jnp.sum(s8(a).reshape(HEADS, HEAD_DIM), axis=1).reshape(1, HEADS)
    late = dict(g_mix=s8(dgmix8), conv_b=s8(dcb8), dt_bias=s8(ddtb8), a_log=s8(dal8), d_ssd=hsum(dd8), g_ssd=s8(dgssd8),
                conv_w=dcw8[0:4], meta_tokens=jnp.concatenate([dmeta_a, dmeta_b], axis=0), loss=loss.reshape(1))
    return grad_x, [(g_a, land_a), (g_b, land_b), (g_c, land_c)], all_early, late


def _perm_rows_w_in(wt):
    return jnp.concatenate([wt[0:1024], wt[2576:3600], wt[1024:2560], wt[2560:2576],
                            jnp.zeros((W_PROJ - 3600, wt.shape[1]), wt.dtype)], axis=0)


def _allgather8(x_shard, name, swap=()):
    m_per, n = x_shard.shape
    n_sw = len(swap)

    def body(*refs):
        x_ref, out_ref = refs[n_sw], refs[2 * n_sw + 1]
        send_sems, recv_sems, stage, local_sems, swap_send, swap_recv = refs[2 * n_sw + 2:]
        x, y, c = _place()
        swaps = [pltpu.make_async_remote_copy(src_ref=refs[k], dst_ref=refs[n_sw + 1 + k], send_sem=swap_send.at[k],
                                              recv_sem=swap_recv.at[k], device_id=(x, y, 1 - c), device_id_type=MESH)
                 for k in range(n_sw)]
        for cp in swaps:
            cp.start()
        me, sibling = (x, y, c), (x, y, 1 - c)
        chips = [(1 - x, y), (x, 1 - y), (1 - x, 1 - y)]

        def rows(px, py, pc):
            return out_ref.at[pl.ds((4 * px + 2 * py + pc) * m_per, m_per), :]

        def copy(k, block, to, src=None):
            return pltpu.make_async_remote_copy(
                src_ref=rows(*block) if src is None else src, dst_ref=rows(*block),
                send_sem=send_sems.at[k], recv_sem=recv_sems.at[k], device_id=to, device_id_type=MESH)

        load = pltpu.make_async_copy(x_ref, stage, local_sems.at[0])
        load.start()
        first = [copy(0, me, sibling, src=x_ref)]
        first += [copy(1 + j, me, (*chip, c), src=x_ref) for j, chip in enumerate(chips)]
        for cp in first:
            cp.start()
        load.wait()
        store = pltpu.make_async_copy(stage, rows(*me), local_sems.at[1])
        store.start()
        passed = [copy(4 + j, (*chip, c), sibling) for j, chip in enumerate(chips)]
        for j, chip in enumerate(chips):
            copy(1 + j, (*chip, c), me).wait_recv()
            passed[j].start()
        copy(0, sibling, me).wait_recv()
        for j, chip in enumerate(chips):
            copy(4 + j, (*chip, 1 - c), me).wait_recv()
        for cp in first + passed:
            cp.wait_send()
        store.wait()
        for cp in swaps:
            cp.wait()

    res = pl.pallas_call(
        body, name=name,
        out_shape=[jax.ShapeDtypeStruct(r.shape, r.dtype) for r in swap] + [jax.ShapeDtypeStruct((8 * m_per, n), x_shard.dtype)],
        in_specs=[_ANY] * (n_sw + 1), out_specs=[_ANY] * (n_sw + 1),
        scratch_shapes=[pltpu.SemaphoreType.DMA((7,)), pltpu.SemaphoreType.DMA((7,)), pltpu.VMEM((m_per, n), x_shard.dtype),
                        pltpu.SemaphoreType.DMA((2,)), pltpu.SemaphoreType.DMA((max(n_sw, 1),)),
                        pltpu.SemaphoreType.DMA((max(n_sw, 1),))])(*swap, x_shard)
    return res[n_sw], list(res[:n_sw])


def _swap_sibling(parts, name):
    n = len(parts)

    def body(*refs):
        send_sems, recv_sems = refs[2 * n:]
        x, y, c = _place()
        cps = [pltpu.make_async_remote_copy(src_ref=refs[k], dst_ref=refs[n + k], send_sem=send_sems.at[k],
                                            recv_sem=recv_sems.at[k], device_id=(x, y, 1 - c), device_id_type=MESH)
               for k in range(n)]
        for cp in cps:
            cp.start()
        for cp in cps:
            cp.wait()

    return pl.pallas_call(
        body, name=name, out_shape=[jax.ShapeDtypeStruct(r.shape, r.dtype) for r in parts], in_specs=[_ANY] * n,
        out_specs=[_ANY] * n, scratch_shapes=[pltpu.SemaphoreType.DMA((n,)), pltpu.SemaphoreType.DMA((n,))])(*parts)


SH_CONVW, SH_META = 4 * 384, 16 * 256
SPARE_ROWS = 17

SMALL_SHAPES = dict(
    g_mix=(1, 1024), conv_b=(1, 1536), dt_bias=(1, 16), a_log=(1, 16), d_ssd=(1, 16), g_ssd=(1, 1024), lam_re=(1, 64, 64),
    lam_im=(1, 64, 64), log_step=(1, 64), b_re=(1, 64, 64, 16), b_im=(1, 64, 64, 16), c_re=(1, 64, 16, 64), c_im=(1, 64, 16, 64),
    d_s5=(1, 64, 16), b_glu=(1, 2048), g_s5=(1, 1024), g_mlp=(1, 1024), g_final=(1024,),
    conv_w=(4, D_XBC), meta_tokens=(N_META, D), loss=(1,))
EARLY = ["lam_re", "lam_im", "log_step", "b_re", "b_im", "c_re", "c_im", "d_s5", "b_glu", "g_s5", "g_mlp", "g_final"]
LATE = ["g_mix", "conv_b", "dt_bias", "a_log", "d_ssd", "g_ssd", "conv_w", "meta_tokens", "loss"]


def _rows_for(names):
    return -(-sum(math.prod(SMALL_SHAPES[n]) for n in names) // (8 * D)) * 8


def _pack_small(arrs, rows):
    flat = jnp.concatenate([a.reshape(-1).astype(f32) for a in arrs])
    return jnp.concatenate([flat, jnp.zeros((rows * D - flat.shape[0],), f32)]).reshape(rows, D)


def _unpack_small(slab, shapes):
    flat = slab.reshape(-1)
    out, o = [], 0
    for shp in shapes:
        n = math.prod(shp)
        out.append(flat[o:o + n].reshape(shp))
        o += n
    return out


def _sum8(g, rows, name):
    def body(g_ref, o_ref):
        acc = g_ref[0]
        for k in range(1, 8):
            acc = acc + g_ref[k]
        o_ref[...] = acc
    return pl.pallas_call(body, name=name, out_shape=jax.ShapeDtypeStruct((rows, D), f32),
                          compiler_params=_cp())(g.reshape(8, rows, D))


def _adam_math(w_, g_, m_, v_):
    m2 = ADAM_B1 * m_ + (1.0 - ADAM_B1) * g_
    v2 = ADAM_B2 * v_ + (1.0 - ADAM_B2) * jnp.square(g_)
    m_hat = m2 / (1.0 - ADAM_B1 ** ADAM_STEP)
    v_hat = v2 / (1.0 - ADAM_B2 ** ADAM_STEP)
    delta = -ADAM_LR * (m_hat / (jnp.sqrt(v_hat) + ADAM_EPS) + ADAM_WD * w_)
    return delta, m2, v2


def _adamw(name, w, g, m, v, bm):
    def fn(i, r, s, q, refs):
        return list(_adam_math(*r)), []
    c = w.shape[1]
    (d, m2, v2), _ = _rowwise(name, fn, w.shape[0] // bm, rows=[w, g, m, v], out_rows=[(c, f32)] * 3, bm=bm)
    return d, m2, v2


def _adamw_whole(name, w, g, m, v):
    def body(w_ref, g_ref, m_ref, v_ref, d_ref, m2_ref, v2_ref):
        d_ref[...], m2_ref[...], v2_ref[...] = _adam_math(w_ref[...], g_ref[...], m_ref[...], v_ref[...])
    return pl.pallas_call(body, name=name, out_shape=[jax.ShapeDtypeStruct(w.shape, f32)] * 3, compiler_params=_cp())(w, g, m, v)


def _sum_parts(name, own, land):
    def fn(i, r, s, q, refs):
        acc = r[0].astype(f32)
        for k in range(7):
            acc = acc + refs[0][k].astype(f32)
        return [acc], []
    rows = own.shape[0]
    bm = CH if rows % CH == 0 else rows
    (o,), _ = _rowwise(name, fn, rows // bm, rows=[own], refs=[(land, pl.BlockSpec((7, bm, D), lambda i: (0, i, 0)))],
                       out_rows=[(D, f32)], bm=bm)
    return o


def kernel(x, meta_tokens, g_mix, w_in, conv_w, conv_b, dt_bias, a_log, d_ssd, g_ssd, lam_re, lam_im, log_step, b_re, b_im, c_re, c_im, d_s5, w_glu, b_glu, g_s5, w_out, g_mlp, w_up, w_down, g_final, loss_target, m_meta_tokens, m_g_mix, m_w_in, m_conv_w, m_conv_b, m_dt_bias, m_a_log, m_d_ssd, m_g_ssd, m_lam_re, m_lam_im, m_log_step, m_b_re, m_b_im, m_c_re, m_c_im, m_d_s5, m_w_glu, m_b_glu, m_g_s5, m_w_out, m_g_mlp, m_w_up, m_w_down, m_g_final, v_meta_tokens, v_g_mix, v_w_in, v_conv_w, v_conv_b, v_dt_bias, v_a_log, v_d_ssd, v_g_ssd, v_lam_re, v_lam_im, v_log_step, v_b_re, v_b_im, v_c_re, v_c_im, v_d_s5, v_w_glu, v_b_glu, v_g_s5, v_w_out, v_g_mlp, v_w_up, v_w_down, v_g_final):
    given = dict(locals())
    cx, cy, cc = _place()
    chip = 2 * cx + cy

    small_f = jnp.concatenate([conv_w.reshape(-1), meta_tokens.reshape(-1)])
    t_hi = small_f.astype(bf16)
    r_1 = small_f - t_hi.astype(f32)
    t_mid = r_1.astype(bf16)
    t_lo = (r_1 - t_mid.astype(f32)).astype(bf16)
    terms = jnp.concatenate([t_hi, t_mid, t_lo])
    spare = jnp.concatenate([terms, jnp.zeros((SPARE_ROWS * D - terms.shape[0],), bf16)]).reshape(SPARE_ROWS, D)
    shards = (w_up[0].astype(bf16), w_down[0].astype(bf16),
              jnp.concatenate([w_glu[0].T.astype(bf16), w_out[0].astype(bf16)], axis=0))
    in_rows = jnp.concatenate([w_in[0].T.astype(bf16), spare, jnp.zeros((1024 - 900 - SPARE_ROWS, D), bf16)], axis=0)
    my_half = lax.dynamic_slice_in_dim(in_rows, cc * 512, 512, axis=0)
    gathered = _allgather8(my_half, "gather_w_in")[0].reshape(4, 1024, D)
    w_in_t = _perm_rows_w_in(jnp.concatenate([gathered[s, 0:900] for s in range(4)], axis=0))
    n_sf = SH_CONVW + SH_META
    tr = gathered[:, 900:900 + SPARE_ROWS].reshape(4, SPARE_ROWS * D)[:, :3 * n_sf].astype(f32).reshape(4, 3, n_sf)
    sp = tr[:, 0] + tr[:, 1] + tr[:, 2]
    conv_w_full = jnp.concatenate([sp[s, :SH_CONVW].reshape(4, 384) for s in range(4)], axis=1)
    meta_full = jnp.concatenate([sp[s, SH_CONVW:].reshape(16, 256) for s in range(4)], axis=1)

    p = dict(g_mix=g_mix, conv_w=conv_w_full, conv_b=conv_b, dt_bias=dt_bias, a_log=a_log, d_ssd=d_ssd, g_ssd=g_ssd,
             lam_re=lam_re[0], lam_im=lam_im[0], log_step=log_step[0], b_re=b_re[0], b_im=b_im[0], c_re=c_re[0], c_im=c_im[0],
             d_s5=d_s5[0], b_glu=b_glu, g_s5=g_s5, g_mlp=g_mlp, g_final=g_final.reshape(1, D))
    grad_x, pieces, all_early, late = _local_step(x[0], loss_target[0], meta_full, p, w_in_t, shards)
    grad_x = grad_x.reshape(x.shape)

    reds = []
    for k, (gp, land) in enumerate(pieces):
        half = gp.shape[1] // 2
        own = lax.dynamic_slice(gp, (chip, cc * half, 0), (1, half, D)).reshape(half, D)
        reds.append(_sum_parts("rs_sum_%d" % k, own, land))
    all_late, others = _allgather8(_pack_small([late[n] for n in LATE], _rows_for(LATE)), "rs_share_gather_small", swap=reds)
    lower = [jnp.where(cc == 0, r, o) for r, o in zip(reds, others)]
    upper = [jnp.where(cc == 0, o, r) for r, o in zip(reds, others)]
    g_up, g_down = lower[0], upper[0]
    g_glu, g_out = lower[1].T, upper[1]
    g_in_t = lax.dynamic_slice_in_dim(jnp.concatenate([lower[2], upper[2]], axis=0), 4 * chip, 900, axis=0)

    gs = dict(zip(EARLY, _unpack_small(_sum8(all_early, _rows_for(EARLY), "sum8_early"), [SMALL_SHAPES[n] for n in EARLY])))
    gs.update(zip(LATE, _unpack_small(_sum8(all_late, _rows_for(LATE), "sum8_late"), [SMALL_SHAPES[n] for n in LATE])))
    g_conv_w = lax.dynamic_slice_in_dim(gs.pop("conv_w"), chip * 384, 384, axis=1).reshape(conv_w.shape)
    g_meta = lax.dynamic_slice_in_dim(gs.pop("meta_tokens"), chip * 256, 256, axis=1)
    loss = gs.pop("loss").reshape(())

    grads = dict(gs, meta_tokens=g_meta, conv_w=g_conv_w, w_in=g_in_t.T.reshape(w_in.shape), w_glu=g_glu.reshape(w_glu.shape),
                 w_out=g_out.reshape(w_out.shape), w_up=g_up.reshape(w_up.shape), w_down=g_down.reshape(w_down.shape))
    delta, new_m, new_v = {}, {}, {}
    d_, m_, v_ = _adamw_whole("adamw_w_in", w_in[0].T, g_in_t, m_w_in[0].T, v_w_in[0].T)
    delta["w_in"], new_m["w_in"], new_v["w_in"] = (a.T.reshape(w_in.shape) for a in (d_, m_, v_))
    for n in ("w_glu", "w_out", "w_up", "w_down"):
        shp = given[n].shape
        two = lambda a: a.reshape(shp[1], shp[2])
        d_, m_, v_ = _adamw("adamw_" + n, two(given[n]), two(grads[n]), two(given["m_" + n]), two(given["v_" + n]), 256)
        delta[n], new_m[n], new_v[n] = d_.reshape(shp), m_.reshape(shp), v_.reshape(shp)
    for n in EARLY + LATE[:-1]:
        shp = given[n].shape
        if len(shp) == 4 and shp[-1] == 16:
            two = back = lambda a: jnp.swapaxes(a, -1, -2)
        else:
            two = (lambda a: a.reshape(1, -1)) if len(shp) == 1 else (lambda a: a)
            back = lambda a: a.reshape(shp)
        d_, m_, v_ = _adamw_whole("adamw_" + n, two(given[n]), two(grads[n].reshape(shp)), two(given["m_" + n]), two(given["v_" + n]))
        delta[n], new_m[n], new_v[n] = back(d_), back(m_), back(v_)

    order = ["meta_tokens", "g_mix", "w_in", "conv_w", "conv_b", "dt_bias", "a_log", "d_ssd", "g_ssd", "lam_re", "lam_im", "log_step",
             "b_re", "b_im", "c_re", "c_im", "d_s5", "w_glu", "b_glu", "g_s5", "w_out", "g_mlp", "w_up", "w_down", "g_final"]
    grads_out = [grads[n].reshape(given[n].shape) for n in order]
    return (loss, grad_x, *grads_out, *[delta[n] for n in order], *[new_m[n] for n in order], *[new_v[n] for n in order])
```
